```python
import math
import jax
import jax.numpy as jnp
from jax import lax
import numpy as np

D_MODEL = 4096
BATCH = 16
SEQ = 2048
DEPTH = 1

HEAD_DIM = 128
N_SB_HEADS = 16
N_DIL_HEADS = 16
SB_WIDTH = N_SB_HEADS * HEAD_DIM
DIL_WIDTH = N_DIL_HEADS * HEAD_DIM
MIX_WIDTH = SB_WIDTH + DIL_WIDTH
SPLIT_SIZES = (SB_WIDTH, SB_WIDTH, SB_WIDTH, SB_WIDTH, DIL_WIDTH, DIL_WIDTH, DIL_WIDTH, DIL_WIDTH)
IN_COLS = sum(SPLIT_SIZES)
SPLIT_POINTS = tuple(sum(SPLIT_SIZES[:i + 1]) for i in range(len(SPLIT_SIZES) - 1))
Q_BLOCK = 128
DIL_PAIRS = ((128, 1), (512, 4), (2048, 16))
ALIBI_MAX_BIAS = 8.0
EPS = 1e-6

kernel_name = 'hybrid_stickbreak_dilated_block'


def rmsnorm(x, g):
    xf = x.astype(jnp.float32)
    y = xf * lax.rsqrt(jnp.mean(xf * xf, axis=-1, keepdims=True) + EPS)
    return (y * g.astype(jnp.float32)).astype(x.dtype)


def split_heads(t, n_heads):
    b, s, _ = t.shape
    return t.reshape(b, s, n_heads, HEAD_DIM).transpose(0, 2, 1, 3)


def head_rmsnorm_merge(y, g):
    b, h, s, d = y.shape
    yf = y.astype(jnp.float32)
    yf = yf * lax.rsqrt(jnp.mean(yf * yf, axis=-1, keepdims=True) + EPS)
    return yf.transpose(0, 2, 1, 3).reshape(b, s, h * d) * g.astype(jnp.float32)


def alibi_slopes(n_heads):
    return jnp.exp2(-ALIBI_MAX_BIAS * jnp.arange(1, n_heads + 1, dtype=jnp.float32) / n_heads)


def stick_breaking_attention(q, k, v):
    b, h, s, d = q.shape
    nb = s // Q_BLOCK
    inv_sqrt_d = 1.0 / math.sqrt(d)
    kf = k.astype(jnp.float32)
    vf = v.astype(jnp.float32)
    qb = q.astype(jnp.float32).reshape(b, h, nb, Q_BLOCK, d).transpose(2, 0, 1, 3, 4)
    k_pos = jnp.arange(s)

    def one_block(args):
        qi, blk = args
        z = jnp.einsum('bhqd,bhkd->bhqk', qi, kf) * inv_sqrt_d
        q_pos = blk * Q_BLOCK + jnp.arange(Q_BLOCK)
        mask = k_pos[None, :] < q_pos[:, None]
        log_beta = jax.nn.log_sigmoid(z)
        log_one_minus = jnp.where(mask, jax.nn.log_sigmoid(-z), 0.0)
        suffix = lax.cumsum(log_one_minus, axis=3, reverse=True) - log_one_minus
        a = jnp.where(mask, jnp.exp(log_beta + suffix), 0.0)
        return jnp.einsum('bhqk,bhkd->bhqd', a, vf)

    out = lax.map(one_block, (qb, jnp.arange(nb)))
    return out.transpose(1, 2, 0, 3, 4).reshape(b, h, s, d)


def dilated_branch(q, k, v, slopes, window, dilation):
    b, h, s, d = q.shape
    n = window // dilation
    L = s // dilation
    Lp = -(-L // n) * n
    nb = Lp // n
    inv_sqrt_d = 1.0 / math.sqrt(d)

    def to_local(t):
        t = t.astype(jnp.float32).reshape(b, h, L, dilation, d).transpose(0, 1, 3, 2, 4)
        t = jnp.pad(t, ((0, 0), (0, 0), (0, 0), (0, Lp - L), (0, 0)))
        return t.reshape(b, h, dilation, nb, n, d)

    def with_prev(t):
        prev = jnp.pad(t, ((0, 0), (0, 0), (0, 0), (1, 0), (0, 0), (0, 0)))[:, :, :, :-1]
        return jnp.concatenate([prev, t], axis=4)

    ql = to_local(q)
    kw = with_prev(to_local(k))
    vw = with_prev(to_local(v))
    qi = jnp.arange(n)[:, None]
    ki = jnp.arange(2 * n)[None, :] - n
    steps = qi - ki
    blk = jnp.arange(nb)[:, None, None]
    valid = (steps >= 0)[None] & (steps <= n)[None] & ((blk * n + ki[None]) >= 0)
    dist = (steps * dilation).astype(jnp.float32)
    sc = jnp.einsum('bhrnqd,bhrnkd->bhrnqk', ql, kw) * inv_sqrt_d
    sc = sc - slopes[None, :, None, None, None, None] * dist
    sc = jnp.where(valid, sc, -jnp.inf)
    m = jnp.max(sc, axis=-1, keepdims=True)
    p = jnp.exp(sc - m)
    den = jnp.sum(p, axis=-1, keepdims=True)
    num = jnp.einsum('bhrnqk,bhrnkd->bhrnqd', p, vw)

    def from_local(t):
        x_dim = t.shape[-1]
        t = t.reshape(b, h, dilation, Lp, x_dim)[:, :, :, :L]
        return t.transpose(0, 1, 3, 2, 4).reshape(b, h, s, x_dim)

    return from_local(num), from_local(den), from_local(m)


def dilated_attention(q, k, v):
    slopes = alibi_slopes(q.shape[1])
    parts = [dilated_branch(q, k, v, slopes, w, r) for (w, r) in DIL_PAIRS]
    m_all = parts[0][2]
    for part in parts[1:]:
        m_all = jnp.maximum(m_all, part[2])
    num = sum(pn * jnp.exp(pm - m_all) for (pn, _, pm) in parts)
    den = sum(pd * jnp.exp(pm - m_all) for (_, pd, pm) in parts)
    return num / den


def _fwd_setup_inputs(seed: int = 0) -> dict:
    key = jax.random.key(seed)
    ks = jax.random.split(key, 10)
    x = jax.random.normal(ks[0], (BATCH, SEQ, D_MODEL), jnp.float32)
    c = jax.random.normal(ks[1], (BATCH, D_MODEL), jnp.float32)
    w_ada = jax.random.normal(ks[2], (DEPTH, D_MODEL, 3 * D_MODEL), jnp.float32) * D_MODEL ** -0.5
    b_ada = 0.01 * jax.random.normal(ks[3], (DEPTH, 3 * D_MODEL), jnp.float32)
    g_norm = 1.0 + 0.1 * jax.random.normal(ks[4], (DEPTH, D_MODEL), jnp.float32)
    w_in = jax.random.normal(ks[5], (DEPTH, D_MODEL, IN_COLS), jnp.float32) * D_MODEL ** -0.5
    g_sb = 1.0 + 0.1 * jax.random.normal(ks[6], (DEPTH, SB_WIDTH), jnp.float32)
    g_dil = 1.0 + 0.1 * jax.random.normal(ks[7], (DEPTH, DIL_WIDTH), jnp.float32)
    w_out = jax.random.normal(ks[8], (DEPTH, MIX_WIDTH, D_MODEL), jnp.float32) * MIX_WIDTH ** -0.5
    g_final = 1.0 + 0.1 * jax.random.normal(ks[9], (D_MODEL,), jnp.float32)
    return {'x': x, 'c': c, 'w_ada': w_ada, 'b_ada': b_ada, 'g_norm': g_norm,
            'w_in': w_in, 'g_sb': g_sb, 'g_dil': g_dil, 'w_out': w_out, 'g_final': g_final}


def _fwd_reference(x, c, w_ada, b_ada, g_norm, w_in, g_sb, g_dil, w_out, g_final):
    cs = jax.nn.silu(c.astype(jnp.float32))
    for layer in range(DEPTH):
        mod = cs @ w_ada[layer].astype(jnp.float32) + b_ada[layer].astype(jnp.float32)
        shift, scale, gate = jnp.split(mod, 3, axis=-1)
        h = rmsnorm(x, g_norm[layer]).astype(jnp.float32) * (1.0 + scale[:, None, :]) + shift[:, None, :]
        proj = jnp.einsum('bsd,de->bse', h.astype(x.dtype), w_in[layer])
        sb_q, sb_k, sb_v, sb_z, dl_q, dl_k, dl_v, dl_z = jnp.split(proj, SPLIT_POINTS, axis=-1)
        y_sb = stick_breaking_attention(split_heads(sb_q, N_SB_HEADS), split_heads(sb_k, N_SB_HEADS),
                                        split_heads(sb_v, N_SB_HEADS))
        y_sb = head_rmsnorm_merge(y_sb, g_sb[layer]) * jax.nn.silu(sb_z.astype(jnp.float32))
        y_dl = dilated_attention(split_heads(dl_q, N_DIL_HEADS), split_heads(dl_k, N_DIL_HEADS),
                                 split_heads(dl_v, N_DIL_HEADS))
        y_dl = head_rmsnorm_merge(y_dl, g_dil[layer]) * jax.nn.silu(dl_z.astype(jnp.float32))
        y = jnp.concatenate([y_sb, y_dl], axis=-1).astype(x.dtype)
        out = jnp.einsum('bse,ed->bsd', y, w_out[layer])
        x = x + (gate[:, None, :] * out.astype(jnp.float32)).astype(x.dtype)
    return rmsnorm(x, g_final)


import jax as _jax
import jax.numpy as _jnp

TWIN_FORMAT = 'train_step'
FWD_PARAMS = ['x', 'c', 'w_ada', 'b_ada', 'g_norm', 'w_in', 'g_sb', 'g_dil', 'w_out', 'g_final']
TWIN_WEIGHTS = ['w_ada', 'b_ada', 'g_norm', 'w_in', 'g_sb', 'g_dil', 'w_out', 'g_final']
TWIN_DIFF_INPUT = 'x'
TWIN_INPUTS = ['x', 'c', 'w_ada', 'b_ada', 'g_norm', 'w_in', 'g_sb', 'g_dil', 'w_out', 'g_final', 'loss_target', 'm_w_ada', 'm_b_ada', 'm_g_norm', 'm_w_in', 'm_g_sb', 'm_g_dil', 'm_w_out', 'm_g_final', 'v_w_ada', 'v_b_ada', 'v_g_norm', 'v_w_in', 'v_g_sb', 'v_g_dil', 'v_w_out', 'v_g_final']
TWIN_OUTPUTS = ['loss', 'grad_x', 'grad_w_ada', 'grad_b_ada', 'grad_g_norm', 'grad_w_in', 'grad_g_sb', 'grad_g_dil', 'grad_w_out', 'grad_g_final', 'delta_w_ada', 'delta_b_ada', 'delta_g_norm', 'delta_w_in', 'delta_g_sb', 'delta_g_dil', 'delta_w_out', 'delta_g_final', 'new_m_w_ada', 'new_m_b_ada', 'new_m_g_norm', 'new_m_w_in', 'new_m_g_sb', 'new_m_g_dil', 'new_m_w_out', 'new_m_g_final', 'new_v_w_ada', 'new_v_b_ada', 'new_v_g_norm', 'new_v_w_in', 'new_v_g_sb', 'new_v_g_dil', 'new_v_w_out', 'new_v_g_final']
TWIN_LEAF_KINDS = {'loss': 'loss', 'grad_x': 'grad_x', 'grad_w_ada': 'grad_w', 'grad_b_ada': 'grad_w', 'grad_g_norm': 'grad_w', 'grad_w_in': 'grad_w', 'grad_g_sb': 'grad_w', 'grad_g_dil': 'grad_w', 'grad_w_out': 'grad_w', 'grad_g_final': 'grad_w', 'delta_w_ada': 'delta_w', 'delta_b_ada': 'delta_w', 'delta_g_norm': 'delta_w', 'delta_w_in': 'delta_w', 'delta_g_sb': 'delta_w', 'delta_g_dil': 'delta_w', 'delta_w_out': 'delta_w', 'delta_g_final': 'delta_w', 'new_m_w_ada': 'new_m', 'new_m_b_ada': 'new_m', 'new_m_g_norm': 'new_m', 'new_m_w_in': 'new_m', 'new_m_g_sb': 'new_m', 'new_m_g_dil': 'new_m', 'new_m_w_out': 'new_m', 'new_m_g_final': 'new_m', 'new_v_w_ada': 'new_v', 'new_v_b_ada': 'new_v', 'new_v_g_norm': 'new_v', 'new_v_w_in': 'new_v', 'new_v_g_sb': 'new_v', 'new_v_g_dil': 'new_v', 'new_v_w_out': 'new_v', 'new_v_g_final': 'new_v'}


def _forward(args):
    return _fwd_reference(*[args[k] for k in FWD_PARAMS])


def _output_shape():
    def fwd():
        inp = _fwd_setup_inputs(0)
        return _fwd_reference(*[inp[k] for k in FWD_PARAMS])
    out = _jax.eval_shape(fwd)
    return out.shape, out.dtype

N_MICROBATCH = 1
ADAM_LR = 0.001
ADAM_B1 = 0.9
ADAM_B2 = 0.999
ADAM_EPS = 1e-08
ADAM_WD = 0.01
ADAM_STEP = 10
PER_EXAMPLE_BATCH_AXIS = {'x': 0, 'c': 0, 'loss_target': 0}
SHARED_INPUTS = []
_WEIGHT_DTYPES = {'w_ada': _jnp.float32, 'b_ada': _jnp.float32, 'g_norm': _jnp.float32, 'w_in': _jnp.float32, 'g_sb': _jnp.float32, 'g_dil': _jnp.float32, 'w_out': _jnp.float32, 'g_final': _jnp.float32}
MOMENT_SCALE = {'w_ada': 6.446261e-02, 'b_ada': 1.416194e-01, 'g_norm': 2.832798e-02, 'w_in': 2.085264e-02, 'g_sb': 2.725300e-02, 'g_dil': 3.198275e-02, 'w_out': 2.922338e-02, 'g_final': 8.078834e+00}


def _to_microbatches(a, axis):
    t = _jnp.moveaxis(a, axis, 0)
    t = t.reshape((N_MICROBATCH, t.shape[0] // N_MICROBATCH) + t.shape[1:])
    return _jnp.moveaxis(t, 1, axis + 1)


def setup_inputs(seed: int = 0) -> dict:
    inp = _fwd_setup_inputs(seed)
    key = _jax.random.fold_in(_jax.random.key(seed), 7919)
    shape, _ = _output_shape()
    out = dict(inp)
    out["loss_target"] = _jax.random.normal(_jax.random.fold_in(key, 0), shape, _jnp.float32)
    for i, name in enumerate(TWIN_WEIGHTS):
        w = inp[name].astype(_jnp.float32)
        if MOMENT_SCALE is None:
            s = _jnp.sqrt(_jnp.mean(_jnp.square(w)) + 1e-30)
        else:
            s = MOMENT_SCALE[name]
        km, kv = _jax.random.split(_jax.random.fold_in(key, i + 1))
        out[name] = w
        out["m_" + name] = s * _jax.random.normal(km, w.shape, _jnp.float32)
        out["v_" + name] = (s * s) * _jax.random.uniform(kv, w.shape, _jnp.float32, 0.5, 1.5)
    if N_MICROBATCH > 1:
        for name, axis in PER_EXAMPLE_BATCH_AXIS.items():
            out[name] = _to_microbatches(out[name], axis)
    return {'x': out['x'], 'c': out['c'], 'w_ada': out['w_ada'], 'b_ada': out['b_ada'], 'g_norm': out['g_norm'], 'w_in': out['w_in'], 'g_sb': out['g_sb'], 'g_dil': out['g_dil'], 'w_out': out['w_out'], 'g_final': out['g_final'], 'loss_target': out['loss_target'], 'm_w_ada': out['m_w_ada'], 'm_b_ada': out['m_b_ada'], 'm_g_norm': out['m_g_norm'], 'm_w_in': out['m_w_in'], 'm_g_sb': out['m_g_sb'], 'm_g_dil': out['m_g_dil'], 'm_w_out': out['m_w_out'], 'm_g_final': out['m_g_final'], 'v_w_ada': out['v_w_ada'], 'v_b_ada': out['v_b_ada'], 'v_g_norm': out['v_g_norm'], 'v_w_in': out['v_w_in'], 'v_g_sb': out['v_g_sb'], 'v_g_dil': out['v_g_dil'], 'v_w_out': out['v_w_out'], 'v_g_final': out['v_g_final']}


def _loss(weights, diff, rest, loss_target):
    with _jax.named_scope("forward"):
        args = {**rest, TWIN_DIFF_INPUT: diff, **{k: w.astype(_WEIGHT_DTYPES[k]) for k, w in weights.items()}}
        y = _forward(args)
    with _jax.named_scope("loss_head"):
        err = _jnp.square(y.astype(_jnp.float32) - loss_target)
        return 0.5 * _jnp.sum(_jnp.mean(err, axis=-1)) if err.ndim else 0.5 * err


def _adamw(w, g, m, v):
    m = ADAM_B1 * m + (1.0 - ADAM_B1) * g
    v = ADAM_B2 * v + (1.0 - ADAM_B2) * _jnp.square(g)
    m_hat = m / (1.0 - ADAM_B1 ** ADAM_STEP)
    v_hat = v / (1.0 - ADAM_B2 ** ADAM_STEP)
    delta = -ADAM_LR * (m_hat / (_jnp.sqrt(v_hat) + ADAM_EPS) + ADAM_WD * w)
    return delta, m, v


def reference(x, c, w_ada, b_ada, g_norm, w_in, g_sb, g_dil, w_out, g_final, loss_target, m_w_ada, m_b_ada, m_g_norm, m_w_in, m_g_sb, m_g_dil, m_w_out, m_g_final, v_w_ada, v_b_ada, v_g_norm, v_w_in, v_g_sb, v_g_dil, v_w_out, v_g_final):
    given = dict(x=x, c=c, w_ada=w_ada, b_ada=b_ada, g_norm=g_norm, w_in=w_in, g_sb=g_sb, g_dil=g_dil, w_out=w_out, g_final=g_final, loss_target=loss_target, m_w_ada=m_w_ada, m_b_ada=m_b_ada, m_g_norm=m_g_norm, m_w_in=m_w_in, m_g_sb=m_g_sb, m_g_dil=m_g_dil, m_w_out=m_w_out, m_g_final=m_g_final, v_w_ada=v_w_ada, v_b_ada=v_b_ada, v_g_norm=v_g_norm, v_w_in=v_w_in, v_g_sb=v_g_sb, v_g_dil=v_g_dil, v_w_out=v_w_out, v_g_final=v_g_final)
    weights = {n: given[n] for n in TWIN_WEIGHTS}
    shared = {n: given[n] for n in SHARED_INPUTS}
    per_example = {n: given[n] for n in ['x', 'c']}
    grad_fn = _jax.value_and_grad(_loss, argnums=(0, 1))

    def one_microbatch(ex, loss_target):
        ex = dict(ex)
        diff = ex.pop(TWIN_DIFF_INPUT)
        return grad_fn(weights, diff, {**shared, **ex}, loss_target)

    if N_MICROBATCH == 1:
        loss, (grad_w, grad_x) = one_microbatch(per_example, given["loss_target"])
    else:
        def body(carry, xs):
            loss_sum, grad_sum = carry
            l_k, (gw_k, gx_k) = one_microbatch(xs[0], xs[1])
            with _jax.named_scope("update"):
                return (loss_sum + l_k, _jax.tree.map(_jnp.add, grad_sum, gw_k)), gx_k

        init = (_jnp.zeros((), _jnp.float32), _jax.tree.map(_jnp.zeros_like, weights))
        (loss, grad_w), grad_x = _jax.lax.scan(body, init, (per_example, given["loss_target"]))
    with _jax.named_scope("update"):
        delta_w, new_m, new_v = {}, {}, {}
        for n in TWIN_WEIGHTS:
            delta_w[n], new_m[n], new_v[n] = _adamw(weights[n], grad_w[n], given["m_" + n], given["v_" + n])
    return (loss, grad_x, *[grad_w[n] for n in TWIN_WEIGHTS], *[delta_w[n] for n in TWIN_WEIGHTS],
            *[new_m[n] for n in TWIN_WEIGHTS], *[new_v[n] for n in TWIN_WEIGHTS])
```

```python
import functools
import math

import jax
import jax.numpy as jnp
from jax import lax
from jax.experimental import pallas as pl
from jax.experimental.pallas import tpu as pltpu

F32 = jnp.float32
BF16 = jnp.bfloat16
MESH = pl.DeviceIdType.MESH

HEAD_DIM = 128
EPS = 1e-6
DIL_PAIRS = ((128, 1), (512, 4), (2048, 16))
ALIBI_MAX_BIAS = 8.0
ADAM_LR = 0.001
ADAM_B1 = 0.9
ADAM_B2 = 0.999
ADAM_EPS = 1e-08
ADAM_WD = 0.01
ADAM_STEP = 10
N_CHIPS = 4
N_DEV = 8
VMEM_LIMIT_BYTES = 56 * 1024 * 1024
NEG_BIG = -1e30

NN = (((1,), (0,)), ((), ()))
NT = (((1,), (1,)), ((), ()))
TN = (((0,), (0,)), ((), ()))

ANY = pl.BlockSpec(memory_space=pl.ANY)
VMEM_SPEC = pl.BlockSpec(memory_space=pltpu.VMEM)


def _cparams(sem=None):
    return pltpu.CompilerParams(dimension_semantics=sem, vmem_limit_bytes=VMEM_LIMIT_BYTES)


def _tile(dim, pref):
    t = min(dim, pref)
    assert dim % t == 0, (dim, pref)
    return t


def _dot(a, b, dims):
    return lax.dot_general(a, b, dims, preferred_element_type=F32)


def _sigmoid(x):
    return 1.0 / (1.0 + jnp.exp(-x))


def _place():
    return lax.axis_index("x"), lax.axis_index("y"), lax.axis_index("c")


def _allgather8(x_shard, name):
    m_per, n = x_shard.shape

    def body(x_ref, out_ref, send_sems, recv_sems, local_sem):
        x, y, c = _place()
        me, sibling = (x, y, c), (x, y, 1 - c)
        chips = [(1 - x, y), (x, 1 - y), (1 - x, 1 - y)]

        def rows(px, py, pc):
            return out_ref.at[pl.ds((4 * px + 2 * py + pc) * m_per, m_per), :]

        def copy(k, block, to, src=None):
            return pltpu.make_async_remote_copy(
                src_ref=rows(*block) if src is None else src, dst_ref=rows(*block),
                send_sem=send_sems.at[k], recv_sem=recv_sems.at[k], device_id=to, device_id_type=MESH)

        mine = pltpu.make_async_copy(x_ref, rows(*me), local_sem)
        mine.start()
        first = [copy(0, me, sibling, src=x_ref)]
        first += [copy(1 + j, me, (*chip, c), src=x_ref) for j, chip in enumerate(chips)]
        for cp in first:
            cp.start()
        passed = [copy(4 + j, (*chip, c), sibling) for j, chip in enumerate(chips)]
        for j, chip in enumerate(chips):
            copy(1 + j, (*chip, c), me).wait_recv()
            passed[j].start()
        copy(0, sibling, me).wait_recv()
        for j, chip in enumerate(chips):
            copy(4 + j, (*chip, 1 - c), me).wait_recv()
        for cp in first + passed:
            cp.wait_send()
        mine.wait()

    return pl.pallas_call(
        body, name=name,
        out_shape=jax.ShapeDtypeStruct((N_DEV * m_per, n), x_shard.dtype),
        in_specs=[VMEM_SPEC], out_specs=VMEM_SPEC,
        scratch_shapes=[pltpu.SemaphoreType.DMA((7,)), pltpu.SemaphoreType.DMA((7,)), pltpu.SemaphoreType.DMA],
    )(x_shard)


def _allgather_chips(shard, name):
    r, cdim = shard.shape
    half = r // 2

    def body(s_ref, out_ref, send_sems, recv_sems, local_sem):
        x, y, c = _place()
        sibling = (x, y, 1 - c)
        chips = [(1 - x, y), (x, 1 - y), (1 - x, 1 - y)]

        def rows(px, py, pc):
            return out_ref.at[2 * px + py, pl.ds(pc * half, half), :]

        def copy(k, block, to, src=None):
            return pltpu.make_async_remote_copy(
                src_ref=rows(*block) if src is None else src, dst_ref=rows(*block),
                send_sem=send_sems.at[k], recv_sem=recv_sems.at[k], device_id=to, device_id_type=MESH)

        mine = pltpu.make_async_copy(s_ref, out_ref.at[2 * x + y], local_sem)
        mine.start()
        my_half = s_ref.at[pl.ds(c * half, half), :]
        first = [copy(j, (x, y, c), (*chip, c), src=my_half) for j, chip in enumerate(chips)]
        for cp in first:
            cp.start()
        passed = [copy(3 + j, (*chip, c), sibling) for j, chip in enumerate(chips)]
        for j, chip in enumerate(chips):
            copy(j, (*chip, c), (x, y, c)).wait_recv()
            passed[j].start()
        for j, chip in enumerate(chips):
            copy(3 + j, (*chip, 1 - c), (x, y, c)).wait_recv()
        for cp in first + passed:
            cp.wait_send()
        mine.wait()

    return pl.pallas_call(
        body, name=name,
        out_shape=jax.ShapeDtypeStruct((N_CHIPS, r, cdim), shard.dtype),
        in_specs=[ANY], out_specs=ANY,
        scratch_shapes=[pltpu.SemaphoreType.DMA((6,)), pltpu.SemaphoreType.DMA((6,)), pltpu.SemaphoreType.DMA],
    )(shard)


def _sibling_half_swap(gs, name):
    n, r, cdim = gs.shape
    half = r // 2

    def body(g_ref, out_ref, send_sem, recv_sem):
        x, y, c = _place()
        cp = pltpu.make_async_remote_copy(
            src_ref=g_ref.at[:, pl.ds((1 - c) * half, half), :], dst_ref=out_ref,
            send_sem=send_sem, recv_sem=recv_sem, device_id=(x, y, 1 - c), device_id_type=MESH)
        cp.start()
        cp.wait()

    return pl.pallas_call(
        body, name=name,
        out_shape=jax.ShapeDtypeStruct((n, half, cdim), gs.dtype),
        in_specs=[ANY], out_specs=ANY,
        scratch_shapes=[pltpu.SemaphoreType.DMA, pltpu.SemaphoreType.DMA],
    )(gs)


def _scatter_chips(pa, name):
    n, h, cdim = pa.shape

    def body(p_ref, out_ref, send_sems, recv_sems, local_sem):
        x, y, c = _place()
        me = 2 * x + y
        chips = [(1 - x, y), (x, 1 - y), (1 - x, 1 - y)]
        mine = pltpu.make_async_copy(p_ref.at[me], out_ref.at[me], local_sem)
        mine.start()
        sends = []
        for j, (px, py) in enumerate(chips):
            sends.append(pltpu.make_async_remote_copy(
                src_ref=p_ref.at[2 * px + py], dst_ref=out_ref.at[me],
                send_sem=send_sems.at[j], recv_sem=recv_sems.at[j], device_id=(px, py, c), device_id_type=MESH))
        for cp in sends:
            cp.start()
        for j, (px, py) in enumerate(chips):
            pltpu.make_async_remote_copy(
                src_ref=p_ref.at[me], dst_ref=out_ref.at[2 * px + py],
                send_sem=send_sems.at[j], recv_sem=recv_sems.at[j], device_id=(px, py, c), device_id_type=MESH).wait_recv()
        for cp in sends:
            cp.wait_send()
        mine.wait()

    return pl.pallas_call(
        body, name=name,
        out_shape=jax.ShapeDtypeStruct((n, h, cdim), pa.dtype),
        in_specs=[ANY], out_specs=ANY,
        scratch_shapes=[pltpu.SemaphoreType.DMA((3,)), pltpu.SemaphoreType.DMA((3,)), pltpu.SemaphoreType.DMA],
    )(pa)


def _sibling_join(s_half, name):
    h, cdim = s_half.shape

    def body(s_ref, out_ref, send_sem, recv_sem, local_sem):
        x, y, c = _place()
        mine = pltpu.make_async_copy(s_ref, out_ref.at[pl.ds(c * h, h), :], local_sem)
        mine.start()
        cp = pltpu.make_async_remote_copy(
            src_ref=s_ref, dst_ref=out_ref.at[pl.ds(c * h, h), :],
            send_sem=send_sem, recv_sem=recv_sem, device_id=(x, y, 1 - c), device_id_type=MESH)
        cp.start()
        pltpu.make_async_remote_copy(
            src_ref=s_ref, dst_ref=out_ref.at[pl.ds((1 - c) * h, h), :],
            send_sem=send_sem, recv_sem=recv_sem, device_id=(x, y, 1 - c), device_id_type=MESH).wait_recv()
        cp.wait_send()
        mine.wait()

    return pl.pallas_call(
        body, name=name,
        out_shape=jax.ShapeDtypeStruct((2 * h, cdim), s_half.dtype),
        in_specs=[ANY], out_specs=ANY,
        scratch_shapes=[pltpu.SemaphoreType.DMA, pltpu.SemaphoreType.DMA, pltpu.SemaphoreType.DMA],
    )(s_half)


def _cast_bf16(w, name):
    r, cdim = w.shape
    tr, tc = _tile(r, 512), _tile(cdim, 2048)

    def body(w_ref, o_ref):
        o_ref[...] = w_ref[...].astype(BF16)

    return pl.pallas_call(
        body, name=name, grid=(r // tr, cdim // tc),
        in_specs=[pl.BlockSpec((tr, tc), lambda i, j: (i, j))],
        out_specs=pl.BlockSpec((tr, tc), lambda i, j: (i, j)),
        out_shape=jax.ShapeDtypeStruct((r, cdim), BF16),
        compiler_params=_cparams(("parallel", "parallel")),
    )(w)


def _add_own_half(gs, ra, name):
    n, r, cdim = gs.shape
    half = r // 2
    tr, tc = _tile(half, 512), _tile(cdim, 2048)
    nt = half // tr

    def body(c_ref, g_ref, r_ref, o_ref):
        o_ref[...] = g_ref[...] + r_ref[...]

    c = lax.axis_index("c").astype(jnp.int32).reshape((1,))
    return pl.pallas_call(
        body, name=name,
        grid_spec=pltpu.PrefetchScalarGridSpec(
            num_scalar_prefetch=1, grid=(n, nt, cdim // tc),
            in_specs=[pl.BlockSpec((None, tr, tc), lambda s, i, j, c_ref: (s, c_ref[0] * nt + i, j)),
                      pl.BlockSpec((None, tr, tc), lambda s, i, j, c_ref: (s, i, j))],
            out_specs=pl.BlockSpec((None, tr, tc), lambda s, i, j, c_ref: (s, i, j))),
        out_shape=jax.ShapeDtypeStruct((n, half, cdim), F32),
        compiler_params=_cparams(("parallel", "parallel", "parallel")),
    )(c, gs, ra)


def _sum_slabs(rb, name):
    n, h, cdim = rb.shape
    tr, tc = _tile(h, 256), _tile(cdim, 2048)

    def body(r_ref, o_ref):
        acc = r_ref[0]
        for p in range(1, n):
            acc = acc + r_ref[p]
        o_ref[...] = acc

    return pl.pallas_call(
        body, name=name, grid=(h // tr, cdim // tc),
        in_specs=[pl.BlockSpec((n, tr, tc), lambda i, j: (0, i, j))],
        out_specs=pl.BlockSpec((tr, tc), lambda i, j: (i, j)),
        out_shape=jax.ShapeDtypeStruct((h, cdim), F32),
        compiler_params=_cparams(("parallel", "parallel")),
    )(rb)


def _adamw_math(w, g, m, v):
    m = ADAM_B1 * m + (1.0 - ADAM_B1) * g
    v = ADAM_B2 * v + (1.0 - ADAM_B2) * (g * g)
    m_hat = m / (1.0 - ADAM_B1 ** ADAM_STEP)
    v_hat = v / (1.0 - ADAM_B2 ** ADAM_STEP)
    delta = -ADAM_LR * (m_hat / (jnp.sqrt(v_hat) + ADAM_EPS) + ADAM_WD * w)
    return delta, m, v


def _adamw(w, g, m, v, name):
    r, cdim = w.shape
    tr, tc = _tile(r, 256), _tile(cdim, 2048)

    def body(w_ref, g_ref, m_ref, v_ref, d_ref, nm_ref, nv_ref):
        d, nm, nv = _adamw_math(w_ref[...], g_ref[...], m_ref[...], v_ref[...])
        d_ref[...] = d
        nm_ref[...] = nm
        nv_ref[...] = nv

    spec = pl.BlockSpec((tr, tc), lambda i, j: (i, j))
    sds = jax.ShapeDtypeStruct((r, cdim), F32)
    return pl.pallas_call(
        body, name=name, grid=(r // tr, cdim // tc),
        in_specs=[spec] * 4, out_specs=[spec] * 3, out_shape=[sds] * 3,
        compiler_params=_cparams(("parallel", "parallel")),
    )(w, g, m, v)


def _matmul(a, b, *, grid, a_spec, b_spec, out_spec, out_shape, acc_shape, dims, name, bias=None, bias_spec=None,
            silu_a=False):
    nk = grid[2]

    def body(*refs):
        if bias is None:
            a_ref, b_ref, o_ref, acc_ref = refs
        else:
            a_ref, b_ref, bias_ref, o_ref, acc_ref = refs
        k = pl.program_id(2)

        @pl.when(k == 0)
        def _():
            acc_ref[...] = jnp.zeros_like(acc_ref)

        av = a_ref[...]
        if silu_a:
            av = av * _sigmoid(av)
        acc_ref[...] += _dot(av.astype(BF16), b_ref[...].astype(BF16), dims)

        @pl.when(k == nk - 1)
        def _():
            res = acc_ref[...]
            if bias is not None:
                res = res + bias_ref[...]
            o_ref[...] = res.astype(o_ref.dtype)

    in_specs = [a_spec, b_spec] + ([] if bias is None else [bias_spec])
    args = (a, b) + (() if bias is None else (bias,))
    return pl.pallas_call(
        body, name=name, grid=grid, in_specs=in_specs, out_specs=out_spec, out_shape=out_shape,
        scratch_shapes=[pltpu.VMEM(acc_shape, F32)],
        compiler_params=_cparams(("parallel", "parallel", "arbitrary")),
    )(*args)


def _mm_tiles(m, n, k):
    return _tile(m, 1024), _tile(n, 1024), _tile(k, 512)


def _proj_fwd(h2, ws_in):
    t, d = h2.shape
    _, _, cs = ws_in.shape
    w = cs // 2
    tm, tn, tk = _mm_tiles(t, w, d)
    nps, npseg = cs // tn, w // tn
    return _matmul(
        h2, ws_in, grid=(t // tm, 8 * npseg, d // tk), dims=NN, name="proj_fwd",
        a_spec=pl.BlockSpec((tm, tk), lambda m, n, k: (m, k)),
        b_spec=pl.BlockSpec((None, tk, tn), lambda m, n, k: (n // nps, k, n % nps)),
        out_spec=pl.BlockSpec((None, tm, tn), lambda m, n, k: (n // npseg, m, n % npseg)),
        out_shape=jax.ShapeDtypeStruct((8, t, w), F32), acc_shape=(tm, tn))


def _proj_bwd_x(dproj8, ws_in):
    _, t, w = dproj8.shape
    _, d, cs = ws_in.shape
    tm, tn, tk = _mm_tiles(t, d, w)
    kps, kpseg = cs // tk, w // tk
    return _matmul(
        dproj8, ws_in, grid=(t // tm, d // tn, 8 * kpseg), dims=NT, name="proj_bwd_x",
        a_spec=pl.BlockSpec((None, tm, tk), lambda m, n, k: (k // kpseg, m, k % kpseg)),
        b_spec=pl.BlockSpec((None, tn, tk), lambda m, n, k: (k // kps, n, k % kps)),
        out_spec=pl.BlockSpec((tm, tn), lambda m, n, k: (m, n)),
        out_shape=jax.ShapeDtypeStruct((t, d), F32), acc_shape=(tm, tn))


def _proj_bwd_w(h2, dproj8):
    t, d = h2.shape
    _, _, w = dproj8.shape
    cs = 2 * w
    tm, tn, tk = _mm_tiles(d, w, t)
    nps, npseg = cs // tn, w // tn
    return _matmul(
        h2, dproj8, grid=(d // tm, 8 * npseg, t // tk), dims=TN, name="proj_bwd_w",
        a_spec=pl.BlockSpec((tk, tm), lambda m, n, k: (k, m)),
        b_spec=pl.BlockSpec((None, tk, tn), lambda m, n, k: (n // npseg, k, n % npseg)),
        out_spec=pl.BlockSpec((None, tm, tn), lambda m, n, k: (n // nps, m, n % nps)),
        out_shape=jax.ShapeDtypeStruct((N_CHIPS, d, cs), F32), acc_shape=(tm, tn))


def _out_fwd(y2, w_out):
    _, t, w = y2.shape
    _, d = w_out.shape
    tm, tn, tk = _mm_tiles(t, d, w)
    kpg = w // tk
    return _matmul(
        y2, w_out, grid=(t // tm, d // tn, 2 * kpg), dims=NN, name="out_fwd",
        a_spec=pl.BlockSpec((None, tm, tk), lambda m, n, k: (k // kpg, m, k % kpg)),
        b_spec=pl.BlockSpec((tk, tn), lambda m, n, k: (k, n)),
        out_spec=pl.BlockSpec((tm, tn), lambda m, n, k: (m, n)),
        out_shape=jax.ShapeDtypeStruct((t, d), F32), acc_shape=(tm, tn))


def _out_bwd_y(dout, w_out):
    t, d = dout.shape
    w = w_out.shape[0] // 2
    tm, tn, tk = _mm_tiles(t, w, d)
    npg = w // tn
    return _matmul(
        dout, w_out, grid=(t // tm, 2 * npg, d // tk), dims=NT, name="out_bwd_y",
        a_spec=pl.BlockSpec((tm, tk), lambda m, n, k: (m, k)),
        b_spec=pl.BlockSpec((tn, tk), lambda m, n, k: (n, k)),
        out_spec=pl.BlockSpec((None, tm, tn), lambda m, n, k: (n // npg, m, n % npg)),
        out_shape=jax.ShapeDtypeStruct((2, t, w), F32), acc_shape=(tm, tn))


def _out_bwd_w(y2, dout):
    _, t, w = y2.shape
    _, d = dout.shape
    tm, tn, tk = _mm_tiles(w, d, t)
    mpg = w // tm
    return _matmul(
        y2, dout, grid=(2 * mpg, d // tn, t // tk), dims=TN, name="out_bwd_w",
        a_spec=pl.BlockSpec((None, tk, tm), lambda m, n, k: (m // mpg, k, m % mpg)),
        b_spec=pl.BlockSpec((tk, tn), lambda m, n, k: (k, n)),
        out_spec=pl.BlockSpec((tm, tn), lambda m, n, k: (m, n)),
        out_shape=jax.ShapeDtypeStruct((2 * w, d), F32), acc_shape=(tm, tn))


def _mod_fwd(c_all, w_ada, b_ada):
    bt, d = c_all.shape
    _, n = w_ada.shape
    tn, tk = _tile(n, 512), _tile(d, 1024)
    return _matmul(
        c_all, w_ada, grid=(1, n // tn, d // tk), dims=NN, name="mod_fwd", silu_a=True,
        a_spec=pl.BlockSpec((bt, tk), lambda i, j, l: (0, l)),
        b_spec=pl.BlockSpec((tk, tn), lambda i, j, l: (l, j)),
        bias=b_ada, bias_spec=pl.BlockSpec((1, tn), lambda i, j, l: (0, j)),
        out_spec=pl.BlockSpec((bt, tn), lambda i, j, l: (0, j)),
        out_shape=jax.ShapeDtypeStruct((bt, n), F32), acc_shape=(bt, tn))


def _norm_mod_fwd(x, g_norm, scale, shift):
    b, s, d = x.shape
    ts = _tile(s, 256)

    def body(x_ref, g_ref, sc_ref, sh_ref, h_ref):
        xv = x_ref[...]
        r = lax.rsqrt(jnp.mean(xv * xv, axis=-1, keepdims=True) + EPS)
        y = (xv * r) * g_ref[...]
        h_ref[...] = (y * (1.0 + sc_ref[...]) + sh_ref[...]).astype(BF16)

    row = pl.BlockSpec((None, ts, d), lambda i, j: (i, j, 0))
    per_b = pl.BlockSpec((None, 1, d), lambda i, j: (i, 0, 0))
    return pl.pallas_call(
        body, name="norm_mod_fwd", grid=(b, s // ts),
        in_specs=[row, pl.BlockSpec((1, d), lambda i, j: (0, 0)), per_b, per_b],
        out_specs=row, out_shape=jax.ShapeDtypeStruct((b, s, d), BF16),
        compiler_params=_cparams(("parallel", "parallel")),
    )(x, g_norm, scale, shift)


def _norm_mod_bwd(x, dh, dx1, g_norm, scale):
    b, s, d = x.shape
    ts = _tile(s, 256)

    def body(x_ref, dh_ref, dx1_ref, g_ref, sc_ref, gx_ref, dsh_ref, dsc_ref, dg_ref):
        i, j = pl.program_id(0), pl.program_id(1)

        @pl.when(j == 0)
        def _():
            dsh_ref[...] = jnp.zeros_like(dsh_ref)
            dsc_ref[...] = jnp.zeros_like(dsc_ref)

        @pl.when((i == 0) & (j == 0))
        def _():
            dg_ref[...] = jnp.zeros_like(dg_ref)

        xv, dhv, g = x_ref[...], dh_ref[...], g_ref[...]
        r = lax.rsqrt(jnp.mean(xv * xv, axis=-1, keepdims=True) + EPS)
        xh = xv * r
        dsh_ref[...] += jnp.sum(dhv, axis=0, keepdims=True)
        dsc_ref[...] += jnp.sum(dhv * (xh * g), axis=0, keepdims=True)
        dn = dhv * (1.0 + sc_ref[...])
        dg_ref[...] += jnp.sum(dn * xh, axis=0, keepdims=True)
        u = dn * g
        dx = r * u - xv * (r * r * r) * jnp.mean(u * xv, axis=-1, keepdims=True)
        gx_ref[...] = dx1_ref[...] + dx

    row = pl.BlockSpec((None, ts, d), lambda i, j: (i, j, 0))
    per_b = pl.BlockSpec((None, 1, d), lambda i, j: (i, 0, 0))
    vec = pl.BlockSpec((1, d), lambda i, j: (0, 0))
    return pl.pallas_call(
        body, name="norm_mod_bwd", grid=(b, s // ts),
        in_specs=[row, row, row, vec, per_b],
        out_specs=[row, per_b, per_b, vec],
        out_shape=[jax.ShapeDtypeStruct((b, s, d), F32), jax.ShapeDtypeStruct((b, 1, d), F32),
                   jax.ShapeDtypeStruct((b, 1, d), F32), jax.ShapeDtypeStruct((1, d), F32)],
        compiler_params=_cparams(("arbitrary", "arbitrary")),
    )(x, dh, dx1, g_norm, scale)


def _loss_head(x, out, gate, g_final, target):
    b, s, d = x.shape
    ts = _tile(s, 256)

    def body(x_ref, o_ref, gt_ref, g_ref, t_ref, dx1_ref, dout_ref, dgt_ref, dg_ref, loss_ref):
        i, j = pl.program_id(0), pl.program_id(1)

        @pl.when(j == 0)
        def _():
            dgt_ref[...] = jnp.zeros_like(dgt_ref)

        @pl.when((i == 0) & (j == 0))
        def _():
            dg_ref[...] = jnp.zeros_like(dg_ref)
            loss_ref[...] = jnp.zeros_like(loss_ref)

        ov, gt, g = o_ref[...], gt_ref[...], g_ref[...]
        x1 = x_ref[...] + gt * ov
        r = lax.rsqrt(jnp.mean(x1 * x1, axis=-1, keepdims=True) + EPS)
        xh = x1 * r
        err = xh * g - t_ref[...]
        loss_ref[...] += 0.5 * jnp.sum(jnp.mean(err * err, axis=-1, keepdims=True))
        dfin = err * (1.0 / d)
        dg_ref[...] += jnp.sum(dfin * xh, axis=0, keepdims=True)
        u = dfin * g
        dx1 = r * u - x1 * (r * r * r) * jnp.mean(u * x1, axis=-1, keepdims=True)
        dx1_ref[...] = dx1
        dgt_ref[...] += jnp.sum(dx1 * ov, axis=0, keepdims=True)
        dout_ref[...] = (gt * dx1).astype(BF16)

    row = pl.BlockSpec((None, ts, d), lambda i, j: (i, j, 0))
    per_b = pl.BlockSpec((None, 1, d), lambda i, j: (i, 0, 0))
    vec = pl.BlockSpec((1, d), lambda i, j: (0, 0))
    return pl.pallas_call(
        body, name="loss_head", grid=(b, s // ts),
        in_specs=[row, row, per_b, vec, row],
        out_specs=[row, row, per_b, vec, pl.BlockSpec((1, 128), lambda i, j: (0, 0))],
        out_shape=[jax.ShapeDtypeStruct((b, s, d), F32), jax.ShapeDtypeStruct((b, s, d), BF16),
                   jax.ShapeDtypeStruct((b, 1, d), F32), jax.ShapeDtypeStruct((1, d), F32),
                   jax.ShapeDtypeStruct((1, 128), F32)],
        compiler_params=_cparams(("arbitrary", "arbitrary")),
    )(x, out, gate, g_final, target)


def _head_out(o, zg, g):
    rinv = lax.rsqrt(jnp.mean(o * o, axis=-1, keepdims=True) + EPS)
    return ((o * rinv) * g) * (zg * _sigmoid(zg))


def _head_out_bwd(o, zg, g, dy):
    rinv = lax.rsqrt(jnp.mean(o * o, axis=-1, keepdims=True) + EPS)
    rn = o * rinv
    sg = _sigmoid(zg)
    sil = zg * sg
    dzg = dy * (rn * g) * (sg * (1.0 + zg * (1.0 - sg)))
    dg = jnp.sum(dy * rn * sil, axis=0, keepdims=True)
    drn = dy * g * sil
    do = rinv * drn - o * (rinv * rinv * rinv) * jnp.mean(drn * o, axis=-1, keepdims=True)
    return do, dzg, dg


def _head_spec(s):
    return pl.BlockSpec((None, s, HEAD_DIM), lambda b, h: (b, 0, h))


def _seg_spec(s, seg):
    return pl.BlockSpec((None, None, s, HEAD_DIM), lambda b, h: (seg, b, 0, h))


def _seg4_spec(s, group):
    return pl.BlockSpec((4, None, s, HEAD_DIM), lambda b, h: (group, b, 0, h))


SB_BLOCK = 256


def _softplus_parts(z):
    e = jnp.exp(-jnp.abs(z))
    sp = jnp.maximum(z, 0.0) + jnp.log1p(e)
    inv = 1.0 / (1.0 + e)
    sig = jnp.where(z >= 0.0, inv, e * inv)
    return sp, sig


def _split_dot(a, u):
    hi = a.astype(BF16)
    lo = (a - hi.astype(F32)).astype(BF16)
    return _dot(hi, u, NN) + _dot(lo, u, NN)


def _sb_fwd(proj8, g_sb):
    _, b, s, w = proj8.shape
    n_heads = w // HEAD_DIM
    tb = _tile(s, SB_BLOCK)
    nq = s // tb
    scale = 1.0 / math.sqrt(HEAD_DIM)

    def body(q_ref, k_ref, v_ref, zg_ref, g_ref, o_ref, tot_ref, y_ref):
        ri = lax.broadcasted_iota(jnp.int32, (tb, tb), 0)
        ci = lax.broadcasted_iota(jnp.int32, (tb, tb), 1)
        u_excl = (ri > ci).astype(BF16)
        g = g_ref[...]

        def qblock(i, _):
            rows = pl.ds(pl.multiple_of(i * tb, tb), tb)
            q = q_ref[rows, :].astype(BF16)

            def kblock(jj, carry):
                acc, csum = carry
                j = i - jj
                cols = pl.ds(pl.multiple_of(j * tb, tb), tb)
                k = k_ref[cols, :].astype(BF16)
                v = v_ref[cols, :].astype(BF16)
                z = _dot(q, k, NT) * scale
                valid = (ci < ri) | (jj > 0)
                sp, _ = _softplus_parts(z)
                l1m = jnp.where(valid, -sp, 0.0)
                suf = _split_dot(l1m, u_excl) + csum
                a = jnp.where(valid, jnp.exp((z - sp) + suf), 0.0)
                acc = acc + _dot(a.astype(BF16), v, NN)
                csum = csum + jnp.sum(l1m, axis=1, keepdims=True)
                return acc, csum

            acc, tot = lax.fori_loop(0, i + 1, kblock, (jnp.zeros((tb, HEAD_DIM), F32), jnp.zeros((tb, 1), F32)))
            o_ref[rows, :] = acc
            tot_ref[rows, :] = jnp.broadcast_to(tot, (tb, HEAD_DIM))
            y_ref[rows, :] = _head_out(acc, zg_ref[rows, :], g).astype(BF16)
            return 0

        lax.fori_loop(0, nq, qblock, 0)

    return pl.pallas_call(
        body, name="sb_fwd", grid=(b, n_heads),
        in_specs=[_seg_spec(s, 0), _seg_spec(s, 1), _seg_spec(s, 2), _seg_spec(s, 3),
                  pl.BlockSpec((1, HEAD_DIM), lambda i, h: (0, h))],
        out_specs=[_head_spec(s), _head_spec(s), _seg_spec(s, 0)],
        out_shape=[jax.ShapeDtypeStruct((b, s, w), F32), jax.ShapeDtypeStruct((b, s, w), F32),
                   jax.ShapeDtypeStruct((2, b, s, w), BF16)],
        compiler_params=_cparams(("parallel", "parallel")),
    )(proj8, proj8, proj8, proj8, g_sb)


def _sb_bwd(proj8, o_sb, tot_sb, dy2, g_sb):
    _, b, s, w = proj8.shape
    n_heads = w // HEAD_DIM
    tb = _tile(s, SB_BLOCK)
    nq = s // tb
    scale = 1.0 / math.sqrt(HEAD_DIM)

    def body(q_ref, k_ref, v_ref, zg_ref, o_ref, tot_ref, dy_ref, g_ref, dp_ref, dg_ref, do_s, dk_s, dv_s):
        dq_ref, dk_ref, dv_ref, dzg_ref = (dp_ref.at[n] for n in range(4))
        ri = lax.broadcasted_iota(jnp.int32, (tb, tb), 0)
        ci = lax.broadcasted_iota(jnp.int32, (tb, tb), 1)
        u_le = (ri <= ci).astype(BF16)
        u_lt = (ri < ci).astype(BF16)
        g = g_ref[...]

        def prologue(i, dg):
            rows = pl.ds(pl.multiple_of(i * tb, tb), tb)
            do, dzg, dgi = _head_out_bwd(o_ref[rows, :], zg_ref[rows, :], g, dy_ref[rows, :])
            dzg_ref[rows, :] = dzg.astype(BF16)
            do_s[rows, :] = do.astype(BF16)
            return dg + dgi

        dg_ref[...] = lax.fori_loop(0, nq, prologue, jnp.zeros((1, HEAD_DIM), F32))
        dk_s[...] = jnp.zeros_like(dk_s)
        dv_s[...] = jnp.zeros_like(dv_s)

        def qblock(i, _):
            rows = pl.ds(pl.multiple_of(i * tb, tb), tb)
            q = q_ref[rows, :].astype(BF16)
            do = do_s[rows, :]
            tot = tot_ref[rows, :][:, :1]

            def kblock(j, carry):
                dq, pre_l, pre_g = carry
                cols = pl.ds(pl.multiple_of(j * tb, tb), tb)
                k = k_ref[cols, :].astype(BF16)
                v = v_ref[cols, :].astype(BF16)
                z = _dot(q, k, NT) * scale
                valid = (ci < ri) | (j < i)
                sp, sig = _softplus_parts(z)
                l1m = jnp.where(valid, -sp, 0.0)
                suf = tot - (_split_dot(l1m, u_le) + pre_l)
                a = jnp.where(valid, jnp.exp((z - sp) + suf), 0.0)
                da = _dot(do, v, NT)
                gg = a * da
                big_g = _split_dot(gg, u_lt) + pre_g
                dz = jnp.where(valid, gg * (1.0 - sig) - big_g * sig, 0.0)
                dzs = (dz * scale).astype(BF16)
                dq = dq + _dot(dzs, k, NN)
                dk_s[cols, :] += _dot(dzs, q, TN)
                dv_s[cols, :] += _dot(a.astype(BF16), do, TN)
                pre_l = pre_l + jnp.sum(l1m, axis=1, keepdims=True)
                pre_g = pre_g + jnp.sum(gg, axis=1, keepdims=True)
                return dq, pre_l, pre_g

            zero = jnp.zeros((tb, 1), F32)
            dq, _, _ = lax.fori_loop(0, i + 1, kblock, (jnp.zeros((tb, HEAD_DIM), F32), zero, zero))
            dq_ref[rows, :] = dq.astype(BF16)
            return 0

        lax.fori_loop(0, nq, qblock, 0)
        dk_ref[...] = dk_s[...].astype(BF16)
        dv_ref[...] = dv_s[...].astype(BF16)

    return pl.pallas_call(
        body, name="sb_bwd", grid=(b, n_heads),
        in_specs=[_seg_spec(s, 0), _seg_spec(s, 1), _seg_spec(s, 2), _seg_spec(s, 3),
                  _head_spec(s), _head_spec(s), _seg_spec(s, 0), pl.BlockSpec((1, HEAD_DIM), lambda i, h: (0, h))],
        out_specs=[_seg4_spec(s, 0), pl.BlockSpec((None, 1, HEAD_DIM), lambda i, h: (i, 0, h))],
        out_shape=[jax.ShapeDtypeStruct((8, b, s, w), BF16), jax.ShapeDtypeStruct((b, 1, w), F32)],
        scratch_shapes=[pltpu.VMEM((s, HEAD_DIM), BF16), pltpu.VMEM((s, HEAD_DIM), F32),
                        pltpu.VMEM((s, HEAD_DIM), F32)],
        compiler_params=_cparams(("parallel", "parallel")),
    )(proj8, proj8, proj8, proj8, o_sb, tot_sb, dy2, g_sb)


DIL_BLOCK = 128


def _dil_chunks(s, r):
    length = s // r
    out = []
    for rho in range(r):
        for cc in range(length // DIL_BLOCK):
            if r == 1:
                nat = pl.ds(cc * DIL_BLOCK, DIL_BLOCK)
            else:
                nat = pl.ds(rho + r * DIL_BLOCK * cc, DIL_BLOCK, stride=r)
            out.append((nat, pl.ds(rho * length + cc * DIL_BLOCK, DIL_BLOCK)))
    return out


def _dil_scores(q, kc, kp, bias_c, bias_p, first):
    n = DIL_BLOCK
    ri = lax.broadcasted_iota(jnp.int32, (n, n), 0)
    ci = lax.broadcasted_iota(jnp.int32, (n, n), 1)
    scale = 1.0 / math.sqrt(HEAD_DIM)
    s_c = jnp.where(ci <= ri, _dot(q, kc, NT) * scale - bias_c, NEG_BIG)
    s_p = jnp.where((ci >= ri) & jnp.logical_not(first), _dot(q, kp, NT) * scale - bias_p, NEG_BIG)
    return s_c, s_p


def _dil_bias(slope, r):
    n = DIL_BLOCK
    ri = lax.broadcasted_iota(jnp.int32, (n, n), 0)
    ci = lax.broadcasted_iota(jnp.int32, (n, n), 1)
    steps = (ri - ci).astype(F32)
    return slope * (steps * r), slope * ((steps + n) * r)


def _dil_check(s):
    for window, r in DIL_PAIRS:
        assert window // r == DIL_BLOCK and s % (r * DIL_BLOCK) == 0, (s, window, r)


def _dil_fwd(proj8, g_dil, slopes, y2):
    _, b, s, w = proj8.shape
    n_heads = w // HEAD_DIM
    _dil_check(s)
    n = DIL_BLOCK
    nt = s // n

    def body(q_ref, k_ref, v_ref, zg_ref, g_ref, sl_ref, y_in, o_ref, lse_ref, y_ref,
             qp, kp, vp, pnum, pm, pl_, acc_s, m_s, l_s):
        del y_in
        slope = sl_ref[...][:, :1]

        for (window, r) in DIL_PAIRS:
            nb = (s // r) // n
            bias_c, bias_p = _dil_bias(slope, float(r))
            for nat, per in _dil_chunks(s, r):
                qp[per, :] = q_ref[nat, :].astype(BF16)
                kp[per, :] = k_ref[nat, :].astype(BF16)
                vp[per, :] = v_ref[nat, :].astype(BF16)
            num_t, m_t, l_t = (acc_s, m_s, l_s) if r == 1 else (pnum, pm, pl_)

            def tile(t, _):
                rows = pl.ds(pl.multiple_of(t * n, n), n)
                prow = pl.ds(pl.multiple_of(jnp.maximum(t - 1, 0) * n, n), n)
                first = lax.rem(t, nb) == 0
                s_c, s_p = _dil_scores(qp[rows, :], kp[rows, :], kp[prow, :], bias_c, bias_p, first)
                m = jnp.maximum(jnp.max(s_c, axis=1, keepdims=True), jnp.max(s_p, axis=1, keepdims=True))
                p_c = jnp.exp(s_c - m)
                p_p = jnp.exp(s_p - m)
                l = jnp.sum(p_c, axis=1, keepdims=True) + jnp.sum(p_p, axis=1, keepdims=True)
                num_t[rows, :] = _dot(p_c.astype(BF16), vp[rows, :], NN) + _dot(p_p.astype(BF16), vp[prow, :], NN)
                m_t[rows, :] = jnp.broadcast_to(m, (n, HEAD_DIM))
                l_t[rows, :] = jnp.broadcast_to(l, (n, HEAD_DIM))
                return 0

            lax.fori_loop(0, nt, tile, 0)
            if r != 1:
                for nat, per in _dil_chunks(s, r):
                    m_old, m_new_p = m_s[nat, :], pm[per, :]
                    m_new = jnp.maximum(m_old, m_new_p)
                    a_old, a_p = jnp.exp(m_old - m_new), jnp.exp(m_new_p - m_new)
                    m_s[nat, :] = m_new
                    l_s[nat, :] = l_s[nat, :] * a_old + pl_[per, :] * a_p
                    acc_s[nat, :] = acc_s[nat, :] * a_old + pnum[per, :] * a_p

        g = g_ref[...]

        def finish(t, _):
            rows = pl.ds(pl.multiple_of(t * n, n), n)
            l = l_s[rows, :]
            o = acc_s[rows, :] / l
            o_ref[rows, :] = o
            lse_ref[rows, :] = m_s[rows, :] + jnp.log(l)
            y_ref[rows, :] = _head_out(o, zg_ref[rows, :], g).astype(BF16)
            return 0

        lax.fori_loop(0, nt, finish, 0)

    f32_s = pltpu.VMEM((s, HEAD_DIM), F32)
    bf_s = pltpu.VMEM((s, HEAD_DIM), BF16)
    return pl.pallas_call(
        body, name="dil_fwd", grid=(b, n_heads),
        in_specs=[_seg_spec(s, 4), _seg_spec(s, 5), _seg_spec(s, 6), _seg_spec(s, 7),
                  pl.BlockSpec((1, HEAD_DIM), lambda i, h: (0, h)),
                  pl.BlockSpec((None, 1, HEAD_DIM), lambda i, h: (h, 0, 0)), ANY],
        out_specs=[_head_spec(s), _head_spec(s), _seg_spec(s, 1)],
        out_shape=[jax.ShapeDtypeStruct((b, s, w), F32), jax.ShapeDtypeStruct((b, s, w), F32),
                   jax.ShapeDtypeStruct((2, b, s, w), BF16)],
        scratch_shapes=[bf_s, bf_s, bf_s, f32_s, f32_s, f32_s, f32_s, f32_s, f32_s],
        input_output_aliases={6: 2},
        compiler_params=_cparams(("parallel", "parallel")),
    )(proj8, proj8, proj8, proj8, g_dil, slopes, y2)


def _dil_bwd(proj8, o_dl, lse_dl, dy2, g_dil, slopes, dproj8):
    _, b, s, w = proj8.shape
    n_heads = w // HEAD_DIM
    _dil_check(s)
    n = DIL_BLOCK
    nt = s // n
    scale = 1.0 / math.sqrt(HEAD_DIM)

    def body(q_ref, k_ref, v_ref, zg_ref, o_ref, lse_ref, dy_ref, g_ref, sl_ref, dp_in, dp_ref, dg_ref,
             do_n, dt_n, dq_n, dk_n, dv_n, qp, kp, vp, dop, dtp, lsep, pdq, pdk, pdv):
        del dp_in
        dq_ref, dk_ref, dv_ref, dzg_ref = (dp_ref.at[i] for i in range(4))
        slope = sl_ref[...][:, :1]
        g = g_ref[...]

        def prologue(t, dg):
            rows = pl.ds(pl.multiple_of(t * n, n), n)
            o = o_ref[rows, :]
            do, dzg, dgi = _head_out_bwd(o, zg_ref[rows, :], g, dy_ref[rows, :])
            dzg_ref[rows, :] = dzg.astype(BF16)
            do_n[rows, :] = do
            dt_n[rows, :] = jnp.broadcast_to(jnp.sum(do * o, axis=-1, keepdims=True), (n, HEAD_DIM))
            return dg + dgi

        dg_ref[...] = lax.fori_loop(0, nt, prologue, jnp.zeros((1, HEAD_DIM), F32))
        dq_n[...] = jnp.zeros_like(dq_n)
        dk_n[...] = jnp.zeros_like(dk_n)
        dv_n[...] = jnp.zeros_like(dv_n)

        for (window, r) in DIL_PAIRS:
            nb = (s // r) // n
            bias_c, bias_p = _dil_bias(slope, float(r))
            for nat, per in _dil_chunks(s, r):
                qp[per, :] = q_ref[nat, :].astype(BF16)
                kp[per, :] = k_ref[nat, :].astype(BF16)
                vp[per, :] = v_ref[nat, :].astype(BF16)
                dop[per, :] = do_n[nat, :].astype(BF16)
                dtp[per, :] = dt_n[nat, :]
                lsep[per, :] = lse_ref[nat, :]
            pdk[...] = jnp.zeros_like(pdk)
            pdv[...] = jnp.zeros_like(pdv)

            def tile(t, _):
                rows = pl.ds(pl.multiple_of(t * n, n), n)
                prow = pl.ds(pl.multiple_of(jnp.maximum(t - 1, 0) * n, n), n)
                first = lax.rem(t, nb) == 0
                q, kc, kpv, do = qp[rows, :], kp[rows, :], kp[prow, :], dop[rows, :]
                s_c, s_p = _dil_scores(q, kc, kpv, bias_c, bias_p, first)
                lse = lsep[rows, :][:, :1]
                dt = dtp[rows, :][:, :1]
                p_c = jnp.exp(s_c - lse)
                p_p = jnp.exp(s_p - lse)
                ds_c = ((p_c * (_dot(do, vp[rows, :], NT) - dt)) * scale).astype(BF16)
                ds_p = ((p_p * (_dot(do, vp[prow, :], NT) - dt)) * scale).astype(BF16)
                pdq[rows, :] = _dot(ds_c, kc, NN) + _dot(ds_p, kpv, NN)
                pdk[rows, :] += _dot(ds_c, q, TN)
                pdk[prow, :] += _dot(ds_p, q, TN)
                pdv[rows, :] += _dot(p_c.astype(BF16), do, TN)
                pdv[prow, :] += _dot(p_p.astype(BF16), do, TN)
                return 0

            lax.fori_loop(0, nt, tile, 0)
            for nat, per in _dil_chunks(s, r):
                dq_n[nat, :] += pdq[per, :]
                dk_n[nat, :] += pdk[per, :]
                dv_n[nat, :] += pdv[per, :]

        dq_ref[...] = dq_n[...].astype(BF16)
        dk_ref[...] = dk_n[...].astype(BF16)
        dv_ref[...] = dv_n[...].astype(BF16)

    f32_s = pltpu.VMEM((s, HEAD_DIM), F32)
    bf_s = pltpu.VMEM((s, HEAD_DIM), BF16)
    return pl.pallas_call(
        body, name="dil_bwd", grid=(b, n_heads),
        in_specs=[_seg_spec(s, 4), _seg_spec(s, 5), _seg_spec(s, 6), _seg_spec(s, 7),
                  _head_spec(s), _head_spec(s), _seg_spec(s, 1),
                  pl.BlockSpec((1, HEAD_DIM), lambda i, h: (0, h)),
                  pl.BlockSpec((None, 1, HEAD_DIM), lambda i, h: (h, 0, 0)), ANY],
        out_specs=[_seg4_spec(s, 1), pl.BlockSpec((None, 1, HEAD_DIM), lambda i, h: (i, 0, h))],
        out_shape=[jax.ShapeDtypeStruct((8, b, s, w), BF16), jax.ShapeDtypeStruct((b, 1, w), F32)],
        scratch_shapes=[f32_s] * 5 + [bf_s] * 4 + [f32_s] * 5,
        input_output_aliases={9: 0},
        compiler_params=_cparams(("parallel", "parallel")),
    )(proj8, proj8, proj8, proj8, o_dl, lse_dl, dy2, g_dil, slopes, dproj8)


def _small_update(gathered, n_b, params, m, v):
    n_dev, _, width = gathered.shape

    def body(g_ref, p_ref, m_ref, v_ref, grad_ref, d_ref, nm_ref, nv_ref, loss_ref):
        for row in range(2):
            acc = None
            for dev in range(n_dev):
                for i in range(n_b):
                    term = g_ref[dev, pl.ds(row * n_b + i, 1), :]
                    acc = term if acc is None else acc + term
            grad_ref[pl.ds(row, 1), :] = acc
        loss = g_ref[0, pl.ds(2 * n_b, 1), pl.ds(0, 128)]
        for dev in range(1, n_dev):
            loss = loss + g_ref[dev, pl.ds(2 * n_b, 1), pl.ds(0, 128)]
        loss_ref[...] = loss
        d, nm, nv = _adamw_math(p_ref[...], grad_ref[...], m_ref[...], v_ref[...])
        d_ref[...] = d
        nm_ref[...] = nm
        nv_ref[...] = nv

    sds = jax.ShapeDtypeStruct((2, width), F32)
    return pl.pallas_call(
        body, name="small_update",
        in_specs=[VMEM_SPEC] * 4, out_specs=[VMEM_SPEC] * 5,
        out_shape=[sds, sds, sds, sds, jax.ShapeDtypeStruct((1, 128), F32)],
        compiler_params=_cparams(),
    )(gathered, params, m, v)


def _wada_update(c_t, dmod, w, m, v):
    d, bt = c_t.shape
    _, n = dmod.shape
    tr, tc = _tile(d, 512), _tile(n, 1024)

    def body(c_ref, dm_ref, w_ref, m_ref, v_ref, g_ref, d_ref, nm_ref, nv_ref):
        cv = c_ref[...]
        cs = (cv * _sigmoid(cv)).astype(BF16)
        grad = _dot(cs, dm_ref[...].astype(BF16), NN)
        g_ref[...] = grad
        dl, nm, nv = _adamw_math(w_ref[...], grad, m_ref[...], v_ref[...])
        d_ref[...] = dl
        nm_ref[...] = nm
        nv_ref[...] = nv

    spec = pl.BlockSpec((tr, tc), lambda i, j: (i, j))
    sds = jax.ShapeDtypeStruct((d, n), F32)
    return pl.pallas_call(
        body, name="wada_update", grid=(d // tr, n // tc),
        in_specs=[pl.BlockSpec((tr, bt), lambda i, j: (i, 0)), pl.BlockSpec((bt, tc), lambda i, j: (0, j)),
                  spec, spec, spec],
        out_specs=[spec] * 4, out_shape=[sds] * 4,
        compiler_params=_cparams(("parallel", "parallel")),
    )(c_t, dmod, w, m, v)


def _reduce_update(gs, w, m, v, tag):
    ra = _sibling_half_swap(gs, "swap_" + tag)
    pa = _add_own_half(gs, ra, "pair_sum_" + tag)
    rb = _scatter_chips(pa, "scatter_" + tag)
    half = _sum_slabs(rb, "chip_sum_" + tag)
    grad = _sibling_join(half, "join_" + tag)
    delta, nm, nv = _adamw(w, grad, m, v, "adamw_" + tag)
    return grad, delta, nm, nv


def kernel(x, c, w_ada, b_ada, g_norm, w_in, g_sb, g_dil, w_out, g_final, loss_target, m_w_ada, m_b_ada, m_g_norm, m_w_in, m_g_sb, m_g_dil, m_w_out, m_g_final, v_w_ada, v_b_ada, v_g_norm, v_w_in, v_g_sb, v_g_dil, v_w_out, v_g_final):
    nb, s, d = x.shape
    t = nb * s
    na = w_ada.shape[2]
    cs = w_in.shape[2]
    w = cs // 2
    n_heads = w // HEAD_DIM
    r_out = w_out.shape[1]
    assert 2 * nb + 1 <= 8 and 2 * d + 2 * w <= 3 * d and N_CHIPS * na == 3 * d and N_CHIPS * r_out == 2 * w
    xi, yi, ci = _place()
    chip = 2 * xi + yi
    dev = 2 * chip + ci

    c_all = _allgather8(jnp.pad(c, ((0, 8 - nb), (0, 0))), "gather_c")
    c16 = c_all.reshape(N_DEV, 8, d)[:, :nb].reshape(N_DEV * nb, d)
    b_ada_shard = lax.dynamic_slice(b_ada, (0, chip * na), (1, na))
    mod_part = _mod_fwd(c16, w_ada[0], b_ada_shard)
    mod_all = _allgather8(mod_part, "gather_mod")
    mod_full = mod_all.reshape(N_CHIPS, 2, N_DEV * nb, na)[:, 0].transpose(1, 0, 2).reshape(N_DEV * nb, 3 * d)
    mod = lax.dynamic_slice(mod_full, (dev * nb, 0), (nb, 3 * d))
    shift, scale, gate = (mod[:, i * d:(i + 1) * d].reshape(nb, 1, d) for i in range(3))

    h = _norm_mod_fwd(x, g_norm, scale, shift)
    h2 = h.reshape(t, d)
    ws_in = _allgather_chips(_cast_bf16(w_in[0], "cast_w_in"), "gather_w_in")
    ws_out = _allgather_chips(_cast_bf16(w_out[0], "cast_w_out"), "gather_w_out")
    w_out_full = ws_out.reshape(2 * w, d)

    proj8 = _proj_fwd(h2, ws_in).reshape(8, nb, s, w)
    slopes = jnp.exp2(-ALIBI_MAX_BIAS * jnp.arange(1, n_heads + 1, dtype=F32) / n_heads)
    slopes = jnp.broadcast_to(slopes[:, None, None], (n_heads, 1, HEAD_DIM))
    o_sb, tot_sb, y2 = _sb_fwd(proj8, g_sb)
    o_dl, lse_dl, y2 = _dil_fwd(proj8, g_dil, slopes, y2)
    y2f = y2.reshape(2, t, w)
    out = _out_fwd(y2f, w_out_full)

    dx1, dout, dgate, dg_final, loss_part = _loss_head(
        x, out.reshape(nb, s, d), gate, g_final.reshape(1, d), loss_target)
    dout2 = dout.reshape(t, d)
    dy2 = _out_bwd_y(dout2, w_out_full).reshape(2, nb, s, w)
    gs_out = _out_bwd_w(y2f, dout2).reshape(N_CHIPS, r_out, d)
    dproj8, dg_sb = _sb_bwd(proj8, o_sb, tot_sb, dy2, g_sb)
    dproj8, dg_dl = _dil_bwd(proj8, o_dl, lse_dl, dy2, g_dil, slopes, dproj8)
    dproj8 = dproj8.reshape(8, t, w)
    dh = _proj_bwd_x(dproj8, ws_in)
    gs_in = _proj_bwd_w(h2, dproj8)
    grad_x, dshift, dscale, dg_norm = _norm_mod_bwd(x, dh.reshape(nb, s, d), dx1, g_norm, scale)

    width = 3 * d
    dmod = jnp.concatenate([dshift, dscale, dgate], axis=-1).reshape(nb, width)
    gains = jnp.concatenate([dg_sb.reshape(nb, w), dg_dl.reshape(nb, w)], axis=-1)
    gains = jnp.pad(gains, ((0, 0), (2 * d, width - 2 * d - 2 * w)))
    first = jnp.pad(jnp.concatenate([dg_norm, dg_final], axis=-1), ((0, nb - 1), (0, width - 2 * d)))
    loss_row = jnp.pad(loss_part, ((0, 0), (0, width - 128)))
    pack = jnp.concatenate([dmod, gains + first, loss_row, jnp.zeros((8 - 2 * nb - 1, width), F32)], axis=0)
    gathered = _allgather8(pack, "gather_small").reshape(N_DEV, 8, width)

    def stack(bias, gn, gf, gsb, gdl):
        row1 = jnp.concatenate([gn.reshape(1, d), gf.reshape(1, d), gsb.reshape(1, w), gdl.reshape(1, w)], axis=-1)
        return jnp.concatenate([bias.reshape(1, width), jnp.pad(row1, ((0, 0), (0, width - 2 * d - 2 * w)))], axis=0)

    small = _small_update(
        gathered, nb, stack(b_ada, g_norm, g_final, g_sb, g_dil),
        stack(m_b_ada, m_g_norm, m_g_final, m_g_sb, m_g_dil), stack(v_b_ada, v_g_norm, v_g_final, v_g_sb, v_g_dil))
    loss = small[4][0, 0]

    def unstack(a):
        return (a[0:1, :], a[1:2, 0:d], a[1, d:2 * d], a[1:2, 2 * d:2 * d + w], a[1:2, 2 * d + w:2 * d + 2 * w])

    (g_b, g_gn, g_gf, g_gsb, g_gdl), (d_b, d_gn, d_gf, d_gsb, d_gdl), (nm_b, nm_gn, nm_gf, nm_gsb, nm_gdl), \
        (nv_b, nv_gn, nv_gf, nv_gsb, nv_gdl) = (unstack(a) for a in small[:4])

    dmod_all = gathered[:, :nb].reshape(N_DEV * nb, width)
    dmod_cols = lax.dynamic_slice(dmod_all, (0, chip * na), (N_DEV * nb, na))
    g_wa, d_wa, nm_wa, nv_wa = _wada_update(c16.T, dmod_cols, w_ada[0], m_w_ada[0], v_w_ada[0])

    g_wi, d_wi, nm_wi, nv_wi = _reduce_update(gs_in, w_in[0], m_w_in[0], v_w_in[0], "w_in")
    g_wo, d_wo, nm_wo, nv_wo = _reduce_update(gs_out, w_out[0], m_w_out[0], v_w_out[0], "w_out")

    lead = lambda a: a[None]
    return (loss, grad_x,
            lead(g_wa), g_b, g_gn, lead(g_wi), g_gsb, g_gdl, lead(g_wo), g_gf,
            lead(d_wa), d_b, d_gn, lead(d_wi), d_gsb, d_gdl, lead(d_wo), d_gf,
            lead(nm_wa), nm_b, nm_gn, lead(nm_wi), nm_gsb, nm_gdl, lead(nm_wo), nm_gf,
            lead(nv_wa), nv_b, nv_gn, lead(nv_wi), nv_gsb, nv_gdl, lead(nv_wo), nv_gf)
```

```python
import functools
import math

import jax
import jax.numpy as jnp
from jax import lax
from jax.experimental import pallas as pl
from jax.experimental.pallas import tpu as pltpu

F32 = jnp.float32
BF16 = jnp.bfloat16
MESH = pl.DeviceIdType.MESH

HEAD_DIM = 128
EPS = 1e-6
DIL_PAIRS = ((128, 1), (512, 4), (2048, 16))
ALIBI_MAX_BIAS = 8.0
ADAM_LR = 0.001
ADAM_B1 = 0.9
ADAM_B2 = 0.999
ADAM_EPS = 1e-08
ADAM_WD = 0.01
ADAM_STEP = 10
N_CHIPS = 4
N_DEV = 8
VMEM_LIMIT_BYTES = 56 * 1024 * 1024
NEG_BIG = -1e30

NN = (((1,), (0,)), ((), ()))
NT = (((1,), (1,)), ((), ()))
TN = (((0,), (0,)), ((), ()))

ANY = pl.BlockSpec(memory_space=pl.ANY)
VMEM_SPEC = pl.BlockSpec(memory_space=pltpu.VMEM)


def _cparams(sem=None):
    return pltpu.CompilerParams(dimension_semantics=sem, vmem_limit_bytes=VMEM_LIMIT_BYTES)


def _tile(dim, pref):
    t = min(dim, pref)
    assert dim % t == 0, (dim, pref)
    return t


def _dot(a, b, dims):
    return lax.dot_general(a, b, dims, preferred_element_type=F32)


def _sigmoid(x):
    return 1.0 / (1.0 + jnp.exp(-x))


def _place():
    return lax.axis_index("x"), lax.axis_index("y"), lax.axis_index("c")


def _allgather8(x_shard, name):
    m_per, n = x_shard.shape

    def body(x_ref, out_ref, send_sems, recv_sems, local_sem):
        x, y, c = _place()
        me, sibling = (x, y, c), (x, y, 1 - c)
        chips = [(1 - x, y), (x, 1 - y), (1 - x, 1 - y)]

        def rows(px, py, pc):
            return out_ref.at[pl.ds((4 * px + 2 * py + pc) * m_per, m_per), :]

        def copy(k, block, to, src=None):
            return pltpu.make_async_remote_copy(
                src_ref=rows(*block) if src is None else src, dst_ref=rows(*block),
                send_sem=send_sems.at[k], recv_sem=recv_sems.at[k], device_id=to, device_id_type=MESH)

        mine = pltpu.make_async_copy(x_ref, rows(*me), local_sem)
        mine.start()
        first = [copy(0, me, sibling, src=x_ref)]
        first += [copy(1 + j, me, (*chip, c), src=x_ref) for j, chip in enumerate(chips)]
        for cp in first:
            cp.start()
        passed = [copy(4 + j, (*chip, c), sibling) for j, chip in enumerate(chips)]
        for j, chip in enumerate(chips):
            copy(1 + j, (*chip, c), me).wait_recv()
            passed[j].start()
        copy(0, sibling, me).wait_recv()
        for j, chip in enumerate(chips):
            copy(4 + j, (*chip, 1 - c), me).wait_recv()
        for cp in first + passed:
            cp.wait_send()
        mine.wait()

    return pl.pallas_call(
        body, name=name,
        out_shape=jax.ShapeDtypeStruct((N_DEV * m_per, n), x_shard.dtype),
        in_specs=[VMEM_SPEC], out_specs=VMEM_SPEC,
        scratch_shapes=[pltpu.SemaphoreType.DMA((7,)), pltpu.SemaphoreType.DMA((7,)), pltpu.SemaphoreType.DMA],
    )(x_shard)


def _allgather_chips(shard, name):
    r, cdim = shard.shape
    half = r // 2

    def body(s_ref, out_ref, send_sems, recv_sems, local_sem):
        x, y, c = _place()
        sibling = (x, y, 1 - c)
        chips = [(1 - x, y), (x, 1 - y), (1 - x, 1 - y)]

        def rows(px, py, pc):
            return out_ref.at[2 * px + py, pl.ds(pc * half, half), :]

        def copy(k, block, to, src=None):
            return pltpu.make_async_remote_copy(
                src_ref=rows(*block) if src is None else src, dst_ref=rows(*block),
                send_sem=send_sems.at[k], recv_sem=recv_sems.at[k], device_id=to, device_id_type=MESH)

        mine = pltpu.make_async_copy(s_ref, out_ref.at[2 * x + y], local_sem)
        mine.start()
        my_half = s_ref.at[pl.ds(c * half, half), :]
        first = [copy(j, (x, y, c), (*chip, c), src=my_half) for j, chip in enumerate(chips)]
        for cp in first:
            cp.start()
        passed = [copy(3 + j, (*chip, c), sibling) for j, chip in enumerate(chips)]
        for j, chip in enumerate(chips):
            copy(j, (*chip, c), (x, y, c)).wait_recv()
            passed[j].start()
        for j, chip in enumerate(chips):
            copy(3 + j, (*chip, 1 - c), (x, y, c)).wait_recv()
        for cp in first + passed:
            cp.wait_send()
        mine.wait()

    return pl.pallas_call(
        body, name=name,
        out_shape=jax.ShapeDtypeStruct((N_CHIPS, r, cdim), shard.dtype),
        in_specs=[ANY], out_specs=ANY,
        scratch_shapes=[pltpu.SemaphoreType.DMA((6,)), pltpu.SemaphoreType.DMA((6,)), pltpu.SemaphoreType.DMA],
    )(shard)


def _sibling_half_swap(gs, name):
    n, r, cdim = gs.shape
    half = r // 2

    def body(g_ref, out_ref, send_sem, recv_sem):
        x, y, c = _place()
        cp = pltpu.make_async_remote_copy(
            src_ref=g_ref.at[:, pl.ds((1 - c) * half, half), :], dst_ref=out_ref,
            send_sem=send_sem, recv_sem=recv_sem, device_id=(x, y, 1 - c), device_id_type=MESH)
        cp.start()
        cp.wait()

    return pl.pallas_call(
        body, name=name,
        out_shape=jax.ShapeDtypeStruct((n, half, cdim), gs.dtype),
        in_specs=[ANY], out_specs=ANY,
        scratch_shapes=[pltpu.SemaphoreType.DMA, pltpu.SemaphoreType.DMA],
    )(gs)


class _ScatterChips:
    def __init__(self, pa):
        self.inputs = [pa]
        self.out_shapes = [jax.ShapeDtypeStruct(pa.shape, pa.dtype)]
        self.scratch = [pltpu.SemaphoreType.DMA((3,)), pltpu.SemaphoreType.DMA((3,)), pltpu.SemaphoreType.DMA]

    @staticmethod
    def _mine(p_ref, out_ref, send_sems, recv_sems, local_sem):
        x, y, _ = _place()
        return pltpu.make_async_copy(p_ref.at[2 * x + y], out_ref.at[2 * x + y], local_sem)

    @staticmethod
    def _remote(p_ref, out_ref, send_sems, recv_sems, local_sem, incoming):
        x, y, c = _place()
        me = 2 * x + y
        remote = []
        for j, (px, py) in enumerate([(1 - x, y), (x, 1 - y), (1 - x, 1 - y)]):
            remote.append(pltpu.make_async_remote_copy(
                src_ref=p_ref.at[me if incoming else 2 * px + py], dst_ref=out_ref.at[2 * px + py if incoming else me],
                send_sem=send_sems.at[j], recv_sem=recv_sems.at[j], device_id=(px, py, c), device_id_type=MESH))
        return remote

    def start(self, *refs):
        self._mine(*refs).start()
        for cp in self._remote(*refs, incoming=False):
            cp.start()

    def wait(self, *refs):
        for cp in self._remote(*refs, incoming=True):
            cp.wait_recv()
        for cp in self._remote(*refs, incoming=False):
            cp.wait_send()
        self._mine(*refs).wait()


def _fused_specs(fused):
    if fused is None:
        return [], [], [], [], []
    return (list(fused.inputs), [ANY] * len(fused.inputs), list(fused.out_shapes), [ANY] * len(fused.out_shapes),
            list(fused.scratch))


def _fused_begin(fused, grid, refs):
    if fused is not None:
        first = functools.reduce(lambda p, q: p & q, [pl.program_id(i) == 0 for i in range(len(grid))])
        pl.when(first)(lambda: fused.start(*refs))


def _fused_end(fused, grid, refs):
    if fused is not None:
        last = functools.reduce(lambda p, q: p & q, [pl.program_id(i) == g - 1 for i, g in enumerate(grid)])
        pl.when(last)(lambda: fused.wait(*refs))


def _sibling_join(s_half, name):
    h, cdim = s_half.shape

    def body(s_ref, out_ref, send_sem, recv_sem, local_sem):
        x, y, c = _place()
        mine = pltpu.make_async_copy(s_ref, out_ref.at[pl.ds(c * h, h), :], local_sem)
        mine.start()
        cp = pltpu.make_async_remote_copy(
            src_ref=s_ref, dst_ref=out_ref.at[pl.ds(c * h, h), :],
            send_sem=send_sem, recv_sem=recv_sem, device_id=(x, y, 1 - c), device_id_type=MESH)
        cp.start()
        pltpu.make_async_remote_copy(
            src_ref=s_ref, dst_ref=out_ref.at[pl.ds((1 - c) * h, h), :],
            send_sem=send_sem, recv_sem=recv_sem, device_id=(x, y, 1 - c), device_id_type=MESH).wait_recv()
        cp.wait_send()
        mine.wait()

    return pl.pallas_call(
        body, name=name,
        out_shape=jax.ShapeDtypeStruct((2 * h, cdim), s_half.dtype),
        in_specs=[ANY], out_specs=ANY,
        scratch_shapes=[pltpu.SemaphoreType.DMA, pltpu.SemaphoreType.DMA, pltpu.SemaphoreType.DMA],
    )(s_half)


def _cast_bf16(w, name):
    r, cdim = w.shape
    tr, tc = _tile(r, 512), _tile(cdim, 2048)

    def body(w_ref, o_ref):
        o_ref[...] = w_ref[...].astype(BF16)

    return pl.pallas_call(
        body, name=name, grid=(r // tr, cdim // tc),
        in_specs=[pl.BlockSpec((tr, tc), lambda i, j: (i, j))],
        out_specs=pl.BlockSpec((tr, tc), lambda i, j: (i, j)),
        out_shape=jax.ShapeDtypeStruct((r, cdim), BF16),
        compiler_params=_cparams(("parallel", "parallel")),
    )(w)


def _place_scalars():
    x, y, c = _place()
    return jnp.stack([c, 2 * x + y]).astype(jnp.int32)


def _pair_sum(gs, ra, name):
    n, r, cdim = gs.shape
    half = r // 2
    tr, tc = _tile(half, 512), _tile(cdim, 2048)
    nt = half // tr

    def body(pc_ref, g_ref, r_ref, o_ref, own_ref):
        val = g_ref[...] + r_ref[...]
        o_ref[...] = val.astype(BF16)

        @pl.when(pl.program_id(2) == pc_ref[1])
        def _():
            own_ref[...] = val

    return pl.pallas_call(
        body, name=name,
        grid_spec=pltpu.PrefetchScalarGridSpec(
            num_scalar_prefetch=1, grid=(nt, cdim // tc, n),
            in_specs=[pl.BlockSpec((None, tr, tc), lambda i, j, s, pc: (s, pc[0] * nt + i, j)),
                      pl.BlockSpec((None, tr, tc), lambda i, j, s, pc: (s, i, j))],
            out_specs=[pl.BlockSpec((None, tr, tc), lambda i, j, s, pc: (s, i, j)),
                       pl.BlockSpec((tr, tc), lambda i, j, s, pc: (i, j))]),
        out_shape=[jax.ShapeDtypeStruct((n, half, cdim), BF16), jax.ShapeDtypeStruct((half, cdim), F32)],
        compiler_params=_cparams(("parallel", "parallel", "arbitrary")),
    )(_place_scalars(), gs, ra)


def _chip_sum(rb, own, name):
    n, h, cdim = rb.shape
    tr, tc = _tile(h, 256), _tile(cdim, 2048)

    def body(pc_ref, r_ref, own_ref, o_ref):
        chip = pc_ref[1]
        acc = None
        for p in range(n):
            term = jnp.where(chip == p, own_ref[...], r_ref[p].astype(F32))
            acc = term if acc is None else acc + term
        o_ref[...] = acc

    return pl.pallas_call(
        body, name=name,
        grid_spec=pltpu.PrefetchScalarGridSpec(
            num_scalar_prefetch=1, grid=(h // tr, cdim // tc),
            in_specs=[pl.BlockSpec((n, tr, tc), lambda i, j, pc: (0, i, j)),
                      pl.BlockSpec((tr, tc), lambda i, j, pc: (i, j))],
            out_specs=pl.BlockSpec((tr, tc), lambda i, j, pc: (i, j))),
        out_shape=jax.ShapeDtypeStruct((h, cdim), F32),
        compiler_params=_cparams(("parallel", "parallel")),
    )(_place_scalars(), rb, own)


def _adamw_math(w, g, m, v):
    m = ADAM_B1 * m + (1.0 - ADAM_B1) * g
    v = ADAM_B2 * v + (1.0 - ADAM_B2) * (g * g)
    m_hat = m / (1.0 - ADAM_B1 ** ADAM_STEP)
    v_hat = v / (1.0 - ADAM_B2 ** ADAM_STEP)
    delta = -ADAM_LR * (m_hat / (jnp.sqrt(v_hat) + ADAM_EPS) + ADAM_WD * w)
    return delta, m, v


def _adamw(w, g, m, v, name):
    r, cdim = w.shape
    tr, tc = _tile(r, 256), _tile(cdim, 2048)

    def body(w_ref, g_ref, m_ref, v_ref, d_ref, nm_ref, nv_ref):
        d, nm, nv = _adamw_math(w_ref[...], g_ref[...], m_ref[...], v_ref[...])
        d_ref[...] = d
        nm_ref[...] = nm
        nv_ref[...] = nv

    spec = pl.BlockSpec((tr, tc), lambda i, j: (i, j))
    sds = jax.ShapeDtypeStruct((r, cdim), F32)
    return pl.pallas_call(
        body, name=name, grid=(r // tr, cdim // tc),
        in_specs=[spec] * 4, out_specs=[spec] * 3, out_shape=[sds] * 3,
        compiler_params=_cparams(("parallel", "parallel")),
    )(w, g, m, v)


def _matmul(a, b, *, grid, a_spec, b_spec, out_spec, out_shape, acc_shape, dims, name, bias=None, bias_spec=None,
            silu_a=False, fused=None):
    nk = grid[2]
    f_in, f_in_specs, f_out, f_out_specs, f_scratch = _fused_specs(fused)
    n_in = 2 + (bias is not None)

    def body(*refs):
        a_ref, b_ref = refs[:2]
        bias_ref = refs[2] if bias is not None else None
        o_ref = refs[n_in + len(f_in)]
        acc_ref = refs[n_in + len(f_in) + 1 + len(f_out)]
        f_refs = (*refs[n_in:n_in + len(f_in)], *refs[n_in + len(f_in) + 1:n_in + len(f_in) + 1 + len(f_out)],
                  *refs[n_in + len(f_in) + 2 + len(f_out):])
        _fused_begin(fused, grid, f_refs)
        k = pl.program_id(2)

        @pl.when(k == 0)
        def _():
            acc_ref[...] = jnp.zeros_like(acc_ref)

        av = a_ref[...]
        if silu_a:
            av = av * _sigmoid(av)
        acc_ref[...] += _dot(av.astype(BF16), b_ref[...].astype(BF16), dims)

        @pl.when(k == nk - 1)
        def _():
            res = acc_ref[...]
            if bias is not None:
                res = res + bias_ref[...]
            o_ref[...] = res.astype(o_ref.dtype)

        _fused_end(fused, grid, f_refs)

    in_specs = [a_spec, b_spec] + ([] if bias is None else [bias_spec]) + f_in_specs
    args = (a, b) + (() if bias is None else (bias,)) + tuple(f_in)
    sem = ("parallel", "parallel", "arbitrary") if fused is None else ("arbitrary",) * 3
    res = pl.pallas_call(
        body, name=name, grid=grid, in_specs=in_specs, out_specs=[out_spec] + f_out_specs,
        out_shape=[out_shape] + f_out,
        scratch_shapes=[pltpu.VMEM(acc_shape, F32)] + f_scratch,
        compiler_params=_cparams(sem),
    )(*args)
    return res[0] if fused is None else tuple(res)


def _mm_tiles(m, n, k):
    return _tile(m, 1024), _tile(n, 1024), _tile(k, 512)


def _proj_fwd(h2, ws_in):
    t, d = h2.shape
    _, _, cs = ws_in.shape
    w = cs // 2
    tm, tn, tk = _mm_tiles(t, w, d)
    nps, npseg = cs // tn, w // tn
    return _matmul(
        h2, ws_in, grid=(t // tm, 8 * npseg, d // tk), dims=NN, name="proj_fwd",
        a_spec=pl.BlockSpec((tm, tk), lambda m, n, k: (m, k)),
        b_spec=pl.BlockSpec((None, tk, tn), lambda m, n, k: (n // nps, k, n % nps)),
        out_spec=pl.BlockSpec((None, tm, tn), lambda m, n, k: (n // npseg, m, n % npseg)),
        out_shape=jax.ShapeDtypeStruct((8, t, w), F32), acc_shape=(tm, tn))


def _proj_bwd_x(dproj8, ws_in, fused=None):
    _, t, w = dproj8.shape
    _, d, cs = ws_in.shape
    tm, tn, tk = _mm_tiles(t, d, w)
    kps, kpseg = cs // tk, w // tk
    return _matmul(
        dproj8, ws_in, grid=(t // tm, d // tn, 8 * kpseg), dims=NT, name="proj_bwd_x", fused=fused,
        a_spec=pl.BlockSpec((None, tm, tk), lambda m, n, k: (k // kpseg, m, k % kpseg)),
        b_spec=pl.BlockSpec((None, tn, tk), lambda m, n, k: (k // kps, n, k % kps)),
        out_spec=pl.BlockSpec((tm, tn), lambda m, n, k: (m, n)),
        out_shape=jax.ShapeDtypeStruct((t, d), F32), acc_shape=(tm, tn))


def _proj_bwd_w(h2, dproj8):
    t, d = h2.shape
    _, _, w = dproj8.shape
    cs = 2 * w
    tm, tn, tk = _mm_tiles(d, w, t)
    nps, npseg = cs // tn, w // tn
    return _matmul(
        h2, dproj8, grid=(d // tm, 8 * npseg, t // tk), dims=TN, name="proj_bwd_w",
        a_spec=pl.BlockSpec((tk, tm), lambda m, n, k: (k, m)),
        b_spec=pl.BlockSpec((None, tk, tn), lambda m, n, k: (n // npseg, k, n % npseg)),
        out_spec=pl.BlockSpec((None, tm, tn), lambda m, n, k: (n // nps, m, n % nps)),
        out_shape=jax.ShapeDtypeStruct((N_CHIPS, d, cs), F32), acc_shape=(tm, tn))


def _out_fwd(y2, w_out):
    _, t, w = y2.shape
    _, d = w_out.shape
    tm, tn, tk = _mm_tiles(t, d, w)
    kpg = w // tk
    return _matmul(
        y2, w_out, grid=(t // tm, d // tn, 2 * kpg), dims=NN, name="out_fwd",
        a_spec=pl.BlockSpec((None, tm, tk), lambda m, n, k: (k // kpg, m, k % kpg)),
        b_spec=pl.BlockSpec((tk, tn), lambda m, n, k: (k, n)),
        out_spec=pl.BlockSpec((tm, tn), lambda m, n, k: (m, n)),
        out_shape=jax.ShapeDtypeStruct((t, d), F32), acc_shape=(tm, tn))


def _out_bwd_y(dout, w_out):
    t, d = dout.shape
    w = w_out.shape[0] // 2
    tm, tn, tk = _mm_tiles(t, w, d)
    npg = w // tn
    return _matmul(
        dout, w_out, grid=(t // tm, 2 * npg, d // tk), dims=NT, name="out_bwd_y",
        a_spec=pl.BlockSpec((tm, tk), lambda m, n, k: (m, k)),
        b_spec=pl.BlockSpec((tn, tk), lambda m, n, k: (n, k)),
        out_spec=pl.BlockSpec((None, tm, tn), lambda m, n, k: (n // npg, m, n % npg)),
        out_shape=jax.ShapeDtypeStruct((2, t, w), F32), acc_shape=(tm, tn))


def _out_bwd_w(y2, dout):
    _, t, w = y2.shape
    _, d = dout.shape
    tm, tn, tk = _mm_tiles(w, d, t)
    mpg = w // tm
    return _matmul(
        y2, dout, grid=(2 * mpg, d // tn, t // tk), dims=TN, name="out_bwd_w",
        a_spec=pl.BlockSpec((None, tk, tm), lambda m, n, k: (m // mpg, k, m % mpg)),
        b_spec=pl.BlockSpec((tk, tn), lambda m, n, k: (k, n)),
        out_spec=pl.BlockSpec((tm, tn), lambda m, n, k: (m, n)),
        out_shape=jax.ShapeDtypeStruct((2 * w, d), F32), acc_shape=(tm, tn))


def _mod_fwd(c_all, w_ada, b_ada):
    bt, d = c_all.shape
    _, n = w_ada.shape
    tn, tk = _tile(n, 512), _tile(d, 1024)
    return _matmul(
        c_all, w_ada, grid=(1, n // tn, d // tk), dims=NN, name="mod_fwd", silu_a=True,
        a_spec=pl.BlockSpec((bt, tk), lambda i, j, l: (0, l)),
        b_spec=pl.BlockSpec((tk, tn), lambda i, j, l: (l, j)),
        bias=b_ada, bias_spec=pl.BlockSpec((1, tn), lambda i, j, l: (0, j)),
        out_spec=pl.BlockSpec((bt, tn), lambda i, j, l: (0, j)),
        out_shape=jax.ShapeDtypeStruct((bt, n), F32), acc_shape=(bt, tn))


def _norm_mod_fwd(x, g_norm, scale, shift):
    b, s, d = x.shape
    ts = _tile(s, 256)

    def body(x_ref, g_ref, sc_ref, sh_ref, h_ref):
        xv = x_ref[...]
        r = lax.rsqrt(jnp.mean(xv * xv, axis=-1, keepdims=True) + EPS)
        y = (xv * r) * g_ref[...]
        h_ref[...] = (y * (1.0 + sc_ref[...]) + sh_ref[...]).astype(BF16)

    row = pl.BlockSpec((None, ts, d), lambda i, j: (i, j, 0))
    per_b = pl.BlockSpec((None, 1, d), lambda i, j: (i, 0, 0))
    return pl.pallas_call(
        body, name="norm_mod_fwd", grid=(b, s // ts),
        in_specs=[row, pl.BlockSpec((1, d), lambda i, j: (0, 0)), per_b, per_b],
        out_specs=row, out_shape=jax.ShapeDtypeStruct((b, s, d), BF16),
        compiler_params=_cparams(("parallel", "parallel")),
    )(x, g_norm, scale, shift)


def _norm_mod_bwd(x, dh, dx1, g_norm, scale):
    b, s, d = x.shape
    ts = _tile(s, 256)

    def body(x_ref, dh_ref, dx1_ref, g_ref, sc_ref, gx_ref, dsh_ref, dsc_ref, dg_ref):
        i, j = pl.program_id(0), pl.program_id(1)

        @pl.when(j == 0)
        def _():
            dsh_ref[...] = jnp.zeros_like(dsh_ref)
            dsc_ref[...] = jnp.zeros_like(dsc_ref)

        @pl.when((i == 0) & (j == 0))
        def _():
            dg_ref[...] = jnp.zeros_like(dg_ref)

        xv, dhv, g = x_ref[...], dh_ref[...], g_ref[...]
        r = lax.rsqrt(jnp.mean(xv * xv, axis=-1, keepdims=True) + EPS)
        xh = xv * r
        dsh_ref[...] += jnp.sum(dhv, axis=0, keepdims=True)
        dsc_ref[...] += jnp.sum(dhv * (xh * g), axis=0, keepdims=True)
        dn = dhv * (1.0 + sc_ref[...])
        dg_ref[...] += jnp.sum(dn * xh, axis=0, keepdims=True)
        u = dn * g
        dx = r * u - xv * (r * r * r) * jnp.mean(u * xv, axis=-1, keepdims=True)
        gx_ref[...] = dx1_ref[...] + dx

    row = pl.BlockSpec((None, ts, d), lambda i, j: (i, j, 0))
    per_b = pl.BlockSpec((None, 1, d), lambda i, j: (i, 0, 0))
    vec = pl.BlockSpec((1, d), lambda i, j: (0, 0))
    return pl.pallas_call(
        body, name="norm_mod_bwd", grid=(b, s // ts),
        in_specs=[row, row, row, vec, per_b],
        out_specs=[row, per_b, per_b, vec],
        out_shape=[jax.ShapeDtypeStruct((b, s, d), F32), jax.ShapeDtypeStruct((b, 1, d), F32),
                   jax.ShapeDtypeStruct((b, 1, d), F32), jax.ShapeDtypeStruct((1, d), F32)],
        compiler_params=_cparams(("arbitrary", "arbitrary")),
    )(x, dh, dx1, g_norm, scale)


def _loss_head(x, out, gate, g_final, target):
    b, s, d = x.shape
    ts = _tile(s, 256)

    def body(x_ref, o_ref, gt_ref, g_ref, t_ref, dx1_ref, dout_ref, dgt_ref, dg_ref, loss_ref):
        i, j = pl.program_id(0), pl.program_id(1)

        @pl.when(j == 0)
        def _():
            dgt_ref[...] = jnp.zeros_like(dgt_ref)

        @pl.when((i == 0) & (j == 0))
        def _():
            dg_ref[...] = jnp.zeros_like(dg_ref)
            loss_ref[...] = jnp.zeros_like(loss_ref)

        ov, gt, g = o_ref[...], gt_ref[...], g_ref[...]
        x1 = x_ref[...] + gt * ov
        r = lax.rsqrt(jnp.mean(x1 * x1, axis=-1, keepdims=True) + EPS)
        xh = x1 * r
        err = xh * g - t_ref[...]
        loss_ref[...] += 0.5 * jnp.sum(jnp.mean(err * err, axis=-1, keepdims=True))
        dfin = err * (1.0 / d)
        dg_ref[...] += jnp.sum(dfin * xh, axis=0, keepdims=True)
        u = dfin * g
        dx1 = r * u - x1 * (r * r * r) * jnp.mean(u * x1, axis=-1, keepdims=True)
        dx1_ref[...] = dx1
        dgt_ref[...] += jnp.sum(dx1 * ov, axis=0, keepdims=True)
        dout_ref[...] = (gt * dx1).astype(BF16)

    row = pl.BlockSpec((None, ts, d), lambda i, j: (i, j, 0))
    per_b = pl.BlockSpec((None, 1, d), lambda i, j: (i, 0, 0))
    vec = pl.BlockSpec((1, d), lambda i, j: (0, 0))
    return pl.pallas_call(
        body, name="loss_head", grid=(b, s // ts),
        in_specs=[row, row, per_b, vec, row],
        out_specs=[row, row, per_b, vec, pl.BlockSpec((1, 128), lambda i, j: (0, 0))],
        out_shape=[jax.ShapeDtypeStruct((b, s, d), F32), jax.ShapeDtypeStruct((b, s, d), BF16),
                   jax.ShapeDtypeStruct((b, 1, d), F32), jax.ShapeDtypeStruct((1, d), F32),
                   jax.ShapeDtypeStruct((1, 128), F32)],
        compiler_params=_cparams(("arbitrary", "arbitrary")),
    )(x, out, gate, g_final, target)


def _head_out(o, zg, g):
    rinv = lax.rsqrt(jnp.mean(o * o, axis=-1, keepdims=True) + EPS)
    return ((o * rinv) * g) * (zg * _sigmoid(zg))


def _head_out_bwd(o, zg, g, dy):
    rinv = lax.rsqrt(jnp.mean(o * o, axis=-1, keepdims=True) + EPS)
    rn = o * rinv
    sg = _sigmoid(zg)
    sil = zg * sg
    dzg = dy * (rn * g) * (sg * (1.0 + zg * (1.0 - sg)))
    dg = jnp.sum(dy * rn * sil, axis=0, keepdims=True)
    drn = dy * g * sil
    do = rinv * drn - o * (rinv * rinv * rinv) * jnp.mean(drn * o, axis=-1, keepdims=True)
    return do, dzg, dg


def _head_spec(s):
    return pl.BlockSpec((None, s, HEAD_DIM), lambda b, h: (b, 0, h))


def _seg_spec(s, seg):
    return pl.BlockSpec((None, None, s, HEAD_DIM), lambda b, h: (seg, b, 0, h))


def _seg4_spec(s, group):
    return pl.BlockSpec((4, None, s, HEAD_DIM), lambda b, h: (group, b, 0, h))


SB_BLOCK = 256


def _softplus_parts(z):
    e = jnp.exp(-jnp.abs(z))
    sp = jnp.maximum(z, 0.0) + jnp.log1p(e)
    inv = 1.0 / (1.0 + e)
    sig = jnp.where(z >= 0.0, inv, e * inv)
    return sp, sig


def _split_dot(a, u):
    hi = a.astype(BF16)
    lo = (a - hi.astype(F32)).astype(BF16)
    return _dot(hi, u, NN) + _dot(lo, u, NN)


def _sb_fwd(proj8, g_sb):
    _, b, s, w = proj8.shape
    n_heads = w // HEAD_DIM
    tb = _tile(s, SB_BLOCK)
    nq = s // tb
    scale = 1.0 / math.sqrt(HEAD_DIM)

    def body(q_ref, k_ref, v_ref, zg_ref, g_ref, o_ref, tot_ref, y_ref):
        ri = lax.broadcasted_iota(jnp.int32, (tb, tb), 0)
        ci = lax.broadcasted_iota(jnp.int32, (tb, tb), 1)
        u_excl = (ri > ci).astype(BF16)
        g = g_ref[...]

        def qblock(i, _):
            rows = pl.ds(pl.multiple_of(i * tb, tb), tb)
            q = q_ref[rows, :].astype(BF16)

            def kblock(jj, carry):
                acc, csum = carry
                j = i - jj
                cols = pl.ds(pl.multiple_of(j * tb, tb), tb)
                k = k_ref[cols, :].astype(BF16)
                v = v_ref[cols, :].astype(BF16)
                z = _dot(q, k, NT) * scale
                valid = (ci < ri) | (jj > 0)
                sp, _ = _softplus_parts(z)
                l1m = jnp.where(valid, -sp, 0.0)
                suf = _split_dot(l1m, u_excl) + csum
                a = jnp.where(valid, jnp.exp((z - sp) + suf), 0.0)
                acc = acc + _dot(a.astype(BF16), v, NN)
                csum = csum + jnp.sum(l1m, axis=1, keepdims=True)
                return acc, csum

            acc, tot = lax.fori_loop(0, i + 1, kblock, (jnp.zeros((tb, HEAD_DIM), F32), jnp.zeros((tb, 1), F32)))
            o_ref[rows, :] = acc
            tot_ref[rows, :] = jnp.broadcast_to(tot, (tb, HEAD_DIM))
            y_ref[rows, :] = _head_out(acc, zg_ref[rows, :], g).astype(BF16)
            return 0

        lax.fori_loop(0, nq, qblock, 0)

    return pl.pallas_call(
        body, name="sb_fwd", grid=(b, n_heads),
        in_specs=[_seg_spec(s, 0), _seg_spec(s, 1), _seg_spec(s, 2), _seg_spec(s, 3),
                  pl.BlockSpec((1, HEAD_DIM), lambda i, h: (0, h))],
        out_specs=[_head_spec(s), _head_spec(s), _seg_spec(s, 0)],
        out_shape=[jax.ShapeDtypeStruct((b, s, w), F32), jax.ShapeDtypeStruct((b, s, w), F32),
                   jax.ShapeDtypeStruct((2, b, s, w), BF16)],
        compiler_params=_cparams(("parallel", "parallel")),
    )(proj8, proj8, proj8, proj8, g_sb)


def _sb_bwd(proj8, o_sb, tot_sb, dy2, g_sb, fused=None):
    _, b, s, w = proj8.shape
    n_heads = w // HEAD_DIM
    tb = _tile(s, SB_BLOCK)
    nq = s // tb
    scale = 1.0 / math.sqrt(HEAD_DIM)

    f_in, f_in_specs, f_out, f_out_specs, f_scratch = _fused_specs(fused)
    grid = (b, n_heads)

    def body(*refs):
        q_ref, k_ref, v_ref, zg_ref, o_ref, tot_ref, dy_ref, g_ref = refs[:8]
        dp_ref, dg_ref = refs[8 + len(f_in):10 + len(f_in)]
        do_s, dk_s, dv_s = refs[10 + len(f_in) + len(f_out):13 + len(f_in) + len(f_out)]
        f_refs = (*refs[8:8 + len(f_in)], *refs[10 + len(f_in):10 + len(f_in) + len(f_out)],
                  *refs[13 + len(f_in) + len(f_out):])
        _fused_begin(fused, grid, f_refs)
        dq_ref, dk_ref, dv_ref, dzg_ref = (dp_ref.at[n] for n in range(4))
        ri = lax.broadcasted_iota(jnp.int32, (tb, tb), 0)
        ci = lax.broadcasted_iota(jnp.int32, (tb, tb), 1)
        u_le = (ri <= ci).astype(BF16)
        u_lt = (ri < ci).astype(BF16)
        g = g_ref[...]

        def prologue(i, dg):
            rows = pl.ds(pl.multiple_of(i * tb, tb), tb)
            do, dzg, dgi = _head_out_bwd(o_ref[rows, :], zg_ref[rows, :], g, dy_ref[rows, :])
            dzg_ref[rows, :] = dzg.astype(BF16)
            do_s[rows, :] = do.astype(BF16)
            return dg + dgi

        dg_ref[...] = lax.fori_loop(0, nq, prologue, jnp.zeros((1, HEAD_DIM), F32))
        dk_s[...] = jnp.zeros_like(dk_s)
        dv_s[...] = jnp.zeros_like(dv_s)

        def qblock(i, _):
            rows = pl.ds(pl.multiple_of(i * tb, tb), tb)
            q = q_ref[rows, :].astype(BF16)
            do = do_s[rows, :]
            tot = tot_ref[rows, :][:, :1]

            def kblock(j, carry):
                dq, pre_l, pre_g = carry
                cols = pl.ds(pl.multiple_of(j * tb, tb), tb)
                k = k_ref[cols, :].astype(BF16)
                v = v_ref[cols, :].astype(BF16)
                z = _dot(q, k, NT) * scale
                valid = (ci < ri) | (j < i)
                sp, sig = _softplus_parts(z)
                l1m = jnp.where(valid, -sp, 0.0)
                suf = tot - (_split_dot(l1m, u_le) + pre_l)
                a = jnp.where(valid, jnp.exp((z - sp) + suf), 0.0)
                da = _dot(do, v, NT)
                gg = a * da
                big_g = _split_dot(gg, u_lt) + pre_g
                dz = jnp.where(valid, gg * (1.0 - sig) - big_g * sig, 0.0)
                dzs = (dz * scale).astype(BF16)
                dq = dq + _dot(dzs, k, NN)
                dk_s[cols, :] += _dot(dzs, q, TN)
                dv_s[cols, :] += _dot(a.astype(BF16), do, TN)
                pre_l = pre_l + jnp.sum(l1m, axis=1, keepdims=True)
                pre_g = pre_g + jnp.sum(gg, axis=1, keepdims=True)
                return dq, pre_l, pre_g

            zero = jnp.zeros((tb, 1), F32)
            dq, _, _ = lax.fori_loop(0, i + 1, kblock, (jnp.zeros((tb, HEAD_DIM), F32), zero, zero))
            dq_ref[rows, :] = dq.astype(BF16)
            return 0

        lax.fori_loop(0, nq, qblock, 0)
        dk_ref[...] = dk_s[...].astype(BF16)
        dv_ref[...] = dv_s[...].astype(BF16)
        _fused_end(fused, grid, f_refs)

    return pl.pallas_call(
        body, name="sb_bwd", grid=grid,
        in_specs=[_seg_spec(s, 0), _seg_spec(s, 1), _seg_spec(s, 2), _seg_spec(s, 3),
                  _head_spec(s), _head_spec(s), _seg_spec(s, 0),
                  pl.BlockSpec((1, HEAD_DIM), lambda i, h: (0, h))] + f_in_specs,
        out_specs=[_seg4_spec(s, 0), pl.BlockSpec((None, 1, HEAD_DIM), lambda i, h: (i, 0, h))] + f_out_specs,
        out_shape=[jax.ShapeDtypeStruct((8, b, s, w), BF16), jax.ShapeDtypeStruct((b, 1, w), F32)] + f_out,
        scratch_shapes=[pltpu.VMEM((s, HEAD_DIM), BF16), pltpu.VMEM((s, HEAD_DIM), F32),
                        pltpu.VMEM((s, HEAD_DIM), F32)] + f_scratch,
        compiler_params=_cparams(("arbitrary", "arbitrary")),
    )(proj8, proj8, proj8, proj8, o_sb, tot_sb, dy2, g_sb, *f_in)


DIL_BLOCK = 128


def _dil_chunks(s, r):
    length = s // r
    out = []
    for rho in range(r):
        for cc in range(length // DIL_BLOCK):
            if r == 1:
                nat = pl.ds(cc * DIL_BLOCK, DIL_BLOCK)
            else:
                nat = pl.ds(rho + r * DIL_BLOCK * cc, DIL_BLOCK, stride=r)
            out.append((nat, pl.ds(rho * length + cc * DIL_BLOCK, DIL_BLOCK)))
    return out


def _dil_scores(q, kc, kp, bias_c, bias_p, first):
    n = DIL_BLOCK
    ri = lax.broadcasted_iota(jnp.int32, (n, n), 0)
    ci = lax.broadcasted_iota(jnp.int32, (n, n), 1)
    scale = 1.0 / math.sqrt(HEAD_DIM)
    s_c = jnp.where(ci <= ri, _dot(q, kc, NT) * scale - bias_c, NEG_BIG)
    s_p = jnp.where((ci >= ri) & jnp.logical_not(first), _dot(q, kp, NT) * scale - bias_p, NEG_BIG)
    return s_c, s_p


def _dil_bias(slope, r):
    n = DIL_BLOCK
    ri = lax.broadcasted_iota(jnp.int32, (n, n), 0)
    ci = lax.broadcasted_iota(jnp.int32, (n, n), 1)
    steps = (ri - ci).astype(F32)
    return slope * (steps * r), slope * ((steps + n) * r)


def _dil_check(s):
    for window, r in DIL_PAIRS:
        assert window // r == DIL_BLOCK and s % (r * DIL_BLOCK) == 0, (s, window, r)


def _dil_fwd(proj8, g_dil, slopes, y2):
    _, b, s, w = proj8.shape
    n_heads = w // HEAD_DIM
    _dil_check(s)
    n = DIL_BLOCK
    nt = s // n

    def body(q_ref, k_ref, v_ref, zg_ref, g_ref, sl_ref, y_in, o_ref, lse_ref, y_ref,
             qp, kp, vp, pnum, pm, pl_, acc_s, m_s, l_s):
        del y_in
        slope = sl_ref[...][:, :1]

        for (window, r) in DIL_PAIRS:
            nb = (s // r) // n
            bias_c, bias_p = _dil_bias(slope, float(r))
            for nat, per in _dil_chunks(s, r):
                qp[per, :] = q_ref[nat, :].astype(BF16)
                kp[per, :] = k_ref[nat, :].astype(BF16)
                vp[per, :] = v_ref[nat, :].astype(BF16)
            num_t, m_t, l_t = (acc_s, m_s, l_s) if r == 1 else (pnum, pm, pl_)

            def tile(t, _):
                rows = pl.ds(pl.multiple_of(t * n, n), n)
                prow = pl.ds(pl.multiple_of(jnp.maximum(t - 1, 0) * n, n), n)
                first = lax.rem(t, nb) == 0
                s_c, s_p = _dil_scores(qp[rows, :], kp[rows, :], kp[prow, :], bias_c, bias_p, first)
                m = jnp.maximum(jnp.max(s_c, axis=1, keepdims=True), jnp.max(s_p, axis=1, keepdims=True))
                p_c = jnp.exp(s_c - m)
                p_p = jnp.exp(s_p - m)
                l = jnp.sum(p_c, axis=1, keepdims=True) + jnp.sum(p_p, axis=1, keepdims=True)
                num_t[rows, :] = _dot(p_c.astype(BF16), vp[rows, :], NN) + _dot(p_p.astype(BF16), vp[prow, :], NN)
                m_t[rows, :] = jnp.broadcast_to(m, (n, HEAD_DIM))
                l_t[rows, :] = jnp.broadcast_to(l, (n, HEAD_DIM))
                return 0

            lax.fori_loop(0, nt, tile, 0)
            if r != 1:
                for nat, per in _dil_chunks(s, r):
                    m_old, m_new_p = m_s[nat, :], pm[per, :]
                    m_new = jnp.maximum(m_old, m_new_p)
                    a_old, a_p = jnp.exp(m_old - m_new), jnp.exp(m_new_p - m_new)
                    m_s[nat, :] = m_new
                    l_s[nat, :] = l_s[nat, :] * a_old + pl_[per, :] * a_p
                    acc_s[nat, :] = acc_s[nat, :] * a_old + pnum[per, :] * a_p

        g = g_ref[...]

        def finish(t, _):
            rows = pl.ds(pl.multiple_of(t * n, n), n)
            l = l_s[rows, :]
            o = acc_s[rows, :] / l
            o_ref[rows, :] = o
            lse_ref[rows, :] = m_s[rows, :] + jnp.log(l)
            y_ref[rows, :] = _head_out(o, zg_ref[rows, :], g).astype(BF16)
            return 0

        lax.fori_loop(0, nt, finish, 0)

    f32_s = pltpu.VMEM((s, HEAD_DIM), F32)
    bf_s = pltpu.VMEM((s, HEAD_DIM), BF16)
    return pl.pallas_call(
        body, name="dil_fwd", grid=(b, n_heads),
        in_specs=[_seg_spec(s, 4), _seg_spec(s, 5), _seg_spec(s, 6), _seg_spec(s, 7),
                  pl.BlockSpec((1, HEAD_DIM), lambda i, h: (0, h)),
                  pl.BlockSpec((None, 1, HEAD_DIM), lambda i, h: (h, 0, 0)), ANY],
        out_specs=[_head_spec(s), _head_spec(s), _seg_spec(s, 1)],
        out_shape=[jax.ShapeDtypeStruct((b, s, w), F32), jax.ShapeDtypeStruct((b, s, w), F32),
                   jax.ShapeDtypeStruct((2, b, s, w), BF16)],
        scratch_shapes=[bf_s, bf_s, bf_s, f32_s, f32_s, f32_s, f32_s, f32_s, f32_s],
        input_output_aliases={6: 2},
        compiler_params=_cparams(("parallel", "parallel")),
    )(proj8, proj8, proj8, proj8, g_dil, slopes, y2)


def _dil_bwd(proj8, o_dl, lse_dl, dy2, g_dil, slopes, dproj8):
    _, b, s, w = proj8.shape
    n_heads = w // HEAD_DIM
    _dil_check(s)
    n = DIL_BLOCK
    nt = s // n
    scale = 1.0 / math.sqrt(HEAD_DIM)

    def body(q_ref, k_ref, v_ref, zg_ref, o_ref, lse_ref, dy_ref, g_ref, sl_ref, dp_in, dp_ref, dg_ref,
             do_n, dt_n, dq_n, dk_n, dv_n, qp, kp, vp, dop, dtp, lsep, pdq, pdk, pdv):
        del dp_in
        dq_ref, dk_ref, dv_ref, dzg_ref = (dp_ref.at[i] for i in range(4))
        slope = sl_ref[...][:, :1]
        g = g_ref[...]

        def prologue(t, dg):
            rows = pl.ds(pl.multiple_of(t * n, n), n)
            o = o_ref[rows, :]
            do, dzg, dgi = _head_out_bwd(o, zg_ref[rows, :], g, dy_ref[rows, :])
            dzg_ref[rows, :] = dzg.astype(BF16)
            do_n[rows, :] = do
            dt_n[rows, :] = jnp.broadcast_to(jnp.sum(do * o, axis=-1, keepdims=True), (n, HEAD_DIM))
            return dg + dgi

        dg_ref[...] = lax.fori_loop(0, nt, prologue, jnp.zeros((1, HEAD_DIM), F32))
        dq_n[...] = jnp.zeros_like(dq_n)
        dk_n[...] = jnp.zeros_like(dk_n)
        dv_n[...] = jnp.zeros_like(dv_n)

        for (window, r) in DIL_PAIRS:
            nb = (s // r) // n
            bias_c, bias_p = _dil_bias(slope, float(r))
            for nat, per in _dil_chunks(s, r):
                qp[per, :] = q_ref[nat, :].astype(BF16)
                kp[per, :] = k_ref[nat, :].astype(BF16)
                vp[per, :] = v_ref[nat, :].astype(BF16)
                dop[per, :] = do_n[nat, :].astype(BF16)
                dtp[per, :] = dt_n[nat, :]
                lsep[per, :] = lse_ref[nat, :]
            pdk[...] = jnp.zeros_like(pdk)
            pdv[...] = jnp.zeros_like(pdv)

            def tile(t, _):
                rows = pl.ds(pl.multiple_of(t * n, n), n)
                prow = pl.ds(pl.multiple_of(jnp.maximum(t - 1, 0) * n, n), n)
                first = lax.rem(t, nb) == 0
                q, kc, kpv, do = qp[rows, :], kp[rows, :], kp[prow, :], dop[rows, :]
                s_c, s_p = _dil_scores(q, kc, kpv, bias_c, bias_p, first)
                lse = lsep[rows, :][:, :1]
                dt = dtp[rows, :][:, :1]
                p_c = jnp.exp(s_c - lse)
                p_p = jnp.exp(s_p - lse)
                ds_c = ((p_c * (_dot(do, vp[rows, :], NT) - dt)) * scale).astype(BF16)
                ds_p = ((p_p * (_dot(do, vp[prow, :], NT) - dt)) * scale).astype(BF16)
                pdq[rows, :] = _dot(ds_c, kc, NN) + _dot(ds_p, kpv, NN)
                pdk[rows, :] += _dot(ds_c, q, TN)
                pdk[prow, :] += _dot(ds_p, q, TN)
                pdv[rows, :] += _dot(p_c.astype(BF16), do, TN)
                pdv[prow, :] += _dot(p_p.astype(BF16), do, TN)
                return 0

            lax.fori_loop(0, nt, tile, 0)
            for nat, per in _dil_chunks(s, r):
                dq_n[nat, :] += pdq[per, :]
                dk_n[nat, :] += pdk[per, :]
                dv_n[nat, :] += pdv[per, :]

        dq_ref[...] = dq_n[...].astype(BF16)
        dk_ref[...] = dk_n[...].astype(BF16)
        dv_ref[...] = dv_n[...].astype(BF16)

    f32_s = pltpu.VMEM((s, HEAD_DIM), F32)
    bf_s = pltpu.VMEM((s, HEAD_DIM), BF16)
    return pl.pallas_call(
        body, name="dil_bwd", grid=(b, n_heads),
        in_specs=[_seg_spec(s, 4), _seg_spec(s, 5), _seg_spec(s, 6), _seg_spec(s, 7),
                  _head_spec(s), _head_spec(s), _seg_spec(s, 1),
                  pl.BlockSpec((1, HEAD_DIM), lambda i, h: (0, h)),
                  pl.BlockSpec((None, 1, HEAD_DIM), lambda i, h: (h, 0, 0)), ANY],
        out_specs=[_seg4_spec(s, 1), pl.BlockSpec((None, 1, HEAD_DIM), lambda i, h: (i, 0, h))],
        out_shape=[jax.ShapeDtypeStruct((8, b, s, w), BF16), jax.ShapeDtypeStruct((b, 1, w), F32)],
        scratch_shapes=[f32_s] * 5 + [bf_s] * 4 + [f32_s] * 5,
        input_output_aliases={9: 0},
        compiler_params=_cparams(("parallel", "parallel")),
    )(proj8, proj8, proj8, proj8, o_dl, lse_dl, dy2, g_dil, slopes, dproj8)


def _small_update(gathered, n_b, params, m, v):
    n_dev, _, width = gathered.shape

    def body(g_ref, p_ref, m_ref, v_ref, grad_ref, d_ref, nm_ref, nv_ref, loss_ref):
        for row in range(2):
            acc = None
            for dev in range(n_dev):
                for i in range(n_b):
                    term = g_ref[dev, pl.ds(row * n_b + i, 1), :]
                    acc = term if acc is None else acc + term
            grad_ref[pl.ds(row, 1), :] = acc
        loss = g_ref[0, pl.ds(2 * n_b, 1), pl.ds(0, 128)]
        for dev in range(1, n_dev):
            loss = loss + g_ref[dev, pl.ds(2 * n_b, 1), pl.ds(0, 128)]
        loss_ref[...] = loss
        d, nm, nv = _adamw_math(p_ref[...], grad_ref[...], m_ref[...], v_ref[...])
        d_ref[...] = d
        nm_ref[...] = nm
        nv_ref[...] = nv

    sds = jax.ShapeDtypeStruct((2, width), F32)
    return pl.pallas_call(
        body, name="small_update",
        in_specs=[VMEM_SPEC] * 4, out_specs=[VMEM_SPEC] * 5,
        out_shape=[sds, sds, sds, sds, jax.ShapeDtypeStruct((1, 128), F32)],
        compiler_params=_cparams(),
    )(gathered, params, m, v)


def _wada_update(c_t, dmod, w, m, v):
    d, bt = c_t.shape
    _, n = dmod.shape
    tr, tc = _tile(d, 512), _tile(n, 1024)

    def body(c_ref, dm_ref, w_ref, m_ref, v_ref, g_ref, d_ref, nm_ref, nv_ref):
        cv = c_ref[...]
        cs = (cv * _sigmoid(cv)).astype(BF16)
        grad = _dot(cs, dm_ref[...].astype(BF16), NN)
        g_ref[...] = grad
        dl, nm, nv = _adamw_math(w_ref[...], grad, m_ref[...], v_ref[...])
        d_ref[...] = dl
        nm_ref[...] = nm
        nv_ref[...] = nv

    spec = pl.BlockSpec((tr, tc), lambda i, j: (i, j))
    sds = jax.ShapeDtypeStruct((d, n), F32)
    return pl.pallas_call(
        body, name="wada_update", grid=(d // tr, n // tc),
        in_specs=[pl.BlockSpec((tr, bt), lambda i, j: (i, 0)), pl.BlockSpec((bt, tc), lambda i, j: (0, j)),
                  spec, spec, spec],
        out_specs=[spec] * 4, out_shape=[sds] * 4,
        compiler_params=_cparams(("parallel", "parallel")),
    )(c_t, dmod, w, m, v)


def _reduce_begin(gs, tag):
    ra = _sibling_half_swap(gs, "swap_" + tag)
    pa, own = _pair_sum(gs, ra, "pair_sum_" + tag)
    return _ScatterChips(pa), own


def _reduce_finish(rb, own, w, m, v, tag):
    half = _chip_sum(rb, own, "chip_sum_" + tag)
    grad = _sibling_join(half, "join_" + tag)
    delta, nm, nv = _adamw(w, grad, m, v, "adamw_" + tag)
    return grad, delta, nm, nv


def kernel(x, c, w_ada, b_ada, g_norm, w_in, g_sb, g_dil, w_out, g_final, loss_target, m_w_ada, m_b_ada, m_g_norm, m_w_in, m_g_sb, m_g_dil, m_w_out, m_g_final, v_w_ada, v_b_ada, v_g_norm, v_w_in, v_g_sb, v_g_dil, v_w_out, v_g_final):
    nb, s, d = x.shape
    t = nb * s
    na = w_ada.shape[2]
    cs = w_in.shape[2]
    w = cs // 2
    n_heads = w // HEAD_DIM
    r_out = w_out.shape[1]
    assert 2 * nb + 1 <= 8 and 2 * d + 2 * w <= 3 * d and N_CHIPS * na == 3 * d and N_CHIPS * r_out == 2 * w
    xi, yi, ci = _place()
    chip = 2 * xi + yi
    dev = 2 * chip + ci

    c_all = _allgather8(jnp.pad(c, ((0, 8 - nb), (0, 0))), "gather_c")
    c16 = c_all.reshape(N_DEV, 8, d)[:, :nb].reshape(N_DEV * nb, d)
    b_ada_shard = lax.dynamic_slice(b_ada, (0, chip * na), (1, na))
    mod_part = _mod_fwd(c16, w_ada[0], b_ada_shard)
    mod_all = _allgather8(mod_part, "gather_mod")
    mod_full = mod_all.reshape(N_CHIPS, 2, N_DEV * nb, na)[:, 0].transpose(1, 0, 2).reshape(N_DEV * nb, 3 * d)
    mod = lax.dynamic_slice(mod_full, (dev * nb, 0), (nb, 3 * d))
    shift, scale, gate = (mod[:, i * d:(i + 1) * d].reshape(nb, 1, d) for i in range(3))

    h = _norm_mod_fwd(x, g_norm, scale, shift)
    h2 = h.reshape(t, d)
    ws_in = _allgather_chips(_cast_bf16(w_in[0], "cast_w_in"), "gather_w_in")
    ws_out = _allgather_chips(_cast_bf16(w_out[0], "cast_w_out"), "gather_w_out")
    w_out_full = ws_out.reshape(2 * w, d)

    proj8 = _proj_fwd(h2, ws_in).reshape(8, nb, s, w)
    slopes = jnp.exp2(-ALIBI_MAX_BIAS * jnp.arange(1, n_heads + 1, dtype=F32) / n_heads)
    slopes = jnp.broadcast_to(slopes[:, None, None], (n_heads, 1, HEAD_DIM))
    o_sb, tot_sb, y2 = _sb_fwd(proj8, g_sb)
    o_dl, lse_dl, y2 = _dil_fwd(proj8, g_dil, slopes, y2)
    y2f = y2.reshape(2, t, w)
    out = _out_fwd(y2f, w_out_full)

    dx1, dout, dgate, dg_final, loss_part = _loss_head(
        x, out.reshape(nb, s, d), gate, g_final.reshape(1, d), loss_target)
    dout2 = dout.reshape(t, d)
    gs_out = _out_bwd_w(y2f, dout2).reshape(N_CHIPS, r_out, d)
    scatter_out, own_out = _reduce_begin(gs_out, "w_out")
    dy2 = _out_bwd_y(dout2, w_out_full).reshape(2, nb, s, w)
    dproj8, dg_sb, rb_out = _sb_bwd(proj8, o_sb, tot_sb, dy2, g_sb, fused=scatter_out)
    dproj8, dg_dl = _dil_bwd(proj8, o_dl, lse_dl, dy2, g_dil, slopes, dproj8)
    dproj8 = dproj8.reshape(8, t, w)
    gs_in = _proj_bwd_w(h2, dproj8)
    scatter_in, own_in = _reduce_begin(gs_in, "w_in")
    dh, rb_in = _proj_bwd_x(dproj8, ws_in, fused=scatter_in)
    grad_x, dshift, dscale, dg_norm = _norm_mod_bwd(x, dh.reshape(nb, s, d), dx1, g_norm, scale)

    width = 3 * d
    dmod = jnp.concatenate([dshift, dscale, dgate], axis=-1).reshape(nb, width)
    gains = jnp.concatenate([dg_sb.reshape(nb, w), dg_dl.reshape(nb, w)], axis=-1)
    gains = jnp.pad(gains, ((0, 0), (2 * d, width - 2 * d - 2 * w)))
    first = jnp.pad(jnp.concatenate([dg_norm, dg_final], axis=-1), ((0, nb - 1), (0, width - 2 * d)))
    loss_row = jnp.pad(loss_part, ((0, 0), (0, width - 128)))
    pack = jnp.concatenate([dmod, gains + first, loss_row, jnp.zeros((8 - 2 * nb - 1, width), F32)], axis=0)
    gathered = _allgather8(pack, "gather_small").reshape(N_DEV, 8, width)

    def stack(bias, gn, gf, gsb, gdl):
        row1 = jnp.concatenate([gn.reshape(1, d), gf.reshape(1, d), gsb.reshape(1, w), gdl.reshape(1, w)], axis=-1)
        return jnp.concatenate([bias.reshape(1, width), jnp.pad(row1, ((0, 0), (0, width - 2 * d - 2 * w)))], axis=0)

    small = _small_update(
        gathered, nb, stack(b_ada, g_norm, g_final, g_sb, g_dil),
        stack(m_b_ada, m_g_norm, m_g_final, m_g_sb, m_g_dil), stack(v_b_ada, v_g_norm, v_g_final, v_g_sb, v_g_dil))
    loss = small[4][0, 0]

    def unstack(a):
        return (a[0:1, :], a[1:2, 0:d], a[1, d:2 * d], a[1:2, 2 * d:2 * d + w], a[1:2, 2 * d + w:2 * d + 2 * w])

    (g_b, g_gn, g_gf, g_gsb, g_gdl), (d_b, d_gn, d_gf, d_gsb, d_gdl), (nm_b, nm_gn, nm_gf, nm_gsb, nm_gdl), \
        (nv_b, nv_gn, nv_gf, nv_gsb, nv_gdl) = (unstack(a) for a in small[:4])

    dmod_all = gathered[:, :nb].reshape(N_DEV * nb, width)
    dmod_cols = lax.dynamic_slice(dmod_all, (0, chip * na), (N_DEV * nb, na))
    g_wa, d_wa, nm_wa, nv_wa = _wada_update(c16.T, dmod_cols, w_ada[0], m_w_ada[0], v_w_ada[0])

    g_wi, d_wi, nm_wi, nv_wi = _reduce_finish(rb_in, own_in, w_in[0], m_w_in[0], v_w_in[0], "w_in")
    g_wo, d_wo, nm_wo, nv_wo = _reduce_finish(rb_out, own_out, w_out[0], m_w_out[0], v_w_out[0], "w_out")

    lead = lambda a: a[None]
    return (loss, grad_x,
            lead(g_wa), g_b, g_gn, lead(g_wi), g_gsb, g_gdl, lead(g_wo), g_gf,
            lead(d_wa), d_b, d_gn, lead(d_wi), d_gsb, d_gdl, lead(d_wo), d_gf,
            lead(nm_wa), nm_b, nm_gn, lead(nm_wi), nm_gsb, nm_gdl, lead(nm_wo), nm_gf,
            lead(nv_wa), nv_b, nv_gn, lead(nv_wi), nv_gsb, nv_gdl, lead(nv_wo), nv_gf)
```

```python
import functools
import math

import jax
import jax.numpy as jnp
from jax import lax
from jax.experimental import pallas as pl
from jax.experimental.pallas import tpu as pltpu

F32 = jnp.float32
BF16 = jnp.bfloat16
MESH = pl.DeviceIdType.MESH

HEAD_DIM = 128
EPS = 1e-6
DIL_PAIRS = ((128, 1), (512, 4), (2048, 16))
ALIBI_MAX_BIAS = 8.0
ADAM_LR = 0.001
ADAM_B1 = 0.9
ADAM_B2 = 0.999
ADAM_EPS = 1e-08
ADAM_WD = 0.01
ADAM_STEP = 10
N_CHIPS = 4
N_DEV = 8
VMEM_LIMIT_BYTES = 56 * 1024 * 1024
NEG_BIG = -1e30

NN = (((1,), (0,)), ((), ()))
NT = (((1,), (1,)), ((), ()))
TN = (((0,), (0,)), ((), ()))

ANY = pl.BlockSpec(memory_space=pl.ANY)
VMEM_SPEC = pl.BlockSpec(memory_space=pltpu.VMEM)


def _cparams(sem=None):
    return pltpu.CompilerParams(dimension_semantics=sem, vmem_limit_bytes=VMEM_LIMIT_BYTES)


def _tile(dim, pref):
    t = min(dim, pref)
    assert dim % t == 0, (dim, pref)
    return t


def _dot(a, b, dims):
    return lax.dot_general(a, b, dims, preferred_element_type=F32)


def _sigmoid(x):
    return 1.0 / (1.0 + jnp.exp(-x))


def _place():
    return lax.axis_index("x"), lax.axis_index("y"), lax.axis_index("c")


def _allgather8(x_shard, name):
    m_per, n = x_shard.shape

    def body(x_ref, out_ref, send_sems, recv_sems, local_sem):
        x, y, c = _place()
        me, sibling = (x, y, c), (x, y, 1 - c)
        chips = [(1 - x, y), (x, 1 - y), (1 - x, 1 - y)]

        def rows(px, py, pc):
            return out_ref.at[pl.ds((4 * px + 2 * py + pc) * m_per, m_per), :]

        def copy(k, block, to, src=None):
            return pltpu.make_async_remote_copy(
                src_ref=rows(*block) if src is None else src, dst_ref=rows(*block),
                send_sem=send_sems.at[k], recv_sem=recv_sems.at[k], device_id=to, device_id_type=MESH)

        mine = pltpu.make_async_copy(x_ref, rows(*me), local_sem)
        mine.start()
        first = [copy(0, me, sibling, src=x_ref)]
        first += [copy(1 + j, me, (*chip, c), src=x_ref) for j, chip in enumerate(chips)]
        for cp in first:
            cp.start()
        passed = [copy(4 + j, (*chip, c), sibling) for j, chip in enumerate(chips)]
        for j, chip in enumerate(chips):
            copy(1 + j, (*chip, c), me).wait_recv()
            passed[j].start()
        copy(0, sibling, me).wait_recv()
        for j, chip in enumerate(chips):
            copy(4 + j, (*chip, 1 - c), me).wait_recv()
        for cp in first + passed:
            cp.wait_send()
        mine.wait()

    return pl.pallas_call(
        body, name=name,
        out_shape=jax.ShapeDtypeStruct((N_DEV * m_per, n), x_shard.dtype),
        in_specs=[VMEM_SPEC], out_specs=VMEM_SPEC,
        scratch_shapes=[pltpu.SemaphoreType.DMA((7,)), pltpu.SemaphoreType.DMA((7,)), pltpu.SemaphoreType.DMA],
    )(x_shard)


def _allgather_chips(ws, name):
    _, r, cdim = ws.shape
    half = r // 2

    def body(in_ref, out_ref, send_sems, recv_sems):
        del in_ref
        x, y, c = _place()
        sibling = (x, y, 1 - c)
        chips = [(1 - x, y), (x, 1 - y), (1 - x, 1 - y)]

        def copy(k, px, py, pc, to):
            rows = out_ref.at[2 * px + py, pl.ds(pc * half, half), :]
            return pltpu.make_async_remote_copy(
                src_ref=rows, dst_ref=rows, send_sem=send_sems.at[k], recv_sem=recv_sems.at[k],
                device_id=to, device_id_type=MESH)

        first = [copy(j, x, y, c, (*chip, c)) for j, chip in enumerate(chips)]
        for cp in first:
            cp.start()
        passed = [copy(3 + j, *chip, c, sibling) for j, chip in enumerate(chips)]
        for j, chip in enumerate(chips):
            copy(j, *chip, c, (x, y, c)).wait_recv()
            passed[j].start()
        for j, chip in enumerate(chips):
            copy(3 + j, *chip, 1 - c, (x, y, c)).wait_recv()
        for cp in first + passed:
            cp.wait_send()

    return pl.pallas_call(
        body, name=name,
        out_shape=jax.ShapeDtypeStruct(ws.shape, ws.dtype),
        in_specs=[ANY], out_specs=ANY, input_output_aliases={0: 0},
        scratch_shapes=[pltpu.SemaphoreType.DMA((6,)), pltpu.SemaphoreType.DMA((6,))],
    )(ws)


def _sibling_half_swap(gs, name):
    n, r, cdim = gs.shape
    half = r // 2

    def body(g_ref, out_ref, send_sem, recv_sem):
        x, y, c = _place()
        cp = pltpu.make_async_remote_copy(
            src_ref=g_ref.at[:, pl.ds((1 - c) * half, half), :], dst_ref=out_ref,
            send_sem=send_sem, recv_sem=recv_sem, device_id=(x, y, 1 - c), device_id_type=MESH)
        cp.start()
        cp.wait()

    return pl.pallas_call(
        body, name=name,
        out_shape=jax.ShapeDtypeStruct((n, half, cdim), gs.dtype),
        in_specs=[ANY], out_specs=ANY,
        scratch_shapes=[pltpu.SemaphoreType.DMA, pltpu.SemaphoreType.DMA],
    )(gs)


class _ScatterChips:
    def __init__(self, pa):
        self.inputs = [pa]
        self.out_shapes = [jax.ShapeDtypeStruct(pa.shape, pa.dtype)]
        self.scratch = [pltpu.SemaphoreType.DMA((3,)), pltpu.SemaphoreType.DMA((3,)), pltpu.SemaphoreType.DMA]

    @staticmethod
    def _mine(p_ref, out_ref, send_sems, recv_sems, local_sem):
        x, y, _ = _place()
        return pltpu.make_async_copy(p_ref.at[2 * x + y], out_ref.at[2 * x + y], local_sem)

    @staticmethod
    def _remote(p_ref, out_ref, send_sems, recv_sems, local_sem, incoming):
        x, y, c = _place()
        me = 2 * x + y
        remote = []
        for j, (px, py) in enumerate([(1 - x, y), (x, 1 - y), (1 - x, 1 - y)]):
            remote.append(pltpu.make_async_remote_copy(
                src_ref=p_ref.at[me if incoming else 2 * px + py], dst_ref=out_ref.at[2 * px + py if incoming else me],
                send_sem=send_sems.at[j], recv_sem=recv_sems.at[j], device_id=(px, py, c), device_id_type=MESH))
        return remote

    def start(self, *refs):
        self._mine(*refs).start()
        for cp in self._remote(*refs, incoming=False):
            cp.start()

    def wait(self, *refs):
        for cp in self._remote(*refs, incoming=True):
            cp.wait_recv()
        for cp in self._remote(*refs, incoming=False):
            cp.wait_send()
        self._mine(*refs).wait()


def _fused_specs(fused):
    if fused is None:
        return [], [], [], [], []
    return (list(fused.inputs), [ANY] * len(fused.inputs), list(fused.out_shapes), [ANY] * len(fused.out_shapes),
            list(fused.scratch))


def _fused_begin(fused, grid, refs):
    if fused is not None:
        first = functools.reduce(lambda p, q: p & q, [pl.program_id(i) == 0 for i in range(len(grid))])
        pl.when(first)(lambda: fused.start(*refs))


def _fused_end(fused, grid, refs):
    if fused is not None:
        last = functools.reduce(lambda p, q: p & q, [pl.program_id(i) == g - 1 for i, g in enumerate(grid)])
        pl.when(last)(lambda: fused.wait(*refs))


def _sibling_join(full, name):
    h2, cdim = full.shape
    h = h2 // 2

    def body(in_ref, out_ref, send_sem, recv_sem):
        del in_ref
        x, y, c = _place()
        mine = out_ref.at[pl.ds(c * h, h), :]
        cp = pltpu.make_async_remote_copy(
            src_ref=mine, dst_ref=mine, send_sem=send_sem, recv_sem=recv_sem,
            device_id=(x, y, 1 - c), device_id_type=MESH)
        cp.start()
        theirs = out_ref.at[pl.ds((1 - c) * h, h), :]
        pltpu.make_async_remote_copy(
            src_ref=theirs, dst_ref=theirs, send_sem=send_sem, recv_sem=recv_sem,
            device_id=(x, y, 1 - c), device_id_type=MESH).wait_recv()
        cp.wait_send()

    return pl.pallas_call(
        body, name=name,
        out_shape=jax.ShapeDtypeStruct(full.shape, full.dtype),
        in_specs=[ANY], out_specs=ANY, input_output_aliases={0: 0},
        scratch_shapes=[pltpu.SemaphoreType.DMA, pltpu.SemaphoreType.DMA],
    )(full)


def _cast_bf16_slab(w, name):
    r, cdim = w.shape
    tr, tc = _tile(r, 512), _tile(cdim, 2048)

    def body(pc_ref, w_ref, o_ref):
        o_ref[...] = w_ref[...].astype(BF16)

    return pl.pallas_call(
        body, name=name,
        grid_spec=pltpu.PrefetchScalarGridSpec(
            num_scalar_prefetch=1, grid=(r // tr, cdim // tc),
            in_specs=[pl.BlockSpec((tr, tc), lambda i, j, pc: (i, j))],
            out_specs=pl.BlockSpec((None, tr, tc), lambda i, j, pc: (pc[1], i, j))),
        out_shape=jax.ShapeDtypeStruct((N_CHIPS, r, cdim), BF16),
        compiler_params=_cparams(("parallel", "parallel")),
    )(_place_scalars(), w)


def _place_scalars():
    x, y, c = _place()
    return jnp.stack([c, 2 * x + y]).astype(jnp.int32)


def _pair_sum(gs, ra, name):
    n, r, cdim = gs.shape
    half = r // 2
    tr, tc = _tile(half, 512), _tile(cdim, 2048)
    nt = half // tr

    def body(pc_ref, g_ref, r_ref, o_ref, own_ref):
        val = g_ref[...] + r_ref[...]
        o_ref[...] = val.astype(BF16)

        @pl.when(pl.program_id(2) == pc_ref[1])
        def _():
            own_ref[...] = val

    return pl.pallas_call(
        body, name=name,
        grid_spec=pltpu.PrefetchScalarGridSpec(
            num_scalar_prefetch=1, grid=(nt, cdim // tc, n),
            in_specs=[pl.BlockSpec((None, tr, tc), lambda i, j, s, pc: (s, pc[0] * nt + i, j)),
                      pl.BlockSpec((None, tr, tc), lambda i, j, s, pc: (s, i, j))],
            out_specs=[pl.BlockSpec((None, tr, tc), lambda i, j, s, pc: (s, i, j)),
                       pl.BlockSpec((tr, tc), lambda i, j, s, pc: (i, j))]),
        out_shape=[jax.ShapeDtypeStruct((n, half, cdim), BF16), jax.ShapeDtypeStruct((half, cdim), F32)],
        compiler_params=_cparams(("parallel", "parallel", "arbitrary")),
    )(_place_scalars(), gs, ra)


def _chip_sum(rb, own, name):
    n, h, cdim = rb.shape
    tr, tc = _tile(h, 256), _tile(cdim, 2048)
    nt = h // tr

    def body(pc_ref, r_ref, own_ref, o_ref):
        chip = pc_ref[1]
        acc = None
        for p in range(n):
            term = jnp.where(chip == p, own_ref[...], r_ref[p].astype(F32))
            acc = term if acc is None else acc + term
        o_ref[...] = acc

    return pl.pallas_call(
        body, name=name,
        grid_spec=pltpu.PrefetchScalarGridSpec(
            num_scalar_prefetch=1, grid=(nt, cdim // tc),
            in_specs=[pl.BlockSpec((n, tr, tc), lambda i, j, pc: (0, i, j)),
                      pl.BlockSpec((tr, tc), lambda i, j, pc: (i, j))],
            out_specs=pl.BlockSpec((tr, tc), lambda i, j, pc: (pc[0] * nt + i, j))),
        out_shape=jax.ShapeDtypeStruct((2 * h, cdim), F32),
        compiler_params=_cparams(("parallel", "parallel")),
    )(_place_scalars(), rb, own)


def _adamw_math(w, g, m, v):
    m = ADAM_B1 * m + (1.0 - ADAM_B1) * g
    v = ADAM_B2 * v + (1.0 - ADAM_B2) * (g * g)
    m_hat = m / (1.0 - ADAM_B1 ** ADAM_STEP)
    v_hat = v / (1.0 - ADAM_B2 ** ADAM_STEP)
    delta = -ADAM_LR * (m_hat / (jnp.sqrt(v_hat) + ADAM_EPS) + ADAM_WD * w)
    return delta, m, v


def _adamw(w, g, m, v, name):
    r, cdim = w.shape
    tr, tc = _tile(r, 256), _tile(cdim, 2048)

    def body(w_ref, g_ref, m_ref, v_ref, d_ref, nm_ref, nv_ref):
        d, nm, nv = _adamw_math(w_ref[...], g_ref[...], m_ref[...], v_ref[...])
        d_ref[...] = d
        nm_ref[...] = nm
        nv_ref[...] = nv

    spec = pl.BlockSpec((tr, tc), lambda i, j: (i, j))
    sds = jax.ShapeDtypeStruct((r, cdim), F32)
    return pl.pallas_call(
        body, name=name, grid=(r // tr, cdim // tc),
        in_specs=[spec] * 4, out_specs=[spec] * 3, out_shape=[sds] * 3,
        compiler_params=_cparams(("parallel", "parallel")),
    )(w, g, m, v)


def _matmul(a, b, *, grid, a_spec, b_spec, out_spec, out_shape, acc_shape, dims, name, bias=None, bias_spec=None,
            silu_a=False, fused=None):
    nk = grid[2]
    f_in, f_in_specs, f_out, f_out_specs, f_scratch = _fused_specs(fused)
    n_in = 2 + (bias is not None)

    def body(*refs):
        a_ref, b_ref = refs[:2]
        bias_ref = refs[2] if bias is not None else None
        o_ref = refs[n_in + len(f_in)]
        acc_ref = refs[n_in + len(f_in) + 1 + len(f_out)]
        f_refs = (*refs[n_in:n_in + len(f_in)], *refs[n_in + len(f_in) + 1:n_in + len(f_in) + 1 + len(f_out)],
                  *refs[n_in + len(f_in) + 2 + len(f_out):])
        _fused_begin(fused, grid, f_refs)
        k = pl.program_id(2)

        @pl.when(k == 0)
        def _():
            acc_ref[...] = jnp.zeros_like(acc_ref)

        av = a_ref[...]
        if silu_a:
            av = av * _sigmoid(av)
        acc_ref[...] += _dot(av.astype(BF16), b_ref[...].astype(BF16), dims)

        @pl.when(k == nk - 1)
        def _():
            res = acc_ref[...]
            if bias is not None:
                res = res + bias_ref[...]
            o_ref[...] = res.astype(o_ref.dtype)

        _fused_end(fused, grid, f_refs)

    in_specs = [a_spec, b_spec] + ([] if bias is None else [bias_spec]) + f_in_specs
    args = (a, b) + (() if bias is None else (bias,)) + tuple(f_in)
    sem = ("parallel", "parallel", "arbitrary") if fused is None else ("arbitrary",) * 3
    res = pl.pallas_call(
        body, name=name, grid=grid, in_specs=in_specs, out_specs=[out_spec] + f_out_specs,
        out_shape=[out_shape] + f_out,
        scratch_shapes=[pltpu.VMEM(acc_shape, F32)] + f_scratch,
        compiler_params=_cparams(sem),
    )(*args)
    return res[0] if fused is None else tuple(res)


def _mm_tiles(m, n, k):
    return _tile(m, 1024), _tile(n, 1024), _tile(k, 512)


def _proj_fwd(h2, ws_in):
    t, d = h2.shape
    _, _, cs = ws_in.shape
    w = cs // 2
    tm, tn, tk = _mm_tiles(t, w, d)
    nps, npseg = cs // tn, w // tn
    return _matmul(
        h2, ws_in, grid=(t // tm, 8 * npseg, d // tk), dims=NN, name="proj_fwd",
        a_spec=pl.BlockSpec((tm, tk), lambda m, n, k: (m, k)),
        b_spec=pl.BlockSpec((None, tk, tn), lambda m, n, k: (n // nps, k, n % nps)),
        out_spec=pl.BlockSpec((None, tm, tn), lambda m, n, k: (n // npseg, m, n % npseg)),
        out_shape=jax.ShapeDtypeStruct((8, t, w), F32), acc_shape=(tm, tn))


def _proj_bwd_x(dproj8, ws_in, fused=None):
    _, t, w = dproj8.shape
    _, d, cs = ws_in.shape
    tm, tn, tk = _mm_tiles(t, d, w)
    kps, kpseg = cs // tk, w // tk
    return _matmul(
        dproj8, ws_in, grid=(t // tm, d // tn, 8 * kpseg), dims=NT, name="proj_bwd_x", fused=fused,
        a_spec=pl.BlockSpec((None, tm, tk), lambda m, n, k: (k // kpseg, m, k % kpseg)),
        b_spec=pl.BlockSpec((None, tn, tk), lambda m, n, k: (k // kps, n, k % kps)),
        out_spec=pl.BlockSpec((tm, tn), lambda m, n, k: (m, n)),
        out_shape=jax.ShapeDtypeStruct((t, d), F32), acc_shape=(tm, tn))


def _proj_bwd_w(h2, dproj8):
    t, d = h2.shape
    _, _, w = dproj8.shape
    cs = 2 * w
    tm, tn, tk = _mm_tiles(d, w, t)
    nps, npseg = cs // tn, w // tn
    return _matmul(
        h2, dproj8, grid=(d // tm, 8 * npseg, t // tk), dims=TN, name="proj_bwd_w",
        a_spec=pl.BlockSpec((tk, tm), lambda m, n, k: (k, m)),
        b_spec=pl.BlockSpec((None, tk, tn), lambda m, n, k: (n // npseg, k, n % npseg)),
        out_spec=pl.BlockSpec((None, tm, tn), lambda m, n, k: (n // nps, m, n % nps)),
        out_shape=jax.ShapeDtypeStruct((N_CHIPS, d, cs), F32), acc_shape=(tm, tn))


def _out_fwd(y2, w_out):
    _, t, w = y2.shape
    _, d = w_out.shape
    tm, tn, tk = _mm_tiles(t, d, w)
    kpg = w // tk
    return _matmul(
        y2, w_out, grid=(t // tm, d // tn, 2 * kpg), dims=NN, name="out_fwd",
        a_spec=pl.BlockSpec((None, tm, tk), lambda m, n, k: (k // kpg, m, k % kpg)),
        b_spec=pl.BlockSpec((tk, tn), lambda m, n, k: (k, n)),
        out_spec=pl.BlockSpec((tm, tn), lambda m, n, k: (m, n)),
        out_shape=jax.ShapeDtypeStruct((t, d), F32), acc_shape=(tm, tn))


def _out_bwd_y(dout, w_out):
    t, d = dout.shape
    w = w_out.shape[0] // 2
    tm, tn, tk = _mm_tiles(t, w, d)
    npg = w // tn
    return _matmul(
        dout, w_out, grid=(t // tm, 2 * npg, d // tk), dims=NT, name="out_bwd_y",
        a_spec=pl.BlockSpec((tm, tk), lambda m, n, k: (m, k)),
        b_spec=pl.BlockSpec((tn, tk), lambda m, n, k: (n, k)),
        out_spec=pl.BlockSpec((None, tm, tn), lambda m, n, k: (n // npg, m, n % npg)),
        out_shape=jax.ShapeDtypeStruct((2, t, w), F32), acc_shape=(tm, tn))


def _out_bwd_w(y2, dout):
    _, t, w = y2.shape
    _, d = dout.shape
    tm, tn, tk = _mm_tiles(w, d, t)
    mpg = w // tm
    return _matmul(
        y2, dout, grid=(2 * mpg, d // tn, t // tk), dims=TN, name="out_bwd_w",
        a_spec=pl.BlockSpec((None, tk, tm), lambda m, n, k: (m // mpg, k, m % mpg)),
        b_spec=pl.BlockSpec((tk, tn), lambda m, n, k: (k, n)),
        out_spec=pl.BlockSpec((tm, tn), lambda m, n, k: (m, n)),
        out_shape=jax.ShapeDtypeStruct((2 * w, d), F32), acc_shape=(tm, tn))


def _mod_fwd(c_all, w_ada, b_ada):
    bt, d = c_all.shape
    _, n = w_ada.shape
    tn, tk = _tile(n, 512), _tile(d, 1024)
    return _matmul(
        c_all, w_ada, grid=(1, n // tn, d // tk), dims=NN, name="mod_fwd", silu_a=True,
        a_spec=pl.BlockSpec((bt, tk), lambda i, j, l: (0, l)),
        b_spec=pl.BlockSpec((tk, tn), lambda i, j, l: (l, j)),
        bias=b_ada, bias_spec=pl.BlockSpec((1, tn), lambda i, j, l: (0, j)),
        out_spec=pl.BlockSpec((bt, tn), lambda i, j, l: (0, j)),
        out_shape=jax.ShapeDtypeStruct((bt, n), F32), acc_shape=(bt, tn))


def _norm_mod_fwd(x, g_norm, scale, shift):
    b, s, d = x.shape
    ts = _tile(s, 256)

    def body(x_ref, g_ref, sc_ref, sh_ref, h_ref):
        xv = x_ref[...]
        r = lax.rsqrt(jnp.mean(xv * xv, axis=-1, keepdims=True) + EPS)
        y = (xv * r) * g_ref[...]
        h_ref[...] = (y * (1.0 + sc_ref[...]) + sh_ref[...]).astype(BF16)

    row = pl.BlockSpec((None, ts, d), lambda i, j: (i, j, 0))
    per_b = pl.BlockSpec((None, 1, d), lambda i, j: (i, 0, 0))
    return pl.pallas_call(
        body, name="norm_mod_fwd", grid=(b, s // ts),
        in_specs=[row, pl.BlockSpec((1, d), lambda i, j: (0, 0)), per_b, per_b],
        out_specs=row, out_shape=jax.ShapeDtypeStruct((b, s, d), BF16),
        compiler_params=_cparams(("parallel", "parallel")),
    )(x, g_norm, scale, shift)


def _norm_mod_bwd(x, dh, dx1, g_norm, scale):
    b, s, d = x.shape
    ts = _tile(s, 256)

    def body(x_ref, dh_ref, dx1_ref, g_ref, sc_ref, gx_ref, dsh_ref, dsc_ref, dg_ref):
        i, j = pl.program_id(0), pl.program_id(1)

        @pl.when(j == 0)
        def _():
            dsh_ref[...] = jnp.zeros_like(dsh_ref)
            dsc_ref[...] = jnp.zeros_like(dsc_ref)

        @pl.when((i == 0) & (j == 0))
        def _():
            dg_ref[...] = jnp.zeros_like(dg_ref)

        xv, dhv, g = x_ref[...], dh_ref[...], g_ref[...]
        r = lax.rsqrt(jnp.mean(xv * xv, axis=-1, keepdims=True) + EPS)
        xh = xv * r
        dsh_ref[...] += jnp.sum(dhv, axis=0, keepdims=True)
        dsc_ref[...] += jnp.sum(dhv * (xh * g), axis=0, keepdims=True)
        dn = dhv * (1.0 + sc_ref[...])
        dg_ref[...] += jnp.sum(dn * xh, axis=0, keepdims=True)
        u = dn * g
        dx = r * u - xv * (r * r * r) * jnp.mean(u * xv, axis=-1, keepdims=True)
        gx_ref[...] = dx1_ref[...] + dx

    row = pl.BlockSpec((None, ts, d), lambda i, j: (i, j, 0))
    per_b = pl.BlockSpec((None, 1, d), lambda i, j: (i, 0, 0))
    vec = pl.BlockSpec((1, d), lambda i, j: (0, 0))
    return pl.pallas_call(
        body, name="norm_mod_bwd", grid=(b, s // ts),
        in_specs=[row, row, row, vec, per_b],
        out_specs=[row, per_b, per_b, vec],
        out_shape=[jax.ShapeDtypeStruct((b, s, d), F32), jax.ShapeDtypeStruct((b, 1, d), F32),
                   jax.ShapeDtypeStruct((b, 1, d), F32), jax.ShapeDtypeStruct((1, d), F32)],
        compiler_params=_cparams(("arbitrary", "arbitrary")),
    )(x, dh, dx1, g_norm, scale)


def _loss_head(x, out, gate, g_final, target):
    b, s, d = x.shape
    ts = _tile(s, 256)

    def body(x_ref, o_ref, gt_ref, g_ref, t_ref, dx1_ref, dout_ref, dgt_ref, dg_ref, loss_ref):
        i, j = pl.program_id(0), pl.program_id(1)

        @pl.when(j == 0)
        def _():
            dgt_ref[...] = jnp.zeros_like(dgt_ref)

        @pl.when((i == 0) & (j == 0))
        def _():
            dg_ref[...] = jnp.zeros_like(dg_ref)
            loss_ref[...] = jnp.zeros_like(loss_ref)

        ov, gt, g = o_ref[...], gt_ref[...], g_ref[...]
        x1 = x_ref[...] + gt * ov
        r = lax.rsqrt(jnp.mean(x1 * x1, axis=-1, keepdims=True) + EPS)
        xh = x1 * r
        err = xh * g - t_ref[...]
        loss_ref[...] += 0.5 * jnp.sum(jnp.mean(err * err, axis=-1, keepdims=True))
        dfin = err * (1.0 / d)
        dg_ref[...] += jnp.sum(dfin * xh, axis=0, keepdims=True)
        u = dfin * g
        dx1 = r * u - x1 * (r * r * r) * jnp.mean(u * x1, axis=-1, keepdims=True)
        dx1_ref[...] = dx1
        dgt_ref[...] += jnp.sum(dx1 * ov, axis=0, keepdims=True)
        dout_ref[...] = (gt * dx1).astype(BF16)

    row = pl.BlockSpec((None, ts, d), lambda i, j: (i, j, 0))
    per_b = pl.BlockSpec((None, 1, d), lambda i, j: (i, 0, 0))
    vec = pl.BlockSpec((1, d), lambda i, j: (0, 0))
    return pl.pallas_call(
        body, name="loss_head", grid=(b, s // ts),
        in_specs=[row, row, per_b, vec, row],
        out_specs=[row, row, per_b, vec, pl.BlockSpec((1, 128), lambda i, j: (0, 0))],
        out_shape=[jax.ShapeDtypeStruct((b, s, d), F32), jax.ShapeDtypeStruct((b, s, d), BF16),
                   jax.ShapeDtypeStruct((b, 1, d), F32), jax.ShapeDtypeStruct((1, d), F32),
                   jax.ShapeDtypeStruct((1, 128), F32)],
        compiler_params=_cparams(("arbitrary", "arbitrary")),
    )(x, out, gate, g_final, target)


def _head_out(o, zg, g):
    rinv = lax.rsqrt(jnp.mean(o * o, axis=-1, keepdims=True) + EPS)
    return ((o * rinv) * g) * (zg * _sigmoid(zg))


def _head_out_bwd(o, zg, g, dy):
    rinv = lax.rsqrt(jnp.mean(o * o, axis=-1, keepdims=True) + EPS)
    rn = o * rinv
    sg = _sigmoid(zg)
    sil = zg * sg
    dzg = dy * (rn * g) * (sg * (1.0 + zg * (1.0 - sg)))
    dg = jnp.sum(dy * rn * sil, axis=0, keepdims=True)
    drn = dy * g * sil
    do = rinv * drn - o * (rinv * rinv * rinv) * jnp.mean(drn * o, axis=-1, keepdims=True)
    return do, dzg, dg


def _head_spec(s):
    return pl.BlockSpec((None, s, HEAD_DIM), lambda b, h: (b, 0, h))


def _seg_spec(s, seg):
    return pl.BlockSpec((None, None, s, HEAD_DIM), lambda b, h: (seg, b, 0, h))


def _seg4_spec(s, group):
    return pl.BlockSpec((4, None, s, HEAD_DIM), lambda b, h: (group, b, 0, h))


SB_BLOCK = 256


def _softplus_parts(z):
    e = jnp.exp(-jnp.abs(z))
    sp = jnp.maximum(z, 0.0) + jnp.log1p(e)
    inv = 1.0 / (1.0 + e)
    sig = jnp.where(z >= 0.0, inv, e * inv)
    return sp, sig


def _split_dot(a, u):
    hi = a.astype(BF16)
    lo = (a - hi.astype(F32)).astype(BF16)
    return _dot(hi, u, NN) + _dot(lo, u, NN)


def _sb_fwd(proj8, g_sb):
    _, b, s, w = proj8.shape
    n_heads = w // HEAD_DIM
    tb = _tile(s, SB_BLOCK)
    nq = s // tb
    scale = 1.0 / math.sqrt(HEAD_DIM)

    def body(q_ref, k_ref, v_ref, zg_ref, g_ref, o_ref, tot_ref, y_ref):
        ri = lax.broadcasted_iota(jnp.int32, (tb, tb), 0)
        ci = lax.broadcasted_iota(jnp.int32, (tb, tb), 1)
        u_excl = (ri > ci).astype(BF16)
        g = g_ref[...]

        def qblock(i, _):
            rows = pl.ds(pl.multiple_of(i * tb, tb), tb)
            q = q_ref[rows, :].astype(BF16)

            def kblock(jj, carry):
                acc, csum = carry
                j = i - jj
                cols = pl.ds(pl.multiple_of(j * tb, tb), tb)
                k = k_ref[cols, :].astype(BF16)
                v = v_ref[cols, :].astype(BF16)
                z = _dot(q, k, NT) * scale
                valid = (ci < ri) | (jj > 0)
                sp, _ = _softplus_parts(z)
                l1m = jnp.where(valid, -sp, 0.0)
                suf = _split_dot(l1m, u_excl) + csum
                a = jnp.where(valid, jnp.exp((z - sp) + suf), 0.0)
                acc = acc + _dot(a.astype(BF16), v, NN)
                csum = csum + jnp.sum(l1m, axis=1, keepdims=True)
                return acc, csum

            acc, tot = lax.fori_loop(0, i + 1, kblock, (jnp.zeros((tb, HEAD_DIM), F32), jnp.zeros((tb, 1), F32)))
            o_ref[rows, :] = acc
            tot_ref[rows, :] = jnp.broadcast_to(tot, (tb, HEAD_DIM))
            y_ref[rows, :] = _head_out(acc, zg_ref[rows, :], g).astype(BF16)
            return 0

        lax.fori_loop(0, nq, qblock, 0)

    return pl.pallas_call(
        body, name="sb_fwd", grid=(b, n_heads),
        in_specs=[_seg_spec(s, 0), _seg_spec(s, 1), _seg_spec(s, 2), _seg_spec(s, 3),
                  pl.BlockSpec((1, HEAD_DIM), lambda i, h: (0, h))],
        out_specs=[_head_spec(s), _head_spec(s), _seg_spec(s, 0)],
        out_shape=[jax.ShapeDtypeStruct((b, s, w), F32), jax.ShapeDtypeStruct((b, s, w), F32),
                   jax.ShapeDtypeStruct((2, b, s, w), BF16)],
        compiler_params=_cparams(("parallel", "parallel")),
    )(proj8, proj8, proj8, proj8, g_sb)


def _sb_bwd(proj8, o_sb, tot_sb, dy2, g_sb, fused=None):
    _, b, s, w = proj8.shape
    n_heads = w // HEAD_DIM
    tb = _tile(s, SB_BLOCK)
    nq = s // tb
    scale = 1.0 / math.sqrt(HEAD_DIM)

    f_in, f_in_specs, f_out, f_out_specs, f_scratch = _fused_specs(fused)
    grid = (b, n_heads)

    def body(*refs):
        q_ref, k_ref, v_ref, zg_ref, o_ref, tot_ref, dy_ref, g_ref = refs[:8]
        dp_ref, dg_ref = refs[8 + len(f_in):10 + len(f_in)]
        do_s, dk_s, dv_s = refs[10 + len(f_in) + len(f_out):13 + len(f_in) + len(f_out)]
        f_refs = (*refs[8:8 + len(f_in)], *refs[10 + len(f_in):10 + len(f_in) + len(f_out)],
                  *refs[13 + len(f_in) + len(f_out):])
        _fused_begin(fused, grid, f_refs)
        dq_ref, dk_ref, dv_ref, dzg_ref = (dp_ref.at[n] for n in range(4))
        ri = lax.broadcasted_iota(jnp.int32, (tb, tb), 0)
        ci = lax.broadcasted_iota(jnp.int32, (tb, tb), 1)
        u_le = (ri <= ci).astype(BF16)
        u_lt = (ri < ci).astype(BF16)
        g = g_ref[...]

        def prologue(i, dg):
            rows = pl.ds(pl.multiple_of(i * tb, tb), tb)
            do, dzg, dgi = _head_out_bwd(o_ref[rows, :], zg_ref[rows, :], g, dy_ref[rows, :])
            dzg_ref[rows, :] = dzg.astype(BF16)
            do_s[rows, :] = do.astype(BF16)
            return dg + dgi

        dg_ref[...] = lax.fori_loop(0, nq, prologue, jnp.zeros((1, HEAD_DIM), F32))
        dk_s[...] = jnp.zeros_like(dk_s)
        dv_s[...] = jnp.zeros_like(dv_s)

        def qblock(i, _):
            rows = pl.ds(pl.multiple_of(i * tb, tb), tb)
            q = q_ref[rows, :].astype(BF16)
            do = do_s[rows, :]
            tot = tot_ref[rows, :][:, :1]

            def kblock(j, carry):
                dq, pre_l, pre_g = carry
                cols = pl.ds(pl.multiple_of(j * tb, tb), tb)
                k = k_ref[cols, :].astype(BF16)
                v = v_ref[cols, :].astype(BF16)
                z = _dot(q, k, NT) * scale
                valid = (ci < ri) | (j < i)
                sp, sig = _softplus_parts(z)
                l1m = jnp.where(valid, -sp, 0.0)
                suf = tot - (_split_dot(l1m, u_le) + pre_l)
                a = jnp.where(valid, jnp.exp((z - sp) + suf), 0.0)
                da = _dot(do, v, NT)
                gg = a * da
                big_g = _split_dot(gg, u_lt) + pre_g
                dz = jnp.where(valid, gg * (1.0 - sig) - big_g * sig, 0.0)
                dzs = (dz * scale).astype(BF16)
                dq = dq + _dot(dzs, k, NN)
                dk_s[cols, :] += _dot(dzs, q, TN)
                dv_s[cols, :] += _dot(a.astype(BF16), do, TN)
                pre_l = pre_l + jnp.sum(l1m, axis=1, keepdims=True)
                pre_g = pre_g + jnp.sum(gg, axis=1, keepdims=True)
                return dq, pre_l, pre_g

            zero = jnp.zeros((tb, 1), F32)
            dq, _, _ = lax.fori_loop(0, i + 1, kblock, (jnp.zeros((tb, HEAD_DIM), F32), zero, zero))
            dq_ref[rows, :] = dq.astype(BF16)
            return 0

        lax.fori_loop(0, nq, qblock, 0)
        dk_ref[...] = dk_s[...].astype(BF16)
        dv_ref[...] = dv_s[...].astype(BF16)
        _fused_end(fused, grid, f_refs)

    return pl.pallas_call(
        body, name="sb_bwd", grid=grid,
        in_specs=[_seg_spec(s, 0), _seg_spec(s, 1), _seg_spec(s, 2), _seg_spec(s, 3),
                  _head_spec(s), _head_spec(s), _seg_spec(s, 0),
                  pl.BlockSpec((1, HEAD_DIM), lambda i, h: (0, h))] + f_in_specs,
        out_specs=[_seg4_spec(s, 0), pl.BlockSpec((None, 1, HEAD_DIM), lambda i, h: (i, 0, h))] + f_out_specs,
        out_shape=[jax.ShapeDtypeStruct((8, b, s, w), BF16), jax.ShapeDtypeStruct((b, 1, w), F32)] + f_out,
        scratch_shapes=[pltpu.VMEM((s, HEAD_DIM), BF16), pltpu.VMEM((s, HEAD_DIM), F32),
                        pltpu.VMEM((s, HEAD_DIM), F32)] + f_scratch,
        compiler_params=_cparams(("arbitrary", "arbitrary")),
    )(proj8, proj8, proj8, proj8, o_sb, tot_sb, dy2, g_sb, *f_in)


DIL_BLOCK = 128


def _dil_chunks(s, r):
    length = s // r
    out = []
    for rho in range(r):
        for cc in range(length // DIL_BLOCK):
            if r == 1:
                nat = pl.ds(cc * DIL_BLOCK, DIL_BLOCK)
            else:
                nat = pl.ds(rho + r * DIL_BLOCK * cc, DIL_BLOCK, stride=r)
            out.append((nat, pl.ds(rho * length + cc * DIL_BLOCK, DIL_BLOCK)))
    return out


def _dil_scores(q, kc, kp, bias_c, bias_p, first):
    n = DIL_BLOCK
    ri = lax.broadcasted_iota(jnp.int32, (n, n), 0)
    ci = lax.broadcasted_iota(jnp.int32, (n, n), 1)
    scale = 1.0 / math.sqrt(HEAD_DIM)
    s_c = jnp.where(ci <= ri, _dot(q, kc, NT) * scale - bias_c, NEG_BIG)
    s_p = jnp.where((ci >= ri) & jnp.logical_not(first), _dot(q, kp, NT) * scale - bias_p, NEG_BIG)
    return s_c, s_p


def _dil_bias(slope, r):
    n = DIL_BLOCK
    ri = lax.broadcasted_iota(jnp.int32, (n, n), 0)
    ci = lax.broadcasted_iota(jnp.int32, (n, n), 1)
    steps = (ri - ci).astype(F32)
    return slope * (steps * r), slope * ((steps + n) * r)


def _dil_check(s):
    for window, r in DIL_PAIRS:
        assert window // r == DIL_BLOCK and s % (r * DIL_BLOCK) == 0, (s, window, r)


def _dil_fwd(proj8, g_dil, slopes, y2):
    _, b, s, w = proj8.shape
    n_heads = w // HEAD_DIM
    _dil_check(s)
    n = DIL_BLOCK
    nt = s // n

    def body(q_ref, k_ref, v_ref, zg_ref, g_ref, sl_ref, y_in, o_ref, lse_ref, y_ref,
             qp, kp, vp, pnum, pm, pl_, acc_s, m_s, l_s):
        del y_in
        slope = sl_ref[...][:, :1]

        for (window, r) in DIL_PAIRS:
            nb = (s // r) // n
            bias_c, bias_p = _dil_bias(slope, float(r))
            for nat, per in _dil_chunks(s, r):
                qp[per, :] = q_ref[nat, :].astype(BF16)
                kp[per, :] = k_ref[nat, :].astype(BF16)
                vp[per, :] = v_ref[nat, :].astype(BF16)
            num_t, m_t, l_t = (acc_s, m_s, l_s) if r == 1 else (pnum, pm, pl_)

            def tile(t, _):
                rows = pl.ds(pl.multiple_of(t * n, n), n)
                prow = pl.ds(pl.multiple_of(jnp.maximum(t - 1, 0) * n, n), n)
                first = lax.rem(t, nb) == 0
                s_c, s_p = _dil_scores(qp[rows, :], kp[rows, :], kp[prow, :], bias_c, bias_p, first)
                m = jnp.maximum(jnp.max(s_c, axis=1, keepdims=True), jnp.max(s_p, axis=1, keepdims=True))
                p_c = jnp.exp(s_c - m)
                p_p = jnp.exp(s_p - m)
                l = jnp.sum(p_c, axis=1, keepdims=True) + jnp.sum(p_p, axis=1, keepdims=True)
                num_t[rows, :] = _dot(p_c.astype(BF16), vp[rows, :], NN) + _dot(p_p.astype(BF16), vp[prow, :], NN)
                m_t[rows, :] = jnp.broadcast_to(m, (n, HEAD_DIM))
                l_t[rows, :] = jnp.broadcast_to(l, (n, HEAD_DIM))
                return 0

            lax.fori_loop(0, nt, tile, 0)
            if r != 1:
                for nat, per in _dil_chunks(s, r):
                    m_old, m_new_p = m_s[nat, :], pm[per, :]
                    m_new = jnp.maximum(m_old, m_new_p)
                    a_old, a_p = jnp.exp(m_old - m_new), jnp.exp(m_new_p - m_new)
                    m_s[nat, :] = m_new
                    l_s[nat, :] = l_s[nat, :] * a_old + pl_[per, :] * a_p
                    acc_s[nat, :] = acc_s[nat, :] * a_old + pnum[per, :] * a_p

        g = g_ref[...]

        def finish(t, _):
            rows = pl.ds(pl.multiple_of(t * n, n), n)
            l = l_s[rows, :]
            o = acc_s[rows, :] / l
            o_ref[rows, :] = o
            lse_ref[rows, :] = m_s[rows, :] + jnp.log(l)
            y_ref[rows, :] = _head_out(o, zg_ref[rows, :], g).astype(BF16)
            return 0

        lax.fori_loop(0, nt, finish, 0)

    f32_s = pltpu.VMEM((s, HEAD_DIM), F32)
    bf_s = pltpu.VMEM((s, HEAD_DIM), BF16)
    return pl.pallas_call(
        body, name="dil_fwd", grid=(b, n_heads),
        in_specs=[_seg_spec(s, 4), _seg_spec(s, 5), _seg_spec(s, 6), _seg_spec(s, 7),
                  pl.BlockSpec((1, HEAD_DIM), lambda i, h: (0, h)),
                  pl.BlockSpec((None, 1, HEAD_DIM), lambda i, h: (h, 0, 0)), ANY],
        out_specs=[_head_spec(s), _head_spec(s), _seg_spec(s, 1)],
        out_shape=[jax.ShapeDtypeStruct((b, s, w), F32), jax.ShapeDtypeStruct((b, s, w), F32),
                   jax.ShapeDtypeStruct((2, b, s, w), BF16)],
        scratch_shapes=[bf_s, bf_s, bf_s, f32_s, f32_s, f32_s, f32_s, f32_s, f32_s],
        input_output_aliases={6: 2},
        compiler_params=_cparams(("parallel", "parallel")),
    )(proj8, proj8, proj8, proj8, g_dil, slopes, y2)


def _dil_bwd(proj8, o_dl, lse_dl, dy2, g_dil, slopes, dproj8):
    _, b, s, w = proj8.shape
    n_heads = w // HEAD_DIM
    _dil_check(s)
    n = DIL_BLOCK
    nt = s // n
    scale = 1.0 / math.sqrt(HEAD_DIM)

    def body(q_ref, k_ref, v_ref, zg_ref, o_ref, lse_ref, dy_ref, g_ref, sl_ref, dp_in, dp_ref, dg_ref,
             do_n, dt_n, dq_n, dk_n, dv_n, qp, kp, vp, dop, dtp, lsep, pdq, pdk, pdv):
        del dp_in
        dq_ref, dk_ref, dv_ref, dzg_ref = (dp_ref.at[i] for i in range(4))
        slope = sl_ref[...][:, :1]
        g = g_ref[...]

        def prologue(t, dg):
            rows = pl.ds(pl.multiple_of(t * n, n), n)
            o = o_ref[rows, :]
            do, dzg, dgi = _head_out_bwd(o, zg_ref[rows, :], g, dy_ref[rows, :])
            dzg_ref[rows, :] = dzg.astype(BF16)
            do_n[rows, :] = do
            dt_n[rows, :] = jnp.broadcast_to(jnp.sum(do * o, axis=-1, keepdims=True), (n, HEAD_DIM))
            return dg + dgi

        dg_ref[...] = lax.fori_loop(0, nt, prologue, jnp.zeros((1, HEAD_DIM), F32))
        dq_n[...] = jnp.zeros_like(dq_n)
        dk_n[...] = jnp.zeros_like(dk_n)
        dv_n[...] = jnp.zeros_like(dv_n)

        for (window, r) in DIL_PAIRS:
            nb = (s // r) // n
            bias_c, bias_p = _dil_bias(slope, float(r))
            for nat, per in _dil_chunks(s, r):
                qp[per, :] = q_ref[nat, :].astype(BF16)
                kp[per, :] = k_ref[nat, :].astype(BF16)
                vp[per, :] = v_ref[nat, :].astype(BF16)
                dop[per, :] = do_n[nat, :].astype(BF16)
                dtp[per, :] = dt_n[nat, :]
                lsep[per, :] = lse_ref[nat, :]
            pdk[...] = jnp.zeros_like(pdk)
            pdv[...] = jnp.zeros_like(pdv)

            def tile(t, _):
                rows = pl.ds(pl.multiple_of(t * n, n), n)
                prow = pl.ds(pl.multiple_of(jnp.maximum(t - 1, 0) * n, n), n)
                first = lax.rem(t, nb) == 0
                q, kc, kpv, do = qp[rows, :], kp[rows, :], kp[prow, :], dop[rows, :]
                s_c, s_p = _dil_scores(q, kc, kpv, bias_c, bias_p, first)
                lse = lsep[rows, :][:, :1]
                dt = dtp[rows, :][:, :1]
                p_c = jnp.exp(s_c - lse)
                p_p = jnp.exp(s_p - lse)
                ds_c = ((p_c * (_dot(do, vp[rows, :], NT) - dt)) * scale).astype(BF16)
                ds_p = ((p_p * (_dot(do, vp[prow, :], NT) - dt)) * scale).astype(BF16)
                pdq[rows, :] = _dot(ds_c, kc, NN) + _dot(ds_p, kpv, NN)
                pdk[rows, :] += _dot(ds_c, q, TN)
                pdk[prow, :] += _dot(ds_p, q, TN)
                pdv[rows, :] += _dot(p_c.astype(BF16), do, TN)
                pdv[prow, :] += _dot(p_p.astype(BF16), do, TN)
                return 0

            lax.fori_loop(0, nt, tile, 0)
            for nat, per in _dil_chunks(s, r):
                dq_n[nat, :] += pdq[per, :]
                dk_n[nat, :] += pdk[per, :]
                dv_n[nat, :] += pdv[per, :]

        dq_ref[...] = dq_n[...].astype(BF16)
        dk_ref[...] = dk_n[...].astype(BF16)
        dv_ref[...] = dv_n[...].astype(BF16)

    f32_s = pltpu.VMEM((s, HEAD_DIM), F32)
    bf_s = pltpu.VMEM((s, HEAD_DIM), BF16)
    return pl.pallas_call(
        body, name="dil_bwd", grid=(b, n_heads),
        in_specs=[_seg_spec(s, 4), _seg_spec(s, 5), _seg_spec(s, 6), _seg_spec(s, 7),
                  _head_spec(s), _head_spec(s), _seg_spec(s, 1),
                  pl.BlockSpec((1, HEAD_DIM), lambda i, h: (0, h)),
                  pl.BlockSpec((None, 1, HEAD_DIM), lambda i, h: (h, 0, 0)), ANY],
        out_specs=[_seg4_spec(s, 1), pl.BlockSpec((None, 1, HEAD_DIM), lambda i, h: (i, 0, h))],
        out_shape=[jax.ShapeDtypeStruct((8, b, s, w), BF16), jax.ShapeDtypeStruct((b, 1, w), F32)],
        scratch_shapes=[f32_s] * 5 + [bf_s] * 4 + [f32_s] * 5,
        input_output_aliases={9: 0},
        compiler_params=_cparams(("parallel", "parallel")),
    )(proj8, proj8, proj8, proj8, o_dl, lse_dl, dy2, g_dil, slopes, dproj8)


def _small_update(gathered, n_b, params, m, v):
    n_dev, _, width = gathered.shape

    def body(g_ref, p_ref, m_ref, v_ref, grad_ref, d_ref, nm_ref, nv_ref, loss_ref):
        for row in range(2):
            acc = None
            for dev in range(n_dev):
                for i in range(n_b):
                    term = g_ref[dev, pl.ds(row * n_b + i, 1), :]
                    acc = term if acc is None else acc + term
            grad_ref[pl.ds(row, 1), :] = acc
        loss = g_ref[0, pl.ds(2 * n_b, 1), pl.ds(0, 128)]
        for dev in range(1, n_dev):
            loss = loss + g_ref[dev, pl.ds(2 * n_b, 1), pl.ds(0, 128)]
        loss_ref[...] = loss
        d, nm, nv = _adamw_math(p_ref[...], grad_ref[...], m_ref[...], v_ref[...])
        d_ref[...] = d
        nm_ref[...] = nm
        nv_ref[...] = nv

    sds = jax.ShapeDtypeStruct((2, width), F32)
    return pl.pallas_call(
        body, name="small_update",
        in_specs=[VMEM_SPEC] * 4, out_specs=[VMEM_SPEC] * 5,
        out_shape=[sds, sds, sds, sds, jax.ShapeDtypeStruct((1, 128), F32)],
        compiler_params=_cparams(),
    )(gathered, params, m, v)


def _wada_update(c_t, dmod, w, m, v):
    d, bt = c_t.shape
    _, n = dmod.shape
    tr, tc = _tile(d, 512), _tile(n, 1024)

    def body(c_ref, dm_ref, w_ref, m_ref, v_ref, g_ref, d_ref, nm_ref, nv_ref):
        cv = c_ref[...]
        cs = (cv * _sigmoid(cv)).astype(BF16)
        grad = _dot(cs, dm_ref[...].astype(BF16), NN)
        g_ref[...] = grad
        dl, nm, nv = _adamw_math(w_ref[...], grad, m_ref[...], v_ref[...])
        d_ref[...] = dl
        nm_ref[...] = nm
        nv_ref[...] = nv

    spec = pl.BlockSpec((tr, tc), lambda i, j: (i, j))
    sds = jax.ShapeDtypeStruct((d, n), F32)
    return pl.pallas_call(
        body, name="wada_update", grid=(d // tr, n // tc),
        in_specs=[pl.BlockSpec((tr, bt), lambda i, j: (i, 0)), pl.BlockSpec((bt, tc), lambda i, j: (0, j)),
                  spec, spec, spec],
        out_specs=[spec] * 4, out_shape=[sds] * 4,
        compiler_params=_cparams(("parallel", "parallel")),
    )(c_t, dmod, w, m, v)


def _reduce_begin(gs, tag):
    ra = _sibling_half_swap(gs, "swap_" + tag)
    pa, own = _pair_sum(gs, ra, "pair_sum_" + tag)
    return _ScatterChips(pa), own


def _reduce_finish(rb, own, w, m, v, tag):
    half = _chip_sum(rb, own, "chip_sum_" + tag)
    grad = _sibling_join(half, "join_" + tag)
    delta, nm, nv = _adamw(w, grad, m, v, "adamw_" + tag)
    return grad, delta, nm, nv


def kernel(x, c, w_ada, b_ada, g_norm, w_in, g_sb, g_dil, w_out, g_final, loss_target, m_w_ada, m_b_ada, m_g_norm, m_w_in, m_g_sb, m_g_dil, m_w_out, m_g_final, v_w_ada, v_b_ada, v_g_norm, v_w_in, v_g_sb, v_g_dil, v_w_out, v_g_final):
    nb, s, d = x.shape
    t = nb * s
    na = w_ada.shape[2]
    cs = w_in.shape[2]
    w = cs // 2
    n_heads = w // HEAD_DIM
    r_out = w_out.shape[1]
    assert 2 * nb + 1 <= 8 and 2 * d + 2 * w <= 3 * d and N_CHIPS * na == 3 * d and N_CHIPS * r_out == 2 * w
    xi, yi, ci = _place()
    chip = 2 * xi + yi
    dev = 2 * chip + ci

    c_all = _allgather8(jnp.pad(c, ((0, 8 - nb), (0, 0))), "gather_c")
    c16 = c_all.reshape(N_DEV, 8, d)[:, :nb].reshape(N_DEV * nb, d)
    b_ada_shard = lax.dynamic_slice(b_ada, (0, chip * na), (1, na))
    mod_part = _mod_fwd(c16, w_ada[0], b_ada_shard)
    mod_all = _allgather8(mod_part, "gather_mod")
    mod_full = mod_all.reshape(N_CHIPS, 2, N_DEV * nb, na)[:, 0].transpose(1, 0, 2).reshape(N_DEV * nb, 3 * d)
    mod = lax.dynamic_slice(mod_full, (dev * nb, 0), (nb, 3 * d))
    shift, scale, gate = (mod[:, i * d:(i + 1) * d].reshape(nb, 1, d) for i in range(3))

    h = _norm_mod_fwd(x, g_norm, scale, shift)
    h2 = h.reshape(t, d)
    ws_in = _allgather_chips(_cast_bf16_slab(w_in[0], "cast_w_in"), "gather_w_in")
    ws_out = _allgather_chips(_cast_bf16_slab(w_out[0], "cast_w_out"), "gather_w_out")
    w_out_full = ws_out.reshape(2 * w, d)

    proj8 = _proj_fwd(h2, ws_in).reshape(8, nb, s, w)
    slopes = jnp.exp2(-ALIBI_MAX_BIAS * jnp.arange(1, n_heads + 1, dtype=F32) / n_heads)
    slopes = jnp.broadcast_to(slopes[:, None, None], (n_heads, 1, HEAD_DIM))
    o_sb, tot_sb, y2 = _sb_fwd(proj8, g_sb)
    o_dl, lse_dl, y2 = _dil_fwd(proj8, g_dil, slopes, y2)
    y2f = y2.reshape(2, t, w)
    out = _out_fwd(y2f, w_out_full)

    dx1, dout, dgate, dg_final, loss_part = _loss_head(
        x, out.reshape(nb, s, d), gate, g_final.reshape(1, d), loss_target)
    dout2 = dout.reshape(t, d)
    gs_out = _out_bwd_w(y2f, dout2).reshape(N_CHIPS, r_out, d)
    scatter_out, own_out = _reduce_begin(gs_out, "w_out")
    dy2 = _out_bwd_y(dout2, w_out_full).reshape(2, nb, s, w)
    dproj8, dg_sb, rb_out = _sb_bwd(proj8, o_sb, tot_sb, dy2, g_sb, fused=scatter_out)
    dproj8, dg_dl = _dil_bwd(proj8, o_dl, lse_dl, dy2, g_dil, slopes, dproj8)
    dproj8 = dproj8.reshape(8, t, w)
    gs_in = _proj_bwd_w(h2, dproj8)
    scatter_in, own_in = _reduce_begin(gs_in, "w_in")
    dh, rb_in = _proj_bwd_x(dproj8, ws_in, fused=scatter_in)
    grad_x, dshift, dscale, dg_norm = _norm_mod_bwd(x, dh.reshape(nb, s, d), dx1, g_norm, scale)

    width = 3 * d
    dmod = jnp.concatenate([dshift, dscale, dgate], axis=-1).reshape(nb, width)
    gains = jnp.concatenate([dg_sb.reshape(nb, w), dg_dl.reshape(nb, w)], axis=-1)
    gains = jnp.pad(gains, ((0, 0), (2 * d, width - 2 * d - 2 * w)))
    first = jnp.pad(jnp.concatenate([dg_norm, dg_final], axis=-1), ((0, nb - 1), (0, width - 2 * d)))
    loss_row = jnp.pad(loss_part, ((0, 0), (0, width - 128)))
    pack = jnp.concatenate([dmod, gains + first, loss_row, jnp.zeros((8 - 2 * nb - 1, width), F32)], axis=0)
    gathered = _allgather8(pack, "gather_small").reshape(N_DEV, 8, width)

    def stack(bias, gn, gf, gsb, gdl):
        row1 = jnp.concatenate([gn.reshape(1, d), gf.reshape(1, d), gsb.reshape(1, w), gdl.reshape(1, w)], axis=-1)
        return jnp.concatenate([bias.reshape(1, width), jnp.pad(row1, ((0, 0), (0, width - 2 * d - 2 * w)))], axis=0)

    small = _small_update(
        gathered, nb, stack(b_ada, g_norm, g_final, g_sb, g_dil),
        stack(m_b_ada, m_g_norm, m_g_final, m_g_sb, m_g_dil), stack(v_b_ada, v_g_norm, v_g_final, v_g_sb, v_g_dil))
    loss = small[4][0, 0]

    def unstack(a):
        return (a[0:1, :], a[1:2, 0:d], a[1, d:2 * d], a[1:2, 2 * d:2 * d + w], a[1:2, 2 * d + w:2 * d + 2 * w])

    (g_b, g_gn, g_gf, g_gsb, g_gdl), (d_b, d_gn, d_gf, d_gsb, d_gdl), (nm_b, nm_gn, nm_gf, nm_gsb, nm_gdl), \
        (nv_b, nv_gn, nv_gf, nv_gsb, nv_gdl) = (unstack(a) for a in small[:4])

    dmod_all = gathered[:, :nb].reshape(N_DEV * nb, width)
    dmod_cols = lax.dynamic_slice(dmod_all, (0, chip * na), (N_DEV * nb, na))
    g_wa, d_wa, nm_wa, nv_wa = _wada_update(c16.T, dmod_cols, w_ada[0], m_w_ada[0], v_w_ada[0])

    g_wi, d_wi, nm_wi, nv_wi = _reduce_finish(rb_in, own_in, w_in[0], m_w_in[0], v_w_in[0], "w_in")
    g_wo, d_wo, nm_wo, nv_wo = _reduce_finish(rb_out, own_out, w_out[0], m_w_out[0], v_w_out[0], "w_out")

    lead = lambda a: a[None]
    return (loss, grad_x,
            lead(g_wa), g_b, g_gn, lead(g_wi), g_gsb, g_gdl, lead(g_wo), g_gf,
            lead(d_wa), d_b, d_gn, lead(d_wi), d_gsb, d_gdl, lead(d_wo), d_gf,
            lead(nm_wa), nm_b, nm_gn, lead(nm_wi), nm_gsb, nm_gdl, lead(nm_wo), nm_gf,
            lead(nv_wa), nv_b, nv_gn, lead(nv_wi), nv_gsb, nv_gdl, lead(nv_wo), nv_gf)
```

```python
import functools
import math

import jax
import jax.numpy as jnp
from jax import lax
from jax.experimental import pallas as pl
from jax.experimental.pallas import tpu as pltpu

F32 = jnp.float32
BF16 = jnp.bfloat16
MESH = pl.DeviceIdType.MESH

HEAD_DIM = 128
EPS = 1e-6
DIL_PAIRS = ((128, 1), (512, 4), (2048, 16))
ALIBI_MAX_BIAS = 8.0
ADAM_LR = 0.001
ADAM_B1 = 0.9
ADAM_B2 = 0.999
ADAM_EPS = 1e-08
ADAM_WD = 0.01
ADAM_STEP = 10
N_CHIPS = 4
N_DEV = 8
VMEM_LIMIT_BYTES = 56 * 1024 * 1024
NEG_BIG = -1e30

NN = (((1,), (0,)), ((), ()))
NT = (((1,), (1,)), ((), ()))
TN = (((0,), (0,)), ((), ()))

ANY = pl.BlockSpec(memory_space=pl.ANY)
VMEM_SPEC = pl.BlockSpec(memory_space=pltpu.VMEM)


def _cparams(sem=None):
    return pltpu.CompilerParams(dimension_semantics=sem, vmem_limit_bytes=VMEM_LIMIT_BYTES)


def _tile(dim, pref):
    t = min(dim, pref)
    assert dim % t == 0, (dim, pref)
    return t


def _dot(a, b, dims):
    return lax.dot_general(a, b, dims, preferred_element_type=F32)


def _sigmoid(x):
    return 1.0 / (1.0 + jnp.exp(-x))


def _place():
    return lax.axis_index("x"), lax.axis_index("y"), lax.axis_index("c")


def _allgather8(x_shard, name):
    m_per, n = x_shard.shape

    def body(x_ref, out_ref, send_sems, recv_sems, local_sem):
        x, y, c = _place()
        me, sibling = (x, y, c), (x, y, 1 - c)
        chips = [(1 - x, y), (x, 1 - y), (1 - x, 1 - y)]

        def rows(px, py, pc):
            return out_ref.at[pl.ds((4 * px + 2 * py + pc) * m_per, m_per), :]

        def copy(k, block, to, src=None):
            return pltpu.make_async_remote_copy(
                src_ref=rows(*block) if src is None else src, dst_ref=rows(*block),
                send_sem=send_sems.at[k], recv_sem=recv_sems.at[k], device_id=to, device_id_type=MESH)

        mine = pltpu.make_async_copy(x_ref, rows(*me), local_sem)
        mine.start()
        first = [copy(0, me, sibling, src=x_ref)]
        first += [copy(1 + j, me, (*chip, c), src=x_ref) for j, chip in enumerate(chips)]
        for cp in first:
            cp.start()
        passed = [copy(4 + j, (*chip, c), sibling) for j, chip in enumerate(chips)]
        for j, chip in enumerate(chips):
            copy(1 + j, (*chip, c), me).wait_recv()
            passed[j].start()
        copy(0, sibling, me).wait_recv()
        for j, chip in enumerate(chips):
            copy(4 + j, (*chip, 1 - c), me).wait_recv()
        for cp in first + passed:
            cp.wait_send()
        mine.wait()

    return pl.pallas_call(
        body, name=name,
        out_shape=jax.ShapeDtypeStruct((N_DEV * m_per, n), x_shard.dtype),
        in_specs=[VMEM_SPEC], out_specs=VMEM_SPEC,
        scratch_shapes=[pltpu.SemaphoreType.DMA((7,)), pltpu.SemaphoreType.DMA((7,)), pltpu.SemaphoreType.DMA],
    )(x_shard)


def _allgather_chips(ws, name):
    _, r, cdim = ws.shape
    half = r // 2

    def body(in_ref, out_ref, send_sems, recv_sems):
        del in_ref
        x, y, c = _place()
        sibling = (x, y, 1 - c)
        chips = [(1 - x, y), (x, 1 - y), (1 - x, 1 - y)]

        def copy(k, px, py, pc, to):
            rows = out_ref.at[2 * px + py, pl.ds(pc * half, half), :]
            return pltpu.make_async_remote_copy(
                src_ref=rows, dst_ref=rows, send_sem=send_sems.at[k], recv_sem=recv_sems.at[k],
                device_id=to, device_id_type=MESH)

        first = [copy(j, x, y, c, (*chip, c)) for j, chip in enumerate(chips)]
        for cp in first:
            cp.start()
        passed = [copy(3 + j, *chip, c, sibling) for j, chip in enumerate(chips)]
        for j, chip in enumerate(chips):
            copy(j, *chip, c, (x, y, c)).wait_recv()
            passed[j].start()
        for j, chip in enumerate(chips):
            copy(3 + j, *chip, 1 - c, (x, y, c)).wait_recv()
        for cp in first + passed:
            cp.wait_send()

    return pl.pallas_call(
        body, name=name,
        out_shape=jax.ShapeDtypeStruct(ws.shape, ws.dtype),
        in_specs=[ANY], out_specs=ANY, input_output_aliases={0: 0},
        scratch_shapes=[pltpu.SemaphoreType.DMA((6,)), pltpu.SemaphoreType.DMA((6,))],
    )(ws)


def _sibling_half_swap(gs, name):
    n, r, cdim = gs.shape
    half = r // 2

    def body(g_ref, out_ref, send_sem, recv_sem):
        x, y, c = _place()
        cp = pltpu.make_async_remote_copy(
            src_ref=g_ref.at[:, pl.ds((1 - c) * half, half), :], dst_ref=out_ref,
            send_sem=send_sem, recv_sem=recv_sem, device_id=(x, y, 1 - c), device_id_type=MESH)
        cp.start()
        cp.wait()

    return pl.pallas_call(
        body, name=name,
        out_shape=jax.ShapeDtypeStruct((n, half, cdim), gs.dtype),
        in_specs=[ANY], out_specs=ANY,
        scratch_shapes=[pltpu.SemaphoreType.DMA, pltpu.SemaphoreType.DMA],
    )(gs)


class _ScatterChips:
    def __init__(self, pa):
        self.inputs = [pa]
        self.out_shapes = [jax.ShapeDtypeStruct(pa.shape, pa.dtype)]
        self.scratch = [pltpu.SemaphoreType.DMA((3,)), pltpu.SemaphoreType.DMA((3,)), pltpu.SemaphoreType.DMA]

    @staticmethod
    def _mine(p_ref, out_ref, send_sems, recv_sems, local_sem):
        x, y, _ = _place()
        return pltpu.make_async_copy(p_ref.at[2 * x + y], out_ref.at[2 * x + y], local_sem)

    @staticmethod
    def _remote(p_ref, out_ref, send_sems, recv_sems, local_sem, incoming):
        x, y, c = _place()
        me = 2 * x + y
        remote = []
        for j, (px, py) in enumerate([(1 - x, y), (x, 1 - y), (1 - x, 1 - y)]):
            remote.append(pltpu.make_async_remote_copy(
                src_ref=p_ref.at[me if incoming else 2 * px + py], dst_ref=out_ref.at[2 * px + py if incoming else me],
                send_sem=send_sems.at[j], recv_sem=recv_sems.at[j], device_id=(px, py, c), device_id_type=MESH))
        return remote

    def start(self, *refs):
        self._mine(*refs).start()
        for cp in self._remote(*refs, incoming=False):
            cp.start()

    def wait(self, *refs):
        for cp in self._remote(*refs, incoming=True):
            cp.wait_recv()
        for cp in self._remote(*refs, incoming=False):
            cp.wait_send()
        self._mine(*refs).wait()


def _fused_specs(fused):
    if fused is None:
        return [], [], [], [], []
    return (list(fused.inputs), [ANY] * len(fused.inputs), list(fused.out_shapes), [ANY] * len(fused.out_shapes),
            list(fused.scratch))


def _fused_begin(fused, grid, refs):
    if fused is not None:
        first = functools.reduce(lambda p, q: p & q, [pl.program_id(i) == 0 for i in range(len(grid))])
        pl.when(first)(lambda: fused.start(*refs))


def _fused_end(fused, grid, refs):
    if fused is not None:
        last = functools.reduce(lambda p, q: p & q, [pl.program_id(i) == g - 1 for i, g in enumerate(grid)])
        pl.when(last)(lambda: fused.wait(*refs))


def _sibling_join(full, name):
    h2, cdim = full.shape
    h = h2 // 2

    def body(in_ref, out_ref, send_sem, recv_sem):
        del in_ref
        x, y, c = _place()
        mine = out_ref.at[pl.ds(c * h, h), :]
        cp = pltpu.make_async_remote_copy(
            src_ref=mine, dst_ref=mine, send_sem=send_sem, recv_sem=recv_sem,
            device_id=(x, y, 1 - c), device_id_type=MESH)
        cp.start()
        theirs = out_ref.at[pl.ds((1 - c) * h, h), :]
        pltpu.make_async_remote_copy(
            src_ref=theirs, dst_ref=theirs, send_sem=send_sem, recv_sem=recv_sem,
            device_id=(x, y, 1 - c), device_id_type=MESH).wait_recv()
        cp.wait_send()

    return pl.pallas_call(
        body, name=name,
        out_shape=jax.ShapeDtypeStruct(full.shape, full.dtype),
        in_specs=[ANY], out_specs=ANY, input_output_aliases={0: 0},
        scratch_shapes=[pltpu.SemaphoreType.DMA, pltpu.SemaphoreType.DMA],
    )(full)


def _cast_bf16_slab(w, name):
    r, cdim = w.shape
    tr, tc = _tile(r, 512), _tile(cdim, 2048)

    def body(pc_ref, w_ref, o_ref):
        o_ref[...] = w_ref[...].astype(BF16)

    return pl.pallas_call(
        body, name=name,
        grid_spec=pltpu.PrefetchScalarGridSpec(
            num_scalar_prefetch=1, grid=(r // tr, cdim // tc),
            in_specs=[pl.BlockSpec((tr, tc), lambda i, j, pc: (i, j))],
            out_specs=pl.BlockSpec((None, tr, tc), lambda i, j, pc: (pc[1], i, j))),
        out_shape=jax.ShapeDtypeStruct((N_CHIPS, r, cdim), BF16),
        compiler_params=_cparams(("parallel", "parallel")),
    )(_place_scalars(), w)


def _place_scalars():
    x, y, c = _place()
    return jnp.stack([c, 2 * x + y]).astype(jnp.int32)


def _pair_sum(gs, ra, name):
    n, r, cdim = gs.shape
    half = r // 2
    tr, tc = _tile(half, 512), _tile(cdim, 2048)
    nt = half // tr

    def body(pc_ref, g_ref, r_ref, o_ref, own_ref):
        val = g_ref[...] + r_ref[...]
        o_ref[...] = val.astype(BF16)

        @pl.when(pl.program_id(2) == pc_ref[1])
        def _():
            own_ref[...] = val

    return pl.pallas_call(
        body, name=name,
        grid_spec=pltpu.PrefetchScalarGridSpec(
            num_scalar_prefetch=1, grid=(nt, cdim // tc, n),
            in_specs=[pl.BlockSpec((None, tr, tc), lambda i, j, s, pc: (s, pc[0] * nt + i, j)),
                      pl.BlockSpec((None, tr, tc), lambda i, j, s, pc: (s, i, j))],
            out_specs=[pl.BlockSpec((None, tr, tc), lambda i, j, s, pc: (s, i, j)),
                       pl.BlockSpec((tr, tc), lambda i, j, s, pc: (i, j))]),
        out_shape=[jax.ShapeDtypeStruct((n, half, cdim), BF16), jax.ShapeDtypeStruct((half, cdim), F32)],
        compiler_params=_cparams(("parallel", "parallel", "arbitrary")),
    )(_place_scalars(), gs, ra)


def _chip_sum(rb, own, name):
    n, h, cdim = rb.shape
    tr, tc = _tile(h, 256), _tile(cdim, 2048)
    nt = h // tr

    def body(pc_ref, r_ref, own_ref, o_ref):
        chip = pc_ref[1]
        acc = None
        for p in range(n):
            term = jnp.where(chip == p, own_ref[...], r_ref[p].astype(F32))
            acc = term if acc is None else acc + term
        o_ref[...] = acc

    return pl.pallas_call(
        body, name=name,
        grid_spec=pltpu.PrefetchScalarGridSpec(
            num_scalar_prefetch=1, grid=(nt, cdim // tc),
            in_specs=[pl.BlockSpec((n, tr, tc), lambda i, j, pc: (0, i, j)),
                      pl.BlockSpec((tr, tc), lambda i, j, pc: (i, j))],
            out_specs=pl.BlockSpec((tr, tc), lambda i, j, pc: (pc[0] * nt + i, j))),
        out_shape=jax.ShapeDtypeStruct((2 * h, cdim), F32),
        compiler_params=_cparams(("parallel", "parallel")),
    )(_place_scalars(), rb, own)


def _adamw_math(w, g, m, v):
    m = ADAM_B1 * m + (1.0 - ADAM_B1) * g
    v = ADAM_B2 * v + (1.0 - ADAM_B2) * (g * g)
    m_hat = m / (1.0 - ADAM_B1 ** ADAM_STEP)
    v_hat = v / (1.0 - ADAM_B2 ** ADAM_STEP)
    delta = -ADAM_LR * (m_hat / (jnp.sqrt(v_hat) + ADAM_EPS) + ADAM_WD * w)
    return delta, m, v


def _adamw(w, g, m, v, name):
    r, cdim = w.shape
    tr, tc = _tile(r, 256), _tile(cdim, 2048)

    def body(w_ref, g_ref, m_ref, v_ref, d_ref, nm_ref, nv_ref):
        d, nm, nv = _adamw_math(w_ref[...], g_ref[...], m_ref[...], v_ref[...])
        d_ref[...] = d
        nm_ref[...] = nm
        nv_ref[...] = nv

    spec = pl.BlockSpec((tr, tc), lambda i, j: (i, j))
    sds = jax.ShapeDtypeStruct((r, cdim), F32)
    return pl.pallas_call(
        body, name=name, grid=(r // tr, cdim // tc),
        in_specs=[spec] * 4, out_specs=[spec] * 3, out_shape=[sds] * 3,
        compiler_params=_cparams(("parallel", "parallel")),
    )(w, g, m, v)


def _matmul(a, b, *, grid, a_spec, b_spec, out_spec, out_shape, acc_shape, dims, name, bias=None, bias_spec=None,
            silu_a=False, fused=None):
    nk = grid[2]
    f_in, f_in_specs, f_out, f_out_specs, f_scratch = _fused_specs(fused)
    n_in = 2 + (bias is not None)

    def body(*refs):
        a_ref, b_ref = refs[:2]
        bias_ref = refs[2] if bias is not None else None
        o_ref = refs[n_in + len(f_in)]
        acc_ref = refs[n_in + len(f_in) + 1 + len(f_out)]
        f_refs = (*refs[n_in:n_in + len(f_in)], *refs[n_in + len(f_in) + 1:n_in + len(f_in) + 1 + len(f_out)],
                  *refs[n_in + len(f_in) + 2 + len(f_out):])
        _fused_begin(fused, grid, f_refs)
        k = pl.program_id(2)

        @pl.when(k == 0)
        def _():
            acc_ref[...] = jnp.zeros_like(acc_ref)

        av = a_ref[...]
        if silu_a:
            av = av * _sigmoid(av)
        acc_ref[...] += _dot(av.astype(BF16), b_ref[...].astype(BF16), dims)

        @pl.when(k == nk - 1)
        def _():
            res = acc_ref[...]
            if bias is not None:
                res = res + bias_ref[...]
            o_ref[...] = res.astype(o_ref.dtype)

        _fused_end(fused, grid, f_refs)

    in_specs = [a_spec, b_spec] + ([] if bias is None else [bias_spec]) + f_in_specs
    args = (a, b) + (() if bias is None else (bias,)) + tuple(f_in)
    sem = ("parallel", "parallel", "arbitrary") if fused is None else ("arbitrary",) * 3
    res = pl.pallas_call(
        body, name=name, grid=grid, in_specs=in_specs, out_specs=[out_spec] + f_out_specs,
        out_shape=[out_shape] + f_out,
        scratch_shapes=[pltpu.VMEM(acc_shape, F32)] + f_scratch,
        compiler_params=_cparams(sem),
    )(*args)
    return res[0] if fused is None else tuple(res)


def _mm_tiles(m, n, k):
    return _tile(m, 1024), _tile(n, 1024), _tile(k, 512)


def _proj_fwd(h2, ws_in):
    t, d = h2.shape
    _, _, cs = ws_in.shape
    w = cs // 2
    tm, tn, tk = _mm_tiles(t, w, d)
    nps, npseg = cs // tn, w // tn
    return _matmul(
        h2, ws_in, grid=(t // tm, 8 * npseg, d // tk), dims=NN, name="proj_fwd",
        a_spec=pl.BlockSpec((tm, tk), lambda m, n, k: (m, k)),
        b_spec=pl.BlockSpec((None, tk, tn), lambda m, n, k: (n // nps, k, n % nps)),
        out_spec=pl.BlockSpec((None, tm, tn), lambda m, n, k: (n // npseg, m, n % npseg)),
        out_shape=jax.ShapeDtypeStruct((8, t, w), F32), acc_shape=(tm, tn))


def _proj_bwd_x(dproj8, ws_in, fused=None):
    _, t, w = dproj8.shape
    _, d, cs = ws_in.shape
    tm, tn, tk = _mm_tiles(t, d, w)
    kps, kpseg = cs // tk, w // tk
    return _matmul(
        dproj8, ws_in, grid=(t // tm, d // tn, 8 * kpseg), dims=NT, name="proj_bwd_x", fused=fused,
        a_spec=pl.BlockSpec((None, tm, tk), lambda m, n, k: (k // kpseg, m, k % kpseg)),
        b_spec=pl.BlockSpec((None, tn, tk), lambda m, n, k: (k // kps, n, k % kps)),
        out_spec=pl.BlockSpec((tm, tn), lambda m, n, k: (m, n)),
        out_shape=jax.ShapeDtypeStruct((t, d), F32), acc_shape=(tm, tn))


def _proj_bwd_w(h2, dproj8):
    t, d = h2.shape
    _, _, w = dproj8.shape
    cs = 2 * w
    tm, tn, tk = _mm_tiles(d, w, t)
    nps, npseg = cs // tn, w // tn
    return _matmul(
        h2, dproj8, grid=(d // tm, 8 * npseg, t // tk), dims=TN, name="proj_bwd_w",
        a_spec=pl.BlockSpec((tk, tm), lambda m, n, k: (k, m)),
        b_spec=pl.BlockSpec((None, tk, tn), lambda m, n, k: (n // npseg, k, n % npseg)),
        out_spec=pl.BlockSpec((None, tm, tn), lambda m, n, k: (n // nps, m, n % nps)),
        out_shape=jax.ShapeDtypeStruct((N_CHIPS, d, cs), F32), acc_shape=(tm, tn))


def _out_fwd(y2, w_out):
    _, t, w = y2.shape
    _, d = w_out.shape
    tm, tn, tk = _mm_tiles(t, d, w)
    kpg = w // tk
    return _matmul(
        y2, w_out, grid=(t // tm, d // tn, 2 * kpg), dims=NN, name="out_fwd",
        a_spec=pl.BlockSpec((None, tm, tk), lambda m, n, k: (k // kpg, m, k % kpg)),
        b_spec=pl.BlockSpec((tk, tn), lambda m, n, k: (k, n)),
        out_spec=pl.BlockSpec((tm, tn), lambda m, n, k: (m, n)),
        out_shape=jax.ShapeDtypeStruct((t, d), F32), acc_shape=(tm, tn))


def _out_bwd_y(dout, w_out):
    t, d = dout.shape
    w = w_out.shape[0] // 2
    tm, tn, tk = _mm_tiles(t, w, d)
    npg = w // tn
    return _matmul(
        dout, w_out, grid=(t // tm, 2 * npg, d // tk), dims=NT, name="out_bwd_y",
        a_spec=pl.BlockSpec((tm, tk), lambda m, n, k: (m, k)),
        b_spec=pl.BlockSpec((tn, tk), lambda m, n, k: (n, k)),
        out_spec=pl.BlockSpec((None, tm, tn), lambda m, n, k: (n // npg, m, n % npg)),
        out_shape=jax.ShapeDtypeStruct((2, t, w), F32), acc_shape=(tm, tn))


def _out_bwd_w(y2, dout):
    _, t, w = y2.shape
    _, d = dout.shape
    tm, tn, tk = _mm_tiles(w, d, t)
    mpg = w // tm
    return _matmul(
        y2, dout, grid=(2 * mpg, d // tn, t // tk), dims=TN, name="out_bwd_w",
        a_spec=pl.BlockSpec((None, tk, tm), lambda m, n, k: (m // mpg, k, m % mpg)),
        b_spec=pl.BlockSpec((tk, tn), lambda m, n, k: (k, n)),
        out_spec=pl.BlockSpec((tm, tn), lambda m, n, k: (m, n)),
        out_shape=jax.ShapeDtypeStruct((2 * w, d), F32), acc_shape=(tm, tn))


def _mod_fwd(c_all, w_ada, b_ada):
    bt, d = c_all.shape
    _, n = w_ada.shape
    tn, tk = _tile(n, 512), _tile(d, 1024)
    return _matmul(
        c_all, w_ada, grid=(1, n // tn, d // tk), dims=NN, name="mod_fwd", silu_a=True,
        a_spec=pl.BlockSpec((bt, tk), lambda i, j, l: (0, l)),
        b_spec=pl.BlockSpec((tk, tn), lambda i, j, l: (l, j)),
        bias=b_ada, bias_spec=pl.BlockSpec((1, tn), lambda i, j, l: (0, j)),
        out_spec=pl.BlockSpec((bt, tn), lambda i, j, l: (0, j)),
        out_shape=jax.ShapeDtypeStruct((bt, n), F32), acc_shape=(bt, tn))


def _norm_mod_fwd(x, g_norm, scale, shift):
    b, s, d = x.shape
    ts = _tile(s, 256)

    def body(x_ref, g_ref, sc_ref, sh_ref, h_ref):
        xv = x_ref[...]
        r = lax.rsqrt(jnp.mean(xv * xv, axis=-1, keepdims=True) + EPS)
        y = (xv * r) * g_ref[...]
        h_ref[...] = (y * (1.0 + sc_ref[...]) + sh_ref[...]).astype(BF16)

    row = pl.BlockSpec((None, ts, d), lambda i, j: (i, j, 0))
    per_b = pl.BlockSpec((None, 1, d), lambda i, j: (i, 0, 0))
    return pl.pallas_call(
        body, name="norm_mod_fwd", grid=(b, s // ts),
        in_specs=[row, pl.BlockSpec((1, d), lambda i, j: (0, 0)), per_b, per_b],
        out_specs=row, out_shape=jax.ShapeDtypeStruct((b, s, d), BF16),
        compiler_params=_cparams(("parallel", "parallel")),
    )(x, g_norm, scale, shift)


def _norm_mod_bwd(x, dh, dx1, g_norm, scale):
    b, s, d = x.shape
    ts = _tile(s, 256)

    def body(x_ref, dh_ref, dx1_ref, g_ref, sc_ref, gx_ref, dsh_ref, dsc_ref, dg_ref):
        i, j = pl.program_id(0), pl.program_id(1)

        @pl.when(j == 0)
        def _():
            dsh_ref[...] = jnp.zeros_like(dsh_ref)
            dsc_ref[...] = jnp.zeros_like(dsc_ref)

        @pl.when((i == 0) & (j == 0))
        def _():
            dg_ref[...] = jnp.zeros_like(dg_ref)

        xv, dhv, g = x_ref[...], dh_ref[...], g_ref[...]
        r = lax.rsqrt(jnp.mean(xv * xv, axis=-1, keepdims=True) + EPS)
        xh = xv * r
        dsh_ref[...] += jnp.sum(dhv, axis=0, keepdims=True)
        dsc_ref[...] += jnp.sum(dhv * (xh * g), axis=0, keepdims=True)
        dn = dhv * (1.0 + sc_ref[...])
        dg_ref[...] += jnp.sum(dn * xh, axis=0, keepdims=True)
        u = dn * g
        dx = r * u - xv * (r * r * r) * jnp.mean(u * xv, axis=-1, keepdims=True)
        gx_ref[...] = dx1_ref[...] + dx

    row = pl.BlockSpec((None, ts, d), lambda i, j: (i, j, 0))
    per_b = pl.BlockSpec((None, 1, d), lambda i, j: (i, 0, 0))
    vec = pl.BlockSpec((1, d), lambda i, j: (0, 0))
    return pl.pallas_call(
        body, name="norm_mod_bwd", grid=(b, s // ts),
        in_specs=[row, row, row, vec, per_b],
        out_specs=[row, per_b, per_b, vec],
        out_shape=[jax.ShapeDtypeStruct((b, s, d), F32), jax.ShapeDtypeStruct((b, 1, d), F32),
                   jax.ShapeDtypeStruct((b, 1, d), F32), jax.ShapeDtypeStruct((1, d), F32)],
        compiler_params=_cparams(("arbitrary", "arbitrary")),
    )(x, dh, dx1, g_norm, scale)


def _loss_head(x, out, gate, g_final, target):
    b, s, d = x.shape
    ts = _tile(s, 256)

    def body(x_ref, o_ref, gt_ref, g_ref, t_ref, dx1_ref, dout_ref, dgt_ref, dg_ref, loss_ref):
        i, j = pl.program_id(0), pl.program_id(1)

        @pl.when(j == 0)
        def _():
            dgt_ref[...] = jnp.zeros_like(dgt_ref)

        @pl.when((i == 0) & (j == 0))
        def _():
            dg_ref[...] = jnp.zeros_like(dg_ref)
            loss_ref[...] = jnp.zeros_like(loss_ref)

        ov, gt, g = o_ref[...], gt_ref[...], g_ref[...]
        x1 = x_ref[...] + gt * ov
        r = lax.rsqrt(jnp.mean(x1 * x1, axis=-1, keepdims=True) + EPS)
        xh = x1 * r
        err = xh * g - t_ref[...]
        loss_ref[...] += 0.5 * jnp.sum(jnp.mean(err * err, axis=-1, keepdims=True))
        dfin = err * (1.0 / d)
        dg_ref[...] += jnp.sum(dfin * xh, axis=0, keepdims=True)
        u = dfin * g
        dx1 = r * u - x1 * (r * r * r) * jnp.mean(u * x1, axis=-1, keepdims=True)
        dx1_ref[...] = dx1
        dgt_ref[...] += jnp.sum(dx1 * ov, axis=0, keepdims=True)
        dout_ref[...] = (gt * dx1).astype(BF16)

    row = pl.BlockSpec((None, ts, d), lambda i, j: (i, j, 0))
    per_b = pl.BlockSpec((None, 1, d), lambda i, j: (i, 0, 0))
    vec = pl.BlockSpec((1, d), lambda i, j: (0, 0))
    return pl.pallas_call(
        body, name="loss_head", grid=(b, s // ts),
        in_specs=[row, row, per_b, vec, row],
        out_specs=[row, row, per_b, vec, pl.BlockSpec((1, 128), lambda i, j: (0, 0))],
        out_shape=[jax.ShapeDtypeStruct((b, s, d), F32), jax.ShapeDtypeStruct((b, s, d), BF16),
                   jax.ShapeDtypeStruct((b, 1, d), F32), jax.ShapeDtypeStruct((1, d), F32),
                   jax.ShapeDtypeStruct((1, 128), F32)],
        compiler_params=_cparams(("arbitrary", "arbitrary")),
    )(x, out, gate, g_final, target)


def _head_out(o, zg, g):
    rinv = lax.rsqrt(jnp.mean(o * o, axis=-1, keepdims=True) + EPS)
    return ((o * rinv) * g) * (zg * _sigmoid(zg))


def _head_out_bwd(o, zg, g, dy):
    rinv = lax.rsqrt(jnp.mean(o * o, axis=-1, keepdims=True) + EPS)
    rn = o * rinv
    sg = _sigmoid(zg)
    sil = zg * sg
    dzg = dy * (rn * g) * (sg * (1.0 + zg * (1.0 - sg)))
    dg = jnp.sum(dy * rn * sil, axis=0, keepdims=True)
    drn = dy * g * sil
    do = rinv * drn - o * (rinv * rinv * rinv) * jnp.mean(drn * o, axis=-1, keepdims=True)
    return do, dzg, dg


def _head_spec(s):
    return pl.BlockSpec((None, s, HEAD_DIM), lambda b, h: (b, 0, h))


def _seg_spec(s, seg):
    return pl.BlockSpec((None, None, s, HEAD_DIM), lambda b, h: (seg, b, 0, h))


def _seg4_spec(s, group):
    return pl.BlockSpec((4, None, s, HEAD_DIM), lambda b, h: (group, b, 0, h))


SB_Q_BLOCK = 512
SB_K_BLOCK = 256


LOG2_E = 1.4426950408889634
SB_LOGIT_SCALE = LOG2_E / math.sqrt(HEAD_DIM)


def _sb_terms(raw, valid):
    t = jnp.where(valid, raw * SB_LOGIT_SCALE, NEG_BIG)
    e = jnp.exp2(-jnp.abs(t))
    l1m = -(jnp.maximum(t, 0.0) + jnp.log2(1.0 + e))
    return t, l1m, e


def _split_dot(a, u):
    hi = a.astype(BF16)
    lo = (a - hi.astype(F32)).astype(BF16)
    return _dot(hi, u, NN) + _dot(lo, u, NN)


def _sb_fwd(proj8, g_sb):
    _, b, s, w = proj8.shape
    n_heads = w // HEAD_DIM
    tq, tk = _tile(s, SB_Q_BLOCK), _tile(s, SB_K_BLOCK)
    nq, kpq = s // tq, tq // tk
    scale = 1.0 / math.sqrt(HEAD_DIM)

    def body(q_ref, k_ref, v_ref, zg_ref, g_ref, o_ref, tot_ref, y_ref):
        u_excl = (lax.broadcasted_iota(jnp.int32, (tk, tk), 0)
                  > lax.broadcasted_iota(jnp.int32, (tk, tk), 1)).astype(BF16)
        ahead = lax.broadcasted_iota(jnp.int32, (tq, tk), 0) - lax.broadcasted_iota(jnp.int32, (tq, tk), 1)
        g = g_ref[...]

        def qblock(i, _):
            rows = pl.ds(pl.multiple_of(i * tq, tq), tq)
            q = q_ref[rows, :].astype(BF16)
            nk = (i + 1) * kpq

            def kblock(jj, carry):
                acc, csum = carry
                j = nk - 1 - jj
                cols = pl.ds(pl.multiple_of(j * tk, tk), tk)
                k = k_ref[cols, :].astype(BF16)
                v = v_ref[cols, :].astype(BF16)
                valid = ahead > j * tk - i * tq
                t, l1m, _ = _sb_terms(_dot(q, k, NT), valid)
                suf = _split_dot(l1m, u_excl) + csum
                a = jnp.exp2((t + l1m) + suf)
                acc = acc + _dot(a.astype(BF16), v, NN)
                csum = csum + jnp.sum(l1m, axis=1, keepdims=True)
                return acc, csum

            acc, tot = lax.fori_loop(0, nk, kblock, (jnp.zeros((tq, HEAD_DIM), F32), jnp.zeros((tq, 1), F32)))
            o_ref[rows, :] = acc
            tot_ref[rows, :] = jnp.broadcast_to(tot, (tq, HEAD_DIM))
            y_ref[rows, :] = _head_out(acc, zg_ref[rows, :], g).astype(BF16)
            return 0

        lax.fori_loop(0, nq, qblock, 0)

    return pl.pallas_call(
        body, name="sb_fwd", grid=(b, n_heads),
        in_specs=[_seg_spec(s, 0), _seg_spec(s, 1), _seg_spec(s, 2), _seg_spec(s, 3),
                  pl.BlockSpec((1, HEAD_DIM), lambda i, h: (0, h))],
        out_specs=[_head_spec(s), _head_spec(s), _seg_spec(s, 0)],
        out_shape=[jax.ShapeDtypeStruct((b, s, w), F32), jax.ShapeDtypeStruct((b, s, w), F32),
                   jax.ShapeDtypeStruct((2, b, s, w), BF16)],
        compiler_params=_cparams(("parallel", "parallel")),
    )(proj8, proj8, proj8, proj8, g_sb)


def _sb_bwd(proj8, o_sb, tot_sb, dy2, g_sb, fused=None):
    _, b, s, w = proj8.shape
    n_heads = w // HEAD_DIM
    tq, tk = _tile(s, SB_Q_BLOCK), _tile(s, SB_K_BLOCK)
    nq, kpq = s // tq, tq // tk
    scale = 1.0 / math.sqrt(HEAD_DIM)

    f_in, f_in_specs, f_out, f_out_specs, f_scratch = _fused_specs(fused)
    grid = (b, n_heads)

    def body(*refs):
        q_ref, k_ref, v_ref, zg_ref, o_ref, tot_ref, dy_ref, g_ref = refs[:8]
        dp_ref, dg_ref = refs[8 + len(f_in):10 + len(f_in)]
        do_s, dk_s, dv_s = refs[10 + len(f_in) + len(f_out):13 + len(f_in) + len(f_out)]
        f_refs = (*refs[8:8 + len(f_in)], *refs[10 + len(f_in):10 + len(f_in) + len(f_out)],
                  *refs[13 + len(f_in) + len(f_out):])
        _fused_begin(fused, grid, f_refs)
        dq_ref, dk_ref, dv_ref, dzg_ref = (dp_ref.at[n] for n in range(4))
        ri = lax.broadcasted_iota(jnp.int32, (tk, tk), 0)
        ci = lax.broadcasted_iota(jnp.int32, (tk, tk), 1)
        u_le = (ri <= ci).astype(BF16)
        u_lt = (ri < ci).astype(BF16)
        ahead = lax.broadcasted_iota(jnp.int32, (tq, tk), 0) - lax.broadcasted_iota(jnp.int32, (tq, tk), 1)
        g = g_ref[...]

        def prologue(i, dg):
            rows = pl.ds(pl.multiple_of(i * tq, tq), tq)
            do, dzg, dgi = _head_out_bwd(o_ref[rows, :], zg_ref[rows, :], g, dy_ref[rows, :])
            dzg_ref[rows, :] = dzg.astype(BF16)
            do_s[rows, :] = do.astype(BF16)
            return dg + dgi

        dg_ref[...] = lax.fori_loop(0, nq, prologue, jnp.zeros((1, HEAD_DIM), F32))
        dk_s[...] = jnp.zeros_like(dk_s)
        dv_s[...] = jnp.zeros_like(dv_s)

        def qblock(i, _):
            rows = pl.ds(pl.multiple_of(i * tq, tq), tq)
            q = q_ref[rows, :].astype(BF16)
            do = do_s[rows, :]
            tot = tot_ref[rows, :][:, :1]

            def kblock(j, carry):
                dq, pre_l, pre_g = carry
                cols = pl.ds(pl.multiple_of(j * tk, tk), tk)
                k = k_ref[cols, :].astype(BF16)
                v = v_ref[cols, :].astype(BF16)
                valid = ahead > j * tk - i * tq
                t, l1m, e = _sb_terms(_dot(q, k, NT), valid)
                suf = tot - (_split_dot(l1m, u_le) + pre_l)
                a = jnp.exp2((t + l1m) + suf)
                da = _dot(do, v, NT)
                gg = a * da
                big_g = _split_dot(gg, u_lt) + pre_g
                inv = 1.0 / (1.0 + e)
                sig = jnp.where(t >= 0.0, inv, e * inv)
                dz = gg - sig * (gg + big_g)
                dzs = (dz * scale).astype(BF16)
                dq = dq + _dot(dzs, k, NN)
                dk_s[cols, :] += _dot(dzs, q, TN)
                dv_s[cols, :] += _dot(a.astype(BF16), do, TN)
                pre_l = pre_l + jnp.sum(l1m, axis=1, keepdims=True)
                pre_g = pre_g + jnp.sum(gg, axis=1, keepdims=True)
                return dq, pre_l, pre_g

            zero = jnp.zeros((tq, 1), F32)
            dq, _, _ = lax.fori_loop(0, (i + 1) * kpq, kblock, (jnp.zeros((tq, HEAD_DIM), F32), zero, zero))
            dq_ref[rows, :] = dq.astype(BF16)
            return 0

        lax.fori_loop(0, nq, qblock, 0)
        dk_ref[...] = dk_s[...].astype(BF16)
        dv_ref[...] = dv_s[...].astype(BF16)
        _fused_end(fused, grid, f_refs)

    return pl.pallas_call(
        body, name="sb_bwd", grid=grid,
        in_specs=[_seg_spec(s, 0), _seg_spec(s, 1), _seg_spec(s, 2), _seg_spec(s, 3),
                  _head_spec(s), _head_spec(s), _seg_spec(s, 0),
                  pl.BlockSpec((1, HEAD_DIM), lambda i, h: (0, h))] + f_in_specs,
        out_specs=[_seg4_spec(s, 0), pl.BlockSpec((None, 1, HEAD_DIM), lambda i, h: (i, 0, h))] + f_out_specs,
        out_shape=[jax.ShapeDtypeStruct((8, b, s, w), BF16), jax.ShapeDtypeStruct((b, 1, w), F32)] + f_out,
        scratch_shapes=[pltpu.VMEM((s, HEAD_DIM), BF16), pltpu.VMEM((s, HEAD_DIM), F32),
                        pltpu.VMEM((s, HEAD_DIM), F32)] + f_scratch,
        compiler_params=_cparams(("arbitrary", "arbitrary")),
    )(proj8, proj8, proj8, proj8, o_sb, tot_sb, dy2, g_sb, *f_in)


DIL_BLOCK = 128
DIL_GROUP = 4


def _dil_chunks(s, r):
    length = s // r
    out = []
    for rho in range(r):
        for cc in range(length // DIL_BLOCK):
            if r == 1:
                nat = pl.ds(cc * DIL_BLOCK, DIL_BLOCK)
            else:
                nat = pl.ds(rho + r * DIL_BLOCK * cc, DIL_BLOCK, stride=r)
            off = rho * length + cc * DIL_BLOCK
            out.append((nat, pl.ds(off, DIL_BLOCK), pl.ds(off + DIL_BLOCK, DIL_BLOCK)))
    return out


def _dil_masks(slope, r):
    n = DIL_BLOCK
    ri = lax.broadcasted_iota(jnp.int32, (n, 2 * n), 0)
    ci = lax.broadcasted_iota(jnp.int32, (n, 2 * n), 1)
    steps = ri - ci + n
    inside = (steps >= 0) & (steps <= n)
    bias = slope * (steps.astype(F32) * r)
    return jnp.where(inside, -bias, NEG_BIG), jnp.where(inside & (ci >= n), -bias, NEG_BIG)


def _dil_scores(q, k_pc, masks, first):
    return _dot(q, k_pc, NT) * (1.0 / math.sqrt(HEAD_DIM)) + jnp.where(first, masks[1], masks[0])


def _dil_check(s):
    assert (s // DIL_BLOCK) % DIL_GROUP == 0, s
    for window, r in DIL_PAIRS:
        assert window // r == DIL_BLOCK and s % (r * DIL_BLOCK) == 0, (s, window, r)


def _dil_fwd(proj8, g_dil, slopes, y2):
    _, b, s, w = proj8.shape
    n_heads = w // HEAD_DIM
    _dil_check(s)
    n = DIL_BLOCK
    nt = s // n

    def body(q_ref, k_ref, v_ref, zg_ref, g_ref, sl_ref, y_in, o_ref, lse_ref, y_ref,
             qp, kp, vp, pnum, pm, pl_, acc_s, m_s, l_s):
        del y_in
        slope = sl_ref[...][:, :1]
        kp[pl.ds(0, n), :] = jnp.zeros((n, HEAD_DIM), BF16)
        vp[pl.ds(0, n), :] = jnp.zeros((n, HEAD_DIM), BF16)

        for (window, r) in DIL_PAIRS:
            nb = (s // r) // n
            masks = _dil_masks(slope, float(r))
            for nat, per, padded in _dil_chunks(s, r):
                qp[per, :] = q_ref[nat, :].astype(BF16)
                kp[padded, :] = k_ref[nat, :].astype(BF16)
                vp[padded, :] = v_ref[nat, :].astype(BF16)
            num_t, m_t, l_t = (acc_s, m_s, l_s) if r == 1 else (pnum, pm, pl_)

            def tiles(tt, _):
                ts = [tt * DIL_GROUP + i for i in range(DIL_GROUP)]
                rows = [pl.ds(pl.multiple_of(t * n, n), n) for t in ts]
                both = [pl.ds(pl.multiple_of(t * n, n), 2 * n) for t in ts]
                sc = [_dil_scores(qp[rw, :], kp[bt, :], masks, lax.rem(t, nb) == 0)
                      for t, rw, bt in zip(ts, rows, both)]
                m = [jnp.max(x, axis=1, keepdims=True) for x in sc]
                p = [jnp.exp(x - mx) for x, mx in zip(sc, m)]
                num = [_dot(x.astype(BF16), vp[bt, :], NN) for x, bt in zip(p, both)]
                for rw, x, mx, nm in zip(rows, p, m, num):
                    num_t[rw, :] = nm
                    m_t[rw, :] = jnp.broadcast_to(mx, (n, HEAD_DIM))
                    l_t[rw, :] = jnp.broadcast_to(jnp.sum(x, axis=1, keepdims=True), (n, HEAD_DIM))
                return 0

            lax.fori_loop(0, nt // DIL_GROUP, tiles, 0)
            if r != 1:
                for nat, per, _ in _dil_chunks(s, r):
                    m_old, m_new_p = m_s[nat, :], pm[per, :]
                    m_new = jnp.maximum(m_old, m_new_p)
                    a_old, a_p = jnp.exp(m_old - m_new), jnp.exp(m_new_p - m_new)
                    m_s[nat, :] = m_new
                    l_s[nat, :] = l_s[nat, :] * a_old + pl_[per, :] * a_p
                    acc_s[nat, :] = acc_s[nat, :] * a_old + pnum[per, :] * a_p

        g = g_ref[...]

        def finish(t, _):
            rows = pl.ds(pl.multiple_of(t * n, n), n)
            l = l_s[rows, :]
            o = acc_s[rows, :] / l
            o_ref[rows, :] = o
            lse_ref[rows, :] = m_s[rows, :] + jnp.log(l)
            y_ref[rows, :] = _head_out(o, zg_ref[rows, :], g).astype(BF16)
            return 0

        lax.fori_loop(0, nt, finish, 0)

    f32_s = pltpu.VMEM((s, HEAD_DIM), F32)
    bf_s = pltpu.VMEM((s, HEAD_DIM), BF16)
    bf_pad = pltpu.VMEM((s + n, HEAD_DIM), BF16)
    return pl.pallas_call(
        body, name="dil_fwd", grid=(b, n_heads),
        in_specs=[_seg_spec(s, 4), _seg_spec(s, 5), _seg_spec(s, 6), _seg_spec(s, 7),
                  pl.BlockSpec((1, HEAD_DIM), lambda i, h: (0, h)),
                  pl.BlockSpec((None, 1, HEAD_DIM), lambda i, h: (h, 0, 0)), ANY],
        out_specs=[_head_spec(s), _head_spec(s), _seg_spec(s, 1)],
        out_shape=[jax.ShapeDtypeStruct((b, s, w), F32), jax.ShapeDtypeStruct((b, s, w), F32),
                   jax.ShapeDtypeStruct((2, b, s, w), BF16)],
        scratch_shapes=[bf_s, bf_pad, bf_pad, f32_s, f32_s, f32_s, f32_s, f32_s, f32_s],
        input_output_aliases={6: 2},
        compiler_params=_cparams(("parallel", "parallel")),
    )(proj8, proj8, proj8, proj8, g_dil, slopes, y2)


def _dil_bwd(proj8, o_dl, lse_dl, dy2, g_dil, slopes, dproj8):
    _, b, s, w = proj8.shape
    n_heads = w // HEAD_DIM
    _dil_check(s)
    n = DIL_BLOCK
    nt = s // n
    scale = 1.0 / math.sqrt(HEAD_DIM)

    def body(q_ref, k_ref, v_ref, zg_ref, o_ref, lse_ref, dy_ref, g_ref, sl_ref, dp_in, dp_ref, dg_ref,
             do_n, dt_n, dq_n, dk_n, dv_n, qp, kp, vp, dop, dtp, lsep, pdq, pdk, pdv):
        del dp_in
        dq_ref, dk_ref, dv_ref, dzg_ref = (dp_ref.at[i] for i in range(4))
        slope = sl_ref[...][:, :1]
        g = g_ref[...]

        def prologue(t, dg):
            rows = pl.ds(pl.multiple_of(t * n, n), n)
            o = o_ref[rows, :]
            do, dzg, dgi = _head_out_bwd(o, zg_ref[rows, :], g, dy_ref[rows, :])
            dzg_ref[rows, :] = dzg.astype(BF16)
            do_n[rows, :] = do
            dt_n[rows, :] = jnp.broadcast_to(jnp.sum(do * o, axis=-1, keepdims=True), (n, HEAD_DIM))
            return dg + dgi

        dg_ref[...] = lax.fori_loop(0, nt, prologue, jnp.zeros((1, HEAD_DIM), F32))
        dq_n[...] = jnp.zeros_like(dq_n)
        dk_n[...] = jnp.zeros_like(dk_n)
        dv_n[...] = jnp.zeros_like(dv_n)
        kp[pl.ds(0, n), :] = jnp.zeros((n, HEAD_DIM), BF16)
        vp[pl.ds(0, n), :] = jnp.zeros((n, HEAD_DIM), BF16)

        for (window, r) in DIL_PAIRS:
            nb = (s // r) // n
            masks = _dil_masks(slope, float(r))
            for nat, per, padded in _dil_chunks(s, r):
                qp[per, :] = q_ref[nat, :].astype(BF16)
                kp[padded, :] = k_ref[nat, :].astype(BF16)
                vp[padded, :] = v_ref[nat, :].astype(BF16)
                dop[per, :] = do_n[nat, :].astype(BF16)
                dtp[per, :] = dt_n[nat, :]
                lsep[per, :] = lse_ref[nat, :]
            pdk[...] = jnp.zeros_like(pdk)
            pdv[...] = jnp.zeros_like(pdv)

            def tiles(tt, _):
                ts = [tt * DIL_GROUP + i for i in range(DIL_GROUP)]
                rows = [pl.ds(pl.multiple_of(t * n, n), n) for t in ts]
                both = [pl.ds(pl.multiple_of(t * n, n), 2 * n) for t in ts]
                q = [qp[rw, :] for rw in rows]
                do = [dop[rw, :] for rw in rows]
                sc = [_dil_scores(qq, kp[bt, :], masks, lax.rem(t, nb) == 0) for t, qq, bt in zip(ts, q, both)]
                dp = [_dot(dd, vp[bt, :], NT) for dd, bt in zip(do, both)]
                p = [jnp.exp(x - lsep[rw, :][:, :1]) for x, rw in zip(sc, rows)]
                ds = [((x * (y - dtp[rw, :][:, :1])) * scale).astype(BF16) for x, y, rw in zip(p, dp, rows)]
                dq = [_dot(x, kp[bt, :], NN) for x, bt in zip(ds, both)]
                dk = [_dot(x, qq, TN) for x, qq in zip(ds, q)]
                dv = [_dot(x.astype(BF16), dd, TN) for x, dd in zip(p, do)]
                for rw, bt, x, y, z in zip(rows, both, dq, dk, dv):
                    pdq[rw, :] = x
                    pdk[bt, :] += y
                    pdv[bt, :] += z
                return 0

            lax.fori_loop(0, nt // DIL_GROUP, tiles, 0)
            for nat, per, padded in _dil_chunks(s, r):
                dq_n[nat, :] += pdq[per, :]
                dk_n[nat, :] += pdk[padded, :]
                dv_n[nat, :] += pdv[padded, :]

        dq_ref[...] = dq_n[...].astype(BF16)
        dk_ref[...] = dk_n[...].astype(BF16)
        dv_ref[...] = dv_n[...].astype(BF16)

    f32_s = pltpu.VMEM((s, HEAD_DIM), F32)
    f32_pad = pltpu.VMEM((s + n, HEAD_DIM), F32)
    bf_s = pltpu.VMEM((s, HEAD_DIM), BF16)
    bf_pad = pltpu.VMEM((s + n, HEAD_DIM), BF16)
    return pl.pallas_call(
        body, name="dil_bwd", grid=(b, n_heads),
        in_specs=[_seg_spec(s, 4), _seg_spec(s, 5), _seg_spec(s, 6), _seg_spec(s, 7),
                  _head_spec(s), _head_spec(s), _seg_spec(s, 1),
                  pl.BlockSpec((1, HEAD_DIM), lambda i, h: (0, h)),
                  pl.BlockSpec((None, 1, HEAD_DIM), lambda i, h: (h, 0, 0)), ANY],
        out_specs=[_seg4_spec(s, 1), pl.BlockSpec((None, 1, HEAD_DIM), lambda i, h: (i, 0, h))],
        out_shape=[jax.ShapeDtypeStruct((8, b, s, w), BF16), jax.ShapeDtypeStruct((b, 1, w), F32)],
        scratch_shapes=[f32_s] * 5 + [bf_s, bf_pad, bf_pad, bf_s] + [f32_s, f32_s, f32_s, f32_pad, f32_pad],
        input_output_aliases={9: 0},
        compiler_params=_cparams(("parallel", "parallel")),
    )(proj8, proj8, proj8, proj8, o_dl, lse_dl, dy2, g_dil, slopes, dproj8)


def _small_update(gathered, n_b, params, m, v):
    n_dev, _, width = gathered.shape

    def body(g_ref, p_ref, m_ref, v_ref, grad_ref, d_ref, nm_ref, nv_ref, loss_ref):
        for row in range(2):
            acc = None
            for dev in range(n_dev):
                for i in range(n_b):
                    term = g_ref[dev, pl.ds(row * n_b + i, 1), :]
                    acc = term if acc is None else acc + term
            grad_ref[pl.ds(row, 1), :] = acc
        loss = g_ref[0, pl.ds(2 * n_b, 1), pl.ds(0, 128)]
        for dev in range(1, n_dev):
            loss = loss + g_ref[dev, pl.ds(2 * n_b, 1), pl.ds(0, 128)]
        loss_ref[...] = loss
        d, nm, nv = _adamw_math(p_ref[...], grad_ref[...], m_ref[...], v_ref[...])
        d_ref[...] = d
        nm_ref[...] = nm
        nv_ref[...] = nv

    sds = jax.ShapeDtypeStruct((2, width), F32)
    return pl.pallas_call(
        body, name="small_update",
        in_specs=[VMEM_SPEC] * 4, out_specs=[VMEM_SPEC] * 5,
        out_shape=[sds, sds, sds, sds, jax.ShapeDtypeStruct((1, 128), F32)],
        compiler_params=_cparams(),
    )(gathered, params, m, v)


def _wada_update(c_t, dmod, w, m, v):
    d, bt = c_t.shape
    _, n = dmod.shape
    tr, tc = _tile(d, 512), _tile(n, 1024)

    def body(c_ref, dm_ref, w_ref, m_ref, v_ref, g_ref, d_ref, nm_ref, nv_ref):
        cv = c_ref[...]
        cs = (cv * _sigmoid(cv)).astype(BF16)
        grad = _dot(cs, dm_ref[...].astype(BF16), NN)
        g_ref[...] = grad
        dl, nm, nv = _adamw_math(w_ref[...], grad, m_ref[...], v_ref[...])
        d_ref[...] = dl
        nm_ref[...] = nm
        nv_ref[...] = nv

    spec = pl.BlockSpec((tr, tc), lambda i, j: (i, j))
    sds = jax.ShapeDtypeStruct((d, n), F32)
    return pl.pallas_call(
        body, name="wada_update", grid=(d // tr, n // tc),
        in_specs=[pl.BlockSpec((tr, bt), lambda i, j: (i, 0)), pl.BlockSpec((bt, tc), lambda i, j: (0, j)),
                  spec, spec, spec],
        out_specs=[spec] * 4, out_shape=[sds] * 4,
        compiler_params=_cparams(("parallel", "parallel")),
    )(c_t, dmod, w, m, v)


def _reduce_begin(gs, tag):
    ra = _sibling_half_swap(gs, "swap_" + tag)
    pa, own = _pair_sum(gs, ra, "pair_sum_" + tag)
    return _ScatterChips(pa), own


def _reduce_finish(rb, own, w, m, v, tag):
    half = _chip_sum(rb, own, "chip_sum_" + tag)
    grad = _sibling_join(half, "join_" + tag)
    delta, nm, nv = _adamw(w, grad, m, v, "adamw_" + tag)
    return grad, delta, nm, nv


def kernel(x, c, w_ada, b_ada, g_norm, w_in, g_sb, g_dil, w_out, g_final, loss_target, m_w_ada, m_b_ada, m_g_norm, m_w_in, m_g_sb, m_g_dil, m_w_out, m_g_final, v_w_ada, v_b_ada, v_g_norm, v_w_in, v_g_sb, v_g_dil, v_w_out, v_g_final):
    nb, s, d = x.shape
    t = nb * s
    na = w_ada.shape[2]
    cs = w_in.shape[2]
    w = cs // 2
    n_heads = w // HEAD_DIM
    r_out = w_out.shape[1]
    assert 2 * nb + 1 <= 8 and 2 * d + 2 * w <= 3 * d and N_CHIPS * na == 3 * d and N_CHIPS * r_out == 2 * w
    xi, yi, ci = _place()
    chip = 2 * xi + yi
    dev = 2 * chip + ci

    c_all = _allgather8(jnp.pad(c, ((0, 8 - nb), (0, 0))), "gather_c")
    c16 = c_all.reshape(N_DEV, 8, d)[:, :nb].reshape(N_DEV * nb, d)
    b_ada_shard = lax.dynamic_slice(b_ada, (0, chip * na), (1, na))
    mod_part = _mod_fwd(c16, w_ada[0], b_ada_shard)
    mod_all = _allgather8(mod_part, "gather_mod")
    mod_full = mod_all.reshape(N_CHIPS, 2, N_DEV * nb, na)[:, 0].transpose(1, 0, 2).reshape(N_DEV * nb, 3 * d)
    mod = lax.dynamic_slice(mod_full, (dev * nb, 0), (nb, 3 * d))
    shift, scale, gate = (mod[:, i * d:(i + 1) * d].reshape(nb, 1, d) for i in range(3))

    h = _norm_mod_fwd(x, g_norm, scale, shift)
    h2 = h.reshape(t, d)
    ws_in = _allgather_chips(_cast_bf16_slab(w_in[0], "cast_w_in"), "gather_w_in")
    ws_out = _allgather_chips(_cast_bf16_slab(w_out[0], "cast_w_out"), "gather_w_out")
    w_out_full = ws_out.reshape(2 * w, d)

    proj8 = _proj_fwd(h2, ws_in).reshape(8, nb, s, w)
    slopes = jnp.exp2(-ALIBI_MAX_BIAS * jnp.arange(1, n_heads + 1, dtype=F32) / n_heads)
    slopes = jnp.broadcast_to(slopes[:, None, None], (n_heads, 1, HEAD_DIM))
    o_sb, tot_sb, y2 = _sb_fwd(proj8, g_sb)
    o_dl, lse_dl, y2 = _dil_fwd(proj8, g_dil, slopes, y2)
    y2f = y2.reshape(2, t, w)
    out = _out_fwd(y2f, w_out_full)

    dx1, dout, dgate, dg_final, loss_part = _loss_head(
        x, out.reshape(nb, s, d), gate, g_final.reshape(1, d), loss_target)
    dout2 = dout.reshape(t, d)
    gs_out = _out_bwd_w(y2f, dout2).reshape(N_CHIPS, r_out, d)
    scatter_out, own_out = _reduce_begin(gs_out, "w_out")
    dy2 = _out_bwd_y(dout2, w_out_full).reshape(2, nb, s, w)
    dproj8, dg_sb, rb_out = _sb_bwd(proj8, o_sb, tot_sb, dy2, g_sb, fused=scatter_out)
    dproj8, dg_dl = _dil_bwd(proj8, o_dl, lse_dl, dy2, g_dil, slopes, dproj8)
    dproj8 = dproj8.reshape(8, t, w)
    gs_in = _proj_bwd_w(h2, dproj8)
    scatter_in, own_in = _reduce_begin(gs_in, "w_in")
    dh, rb_in = _proj_bwd_x(dproj8, ws_in, fused=scatter_in)
    grad_x, dshift, dscale, dg_norm = _norm_mod_bwd(x, dh.reshape(nb, s, d), dx1, g_norm, scale)

    width = 3 * d
    dmod = jnp.concatenate([dshift, dscale, dgate], axis=-1).reshape(nb, width)
    gains = jnp.concatenate([dg_sb.reshape(nb, w), dg_dl.reshape(nb, w)], axis=-1)
    gains = jnp.pad(gains, ((0, 0), (2 * d, width - 2 * d - 2 * w)))
    first = jnp.pad(jnp.concatenate([dg_norm, dg_final], axis=-1), ((0, nb - 1), (0, width - 2 * d)))
    loss_row = jnp.pad(loss_part, ((0, 0), (0, width - 128)))
    pack = jnp.concatenate([dmod, gains + first, loss_row, jnp.zeros((8 - 2 * nb - 1, width), F32)], axis=0)
    gathered = _allgather8(pack, "gather_small").reshape(N_DEV, 8, width)

    def stack(bias, gn, gf, gsb, gdl):
        row1 = jnp.concatenate([gn.reshape(1, d), gf.reshape(1, d), gsb.reshape(1, w), gdl.reshape(1, w)], axis=-1)
        return jnp.concatenate([bias.reshape(1, width), jnp.pad(row1, ((0, 0), (0, width - 2 * d - 2 * w)))], axis=0)

    small = _small_update(
        gathered, nb, stack(b_ada, g_norm, g_final, g_sb, g_dil),
        stack(m_b_ada, m_g_norm, m_g_final, m_g_sb, m_g_dil), stack(v_b_ada, v_g_norm, v_g_final, v_g_sb, v_g_dil))
    loss = small[4][0, 0]

    def unstack(a):
        return (a[0:1, :], a[1:2, 0:d], a[1, d:2 * d], a[1:2, 2 * d:2 * d + w], a[1:2, 2 * d + w:2 * d + 2 * w])

    (g_b, g_gn, g_gf, g_gsb, g_gdl), (d_b, d_gn, d_gf, d_gsb, d_gdl), (nm_b, nm_gn, nm_gf, nm_gsb, nm_gdl), \
        (nv_b, nv_gn, nv_gf, nv_gsb, nv_gdl) = (unstack(a) for a in small[:4])

    dmod_all = gathered[:, :nb].reshape(N_DEV * nb, width)
    dmod_cols = lax.dynamic_slice(dmod_all, (0, chip * na), (N_DEV * nb, na))
    g_wa, d_wa, nm_wa, nv_wa = _wada_update(c16.T, dmod_cols, w_ada[0], m_w_ada[0], v_w_ada[0])

    g_wi, d_wi, nm_wi, nv_wi = _reduce_finish(rb_in, own_in, w_in[0], m_w_in[0], v_w_in[0], "w_in")
    g_wo, d_wo, nm_wo, nv_wo = _reduce_finish(rb_out, own_out, w_out[0], m_w_out[0], v_w_out[0], "w_out")

    lead = lambda a: a[None]
    return (loss, grad_x,
            lead(g_wa), g_b, g_gn, lead(g_wi), g_gsb, g_gdl, lead(g_wo), g_gf,
            lead(d_wa), d_b, d_gn, lead(d_wi), d_gsb, d_gdl, lead(d_wo), d_gf,
            lead(nm_wa), nm_b, nm_gn, lead(nm_wi), nm_gsb, nm_gdl, lead(nm_wo), nm_gf,
            lead(nv_wa), nv_b, nv_gn, lead(nv_wi), nv_gsb, nv_gdl, lead(nv_wo), nv_gf)
```

```python
import functools
import math

import jax
import jax.numpy as jnp
from jax import lax
from jax.experimental import pallas as pl
from jax.experimental.pallas import tpu as pltpu

F32 = jnp.float32
BF16 = jnp.bfloat16
MESH = pl.DeviceIdType.MESH

HEAD_DIM = 128
EPS = 1e-6
DIL_PAIRS = ((128, 1), (512, 4), (2048, 16))
ALIBI_MAX_BIAS = 8.0
ADAM_LR = 0.001
ADAM_B1 = 0.9
ADAM_B2 = 0.999
ADAM_EPS = 1e-08
ADAM_WD = 0.01
ADAM_STEP = 10
N_CHIPS = 4
N_DEV = 8
VMEM_LIMIT_BYTES = 56 * 1024 * 1024
NEG_BIG = -1e30

NN = (((1,), (0,)), ((), ()))
NT = (((1,), (1,)), ((), ()))
TN = (((0,), (0,)), ((), ()))

ANY = pl.BlockSpec(memory_space=pl.ANY)
VMEM_SPEC = pl.BlockSpec(memory_space=pltpu.VMEM)


def _cparams(sem=None):
    return pltpu.CompilerParams(dimension_semantics=sem, vmem_limit_bytes=VMEM_LIMIT_BYTES)


def _tile(dim, pref):
    t = min(dim, pref)
    assert dim % t == 0, (dim, pref)
    return t


def _dot(a, b, dims):
    return lax.dot_general(a, b, dims, preferred_element_type=F32)


def _sigmoid(x):
    return 1.0 / (1.0 + jnp.exp(-x))


def _place():
    return lax.axis_index("x"), lax.axis_index("y"), lax.axis_index("c")


def _allgather8(x_shard, name):
    m_per, n = x_shard.shape

    def body(x_ref, out_ref, send_sems, recv_sems, local_sem):
        x, y, c = _place()
        me, sibling = (x, y, c), (x, y, 1 - c)
        chips = [(1 - x, y), (x, 1 - y), (1 - x, 1 - y)]

        def rows(px, py, pc):
            return out_ref.at[pl.ds((4 * px + 2 * py + pc) * m_per, m_per), :]

        def copy(k, block, to, src=None):
            return pltpu.make_async_remote_copy(
                src_ref=rows(*block) if src is None else src, dst_ref=rows(*block),
                send_sem=send_sems.at[k], recv_sem=recv_sems.at[k], device_id=to, device_id_type=MESH)

        mine = pltpu.make_async_copy(x_ref, rows(*me), local_sem)
        mine.start()
        first = [copy(0, me, sibling, src=x_ref)]
        first += [copy(1 + j, me, (*chip, c), src=x_ref) for j, chip in enumerate(chips)]
        for cp in first:
            cp.start()
        passed = [copy(4 + j, (*chip, c), sibling) for j, chip in enumerate(chips)]
        for j, chip in enumerate(chips):
            copy(1 + j, (*chip, c), me).wait_recv()
            passed[j].start()
        copy(0, sibling, me).wait_recv()
        for j, chip in enumerate(chips):
            copy(4 + j, (*chip, 1 - c), me).wait_recv()
        for cp in first + passed:
            cp.wait_send()
        mine.wait()

    return pl.pallas_call(
        body, name=name,
        out_shape=jax.ShapeDtypeStruct((N_DEV * m_per, n), x_shard.dtype),
        in_specs=[VMEM_SPEC], out_specs=VMEM_SPEC,
        scratch_shapes=[pltpu.SemaphoreType.DMA((7,)), pltpu.SemaphoreType.DMA((7,)), pltpu.SemaphoreType.DMA],
    )(x_shard)


def _allgather_chips(ws, name):
    _, r, cdim = ws.shape
    half = r // 2

    def body(in_ref, out_ref, send_sems, recv_sems):
        del in_ref
        x, y, c = _place()
        sibling = (x, y, 1 - c)
        chips = [(1 - x, y), (x, 1 - y), (1 - x, 1 - y)]

        def copy(k, px, py, pc, to):
            rows = out_ref.at[2 * px + py, pl.ds(pc * half, half), :]
            return pltpu.make_async_remote_copy(
                src_ref=rows, dst_ref=rows, send_sem=send_sems.at[k], recv_sem=recv_sems.at[k],
                device_id=to, device_id_type=MESH)

        first = [copy(j, x, y, c, (*chip, c)) for j, chip in enumerate(chips)]
        for cp in first:
            cp.start()
        passed = [copy(3 + j, *chip, c, sibling) for j, chip in enumerate(chips)]
        for j, chip in enumerate(chips):
            copy(j, *chip, c, (x, y, c)).wait_recv()
            passed[j].start()
        for j, chip in enumerate(chips):
            copy(3 + j, *chip, 1 - c, (x, y, c)).wait_recv()
        for cp in first + passed:
            cp.wait_send()

    return pl.pallas_call(
        body, name=name,
        out_shape=jax.ShapeDtypeStruct(ws.shape, ws.dtype),
        in_specs=[ANY], out_specs=ANY, input_output_aliases={0: 0},
        scratch_shapes=[pltpu.SemaphoreType.DMA((6,)), pltpu.SemaphoreType.DMA((6,))],
    )(ws)


def _sibling_half_swap(gs, name):
    n, r, cdim = gs.shape
    half = r // 2

    def body(g_ref, out_ref, send_sem, recv_sem):
        x, y, c = _place()
        cp = pltpu.make_async_remote_copy(
            src_ref=g_ref.at[:, pl.ds((1 - c) * half, half), :], dst_ref=out_ref,
            send_sem=send_sem, recv_sem=recv_sem, device_id=(x, y, 1 - c), device_id_type=MESH)
        cp.start()
        cp.wait()

    return pl.pallas_call(
        body, name=name,
        out_shape=jax.ShapeDtypeStruct((n, half, cdim), gs.dtype),
        in_specs=[ANY], out_specs=ANY,
        scratch_shapes=[pltpu.SemaphoreType.DMA, pltpu.SemaphoreType.DMA],
    )(gs)


class _ScatterChips:
    def __init__(self, pa):
        self.inputs = [pa]
        self.out_shapes = [jax.ShapeDtypeStruct(pa.shape, pa.dtype)]
        self.scratch = [pltpu.SemaphoreType.DMA((3,)), pltpu.SemaphoreType.DMA((3,)), pltpu.SemaphoreType.DMA]

    @staticmethod
    def _mine(p_ref, out_ref, send_sems, recv_sems, local_sem):
        x, y, _ = _place()
        return pltpu.make_async_copy(p_ref.at[2 * x + y], out_ref.at[2 * x + y], local_sem)

    @staticmethod
    def _remote(p_ref, out_ref, send_sems, recv_sems, local_sem, incoming):
        x, y, c = _place()
        me = 2 * x + y
        remote = []
        for j, (px, py) in enumerate([(1 - x, y), (x, 1 - y), (1 - x, 1 - y)]):
            remote.append(pltpu.make_async_remote_copy(
                src_ref=p_ref.at[me if incoming else 2 * px + py], dst_ref=out_ref.at[2 * px + py if incoming else me],
                send_sem=send_sems.at[j], recv_sem=recv_sems.at[j], device_id=(px, py, c), device_id_type=MESH))
        return remote

    def start(self, *refs):
        self._mine(*refs).start()
        for cp in self._remote(*refs, incoming=False):
            cp.start()

    def wait(self, *refs):
        for cp in self._remote(*refs, incoming=True):
            cp.wait_recv()
        for cp in self._remote(*refs, incoming=False):
            cp.wait_send()
        self._mine(*refs).wait()


def _fused_specs(fused):
    if fused is None:
        return [], [], [], [], []
    return (list(fused.inputs), [ANY] * len(fused.inputs), list(fused.out_shapes), [ANY] * len(fused.out_shapes),
            list(fused.scratch))


def _fused_begin(fused, grid, refs):
    if fused is not None:
        first = functools.reduce(lambda p, q: p & q, [pl.program_id(i) == 0 for i in range(len(grid))])
        pl.when(first)(lambda: fused.start(*refs))


def _fused_end(fused, grid, refs):
    if fused is not None:
        last = functools.reduce(lambda p, q: p & q, [pl.program_id(i) == g - 1 for i, g in enumerate(grid)])
        pl.when(last)(lambda: fused.wait(*refs))


def _sibling_join(full, name):
    h2, cdim = full.shape
    h = h2 // 2

    def body(in_ref, out_ref, send_sem, recv_sem):
        del in_ref
        x, y, c = _place()
        mine = out_ref.at[pl.ds(c * h, h), :]
        cp = pltpu.make_async_remote_copy(
            src_ref=mine, dst_ref=mine, send_sem=send_sem, recv_sem=recv_sem,
            device_id=(x, y, 1 - c), device_id_type=MESH)
        cp.start()
        theirs = out_ref.at[pl.ds((1 - c) * h, h), :]
        pltpu.make_async_remote_copy(
            src_ref=theirs, dst_ref=theirs, send_sem=send_sem, recv_sem=recv_sem,
            device_id=(x, y, 1 - c), device_id_type=MESH).wait_recv()
        cp.wait_send()

    return pl.pallas_call(
        body, name=name,
        out_shape=jax.ShapeDtypeStruct(full.shape, full.dtype),
        in_specs=[ANY], out_specs=ANY, input_output_aliases={0: 0},
        scratch_shapes=[pltpu.SemaphoreType.DMA, pltpu.SemaphoreType.DMA],
    )(full)


def _cast_bf16_slab(w, name):
    r, cdim = w.shape
    tr, tc = _tile(r, 512), _tile(cdim, 2048)

    def body(pc_ref, w_ref, o_ref):
        o_ref[...] = w_ref[...].astype(BF16)

    return pl.pallas_call(
        body, name=name,
        grid_spec=pltpu.PrefetchScalarGridSpec(
            num_scalar_prefetch=1, grid=(r // tr, cdim // tc),
            in_specs=[pl.BlockSpec((tr, tc), lambda i, j, pc: (i, j))],
            out_specs=pl.BlockSpec((None, tr, tc), lambda i, j, pc: (pc[1], i, j))),
        out_shape=jax.ShapeDtypeStruct((N_CHIPS, r, cdim), BF16),
        compiler_params=_cparams(("parallel", "parallel")),
    )(_place_scalars(), w)


def _place_scalars():
    x, y, c = _place()
    return jnp.stack([c, 2 * x + y]).astype(jnp.int32)


def _pair_sum(gs, ra, name):
    n, r, cdim = gs.shape
    half = r // 2
    tr, tc = _tile(half, 512), _tile(cdim, 2048)
    nt = half // tr

    def body(pc_ref, g_ref, r_ref, o_ref, own_ref):
        val = g_ref[...] + r_ref[...]
        o_ref[...] = val.astype(BF16)

        @pl.when(pl.program_id(2) == pc_ref[1])
        def _():
            own_ref[...] = val

    return pl.pallas_call(
        body, name=name,
        grid_spec=pltpu.PrefetchScalarGridSpec(
            num_scalar_prefetch=1, grid=(nt, cdim // tc, n),
            in_specs=[pl.BlockSpec((None, tr, tc), lambda i, j, s, pc: (s, pc[0] * nt + i, j)),
                      pl.BlockSpec((None, tr, tc), lambda i, j, s, pc: (s, i, j))],
            out_specs=[pl.BlockSpec((None, tr, tc), lambda i, j, s, pc: (s, i, j)),
                       pl.BlockSpec((tr, tc), lambda i, j, s, pc: (i, j))]),
        out_shape=[jax.ShapeDtypeStruct((n, half, cdim), BF16), jax.ShapeDtypeStruct((half, cdim), F32)],
        compiler_params=_cparams(("parallel", "parallel", "arbitrary")),
    )(_place_scalars(), gs, ra)


def _chip_sum(rb, own, name):
    n, h, cdim = rb.shape
    tr, tc = _tile(h, 256), _tile(cdim, 2048)
    nt = h // tr

    def body(pc_ref, r_ref, own_ref, o_ref):
        chip = pc_ref[1]
        acc = None
        for p in range(n):
            term = jnp.where(chip == p, own_ref[...], r_ref[p].astype(F32))
            acc = term if acc is None else acc + term
        o_ref[...] = acc

    return pl.pallas_call(
        body, name=name,
        grid_spec=pltpu.PrefetchScalarGridSpec(
            num_scalar_prefetch=1, grid=(nt, cdim // tc),
            in_specs=[pl.BlockSpec((n, tr, tc), lambda i, j, pc: (0, i, j)),
                      pl.BlockSpec((tr, tc), lambda i, j, pc: (i, j))],
            out_specs=pl.BlockSpec((tr, tc), lambda i, j, pc: (pc[0] * nt + i, j))),
        out_shape=jax.ShapeDtypeStruct((2 * h, cdim), F32),
        compiler_params=_cparams(("parallel", "parallel")),
    )(_place_scalars(), rb, own)


def _adamw_math(w, g, m, v):
    m = ADAM_B1 * m + (1.0 - ADAM_B1) * g
    v = ADAM_B2 * v + (1.0 - ADAM_B2) * (g * g)
    m_hat = m / (1.0 - ADAM_B1 ** ADAM_STEP)
    v_hat = v / (1.0 - ADAM_B2 ** ADAM_STEP)
    delta = -ADAM_LR * (m_hat / (jnp.sqrt(v_hat) + ADAM_EPS) + ADAM_WD * w)
    return delta, m, v


def _adamw(w, g, m, v, name):
    r, cdim = w.shape
    tr, tc = _tile(r, 256), _tile(cdim, 2048)

    def body(w_ref, g_ref, m_ref, v_ref, d_ref, nm_ref, nv_ref):
        d, nm, nv = _adamw_math(w_ref[...], g_ref[...], m_ref[...], v_ref[...])
        d_ref[...] = d
        nm_ref[...] = nm
        nv_ref[...] = nv

    spec = pl.BlockSpec((tr, tc), lambda i, j: (i, j))
    sds = jax.ShapeDtypeStruct((r, cdim), F32)
    return pl.pallas_call(
        body, name=name, grid=(r // tr, cdim // tc),
        in_specs=[spec] * 4, out_specs=[spec] * 3, out_shape=[sds] * 3,
        compiler_params=_cparams(("parallel", "parallel")),
    )(w, g, m, v)


def _matmul(a, b, *, grid, a_spec, b_spec, out_spec, out_shape, acc_shape, dims, name, bias=None, bias_spec=None,
            silu_a=False, fused=None):
    nk = grid[2]
    f_in, f_in_specs, f_out, f_out_specs, f_scratch = _fused_specs(fused)
    n_in = 2 + (bias is not None)

    acc_scratch = [pltpu.VMEM(acc_shape, F32)] if nk > 1 else []

    def body(*refs):
        a_ref, b_ref = refs[:2]
        bias_ref = refs[2] if bias is not None else None
        o_ref = refs[n_in + len(f_in)]
        n_fixed = n_in + len(f_in) + 1 + len(f_out)
        f_refs = (*refs[n_in:n_in + len(f_in)], *refs[n_in + len(f_in) + 1:n_fixed],
                  *refs[n_fixed + len(acc_scratch):])
        _fused_begin(fused, grid, f_refs)

        def product():
            av = a_ref[...]
            if silu_a:
                av = av * _sigmoid(av)
            return _dot(av.astype(BF16), b_ref[...].astype(BF16), dims)

        def finish(res):
            if bias is not None:
                res = res + bias_ref[...]
            o_ref[...] = res.astype(o_ref.dtype)

        if nk == 1:
            finish(product())
        else:
            acc_ref = refs[n_fixed]
            k = pl.program_id(2)

            @pl.when(k == 0)
            def _():
                acc_ref[...] = product()

            if nk > 2:
                @pl.when((k > 0) & (k < nk - 1))
                def _():
                    acc_ref[...] += product()

            @pl.when(k == nk - 1)
            def _():
                finish(acc_ref[...] + product())

        _fused_end(fused, grid, f_refs)

    in_specs = [a_spec, b_spec] + ([] if bias is None else [bias_spec]) + f_in_specs
    args = (a, b) + (() if bias is None else (bias,)) + tuple(f_in)
    sem = ("parallel", "parallel", "arbitrary") if fused is None else ("arbitrary",) * 3
    res = pl.pallas_call(
        body, name=name, grid=grid, in_specs=in_specs, out_specs=[out_spec] + f_out_specs,
        out_shape=[out_shape] + f_out,
        scratch_shapes=acc_scratch + f_scratch,
        compiler_params=_cparams(sem),
    )(*args)
    return res[0] if fused is None else tuple(res)


def _mm_tiles(m, n, k):
    return _tile(m, 1024), _tile(n, 1024), _tile(k, 4096)


def _proj_fwd(h2, ws_in):
    t, d = h2.shape
    _, _, cs = ws_in.shape
    w = cs // 2
    tm, tn, tk = _mm_tiles(t, w, d)
    nps, npseg = cs // tn, w // tn
    return _matmul(
        h2, ws_in, grid=(t // tm, 8 * npseg, d // tk), dims=NN, name="proj_fwd",
        a_spec=pl.BlockSpec((tm, tk), lambda m, n, k: (m, k)),
        b_spec=pl.BlockSpec((None, tk, tn), lambda m, n, k: (n // nps, k, n % nps)),
        out_spec=pl.BlockSpec((None, tm, tn), lambda m, n, k: (n // npseg, m, n % npseg)),
        out_shape=jax.ShapeDtypeStruct((8, t, w), F32), acc_shape=(tm, tn))


def _proj_bwd_x(dproj8, ws_in, fused=None):
    _, t, w = dproj8.shape
    _, d, cs = ws_in.shape
    tm, tn, tk = _mm_tiles(t, d, w)
    kps, kpseg = cs // tk, w // tk
    return _matmul(
        dproj8, ws_in, grid=(t // tm, d // tn, 8 * kpseg), dims=NT, name="proj_bwd_x", fused=fused,
        a_spec=pl.BlockSpec((None, tm, tk), lambda m, n, k: (k // kpseg, m, k % kpseg)),
        b_spec=pl.BlockSpec((None, tn, tk), lambda m, n, k: (k // kps, n, k % kps)),
        out_spec=pl.BlockSpec((tm, tn), lambda m, n, k: (m, n)),
        out_shape=jax.ShapeDtypeStruct((t, d), F32), acc_shape=(tm, tn))


def _proj_bwd_w(h2, dproj8):
    t, d = h2.shape
    _, _, w = dproj8.shape
    cs = 2 * w
    tm, tn, tk = _mm_tiles(d, w, t)
    nps, npseg = cs // tn, w // tn
    return _matmul(
        h2, dproj8, grid=(d // tm, 8 * npseg, t // tk), dims=TN, name="proj_bwd_w",
        a_spec=pl.BlockSpec((tk, tm), lambda m, n, k: (k, m)),
        b_spec=pl.BlockSpec((None, tk, tn), lambda m, n, k: (n // npseg, k, n % npseg)),
        out_spec=pl.BlockSpec((None, tm, tn), lambda m, n, k: (n // nps, m, n % nps)),
        out_shape=jax.ShapeDtypeStruct((N_CHIPS, d, cs), F32), acc_shape=(tm, tn))


def _out_fwd(y2, w_out):
    _, t, w = y2.shape
    _, d = w_out.shape
    tm, tn, tk = _mm_tiles(t, d, w)
    kpg = w // tk
    return _matmul(
        y2, w_out, grid=(t // tm, d // tn, 2 * kpg), dims=NN, name="out_fwd",
        a_spec=pl.BlockSpec((None, tm, tk), lambda m, n, k: (k // kpg, m, k % kpg)),
        b_spec=pl.BlockSpec((tk, tn), lambda m, n, k: (k, n)),
        out_spec=pl.BlockSpec((tm, tn), lambda m, n, k: (m, n)),
        out_shape=jax.ShapeDtypeStruct((t, d), F32), acc_shape=(tm, tn))


def _out_bwd_y(dout, w_out):
    t, d = dout.shape
    w = w_out.shape[0] // 2
    tm, tn, tk = _mm_tiles(t, w, d)
    npg = w // tn
    return _matmul(
        dout, w_out, grid=(t // tm, 2 * npg, d // tk), dims=NT, name="out_bwd_y",
        a_spec=pl.BlockSpec((tm, tk), lambda m, n, k: (m, k)),
        b_spec=pl.BlockSpec((tn, tk), lambda m, n, k: (n, k)),
        out_spec=pl.BlockSpec((None, tm, tn), lambda m, n, k: (n // npg, m, n % npg)),
        out_shape=jax.ShapeDtypeStruct((2, t, w), F32), acc_shape=(tm, tn))


def _out_bwd_w(y2, dout):
    _, t, w = y2.shape
    _, d = dout.shape
    tm, tn, tk = _mm_tiles(w, d, t)
    mpg = w // tm
    return _matmul(
        y2, dout, grid=(2 * mpg, d // tn, t // tk), dims=TN, name="out_bwd_w",
        a_spec=pl.BlockSpec((None, tk, tm), lambda m, n, k: (m // mpg, k, m % mpg)),
        b_spec=pl.BlockSpec((tk, tn), lambda m, n, k: (k, n)),
        out_spec=pl.BlockSpec((tm, tn), lambda m, n, k: (m, n)),
        out_shape=jax.ShapeDtypeStruct((2 * w, d), F32), acc_shape=(tm, tn))


def _mod_fwd(c_all, w_ada, b_ada):
    bt, d = c_all.shape
    _, n = w_ada.shape
    tn, tk = _tile(n, 512), _tile(d, 1024)
    return _matmul(
        c_all, w_ada, grid=(1, n // tn, d // tk), dims=NN, name="mod_fwd", silu_a=True,
        a_spec=pl.BlockSpec((bt, tk), lambda i, j, l: (0, l)),
        b_spec=pl.BlockSpec((tk, tn), lambda i, j, l: (l, j)),
        bias=b_ada, bias_spec=pl.BlockSpec((1, tn), lambda i, j, l: (0, j)),
        out_spec=pl.BlockSpec((bt, tn), lambda i, j, l: (0, j)),
        out_shape=jax.ShapeDtypeStruct((bt, n), F32), acc_shape=(bt, tn))


def _norm_mod_fwd(x, g_norm, scale, shift):
    b, s, d = x.shape
    ts = _tile(s, 256)

    def body(x_ref, g_ref, sc_ref, sh_ref, h_ref):
        xv = x_ref[...]
        r = lax.rsqrt(jnp.mean(xv * xv, axis=-1, keepdims=True) + EPS)
        y = (xv * r) * g_ref[...]
        h_ref[...] = (y * (1.0 + sc_ref[...]) + sh_ref[...]).astype(BF16)

    row = pl.BlockSpec((None, ts, d), lambda i, j: (i, j, 0))
    per_b = pl.BlockSpec((None, 1, d), lambda i, j: (i, 0, 0))
    return pl.pallas_call(
        body, name="norm_mod_fwd", grid=(b, s // ts),
        in_specs=[row, pl.BlockSpec((1, d), lambda i, j: (0, 0)), per_b, per_b],
        out_specs=row, out_shape=jax.ShapeDtypeStruct((b, s, d), BF16),
        compiler_params=_cparams(("parallel", "parallel")),
    )(x, g_norm, scale, shift)


def _norm_mod_bwd(x, dh, dx1, g_norm, scale):
    b, s, d = x.shape
    ts = _tile(s, 256)

    def body(x_ref, dh_ref, dx1_ref, g_ref, sc_ref, gx_ref, dsh_ref, dsc_ref, dg_ref):
        i, j = pl.program_id(0), pl.program_id(1)

        @pl.when(j == 0)
        def _():
            dsh_ref[...] = jnp.zeros_like(dsh_ref)
            dsc_ref[...] = jnp.zeros_like(dsc_ref)

        @pl.when((i == 0) & (j == 0))
        def _():
            dg_ref[...] = jnp.zeros_like(dg_ref)

        xv, dhv, g = x_ref[...], dh_ref[...], g_ref[...]
        r = lax.rsqrt(jnp.mean(xv * xv, axis=-1, keepdims=True) + EPS)
        xh = xv * r
        dsh_ref[...] += jnp.sum(dhv, axis=0, keepdims=True)
        dsc_ref[...] += jnp.sum(dhv * (xh * g), axis=0, keepdims=True)
        dn = dhv * (1.0 + sc_ref[...])
        dg_ref[...] += jnp.sum(dn * xh, axis=0, keepdims=True)
        u = dn * g
        dx = r * u - xv * (r * r * r) * jnp.mean(u * xv, axis=-1, keepdims=True)
        gx_ref[...] = dx1_ref[...] + dx

    row = pl.BlockSpec((None, ts, d), lambda i, j: (i, j, 0))
    per_b = pl.BlockSpec((None, 1, d), lambda i, j: (i, 0, 0))
    vec = pl.BlockSpec((1, d), lambda i, j: (0, 0))
    return pl.pallas_call(
        body, name="norm_mod_bwd", grid=(b, s // ts),
        in_specs=[row, row, row, vec, per_b],
        out_specs=[row, per_b, per_b, vec],
        out_shape=[jax.ShapeDtypeStruct((b, s, d), F32), jax.ShapeDtypeStruct((b, 1, d), F32),
                   jax.ShapeDtypeStruct((b, 1, d), F32), jax.ShapeDtypeStruct((1, d), F32)],
        compiler_params=_cparams(("arbitrary", "arbitrary")),
    )(x, dh, dx1, g_norm, scale)


def _loss_head(x, out, gate, g_final, target):
    b, s, d = x.shape
    ts = _tile(s, 256)

    def body(x_ref, o_ref, gt_ref, g_ref, t_ref, dx1_ref, dout_ref, dgt_ref, dg_ref, loss_ref):
        i, j = pl.program_id(0), pl.program_id(1)

        @pl.when(j == 0)
        def _():
            dgt_ref[...] = jnp.zeros_like(dgt_ref)

        @pl.when((i == 0) & (j == 0))
        def _():
            dg_ref[...] = jnp.zeros_like(dg_ref)
            loss_ref[...] = jnp.zeros_like(loss_ref)

        ov, gt, g = o_ref[...], gt_ref[...], g_ref[...]
        x1 = x_ref[...] + gt * ov
        r = lax.rsqrt(jnp.mean(x1 * x1, axis=-1, keepdims=True) + EPS)
        xh = x1 * r
        err = xh * g - t_ref[...]
        loss_ref[...] += 0.5 * jnp.sum(jnp.mean(err * err, axis=-1, keepdims=True))
        dfin = err * (1.0 / d)
        dg_ref[...] += jnp.sum(dfin * xh, axis=0, keepdims=True)
        u = dfin * g
        dx1 = r * u - x1 * (r * r * r) * jnp.mean(u * x1, axis=-1, keepdims=True)
        dx1_ref[...] = dx1
        dgt_ref[...] += jnp.sum(dx1 * ov, axis=0, keepdims=True)
        dout_ref[...] = (gt * dx1).astype(BF16)

    row = pl.BlockSpec((None, ts, d), lambda i, j: (i, j, 0))
    per_b = pl.BlockSpec((None, 1, d), lambda i, j: (i, 0, 0))
    vec = pl.BlockSpec((1, d), lambda i, j: (0, 0))
    return pl.pallas_call(
        body, name="loss_head", grid=(b, s // ts),
        in_specs=[row, row, per_b, vec, row],
        out_specs=[row, row, per_b, vec, pl.BlockSpec((1, 128), lambda i, j: (0, 0))],
        out_shape=[jax.ShapeDtypeStruct((b, s, d), F32), jax.ShapeDtypeStruct((b, s, d), BF16),
                   jax.ShapeDtypeStruct((b, 1, d), F32), jax.ShapeDtypeStruct((1, d), F32),
                   jax.ShapeDtypeStruct((1, 128), F32)],
        compiler_params=_cparams(("arbitrary", "arbitrary")),
    )(x, out, gate, g_final, target)


def _head_out(o, zg, g):
    rinv = lax.rsqrt(jnp.mean(o * o, axis=-1, keepdims=True) + EPS)
    return ((o * rinv) * g) * (zg * _sigmoid(zg))


def _head_out_bwd(o, zg, g, dy):
    rinv = lax.rsqrt(jnp.mean(o * o, axis=-1, keepdims=True) + EPS)
    rn = o * rinv
    sg = _sigmoid(zg)
    sil = zg * sg
    dzg = dy * (rn * g) * (sg * (1.0 + zg * (1.0 - sg)))
    dg = jnp.sum(dy * rn * sil, axis=0, keepdims=True)
    drn = dy * g * sil
    do = rinv * drn - o * (rinv * rinv * rinv) * jnp.mean(drn * o, axis=-1, keepdims=True)
    return do, dzg, dg


def _head_spec(s):
    return pl.BlockSpec((None, s, HEAD_DIM), lambda b, h: (b, 0, h))


def _seg_spec(s, seg):
    return pl.BlockSpec((None, None, s, HEAD_DIM), lambda b, h: (seg, b, 0, h))


def _seg4_spec(s, group):
    return pl.BlockSpec((4, None, s, HEAD_DIM), lambda b, h: (group, b, 0, h))


SB_Q_BLOCK = 512
SB_K_BLOCK = 256


LOG2_E = 1.4426950408889634
SB_LOGIT_SCALE = LOG2_E / math.sqrt(HEAD_DIM)


def _sb_terms(raw, valid):
    t = jnp.where(valid, raw * SB_LOGIT_SCALE, NEG_BIG)
    e = jnp.exp2(-jnp.abs(t))
    l1m = -(jnp.maximum(t, 0.0) + jnp.log2(1.0 + e))
    return t, l1m, e


def _split_dot(a, u):
    hi = a.astype(BF16)
    lo = (a - hi.astype(F32)).astype(BF16)
    return _dot(hi, u, NN) + _dot(lo, u, NN)


def _sb_fwd(proj8, g_sb):
    _, b, s, w = proj8.shape
    n_heads = w // HEAD_DIM
    tq, tk = _tile(s, SB_Q_BLOCK), _tile(s, SB_K_BLOCK)
    nq, kpq = s // tq, tq // tk
    scale = 1.0 / math.sqrt(HEAD_DIM)

    def body(q_ref, k_ref, v_ref, zg_ref, g_ref, o_ref, tot_ref, y_ref):
        u_excl = (lax.broadcasted_iota(jnp.int32, (tk, tk), 0)
                  > lax.broadcasted_iota(jnp.int32, (tk, tk), 1)).astype(BF16)
        ahead = lax.broadcasted_iota(jnp.int32, (tq, tk), 0) - lax.broadcasted_iota(jnp.int32, (tq, tk), 1)
        g = g_ref[...]

        def qblock(i, _):
            rows = pl.ds(pl.multiple_of(i * tq, tq), tq)
            q = q_ref[rows, :].astype(BF16)
            nk = (i + 1) * kpq

            def kblocks(jj, carry):
                acc, csum = carry
                js = [nk - 1 - (jj * kpq + n) for n in range(kpq)]
                cols = [pl.ds(pl.multiple_of(j * tk, tk), tk) for j in js]
                raw = [_dot(q, k_ref[c, :].astype(BF16), NT) for c in cols]
                terms = [_sb_terms(x, ahead > j * tk - i * tq) for x, j in zip(raw, js)]
                sums = [_split_dot(l1m, u_excl) for _, l1m, _ in terms]
                for (t, l1m, _), part, c in zip(terms, sums, cols):
                    a = jnp.exp2((t + l1m) + (part + csum))
                    acc = acc + _dot(a.astype(BF16), v_ref[c, :].astype(BF16), NN)
                    csum = csum + jnp.sum(l1m, axis=1, keepdims=True)
                return acc, csum

            acc, tot = lax.fori_loop(0, i + 1, kblocks, (jnp.zeros((tq, HEAD_DIM), F32), jnp.zeros((tq, 1), F32)))
            o_ref[rows, :] = acc
            tot_ref[rows, :] = jnp.broadcast_to(tot, (tq, HEAD_DIM))
            y_ref[rows, :] = _head_out(acc, zg_ref[rows, :], g).astype(BF16)
            return 0

        lax.fori_loop(0, nq, qblock, 0)

    return pl.pallas_call(
        body, name="sb_fwd", grid=(b, n_heads),
        in_specs=[_seg_spec(s, 0), _seg_spec(s, 1), _seg_spec(s, 2), _seg_spec(s, 3),
                  pl.BlockSpec((1, HEAD_DIM), lambda i, h: (0, h))],
        out_specs=[_head_spec(s), _head_spec(s), _seg_spec(s, 0)],
        out_shape=[jax.ShapeDtypeStruct((b, s, w), F32), jax.ShapeDtypeStruct((b, s, w), F32),
                   jax.ShapeDtypeStruct((2, b, s, w), BF16)],
        compiler_params=_cparams(("parallel", "parallel")),
    )(proj8, proj8, proj8, proj8, g_sb)


def _sb_bwd(proj8, o_sb, tot_sb, dy2, g_sb, fused=None):
    _, b, s, w = proj8.shape
    n_heads = w // HEAD_DIM
    tq, tk = _tile(s, SB_Q_BLOCK), _tile(s, SB_K_BLOCK)
    nq, kpq = s // tq, tq // tk
    scale = 1.0 / math.sqrt(HEAD_DIM)

    f_in, f_in_specs, f_out, f_out_specs, f_scratch = _fused_specs(fused)
    grid = (b, n_heads)

    def body(*refs):
        q_ref, k_ref, v_ref, zg_ref, o_ref, tot_ref, dy_ref, g_ref = refs[:8]
        dp_ref, dg_ref = refs[8 + len(f_in):10 + len(f_in)]
        do_s, dk_s, dv_s = refs[10 + len(f_in) + len(f_out):13 + len(f_in) + len(f_out)]
        f_refs = (*refs[8:8 + len(f_in)], *refs[10 + len(f_in):10 + len(f_in) + len(f_out)],
                  *refs[13 + len(f_in) + len(f_out):])
        _fused_begin(fused, grid, f_refs)
        dq_ref, dk_ref, dv_ref, dzg_ref = (dp_ref.at[n] for n in range(4))
        ri = lax.broadcasted_iota(jnp.int32, (tk, tk), 0)
        ci = lax.broadcasted_iota(jnp.int32, (tk, tk), 1)
        u_le = (ri <= ci).astype(BF16)
        u_lt = (ri < ci).astype(BF16)
        ahead = lax.broadcasted_iota(jnp.int32, (tq, tk), 0) - lax.broadcasted_iota(jnp.int32, (tq, tk), 1)
        g = g_ref[...]

        def prologue(i, dg):
            rows = pl.ds(pl.multiple_of(i * tq, tq), tq)
            do, dzg, dgi = _head_out_bwd(o_ref[rows, :], zg_ref[rows, :], g, dy_ref[rows, :])
            dzg_ref[rows, :] = dzg.astype(BF16)
            do_s[rows, :] = do.astype(BF16)
            return dg + dgi

        dg_ref[...] = lax.fori_loop(0, nq, prologue, jnp.zeros((1, HEAD_DIM), F32))
        dk_s[...] = jnp.zeros_like(dk_s)
        dv_s[...] = jnp.zeros_like(dv_s)

        def qblock(i, _):
            rows = pl.ds(pl.multiple_of(i * tq, tq), tq)
            q = q_ref[rows, :].astype(BF16)
            do = do_s[rows, :]
            tot = tot_ref[rows, :][:, :1]

            def kblocks(jj, carry):
                dq, pre_l, pre_g = carry
                js = [jj * kpq + n for n in range(kpq)]
                cols = [pl.ds(pl.multiple_of(j * tk, tk), tk) for j in js]
                ks = [k_ref[c, :].astype(BF16) for c in cols]
                raw = [_dot(q, k, NT) for k in ks]
                da = [_dot(do, v_ref[c, :].astype(BF16), NT) for c in cols]
                terms = [_sb_terms(x, ahead > j * tk - i * tq) for x, j in zip(raw, js)]
                sums_l = [_split_dot(l1m, u_le) for _, l1m, _ in terms]
                a, gg = [], []
                for (t, l1m, _), part, d in zip(terms, sums_l, da):
                    a.append(jnp.exp2((t + l1m) + (tot - (part + pre_l))))
                    gg.append(a[-1] * d)
                    pre_l = pre_l + jnp.sum(l1m, axis=1, keepdims=True)
                sums_g = [_split_dot(x, u_lt) for x in gg]
                dzs = []
                for (t, _, e), x, part in zip(terms, gg, sums_g):
                    big_g = part + pre_g
                    pre_g = pre_g + jnp.sum(x, axis=1, keepdims=True)
                    inv = 1.0 / (1.0 + e)
                    sig = jnp.where(t >= 0.0, inv, e * inv)
                    dzs.append(((x - sig * (x + big_g)) * scale).astype(BF16))
                for x, k in zip(dzs, ks):
                    dq = dq + _dot(x, k, NN)
                for x, y, c in zip(dzs, a, cols):
                    dk_s[c, :] += _dot(x, q, TN)
                    dv_s[c, :] += _dot(y.astype(BF16), do, TN)
                return dq, pre_l, pre_g

            zero = jnp.zeros((tq, 1), F32)
            dq, _, _ = lax.fori_loop(0, i + 1, kblocks, (jnp.zeros((tq, HEAD_DIM), F32), zero, zero))
            dq_ref[rows, :] = dq.astype(BF16)
            return 0

        lax.fori_loop(0, nq, qblock, 0)
        dk_ref[...] = dk_s[...].astype(BF16)
        dv_ref[...] = dv_s[...].astype(BF16)
        _fused_end(fused, grid, f_refs)

    return pl.pallas_call(
        body, name="sb_bwd", grid=grid,
        in_specs=[_seg_spec(s, 0), _seg_spec(s, 1), _seg_spec(s, 2), _seg_spec(s, 3),
                  _head_spec(s), _head_spec(s), _seg_spec(s, 0),
                  pl.BlockSpec((1, HEAD_DIM), lambda i, h: (0, h))] + f_in_specs,
        out_specs=[_seg4_spec(s, 0), pl.BlockSpec((None, 1, HEAD_DIM), lambda i, h: (i, 0, h))] + f_out_specs,
        out_shape=[jax.ShapeDtypeStruct((8, b, s, w), BF16), jax.ShapeDtypeStruct((b, 1, w), F32)] + f_out,
        scratch_shapes=[pltpu.VMEM((s, HEAD_DIM), BF16), pltpu.VMEM((s, HEAD_DIM), F32),
                        pltpu.VMEM((s, HEAD_DIM), F32)] + f_scratch,
        compiler_params=_cparams(("arbitrary", "arbitrary")),
    )(proj8, proj8, proj8, proj8, o_sb, tot_sb, dy2, g_sb, *f_in)


DIL_BLOCK = 128
DIL_GROUP = 4


def _dil_chunks(s, r):
    length = s // r
    out = []
    for rho in range(r):
        for cc in range(length // DIL_BLOCK):
            if r == 1:
                nat = pl.ds(cc * DIL_BLOCK, DIL_BLOCK)
            else:
                nat = pl.ds(rho + r * DIL_BLOCK * cc, DIL_BLOCK, stride=r)
            off = rho * length + cc * DIL_BLOCK
            out.append((nat, pl.ds(off, DIL_BLOCK), pl.ds(off + DIL_BLOCK, DIL_BLOCK)))
    return out


def _dil_masks(slope, r):
    n = DIL_BLOCK
    ri = lax.broadcasted_iota(jnp.int32, (n, 2 * n), 0)
    ci = lax.broadcasted_iota(jnp.int32, (n, 2 * n), 1)
    steps = ri - ci + n
    inside = (steps >= 0) & (steps <= n)
    bias = slope * (steps.astype(F32) * r)
    return jnp.where(inside, -bias, NEG_BIG), jnp.where(inside & (ci >= n), -bias, NEG_BIG)


def _dil_scores(q, k_pc, masks, first):
    return _dot(q, k_pc, NT) * (1.0 / math.sqrt(HEAD_DIM)) + jnp.where(first, masks[1], masks[0])


def _dil_check(s):
    assert (s // DIL_BLOCK) % DIL_GROUP == 0, s
    for window, r in DIL_PAIRS:
        assert window // r == DIL_BLOCK and s % (r * DIL_BLOCK) == 0, (s, window, r)


def _dil_fwd(proj8, g_dil, slopes, y2):
    _, b, s, w = proj8.shape
    n_heads = w // HEAD_DIM
    _dil_check(s)
    n = DIL_BLOCK
    nt = s // n

    def body(q_ref, k_ref, v_ref, zg_ref, g_ref, sl_ref, y_in, o_ref, lse_ref, y_ref,
             qp, kp, vp, pnum, pm, pl_, acc_s, m_s, l_s):
        del y_in
        slope = sl_ref[...][:, :1]
        kp[pl.ds(0, n), :] = jnp.zeros((n, HEAD_DIM), BF16)
        vp[pl.ds(0, n), :] = jnp.zeros((n, HEAD_DIM), BF16)

        for (window, r) in DIL_PAIRS:
            nb = (s // r) // n
            masks = _dil_masks(slope, float(r))
            for nat, per, padded in _dil_chunks(s, r):
                qp[per, :] = q_ref[nat, :].astype(BF16)
                kp[padded, :] = k_ref[nat, :].astype(BF16)
                vp[padded, :] = v_ref[nat, :].astype(BF16)
            num_t, m_t, l_t = (acc_s, m_s, l_s) if r == 1 else (pnum, pm, pl_)

            def tiles(tt, _):
                ts = [tt * DIL_GROUP + i for i in range(DIL_GROUP)]
                rows = [pl.ds(pl.multiple_of(t * n, n), n) for t in ts]
                both = [pl.ds(pl.multiple_of(t * n, n), 2 * n) for t in ts]
                sc = [_dil_scores(qp[rw, :], kp[bt, :], masks, lax.rem(t, nb) == 0)
                      for t, rw, bt in zip(ts, rows, both)]
                m = [jnp.max(x, axis=1, keepdims=True) for x in sc]
                p = [jnp.exp(x - mx) for x, mx in zip(sc, m)]
                num = [_dot(x.astype(BF16), vp[bt, :], NN) for x, bt in zip(p, both)]
                for rw, x, mx, nm in zip(rows, p, m, num):
                    num_t[rw, :] = nm
                    m_t[rw, :] = jnp.broadcast_to(mx, (n, HEAD_DIM))
                    l_t[rw, :] = jnp.broadcast_to(jnp.sum(x, axis=1, keepdims=True), (n, HEAD_DIM))
                return 0

            lax.fori_loop(0, nt // DIL_GROUP, tiles, 0)
            if r != 1:
                for nat, per, _ in _dil_chunks(s, r):
                    m_old, m_new_p = m_s[nat, :], pm[per, :]
                    m_new = jnp.maximum(m_old, m_new_p)
                    a_old, a_p = jnp.exp(m_old - m_new), jnp.exp(m_new_p - m_new)
                    m_s[nat, :] = m_new
                    l_s[nat, :] = l_s[nat, :] * a_old + pl_[per, :] * a_p
                    acc_s[nat, :] = acc_s[nat, :] * a_old + pnum[per, :] * a_p

        g = g_ref[...]

        def finish(t, _):
            rows = pl.ds(pl.multiple_of(t * n, n), n)
            l = l_s[rows, :]
            o = acc_s[rows, :] / l
            o_ref[rows, :] = o
            lse_ref[rows, :] = m_s[rows, :] + jnp.log(l)
            y_ref[rows, :] = _head_out(o, zg_ref[rows, :], g).astype(BF16)
            return 0

        lax.fori_loop(0, nt, finish, 0)

    f32_s = pltpu.VMEM((s, HEAD_DIM), F32)
    bf_s = pltpu.VMEM((s, HEAD_DIM), BF16)
    bf_pad = pltpu.VMEM((s + n, HEAD_DIM), BF16)
    return pl.pallas_call(
        body, name="dil_fwd", grid=(b, n_heads),
        in_specs=[_seg_spec(s, 4), _seg_spec(s, 5), _seg_spec(s, 6), _seg_spec(s, 7),
                  pl.BlockSpec((1, HEAD_DIM), lambda i, h: (0, h)),
                  pl.BlockSpec((None, 1, HEAD_DIM), lambda i, h: (h, 0, 0)), ANY],
        out_specs=[_head_spec(s), _head_spec(s), _seg_spec(s, 1)],
        out_shape=[jax.ShapeDtypeStruct((b, s, w), F32), jax.ShapeDtypeStruct((b, s, w), F32),
                   jax.ShapeDtypeStruct((2, b, s, w), BF16)],
        scratch_shapes=[bf_s, bf_pad, bf_pad, f32_s, f32_s, f32_s, f32_s, f32_s, f32_s],
        input_output_aliases={6: 2},
        compiler_params=_cparams(("parallel", "parallel")),
    )(proj8, proj8, proj8, proj8, g_dil, slopes, y2)


def _dil_bwd(proj8, o_dl, lse_dl, dy2, g_dil, slopes, dproj8):
    _, b, s, w = proj8.shape
    n_heads = w // HEAD_DIM
    _dil_check(s)
    n = DIL_BLOCK
    nt = s // n
    scale = 1.0 / math.sqrt(HEAD_DIM)

    def body(q_ref, k_ref, v_ref, zg_ref, o_ref, lse_ref, dy_ref, g_ref, sl_ref, dp_in, dp_ref, dg_ref,
             do_n, dt_n, dq_n, dk_n, dv_n, qp, kp, vp, dop, dtp, lsep, pdq, pdk, pdv):
        del dp_in
        dq_ref, dk_ref, dv_ref, dzg_ref = (dp_ref.at[i] for i in range(4))
        slope = sl_ref[...][:, :1]
        g = g_ref[...]

        def prologue(t, dg):
            rows = pl.ds(pl.multiple_of(t * n, n), n)
            o = o_ref[rows, :]
            do, dzg, dgi = _head_out_bwd(o, zg_ref[rows, :], g, dy_ref[rows, :])
            dzg_ref[rows, :] = dzg.astype(BF16)
            do_n[rows, :] = do
            dt_n[rows, :] = jnp.broadcast_to(jnp.sum(do * o, axis=-1, keepdims=True), (n, HEAD_DIM))
            return dg + dgi

        dg_ref[...] = lax.fori_loop(0, nt, prologue, jnp.zeros((1, HEAD_DIM), F32))
        dq_n[...] = jnp.zeros_like(dq_n)
        dk_n[...] = jnp.zeros_like(dk_n)
        dv_n[...] = jnp.zeros_like(dv_n)
        kp[pl.ds(0, n), :] = jnp.zeros((n, HEAD_DIM), BF16)
        vp[pl.ds(0, n), :] = jnp.zeros((n, HEAD_DIM), BF16)

        for (window, r) in DIL_PAIRS:
            nb = (s // r) // n
            masks = _dil_masks(slope, float(r))
            for nat, per, padded in _dil_chunks(s, r):
                qp[per, :] = q_ref[nat, :].astype(BF16)
                kp[padded, :] = k_ref[nat, :].astype(BF16)
                vp[padded, :] = v_ref[nat, :].astype(BF16)
                dop[per, :] = do_n[nat, :].astype(BF16)
                dtp[per, :] = dt_n[nat, :]
                lsep[per, :] = lse_ref[nat, :]
            pdk[...] = jnp.zeros_like(pdk)
            pdv[...] = jnp.zeros_like(pdv)

            def tiles(tt, _):
                ts = [tt * DIL_GROUP + i for i in range(DIL_GROUP)]
                rows = [pl.ds(pl.multiple_of(t * n, n), n) for t in ts]
                both = [pl.ds(pl.multiple_of(t * n, n), 2 * n) for t in ts]
                q = [qp[rw, :] for rw in rows]
                do = [dop[rw, :] for rw in rows]
                sc = [_dil_scores(qq, kp[bt, :], masks, lax.rem(t, nb) == 0) for t, qq, bt in zip(ts, q, both)]
                dp = [_dot(dd, vp[bt, :], NT) for dd, bt in zip(do, both)]
                p = [jnp.exp(x - lsep[rw, :][:, :1]) for x, rw in zip(sc, rows)]
                ds = [((x * (y - dtp[rw, :][:, :1])) * scale).astype(BF16) for x, y, rw in zip(p, dp, rows)]
                dq = [_dot(x, kp[bt, :], NN) for x, bt in zip(ds, both)]
                dk = [_dot(x, qq, TN) for x, qq in zip(ds, q)]
                dv = [_dot(x.astype(BF16), dd, TN) for x, dd in zip(p, do)]
                for rw, bt, x, y, z in zip(rows, both, dq, dk, dv):
                    pdq[rw, :] = x
                    pdk[bt, :] += y
                    pdv[bt, :] += z
                return 0

            lax.fori_loop(0, nt // DIL_GROUP, tiles, 0)
            for nat, per, padded in _dil_chunks(s, r):
                dq_n[nat, :] += pdq[per, :]
                dk_n[nat, :] += pdk[padded, :]
                dv_n[nat, :] += pdv[padded, :]

        dq_ref[...] = dq_n[...].astype(BF16)
        dk_ref[...] = dk_n[...].astype(BF16)
        dv_ref[...] = dv_n[...].astype(BF16)

    f32_s = pltpu.VMEM((s, HEAD_DIM), F32)
    f32_pad = pltpu.VMEM((s + n, HEAD_DIM), F32)
    bf_s = pltpu.VMEM((s, HEAD_DIM), BF16)
    bf_pad = pltpu.VMEM((s + n, HEAD_DIM), BF16)
    return pl.pallas_call(
        body, name="dil_bwd", grid=(b, n_heads),
        in_specs=[_seg_spec(s, 4), _seg_spec(s, 5), _seg_spec(s, 6), _seg_spec(s, 7),
                  _head_spec(s), _head_spec(s), _seg_spec(s, 1),
                  pl.BlockSpec((1, HEAD_DIM), lambda i, h: (0, h)),
                  pl.BlockSpec((None, 1, HEAD_DIM), lambda i, h: (h, 0, 0)), ANY],
        out_specs=[_seg4_spec(s, 1), pl.BlockSpec((None, 1, HEAD_DIM), lambda i, h: (i, 0, h))],
        out_shape=[jax.ShapeDtypeStruct((8, b, s, w), BF16), jax.ShapeDtypeStruct((b, 1, w), F32)],
        scratch_shapes=[f32_s] * 5 + [bf_s, bf_pad, bf_pad, bf_s] + [f32_s, f32_s, f32_s, f32_pad, f32_pad],
        input_output_aliases={9: 0},
        compiler_params=_cparams(("parallel", "parallel")),
    )(proj8, proj8, proj8, proj8, o_dl, lse_dl, dy2, g_dil, slopes, dproj8)


def _small_update(gathered, n_b, params, m, v):
    n_dev, _, width = gathered.shape

    def body(g_ref, p_ref, m_ref, v_ref, grad_ref, d_ref, nm_ref, nv_ref, loss_ref):
        for row in range(2):
            acc = None
            for dev in range(n_dev):
                for i in range(n_b):
                    term = g_ref[dev, pl.ds(row * n_b + i, 1), :]
                    acc = term if acc is None else acc + term
            grad_ref[pl.ds(row, 1), :] = acc
        loss = g_ref[0, pl.ds(2 * n_b, 1), pl.ds(0, 128)]
        for dev in range(1, n_dev):
            loss = loss + g_ref[dev, pl.ds(2 * n_b, 1), pl.ds(0, 128)]
        loss_ref[...] = loss
        d, nm, nv = _adamw_math(p_ref[...], grad_ref[...], m_ref[...], v_ref[...])
        d_ref[...] = d
        nm_ref[...] = nm
        nv_ref[...] = nv

    sds = jax.ShapeDtypeStruct((2, width), F32)
    return pl.pallas_call(
        body, name="small_update",
        in_specs=[VMEM_SPEC] * 4, out_specs=[VMEM_SPEC] * 5,
        out_shape=[sds, sds, sds, sds, jax.ShapeDtypeStruct((1, 128), F32)],
        compiler_params=_cparams(),
    )(gathered, params, m, v)


def _wada_update(c_t, dmod, w, m, v):
    d, bt = c_t.shape
    _, n = dmod.shape
    tr, tc = _tile(d, 512), _tile(n, 1024)

    def body(c_ref, dm_ref, w_ref, m_ref, v_ref, g_ref, d_ref, nm_ref, nv_ref):
        cv = c_ref[...]
        cs = (cv * _sigmoid(cv)).astype(BF16)
        grad = _dot(cs, dm_ref[...].astype(BF16), NN)
        g_ref[...] = grad
        dl, nm, nv = _adamw_math(w_ref[...], grad, m_ref[...], v_ref[...])
        d_ref[...] = dl
        nm_ref[...] = nm
        nv_ref[...] = nv

    spec = pl.BlockSpec((tr, tc), lambda i, j: (i, j))
    sds = jax.ShapeDtypeStruct((d, n), F32)
    return pl.pallas_call(
        body, name="wada_update", grid=(d // tr, n // tc),
        in_specs=[pl.BlockSpec((tr, bt), lambda i, j: (i, 0)), pl.BlockSpec((bt, tc), lambda i, j: (0, j)),
                  spec, spec, spec],
        out_specs=[spec] * 4, out_shape=[sds] * 4,
        compiler_params=_cparams(("parallel", "parallel")),
    )(c_t, dmod, w, m, v)


def _reduce_begin(gs, tag):
    ra = _sibling_half_swap(gs, "swap_" + tag)
    pa, own = _pair_sum(gs, ra, "pair_sum_" + tag)
    return _ScatterChips(pa), own


def _reduce_finish(rb, own, w, m, v, tag):
    half = _chip_sum(rb, own, "chip_sum_" + tag)
    grad = _sibling_join(half, "join_" + tag)
    delta, nm, nv = _adamw(w, grad, m, v, "adamw_" + tag)
    return grad, delta, nm, nv


def kernel(x, c, w_ada, b_ada, g_norm, w_in, g_sb, g_dil, w_out, g_final, loss_target, m_w_ada, m_b_ada, m_g_norm, m_w_in, m_g_sb, m_g_dil, m_w_out, m_g_final, v_w_ada, v_b_ada, v_g_norm, v_w_in, v_g_sb, v_g_dil, v_w_out, v_g_final):
    nb, s, d = x.shape
    t = nb * s
    na = w_ada.shape[2]
    cs = w_in.shape[2]
    w = cs // 2
    n_heads = w // HEAD_DIM
    r_out = w_out.shape[1]
    assert 2 * nb + 1 <= 8 and 2 * d + 2 * w <= 3 * d and N_CHIPS * na == 3 * d and N_CHIPS * r_out == 2 * w
    xi, yi, ci = _place()
    chip = 2 * xi + yi
    dev = 2 * chip + ci

    c_all = _allgather8(jnp.pad(c, ((0, 8 - nb), (0, 0))), "gather_c")
    c16 = c_all.reshape(N_DEV, 8, d)[:, :nb].reshape(N_DEV * nb, d)
    b_ada_shard = lax.dynamic_slice(b_ada, (0, chip * na), (1, na))
    mod_part = _mod_fwd(c16, w_ada[0], b_ada_shard)
    mod_all = _allgather8(mod_part, "gather_mod")
    mod_full = mod_all.reshape(N_CHIPS, 2, N_DEV * nb, na)[:, 0].transpose(1, 0, 2).reshape(N_DEV * nb, 3 * d)
    mod = lax.dynamic_slice(mod_full, (dev * nb, 0), (nb, 3 * d))
    shift, scale, gate = (mod[:, i * d:(i + 1) * d].reshape(nb, 1, d) for i in range(3))

    h = _norm_mod_fwd(x, g_norm, scale, shift)
    h2 = h.reshape(t, d)
    ws_in = _allgather_chips(_cast_bf16_slab(w_in[0], "cast_w_in"), "gather_w_in")
    ws_out = _allgather_chips(_cast_bf16_slab(w_out[0], "cast_w_out"), "gather_w_out")
    w_out_full = ws_out.reshape(2 * w, d)

    proj8 = _proj_fwd(h2, ws_in).reshape(8, nb, s, w)
    slopes = jnp.exp2(-ALIBI_MAX_BIAS * jnp.arange(1, n_heads + 1, dtype=F32) / n_heads)
    slopes = jnp.broadcast_to(slopes[:, None, None], (n_heads, 1, HEAD_DIM))
    o_sb, tot_sb, y2 = _sb_fwd(proj8, g_sb)
    o_dl, lse_dl, y2 = _dil_fwd(proj8, g_dil, slopes, y2)
    y2f = y2.reshape(2, t, w)
    out = _out_fwd(y2f, w_out_full)

    dx1, dout, dgate, dg_final, loss_part = _loss_head(
        x, out.reshape(nb, s, d), gate, g_final.reshape(1, d), loss_target)
    dout2 = dout.reshape(t, d)
    gs_out = _out_bwd_w(y2f, dout2).reshape(N_CHIPS, r_out, d)
    scatter_out, own_out = _reduce_begin(gs_out, "w_out")
    dy2 = _out_bwd_y(dout2, w_out_full).reshape(2, nb, s, w)
    dproj8, dg_sb, rb_out = _sb_bwd(proj8, o_sb, tot_sb, dy2, g_sb, fused=scatter_out)
    dproj8, dg_dl = _dil_bwd(proj8, o_dl, lse_dl, dy2, g_dil, slopes, dproj8)
    dproj8 = dproj8.reshape(8, t, w)
    gs_in = _proj_bwd_w(h2, dproj8)
    scatter_in, own_in = _reduce_begin(gs_in, "w_in")
    dh, rb_in = _proj_bwd_x(dproj8, ws_in, fused=scatter_in)
    grad_x, dshift, dscale, dg_norm = _norm_mod_bwd(x, dh.reshape(nb, s, d), dx1, g_norm, scale)

    width = 3 * d
    dmod = jnp.concatenate([dshift, dscale, dgate], axis=-1).reshape(nb, width)
    gains = jnp.concatenate([dg_sb.reshape(nb, w), dg_dl.reshape(nb, w)], axis=-1)
    gains = jnp.pad(gains, ((0, 0), (2 * d, width - 2 * d - 2 * w)))
    first = jnp.pad(jnp.concatenate([dg_norm, dg_final], axis=-1), ((0, nb - 1), (0, width - 2 * d)))
    loss_row = jnp.pad(loss_part, ((0, 0), (0, width - 128)))
    pack = jnp.concatenate([dmod, gains + first, loss_row, jnp.zeros((8 - 2 * nb - 1, width), F32)], axis=0)
    gathered = _allgather8(pack, "gather_small").reshape(N_DEV, 8, width)

    def stack(bias, gn, gf, gsb, gdl):
        row1 = jnp.concatenate([gn.reshape(1, d), gf.reshape(1, d), gsb.reshape(1, w), gdl.reshape(1, w)], axis=-1)
        return jnp.concatenate([bias.reshape(1, width), jnp.pad(row1, ((0, 0), (0, width - 2 * d - 2 * w)))], axis=0)

    small = _small_update(
        gathered, nb, stack(b_ada, g_norm, g_final, g_sb, g_dil),
        stack(m_b_ada, m_g_norm, m_g_final, m_g_sb, m_g_dil), stack(v_b_ada, v_g_norm, v_g_final, v_g_sb, v_g_dil))
    loss = small[4][0, 0]

    def unstack(a):
        return (a[0:1, :], a[1:2, 0:d], a[1, d:2 * d], a[1:2, 2 * d:2 * d + w], a[1:2, 2 * d + w:2 * d + 2 * w])

    (g_b, g_gn, g_gf, g_gsb, g_gdl), (d_b, d_gn, d_gf, d_gsb, d_gdl), (nm_b, nm_gn, nm_gf, nm_gsb, nm_gdl), \
        (nv_b, nv_gn, nv_gf, nv_gsb, nv_gdl) = (unstack(a) for a in small[:4])

    dmod_all = gathered[:, :nb].reshape(N_DEV * nb, width)
    dmod_cols = lax.dynamic_slice(dmod_all, (0, chip * na), (N_DEV * nb, na))
    g_wa, d_wa, nm_wa, nv_wa = _wada_update(c16.T, dmod_cols, w_ada[0], m_w_ada[0], v_w_ada[0])

    g_wi, d_wi, nm_wi, nv_wi = _reduce_finish(rb_in, own_in, w_in[0], m_w_in[0], v_w_in[0], "w_in")
    g_wo, d_wo, nm_wo, nv_wo = _reduce_finish(rb_out, own_out, w_out[0], m_w_out[0], v_w_out[0], "w_out")

    lead = lambda a: a[None]
    return (loss, grad_x,
            lead(g_wa), g_b, g_gn, lead(g_wi), g_gsb, g_gdl, lead(g_wo), g_gf,
            lead(d_wa), d_b, d_gn, lead(d_wi), d_gsb, d_gdl, lead(d_wo), d_gf,
            lead(nm_wa), nm_b, nm_gn, lead(nm_wi), nm_gsb, nm_gdl, lead(nm_wo), nm_gf,
            lead(nv_wa), nv_b, nv_gn, lead(nv_wi), nv_gsb, nv_gdl, lead(nv_wo), nv_gf)
```

```python
import functools
import math

import jax
import jax.numpy as jnp
from jax import lax
from jax.experimental import pallas as pl
from jax.experimental.pallas import tpu as pltpu

F32 = jnp.float32
BF16 = jnp.bfloat16
MESH = pl.DeviceIdType.MESH

HEAD_DIM = 128
EPS = 1e-6
DIL_PAIRS = ((128, 1), (512, 4), (2048, 16))
ALIBI_MAX_BIAS = 8.0
ADAM_LR = 0.001
ADAM_B1 = 0.9
ADAM_B2 = 0.999
ADAM_EPS = 1e-08
ADAM_WD = 0.01
ADAM_STEP = 10
N_CHIPS = 4
N_DEV = 8
VMEM_LIMIT_BYTES = 56 * 1024 * 1024
NEG_BIG = -1e30

NN = (((1,), (0,)), ((), ()))
NT = (((1,), (1,)), ((), ()))
TN = (((0,), (0,)), ((), ()))

ANY = pl.BlockSpec(memory_space=pl.ANY)
VMEM_SPEC = pl.BlockSpec(memory_space=pltpu.VMEM)


def _cparams(sem=None):
    return pltpu.CompilerParams(dimension_semantics=sem, vmem_limit_bytes=VMEM_LIMIT_BYTES)


def _tile(dim, pref):
    t = min(dim, pref)
    assert dim % t == 0, (dim, pref)
    return t


def _dot(a, b, dims):
    return lax.dot_general(a, b, dims, preferred_element_type=F32)


def _sigmoid(x):
    return 1.0 / (1.0 + jnp.exp(-x))


def _place():
    return lax.axis_index("x"), lax.axis_index("y"), lax.axis_index("c")


def _allgather8(x_shard, name):
    m_per, n = x_shard.shape

    def body(x_ref, out_ref, send_sems, recv_sems, local_sem):
        x, y, c = _place()
        me, sibling = (x, y, c), (x, y, 1 - c)
        chips = [(1 - x, y), (x, 1 - y), (1 - x, 1 - y)]

        def rows(px, py, pc):
            return out_ref.at[pl.ds((4 * px + 2 * py + pc) * m_per, m_per), :]

        def copy(k, block, to, src=None):
            return pltpu.make_async_remote_copy(
                src_ref=rows(*block) if src is None else src, dst_ref=rows(*block),
                send_sem=send_sems.at[k], recv_sem=recv_sems.at[k], device_id=to, device_id_type=MESH)

        mine = pltpu.make_async_copy(x_ref, rows(*me), local_sem)
        mine.start()
        first = [copy(0, me, sibling, src=x_ref)]
        first += [copy(1 + j, me, (*chip, c), src=x_ref) for j, chip in enumerate(chips)]
        for cp in first:
            cp.start()
        passed = [copy(4 + j, (*chip, c), sibling) for j, chip in enumerate(chips)]
        for j, chip in enumerate(chips):
            copy(1 + j, (*chip, c), me).wait_recv()
            passed[j].start()
        copy(0, sibling, me).wait_recv()
        for j, chip in enumerate(chips):
            copy(4 + j, (*chip, 1 - c), me).wait_recv()
        for cp in first + passed:
            cp.wait_send()
        mine.wait()

    return pl.pallas_call(
        body, name=name,
        out_shape=jax.ShapeDtypeStruct((N_DEV * m_per, n), x_shard.dtype),
        in_specs=[VMEM_SPEC], out_specs=VMEM_SPEC,
        scratch_shapes=[pltpu.SemaphoreType.DMA((7,)), pltpu.SemaphoreType.DMA((7,)), pltpu.SemaphoreType.DMA],
    )(x_shard)


def _allgather_chips(ws, name):
    _, r, cdim = ws.shape
    quarter = r // 4

    def body(in_ref, out_ref, send_sems, recv_sems):
        del in_ref
        x, y, c = _place()
        me, sibling = (x, y, c), (x, y, 1 - c)
        x_nbr, y_nbr = (1 - x, y), (x, 1 - y)

        def copy(k, chip, pc, part, to):
            rows = out_ref.at[2 * chip[0] + chip[1], pl.ds((2 * pc + part) * quarter, quarter), :]
            return pltpu.make_async_remote_copy(
                src_ref=rows, dst_ref=rows, send_sem=send_sems.at[k], recv_sem=recv_sems.at[k],
                device_id=to, device_id_type=MESH)

        own = [(0, (x, y), 0, x_nbr), (1, (x, y), 1, y_nbr), (2, (x, y), 1, x_nbr), (3, (x, y), 0, y_nbr)]
        sends = [copy(k, chip, c, part, (*to, c)) for k, chip, part, to in own]
        for cp in sends:
            cp.start()
        diag = (1 - x, 1 - y)
        landing = [(0, x_nbr, 0, (4, y_nbr)), (1, y_nbr, 1, (5, x_nbr)), (2, x_nbr, 1, None), (3, y_nbr, 0, None),
                   (4, diag, 0, None), (5, diag, 1, None)]
        for k, chip, part, onward in landing:
            copy(k, chip, c, part, me).wait_recv()
            if onward is not None:
                sends.append(copy(onward[0], chip, c, part, (*onward[1], c)))
                sends[-1].start()
            sends.append(copy(6 + k, chip, c, part, sibling))
            sends[-1].start()
        for k, chip, part, _ in landing:
            copy(6 + k, chip, 1 - c, part, me).wait_recv()
        for cp in sends:
            cp.wait_send()

    return pl.pallas_call(
        body, name=name,
        out_shape=jax.ShapeDtypeStruct(ws.shape, ws.dtype),
        in_specs=[ANY], out_specs=ANY, input_output_aliases={0: 0},
        scratch_shapes=[pltpu.SemaphoreType.DMA((12,)), pltpu.SemaphoreType.DMA((12,))],
    )(ws)


def _sibling_half_swap(gs, name):
    n, r, cdim = gs.shape
    half = r // 2

    def body(g_ref, out_ref, send_sem, recv_sem):
        x, y, c = _place()
        cp = pltpu.make_async_remote_copy(
            src_ref=g_ref.at[:, pl.ds((1 - c) * half, half), :], dst_ref=out_ref,
            send_sem=send_sem, recv_sem=recv_sem, device_id=(x, y, 1 - c), device_id_type=MESH)
        cp.start()
        cp.wait()

    return pl.pallas_call(
        body, name=name,
        out_shape=jax.ShapeDtypeStruct((n, half, cdim), gs.dtype),
        in_specs=[ANY], out_specs=ANY,
        scratch_shapes=[pltpu.SemaphoreType.DMA, pltpu.SemaphoreType.DMA],
    )(gs)


class _ScatterChips:
    def __init__(self, pa):
        self.inputs = [pa]
        self.out_shapes = [jax.ShapeDtypeStruct(pa.shape, pa.dtype)]
        self.scratch = [pltpu.SemaphoreType.DMA((3,)), pltpu.SemaphoreType.DMA((3,)), pltpu.SemaphoreType.DMA]

    @staticmethod
    def _mine(p_ref, out_ref, send_sems, recv_sems, local_sem):
        x, y, _ = _place()
        return pltpu.make_async_copy(p_ref.at[2 * x + y], out_ref.at[2 * x + y], local_sem)

    @staticmethod
    def _remote(p_ref, out_ref, send_sems, recv_sems, local_sem, incoming):
        x, y, c = _place()
        me = 2 * x + y
        remote = []
        for j, (px, py) in enumerate([(1 - x, y), (x, 1 - y), (1 - x, 1 - y)]):
            remote.append(pltpu.make_async_remote_copy(
                src_ref=p_ref.at[me if incoming else 2 * px + py], dst_ref=out_ref.at[2 * px + py if incoming else me],
                send_sem=send_sems.at[j], recv_sem=recv_sems.at[j], device_id=(px, py, c), device_id_type=MESH))
        return remote

    def start(self, *refs):
        self._mine(*refs).start()
        for cp in self._remote(*refs, incoming=False):
            cp.start()

    def wait(self, *refs):
        for cp in self._remote(*refs, incoming=True):
            cp.wait_recv()
        for cp in self._remote(*refs, incoming=False):
            cp.wait_send()
        self._mine(*refs).wait()


def _fused_specs(fused):
    if fused is None:
        return [], [], [], [], []
    return (list(fused.inputs), [ANY] * len(fused.inputs), list(fused.out_shapes), [ANY] * len(fused.out_shapes),
            list(fused.scratch))


def _fused_begin(fused, grid, refs):
    if fused is not None:
        first = functools.reduce(lambda p, q: p & q, [pl.program_id(i) == 0 for i in range(len(grid))])
        pl.when(first)(lambda: fused.start(*refs))


def _fused_end(fused, grid, refs):
    if fused is not None:
        last = functools.reduce(lambda p, q: p & q, [pl.program_id(i) == g - 1 for i, g in enumerate(grid)])
        pl.when(last)(lambda: fused.wait(*refs))


def _sibling_join(full, name):
    h2, cdim = full.shape
    h = h2 // 2

    def body(in_ref, out_ref, send_sem, recv_sem):
        del in_ref
        x, y, c = _place()
        mine = out_ref.at[pl.ds(c * h, h), :]
        cp = pltpu.make_async_remote_copy(
            src_ref=mine, dst_ref=mine, send_sem=send_sem, recv_sem=recv_sem,
            device_id=(x, y, 1 - c), device_id_type=MESH)
        cp.start()
        theirs = out_ref.at[pl.ds((1 - c) * h, h), :]
        pltpu.make_async_remote_copy(
            src_ref=theirs, dst_ref=theirs, send_sem=send_sem, recv_sem=recv_sem,
            device_id=(x, y, 1 - c), device_id_type=MESH).wait_recv()
        cp.wait_send()

    return pl.pallas_call(
        body, name=name,
        out_shape=jax.ShapeDtypeStruct(full.shape, full.dtype),
        in_specs=[ANY], out_specs=ANY, input_output_aliases={0: 0},
        scratch_shapes=[pltpu.SemaphoreType.DMA, pltpu.SemaphoreType.DMA],
    )(full)


def _cast_bf16_slab(w, name):
    r, cdim = w.shape
    tr, tc = _tile(r, 512), _tile(cdim, 2048)

    def body(pc_ref, w_ref, o_ref):
        o_ref[...] = w_ref[...].astype(BF16)

    return pl.pallas_call(
        body, name=name,
        grid_spec=pltpu.PrefetchScalarGridSpec(
            num_scalar_prefetch=1, grid=(r // tr, cdim // tc),
            in_specs=[pl.BlockSpec((tr, tc), lambda i, j, pc: (i, j))],
            out_specs=pl.BlockSpec((None, tr, tc), lambda i, j, pc: (pc[1], i, j))),
        out_shape=jax.ShapeDtypeStruct((N_CHIPS, r, cdim), BF16),
        compiler_params=_cparams(("parallel", "parallel")),
    )(_place_scalars(), w)


def _place_scalars():
    x, y, c = _place()
    return jnp.stack([c, 2 * x + y]).astype(jnp.int32)


def _pair_sum(gs, ra, name):
    n, r, cdim = gs.shape
    half = r // 2
    tr, tc = _tile(half, 512), _tile(cdim, 2048)
    nt = half // tr

    def body(pc_ref, g_ref, r_ref, o_ref, own_ref):
        val = g_ref[...].astype(F32) + r_ref[...].astype(F32)
        o_ref[...] = val.astype(BF16)

        @pl.when(pl.program_id(2) == pc_ref[1])
        def _():
            own_ref[...] = val

    return pl.pallas_call(
        body, name=name,
        grid_spec=pltpu.PrefetchScalarGridSpec(
            num_scalar_prefetch=1, grid=(nt, cdim // tc, n),
            in_specs=[pl.BlockSpec((None, tr, tc), lambda i, j, s, pc: (s, pc[0] * nt + i, j)),
                      pl.BlockSpec((None, tr, tc), lambda i, j, s, pc: (s, i, j))],
            out_specs=[pl.BlockSpec((None, tr, tc), lambda i, j, s, pc: (s, i, j)),
                       pl.BlockSpec((tr, tc), lambda i, j, s, pc: (i, j))]),
        out_shape=[jax.ShapeDtypeStruct((n, half, cdim), BF16), jax.ShapeDtypeStruct((half, cdim), F32)],
        compiler_params=_cparams(("parallel", "parallel", "arbitrary")),
    )(_place_scalars(), gs, ra)


def _chip_sum(rb, own, name):
    n, h, cdim = rb.shape
    tr, tc = _tile(h, 256), _tile(cdim, 2048)
    nt = h // tr

    def body(pc_ref, r_ref, own_ref, o_ref):
        chip = pc_ref[1]
        acc = None
        for p in range(n):
            term = jnp.where(chip == p, own_ref[...], r_ref[p].astype(F32))
            acc = term if acc is None else acc + term
        o_ref[...] = acc

    return pl.pallas_call(
        body, name=name,
        grid_spec=pltpu.PrefetchScalarGridSpec(
            num_scalar_prefetch=1, grid=(nt, cdim // tc),
            in_specs=[pl.BlockSpec((n, tr, tc), lambda i, j, pc: (0, i, j)),
                      pl.BlockSpec((tr, tc), lambda i, j, pc: (i, j))],
            out_specs=pl.BlockSpec((tr, tc), lambda i, j, pc: (pc[0] * nt + i, j))),
        out_shape=jax.ShapeDtypeStruct((2 * h, cdim), F32),
        compiler_params=_cparams(("parallel", "parallel")),
    )(_place_scalars(), rb, own)


def _adamw_math(w, g, m, v):
    m = ADAM_B1 * m + (1.0 - ADAM_B1) * g
    v = ADAM_B2 * v + (1.0 - ADAM_B2) * (g * g)
    m_hat = m / (1.0 - ADAM_B1 ** ADAM_STEP)
    v_hat = v / (1.0 - ADAM_B2 ** ADAM_STEP)
    delta = -ADAM_LR * (m_hat / (jnp.sqrt(v_hat) + ADAM_EPS) + ADAM_WD * w)
    return delta, m, v


def _adamw(w, g, m, v, name):
    r, cdim = w.shape
    tr, tc = _tile(r, 256), _tile(cdim, 2048)

    def body(w_ref, g_ref, m_ref, v_ref, go_ref, d_ref, nm_ref, nv_ref):
        gv = g_ref[...]
        d, nm, nv = _adamw_math(w_ref[...], gv, m_ref[...], v_ref[...])
        go_ref[...] = gv
        d_ref[...] = d
        nm_ref[...] = nm
        nv_ref[...] = nv

    spec = pl.BlockSpec((tr, tc), lambda i, j: (i, j))
    sds = jax.ShapeDtypeStruct((r, cdim), F32)
    return pl.pallas_call(
        body, name=name, grid=(r // tr, cdim // tc),
        in_specs=[spec] * 4, out_specs=[spec] * 4, out_shape=[sds] * 4,
        compiler_params=_cparams(("parallel", "parallel")),
    )(w, g, m, v)


def _matmul(a, b, *, grid, a_spec, b_spec, out_spec, out_shape, acc_shape, dims, name, bias=None, bias_spec=None,
            silu_a=False, fused=None):
    nk = grid[2]
    f_in, f_in_specs, f_out, f_out_specs, f_scratch = _fused_specs(fused)
    n_in = 2 + (bias is not None)

    acc_scratch = [pltpu.VMEM(acc_shape, F32)] if nk > 1 else []

    def body(*refs):
        a_ref, b_ref = refs[:2]
        bias_ref = refs[2] if bias is not None else None
        o_ref = refs[n_in + len(f_in)]
        n_fixed = n_in + len(f_in) + 1 + len(f_out)
        f_refs = (*refs[n_in:n_in + len(f_in)], *refs[n_in + len(f_in) + 1:n_fixed],
                  *refs[n_fixed + len(acc_scratch):])
        _fused_begin(fused, grid, f_refs)

        def product():
            if len(a_ref.shape) == 3:
                tks = a_ref.shape[2]
                parts = [_dot(a_ref[i], b_ref[:, i * tks:(i + 1) * tks], dims) for i in range(a_ref.shape[0])]
                return functools.reduce(lambda p, q: p + q, parts)
            av = a_ref[...]
            if silu_a:
                av = av * _sigmoid(av)
            return _dot(av.astype(BF16), b_ref[...].astype(BF16), dims)

        def finish(res):
            if bias is not None:
                res = res + bias_ref[...]
            o_ref[...] = res.astype(o_ref.dtype)

        if nk == 1:
            finish(product())
        else:
            acc_ref = refs[n_fixed]
            k = pl.program_id(2)

            @pl.when(k == 0)
            def _():
                acc_ref[...] = product()

            if nk > 2:
                @pl.when((k > 0) & (k < nk - 1))
                def _():
                    acc_ref[...] += product()

            @pl.when(k == nk - 1)
            def _():
                finish(acc_ref[...] + product())

        _fused_end(fused, grid, f_refs)

    in_specs = [a_spec, b_spec] + ([] if bias is None else [bias_spec]) + f_in_specs
    args = (a, b) + (() if bias is None else (bias,)) + tuple(f_in)
    sem = ("parallel", "parallel", "arbitrary") if fused is None else ("arbitrary",) * 3
    res = pl.pallas_call(
        body, name=name, grid=grid, in_specs=in_specs, out_specs=[out_spec] + f_out_specs,
        out_shape=[out_shape] + f_out,
        scratch_shapes=acc_scratch + f_scratch,
        compiler_params=_cparams(sem),
    )(*args)
    return res[0] if fused is None else tuple(res)


def _mm_tiles(m, n, k):
    return _tile(m, 1024), _tile(n, 1024), _tile(k, 4096)


def _proj_fwd(h2, ws_in):
    t, d = h2.shape
    _, _, cs = ws_in.shape
    w = cs // 2
    tm, tn, tk = _mm_tiles(t, w, d)
    nps, npseg = cs // tn, w // tn
    return _matmul(
        h2, ws_in, grid=(t // tm, 8 * npseg, d // tk), dims=NN, name="proj_fwd",
        a_spec=pl.BlockSpec((tm, tk), lambda m, n, k: (m, k)),
        b_spec=pl.BlockSpec((None, tk, tn), lambda m, n, k: (n // nps, k, n % nps)),
        out_spec=pl.BlockSpec((None, tm, tn), lambda m, n, k: (n // npseg, m, n % npseg)),
        out_shape=jax.ShapeDtypeStruct((8, t, w), F32), acc_shape=(tm, tn))


def _proj_bwd_x(dproj8, ws_in, fused=None):
    _, t, w = dproj8.shape
    _, d, cs = ws_in.shape
    tm, tn, _ = _mm_tiles(t, d, w)
    return _matmul(
        dproj8, ws_in, grid=(t // tm, d // tn, N_CHIPS), dims=NT, name="proj_bwd_x", fused=fused,
        a_spec=pl.BlockSpec((2, tm, w), lambda m, n, k: (k, m, 0)),
        b_spec=pl.BlockSpec((None, tn, cs), lambda m, n, k: (k, n, 0)),
        out_spec=pl.BlockSpec((tm, tn), lambda m, n, k: (m, n)),
        out_shape=jax.ShapeDtypeStruct((t, d), F32), acc_shape=(tm, tn))


def _proj_bwd_w(h2, dproj8):
    t, d = h2.shape
    _, _, w = dproj8.shape
    cs = 2 * w
    tm, tn, tk = _mm_tiles(d, w, t)
    nps, npseg = cs // tn, w // tn
    return _matmul(
        h2, dproj8, grid=(d // tm, 8 * npseg, t // tk), dims=TN, name="proj_bwd_w",
        a_spec=pl.BlockSpec((tk, tm), lambda m, n, k: (k, m)),
        b_spec=pl.BlockSpec((None, tk, tn), lambda m, n, k: (n // npseg, k, n % npseg)),
        out_spec=pl.BlockSpec((None, tm, tn), lambda m, n, k: (n // nps, m, n % nps)),
        out_shape=jax.ShapeDtypeStruct((N_CHIPS, d, cs), BF16), acc_shape=(tm, tn))


def _out_fwd(y2, w_out):
    _, t, w = y2.shape
    _, d = w_out.shape
    tm, tn, tk = _mm_tiles(t, d, w)
    kpg = w // tk
    return _matmul(
        y2, w_out, grid=(t // tm, d // tn, 2 * kpg), dims=NN, name="out_fwd",
        a_spec=pl.BlockSpec((None, tm, tk), lambda m, n, k: (k // kpg, m, k % kpg)),
        b_spec=pl.BlockSpec((tk, tn), lambda m, n, k: (k, n)),
        out_spec=pl.BlockSpec((tm, tn), lambda m, n, k: (m, n)),
        out_shape=jax.ShapeDtypeStruct((t, d), F32), acc_shape=(tm, tn))


def _out_bwd_y(dout, w_out):
    t, d = dout.shape
    w = w_out.shape[0] // 2
    tm, tn, tk = _mm_tiles(t, w, d)
    npg = w // tn
    return _matmul(
        dout, w_out, grid=(t // tm, 2 * npg, d // tk), dims=NT, name="out_bwd_y",
        a_spec=pl.BlockSpec((tm, tk), lambda m, n, k: (m, k)),
        b_spec=pl.BlockSpec((tn, tk), lambda m, n, k: (n, k)),
        out_spec=pl.BlockSpec((None, tm, tn), lambda m, n, k: (n // npg, m, n % npg)),
        out_shape=jax.ShapeDtypeStruct((2, t, w), F32), acc_shape=(tm, tn))


def _out_bwd_w(y2, dout):
    _, t, w = y2.shape
    _, d = dout.shape
    tm, tn, tk = _mm_tiles(w, d, t)
    mpg = w // tm
    return _matmul(
        y2, dout, grid=(2 * mpg, d // tn, t // tk), dims=TN, name="out_bwd_w",
        a_spec=pl.BlockSpec((None, tk, tm), lambda m, n, k: (m // mpg, k, m % mpg)),
        b_spec=pl.BlockSpec((tk, tn), lambda m, n, k: (k, n)),
        out_spec=pl.BlockSpec((tm, tn), lambda m, n, k: (m, n)),
        out_shape=jax.ShapeDtypeStruct((2 * w, d), BF16), acc_shape=(tm, tn))


def _mod_fwd(c_all, w_ada, b_ada):
    bt, d = c_all.shape
    _, n = w_ada.shape
    tn, tk = _tile(n, 512), _tile(d, 1024)
    return _matmul(
        c_all, w_ada, grid=(1, n // tn, d // tk), dims=NN, name="mod_fwd", silu_a=True,
        a_spec=pl.BlockSpec((bt, tk), lambda i, j, l: (0, l)),
        b_spec=pl.BlockSpec((tk, tn), lambda i, j, l: (l, j)),
        bias=b_ada, bias_spec=pl.BlockSpec((1, tn), lambda i, j, l: (0, j)),
        out_spec=pl.BlockSpec((bt, tn), lambda i, j, l: (0, j)),
        out_shape=jax.ShapeDtypeStruct((bt, n), F32), acc_shape=(bt, tn))


def _norm_mod_fwd(x, g_norm, scale, shift):
    b, s, d = x.shape
    ts = _tile(s, 256)

    def body(x_ref, g_ref, sc_ref, sh_ref, h_ref):
        xv = x_ref[...]
        r = lax.rsqrt(jnp.mean(xv * xv, axis=-1, keepdims=True) + EPS)
        y = (xv * r) * g_ref[...]
        h_ref[...] = (y * (1.0 + sc_ref[...]) + sh_ref[...]).astype(BF16)

    row = pl.BlockSpec((None, ts, d), lambda i, j: (i, j, 0))
    per_b = pl.BlockSpec((None, 1, d), lambda i, j: (i, 0, 0))
    return pl.pallas_call(
        body, name="norm_mod_fwd", grid=(b, s // ts),
        in_specs=[row, pl.BlockSpec((1, d), lambda i, j: (0, 0)), per_b, per_b],
        out_specs=row, out_shape=jax.ShapeDtypeStruct((b, s, d), BF16),
        compiler_params=_cparams(("parallel", "parallel")),
    )(x, g_norm, scale, shift)


def _norm_mod_bwd(x, dh, dx1, g_norm, scale):
    b, s, d = x.shape
    ts = _tile(s, 256)

    def body(x_ref, dh_ref, dx1_ref, g_ref, sc_ref, gx_ref, dsh_ref, dsc_ref, dg_ref):
        i, j = pl.program_id(0), pl.program_id(1)

        @pl.when(j == 0)
        def _():
            dsh_ref[...] = jnp.zeros_like(dsh_ref)
            dsc_ref[...] = jnp.zeros_like(dsc_ref)

        @pl.when((i == 0) & (j == 0))
        def _():
            dg_ref[...] = jnp.zeros_like(dg_ref)

        xv, dhv, g = x_ref[...], dh_ref[...], g_ref[...]
        r = lax.rsqrt(jnp.mean(xv * xv, axis=-1, keepdims=True) + EPS)
        xh = xv * r
        dsh_ref[...] += jnp.sum(dhv, axis=0, keepdims=True)
        dsc_ref[...] += jnp.sum(dhv * (xh * g), axis=0, keepdims=True)
        dn = dhv * (1.0 + sc_ref[...])
        dg_ref[...] += jnp.sum(dn * xh, axis=0, keepdims=True)
        u = dn * g
        dx = r * u - xv * (r * r * r) * jnp.mean(u * xv, axis=-1, keepdims=True)
        gx_ref[...] = dx1_ref[...] + dx

    row = pl.BlockSpec((None, ts, d), lambda i, j: (i, j, 0))
    per_b = pl.BlockSpec((None, 1, d), lambda i, j: (i, 0, 0))
    vec = pl.BlockSpec((1, d), lambda i, j: (0, 0))
    return pl.pallas_call(
        body, name="norm_mod_bwd", grid=(b, s // ts),
        in_specs=[row, row, row, vec, per_b],
        out_specs=[row, per_b, per_b, vec],
        out_shape=[jax.ShapeDtypeStruct((b, s, d), F32), jax.ShapeDtypeStruct((b, 1, d), F32),
                   jax.ShapeDtypeStruct((b, 1, d), F32), jax.ShapeDtypeStruct((1, d), F32)],
        compiler_params=_cparams(("arbitrary", "arbitrary")),
    )(x, dh, dx1, g_norm, scale)


def _loss_head(x, out, gate, g_final, target):
    b, s, d = x.shape
    ts = _tile(s, 256)

    def body(x_ref, o_ref, gt_ref, g_ref, t_ref, dx1_ref, dout_ref, dgt_ref, dg_ref, loss_ref):
        i, j = pl.program_id(0), pl.program_id(1)

        @pl.when(j == 0)
        def _():
            dgt_ref[...] = jnp.zeros_like(dgt_ref)

        @pl.when((i == 0) & (j == 0))
        def _():
            dg_ref[...] = jnp.zeros_like(dg_ref)
            loss_ref[...] = jnp.zeros_like(loss_ref)

        ov, gt, g = o_ref[...], gt_ref[...], g_ref[...]
        x1 = x_ref[...] + gt * ov
        r = lax.rsqrt(jnp.mean(x1 * x1, axis=-1, keepdims=True) + EPS)
        xh = x1 * r
        err = xh * g - t_ref[...]
        loss_ref[...] += 0.5 * jnp.sum(jnp.mean(err * err, axis=-1, keepdims=True))
        dfin = err * (1.0 / d)
        dg_ref[...] += jnp.sum(dfin * xh, axis=0, keepdims=True)
        u = dfin * g
        dx1 = r * u - x1 * (r * r * r) * jnp.mean(u * x1, axis=-1, keepdims=True)
        dx1_ref[...] = dx1
        dgt_ref[...] += jnp.sum(dx1 * ov, axis=0, keepdims=True)
        dout_ref[...] = (gt * dx1).astype(BF16)

    row = pl.BlockSpec((None, ts, d), lambda i, j: (i, j, 0))
    per_b = pl.BlockSpec((None, 1, d), lambda i, j: (i, 0, 0))
    vec = pl.BlockSpec((1, d), lambda i, j: (0, 0))
    return pl.pallas_call(
        body, name="loss_head", grid=(b, s // ts),
        in_specs=[row, row, per_b, vec, row],
        out_specs=[row, row, per_b, vec, pl.BlockSpec((1, 128), lambda i, j: (0, 0))],
        out_shape=[jax.ShapeDtypeStruct((b, s, d), F32), jax.ShapeDtypeStruct((b, s, d), BF16),
                   jax.ShapeDtypeStruct((b, 1, d), F32), jax.ShapeDtypeStruct((1, d), F32),
                   jax.ShapeDtypeStruct((1, 128), F32)],
        compiler_params=_cparams(("arbitrary", "arbitrary")),
    )(x, out, gate, g_final, target)


def _head_out(o, zg, g):
    rinv = lax.rsqrt(jnp.mean(o * o, axis=-1, keepdims=True) + EPS)
    return ((o * rinv) * g) * (zg * _sigmoid(zg))


def _head_out_bwd(o, zg, g, dy):
    rinv = lax.rsqrt(jnp.mean(o * o, axis=-1, keepdims=True) + EPS)
    rn = o * rinv
    sg = _sigmoid(zg)
    sil = zg * sg
    dzg = dy * (rn * g) * (sg * (1.0 + zg * (1.0 - sg)))
    dg = jnp.sum(dy * rn * sil, axis=0, keepdims=True)
    drn = dy * g * sil
    do = rinv * drn - o * (rinv * rinv * rinv) * jnp.mean(drn * o, axis=-1, keepdims=True)
    return do, dzg, dg


def _head_spec(s):
    return pl.BlockSpec((None, s, HEAD_DIM), lambda b, h: (b, 0, h))


def _seg_spec(s, seg):
    return pl.BlockSpec((None, None, s, HEAD_DIM), lambda b, h: (seg, b, 0, h))


def _seg4_spec(s, group):
    return pl.BlockSpec((4, None, s, HEAD_DIM), lambda b, h: (group, b, 0, h))


SB_Q_BLOCK = 512
SB_K_BLOCK = 256


LOG2_E = 1.4426950408889634
SB_LOGIT_SCALE = LOG2_E / math.sqrt(HEAD_DIM)


def _sb_terms(raw, valid):
    t = jnp.where(valid, raw * SB_LOGIT_SCALE, NEG_BIG)
    e = jnp.exp2(-jnp.abs(t))
    l1m = -(jnp.maximum(t, 0.0) + jnp.log2(1.0 + e))
    return t, l1m, e


def _split_dot(a, u):
    hi = a.astype(BF16)
    lo = (a - hi.astype(F32)).astype(BF16)
    return _dot(hi, u, NN) + _dot(lo, u, NN)


def _sb_fwd(proj8, g_sb):
    _, b, s, w = proj8.shape
    n_heads = w // HEAD_DIM
    tq, tk = _tile(s, SB_Q_BLOCK), _tile(s, SB_K_BLOCK)
    nq, kpq = s // tq, tq // tk
    scale = 1.0 / math.sqrt(HEAD_DIM)

    def body(q_ref, k_ref, v_ref, zg_ref, g_ref, o_ref, tot_ref, y_ref):
        u_excl = (lax.broadcasted_iota(jnp.int32, (tk, tk), 0)
                  > lax.broadcasted_iota(jnp.int32, (tk, tk), 1)).astype(BF16)
        ahead = lax.broadcasted_iota(jnp.int32, (tq, tk), 0) - lax.broadcasted_iota(jnp.int32, (tq, tk), 1)
        g = g_ref[...]

        def qblock(i, _):
            rows = pl.ds(pl.multiple_of(i * tq, tq), tq)
            q = q_ref[rows, :].astype(BF16)
            nk = (i + 1) * kpq

            def kblocks(jj, carry):
                acc, csum = carry
                js = [nk - 1 - (jj * kpq + n) for n in range(kpq)]
                cols = [pl.ds(pl.multiple_of(j * tk, tk), tk) for j in js]
                raw = [_dot(q, k_ref[c, :].astype(BF16), NT) for c in cols]
                terms = [_sb_terms(x, ahead > j * tk - i * tq) for x, j in zip(raw, js)]
                sums = [_split_dot(l1m, u_excl) for _, l1m, _ in terms]
                for (t, l1m, _), part, c in zip(terms, sums, cols):
                    a = jnp.exp2((t + l1m) + (part + csum))
                    acc = acc + _dot(a.astype(BF16), v_ref[c, :].astype(BF16), NN)
                    csum = csum + jnp.sum(l1m, axis=1, keepdims=True)
                return acc, csum

            acc, tot = lax.fori_loop(0, i + 1, kblocks, (jnp.zeros((tq, HEAD_DIM), F32), jnp.zeros((tq, 1), F32)))
            o_ref[rows, :] = acc
            tot_ref[rows, :] = jnp.broadcast_to(tot, (tq, HEAD_DIM))
            y_ref[rows, :] = _head_out(acc, zg_ref[rows, :], g).astype(BF16)
            return 0

        lax.fori_loop(0, nq, qblock, 0)

    return pl.pallas_call(
        body, name="sb_fwd", grid=(b, n_heads),
        in_specs=[_seg_spec(s, 0), _seg_spec(s, 1), _seg_spec(s, 2), _seg_spec(s, 3),
                  pl.BlockSpec((1, HEAD_DIM), lambda i, h: (0, h))],
        out_specs=[_head_spec(s), _head_spec(s), _seg_spec(s, 0)],
        out_shape=[jax.ShapeDtypeStruct((b, s, w), F32), jax.ShapeDtypeStruct((b, s, w), F32),
                   jax.ShapeDtypeStruct((2, b, s, w), BF16)],
        compiler_params=_cparams(("parallel", "parallel")),
    )(proj8, proj8, proj8, proj8, g_sb)


def _sb_bwd(proj8, o_sb, tot_sb, dy2, g_sb, fused=None):
    _, b, s, w = proj8.shape
    n_heads = w // HEAD_DIM
    tq, tk = _tile(s, SB_Q_BLOCK), _tile(s, SB_K_BLOCK)
    nq, kpq = s // tq, tq // tk
    scale = 1.0 / math.sqrt(HEAD_DIM)

    f_in, f_in_specs, f_out, f_out_specs, f_scratch = _fused_specs(fused)
    grid = (b, n_heads)

    def body(*refs):
        q_ref, k_ref, v_ref, zg_ref, o_ref, tot_ref, dy_ref, g_ref = refs[:8]
        dp_ref, dg_ref = refs[8 + len(f_in):10 + len(f_in)]
        do_s, dk_s, dv_s = refs[10 + len(f_in) + len(f_out):13 + len(f_in) + len(f_out)]
        f_refs = (*refs[8:8 + len(f_in)], *refs[10 + len(f_in):10 + len(f_in) + len(f_out)],
                  *refs[13 + len(f_in) + len(f_out):])
        _fused_begin(fused, grid, f_refs)
        dq_ref, dk_ref, dv_ref, dzg_ref = (dp_ref.at[n] for n in range(4))
        ri = lax.broadcasted_iota(jnp.int32, (tk, tk), 0)
        ci = lax.broadcasted_iota(jnp.int32, (tk, tk), 1)
        u_le = (ri <= ci).astype(BF16)
        u_lt = (ri < ci).astype(BF16)
        ahead = lax.broadcasted_iota(jnp.int32, (tq, tk), 0) - lax.broadcasted_iota(jnp.int32, (tq, tk), 1)
        g = g_ref[...]

        def prologue(i, dg):
            rows = pl.ds(pl.multiple_of(i * tq, tq), tq)
            do, dzg, dgi = _head_out_bwd(o_ref[rows, :], zg_ref[rows, :], g, dy_ref[rows, :])
            dzg_ref[rows, :] = dzg.astype(BF16)
            do_s[rows, :] = do.astype(BF16)
            return dg + dgi

        dg_ref[...] = lax.fori_loop(0, nq, prologue, jnp.zeros((1, HEAD_DIM), F32))
        dk_s[...] = jnp.zeros_like(dk_s)
        dv_s[...] = jnp.zeros_like(dv_s)

        def qblock(i, _):
            rows = pl.ds(pl.multiple_of(i * tq, tq), tq)
            q = q_ref[rows, :].astype(BF16)
            do = do_s[rows, :]
            tot = tot_ref[rows, :][:, :1]

            def kblocks(jj, carry):
                dq, pre_l, pre_g = carry
                js = [jj * kpq + n for n in range(kpq)]
                cols = [pl.ds(pl.multiple_of(j * tk, tk), tk) for j in js]
                ks = [k_ref[c, :].astype(BF16) for c in cols]
                raw = [_dot(q, k, NT) for k in ks]
                da = [_dot(do, v_ref[c, :].astype(BF16), NT) for c in cols]
                terms = [_sb_terms(x, ahead > j * tk - i * tq) for x, j in zip(raw, js)]
                sums_l = [_split_dot(l1m, u_le) for _, l1m, _ in terms]
                a, gg = [], []
                for (t, l1m, _), part, d in zip(terms, sums_l, da):
                    a.append(jnp.exp2((t + l1m) + (tot - (part + pre_l))))
                    gg.append(a[-1] * d)
                    pre_l = pre_l + jnp.sum(l1m, axis=1, keepdims=True)
                sums_g = [_split_dot(x, u_lt) for x in gg]
                dzs = []
                for (t, _, e), x, part in zip(terms, gg, sums_g):
                    big_g = part + pre_g
                    pre_g = pre_g + jnp.sum(x, axis=1, keepdims=True)
                    inv = 1.0 / (1.0 + e)
                    sig = jnp.where(t >= 0.0, inv, e * inv)
                    dzs.append(((x - sig * (x + big_g)) * scale).astype(BF16))
                for x, k in zip(dzs, ks):
                    dq = dq + _dot(x, k, NN)
                for x, y, c in zip(dzs, a, cols):
                    dk_s[c, :] += _dot(x, q, TN)
                    dv_s[c, :] += _dot(y.astype(BF16), do, TN)
                return dq, pre_l, pre_g

            zero = jnp.zeros((tq, 1), F32)
            dq, _, _ = lax.fori_loop(0, i + 1, kblocks, (jnp.zeros((tq, HEAD_DIM), F32), zero, zero))
            dq_ref[rows, :] = dq.astype(BF16)
            return 0

        lax.fori_loop(0, nq, qblock, 0)
        dk_ref[...] = dk_s[...].astype(BF16)
        dv_ref[...] = dv_s[...].astype(BF16)
        _fused_end(fused, grid, f_refs)

    return pl.pallas_call(
        body, name="sb_bwd", grid=grid,
        in_specs=[_seg_spec(s, 0), _seg_spec(s, 1), _seg_spec(s, 2), _seg_spec(s, 3),
                  _head_spec(s), _head_spec(s), _seg_spec(s, 0),
                  pl.BlockSpec((1, HEAD_DIM), lambda i, h: (0, h))] + f_in_specs,
        out_specs=[_seg4_spec(s, 0), pl.BlockSpec((None, 1, HEAD_DIM), lambda i, h: (i, 0, h))] + f_out_specs,
        out_shape=[jax.ShapeDtypeStruct((8, b, s, w), BF16), jax.ShapeDtypeStruct((b, 1, w), F32)] + f_out,
        scratch_shapes=[pltpu.VMEM((s, HEAD_DIM), BF16), pltpu.VMEM((s, HEAD_DIM), F32),
                        pltpu.VMEM((s, HEAD_DIM), F32)] + f_scratch,
        compiler_params=_cparams(("arbitrary", "arbitrary")),
    )(proj8, proj8, proj8, proj8, o_sb, tot_sb, dy2, g_sb, *f_in)


DIL_BLOCK = 128
DIL_GROUP = 4


def _dil_chunks(s, r):
    length = s // r
    out = []
    for rho in range(r):
        for cc in range(length // DIL_BLOCK):
            if r == 1:
                nat = pl.ds(cc * DIL_BLOCK, DIL_BLOCK)
            else:
                nat = pl.ds(rho + r * DIL_BLOCK * cc, DIL_BLOCK, stride=r)
            off = rho * length + cc * DIL_BLOCK
            out.append((nat, pl.ds(off, DIL_BLOCK), pl.ds(off + DIL_BLOCK, DIL_BLOCK)))
    return out


def _dil_masks(slope, r):
    n = DIL_BLOCK
    ri = lax.broadcasted_iota(jnp.int32, (n, 2 * n), 0)
    ci = lax.broadcasted_iota(jnp.int32, (n, 2 * n), 1)
    steps = ri - ci + n
    inside = (steps >= 0) & (steps <= n)
    bias = slope * (steps.astype(F32) * r)
    return jnp.where(inside, -bias, NEG_BIG), jnp.where(inside & (ci >= n), -bias, NEG_BIG)


def _dil_scores(q, k_pc, masks, first):
    return _dot(q, k_pc, NT) * (1.0 / math.sqrt(HEAD_DIM)) + jnp.where(first, masks[1], masks[0])


def _dil_check(s):
    assert (s // DIL_BLOCK) % DIL_GROUP == 0, s
    for window, r in DIL_PAIRS:
        assert window // r == DIL_BLOCK and s % (r * DIL_BLOCK) == 0, (s, window, r)


def _dil_fwd(proj8, g_dil, slopes, y2):
    _, b, s, w = proj8.shape
    n_heads = w // HEAD_DIM
    _dil_check(s)
    n = DIL_BLOCK
    nt = s // n

    def body(q_ref, k_ref, v_ref, zg_ref, g_ref, sl_ref, y_in, o_ref, lse_ref, y_ref,
             qp, kp, vp, pnum, pm, pl_, acc_s, m_s, l_s):
        del y_in
        slope = sl_ref[...][:, :1]
        kp[pl.ds(0, n), :] = jnp.zeros((n, HEAD_DIM), BF16)
        vp[pl.ds(0, n), :] = jnp.zeros((n, HEAD_DIM), BF16)

        for (window, r) in DIL_PAIRS:
            nb = (s // r) // n
            masks = _dil_masks(slope, float(r))
            for nat, per, padded in _dil_chunks(s, r):
                qp[per, :] = q_ref[nat, :].astype(BF16)
                kp[padded, :] = k_ref[nat, :].astype(BF16)
                vp[padded, :] = v_ref[nat, :].astype(BF16)
            num_t, m_t, l_t = (acc_s, m_s, l_s) if r == 1 else (pnum, pm, pl_)

            def tiles(tt, _):
                ts = [tt * DIL_GROUP + i for i in range(DIL_GROUP)]
                rows = [pl.ds(pl.multiple_of(t * n, n), n) for t in ts]
                both = [pl.ds(pl.multiple_of(t * n, n), 2 * n) for t in ts]
                sc = [_dil_scores(qp[rw, :], kp[bt, :], masks, lax.rem(t, nb) == 0)
                      for t, rw, bt in zip(ts, rows, both)]
                m = [jnp.max(x, axis=1, keepdims=True) for x in sc]
                p = [jnp.exp(x - mx) for x, mx in zip(sc, m)]
                num = [_dot(x.astype(BF16), vp[bt, :], NN) for x, bt in zip(p, both)]
                for rw, x, mx, nm in zip(rows, p, m, num):
                    num_t[rw, :] = nm
                    m_t[rw, :] = jnp.broadcast_to(mx, (n, HEAD_DIM))
                    l_t[rw, :] = jnp.broadcast_to(jnp.sum(x, axis=1, keepdims=True), (n, HEAD_DIM))
                return 0

            lax.fori_loop(0, nt // DIL_GROUP, tiles, 0)
            if r != 1:
                for nat, per, _ in _dil_chunks(s, r):
                    m_old, m_new_p = m_s[nat, :], pm[per, :]
                    m_new = jnp.maximum(m_old, m_new_p)
                    a_old, a_p = jnp.exp(m_old - m_new), jnp.exp(m_new_p - m_new)
                    m_s[nat, :] = m_new
                    l_s[nat, :] = l_s[nat, :] * a_old + pl_[per, :] * a_p
                    acc_s[nat, :] = acc_s[nat, :] * a_old + pnum[per, :] * a_p

        g = g_ref[...]

        def finish(t, _):
            rows = pl.ds(pl.multiple_of(t * n, n), n)
            l = l_s[rows, :]
            o = acc_s[rows, :] / l
            o_ref[rows, :] = o
            lse_ref[rows, :] = m_s[rows, :] + jnp.log(l)
            y_ref[rows, :] = _head_out(o, zg_ref[rows, :], g).astype(BF16)
            return 0

        lax.fori_loop(0, nt, finish, 0)

    f32_s = pltpu.VMEM((s, HEAD_DIM), F32)
    bf_s = pltpu.VMEM((s, HEAD_DIM), BF16)
    bf_pad = pltpu.VMEM((s + n, HEAD_DIM), BF16)
    return pl.pallas_call(
        body, name="dil_fwd", grid=(b, n_heads),
        in_specs=[_seg_spec(s, 4), _seg_spec(s, 5), _seg_spec(s, 6), _seg_spec(s, 7),
                  pl.BlockSpec((1, HEAD_DIM), lambda i, h: (0, h)),
                  pl.BlockSpec((None, 1, HEAD_DIM), lambda i, h: (h, 0, 0)), ANY],
        out_specs=[_head_spec(s), _head_spec(s), _seg_spec(s, 1)],
        out_shape=[jax.ShapeDtypeStruct((b, s, w), F32), jax.ShapeDtypeStruct((b, s, w), F32),
                   jax.ShapeDtypeStruct((2, b, s, w), BF16)],
        scratch_shapes=[bf_s, bf_pad, bf_pad, f32_s, f32_s, f32_s, f32_s, f32_s, f32_s],
        input_output_aliases={6: 2},
        compiler_params=_cparams(("parallel", "parallel")),
    )(proj8, proj8, proj8, proj8, g_dil, slopes, y2)


def _dil_bwd(proj8, o_dl, lse_dl, dy2, g_dil, slopes, dproj8):
    _, b, s, w = proj8.shape
    n_heads = w // HEAD_DIM
    _dil_check(s)
    n = DIL_BLOCK
    nt = s // n
    scale = 1.0 / math.sqrt(HEAD_DIM)

    def body(q_ref, k_ref, v_ref, zg_ref, o_ref, lse_ref, dy_ref, g_ref, sl_ref, dp_in, dp_ref, dg_ref,
             do_n, dt_n, dq_n, dk_n, dv_n, qp, kp, vp, dop, dtp, lsep, pdq, pdk, pdv):
        del dp_in
        dq_ref, dk_ref, dv_ref, dzg_ref = (dp_ref.at[i] for i in range(4))
        slope = sl_ref[...][:, :1]
        g = g_ref[...]

        def prologue(t, dg):
            rows = pl.ds(pl.multiple_of(t * n, n), n)
            o = o_ref[rows, :]
            do, dzg, dgi = _head_out_bwd(o, zg_ref[rows, :], g, dy_ref[rows, :])
            dzg_ref[rows, :] = dzg.astype(BF16)
            do_n[rows, :] = do
            dt_n[rows, :] = jnp.broadcast_to(jnp.sum(do * o, axis=-1, keepdims=True), (n, HEAD_DIM))
            return dg + dgi

        dg_ref[...] = lax.fori_loop(0, nt, prologue, jnp.zeros((1, HEAD_DIM), F32))
        dq_n[...] = jnp.zeros_like(dq_n)
        dk_n[...] = jnp.zeros_like(dk_n)
        dv_n[...] = jnp.zeros_like(dv_n)
        kp[pl.ds(0, n), :] = jnp.zeros((n, HEAD_DIM), BF16)
        vp[pl.ds(0, n), :] = jnp.zeros((n, HEAD_DIM), BF16)

        for (window, r) in DIL_PAIRS:
            nb = (s // r) // n
            masks = _dil_masks(slope, float(r))
            for nat, per, padded in _dil_chunks(s, r):
                qp[per, :] = q_ref[nat, :].astype(BF16)
                kp[padded, :] = k_ref[nat, :].astype(BF16)
                vp[padded, :] = v_ref[nat, :].astype(BF16)
                dop[per, :] = do_n[nat, :].astype(BF16)
                dtp[per, :] = dt_n[nat, :]
                lsep[per, :] = lse_ref[nat, :]
            pdk[...] = jnp.zeros_like(pdk)
            pdv[...] = jnp.zeros_like(pdv)

            def tiles(tt, _):
                ts = [tt * DIL_GROUP + i for i in range(DIL_GROUP)]
                rows = [pl.ds(pl.multiple_of(t * n, n), n) for t in ts]
                both = [pl.ds(pl.multiple_of(t * n, n), 2 * n) for t in ts]
                q = [qp[rw, :] for rw in rows]
                do = [dop[rw, :] for rw in rows]
                sc = [_dil_scores(qq, kp[bt, :], masks, lax.rem(t, nb) == 0) for t, qq, bt in zip(ts, q, both)]
                dp = [_dot(dd, vp[bt, :], NT) for dd, bt in zip(do, both)]
                p = [jnp.exp(x - lsep[rw, :][:, :1]) for x, rw in zip(sc, rows)]
                ds = [((x * (y - dtp[rw, :][:, :1])) * scale).astype(BF16) for x, y, rw in zip(p, dp, rows)]
                dq = [_dot(x, kp[bt, :], NN) for x, bt in zip(ds, both)]
                dk = [_dot(x, qq, TN) for x, qq in zip(ds, q)]
                dv = [_dot(x.astype(BF16), dd, TN) for x, dd in zip(p, do)]
                for rw, bt, x, y, z in zip(rows, both, dq, dk, dv):
                    pdq[rw, :] = x
                    pdk[bt, :] += y
                    pdv[bt, :] += z
                return 0

            lax.fori_loop(0, nt // DIL_GROUP, tiles, 0)
            for nat, per, padded in _dil_chunks(s, r):
                dq_n[nat, :] += pdq[per, :]
                dk_n[nat, :] += pdk[padded, :]
                dv_n[nat, :] += pdv[padded, :]

        dq_ref[...] = dq_n[...].astype(BF16)
        dk_ref[...] = dk_n[...].astype(BF16)
        dv_ref[...] = dv_n[...].astype(BF16)

    f32_s = pltpu.VMEM((s, HEAD_DIM), F32)
    f32_pad = pltpu.VMEM((s + n, HEAD_DIM), F32)
    bf_s = pltpu.VMEM((s, HEAD_DIM), BF16)
    bf_pad = pltpu.VMEM((s + n, HEAD_DIM), BF16)
    return pl.pallas_call(
        body, name="dil_bwd", grid=(b, n_heads),
        in_specs=[_seg_spec(s, 4), _seg_spec(s, 5), _seg_spec(s, 6), _seg_spec(s, 7),
                  _head_spec(s), _head_spec(s), _seg_spec(s, 1),
                  pl.BlockSpec((1, HEAD_DIM), lambda i, h: (0, h)),
                  pl.BlockSpec((None, 1, HEAD_DIM), lambda i, h: (h, 0, 0)), ANY],
        out_specs=[_seg4_spec(s, 1), pl.BlockSpec((None, 1, HEAD_DIM), lambda i, h: (i, 0, h))],
        out_shape=[jax.ShapeDtypeStruct((8, b, s, w), BF16), jax.ShapeDtypeStruct((b, 1, w), F32)],
        scratch_shapes=[f32_s] * 5 + [bf_s, bf_pad, bf_pad, bf_s] + [f32_s, f32_s, f32_s, f32_pad, f32_pad],
        input_output_aliases={9: 0},
        compiler_params=_cparams(("parallel", "parallel")),
    )(proj8, proj8, proj8, proj8, o_dl, lse_dl, dy2, g_dil, slopes, dproj8)


def _small_update(gathered, n_b, params, m, v):
    n_dev, _, width = gathered.shape

    def body(g_ref, p_ref, m_ref, v_ref, grad_ref, d_ref, nm_ref, nv_ref, loss_ref):
        for row in range(2):
            acc = None
            for dev in range(n_dev):
                for i in range(n_b):
                    term = g_ref[dev, pl.ds(row * n_b + i, 1), :]
                    acc = term if acc is None else acc + term
            grad_ref[pl.ds(row, 1), :] = acc
        loss = g_ref[0, pl.ds(2 * n_b, 1), pl.ds(0, 128)]
        for dev in range(1, n_dev):
            loss = loss + g_ref[dev, pl.ds(2 * n_b, 1), pl.ds(0, 128)]
        loss_ref[...] = loss
        d, nm, nv = _adamw_math(p_ref[...], grad_ref[...], m_ref[...], v_ref[...])
        d_ref[...] = d
        nm_ref[...] = nm
        nv_ref[...] = nv

    sds = jax.ShapeDtypeStruct((2, width), F32)
    return pl.pallas_call(
        body, name="small_update",
        in_specs=[VMEM_SPEC] * 4, out_specs=[VMEM_SPEC] * 5,
        out_shape=[sds, sds, sds, sds, jax.ShapeDtypeStruct((1, 128), F32)],
        compiler_params=_cparams(),
    )(gathered, params, m, v)


def _wada_update(c_t, dmod, w, m, v):
    d, bt = c_t.shape
    _, n = dmod.shape
    tr, tc = _tile(d, 512), _tile(n, 1024)

    def body(c_ref, dm_ref, w_ref, m_ref, v_ref, g_ref, d_ref, nm_ref, nv_ref):
        cv = c_ref[...]
        cs = (cv * _sigmoid(cv)).astype(BF16)
        grad = _dot(cs, dm_ref[...].astype(BF16), NN)
        g_ref[...] = grad
        dl, nm, nv = _adamw_math(w_ref[...], grad, m_ref[...], v_ref[...])
        d_ref[...] = dl
        nm_ref[...] = nm
        nv_ref[...] = nv

    spec = pl.BlockSpec((tr, tc), lambda i, j: (i, j))
    sds = jax.ShapeDtypeStruct((d, n), F32)
    return pl.pallas_call(
        body, name="wada_update", grid=(d // tr, n // tc),
        in_specs=[pl.BlockSpec((tr, bt), lambda i, j: (i, 0)), pl.BlockSpec((bt, tc), lambda i, j: (0, j)),
                  spec, spec, spec],
        out_specs=[spec] * 4, out_shape=[sds] * 4,
        compiler_params=_cparams(("parallel", "parallel")),
    )(c_t, dmod, w, m, v)


def _reduce_begin(gs, tag):
    ra = _sibling_half_swap(gs, "swap_" + tag)
    pa, own = _pair_sum(gs, ra, "pair_sum_" + tag)
    return _ScatterChips(pa), own


def _reduce_finish(rb, own, w, m, v, tag):
    half = _chip_sum(rb, own, "chip_sum_" + tag)
    return _adamw(w, _sibling_join(half, "join_" + tag), m, v, "adamw_" + tag)


def kernel(x, c, w_ada, b_ada, g_norm, w_in, g_sb, g_dil, w_out, g_final, loss_target, m_w_ada, m_b_ada, m_g_norm, m_w_in, m_g_sb, m_g_dil, m_w_out, m_g_final, v_w_ada, v_b_ada, v_g_norm, v_w_in, v_g_sb, v_g_dil, v_w_out, v_g_final):
    nb, s, d = x.shape
    t = nb * s
    na = w_ada.shape[2]
    cs = w_in.shape[2]
    w = cs // 2
    n_heads = w // HEAD_DIM
    r_out = w_out.shape[1]
    assert 2 * nb + 1 <= 8 and 2 * d + 2 * w <= 3 * d and N_CHIPS * na == 3 * d and N_CHIPS * r_out == 2 * w
    xi, yi, ci = _place()
    chip = 2 * xi + yi
    dev = 2 * chip + ci

    c_all = _allgather8(jnp.pad(c, ((0, 8 - nb), (0, 0))), "gather_c")
    c16 = c_all.reshape(N_DEV, 8, d)[:, :nb].reshape(N_DEV * nb, d)
    b_ada_shard = lax.dynamic_slice(b_ada, (0, chip * na), (1, na))
    mod_part = _mod_fwd(c16, w_ada[0], b_ada_shard)
    mod_all = _allgather8(mod_part, "gather_mod")
    mod_full = mod_all.reshape(N_CHIPS, 2, N_DEV * nb, na)[:, 0].transpose(1, 0, 2).reshape(N_DEV * nb, 3 * d)
    mod = lax.dynamic_slice(mod_full, (dev * nb, 0), (nb, 3 * d))
    shift, scale, gate = (mod[:, i * d:(i + 1) * d].reshape(nb, 1, d) for i in range(3))

    h = _norm_mod_fwd(x, g_norm, scale, shift)
    h2 = h.reshape(t, d)
    ws_in = _allgather_chips(_cast_bf16_slab(w_in[0], "cast_w_in"), "gather_w_in")
    ws_out = _allgather_chips(_cast_bf16_slab(w_out[0], "cast_w_out"), "gather_w_out")
    w_out_full = ws_out.reshape(2 * w, d)

    proj8 = _proj_fwd(h2, ws_in).reshape(8, nb, s, w)
    slopes = jnp.exp2(-ALIBI_MAX_BIAS * jnp.arange(1, n_heads + 1, dtype=F32) / n_heads)
    slopes = jnp.broadcast_to(slopes[:, None, None], (n_heads, 1, HEAD_DIM))
    o_sb, tot_sb, y2 = _sb_fwd(proj8, g_sb)
    o_dl, lse_dl, y2 = _dil_fwd(proj8, g_dil, slopes, y2)
    y2f = y2.reshape(2, t, w)
    out = _out_fwd(y2f, w_out_full)

    dx1, dout, dgate, dg_final, loss_part = _loss_head(
        x, out.reshape(nb, s, d), gate, g_final.reshape(1, d), loss_target)
    dout2 = dout.reshape(t, d)
    gs_out = _out_bwd_w(y2f, dout2).reshape(N_CHIPS, r_out, d)
    scatter_out, own_out = _reduce_begin(gs_out, "w_out")
    dy2 = _out_bwd_y(dout2, w_out_full).reshape(2, nb, s, w)
    dproj8, dg_sb, rb_out = _sb_bwd(proj8, o_sb, tot_sb, dy2, g_sb, fused=scatter_out)
    dproj8, dg_dl = _dil_bwd(proj8, o_dl, lse_dl, dy2, g_dil, slopes, dproj8)
    dproj8 = dproj8.reshape(8, t, w)
    gs_in = _proj_bwd_w(h2, dproj8)
    scatter_in, own_in = _reduce_begin(gs_in, "w_in")
    dh, rb_in = _proj_bwd_x(dproj8, ws_in, fused=scatter_in)
    grad_x, dshift, dscale, dg_norm = _norm_mod_bwd(x, dh.reshape(nb, s, d), dx1, g_norm, scale)

    width = 3 * d
    dmod = jnp.concatenate([dshift, dscale, dgate], axis=-1).reshape(nb, width)
    gains = jnp.concatenate([dg_sb.reshape(nb, w), dg_dl.reshape(nb, w)], axis=-1)
    gains = jnp.pad(gains, ((0, 0), (2 * d, width - 2 * d - 2 * w)))
    first = jnp.pad(jnp.concatenate([dg_norm, dg_final], axis=-1), ((0, nb - 1), (0, width - 2 * d)))
    loss_row = jnp.pad(loss_part, ((0, 0), (0, width - 128)))
    pack = jnp.concatenate([dmod, gains + first, loss_row, jnp.zeros((8 - 2 * nb - 1, width), F32)], axis=0)
    gathered = _allgather8(pack, "gather_small").reshape(N_DEV, 8, width)

    def stack(bias, gn, gf, gsb, gdl):
        row1 = jnp.concatenate([gn.reshape(1, d), gf.reshape(1, d), gsb.reshape(1, w), gdl.reshape(1, w)], axis=-1)
        return jnp.concatenate([bias.reshape(1, width), jnp.pad(row1, ((0, 0), (0, width - 2 * d - 2 * w)))], axis=0)

    small = _small_update(
        gathered, nb, stack(b_ada, g_norm, g_final, g_sb, g_dil),
        stack(m_b_ada, m_g_norm, m_g_final, m_g_sb, m_g_dil), stack(v_b_ada, v_g_norm, v_g_final, v_g_sb, v_g_dil))
    loss = small[4][0, 0]

    def unstack(a):
        return (a[0:1, :], a[1:2, 0:d], a[1, d:2 * d], a[1:2, 2 * d:2 * d + w], a[1:2, 2 * d + w:2 * d + 2 * w])

    (g_b, g_gn, g_gf, g_gsb, g_gdl), (d_b, d_gn, d_gf, d_gsb, d_gdl), (nm_b, nm_gn, nm_gf, nm_gsb, nm_gdl), \
        (nv_b, nv_gn, nv_gf, nv_gsb, nv_gdl) = (unstack(a) for a in small[:4])

    dmod_all = gathered[:, :nb].reshape(N_DEV * nb, width)
    dmod_cols = lax.dynamic_slice(dmod_all, (0, chip * na), (N_DEV * nb, na))
    g_wa, d_wa, nm_wa, nv_wa = _wada_update(c16.T, dmod_cols, w_ada[0], m_w_ada[0], v_w_ada[0])

    g_wi, d_wi, nm_wi, nv_wi = _reduce_finish(rb_in, own_in, w_in[0], m_w_in[0], v_w_in[0], "w_in")
    g_wo, d_wo, nm_wo, nv_wo = _reduce_finish(rb_out, own_out, w_out[0], m_w_out[0], v_w_out[0], "w_out")

    lead = lambda a: a[None]
    return (loss, grad_x,
            lead(g_wa), g_b, g_gn, lead(g_wi), g_gsb, g_gdl, lead(g_wo), g_gf,
            lead(d_wa), d_b, d_gn, lead(d_wi), d_gsb, d_gdl, lead(d_wo), d_gf,
            lead(nm_wa), nm_b, nm_gn, lead(nm_wi), nm_gsb, nm_gdl, lead(nm_wo), nm_gf,
            lead(nv_wa), nv_b, nv_gn, lead(nv_wi), nv_gsb, nv_gdl, lead(nv_wo), nv_gf)
```

```python
import functools
import math

import jax
import jax.numpy as jnp
from jax import lax
from jax.experimental import pallas as pl
from jax.experimental.pallas import tpu as pltpu

F32 = jnp.float32
BF16 = jnp.bfloat16
MESH = pl.DeviceIdType.MESH

HEAD_DIM = 128
EPS = 1e-6
DIL_PAIRS = ((128, 1), (512, 4), (2048, 16))
ALIBI_MAX_BIAS = 8.0
ADAM_LR = 0.001
ADAM_B1 = 0.9
ADAM_B2 = 0.999
ADAM_EPS = 1e-08
ADAM_WD = 0.01
ADAM_STEP = 10
N_CHIPS = 4
N_DEV = 8
VMEM_LIMIT_BYTES = 56 * 1024 * 1024
NEG_BIG = -1e30

NN = (((1,), (0,)), ((), ()))
NT = (((1,), (1,)), ((), ()))
TN = (((0,), (0,)), ((), ()))

ANY = pl.BlockSpec(memory_space=pl.ANY)
VMEM_SPEC = pl.BlockSpec(memory_space=pltpu.VMEM)


def _cparams(sem=None):
    return pltpu.CompilerParams(dimension_semantics=sem, vmem_limit_bytes=VMEM_LIMIT_BYTES)


def _tile(dim, pref):
    t = min(dim, pref)
    assert dim % t == 0, (dim, pref)
    return t


def _dot(a, b, dims):
    return lax.dot_general(a, b, dims, preferred_element_type=F32)


def _sigmoid(x):
    return 1.0 / (1.0 + jnp.exp(-x))


def _place():
    return lax.axis_index("x"), lax.axis_index("y"), lax.axis_index("c")


def _allgather8(x_shard, name):
    m_per, n = x_shard.shape

    def body(x_ref, out_ref, send_sems, recv_sems, local_sem):
        x, y, c = _place()
        me, sibling = (x, y, c), (x, y, 1 - c)
        chips = [(1 - x, y), (x, 1 - y), (1 - x, 1 - y)]

        def rows(px, py, pc):
            return out_ref.at[pl.ds((4 * px + 2 * py + pc) * m_per, m_per), :]

        def copy(k, block, to, src=None):
            return pltpu.make_async_remote_copy(
                src_ref=rows(*block) if src is None else src, dst_ref=rows(*block),
                send_sem=send_sems.at[k], recv_sem=recv_sems.at[k], device_id=to, device_id_type=MESH)

        mine = pltpu.make_async_copy(x_ref, rows(*me), local_sem)
        mine.start()
        first = [copy(0, me, sibling, src=x_ref)]
        first += [copy(1 + j, me, (*chip, c), src=x_ref) for j, chip in enumerate(chips)]
        for cp in first:
            cp.start()
        passed = [copy(4 + j, (*chip, c), sibling) for j, chip in enumerate(chips)]
        for j, chip in enumerate(chips):
            copy(1 + j, (*chip, c), me).wait_recv()
            passed[j].start()
        copy(0, sibling, me).wait_recv()
        for j, chip in enumerate(chips):
            copy(4 + j, (*chip, 1 - c), me).wait_recv()
        for cp in first + passed:
            cp.wait_send()
        mine.wait()

    return pl.pallas_call(
        body, name=name,
        out_shape=jax.ShapeDtypeStruct((N_DEV * m_per, n), x_shard.dtype),
        in_specs=[VMEM_SPEC], out_specs=VMEM_SPEC,
        scratch_shapes=[pltpu.SemaphoreType.DMA((7,)), pltpu.SemaphoreType.DMA((7,)), pltpu.SemaphoreType.DMA],
    )(x_shard)


def _allgather_chips(ws, name):
    _, r, cdim = ws.shape
    quarter = r // 4

    def body(in_ref, out_ref, send_sems, recv_sems):
        del in_ref
        x, y, c = _place()
        me, sibling = (x, y, c), (x, y, 1 - c)
        x_nbr, y_nbr = (1 - x, y), (x, 1 - y)

        def copy(k, chip, pc, part, to):
            rows = out_ref.at[2 * chip[0] + chip[1], pl.ds((2 * pc + part) * quarter, quarter), :]
            return pltpu.make_async_remote_copy(
                src_ref=rows, dst_ref=rows, send_sem=send_sems.at[k], recv_sem=recv_sems.at[k],
                device_id=to, device_id_type=MESH)

        own = [(0, (x, y), 0, x_nbr), (1, (x, y), 1, y_nbr), (2, (x, y), 1, x_nbr), (3, (x, y), 0, y_nbr)]
        sends = [copy(k, chip, c, part, (*to, c)) for k, chip, part, to in own]
        for cp in sends:
            cp.start()
        diag = (1 - x, 1 - y)
        landing = [(0, x_nbr, 0, (4, y_nbr)), (1, y_nbr, 1, (5, x_nbr)), (2, x_nbr, 1, None), (3, y_nbr, 0, None),
                   (4, diag, 0, None), (5, diag, 1, None)]
        for k, chip, part, onward in landing:
            copy(k, chip, c, part, me).wait_recv()
            if onward is not None:
                sends.append(copy(onward[0], chip, c, part, (*onward[1], c)))
                sends[-1].start()
            sends.append(copy(6 + k, chip, c, part, sibling))
            sends[-1].start()
        for k, chip, part, _ in landing:
            copy(6 + k, chip, 1 - c, part, me).wait_recv()
        for cp in sends:
            cp.wait_send()

    return pl.pallas_call(
        body, name=name,
        out_shape=jax.ShapeDtypeStruct(ws.shape, ws.dtype),
        in_specs=[ANY], out_specs=ANY, input_output_aliases={0: 0},
        scratch_shapes=[pltpu.SemaphoreType.DMA((12,)), pltpu.SemaphoreType.DMA((12,))],
    )(ws)


def _sibling_half_swap(gs, name):
    n, r, cdim = gs.shape
    half = r // 2

    def body(g_ref, out_ref, send_sem, recv_sem):
        x, y, c = _place()
        cp = pltpu.make_async_remote_copy(
            src_ref=g_ref.at[:, pl.ds((1 - c) * half, half), :], dst_ref=out_ref,
            send_sem=send_sem, recv_sem=recv_sem, device_id=(x, y, 1 - c), device_id_type=MESH)
        cp.start()
        cp.wait()

    return pl.pallas_call(
        body, name=name,
        out_shape=jax.ShapeDtypeStruct((n, half, cdim), gs.dtype),
        in_specs=[ANY], out_specs=ANY,
        scratch_shapes=[pltpu.SemaphoreType.DMA, pltpu.SemaphoreType.DMA],
    )(gs)


class _ScatterChips:
    def __init__(self, pa):
        self.inputs = [pa]
        self.out_shapes = [jax.ShapeDtypeStruct(pa.shape, pa.dtype)]
        self.scratch = [pltpu.SemaphoreType.DMA((3,)), pltpu.SemaphoreType.DMA((3,)), pltpu.SemaphoreType.DMA]

    @staticmethod
    def _mine(p_ref, out_ref, send_sems, recv_sems, local_sem):
        x, y, _ = _place()
        return pltpu.make_async_copy(p_ref.at[2 * x + y], out_ref.at[2 * x + y], local_sem)

    @staticmethod
    def _remote(p_ref, out_ref, send_sems, recv_sems, local_sem, incoming):
        x, y, c = _place()
        me = 2 * x + y
        remote = []
        for j, (px, py) in enumerate([(1 - x, y), (x, 1 - y), (1 - x, 1 - y)]):
            remote.append(pltpu.make_async_remote_copy(
                src_ref=p_ref.at[me if incoming else 2 * px + py], dst_ref=out_ref.at[2 * px + py if incoming else me],
                send_sem=send_sems.at[j], recv_sem=recv_sems.at[j], device_id=(px, py, c), device_id_type=MESH))
        return remote

    def start(self, *refs):
        self._mine(*refs).start()
        for cp in self._remote(*refs, incoming=False):
            cp.start()

    def wait(self, *refs):
        for cp in self._remote(*refs, incoming=True):
            cp.wait_recv()
        for cp in self._remote(*refs, incoming=False):
            cp.wait_send()
        self._mine(*refs).wait()


def _fused_specs(fused):
    if fused is None:
        return [], [], [], [], []
    return (list(fused.inputs), [ANY] * len(fused.inputs), list(fused.out_shapes), [ANY] * len(fused.out_shapes),
            list(fused.scratch))


def _fused_begin(fused, grid, refs):
    if fused is not None:
        first = functools.reduce(lambda p, q: p & q, [pl.program_id(i) == 0 for i in range(len(grid))])
        pl.when(first)(lambda: fused.start(*refs))


def _fused_end(fused, grid, refs):
    if fused is not None:
        last = functools.reduce(lambda p, q: p & q, [pl.program_id(i) == g - 1 for i, g in enumerate(grid)])
        pl.when(last)(lambda: fused.wait(*refs))


def _sibling_join(full, name):
    h2, cdim = full.shape
    h = h2 // 2

    def body(in_ref, out_ref, send_sem, recv_sem):
        del in_ref
        x, y, c = _place()
        mine = out_ref.at[pl.ds(c * h, h), :]
        cp = pltpu.make_async_remote_copy(
            src_ref=mine, dst_ref=mine, send_sem=send_sem, recv_sem=recv_sem,
            device_id=(x, y, 1 - c), device_id_type=MESH)
        cp.start()
        theirs = out_ref.at[pl.ds((1 - c) * h, h), :]
        pltpu.make_async_remote_copy(
            src_ref=theirs, dst_ref=theirs, send_sem=send_sem, recv_sem=recv_sem,
            device_id=(x, y, 1 - c), device_id_type=MESH).wait_recv()
        cp.wait_send()

    return pl.pallas_call(
        body, name=name,
        out_shape=jax.ShapeDtypeStruct(full.shape, full.dtype),
        in_specs=[ANY], out_specs=ANY, input_output_aliases={0: 0},
        scratch_shapes=[pltpu.SemaphoreType.DMA, pltpu.SemaphoreType.DMA],
    )(full)


def _cast_bf16_slab(w, name):
    r, cdim = w.shape
    tr, tc = _tile(r, 512), _tile(cdim, 2048)

    def body(pc_ref, w_ref, o_ref):
        o_ref[...] = w_ref[...].astype(BF16)

    return pl.pallas_call(
        body, name=name,
        grid_spec=pltpu.PrefetchScalarGridSpec(
            num_scalar_prefetch=1, grid=(r // tr, cdim // tc),
            in_specs=[pl.BlockSpec((tr, tc), lambda i, j, pc: (i, j))],
            out_specs=pl.BlockSpec((None, tr, tc), lambda i, j, pc: (pc[1], i, j))),
        out_shape=jax.ShapeDtypeStruct((N_CHIPS, r, cdim), BF16),
        compiler_params=_cparams(("parallel", "parallel")),
    )(_place_scalars(), w)


def _place_scalars():
    x, y, c = _place()
    return jnp.stack([c, 2 * x + y]).astype(jnp.int32)


def _pair_sum(gs, ra, name):
    n, r, cdim = gs.shape
    half = r // 2
    tr, tc = _tile(half, 512), _tile(cdim, 2048)
    nt = half // tr

    def body(pc_ref, g_ref, r_ref, o_ref, own_ref):
        val = g_ref[...].astype(F32) + r_ref[...].astype(F32)
        o_ref[...] = val.astype(BF16)

        @pl.when(pl.program_id(2) == pc_ref[1])
        def _():
            own_ref[...] = val

    return pl.pallas_call(
        body, name=name,
        grid_spec=pltpu.PrefetchScalarGridSpec(
            num_scalar_prefetch=1, grid=(nt, cdim // tc, n),
            in_specs=[pl.BlockSpec((None, tr, tc), lambda i, j, s, pc: (s, pc[0] * nt + i, j)),
                      pl.BlockSpec((None, tr, tc), lambda i, j, s, pc: (s, i, j))],
            out_specs=[pl.BlockSpec((None, tr, tc), lambda i, j, s, pc: (s, i, j)),
                       pl.BlockSpec((tr, tc), lambda i, j, s, pc: (i, j))]),
        out_shape=[jax.ShapeDtypeStruct((n, half, cdim), BF16), jax.ShapeDtypeStruct((half, cdim), F32)],
        compiler_params=_cparams(("parallel", "parallel", "arbitrary")),
    )(_place_scalars(), gs, ra)


def _chip_sum(rb, own, name):
    n, h, cdim = rb.shape
    tr, tc = _tile(h, 256), _tile(cdim, 2048)
    nt = h // tr

    def body(pc_ref, r_ref, own_ref, o_ref):
        chip = pc_ref[1]
        acc = None
        for p in range(n):
            term = jnp.where(chip == p, own_ref[...], r_ref[p].astype(F32))
            acc = term if acc is None else acc + term
        o_ref[...] = acc

    return pl.pallas_call(
        body, name=name,
        grid_spec=pltpu.PrefetchScalarGridSpec(
            num_scalar_prefetch=1, grid=(nt, cdim // tc),
            in_specs=[pl.BlockSpec((n, tr, tc), lambda i, j, pc: (0, i, j)),
                      pl.BlockSpec((tr, tc), lambda i, j, pc: (i, j))],
            out_specs=pl.BlockSpec((tr, tc), lambda i, j, pc: (pc[0] * nt + i, j))),
        out_shape=jax.ShapeDtypeStruct((2 * h, cdim), F32),
        compiler_params=_cparams(("parallel", "parallel")),
    )(_place_scalars(), rb, own)


def _adamw_math(w, g, m, v):
    m = ADAM_B1 * m + (1.0 - ADAM_B1) * g
    v = ADAM_B2 * v + (1.0 - ADAM_B2) * (g * g)
    m_hat = m / (1.0 - ADAM_B1 ** ADAM_STEP)
    v_hat = v / (1.0 - ADAM_B2 ** ADAM_STEP)
    delta = -ADAM_LR * (m_hat / (jnp.sqrt(v_hat) + ADAM_EPS) + ADAM_WD * w)
    return delta, m, v


def _adamw(w, g, m, v, name):
    r, cdim = w.shape
    tr, tc = _tile(r, 256), _tile(cdim, 2048)

    def body(w_ref, g_ref, m_ref, v_ref, go_ref, d_ref, nm_ref, nv_ref):
        gv = g_ref[...]
        d, nm, nv = _adamw_math(w_ref[...], gv, m_ref[...], v_ref[...])
        go_ref[...] = gv
        d_ref[...] = d
        nm_ref[...] = nm
        nv_ref[...] = nv

    spec = pl.BlockSpec((tr, tc), lambda i, j: (i, j))
    sds = jax.ShapeDtypeStruct((r, cdim), F32)
    return pl.pallas_call(
        body, name=name, grid=(r // tr, cdim // tc),
        in_specs=[spec] * 4, out_specs=[spec] * 4, out_shape=[sds] * 4,
        compiler_params=_cparams(("parallel", "parallel")),
    )(w, g, m, v)


def _matmul(a, b, *, grid, a_spec, b_spec, out_spec, out_shape, acc_shape, dims, name, bias=None, bias_spec=None,
            silu_a=False, fused=None):
    nk = grid[2]
    f_in, f_in_specs, f_out, f_out_specs, f_scratch = _fused_specs(fused)
    n_in = 2 + (bias is not None)

    acc_scratch = [pltpu.VMEM(acc_shape, F32)] if nk > 1 else []

    def body(*refs):
        a_ref, b_ref = refs[:2]
        bias_ref = refs[2] if bias is not None else None
        o_ref = refs[n_in + len(f_in)]
        n_fixed = n_in + len(f_in) + 1 + len(f_out)
        f_refs = (*refs[n_in:n_in + len(f_in)], *refs[n_in + len(f_in) + 1:n_fixed],
                  *refs[n_fixed + len(acc_scratch):])
        _fused_begin(fused, grid, f_refs)

        def product():
            if len(a_ref.shape) == 3:
                tks = a_ref.shape[2]
                parts = [_dot(a_ref[i], b_ref[:, i * tks:(i + 1) * tks], dims) for i in range(a_ref.shape[0])]
                return functools.reduce(lambda p, q: p + q, parts)
            av = a_ref[...]
            if silu_a:
                av = av * _sigmoid(av)
            return _dot(av.astype(BF16), b_ref[...].astype(BF16), dims)

        def finish(res):
            if bias is not None:
                res = res + bias_ref[...]
            o_ref[...] = res.astype(o_ref.dtype)

        if nk == 1:
            finish(product())
        else:
            acc_ref = refs[n_fixed]
            k = pl.program_id(2)

            @pl.when(k == 0)
            def _():
                acc_ref[...] = product()

            if nk > 2:
                @pl.when((k > 0) & (k < nk - 1))
                def _():
                    acc_ref[...] += product()

            @pl.when(k == nk - 1)
            def _():
                finish(acc_ref[...] + product())

        _fused_end(fused, grid, f_refs)

    in_specs = [a_spec, b_spec] + ([] if bias is None else [bias_spec]) + f_in_specs
    args = (a, b) + (() if bias is None else (bias,)) + tuple(f_in)
    sem = ("parallel", "parallel", "arbitrary") if fused is None else ("arbitrary",) * 3
    res = pl.pallas_call(
        body, name=name, grid=grid, in_specs=in_specs, out_specs=[out_spec] + f_out_specs,
        out_shape=[out_shape] + f_out,
        scratch_shapes=acc_scratch + f_scratch,
        compiler_params=_cparams(sem),
    )(*args)
    return res[0] if fused is None else tuple(res)


def _mm_tiles(m, n, k):
    return _tile(m, 1024), _tile(n, 1024), _tile(k, 4096)


def _proj_fwd(h2, ws_in):
    t, d = h2.shape
    _, _, cs = ws_in.shape
    w = cs // 2
    tm, tn, tk = _mm_tiles(t, w, d)
    nps, npseg = cs // tn, w // tn
    return _matmul(
        h2, ws_in, grid=(t // tm, 8 * npseg, d // tk), dims=NN, name="proj_fwd",
        a_spec=pl.BlockSpec((tm, tk), lambda m, n, k: (m, k)),
        b_spec=pl.BlockSpec((None, tk, tn), lambda m, n, k: (n // nps, k, n % nps)),
        out_spec=pl.BlockSpec((None, tm, tn), lambda m, n, k: (n // npseg, m, n % npseg)),
        out_shape=jax.ShapeDtypeStruct((8, t, w), F32), acc_shape=(tm, tn))


def _proj_bwd_x(dproj8, ws_in, fused=None):
    _, t, w = dproj8.shape
    _, d, cs = ws_in.shape
    tm, tn, _ = _mm_tiles(t, d, w)
    return _matmul(
        dproj8, ws_in, grid=(t // tm, d // tn, N_CHIPS), dims=NT, name="proj_bwd_x", fused=fused,
        a_spec=pl.BlockSpec((2, tm, w), lambda m, n, k: (k, m, 0)),
        b_spec=pl.BlockSpec((None, tn, cs), lambda m, n, k: (k, n, 0)),
        out_spec=pl.BlockSpec((tm, tn), lambda m, n, k: (m, n)),
        out_shape=jax.ShapeDtypeStruct((t, d), F32), acc_shape=(tm, tn))


def _proj_bwd_w(h2, dproj8):
    t, d = h2.shape
    _, _, w = dproj8.shape
    cs = 2 * w
    tm, tn, tk = _mm_tiles(d, w, t)
    nps, npseg = cs // tn, w // tn
    return _matmul(
        h2, dproj8, grid=(d // tm, 8 * npseg, t // tk), dims=TN, name="proj_bwd_w",
        a_spec=pl.BlockSpec((tk, tm), lambda m, n, k: (k, m)),
        b_spec=pl.BlockSpec((None, tk, tn), lambda m, n, k: (n // npseg, k, n % npseg)),
        out_spec=pl.BlockSpec((None, tm, tn), lambda m, n, k: (n // nps, m, n % nps)),
        out_shape=jax.ShapeDtypeStruct((N_CHIPS, d, cs), BF16), acc_shape=(tm, tn))


def _out_fwd(y2, w_out):
    _, t, w = y2.shape
    _, d = w_out.shape
    tm, tn, tk = _mm_tiles(t, d, w)
    kpg = w // tk
    return _matmul(
        y2, w_out, grid=(t // tm, d // tn, 2 * kpg), dims=NN, name="out_fwd",
        a_spec=pl.BlockSpec((None, tm, tk), lambda m, n, k: (k // kpg, m, k % kpg)),
        b_spec=pl.BlockSpec((tk, tn), lambda m, n, k: (k, n)),
        out_spec=pl.BlockSpec((tm, tn), lambda m, n, k: (m, n)),
        out_shape=jax.ShapeDtypeStruct((t, d), F32), acc_shape=(tm, tn))


def _out_bwd_y(dout, w_out):
    t, d = dout.shape
    w = w_out.shape[0] // 2
    tm, tn, tk = _mm_tiles(t, w, d)
    npg = w // tn
    return _matmul(
        dout, w_out, grid=(t // tm, 2 * npg, d // tk), dims=NT, name="out_bwd_y",
        a_spec=pl.BlockSpec((tm, tk), lambda m, n, k: (m, k)),
        b_spec=pl.BlockSpec((tn, tk), lambda m, n, k: (n, k)),
        out_spec=pl.BlockSpec((None, tm, tn), lambda m, n, k: (n // npg, m, n % npg)),
        out_shape=jax.ShapeDtypeStruct((2, t, w), F32), acc_shape=(tm, tn))


def _out_bwd_w(y2, dout):
    _, t, w = y2.shape
    _, d = dout.shape
    tm, tn, tk = _mm_tiles(w, d, t)
    mpg = w // tm
    return _matmul(
        y2, dout, grid=(2 * mpg, d // tn, t // tk), dims=TN, name="out_bwd_w",
        a_spec=pl.BlockSpec((None, tk, tm), lambda m, n, k: (m // mpg, k, m % mpg)),
        b_spec=pl.BlockSpec((tk, tn), lambda m, n, k: (k, n)),
        out_spec=pl.BlockSpec((tm, tn), lambda m, n, k: (m, n)),
        out_shape=jax.ShapeDtypeStruct((2 * w, d), BF16), acc_shape=(tm, tn))


def _mod_fwd(c_all, w_ada, b_ada):
    bt, d = c_all.shape
    _, n = w_ada.shape
    tn, tk = _tile(n, 512), _tile(d, 1024)
    return _matmul(
        c_all, w_ada, grid=(1, n // tn, d // tk), dims=NN, name="mod_fwd", silu_a=True,
        a_spec=pl.BlockSpec((bt, tk), lambda i, j, l: (0, l)),
        b_spec=pl.BlockSpec((tk, tn), lambda i, j, l: (l, j)),
        bias=b_ada, bias_spec=pl.BlockSpec((1, tn), lambda i, j, l: (0, j)),
        out_spec=pl.BlockSpec((bt, tn), lambda i, j, l: (0, j)),
        out_shape=jax.ShapeDtypeStruct((bt, n), F32), acc_shape=(bt, tn))


def _norm_mod_fwd(x, g_norm, scale, shift):
    b, s, d = x.shape
    ts = _tile(s, 256)

    def body(x_ref, g_ref, sc_ref, sh_ref, h_ref):
        xv = x_ref[...]
        r = lax.rsqrt(jnp.mean(xv * xv, axis=-1, keepdims=True) + EPS)
        y = (xv * r) * g_ref[...]
        h_ref[...] = (y * (1.0 + sc_ref[...]) + sh_ref[...]).astype(BF16)

    row = pl.BlockSpec((None, ts, d), lambda i, j: (i, j, 0))
    per_b = pl.BlockSpec((None, 1, d), lambda i, j: (i, 0, 0))
    return pl.pallas_call(
        body, name="norm_mod_fwd", grid=(b, s // ts),
        in_specs=[row, pl.BlockSpec((1, d), lambda i, j: (0, 0)), per_b, per_b],
        out_specs=row, out_shape=jax.ShapeDtypeStruct((b, s, d), BF16),
        compiler_params=_cparams(("parallel", "parallel")),
    )(x, g_norm, scale, shift)


def _norm_mod_bwd(x, dh, dx1, g_norm, scale):
    b, s, d = x.shape
    ts = _tile(s, 256)

    def body(x_ref, dh_ref, dx1_ref, g_ref, sc_ref, gx_ref, dsh_ref, dsc_ref, dg_ref):
        i, j = pl.program_id(0), pl.program_id(1)

        @pl.when(j == 0)
        def _():
            dsh_ref[...] = jnp.zeros_like(dsh_ref)
            dsc_ref[...] = jnp.zeros_like(dsc_ref)

        @pl.when((i == 0) & (j == 0))
        def _():
            dg_ref[...] = jnp.zeros_like(dg_ref)

        xv, dhv, g = x_ref[...], dh_ref[...], g_ref[...]
        r = lax.rsqrt(jnp.mean(xv * xv, axis=-1, keepdims=True) + EPS)
        xh = xv * r
        dsh_ref[...] += jnp.sum(dhv, axis=0, keepdims=True)
        dsc_ref[...] += jnp.sum(dhv * (xh * g), axis=0, keepdims=True)
        dn = dhv * (1.0 + sc_ref[...])
        dg_ref[...] += jnp.sum(dn * xh, axis=0, keepdims=True)
        u = dn * g
        dx = r * u - xv * (r * r * r) * jnp.mean(u * xv, axis=-1, keepdims=True)
        gx_ref[...] = dx1_ref[...] + dx

    row = pl.BlockSpec((None, ts, d), lambda i, j: (i, j, 0))
    per_b = pl.BlockSpec((None, 1, d), lambda i, j: (i, 0, 0))
    vec = pl.BlockSpec((1, d), lambda i, j: (0, 0))
    return pl.pallas_call(
        body, name="norm_mod_bwd", grid=(b, s // ts),
        in_specs=[row, row, row, vec, per_b],
        out_specs=[row, per_b, per_b, vec],
        out_shape=[jax.ShapeDtypeStruct((b, s, d), F32), jax.ShapeDtypeStruct((b, 1, d), F32),
                   jax.ShapeDtypeStruct((b, 1, d), F32), jax.ShapeDtypeStruct((1, d), F32)],
        compiler_params=_cparams(("arbitrary", "arbitrary")),
    )(x, dh, dx1, g_norm, scale)


def _loss_head(x, out, gate, g_final, target):
    b, s, d = x.shape
    ts = _tile(s, 256)

    def body(x_ref, o_ref, gt_ref, g_ref, t_ref, dx1_ref, dout_ref, dgt_ref, dg_ref, loss_ref):
        i, j = pl.program_id(0), pl.program_id(1)

        @pl.when(j == 0)
        def _():
            dgt_ref[...] = jnp.zeros_like(dgt_ref)

        @pl.when((i == 0) & (j == 0))
        def _():
            dg_ref[...] = jnp.zeros_like(dg_ref)
            loss_ref[...] = jnp.zeros_like(loss_ref)

        ov, gt, g = o_ref[...], gt_ref[...], g_ref[...]
        x1 = x_ref[...] + gt * ov
        r = lax.rsqrt(jnp.mean(x1 * x1, axis=-1, keepdims=True) + EPS)
        xh = x1 * r
        err = xh * g - t_ref[...]
        loss_ref[...] += 0.5 * jnp.sum(jnp.mean(err * err, axis=-1, keepdims=True))
        dfin = err * (1.0 / d)
        dg_ref[...] += jnp.sum(dfin * xh, axis=0, keepdims=True)
        u = dfin * g
        dx1 = r * u - x1 * (r * r * r) * jnp.mean(u * x1, axis=-1, keepdims=True)
        dx1_ref[...] = dx1
        dgt_ref[...] += jnp.sum(dx1 * ov, axis=0, keepdims=True)
        dout_ref[...] = (gt * dx1).astype(BF16)

    row = pl.BlockSpec((None, ts, d), lambda i, j: (i, j, 0))
    per_b = pl.BlockSpec((None, 1, d), lambda i, j: (i, 0, 0))
    vec = pl.BlockSpec((1, d), lambda i, j: (0, 0))
    return pl.pallas_call(
        body, name="loss_head", grid=(b, s // ts),
        in_specs=[row, row, per_b, vec, row],
        out_specs=[row, row, per_b, vec, pl.BlockSpec((1, 128), lambda i, j: (0, 0))],
        out_shape=[jax.ShapeDtypeStruct((b, s, d), F32), jax.ShapeDtypeStruct((b, s, d), BF16),
                   jax.ShapeDtypeStruct((b, 1, d), F32), jax.ShapeDtypeStruct((1, d), F32),
                   jax.ShapeDtypeStruct((1, 128), F32)],
        compiler_params=_cparams(("arbitrary", "arbitrary")),
    )(x, out, gate, g_final, target)


def _head_out(o, zg, g):
    rinv = lax.rsqrt(jnp.mean(o * o, axis=-1, keepdims=True) + EPS)
    return ((o * rinv) * g) * (zg * _sigmoid(zg))


def _head_out_bwd(o, zg, g, dy):
    rinv = lax.rsqrt(jnp.mean(o * o, axis=-1, keepdims=True) + EPS)
    rn = o * rinv
    sg = _sigmoid(zg)
    sil = zg * sg
    dzg = dy * (rn * g) * (sg * (1.0 + zg * (1.0 - sg)))
    dg = jnp.sum(dy * rn * sil, axis=0, keepdims=True)
    drn = dy * g * sil
    do = rinv * drn - o * (rinv * rinv * rinv) * jnp.mean(drn * o, axis=-1, keepdims=True)
    return do, dzg, dg


def _head_spec(s):
    return pl.BlockSpec((None, s, HEAD_DIM), lambda b, h: (b, 0, h))


def _seg_spec(s, seg):
    return pl.BlockSpec((None, None, s, HEAD_DIM), lambda b, h: (seg, b, 0, h))


def _seg4_spec(s, group):
    return pl.BlockSpec((4, None, s, HEAD_DIM), lambda b, h: (group, b, 0, h))


SB_Q_BLOCK = 512
SB_K_BLOCK = 256


SB_DEAD_LOG2 = -160.0
LOG2_E = 1.4426950408889634
SB_LOGIT_SCALE = LOG2_E / math.sqrt(HEAD_DIM)


def _sb_terms(raw, valid):
    t = jnp.where(valid, raw * SB_LOGIT_SCALE, NEG_BIG)
    e = jnp.exp2(-jnp.abs(t))
    l1m = -(jnp.maximum(t, 0.0) + jnp.log2(1.0 + e))
    return t, l1m, e


def _split_dot(a, u):
    hi = a.astype(BF16)
    lo = (a - hi.astype(F32)).astype(BF16)
    return _dot(hi, u, NN) + _dot(lo, u, NN)


def _sb_fwd(proj8, g_sb):
    _, b, s, w = proj8.shape
    n_heads = w // HEAD_DIM
    tq, tk = _tile(s, SB_Q_BLOCK), _tile(s, SB_K_BLOCK)
    nq, kpq = s // tq, tq // tk
    scale = 1.0 / math.sqrt(HEAD_DIM)

    def body(q_ref, k_ref, v_ref, zg_ref, g_ref, o_ref, tot_ref, y_ref, trips_ref):
        u_excl = (lax.broadcasted_iota(jnp.int32, (tk, tk), 0)
                  > lax.broadcasted_iota(jnp.int32, (tk, tk), 1)).astype(BF16)
        ahead = lax.broadcasted_iota(jnp.int32, (tq, tk), 0) - lax.broadcasted_iota(jnp.int32, (tq, tk), 1)
        g = g_ref[...]

        def qblock(i, _):
            rows = pl.ds(pl.multiple_of(i * tq, tq), tq)
            q = q_ref[rows, :].astype(BF16)
            nk = (i + 1) * kpq

            def alive(state):
                jj, _, csum = state
                return (jj <= i) & ((jj == 0) | (jnp.max(csum) > SB_DEAD_LOG2))

            def kblocks(state):
                jj, acc, csum = state
                js = [nk - 1 - (jj * kpq + n) for n in range(kpq)]
                cols = [pl.ds(pl.multiple_of(j * tk, tk), tk) for j in js]
                raw = [_dot(q, k_ref[c, :].astype(BF16), NT) for c in cols]
                terms = [_sb_terms(x, ahead > j * tk - i * tq) for x, j in zip(raw, js)]
                sums = [_split_dot(l1m, u_excl) for _, l1m, _ in terms]
                for (t, l1m, _), part, c in zip(terms, sums, cols):
                    a = jnp.exp2((t + l1m) + (part + csum))
                    acc = acc + _dot(a.astype(BF16), v_ref[c, :].astype(BF16), NN)
                    csum = csum + jnp.sum(l1m, axis=1, keepdims=True)
                return jj + 1, acc, csum

            trips, acc, tot = lax.while_loop(
                alive, kblocks, (jnp.int32(0), jnp.zeros((tq, HEAD_DIM), F32), jnp.zeros((tq, 1), F32)))
            o_ref[rows, :] = acc
            tot_ref[rows, :] = jnp.broadcast_to(tot, (tq, HEAD_DIM))
            y_ref[rows, :] = _head_out(acc, zg_ref[rows, :], g).astype(BF16)
            trips_ref[0, i] = trips.astype(F32)
            return 0

        lax.fori_loop(0, nq, qblock, 0)

    return pl.pallas_call(
        body, name="sb_fwd", grid=(b, n_heads),
        in_specs=[_seg_spec(s, 0), _seg_spec(s, 1), _seg_spec(s, 2), _seg_spec(s, 3),
                  pl.BlockSpec((1, HEAD_DIM), lambda i, h: (0, h))],
        out_specs=[_head_spec(s), _head_spec(s), _seg_spec(s, 0),
                   pl.BlockSpec((None, None, 1, nq), lambda i, h: (i, h, 0, 0), memory_space=pltpu.SMEM)],
        out_shape=[jax.ShapeDtypeStruct((b, s, w), F32), jax.ShapeDtypeStruct((b, s, w), F32),
                   jax.ShapeDtypeStruct((2, b, s, w), BF16), jax.ShapeDtypeStruct((b, n_heads, 1, nq), F32)],
        compiler_params=_cparams(("parallel", "parallel")),
    )(proj8, proj8, proj8, proj8, g_sb)


def _sb_bwd(proj8, o_sb, tot_sb, trips, dy2, g_sb, fused=None):
    _, b, s, w = proj8.shape
    n_heads = w // HEAD_DIM
    tq, tk = _tile(s, SB_Q_BLOCK), _tile(s, SB_K_BLOCK)
    nq, kpq = s // tq, tq // tk
    scale = 1.0 / math.sqrt(HEAD_DIM)

    f_in, f_in_specs, f_out, f_out_specs, f_scratch = _fused_specs(fused)
    grid = (b, n_heads)

    def body(*refs):
        q_ref, k_ref, v_ref, zg_ref, o_ref, tot_ref, dy_ref, g_ref, trips_ref = refs[:9]
        dp_ref, dg_ref = refs[9 + len(f_in):11 + len(f_in)]
        do_s, dk_s, dv_s = refs[11 + len(f_in) + len(f_out):14 + len(f_in) + len(f_out)]
        f_refs = (*refs[9:9 + len(f_in)], *refs[11 + len(f_in):11 + len(f_in) + len(f_out)],
                  *refs[14 + len(f_in) + len(f_out):])
        _fused_begin(fused, grid, f_refs)
        dq_ref, dk_ref, dv_ref, dzg_ref = (dp_ref.at[n] for n in range(4))
        ri = lax.broadcasted_iota(jnp.int32, (tk, tk), 0)
        ci = lax.broadcasted_iota(jnp.int32, (tk, tk), 1)
        u_le = (ri <= ci).astype(BF16)
        u_lt = (ri < ci).astype(BF16)
        ahead = lax.broadcasted_iota(jnp.int32, (tq, tk), 0) - lax.broadcasted_iota(jnp.int32, (tq, tk), 1)
        g = g_ref[...]

        def prologue(i, dg):
            rows = pl.ds(pl.multiple_of(i * tq, tq), tq)
            do, dzg, dgi = _head_out_bwd(o_ref[rows, :], zg_ref[rows, :], g, dy_ref[rows, :])
            dzg_ref[rows, :] = dzg.astype(BF16)
            do_s[rows, :] = do.astype(BF16)
            return dg + dgi

        dg_ref[...] = lax.fori_loop(0, nq, prologue, jnp.zeros((1, HEAD_DIM), F32))
        dk_s[...] = jnp.zeros_like(dk_s)
        dv_s[...] = jnp.zeros_like(dv_s)

        def qblock(i, _):
            rows = pl.ds(pl.multiple_of(i * tq, tq), tq)
            q = q_ref[rows, :].astype(BF16)
            do = do_s[rows, :]
            tot = tot_ref[rows, :][:, :1]

            def kblocks(jj, carry):
                dq, pre_l, pre_g = carry
                js = [jj * kpq + n for n in range(kpq)]
                cols = [pl.ds(pl.multiple_of(j * tk, tk), tk) for j in js]
                ks = [k_ref[c, :].astype(BF16) for c in cols]
                raw = [_dot(q, k, NT) for k in ks]
                da = [_dot(do, v_ref[c, :].astype(BF16), NT) for c in cols]
                terms = [_sb_terms(x, ahead > j * tk - i * tq) for x, j in zip(raw, js)]
                sums_l = [_split_dot(l1m, u_le) for _, l1m, _ in terms]
                a, gg = [], []
                for (t, l1m, _), part, d in zip(terms, sums_l, da):
                    a.append(jnp.exp2((t + l1m) + (tot - (part + pre_l))))
                    gg.append(a[-1] * d)
                    pre_l = pre_l + jnp.sum(l1m, axis=1, keepdims=True)
                sums_g = [_split_dot(x, u_lt) for x in gg]
                dzs = []
                for (t, _, e), x, part in zip(terms, gg, sums_g):
                    big_g = part + pre_g
                    pre_g = pre_g + jnp.sum(x, axis=1, keepdims=True)
                    inv = 1.0 / (1.0 + e)
                    sig = jnp.where(t >= 0.0, inv, e * inv)
                    dzs.append(((x - sig * (x + big_g)) * scale).astype(BF16))
                for x, k in zip(dzs, ks):
                    dq = dq + _dot(x, k, NN)
                for x, y, c in zip(dzs, a, cols):
                    dk_s[c, :] += _dot(x, q, TN)
                    dv_s[c, :] += _dot(y.astype(BF16), do, TN)
                return dq, pre_l, pre_g

            zero = jnp.zeros((tq, 1), F32)
            walked = jnp.clip(trips_ref[0, i].astype(jnp.int32), 1, i + 1)
            dq, _, _ = lax.fori_loop(i + 1 - walked, i + 1, kblocks, (jnp.zeros((tq, HEAD_DIM), F32), zero, zero))
            dq_ref[rows, :] = dq.astype(BF16)
            return 0

        lax.fori_loop(0, nq, qblock, 0)
        dk_ref[...] = dk_s[...].astype(BF16)
        dv_ref[...] = dv_s[...].astype(BF16)
        _fused_end(fused, grid, f_refs)

    return pl.pallas_call(
        body, name="sb_bwd", grid=grid,
        in_specs=[_seg_spec(s, 0), _seg_spec(s, 1), _seg_spec(s, 2), _seg_spec(s, 3),
                  _head_spec(s), _head_spec(s), _seg_spec(s, 0),
                  pl.BlockSpec((1, HEAD_DIM), lambda i, h: (0, h)),
                  pl.BlockSpec((None, None, 1, nq), lambda i, h: (i, h, 0, 0), memory_space=pltpu.SMEM)] + f_in_specs,
        out_specs=[_seg4_spec(s, 0), pl.BlockSpec((None, 1, HEAD_DIM), lambda i, h: (i, 0, h))] + f_out_specs,
        out_shape=[jax.ShapeDtypeStruct((8, b, s, w), BF16), jax.ShapeDtypeStruct((b, 1, w), F32)] + f_out,
        scratch_shapes=[pltpu.VMEM((s, HEAD_DIM), BF16), pltpu.VMEM((s, HEAD_DIM), F32),
                        pltpu.VMEM((s, HEAD_DIM), F32)] + f_scratch,
        compiler_params=_cparams(("arbitrary", "arbitrary")),
    )(proj8, proj8, proj8, proj8, o_sb, tot_sb, dy2, g_sb, trips, *f_in)


DIL_BLOCK = 128
DIL_GROUP = 4


def _dil_chunks(s, r):
    length = s // r
    out = []
    for rho in range(r):
        for cc in range(length // DIL_BLOCK):
            if r == 1:
                nat = pl.ds(cc * DIL_BLOCK, DIL_BLOCK)
            else:
                nat = pl.ds(rho + r * DIL_BLOCK * cc, DIL_BLOCK, stride=r)
            off = rho * length + cc * DIL_BLOCK
            out.append((nat, pl.ds(off, DIL_BLOCK), pl.ds(off + DIL_BLOCK, DIL_BLOCK)))
    return out


def _dil_masks(slope, r):
    n = DIL_BLOCK
    ri = lax.broadcasted_iota(jnp.int32, (n, 2 * n), 0)
    ci = lax.broadcasted_iota(jnp.int32, (n, 2 * n), 1)
    steps = ri - ci + n
    inside = (steps >= 0) & (steps <= n)
    bias = slope * (steps.astype(F32) * r)
    return jnp.where(inside, -bias, NEG_BIG), jnp.where(inside & (ci >= n), -bias, NEG_BIG)


def _dil_scores(q, k_pc, masks, first):
    return _dot(q, k_pc, NT) * (1.0 / math.sqrt(HEAD_DIM)) + jnp.where(first, masks[1], masks[0])


def _dil_check(s):
    assert (s // DIL_BLOCK) % DIL_GROUP == 0, s
    for window, r in DIL_PAIRS:
        assert window // r == DIL_BLOCK and s % (r * DIL_BLOCK) == 0, (s, window, r)


def _dil_fwd(proj8, g_dil, slopes, y2):
    _, b, s, w = proj8.shape
    n_heads = w // HEAD_DIM
    _dil_check(s)
    n = DIL_BLOCK
    nt = s // n

    def body(q_ref, k_ref, v_ref, zg_ref, g_ref, sl_ref, y_in, o_ref, lse_ref, y_ref,
             qp, kp, vp, pnum, pm, pl_, acc_s, m_s, l_s):
        del y_in
        slope = sl_ref[...][:, :1]
        kp[pl.ds(0, n), :] = jnp.zeros((n, HEAD_DIM), BF16)
        vp[pl.ds(0, n), :] = jnp.zeros((n, HEAD_DIM), BF16)

        for (window, r) in DIL_PAIRS:
            nb = (s // r) // n
            masks = _dil_masks(slope, float(r))
            for nat, per, padded in _dil_chunks(s, r):
                qp[per, :] = q_ref[nat, :].astype(BF16)
                kp[padded, :] = k_ref[nat, :].astype(BF16)
                vp[padded, :] = v_ref[nat, :].astype(BF16)
            num_t, m_t, l_t = (acc_s, m_s, l_s) if r == 1 else (pnum, pm, pl_)

            def tiles(tt, _):
                ts = [tt * DIL_GROUP + i for i in range(DIL_GROUP)]
                rows = [pl.ds(pl.multiple_of(t * n, n), n) for t in ts]
                both = [pl.ds(pl.multiple_of(t * n, n), 2 * n) for t in ts]
                sc = [_dil_scores(qp[rw, :], kp[bt, :], masks, lax.rem(t, nb) == 0)
                      for t, rw, bt in zip(ts, rows, both)]
                m = [jnp.max(x, axis=1, keepdims=True) for x in sc]
                p = [jnp.exp(x - mx) for x, mx in zip(sc, m)]
                num = [_dot(x.astype(BF16), vp[bt, :], NN) for x, bt in zip(p, both)]
                for rw, x, mx, nm in zip(rows, p, m, num):
                    num_t[rw, :] = nm
                    m_t[rw, :] = jnp.broadcast_to(mx, (n, HEAD_DIM))
                    l_t[rw, :] = jnp.broadcast_to(jnp.sum(x, axis=1, keepdims=True), (n, HEAD_DIM))
                return 0

            lax.fori_loop(0, nt // DIL_GROUP, tiles, 0)
            if r != 1:
                for nat, per, _ in _dil_chunks(s, r):
                    m_old, m_new_p = m_s[nat, :], pm[per, :]
                    m_new = jnp.maximum(m_old, m_new_p)
                    a_old, a_p = jnp.exp(m_old - m_new), jnp.exp(m_new_p - m_new)
                    m_s[nat, :] = m_new
                    l_s[nat, :] = l_s[nat, :] * a_old + pl_[per, :] * a_p
                    acc_s[nat, :] = acc_s[nat, :] * a_old + pnum[per, :] * a_p

        g = g_ref[...]

        def finish(t, _):
            rows = pl.ds(pl.multiple_of(t * n, n), n)
            l = l_s[rows, :]
            o = acc_s[rows, :] / l
            o_ref[rows, :] = o
            lse_ref[rows, :] = m_s[rows, :] + jnp.log(l)
            y_ref[rows, :] = _head_out(o, zg_ref[rows, :], g).astype(BF16)
            return 0

        lax.fori_loop(0, nt, finish, 0)

    f32_s = pltpu.VMEM((s, HEAD_DIM), F32)
    bf_s = pltpu.VMEM((s, HEAD_DIM), BF16)
    bf_pad = pltpu.VMEM((s + n, HEAD_DIM), BF16)
    return pl.pallas_call(
        body, name="dil_fwd", grid=(b, n_heads),
        in_specs=[_seg_spec(s, 4), _seg_spec(s, 5), _seg_spec(s, 6), _seg_spec(s, 7),
                  pl.BlockSpec((1, HEAD_DIM), lambda i, h: (0, h)),
                  pl.BlockSpec((None, 1, HEAD_DIM), lambda i, h: (h, 0, 0)), ANY],
        out_specs=[_head_spec(s), _head_spec(s), _seg_spec(s, 1)],
        out_shape=[jax.ShapeDtypeStruct((b, s, w), F32), jax.ShapeDtypeStruct((b, s, w), F32),
                   jax.ShapeDtypeStruct((2, b, s, w), BF16)],
        scratch_shapes=[bf_s, bf_pad, bf_pad, f32_s, f32_s, f32_s, f32_s, f32_s, f32_s],
        input_output_aliases={6: 2},
        compiler_params=_cparams(("parallel", "parallel")),
    )(proj8, proj8, proj8, proj8, g_dil, slopes, y2)


def _dil_bwd(proj8, o_dl, lse_dl, dy2, g_dil, slopes, dproj8):
    _, b, s, w = proj8.shape
    n_heads = w // HEAD_DIM
    _dil_check(s)
    n = DIL_BLOCK
    nt = s // n
    scale = 1.0 / math.sqrt(HEAD_DIM)

    def body(q_ref, k_ref, v_ref, zg_ref, o_ref, lse_ref, dy_ref, g_ref, sl_ref, dp_in, dp_ref, dg_ref,
             do_n, dt_n, dq_n, dk_n, dv_n, qp, kp, vp, dop, dtp, lsep, pdq, pdk, pdv):
        del dp_in
        dq_ref, dk_ref, dv_ref, dzg_ref = (dp_ref.at[i] for i in range(4))
        slope = sl_ref[...][:, :1]
        g = g_ref[...]

        def prologue(t, dg):
            rows = pl.ds(pl.multiple_of(t * n, n), n)
            o = o_ref[rows, :]
            do, dzg, dgi = _head_out_bwd(o, zg_ref[rows, :], g, dy_ref[rows, :])
            dzg_ref[rows, :] = dzg.astype(BF16)
            do_n[rows, :] = do
            dt_n[rows, :] = jnp.broadcast_to(jnp.sum(do * o, axis=-1, keepdims=True), (n, HEAD_DIM))
            return dg + dgi

        dg_ref[...] = lax.fori_loop(0, nt, prologue, jnp.zeros((1, HEAD_DIM), F32))
        dq_n[...] = jnp.zeros_like(dq_n)
        dk_n[...] = jnp.zeros_like(dk_n)
        dv_n[...] = jnp.zeros_like(dv_n)
        kp[pl.ds(0, n), :] = jnp.zeros((n, HEAD_DIM), BF16)
        vp[pl.ds(0, n), :] = jnp.zeros((n, HEAD_DIM), BF16)

        for (window, r) in DIL_PAIRS:
            nb = (s // r) // n
            masks = _dil_masks(slope, float(r))
            for nat, per, padded in _dil_chunks(s, r):
                qp[per, :] = q_ref[nat, :].astype(BF16)
                kp[padded, :] = k_ref[nat, :].astype(BF16)
                vp[padded, :] = v_ref[nat, :].astype(BF16)
                dop[per, :] = do_n[nat, :].astype(BF16)
                dtp[per, :] = dt_n[nat, :]
                lsep[per, :] = lse_ref[nat, :]
            pdk[...] = jnp.zeros_like(pdk)
            pdv[...] = jnp.zeros_like(pdv)

            def tiles(tt, _):
                ts = [tt * DIL_GROUP + i for i in range(DIL_GROUP)]
                rows = [pl.ds(pl.multiple_of(t * n, n), n) for t in ts]
                both = [pl.ds(pl.multiple_of(t * n, n), 2 * n) for t in ts]
                q = [qp[rw, :] for rw in rows]
                do = [dop[rw, :] for rw in rows]
                sc = [_dil_scores(qq, kp[bt, :], masks, lax.rem(t, nb) == 0) for t, qq, bt in zip(ts, q, both)]
                dp = [_dot(dd, vp[bt, :], NT) for dd, bt in zip(do, both)]
                p = [jnp.exp(x - lsep[rw, :][:, :1]) for x, rw in zip(sc, rows)]
                ds = [((x * (y - dtp[rw, :][:, :1])) * scale).astype(BF16) for x, y, rw in zip(p, dp, rows)]
                dq = [_dot(x, kp[bt, :], NN) for x, bt in zip(ds, both)]
                dk = [_dot(x, qq, TN) for x, qq in zip(ds, q)]
                dv = [_dot(x.astype(BF16), dd, TN) for x, dd in zip(p, do)]
                for rw, bt, x, y, z in zip(rows, both, dq, dk, dv):
                    pdq[rw, :] = x
                    pdk[bt, :] += y
                    pdv[bt, :] += z
                return 0

            lax.fori_loop(0, nt // DIL_GROUP, tiles, 0)
            for nat, per, padded in _dil_chunks(s, r):
                dq_n[nat, :] += pdq[per, :]
                dk_n[nat, :] += pdk[padded, :]
                dv_n[nat, :] += pdv[padded, :]

        dq_ref[...] = dq_n[...].astype(BF16)
        dk_ref[...] = dk_n[...].astype(BF16)
        dv_ref[...] = dv_n[...].astype(BF16)

    f32_s = pltpu.VMEM((s, HEAD_DIM), F32)
    f32_pad = pltpu.VMEM((s + n, HEAD_DIM), F32)
    bf_s = pltpu.VMEM((s, HEAD_DIM), BF16)
    bf_pad = pltpu.VMEM((s + n, HEAD_DIM), BF16)
    return pl.pallas_call(
        body, name="dil_bwd", grid=(b, n_heads),
        in_specs=[_seg_spec(s, 4), _seg_spec(s, 5), _seg_spec(s, 6), _seg_spec(s, 7),
                  _head_spec(s), _head_spec(s), _seg_spec(s, 1),
                  pl.BlockSpec((1, HEAD_DIM), lambda i, h: (0, h)),
                  pl.BlockSpec((None, 1, HEAD_DIM), lambda i, h: (h, 0, 0)), ANY],
        out_specs=[_seg4_spec(s, 1), pl.BlockSpec((None, 1, HEAD_DIM), lambda i, h: (i, 0, h))],
        out_shape=[jax.ShapeDtypeStruct((8, b, s, w), BF16), jax.ShapeDtypeStruct((b, 1, w), F32)],
        scratch_shapes=[f32_s] * 5 + [bf_s, bf_pad, bf_pad, bf_s] + [f32_s, f32_s, f32_s, f32_pad, f32_pad],
        input_output_aliases={9: 0},
        compiler_params=_cparams(("parallel", "parallel")),
    )(proj8, proj8, proj8, proj8, o_dl, lse_dl, dy2, g_dil, slopes, dproj8)


def _small_update(gathered, n_b, params, m, v):
    n_dev, _, width = gathered.shape

    def body(g_ref, p_ref, m_ref, v_ref, grad_ref, d_ref, nm_ref, nv_ref, loss_ref):
        for row in range(2):
            acc = None
            for dev in range(n_dev):
                for i in range(n_b):
                    term = g_ref[dev, pl.ds(row * n_b + i, 1), :]
                    acc = term if acc is None else acc + term
            grad_ref[pl.ds(row, 1), :] = acc
        loss = g_ref[0, pl.ds(2 * n_b, 1), pl.ds(0, 128)]
        for dev in range(1, n_dev):
            loss = loss + g_ref[dev, pl.ds(2 * n_b, 1), pl.ds(0, 128)]
        loss_ref[...] = loss
        d, nm, nv = _adamw_math(p_ref[...], grad_ref[...], m_ref[...], v_ref[...])
        d_ref[...] = d
        nm_ref[...] = nm
        nv_ref[...] = nv

    sds = jax.ShapeDtypeStruct((2, width), F32)
    return pl.pallas_call(
        body, name="small_update",
        in_specs=[VMEM_SPEC] * 4, out_specs=[VMEM_SPEC] * 5,
        out_shape=[sds, sds, sds, sds, jax.ShapeDtypeStruct((1, 128), F32)],
        compiler_params=_cparams(),
    )(gathered, params, m, v)


def _wada_update(c_t, dmod, w, m, v):
    d, bt = c_t.shape
    _, n = dmod.shape
    tr, tc = _tile(d, 512), _tile(n, 1024)

    def body(c_ref, dm_ref, w_ref, m_ref, v_ref, g_ref, d_ref, nm_ref, nv_ref):
        cv = c_ref[...]
        cs = (cv * _sigmoid(cv)).astype(BF16)
        grad = _dot(cs, dm_ref[...].astype(BF16), NN)
        g_ref[...] = grad
        dl, nm, nv = _adamw_math(w_ref[...], grad, m_ref[...], v_ref[...])
        d_ref[...] = dl
        nm_ref[...] = nm
        nv_ref[...] = nv

    spec = pl.BlockSpec((tr, tc), lambda i, j: (i, j))
    sds = jax.ShapeDtypeStruct((d, n), F32)
    return pl.pallas_call(
        body, name="wada_update", grid=(d // tr, n // tc),
        in_specs=[pl.BlockSpec((tr, bt), lambda i, j: (i, 0)), pl.BlockSpec((bt, tc), lambda i, j: (0, j)),
                  spec, spec, spec],
        out_specs=[spec] * 4, out_shape=[sds] * 4,
        compiler_params=_cparams(("parallel", "parallel")),
    )(c_t, dmod, w, m, v)


def _reduce_begin(gs, tag):
    ra = _sibling_half_swap(gs, "swap_" + tag)
    pa, own = _pair_sum(gs, ra, "pair_sum_" + tag)
    return _ScatterChips(pa), own


def _reduce_finish(rb, own, w, m, v, tag):
    half = _chip_sum(rb, own, "chip_sum_" + tag)
    return _adamw(w, _sibling_join(half, "join_" + tag), m, v, "adamw_" + tag)


def kernel(x, c, w_ada, b_ada, g_norm, w_in, g_sb, g_dil, w_out, g_final, loss_target, m_w_ada, m_b_ada, m_g_norm, m_w_in, m_g_sb, m_g_dil, m_w_out, m_g_final, v_w_ada, v_b_ada, v_g_norm, v_w_in, v_g_sb, v_g_dil, v_w_out, v_g_final):
    nb, s, d = x.shape
    t = nb * s
    na = w_ada.shape[2]
    cs = w_in.shape[2]
    w = cs // 2
    n_heads = w // HEAD_DIM
    r_out = w_out.shape[1]
    assert 2 * nb + 1 <= 8 and 2 * d + 2 * w <= 3 * d and N_CHIPS * na == 3 * d and N_CHIPS * r_out == 2 * w
    xi, yi, ci = _place()
    chip = 2 * xi + yi
    dev = 2 * chip + ci

    c_all = _allgather8(jnp.pad(c, ((0, 8 - nb), (0, 0))), "gather_c")
    c16 = c_all.reshape(N_DEV, 8, d)[:, :nb].reshape(N_DEV * nb, d)
    b_ada_shard = lax.dynamic_slice(b_ada, (0, chip * na), (1, na))
    mod_part = _mod_fwd(c16, w_ada[0], b_ada_shard)
    mod_all = _allgather8(mod_part, "gather_mod")
    mod_full = mod_all.reshape(N_CHIPS, 2, N_DEV * nb, na)[:, 0].transpose(1, 0, 2).reshape(N_DEV * nb, 3 * d)
    mod = lax.dynamic_slice(mod_full, (dev * nb, 0), (nb, 3 * d))
    shift, scale, gate = (mod[:, i * d:(i + 1) * d].reshape(nb, 1, d) for i in range(3))

    h = _norm_mod_fwd(x, g_norm, scale, shift)
    h2 = h.reshape(t, d)
    ws_in = _allgather_chips(_cast_bf16_slab(w_in[0], "cast_w_in"), "gather_w_in")
    ws_out = _allgather_chips(_cast_bf16_slab(w_out[0], "cast_w_out"), "gather_w_out")
    w_out_full = ws_out.reshape(2 * w, d)

    proj8 = _proj_fwd(h2, ws_in).reshape(8, nb, s, w)
    slopes = jnp.exp2(-ALIBI_MAX_BIAS * jnp.arange(1, n_heads + 1, dtype=F32) / n_heads)
    slopes = jnp.broadcast_to(slopes[:, None, None], (n_heads, 1, HEAD_DIM))
    o_sb, tot_sb, y2, sb_trips = _sb_fwd(proj8, g_sb)
    o_dl, lse_dl, y2 = _dil_fwd(proj8, g_dil, slopes, y2)
    y2f = y2.reshape(2, t, w)
    out = _out_fwd(y2f, w_out_full)

    dx1, dout, dgate, dg_final, loss_part = _loss_head(
        x, out.reshape(nb, s, d), gate, g_final.reshape(1, d), loss_target)
    dout2 = dout.reshape(t, d)
    gs_out = _out_bwd_w(y2f, dout2).reshape(N_CHIPS, r_out, d)
    scatter_out, own_out = _reduce_begin(gs_out, "w_out")
    dy2 = _out_bwd_y(dout2, w_out_full).reshape(2, nb, s, w)
    dproj8, dg_sb, rb_out = _sb_bwd(proj8, o_sb, tot_sb, sb_trips, dy2, g_sb, fused=scatter_out)
    dproj8, dg_dl = _dil_bwd(proj8, o_dl, lse_dl, dy2, g_dil, slopes, dproj8)
    dproj8 = dproj8.reshape(8, t, w)
    gs_in = _proj_bwd_w(h2, dproj8)
    scatter_in, own_in = _reduce_begin(gs_in, "w_in")
    dh, rb_in = _proj_bwd_x(dproj8, ws_in, fused=scatter_in)
    grad_x, dshift, dscale, dg_norm = _norm_mod_bwd(x, dh.reshape(nb, s, d), dx1, g_norm, scale)

    width = 3 * d
    dmod = jnp.concatenate([dshift, dscale, dgate], axis=-1).reshape(nb, width)
    gains = jnp.concatenate([dg_sb.reshape(nb, w), dg_dl.reshape(nb, w)], axis=-1)
    gains = jnp.pad(gains, ((0, 0), (2 * d, width - 2 * d - 2 * w)))
    first = jnp.pad(jnp.concatenate([dg_norm, dg_final], axis=-1), ((0, nb - 1), (0, width - 2 * d)))
    loss_row = jnp.pad(loss_part, ((0, 0), (0, width - 128)))
    pack = jnp.concatenate([dmod, gains + first, loss_row, jnp.zeros((8 - 2 * nb - 1, width), F32)], axis=0)
    gathered = _allgather8(pack, "gather_small").reshape(N_DEV, 8, width)

    def stack(bias, gn, gf, gsb, gdl):
        row1 = jnp.concatenate([gn.reshape(1, d), gf.reshape(1, d), gsb.reshape(1, w), gdl.reshape(1, w)], axis=-1)
        return jnp.concatenate([bias.reshape(1, width), jnp.pad(row1, ((0, 0), (0, width - 2 * d - 2 * w)))], axis=0)

    small = _small_update(
        gathered, nb, stack(b_ada, g_norm, g_final, g_sb, g_dil),
        stack(m_b_ada, m_g_norm, m_g_final, m_g_sb, m_g_dil), stack(v_b_ada, v_g_norm, v_g_final, v_g_sb, v_g_dil))
    loss = small[4][0, 0]

    def unstack(a):
        return (a[0:1, :], a[1:2, 0:d], a[1, d:2 * d], a[1:2, 2 * d:2 * d + w], a[1:2, 2 * d + w:2 * d + 2 * w])

    (g_b, g_gn, g_gf, g_gsb, g_gdl), (d_b, d_gn, d_gf, d_gsb, d_gdl), (nm_b, nm_gn, nm_gf, nm_gsb, nm_gdl), \
        (nv_b, nv_gn, nv_gf, nv_gsb, nv_gdl) = (unstack(a) for a in small[:4])

    dmod_all = gathered[:, :nb].reshape(N_DEV * nb, width)
    dmod_cols = lax.dynamic_slice(dmod_all, (0, chip * na), (N_DEV * nb, na))
    g_wa, d_wa, nm_wa, nv_wa = _wada_update(c16.T, dmod_cols, w_ada[0], m_w_ada[0], v_w_ada[0])

    g_wi, d_wi, nm_wi, nv_wi = _reduce_finish(rb_in, own_in, w_in[0], m_w_in[0], v_w_in[0], "w_in")
    g_wo, d_wo, nm_wo, nv_wo = _reduce_finish(rb_out, own_out, w_out[0], m_w_out[0], v_w_out[0], "w_out")

    lead = lambda a: a[None]
    return (loss, grad_x,
            lead(g_wa), g_b, g_gn, lead(g_wi), g_gsb, g_gdl, lead(g_wo), g_gf,
            lead(d_wa), d_b, d_gn, lead(d_wi), d_gsb, d_gdl, lead(d_wo), d_gf,
            lead(nm_wa), nm_b, nm_gn, lead(nm_wi), nm_gsb, nm_gdl, lead(nm_wo), nm_gf,
            lead(nv_wa), nv_b, nv_gn, lead(nv_wi), nv_gsb, nv_gdl, lead(nv_wo), nv_gf)
```

```python
import functools
import math

import jax
import jax.numpy as jnp
from jax import lax
from jax.experimental import pallas as pl
from jax.experimental.pallas import tpu as pltpu

F32 = jnp.float32
BF16 = jnp.bfloat16
MESH = pl.DeviceIdType.MESH

HEAD_DIM = 128
EPS = 1e-6
DIL_PAIRS = ((128, 1), (512, 4), (2048, 16))
ALIBI_MAX_BIAS = 8.0
ADAM_LR = 0.001
ADAM_B1 = 0.9
ADAM_B2 = 0.999
ADAM_EPS = 1e-08
ADAM_WD = 0.01
ADAM_STEP = 10
N_CHIPS = 4
N_DEV = 8
VMEM_LIMIT_BYTES = 56 * 1024 * 1024
NEG_BIG = -1e30

NN = (((1,), (0,)), ((), ()))
NT = (((1,), (1,)), ((), ()))
TN = (((0,), (0,)), ((), ()))

ANY = pl.BlockSpec(memory_space=pl.ANY)
VMEM_SPEC = pl.BlockSpec(memory_space=pltpu.VMEM)


def _cparams(sem=None):
    return pltpu.CompilerParams(dimension_semantics=sem, vmem_limit_bytes=VMEM_LIMIT_BYTES)


def _tile(dim, pref):
    t = min(dim, pref)
    assert dim % t == 0, (dim, pref)
    return t


def _dot(a, b, dims):
    return lax.dot_general(a, b, dims, preferred_element_type=F32)


def _sigmoid(x):
    return 1.0 / (1.0 + jnp.exp(-x))


def _place():
    return lax.axis_index("x"), lax.axis_index("y"), lax.axis_index("c")


def _allgather8(x_shard, name):
    m_per, n = x_shard.shape

    def body(x_ref, out_ref, send_sems, recv_sems, local_sem):
        x, y, c = _place()
        me, sibling = (x, y, c), (x, y, 1 - c)
        chips = [(1 - x, y), (x, 1 - y), (1 - x, 1 - y)]

        def rows(px, py, pc):
            return out_ref.at[pl.ds((4 * px + 2 * py + pc) * m_per, m_per), :]

        def copy(k, block, to, src=None):
            return pltpu.make_async_remote_copy(
                src_ref=rows(*block) if src is None else src, dst_ref=rows(*block),
                send_sem=send_sems.at[k], recv_sem=recv_sems.at[k], device_id=to, device_id_type=MESH)

        mine = pltpu.make_async_copy(x_ref, rows(*me), local_sem)
        mine.start()
        first = [copy(0, me, sibling, src=x_ref)]
        first += [copy(1 + j, me, (*chip, c), src=x_ref) for j, chip in enumerate(chips)]
        for cp in first:
            cp.start()
        passed = [copy(4 + j, (*chip, c), sibling) for j, chip in enumerate(chips)]
        for j, chip in enumerate(chips):
            copy(1 + j, (*chip, c), me).wait_recv()
            passed[j].start()
        copy(0, sibling, me).wait_recv()
        for j, chip in enumerate(chips):
            copy(4 + j, (*chip, 1 - c), me).wait_recv()
        for cp in first + passed:
            cp.wait_send()
        mine.wait()

    return pl.pallas_call(
        body, name=name,
        out_shape=jax.ShapeDtypeStruct((N_DEV * m_per, n), x_shard.dtype),
        in_specs=[VMEM_SPEC], out_specs=VMEM_SPEC,
        scratch_shapes=[pltpu.SemaphoreType.DMA((7,)), pltpu.SemaphoreType.DMA((7,)), pltpu.SemaphoreType.DMA],
    )(x_shard)


class _GatherChips:
    def __init__(self, ws):
        self.inputs = [ws]
        self.out_shapes = [jax.ShapeDtypeStruct(ws.shape, ws.dtype)]
        self.aliases = {0: 0}
        self.scratch = [pltpu.SemaphoreType.DMA((12,)), pltpu.SemaphoreType.DMA((12,))]
        self.quarter = ws.shape[1] // 4

    def _copy(self, refs, k, chip, pc, part, to):
        _, out_ref, send_sems, recv_sems = refs
        rows = out_ref.at[2 * chip[0] + chip[1], pl.ds((2 * pc + part) * self.quarter, self.quarter), :]
        return pltpu.make_async_remote_copy(
            src_ref=rows, dst_ref=rows, send_sem=send_sems.at[k], recv_sem=recv_sems.at[k],
            device_id=to, device_id_type=MESH)

    def _sends(self, refs, phase):
        x, y, c = _place()
        sibling, x_nbr, y_nbr, diag = (x, y, 1 - c), (1 - x, y), (x, 1 - y), (1 - x, 1 - y)
        plan = {
            "start": [(0, (x, y), 0, (*x_nbr, c)), (1, (x, y), 1, (*y_nbr, c)),
                      (2, (x, y), 1, (*x_nbr, c)), (3, (x, y), 0, (*y_nbr, c))],
            "middle": [(4, x_nbr, 0, (*y_nbr, c)), (6, x_nbr, 0, sibling), (5, y_nbr, 1, (*x_nbr, c)),
                       (7, y_nbr, 1, sibling), (8, x_nbr, 1, sibling), (9, y_nbr, 0, sibling)],
            "wait": [(10, diag, 0, sibling), (11, diag, 1, sibling)],
        }[phase]
        return [self._copy(refs, k, chip, c, part, to) for k, chip, part, to in plan]

    def _landings(self, refs, phase):
        x, y, c = _place()
        me, x_nbr, y_nbr, diag = (x, y, c), (1 - x, y), (x, 1 - y), (1 - x, 1 - y)
        plan = {
            "middle": [(0, x_nbr, c, 0), (1, y_nbr, c, 1), (2, x_nbr, c, 1), (3, y_nbr, c, 0)],
            "wait": [(4, diag, c, 0), (5, diag, c, 1)],
            "sibling": [(6, x_nbr, 1 - c, 0), (7, y_nbr, 1 - c, 1), (8, x_nbr, 1 - c, 1), (9, y_nbr, 1 - c, 0),
                        (10, diag, 1 - c, 0), (11, diag, 1 - c, 1)],
        }[phase]
        return [self._copy(refs, k, chip, pc, part, me) for k, chip, pc, part in plan]

    def start(self, *refs):
        for cp in self._sends(refs, "start"):
            cp.start()

    def middle(self, *refs):
        landed = self._landings(refs, "middle")
        passed = self._sends(refs, "middle")
        landed[0].wait_recv()
        passed[0].start()
        passed[1].start()
        landed[1].wait_recv()
        passed[2].start()
        passed[3].start()
        landed[2].wait_recv()
        passed[4].start()
        landed[3].wait_recv()
        passed[5].start()

    def wait(self, *refs):
        landed = self._landings(refs, "wait")
        passed = self._sends(refs, "wait")
        for arrival, cp in zip(landed, passed):
            arrival.wait_recv()
            cp.start()
        for arrival in self._landings(refs, "sibling"):
            arrival.wait_recv()
        for phase in ("start", "middle", "wait"):
            for cp in self._sends(refs, phase):
                cp.wait_send()


def _allgather_chips(ws, name):
    gather = _GatherChips(ws)

    def body(*refs):
        gather.start(*refs)
        gather.middle(*refs)
        gather.wait(*refs)

    return pl.pallas_call(
        body, name=name, out_shape=gather.out_shapes[0],
        in_specs=[ANY], out_specs=ANY, input_output_aliases=gather.aliases, scratch_shapes=gather.scratch,
    )(ws)


def _sibling_half_swap(gs, name):
    n, r, cdim = gs.shape
    half = r // 2

    def body(g_ref, out_ref, send_sem, recv_sem):
        x, y, c = _place()
        cp = pltpu.make_async_remote_copy(
            src_ref=g_ref.at[:, pl.ds((1 - c) * half, half), :], dst_ref=out_ref,
            send_sem=send_sem, recv_sem=recv_sem, device_id=(x, y, 1 - c), device_id_type=MESH)
        cp.start()
        cp.wait()

    return pl.pallas_call(
        body, name=name,
        out_shape=jax.ShapeDtypeStruct((n, half, cdim), gs.dtype),
        in_specs=[ANY], out_specs=ANY,
        scratch_shapes=[pltpu.SemaphoreType.DMA, pltpu.SemaphoreType.DMA],
    )(gs)


class _ScatterChips:
    def __init__(self, pa):
        self.inputs = [pa]
        self.out_shapes = [jax.ShapeDtypeStruct(pa.shape, pa.dtype)]
        self.scratch = [pltpu.SemaphoreType.DMA((3,)), pltpu.SemaphoreType.DMA((3,)), pltpu.SemaphoreType.DMA]

    @staticmethod
    def _mine(p_ref, out_ref, send_sems, recv_sems, local_sem):
        x, y, _ = _place()
        return pltpu.make_async_copy(p_ref.at[2 * x + y], out_ref.at[2 * x + y], local_sem)

    @staticmethod
    def _remote(p_ref, out_ref, send_sems, recv_sems, local_sem, incoming):
        x, y, c = _place()
        me = 2 * x + y
        remote = []
        for j, (px, py) in enumerate([(1 - x, y), (x, 1 - y), (1 - x, 1 - y)]):
            remote.append(pltpu.make_async_remote_copy(
                src_ref=p_ref.at[me if incoming else 2 * px + py], dst_ref=out_ref.at[2 * px + py if incoming else me],
                send_sem=send_sems.at[j], recv_sem=recv_sems.at[j], device_id=(px, py, c), device_id_type=MESH))
        return remote

    def start(self, *refs):
        self._mine(*refs).start()
        for cp in self._remote(*refs, incoming=False):
            cp.start()

    def wait(self, *refs):
        for cp in self._remote(*refs, incoming=True):
            cp.wait_recv()
        for cp in self._remote(*refs, incoming=False):
            cp.wait_send()
        self._mine(*refs).wait()


def _fused_specs(fused):
    if fused is None:
        return [], [], [], [], []
    return (list(fused.inputs), [ANY] * len(fused.inputs), list(fused.out_shapes), [ANY] * len(fused.out_shapes),
            list(fused.scratch))


def _fused_aliases(fused, first_input, first_output):
    aliases = getattr(fused, "aliases", {}) if fused is not None else {}
    return {first_input + i: first_output + o for i, o in aliases.items()}


def _fused_begin(fused, grid, refs):
    if fused is not None:
        first = functools.reduce(lambda p, q: p & q, [pl.program_id(i) == 0 for i in range(len(grid))])
        pl.when(first)(lambda: fused.start(*refs))
        if hasattr(fused, "middle"):
            step = functools.reduce(lambda acc, ig: acc * ig[1] + pl.program_id(ig[0]), enumerate(grid), 0)
            pl.when(step == math.prod(grid) // 2)(lambda: fused.middle(*refs))


def _fused_end(fused, grid, refs):
    if fused is not None:
        last = functools.reduce(lambda p, q: p & q, [pl.program_id(i) == g - 1 for i, g in enumerate(grid)])
        pl.when(last)(lambda: fused.wait(*refs))


def _sibling_join(full, name):
    h2, cdim = full.shape
    h = h2 // 2

    def body(in_ref, out_ref, send_sem, recv_sem):
        del in_ref
        x, y, c = _place()
        mine = out_ref.at[pl.ds(c * h, h), :]
        cp = pltpu.make_async_remote_copy(
            src_ref=mine, dst_ref=mine, send_sem=send_sem, recv_sem=recv_sem,
            device_id=(x, y, 1 - c), device_id_type=MESH)
        cp.start()
        theirs = out_ref.at[pl.ds((1 - c) * h, h), :]
        pltpu.make_async_remote_copy(
            src_ref=theirs, dst_ref=theirs, send_sem=send_sem, recv_sem=recv_sem,
            device_id=(x, y, 1 - c), device_id_type=MESH).wait_recv()
        cp.wait_send()

    return pl.pallas_call(
        body, name=name,
        out_shape=jax.ShapeDtypeStruct(full.shape, full.dtype),
        in_specs=[ANY], out_specs=ANY, input_output_aliases={0: 0},
        scratch_shapes=[pltpu.SemaphoreType.DMA, pltpu.SemaphoreType.DMA],
    )(full)


def _cast_bf16_slab(w, name):
    r, cdim = w.shape
    tr, tc = _tile(r, 512), _tile(cdim, 2048)

    def body(pc_ref, w_ref, o_ref):
        o_ref[...] = w_ref[...].astype(BF16)

    return pl.pallas_call(
        body, name=name,
        grid_spec=pltpu.PrefetchScalarGridSpec(
            num_scalar_prefetch=1, grid=(r // tr, cdim // tc),
            in_specs=[pl.BlockSpec((tr, tc), lambda i, j, pc: (i, j))],
            out_specs=pl.BlockSpec((None, tr, tc), lambda i, j, pc: (pc[1], i, j))),
        out_shape=jax.ShapeDtypeStruct((N_CHIPS, r, cdim), BF16),
        compiler_params=_cparams(("parallel", "parallel")),
    )(_place_scalars(), w)


def _place_scalars():
    x, y, c = _place()
    return jnp.stack([c, 2 * x + y]).astype(jnp.int32)


def _pair_sum(gs, ra, name):
    n, r, cdim = gs.shape
    half = r // 2
    tr, tc = _tile(half, 512), _tile(cdim, 2048)
    nt = half // tr

    def body(pc_ref, g_ref, r_ref, o_ref, own_ref):
        val = g_ref[...].astype(F32) + r_ref[...].astype(F32)
        o_ref[...] = val.astype(BF16)

        @pl.when(pl.program_id(2) == pc_ref[1])
        def _():
            own_ref[...] = val

    return pl.pallas_call(
        body, name=name,
        grid_spec=pltpu.PrefetchScalarGridSpec(
            num_scalar_prefetch=1, grid=(nt, cdim // tc, n),
            in_specs=[pl.BlockSpec((None, tr, tc), lambda i, j, s, pc: (s, pc[0] * nt + i, j)),
                      pl.BlockSpec((None, tr, tc), lambda i, j, s, pc: (s, i, j))],
            out_specs=[pl.BlockSpec((None, tr, tc), lambda i, j, s, pc: (s, i, j)),
                       pl.BlockSpec((tr, tc), lambda i, j, s, pc: (i, j))]),
        out_shape=[jax.ShapeDtypeStruct((n, half, cdim), BF16), jax.ShapeDtypeStruct((half, cdim), F32)],
        compiler_params=_cparams(("parallel", "parallel", "arbitrary")),
    )(_place_scalars(), gs, ra)


def _chip_sum(rb, own, name):
    n, h, cdim = rb.shape
    tr, tc = _tile(h, 256), _tile(cdim, 2048)
    nt = h // tr

    def body(pc_ref, r_ref, own_ref, o_ref):
        chip = pc_ref[1]
        acc = None
        for p in range(n):
            term = jnp.where(chip == p, own_ref[...], r_ref[p].astype(F32))
            acc = term if acc is None else acc + term
        o_ref[...] = acc

    return pl.pallas_call(
        body, name=name,
        grid_spec=pltpu.PrefetchScalarGridSpec(
            num_scalar_prefetch=1, grid=(nt, cdim // tc),
            in_specs=[pl.BlockSpec((n, tr, tc), lambda i, j, pc: (0, i, j)),
                      pl.BlockSpec((tr, tc), lambda i, j, pc: (i, j))],
            out_specs=pl.BlockSpec((tr, tc), lambda i, j, pc: (pc[0] * nt + i, j))),
        out_shape=jax.ShapeDtypeStruct((2 * h, cdim), F32),
        compiler_params=_cparams(("parallel", "parallel")),
    )(_place_scalars(), rb, own)


def _adamw_math(w, g, m, v):
    m = ADAM_B1 * m + (1.0 - ADAM_B1) * g
    v = ADAM_B2 * v + (1.0 - ADAM_B2) * (g * g)
    m_hat = m / (1.0 - ADAM_B1 ** ADAM_STEP)
    v_hat = v / (1.0 - ADAM_B2 ** ADAM_STEP)
    delta = -ADAM_LR * (m_hat / (jnp.sqrt(v_hat) + ADAM_EPS) + ADAM_WD * w)
    return delta, m, v


def _adamw(w, g, m, v, name):
    r, cdim = w.shape
    tr, tc = _tile(r, 256), _tile(cdim, 2048)

    def body(w_ref, g_ref, m_ref, v_ref, go_ref, d_ref, nm_ref, nv_ref):
        gv = g_ref[...]
        d, nm, nv = _adamw_math(w_ref[...], gv, m_ref[...], v_ref[...])
        go_ref[...] = gv
        d_ref[...] = d
        nm_ref[...] = nm
        nv_ref[...] = nv

    spec = pl.BlockSpec((tr, tc), lambda i, j: (i, j))
    sds = jax.ShapeDtypeStruct((r, cdim), F32)
    return pl.pallas_call(
        body, name=name, grid=(r // tr, cdim // tc),
        in_specs=[spec] * 4, out_specs=[spec] * 4, out_shape=[sds] * 4,
        compiler_params=_cparams(("parallel", "parallel")),
    )(w, g, m, v)


def _matmul(a, b, *, grid, a_spec, b_spec, out_spec, out_shape, acc_shape, dims, name, bias=None, bias_spec=None,
            silu_a=False, fused=None):
    nk = grid[2]
    f_in, f_in_specs, f_out, f_out_specs, f_scratch = _fused_specs(fused)
    n_in = 2 + (bias is not None)

    acc_scratch = [pltpu.VMEM(acc_shape, F32)] if nk > 1 else []

    def body(*refs):
        a_ref, b_ref = refs[:2]
        bias_ref = refs[2] if bias is not None else None
        o_ref = refs[n_in + len(f_in)]
        n_fixed = n_in + len(f_in) + 1 + len(f_out)
        f_refs = (*refs[n_in:n_in + len(f_in)], *refs[n_in + len(f_in) + 1:n_fixed],
                  *refs[n_fixed + len(acc_scratch):])
        _fused_begin(fused, grid, f_refs)

        def product():
            if len(a_ref.shape) == 3:
                tks = a_ref.shape[2]
                parts = [_dot(a_ref[i], b_ref[:, i * tks:(i + 1) * tks], dims) for i in range(a_ref.shape[0])]
                return functools.reduce(lambda p, q: p + q, parts)
            av = a_ref[...]
            if silu_a:
                av = av * _sigmoid(av)
            return _dot(av.astype(BF16), b_ref[...].astype(BF16), dims)

        def finish(res):
            if bias is not None:
                res = res + bias_ref[...]
            o_ref[...] = res.astype(o_ref.dtype)

        if nk == 1:
            finish(product())
        else:
            acc_ref = refs[n_fixed]
            k = pl.program_id(2)

            @pl.when(k == 0)
            def _():
                acc_ref[...] = product()

            if nk > 2:
                @pl.when((k > 0) & (k < nk - 1))
                def _():
                    acc_ref[...] += product()

            @pl.when(k == nk - 1)
            def _():
                finish(acc_ref[...] + product())

        _fused_end(fused, grid, f_refs)

    in_specs = [a_spec, b_spec] + ([] if bias is None else [bias_spec]) + f_in_specs
    args = (a, b) + (() if bias is None else (bias,)) + tuple(f_in)
    sem = ("parallel", "parallel", "arbitrary") if fused is None else ("arbitrary",) * 3
    res = pl.pallas_call(
        body, name=name, grid=grid, in_specs=in_specs, out_specs=[out_spec] + f_out_specs,
        out_shape=[out_shape] + f_out,
        scratch_shapes=acc_scratch + f_scratch,
        input_output_aliases=_fused_aliases(fused, n_in, 1),
        compiler_params=_cparams(sem),
    )(*args)
    return res[0] if fused is None else tuple(res)


def _mm_tiles(m, n, k):
    return _tile(m, 1024), _tile(n, 1024), _tile(k, 4096)


def _proj_fwd(h2, ws_in, fused=None):
    t, d = h2.shape
    _, _, cs = ws_in.shape
    w = cs // 2
    tm, tn, tk = _mm_tiles(t, w, d)
    nps, npseg = cs // tn, w // tn
    return _matmul(
        h2, ws_in, grid=(t // tm, 8 * npseg, d // tk), dims=NN, name="proj_fwd", fused=fused,
        a_spec=pl.BlockSpec((tm, tk), lambda m, n, k: (m, k)),
        b_spec=pl.BlockSpec((None, tk, tn), lambda m, n, k: (n // nps, k, n % nps)),
        out_spec=pl.BlockSpec((None, tm, tn), lambda m, n, k: (n // npseg, m, n % npseg)),
        out_shape=jax.ShapeDtypeStruct((8, t, w), F32), acc_shape=(tm, tn))


def _proj_bwd_x(dproj8, ws_in, fused=None):
    _, t, w = dproj8.shape
    _, d, cs = ws_in.shape
    tm, tn, _ = _mm_tiles(t, d, w)
    return _matmul(
        dproj8, ws_in, grid=(t // tm, d // tn, N_CHIPS), dims=NT, name="proj_bwd_x", fused=fused,
        a_spec=pl.BlockSpec((2, tm, w), lambda m, n, k: (k, m, 0)),
        b_spec=pl.BlockSpec((None, tn, cs), lambda m, n, k: (k, n, 0)),
        out_spec=pl.BlockSpec((tm, tn), lambda m, n, k: (m, n)),
        out_shape=jax.ShapeDtypeStruct((t, d), F32), acc_shape=(tm, tn))


def _proj_bwd_w(h2, dproj8):
    t, d = h2.shape
    _, _, w = dproj8.shape
    cs = 2 * w
    tm, tn, tk = _mm_tiles(d, w, t)
    nps, npseg = cs // tn, w // tn
    return _matmul(
        h2, dproj8, grid=(d // tm, 8 * npseg, t // tk), dims=TN, name="proj_bwd_w",
        a_spec=pl.BlockSpec((tk, tm), lambda m, n, k: (k, m)),
        b_spec=pl.BlockSpec((None, tk, tn), lambda m, n, k: (n // npseg, k, n % npseg)),
        out_spec=pl.BlockSpec((None, tm, tn), lambda m, n, k: (n // nps, m, n % nps)),
        out_shape=jax.ShapeDtypeStruct((N_CHIPS, d, cs), BF16), acc_shape=(tm, tn))


def _out_fwd(y2, w_out):
    _, t, w = y2.shape
    _, d = w_out.shape
    tm, tn, tk = _mm_tiles(t, d, w)
    kpg = w // tk
    return _matmul(
        y2, w_out, grid=(t // tm, d // tn, 2 * kpg), dims=NN, name="out_fwd",
        a_spec=pl.BlockSpec((None, tm, tk), lambda m, n, k: (k // kpg, m, k % kpg)),
        b_spec=pl.BlockSpec((tk, tn), lambda m, n, k: (k, n)),
        out_spec=pl.BlockSpec((tm, tn), lambda m, n, k: (m, n)),
        out_shape=jax.ShapeDtypeStruct((t, d), F32), acc_shape=(tm, tn))


def _out_bwd_y(dout, w_out):
    t, d = dout.shape
    w = w_out.shape[0] // 2
    tm, tn, tk = _mm_tiles(t, w, d)
    npg = w // tn
    return _matmul(
        dout, w_out, grid=(t // tm, 2 * npg, d // tk), dims=NT, name="out_bwd_y",
        a_spec=pl.BlockSpec((tm, tk), lambda m, n, k: (m, k)),
        b_spec=pl.BlockSpec((tn, tk), lambda m, n, k: (n, k)),
        out_spec=pl.BlockSpec((None, tm, tn), lambda m, n, k: (n // npg, m, n % npg)),
        out_shape=jax.ShapeDtypeStruct((2, t, w), F32), acc_shape=(tm, tn))


def _out_bwd_w(y2, dout):
    _, t, w = y2.shape
    _, d = dout.shape
    tm, tn, tk = _mm_tiles(w, d, t)
    mpg = w // tm
    return _matmul(
        y2, dout, grid=(2 * mpg, d // tn, t // tk), dims=TN, name="out_bwd_w",
        a_spec=pl.BlockSpec((None, tk, tm), lambda m, n, k: (m // mpg, k, m % mpg)),
        b_spec=pl.BlockSpec((tk, tn), lambda m, n, k: (k, n)),
        out_spec=pl.BlockSpec((tm, tn), lambda m, n, k: (m, n)),
        out_shape=jax.ShapeDtypeStruct((2 * w, d), BF16), acc_shape=(tm, tn))


def _mod_fwd(c_all, w_ada, b_ada):
    bt, d = c_all.shape
    _, n = w_ada.shape
    tn, tk = _tile(n, 512), _tile(d, 1024)
    return _matmul(
        c_all, w_ada, grid=(1, n // tn, d // tk), dims=NN, name="mod_fwd", silu_a=True,
        a_spec=pl.BlockSpec((bt, tk), lambda i, j, l: (0, l)),
        b_spec=pl.BlockSpec((tk, tn), lambda i, j, l: (l, j)),
        bias=b_ada, bias_spec=pl.BlockSpec((1, tn), lambda i, j, l: (0, j)),
        out_spec=pl.BlockSpec((bt, tn), lambda i, j, l: (0, j)),
        out_shape=jax.ShapeDtypeStruct((bt, n), F32), acc_shape=(bt, tn))


def _norm_mod_fwd(x, g_norm, scale, shift):
    b, s, d = x.shape
    ts = _tile(s, 256)

    def body(x_ref, g_ref, sc_ref, sh_ref, h_ref):
        xv = x_ref[...]
        r = lax.rsqrt(jnp.mean(xv * xv, axis=-1, keepdims=True) + EPS)
        y = (xv * r) * g_ref[...]
        h_ref[...] = (y * (1.0 + sc_ref[...]) + sh_ref[...]).astype(BF16)

    row = pl.BlockSpec((None, ts, d), lambda i, j: (i, j, 0))
    per_b = pl.BlockSpec((None, 1, d), lambda i, j: (i, 0, 0))
    return pl.pallas_call(
        body, name="norm_mod_fwd", grid=(b, s // ts),
        in_specs=[row, pl.BlockSpec((1, d), lambda i, j: (0, 0)), per_b, per_b],
        out_specs=row, out_shape=jax.ShapeDtypeStruct((b, s, d), BF16),
        compiler_params=_cparams(("parallel", "parallel")),
    )(x, g_norm, scale, shift)


def _norm_mod_bwd(x, dh, dx1, g_norm, scale):
    b, s, d = x.shape
    ts = _tile(s, 256)

    def body(x_ref, dh_ref, dx1_ref, g_ref, sc_ref, gx_ref, dsh_ref, dsc_ref, dg_ref):
        i, j = pl.program_id(0), pl.program_id(1)

        @pl.when(j == 0)
        def _():
            dsh_ref[...] = jnp.zeros_like(dsh_ref)
            dsc_ref[...] = jnp.zeros_like(dsc_ref)

        @pl.when((i == 0) & (j == 0))
        def _():
            dg_ref[...] = jnp.zeros_like(dg_ref)

        xv, dhv, g = x_ref[...], dh_ref[...], g_ref[...]
        r = lax.rsqrt(jnp.mean(xv * xv, axis=-1, keepdims=True) + EPS)
        xh = xv * r
        dsh_ref[...] += jnp.sum(dhv, axis=0, keepdims=True)
        dsc_ref[...] += jnp.sum(dhv * (xh * g), axis=0, keepdims=True)
        dn = dhv * (1.0 + sc_ref[...])
        dg_ref[...] += jnp.sum(dn * xh, axis=0, keepdims=True)
        u = dn * g
        dx = r * u - xv * (r * r * r) * jnp.mean(u * xv, axis=-1, keepdims=True)
        gx_ref[...] = dx1_ref[...] + dx

    row = pl.BlockSpec((None, ts, d), lambda i, j: (i, j, 0))
    per_b = pl.BlockSpec((None, 1, d), lambda i, j: (i, 0, 0))
    vec = pl.BlockSpec((1, d), lambda i, j: (0, 0))
    return pl.pallas_call(
        body, name="norm_mod_bwd", grid=(b, s // ts),
        in_specs=[row, row, row, vec, per_b],
        out_specs=[row, per_b, per_b, vec],
        out_shape=[jax.ShapeDtypeStruct((b, s, d), F32), jax.ShapeDtypeStruct((b, 1, d), F32),
                   jax.ShapeDtypeStruct((b, 1, d), F32), jax.ShapeDtypeStruct((1, d), F32)],
        compiler_params=_cparams(("arbitrary", "arbitrary")),
    )(x, dh, dx1, g_norm, scale)


def _loss_head(x, out, gate, g_final, target):
    b, s, d = x.shape
    ts = _tile(s, 256)

    def body(x_ref, o_ref, gt_ref, g_ref, t_ref, dx1_ref, dout_ref, dgt_ref, dg_ref, loss_ref):
        i, j = pl.program_id(0), pl.program_id(1)

        @pl.when(j == 0)
        def _():
            dgt_ref[...] = jnp.zeros_like(dgt_ref)

        @pl.when((i == 0) & (j == 0))
        def _():
            dg_ref[...] = jnp.zeros_like(dg_ref)
            loss_ref[...] = jnp.zeros_like(loss_ref)

        ov, gt, g = o_ref[...], gt_ref[...], g_ref[...]
        x1 = x_ref[...] + gt * ov
        r = lax.rsqrt(jnp.mean(x1 * x1, axis=-1, keepdims=True) + EPS)
        xh = x1 * r
        err = xh * g - t_ref[...]
        loss_ref[...] += 0.5 * jnp.sum(jnp.mean(err * err, axis=-1, keepdims=True))
        dfin = err * (1.0 / d)
        dg_ref[...] += jnp.sum(dfin * xh, axis=0, keepdims=True)
        u = dfin * g
        dx1 = r * u - x1 * (r * r * r) * jnp.mean(u * x1, axis=-1, keepdims=True)
        dx1_ref[...] = dx1
        dgt_ref[...] += jnp.sum(dx1 * ov, axis=0, keepdims=True)
        dout_ref[...] = (gt * dx1).astype(BF16)

    row = pl.BlockSpec((None, ts, d), lambda i, j: (i, j, 0))
    per_b = pl.BlockSpec((None, 1, d), lambda i, j: (i, 0, 0))
    vec = pl.BlockSpec((1, d), lambda i, j: (0, 0))
    return pl.pallas_call(
        body, name="loss_head", grid=(b, s // ts),
        in_specs=[row, row, per_b, vec, row],
        out_specs=[row, row, per_b, vec, pl.BlockSpec((1, 128), lambda i, j: (0, 0))],
        out_shape=[jax.ShapeDtypeStruct((b, s, d), F32), jax.ShapeDtypeStruct((b, s, d), BF16),
                   jax.ShapeDtypeStruct((b, 1, d), F32), jax.ShapeDtypeStruct((1, d), F32),
                   jax.ShapeDtypeStruct((1, 128), F32)],
        compiler_params=_cparams(("arbitrary", "arbitrary")),
    )(x, out, gate, g_final, target)


def _head_out(o, zg, g):
    rinv = lax.rsqrt(jnp.mean(o * o, axis=-1, keepdims=True) + EPS)
    return ((o * rinv) * g) * (zg * _sigmoid(zg))


def _head_out_bwd(o, zg, g, dy):
    rinv = lax.rsqrt(jnp.mean(o * o, axis=-1, keepdims=True) + EPS)
    rn = o * rinv
    sg = _sigmoid(zg)
    sil = zg * sg
    dzg = dy * (rn * g) * (sg * (1.0 + zg * (1.0 - sg)))
    dg = jnp.sum(dy * rn * sil, axis=0, keepdims=True)
    drn = dy * g * sil
    do = rinv * drn - o * (rinv * rinv * rinv) * jnp.mean(drn * o, axis=-1, keepdims=True)
    return do, dzg, dg


def _head_spec(s):
    return pl.BlockSpec((None, s, HEAD_DIM), lambda b, h: (b, 0, h))


def _seg_spec(s, seg):
    return pl.BlockSpec((None, None, s, HEAD_DIM), lambda b, h: (seg, b, 0, h))


def _seg4_spec(s, group):
    return pl.BlockSpec((4, None, s, HEAD_DIM), lambda b, h: (group, b, 0, h))


SB_Q_BLOCK = 512
SB_K_BLOCK = 256


SB_DEAD_LOG2 = -160.0
LOG2_E = 1.4426950408889634
SB_LOGIT_SCALE = LOG2_E / math.sqrt(HEAD_DIM)


def _sb_terms(raw, valid):
    t = jnp.where(valid, raw * SB_LOGIT_SCALE, NEG_BIG)
    e = jnp.exp2(-jnp.abs(t))
    l1m = -(jnp.maximum(t, 0.0) + jnp.log2(1.0 + e))
    return t, l1m, e


def _split_dot(a, u):
    hi = a.astype(BF16)
    lo = (a - hi.astype(F32)).astype(BF16)
    return _dot(hi, u, NN) + _dot(lo, u, NN)


def _sb_fwd(proj8, g_sb):
    _, b, s, w = proj8.shape
    n_heads = w // HEAD_DIM
    tq, tk = _tile(s, SB_Q_BLOCK), _tile(s, SB_K_BLOCK)
    nq, kpq = s // tq, tq // tk
    scale = 1.0 / math.sqrt(HEAD_DIM)

    def body(q_ref, k_ref, v_ref, zg_ref, g_ref, o_ref, tot_ref, y_ref, trips_ref):
        u_excl = (lax.broadcasted_iota(jnp.int32, (tk, tk), 0)
                  > lax.broadcasted_iota(jnp.int32, (tk, tk), 1)).astype(BF16)
        ahead = lax.broadcasted_iota(jnp.int32, (tq, tk), 0) - lax.broadcasted_iota(jnp.int32, (tq, tk), 1)
        g = g_ref[...]

        def qblock(i, _):
            rows = pl.ds(pl.multiple_of(i * tq, tq), tq)
            q = q_ref[rows, :].astype(BF16)
            nk = (i + 1) * kpq

            def alive(state):
                jj, _, csum = state
                return (jj <= i) & ((jj == 0) | (jnp.max(csum) > SB_DEAD_LOG2))

            def kblocks(state):
                jj, acc, csum = state
                js = [nk - 1 - (jj * kpq + n) for n in range(kpq)]
                cols = [pl.ds(pl.multiple_of(j * tk, tk), tk) for j in js]
                raw = [_dot(q, k_ref[c, :].astype(BF16), NT) for c in cols]
                terms = [_sb_terms(x, ahead > j * tk - i * tq) for x, j in zip(raw, js)]
                sums = [_split_dot(l1m, u_excl) for _, l1m, _ in terms]
                for (t, l1m, _), part, c in zip(terms, sums, cols):
                    a = jnp.exp2((t + l1m) + (part + csum))
                    acc = acc + _dot(a.astype(BF16), v_ref[c, :].astype(BF16), NN)
                    csum = csum + jnp.sum(l1m, axis=1, keepdims=True)
                return jj + 1, acc, csum

            trips, acc, tot = lax.while_loop(
                alive, kblocks, (jnp.int32(0), jnp.zeros((tq, HEAD_DIM), F32), jnp.zeros((tq, 1), F32)))
            o_ref[rows, :] = acc
            tot_ref[rows, :] = jnp.broadcast_to(tot, (tq, HEAD_DIM))
            y_ref[rows, :] = _head_out(acc, zg_ref[rows, :], g).astype(BF16)
            trips_ref[0, i] = trips.astype(F32)
            return 0

        lax.fori_loop(0, nq, qblock, 0)

    return pl.pallas_call(
        body, name="sb_fwd", grid=(b, n_heads),
        in_specs=[_seg_spec(s, 0), _seg_spec(s, 1), _seg_spec(s, 2), _seg_spec(s, 3),
                  pl.BlockSpec((1, HEAD_DIM), lambda i, h: (0, h))],
        out_specs=[_head_spec(s), _head_spec(s), _seg_spec(s, 0),
                   pl.BlockSpec((None, None, 1, nq), lambda i, h: (i, h, 0, 0), memory_space=pltpu.SMEM)],
        out_shape=[jax.ShapeDtypeStruct((b, s, w), F32), jax.ShapeDtypeStruct((b, s, w), F32),
                   jax.ShapeDtypeStruct((2, b, s, w), BF16), jax.ShapeDtypeStruct((b, n_heads, 1, nq), F32)],
        compiler_params=_cparams(("parallel", "parallel")),
    )(proj8, proj8, proj8, proj8, g_sb)


def _sb_bwd(proj8, o_sb, tot_sb, trips, dy2, g_sb, fused=None):
    _, b, s, w = proj8.shape
    n_heads = w // HEAD_DIM
    tq, tk = _tile(s, SB_Q_BLOCK), _tile(s, SB_K_BLOCK)
    nq, kpq = s // tq, tq // tk
    scale = 1.0 / math.sqrt(HEAD_DIM)

    f_in, f_in_specs, f_out, f_out_specs, f_scratch = _fused_specs(fused)
    grid = (b, n_heads)

    def body(*refs):
        q_ref, k_ref, v_ref, zg_ref, o_ref, tot_ref, dy_ref, g_ref, trips_ref = refs[:9]
        dp_ref, dg_ref = refs[9 + len(f_in):11 + len(f_in)]
        do_s, dk_s, dv_s = refs[11 + len(f_in) + len(f_out):14 + len(f_in) + len(f_out)]
        f_refs = (*refs[9:9 + len(f_in)], *refs[11 + len(f_in):11 + len(f_in) + len(f_out)],
                  *refs[14 + len(f_in) + len(f_out):])
        _fused_begin(fused, grid, f_refs)
        dq_ref, dk_ref, dv_ref, dzg_ref = (dp_ref.at[n] for n in range(4))
        ri = lax.broadcasted_iota(jnp.int32, (tk, tk), 0)
        ci = lax.broadcasted_iota(jnp.int32, (tk, tk), 1)
        u_le = (ri <= ci).astype(BF16)
        u_lt = (ri < ci).astype(BF16)
        ahead = lax.broadcasted_iota(jnp.int32, (tq, tk), 0) - lax.broadcasted_iota(jnp.int32, (tq, tk), 1)
        g = g_ref[...]

        def prologue(i, dg):
            rows = pl.ds(pl.multiple_of(i * tq, tq), tq)
            do, dzg, dgi = _head_out_bwd(o_ref[rows, :], zg_ref[rows, :], g, dy_ref[rows, :])
            dzg_ref[rows, :] = dzg.astype(BF16)
            do_s[rows, :] = do.astype(BF16)
            return dg + dgi

        dg_ref[...] = lax.fori_loop(0, nq, prologue, jnp.zeros((1, HEAD_DIM), F32))
        dk_s[...] = jnp.zeros_like(dk_s)
        dv_s[...] = jnp.zeros_like(dv_s)

        def qblock(i, _):
            rows = pl.ds(pl.multiple_of(i * tq, tq), tq)
            q = q_ref[rows, :].astype(BF16)
            do = do_s[rows, :]
            tot = tot_ref[rows, :][:, :1]

            def kblocks(jj, carry):
                dq, pre_l, pre_g = carry
                js = [jj * kpq + n for n in range(kpq)]
                cols = [pl.ds(pl.multiple_of(j * tk, tk), tk) for j in js]
                ks = [k_ref[c, :].astype(BF16) for c in cols]
                raw = [_dot(q, k, NT) for k in ks]
                da = [_dot(do, v_ref[c, :].astype(BF16), NT) for c in cols]
                terms = [_sb_terms(x, ahead > j * tk - i * tq) for x, j in zip(raw, js)]
                sums_l = [_split_dot(l1m, u_le) for _, l1m, _ in terms]
                a, gg = [], []
                for (t, l1m, _), part, d in zip(terms, sums_l, da):
                    a.append(jnp.exp2((t + l1m) + (tot - (part + pre_l))))
                    gg.append(a[-1] * d)
                    pre_l = pre_l + jnp.sum(l1m, axis=1, keepdims=True)
                sums_g = [_split_dot(x, u_lt) for x in gg]
                dzs = []
                for (t, _, e), x, part in zip(terms, gg, sums_g):
                    big_g = part + pre_g
                    pre_g = pre_g + jnp.sum(x, axis=1, keepdims=True)
                    inv = 1.0 / (1.0 + e)
                    sig = jnp.where(t >= 0.0, inv, e * inv)
                    dzs.append(((x - sig * (x + big_g)) * scale).astype(BF16))
                for x, k in zip(dzs, ks):
                    dq = dq + _dot(x, k, NN)
                for x, y, c in zip(dzs, a, cols):
                    dk_s[c, :] += _dot(x, q, TN)
                    dv_s[c, :] += _dot(y.astype(BF16), do, TN)
                return dq, pre_l, pre_g

            zero = jnp.zeros((tq, 1), F32)
            walked = jnp.clip(trips_ref[0, i].astype(jnp.int32), 1, i + 1)
            dq, _, _ = lax.fori_loop(i + 1 - walked, i + 1, kblocks, (jnp.zeros((tq, HEAD_DIM), F32), zero, zero))
            dq_ref[rows, :] = dq.astype(BF16)
            return 0

        lax.fori_loop(0, nq, qblock, 0)
        dk_ref[...] = dk_s[...].astype(BF16)
        dv_ref[...] = dv_s[...].astype(BF16)
        _fused_end(fused, grid, f_refs)

    return pl.pallas_call(
        body, name="sb_bwd", grid=grid,
        in_specs=[_seg_spec(s, 0), _seg_spec(s, 1), _seg_spec(s, 2), _seg_spec(s, 3),
                  _head_spec(s), _head_spec(s), _seg_spec(s, 0),
                  pl.BlockSpec((1, HEAD_DIM), lambda i, h: (0, h)),
                  pl.BlockSpec((None, None, 1, nq), lambda i, h: (i, h, 0, 0), memory_space=pltpu.SMEM)] + f_in_specs,
        out_specs=[_seg4_spec(s, 0), pl.BlockSpec((None, 1, HEAD_DIM), lambda i, h: (i, 0, h))] + f_out_specs,
        out_shape=[jax.ShapeDtypeStruct((8, b, s, w), BF16), jax.ShapeDtypeStruct((b, 1, w), F32)] + f_out,
        scratch_shapes=[pltpu.VMEM((s, HEAD_DIM), BF16), pltpu.VMEM((s, HEAD_DIM), F32),
                        pltpu.VMEM((s, HEAD_DIM), F32)] + f_scratch,
        compiler_params=_cparams(("arbitrary", "arbitrary")),
    )(proj8, proj8, proj8, proj8, o_sb, tot_sb, dy2, g_sb, trips, *f_in)


DIL_BLOCK = 128
DIL_GROUP = 4


def _dil_chunks(s, r):
    length = s // r
    out = []
    for rho in range(r):
        for cc in range(length // DIL_BLOCK):
            if r == 1:
                nat = pl.ds(cc * DIL_BLOCK, DIL_BLOCK)
            else:
                nat = pl.ds(rho + r * DIL_BLOCK * cc, DIL_BLOCK, stride=r)
            off = rho * length + cc * DIL_BLOCK
            out.append((nat, pl.ds(off, DIL_BLOCK), pl.ds(off + DIL_BLOCK, DIL_BLOCK)))
    return out


def _dil_masks(slope, r):
    n = DIL_BLOCK
    ri = lax.broadcasted_iota(jnp.int32, (n, 2 * n), 0)
    ci = lax.broadcasted_iota(jnp.int32, (n, 2 * n), 1)
    steps = ri - ci + n
    inside = (steps >= 0) & (steps <= n)
    bias = slope * (steps.astype(F32) * r)
    return jnp.where(inside, -bias, NEG_BIG), jnp.where(inside & (ci >= n), -bias, NEG_BIG)


def _dil_scores(q, k_pc, masks, first):
    return _dot(q, k_pc, NT) * (1.0 / math.sqrt(HEAD_DIM)) + jnp.where(first, masks[1], masks[0])


def _dil_check(s):
    assert (s // DIL_BLOCK) % DIL_GROUP == 0, s
    for window, r in DIL_PAIRS:
        assert window // r == DIL_BLOCK and s % (r * DIL_BLOCK) == 0, (s, window, r)


def _dil_fwd(proj8, g_dil, slopes, y2):
    _, b, s, w = proj8.shape
    n_heads = w // HEAD_DIM
    _dil_check(s)
    n = DIL_BLOCK
    nt = s // n

    def body(q_ref, k_ref, v_ref, zg_ref, g_ref, sl_ref, y_in, o_ref, lse_ref, y_ref,
             qp, kp, vp, pnum, pm, pl_, acc_s, m_s, l_s):
        del y_in
        slope = sl_ref[...][:, :1]
        kp[pl.ds(0, n), :] = jnp.zeros((n, HEAD_DIM), BF16)
        vp[pl.ds(0, n), :] = jnp.zeros((n, HEAD_DIM), BF16)

        for (window, r) in DIL_PAIRS:
            nb = (s // r) // n
            masks = _dil_masks(slope, float(r))
            for nat, per, padded in _dil_chunks(s, r):
                qp[per, :] = q_ref[nat, :].astype(BF16)
                kp[padded, :] = k_ref[nat, :].astype(BF16)
                vp[padded, :] = v_ref[nat, :].astype(BF16)
            num_t, m_t, l_t = (acc_s, m_s, l_s) if r == 1 else (pnum, pm, pl_)

            def tiles(tt, _):
                ts = [tt * DIL_GROUP + i for i in range(DIL_GROUP)]
                rows = [pl.ds(pl.multiple_of(t * n, n), n) for t in ts]
                both = [pl.ds(pl.multiple_of(t * n, n), 2 * n) for t in ts]
                sc = [_dil_scores(qp[rw, :], kp[bt, :], masks, lax.rem(t, nb) == 0)
                      for t, rw, bt in zip(ts, rows, both)]
                m = [jnp.max(x, axis=1, keepdims=True) for x in sc]
                p = [jnp.exp(x - mx) for x, mx in zip(sc, m)]
                num = [_dot(x.astype(BF16), vp[bt, :], NN) for x, bt in zip(p, both)]
                for rw, x, mx, nm in zip(rows, p, m, num):
                    num_t[rw, :] = nm
                    m_t[rw, :] = jnp.broadcast_to(mx, (n, HEAD_DIM))
                    l_t[rw, :] = jnp.broadcast_to(jnp.sum(x, axis=1, keepdims=True), (n, HEAD_DIM))
                return 0

            lax.fori_loop(0, nt // DIL_GROUP, tiles, 0)
            if r != 1:
                for nat, per, _ in _dil_chunks(s, r):
                    m_old, m_new_p = m_s[nat, :], pm[per, :]
                    m_new = jnp.maximum(m_old, m_new_p)
                    a_old, a_p = jnp.exp(m_old - m_new), jnp.exp(m_new_p - m_new)
                    m_s[nat, :] = m_new
                    l_s[nat, :] = l_s[nat, :] * a_old + pl_[per, :] * a_p
                    acc_s[nat, :] = acc_s[nat, :] * a_old + pnum[per, :] * a_p

        g = g_ref[...]

        def finish(t, _):
            rows = pl.ds(pl.multiple_of(t * n, n), n)
            l = l_s[rows, :]
            o = acc_s[rows, :] / l
            o_ref[rows, :] = o
            lse_ref[rows, :] = m_s[rows, :] + jnp.log(l)
            y_ref[rows, :] = _head_out(o, zg_ref[rows, :], g).astype(BF16)
            return 0

        lax.fori_loop(0, nt, finish, 0)

    f32_s = pltpu.VMEM((s, HEAD_DIM), F32)
    bf_s = pltpu.VMEM((s, HEAD_DIM), BF16)
    bf_pad = pltpu.VMEM((s + n, HEAD_DIM), BF16)
    return pl.pallas_call(
        body, name="dil_fwd", grid=(b, n_heads),
        in_specs=[_seg_spec(s, 4), _seg_spec(s, 5), _seg_spec(s, 6), _seg_spec(s, 7),
                  pl.BlockSpec((1, HEAD_DIM), lambda i, h: (0, h)),
                  pl.BlockSpec((None, 1, HEAD_DIM), lambda i, h: (h, 0, 0)), ANY],
        out_specs=[_head_spec(s), _head_spec(s), _seg_spec(s, 1)],
        out_shape=[jax.ShapeDtypeStruct((b, s, w), F32), jax.ShapeDtypeStruct((b, s, w), F32),
                   jax.ShapeDtypeStruct((2, b, s, w), BF16)],
        scratch_shapes=[bf_s, bf_pad, bf_pad, f32_s, f32_s, f32_s, f32_s, f32_s, f32_s],
        input_output_aliases={6: 2},
        compiler_params=_cparams(("parallel", "parallel")),
    )(proj8, proj8, proj8, proj8, g_dil, slopes, y2)


def _dil_bwd(proj8, o_dl, lse_dl, dy2, g_dil, slopes, dproj8):
    _, b, s, w = proj8.shape
    n_heads = w // HEAD_DIM
    _dil_check(s)
    n = DIL_BLOCK
    nt = s // n
    scale = 1.0 / math.sqrt(HEAD_DIM)

    def body(q_ref, k_ref, v_ref, zg_ref, o_ref, lse_ref, dy_ref, g_ref, sl_ref, dp_in, dp_ref, dg_ref,
             do_n, dt_n, dq_n, dk_n, dv_n, qp, kp, vp, dop, dtp, lsep, pdq, pdk, pdv):
        del dp_in
        dq_ref, dk_ref, dv_ref, dzg_ref = (dp_ref.at[i] for i in range(4))
        slope = sl_ref[...][:, :1]
        g = g_ref[...]

        def prologue(t, dg):
            rows = pl.ds(pl.multiple_of(t * n, n), n)
            o = o_ref[rows, :]
            do, dzg, dgi = _head_out_bwd(o, zg_ref[rows, :], g, dy_ref[rows, :])
            dzg_ref[rows, :] = dzg.astype(BF16)
            do_n[rows, :] = do
            dt_n[rows, :] = jnp.broadcast_to(jnp.sum(do * o, axis=-1, keepdims=True), (n, HEAD_DIM))
            return dg + dgi

        dg_ref[...] = lax.fori_loop(0, nt, prologue, jnp.zeros((1, HEAD_DIM), F32))
        dq_n[...] = jnp.zeros_like(dq_n)
        dk_n[...] = jnp.zeros_like(dk_n)
        dv_n[...] = jnp.zeros_like(dv_n)
        kp[pl.ds(0, n), :] = jnp.zeros((n, HEAD_DIM), BF16)
        vp[pl.ds(0, n), :] = jnp.zeros((n, HEAD_DIM), BF16)

        for (window, r) in DIL_PAIRS:
            nb = (s // r) // n
            masks = _dil_masks(slope, float(r))
            for nat, per, padded in _dil_chunks(s, r):
                qp[per, :] = q_ref[nat, :].astype(BF16)
                kp[padded, :] = k_ref[nat, :].astype(BF16)
                vp[padded, :] = v_ref[nat, :].astype(BF16)
                dop[per, :] = do_n[nat, :].astype(BF16)
                dtp[per, :] = dt_n[nat, :]
                lsep[per, :] = lse_ref[nat, :]
            pdk[...] = jnp.zeros_like(pdk)
            pdv[...] = jnp.zeros_like(pdv)

            def tiles(tt, _):
                ts = [tt * DIL_GROUP + i for i in range(DIL_GROUP)]
                rows = [pl.ds(pl.multiple_of(t * n, n), n) for t in ts]
                both = [pl.ds(pl.multiple_of(t * n, n), 2 * n) for t in ts]
                q = [qp[rw, :] for rw in rows]
                do = [dop[rw, :] for rw in rows]
                sc = [_dil_scores(qq, kp[bt, :], masks, lax.rem(t, nb) == 0) for t, qq, bt in zip(ts, q, both)]
                dp = [_dot(dd, vp[bt, :], NT) for dd, bt in zip(do, both)]
                p = [jnp.exp(x - lsep[rw, :][:, :1]) for x, rw in zip(sc, rows)]
                ds = [((x * (y - dtp[rw, :][:, :1])) * scale).astype(BF16) for x, y, rw in zip(p, dp, rows)]
                dq = [_dot(x, kp[bt, :], NN) for x, bt in zip(ds, both)]
                dk = [_dot(x, qq, TN) for x, qq in zip(ds, q)]
                dv = [_dot(x.astype(BF16), dd, TN) for x, dd in zip(p, do)]
                for rw, bt, x, y, z in zip(rows, both, dq, dk, dv):
                    pdq[rw, :] = x
                    pdk[bt, :] += y
                    pdv[bt, :] += z
                return 0

            lax.fori_loop(0, nt // DIL_GROUP, tiles, 0)
            for nat, per, padded in _dil_chunks(s, r):
                dq_n[nat, :] += pdq[per, :]
                dk_n[nat, :] += pdk[padded, :]
                dv_n[nat, :] += pdv[padded, :]

        dq_ref[...] = dq_n[...].astype(BF16)
        dk_ref[...] = dk_n[...].astype(BF16)
        dv_ref[...] = dv_n[...].astype(BF16)

    f32_s = pltpu.VMEM((s, HEAD_DIM), F32)
    f32_pad = pltpu.VMEM((s + n, HEAD_DIM), F32)
    bf_s = pltpu.VMEM((s, HEAD_DIM), BF16)
    bf_pad = pltpu.VMEM((s + n, HEAD_DIM), BF16)
    return pl.pallas_call(
        body, name="dil_bwd", grid=(b, n_heads),
        in_specs=[_seg_spec(s, 4), _seg_spec(s, 5), _seg_spec(s, 6), _seg_spec(s, 7),
                  _head_spec(s), _head_spec(s), _seg_spec(s, 1),
                  pl.BlockSpec((1, HEAD_DIM), lambda i, h: (0, h)),
                  pl.BlockSpec((None, 1, HEAD_DIM), lambda i, h: (h, 0, 0)), ANY],
        out_specs=[_seg4_spec(s, 1), pl.BlockSpec((None, 1, HEAD_DIM), lambda i, h: (i, 0, h))],
        out_shape=[jax.ShapeDtypeStruct((8, b, s, w), BF16), jax.ShapeDtypeStruct((b, 1, w), F32)],
        scratch_shapes=[f32_s] * 5 + [bf_s, bf_pad, bf_pad, bf_s] + [f32_s, f32_s, f32_s, f32_pad, f32_pad],
        input_output_aliases={9: 0},
        compiler_params=_cparams(("parallel", "parallel")),
    )(proj8, proj8, proj8, proj8, o_dl, lse_dl, dy2, g_dil, slopes, dproj8)


def _small_update(gathered, n_b, params, m, v):
    n_dev, _, width = gathered.shape

    def body(g_ref, p_ref, m_ref, v_ref, grad_ref, d_ref, nm_ref, nv_ref, loss_ref):
        for row in range(2):
            acc = None
            for dev in range(n_dev):
                for i in range(n_b):
                    term = g_ref[dev, pl.ds(row * n_b + i, 1), :]
                    acc = term if acc is None else acc + term
            grad_ref[pl.ds(row, 1), :] = acc
        loss = g_ref[0, pl.ds(2 * n_b, 1), pl.ds(0, 128)]
        for dev in range(1, n_dev):
            loss = loss + g_ref[dev, pl.ds(2 * n_b, 1), pl.ds(0, 128)]
        loss_ref[...] = loss
        d, nm, nv = _adamw_math(p_ref[...], grad_ref[...], m_ref[...], v_ref[...])
        d_ref[...] = d
        nm_ref[...] = nm
        nv_ref[...] = nv

    sds = jax.ShapeDtypeStruct((2, width), F32)
    return pl.pallas_call(
        body, name="small_update",
        in_specs=[VMEM_SPEC] * 4, out_specs=[VMEM_SPEC] * 5,
        out_shape=[sds, sds, sds, sds, jax.ShapeDtypeStruct((1, 128), F32)],
        compiler_params=_cparams(),
    )(gathered, params, m, v)


def _wada_update(c_t, dmod, w, m, v):
    d, bt = c_t.shape
    _, n = dmod.shape
    tr, tc = _tile(d, 512), _tile(n, 1024)

    def body(c_ref, dm_ref, w_ref, m_ref, v_ref, g_ref, d_ref, nm_ref, nv_ref):
        cv = c_ref[...]
        cs = (cv * _sigmoid(cv)).astype(BF16)
        grad = _dot(cs, dm_ref[...].astype(BF16), NN)
        g_ref[...] = grad
        dl, nm, nv = _adamw_math(w_ref[...], grad, m_ref[...], v_ref[...])
        d_ref[...] = dl
        nm_ref[...] = nm
        nv_ref[...] = nv

    spec = pl.BlockSpec((tr, tc), lambda i, j: (i, j))
    sds = jax.ShapeDtypeStruct((d, n), F32)
    return pl.pallas_call(
        body, name="wada_update", grid=(d // tr, n // tc),
        in_specs=[pl.BlockSpec((tr, bt), lambda i, j: (i, 0)), pl.BlockSpec((bt, tc), lambda i, j: (0, j)),
                  spec, spec, spec],
        out_specs=[spec] * 4, out_shape=[sds] * 4,
        compiler_params=_cparams(("parallel", "parallel")),
    )(c_t, dmod, w, m, v)


def _reduce_begin(gs, tag):
    ra = _sibling_half_swap(gs, "swap_" + tag)
    pa, own = _pair_sum(gs, ra, "pair_sum_" + tag)
    return _ScatterChips(pa), own


def _reduce_finish(rb, own, w, m, v, tag):
    half = _chip_sum(rb, own, "chip_sum_" + tag)
    return _adamw(w, _sibling_join(half, "join_" + tag), m, v, "adamw_" + tag)


def kernel(x, c, w_ada, b_ada, g_norm, w_in, g_sb, g_dil, w_out, g_final, loss_target, m_w_ada, m_b_ada, m_g_norm, m_w_in, m_g_sb, m_g_dil, m_w_out, m_g_final, v_w_ada, v_b_ada, v_g_norm, v_w_in, v_g_sb, v_g_dil, v_w_out, v_g_final):
    nb, s, d = x.shape
    t = nb * s
    na = w_ada.shape[2]
    cs = w_in.shape[2]
    w = cs // 2
    n_heads = w // HEAD_DIM
    r_out = w_out.shape[1]
    assert 2 * nb + 1 <= 8 and 2 * d + 2 * w <= 3 * d and N_CHIPS * na == 3 * d and N_CHIPS * r_out == 2 * w
    xi, yi, ci = _place()
    chip = 2 * xi + yi
    dev = 2 * chip + ci

    c_all = _allgather8(jnp.pad(c, ((0, 8 - nb), (0, 0))), "gather_c")
    c16 = c_all.reshape(N_DEV, 8, d)[:, :nb].reshape(N_DEV * nb, d)
    b_ada_shard = lax.dynamic_slice(b_ada, (0, chip * na), (1, na))
    mod_part = _mod_fwd(c16, w_ada[0], b_ada_shard)
    mod_all = _allgather8(mod_part, "gather_mod")
    mod_full = mod_all.reshape(N_CHIPS, 2, N_DEV * nb, na)[:, 0].transpose(1, 0, 2).reshape(N_DEV * nb, 3 * d)
    mod = lax.dynamic_slice(mod_full, (dev * nb, 0), (nb, 3 * d))
    shift, scale, gate = (mod[:, i * d:(i + 1) * d].reshape(nb, 1, d) for i in range(3))

    h = _norm_mod_fwd(x, g_norm, scale, shift)
    h2 = h.reshape(t, d)
    ws_in = _allgather_chips(_cast_bf16_slab(w_in[0], "cast_w_in"), "gather_w_in")
    proj8, ws_out = _proj_fwd(h2, ws_in, fused=_GatherChips(_cast_bf16_slab(w_out[0], "cast_w_out")))
    proj8 = proj8.reshape(8, nb, s, w)
    w_out_full = ws_out.reshape(2 * w, d)

    slopes = jnp.exp2(-ALIBI_MAX_BIAS * jnp.arange(1, n_heads + 1, dtype=F32) / n_heads)
    slopes = jnp.broadcast_to(slopes[:, None, None], (n_heads, 1, HEAD_DIM))
    o_sb, tot_sb, y2, sb_trips = _sb_fwd(proj8, g_sb)
    o_dl, lse_dl, y2 = _dil_fwd(proj8, g_dil, slopes, y2)
    y2f = y2.reshape(2, t, w)
    out = _out_fwd(y2f, w_out_full)

    dx1, dout, dgate, dg_final, loss_part = _loss_head(
        x, out.reshape(nb, s, d), gate, g_final.reshape(1, d), loss_target)
    dout2 = dout.reshape(t, d)
    gs_out = _out_bwd_w(y2f, dout2).reshape(N_CHIPS, r_out, d)
    scatter_out, own_out = _reduce_begin(gs_out, "w_out")
    dy2 = _out_bwd_y(dout2, w_out_full).reshape(2, nb, s, w)
    dproj8, dg_sb, rb_out = _sb_bwd(proj8, o_sb, tot_sb, sb_trips, dy2, g_sb, fused=scatter_out)
    dproj8, dg_dl = _dil_bwd(proj8, o_dl, lse_dl, dy2, g_dil, slopes, dproj8)
    dproj8 = dproj8.reshape(8, t, w)
    gs_in = _proj_bwd_w(h2, dproj8)
    scatter_in, own_in = _reduce_begin(gs_in, "w_in")
    dh, rb_in = _proj_bwd_x(dproj8, ws_in, fused=scatter_in)
    grad_x, dshift, dscale, dg_norm = _norm_mod_bwd(x, dh.reshape(nb, s, d), dx1, g_norm, scale)

    width = 3 * d
    dmod = jnp.concatenate([dshift, dscale, dgate], axis=-1).reshape(nb, width)
    gains = jnp.concatenate([dg_sb.reshape(nb, w), dg_dl.reshape(nb, w)], axis=-1)
    gains = jnp.pad(gains, ((0, 0), (2 * d, width - 2 * d - 2 * w)))
    first = jnp.pad(jnp.concatenate([dg_norm, dg_final], axis=-1), ((0, nb - 1), (0, width - 2 * d)))
    loss_row = jnp.pad(loss_part, ((0, 0), (0, width - 128)))
    pack = jnp.concatenate([dmod, gains + first, loss_row, jnp.zeros((8 - 2 * nb - 1, width), F32)], axis=0)
    gathered = _allgather8(pack, "gather_small").reshape(N_DEV, 8, width)

    def stack(bias, gn, gf, gsb, gdl):
        row1 = jnp.concatenate([gn.reshape(1, d), gf.reshape(1, d), gsb.reshape(1, w), gdl.reshape(1, w)], axis=-1)
        return jnp.concatenate([bias.reshape(1, width), jnp.pad(row1, ((0, 0), (0, width - 2 * d - 2 * w)))], axis=0)

    small = _small_update(
        gathered, nb, stack(b_ada, g_norm, g_final, g_sb, g_dil),
        stack(m_b_ada, m_g_norm, m_g_final, m_g_sb, m_g_dil), stack(v_b_ada, v_g_norm, v_g_final, v_g_sb, v_g_dil))
    loss = small[4][0, 0]

    def unstack(a):
        return (a[0:1, :], a[1:2, 0:d], a[1, d:2 * d], a[1:2, 2 * d:2 * d + w], a[1:2, 2 * d + w:2 * d + 2 * w])

    (g_b, g_gn, g_gf, g_gsb, g_gdl), (d_b, d_gn, d_gf, d_gsb, d_gdl), (nm_b, nm_gn, nm_gf, nm_gsb, nm_gdl), \
        (nv_b, nv_gn, nv_gf, nv_gsb, nv_gdl) = (unstack(a) for a in small[:4])

    dmod_all = gathered[:, :nb].reshape(N_DEV * nb, width)
    dmod_cols = lax.dynamic_slice(dmod_all, (0, chip * na), (N_DEV * nb, na))
    g_wa, d_wa, nm_wa, nv_wa = _wada_update(c16.T, dmod_cols, w_ada[0], m_w_ada[0], v_w_ada[0])

    g_wi, d_wi, nm_wi, nv_wi = _reduce_finish(rb_in, own_in, w_in[0], m_w_in[0], v_w_in[0], "w_in")
    g_wo, d_wo, nm_wo, nv_wo = _reduce_finish(rb_out, own_out, w_out[0], m_w_out[0], v_w_out[0], "w_out")

    lead = lambda a: a[None]
    return (loss, grad_x,
            lead(g_wa), g_b, g_gn, lead(g_wi), g_gsb, g_gdl, lead(g_wo), g_gf,
            lead(d_wa), d_b, d_gn, lead(d_wi), d_gsb, d_gdl, lead(d_wo), d_gf,
            lead(nm_wa), nm_b, nm_gn, lead(nm_wi), nm_gsb, nm_gdl, lead(nm_wo), nm_gf,
            lead(nv_wa), nv_b, nv_gn, lead(nv_wi), nv_gsb, nv_gdl, lead(nv_wo), nv_gf)
```

```python
import functools
import math

import jax
import jax.numpy as jnp
from jax import lax
from jax.experimental import pallas as pl
from jax.experimental.pallas import tpu as pltpu

F32 = jnp.float32
BF16 = jnp.bfloat16
MESH = pl.DeviceIdType.MESH

HEAD_DIM = 128
EPS = 1e-6
DIL_PAIRS = ((128, 1), (512, 4), (2048, 16))
ALIBI_MAX_BIAS = 8.0
ADAM_LR = 0.001
ADAM_B1 = 0.9
ADAM_B2 = 0.999
ADAM_EPS = 1e-08
ADAM_WD = 0.01
ADAM_STEP = 10
N_CHIPS = 4
N_DEV = 8
VMEM_LIMIT_BYTES = 56 * 1024 * 1024
NEG_BIG = -1e30

NN = (((1,), (0,)), ((), ()))
NT = (((1,), (1,)), ((), ()))
TN = (((0,), (0,)), ((), ()))

ANY = pl.BlockSpec(memory_space=pl.ANY)
VMEM_SPEC = pl.BlockSpec(memory_space=pltpu.VMEM)


def _cparams(sem=None):
    return pltpu.CompilerParams(dimension_semantics=sem, vmem_limit_bytes=VMEM_LIMIT_BYTES)


def _tile(dim, pref):
    t = min(dim, pref)
    assert dim % t == 0, (dim, pref)
    return t


def _dot(a, b, dims):
    return lax.dot_general(a, b, dims, preferred_element_type=F32)


def _sigmoid(x):
    return 1.0 / (1.0 + jnp.exp(-x))


def _place():
    return lax.axis_index("x"), lax.axis_index("y"), lax.axis_index("c")


def _allgather8(x_shard, name):
    m_per, n = x_shard.shape

    def body(x_ref, out_ref, send_sems, recv_sems, local_sem):
        x, y, c = _place()
        me, sibling = (x, y, c), (x, y, 1 - c)
        chips = [(1 - x, y), (x, 1 - y), (1 - x, 1 - y)]

        def rows(px, py, pc):
            return out_ref.at[pl.ds((4 * px + 2 * py + pc) * m_per, m_per), :]

        def copy(k, block, to, src=None):
            return pltpu.make_async_remote_copy(
                src_ref=rows(*block) if src is None else src, dst_ref=rows(*block),
                send_sem=send_sems.at[k], recv_sem=recv_sems.at[k], device_id=to, device_id_type=MESH)

        mine = pltpu.make_async_copy(x_ref, rows(*me), local_sem)
        mine.start()
        first = [copy(0, me, sibling, src=x_ref)]
        first += [copy(1 + j, me, (*chip, c), src=x_ref) for j, chip in enumerate(chips)]
        for cp in first:
            cp.start()
        passed = [copy(4 + j, (*chip, c), sibling) for j, chip in enumerate(chips)]
        for j, chip in enumerate(chips):
            copy(1 + j, (*chip, c), me).wait_recv()
            passed[j].start()
        copy(0, sibling, me).wait_recv()
        for j, chip in enumerate(chips):
            copy(4 + j, (*chip, 1 - c), me).wait_recv()
        for cp in first + passed:
            cp.wait_send()
        mine.wait()

    return pl.pallas_call(
        body, name=name,
        out_shape=jax.ShapeDtypeStruct((N_DEV * m_per, n), x_shard.dtype),
        in_specs=[VMEM_SPEC], out_specs=VMEM_SPEC,
        scratch_shapes=[pltpu.SemaphoreType.DMA((7,)), pltpu.SemaphoreType.DMA((7,)), pltpu.SemaphoreType.DMA],
    )(x_shard)


class _GatherChips:
    def __init__(self, ws):
        self.inputs = [ws]
        self.out_shapes = [jax.ShapeDtypeStruct(ws.shape, ws.dtype)]
        self.aliases = {0: 0}
        self.scratch = [pltpu.SemaphoreType.DMA((12,)), pltpu.SemaphoreType.DMA((12,))]
        self.quarter = ws.shape[1] // 4

    def _copy(self, refs, k, chip, pc, part, to):
        _, out_ref, send_sems, recv_sems = refs
        rows = out_ref.at[2 * chip[0] + chip[1], pl.ds((2 * pc + part) * self.quarter, self.quarter), :]
        return pltpu.make_async_remote_copy(
            src_ref=rows, dst_ref=rows, send_sem=send_sems.at[k], recv_sem=recv_sems.at[k],
            device_id=to, device_id_type=MESH)

    def _sends(self, refs, phase):
        x, y, c = _place()
        sibling, x_nbr, y_nbr, diag = (x, y, 1 - c), (1 - x, y), (x, 1 - y), (1 - x, 1 - y)
        plan = {
            "start": [(0, (x, y), 0, (*x_nbr, c)), (1, (x, y), 1, (*y_nbr, c)),
                      (2, (x, y), 1, (*x_nbr, c)), (3, (x, y), 0, (*y_nbr, c))],
            "middle": [(4, x_nbr, 0, (*y_nbr, c)), (6, x_nbr, 0, sibling), (5, y_nbr, 1, (*x_nbr, c)),
                       (7, y_nbr, 1, sibling), (8, x_nbr, 1, sibling), (9, y_nbr, 0, sibling)],
            "wait": [(10, diag, 0, sibling), (11, diag, 1, sibling)],
        }[phase]
        return [self._copy(refs, k, chip, c, part, to) for k, chip, part, to in plan]

    def _landings(self, refs, phase):
        x, y, c = _place()
        me, x_nbr, y_nbr, diag = (x, y, c), (1 - x, y), (x, 1 - y), (1 - x, 1 - y)
        plan = {
            "middle": [(0, x_nbr, c, 0), (1, y_nbr, c, 1), (2, x_nbr, c, 1), (3, y_nbr, c, 0)],
            "wait": [(4, diag, c, 0), (5, diag, c, 1)],
            "sibling": [(6, x_nbr, 1 - c, 0), (7, y_nbr, 1 - c, 1), (8, x_nbr, 1 - c, 1), (9, y_nbr, 1 - c, 0),
                        (10, diag, 1 - c, 0), (11, diag, 1 - c, 1)],
        }[phase]
        return [self._copy(refs, k, chip, pc, part, me) for k, chip, pc, part in plan]

    def start(self, *refs):
        for cp in self._sends(refs, "start"):
            cp.start()

    def middle(self, *refs):
        landed = self._landings(refs, "middle")
        passed = self._sends(refs, "middle")
        landed[0].wait_recv()
        passed[0].start()
        passed[1].start()
        landed[1].wait_recv()
        passed[2].start()
        passed[3].start()
        landed[2].wait_recv()
        passed[4].start()
        landed[3].wait_recv()
        passed[5].start()

    def wait(self, *refs):
        landed = self._landings(refs, "wait")
        passed = self._sends(refs, "wait")
        for arrival, cp in zip(landed, passed):
            arrival.wait_recv()
            cp.start()
        for arrival in self._landings(refs, "sibling"):
            arrival.wait_recv()
        for phase in ("start", "middle", "wait"):
            for cp in self._sends(refs, phase):
                cp.wait_send()


def _allgather_chips(ws, name):
    gather = _GatherChips(ws)

    def body(*refs):
        gather.start(*refs)
        gather.middle(*refs)
        gather.wait(*refs)

    return pl.pallas_call(
        body, name=name, out_shape=gather.out_shapes[0],
        in_specs=[ANY], out_specs=ANY, input_output_aliases=gather.aliases, scratch_shapes=gather.scratch,
    )(ws)


def _sibling_half_swap(gs, name):
    n, r, cdim = gs.shape
    half = r // 2

    def body(g_ref, out_ref, send_sem, recv_sem):
        x, y, c = _place()
        cp = pltpu.make_async_remote_copy(
            src_ref=g_ref.at[:, pl.ds((1 - c) * half, half), :], dst_ref=out_ref,
            send_sem=send_sem, recv_sem=recv_sem, device_id=(x, y, 1 - c), device_id_type=MESH)
        cp.start()
        cp.wait()

    return pl.pallas_call(
        body, name=name,
        out_shape=jax.ShapeDtypeStruct((n, half, cdim), gs.dtype),
        in_specs=[ANY], out_specs=ANY,
        scratch_shapes=[pltpu.SemaphoreType.DMA, pltpu.SemaphoreType.DMA],
    )(gs)


class _ScatterChips:
    def __init__(self, pa):
        self.inputs = [pa]
        self.out_shapes = [jax.ShapeDtypeStruct(pa.shape, pa.dtype)]
        self.scratch = [pltpu.SemaphoreType.DMA((3,)), pltpu.SemaphoreType.DMA((3,)), pltpu.SemaphoreType.DMA]

    @staticmethod
    def _mine(p_ref, out_ref, send_sems, recv_sems, local_sem):
        x, y, _ = _place()
        return pltpu.make_async_copy(p_ref.at[2 * x + y], out_ref.at[2 * x + y], local_sem)

    @staticmethod
    def _remote(p_ref, out_ref, send_sems, recv_sems, local_sem, incoming):
        x, y, c = _place()
        me = 2 * x + y
        remote = []
        for j, (px, py) in enumerate([(1 - x, y), (x, 1 - y), (1 - x, 1 - y)]):
            remote.append(pltpu.make_async_remote_copy(
                src_ref=p_ref.at[me if incoming else 2 * px + py], dst_ref=out_ref.at[2 * px + py if incoming else me],
                send_sem=send_sems.at[j], recv_sem=recv_sems.at[j], device_id=(px, py, c), device_id_type=MESH))
        return remote

    def start(self, *refs):
        self._mine(*refs).start()
        for cp in self._remote(*refs, incoming=False):
            cp.start()

    def wait(self, *refs):
        for cp in self._remote(*refs, incoming=True):
            cp.wait_recv()
        for cp in self._remote(*refs, incoming=False):
            cp.wait_send()
        self._mine(*refs).wait()


def _fused_specs(fused):
    if fused is None:
        return [], [], [], [], []
    return (list(fused.inputs), [ANY] * len(fused.inputs), list(fused.out_shapes), [ANY] * len(fused.out_shapes),
            list(fused.scratch))


def _fused_aliases(fused, first_input, first_output):
    aliases = getattr(fused, "aliases", {}) if fused is not None else {}
    return {first_input + i: first_output + o for i, o in aliases.items()}


def _fused_begin(fused, grid, refs):
    if fused is not None:
        first = functools.reduce(lambda p, q: p & q, [pl.program_id(i) == 0 for i in range(len(grid))])
        pl.when(first)(lambda: fused.start(*refs))
        if hasattr(fused, "middle"):
            step = functools.reduce(lambda acc, ig: acc * ig[1] + pl.program_id(ig[0]), enumerate(grid), 0)
            pl.when(step == math.prod(grid) // 2)(lambda: fused.middle(*refs))


def _fused_end(fused, grid, refs):
    if fused is not None:
        last = functools.reduce(lambda p, q: p & q, [pl.program_id(i) == g - 1 for i, g in enumerate(grid)])
        pl.when(last)(lambda: fused.wait(*refs))


def _sibling_join(full, name):
    h2, cdim = full.shape
    h = h2 // 2

    def body(in_ref, out_ref, send_sem, recv_sem):
        del in_ref
        x, y, c = _place()
        mine = out_ref.at[pl.ds(c * h, h), :]
        cp = pltpu.make_async_remote_copy(
            src_ref=mine, dst_ref=mine, send_sem=send_sem, recv_sem=recv_sem,
            device_id=(x, y, 1 - c), device_id_type=MESH)
        cp.start()
        theirs = out_ref.at[pl.ds((1 - c) * h, h), :]
        pltpu.make_async_remote_copy(
            src_ref=theirs, dst_ref=theirs, send_sem=send_sem, recv_sem=recv_sem,
            device_id=(x, y, 1 - c), device_id_type=MESH).wait_recv()
        cp.wait_send()

    return pl.pallas_call(
        body, name=name,
        out_shape=jax.ShapeDtypeStruct(full.shape, full.dtype),
        in_specs=[ANY], out_specs=ANY, input_output_aliases={0: 0},
        scratch_shapes=[pltpu.SemaphoreType.DMA, pltpu.SemaphoreType.DMA],
    )(full)


def _cast_bf16_slab(w, name):
    r, cdim = w.shape
    tr, tc = _tile(r, 512), _tile(cdim, 2048)

    def body(pc_ref, w_ref, o_ref):
        o_ref[...] = w_ref[...].astype(BF16)

    return pl.pallas_call(
        body, name=name,
        grid_spec=pltpu.PrefetchScalarGridSpec(
            num_scalar_prefetch=1, grid=(r // tr, cdim // tc),
            in_specs=[pl.BlockSpec((tr, tc), lambda i, j, pc: (i, j))],
            out_specs=pl.BlockSpec((None, tr, tc), lambda i, j, pc: (pc[1], i, j))),
        out_shape=jax.ShapeDtypeStruct((N_CHIPS, r, cdim), BF16),
        compiler_params=_cparams(("parallel", "parallel")),
    )(_place_scalars(), w)


def _place_scalars():
    x, y, c = _place()
    return jnp.stack([c, 2 * x + y]).astype(jnp.int32)


def _pair_sum(gs, ra, name):
    n, r, cdim = gs.shape
    half = r // 2
    tr, tc = _tile(half, 512), _tile(cdim, 2048)
    nt = half // tr

    def body(pc_ref, g_ref, r_ref, o_ref, own_ref):
        val = g_ref[...].astype(F32) + r_ref[...].astype(F32)
        o_ref[...] = val.astype(BF16)

        @pl.when(pl.program_id(2) == pc_ref[1])
        def _():
            own_ref[...] = val

    return pl.pallas_call(
        body, name=name,
        grid_spec=pltpu.PrefetchScalarGridSpec(
            num_scalar_prefetch=1, grid=(nt, cdim // tc, n),
            in_specs=[pl.BlockSpec((None, tr, tc), lambda i, j, s, pc: (s, pc[0] * nt + i, j)),
                      pl.BlockSpec((None, tr, tc), lambda i, j, s, pc: (s, i, j))],
            out_specs=[pl.BlockSpec((None, tr, tc), lambda i, j, s, pc: (s, i, j)),
                       pl.BlockSpec((tr, tc), lambda i, j, s, pc: (i, j))]),
        out_shape=[jax.ShapeDtypeStruct((n, half, cdim), BF16), jax.ShapeDtypeStruct((half, cdim), F32)],
        compiler_params=_cparams(("parallel", "parallel", "arbitrary")),
    )(_place_scalars(), gs, ra)


def _chip_sum(rb, own, name):
    n, h, cdim = rb.shape
    tr, tc = _tile(h, 256), _tile(cdim, 2048)
    nt = h // tr

    def body(pc_ref, r_ref, own_ref, o_ref):
        chip = pc_ref[1]
        acc = None
        for p in range(n):
            term = jnp.where(chip == p, own_ref[...], r_ref[p].astype(F32))
            acc = term if acc is None else acc + term
        o_ref[...] = acc

    return pl.pallas_call(
        body, name=name,
        grid_spec=pltpu.PrefetchScalarGridSpec(
            num_scalar_prefetch=1, grid=(nt, cdim // tc),
            in_specs=[pl.BlockSpec((n, tr, tc), lambda i, j, pc: (0, i, j)),
                      pl.BlockSpec((tr, tc), lambda i, j, pc: (i, j))],
            out_specs=pl.BlockSpec((tr, tc), lambda i, j, pc: (pc[0] * nt + i, j))),
        out_shape=jax.ShapeDtypeStruct((2 * h, cdim), F32),
        compiler_params=_cparams(("parallel", "parallel")),
    )(_place_scalars(), rb, own)


def _adamw_math(w, g, m, v):
    m = ADAM_B1 * m + (1.0 - ADAM_B1) * g
    v = ADAM_B2 * v + (1.0 - ADAM_B2) * (g * g)
    m_hat = m / (1.0 - ADAM_B1 ** ADAM_STEP)
    v_hat = v / (1.0 - ADAM_B2 ** ADAM_STEP)
    delta = -ADAM_LR * (m_hat / (jnp.sqrt(v_hat) + ADAM_EPS) + ADAM_WD * w)
    return delta, m, v


def _adamw(w, g, m, v, name):
    r, cdim = w.shape
    tr, tc = _tile(r, 256), _tile(cdim, 2048)

    def body(w_ref, g_ref, m_ref, v_ref, go_ref, d_ref, nm_ref, nv_ref):
        gv = g_ref[...]
        d, nm, nv = _adamw_math(w_ref[...], gv, m_ref[...], v_ref[...])
        go_ref[...] = gv
        d_ref[...] = d
        nm_ref[...] = nm
        nv_ref[...] = nv

    spec = pl.BlockSpec((tr, tc), lambda i, j: (i, j))
    sds = jax.ShapeDtypeStruct((r, cdim), F32)
    return pl.pallas_call(
        body, name=name, grid=(r // tr, cdim // tc),
        in_specs=[spec] * 4, out_specs=[spec] * 4, out_shape=[sds] * 4,
        compiler_params=_cparams(("parallel", "parallel")),
    )(w, g, m, v)


def _matmul(a, b, *, grid, a_spec, b_spec, out_spec, out_shape, acc_shape, dims, name, bias=None, bias_spec=None,
            silu_a=False, fused=None):
    nk = grid[2]
    f_in, f_in_specs, f_out, f_out_specs, f_scratch = _fused_specs(fused)
    n_in = 2 + (bias is not None)

    acc_scratch = [pltpu.VMEM(acc_shape, F32)] if nk > 1 else []

    def body(*refs):
        a_ref, b_ref = refs[:2]
        bias_ref = refs[2] if bias is not None else None
        o_ref = refs[n_in + len(f_in)]
        n_fixed = n_in + len(f_in) + 1 + len(f_out)
        f_refs = (*refs[n_in:n_in + len(f_in)], *refs[n_in + len(f_in) + 1:n_fixed],
                  *refs[n_fixed + len(acc_scratch):])
        _fused_begin(fused, grid, f_refs)

        def product():
            if len(a_ref.shape) == 3:
                tks = a_ref.shape[2]
                parts = [_dot(a_ref[i], b_ref[:, i * tks:(i + 1) * tks], dims) for i in range(a_ref.shape[0])]
                return functools.reduce(lambda p, q: p + q, parts)
            av = a_ref[...]
            if silu_a:
                av = av * _sigmoid(av)
            return _dot(av.astype(BF16), b_ref[...].astype(BF16), dims)

        def finish(res):
            if bias is not None:
                res = res + bias_ref[...]
            o_ref[...] = res.astype(o_ref.dtype)

        if nk == 1:
            finish(product())
        else:
            acc_ref = refs[n_fixed]
            k = pl.program_id(2)

            @pl.when(k == 0)
            def _():
                acc_ref[...] = product()

            if nk > 2:
                @pl.when((k > 0) & (k < nk - 1))
                def _():
                    acc_ref[...] += product()

            @pl.when(k == nk - 1)
            def _():
                finish(acc_ref[...] + product())

        _fused_end(fused, grid, f_refs)

    in_specs = [a_spec, b_spec] + ([] if bias is None else [bias_spec]) + f_in_specs
    args = (a, b) + (() if bias is None else (bias,)) + tuple(f_in)
    sem = ("parallel", "parallel", "arbitrary") if fused is None else ("arbitrary",) * 3
    res = pl.pallas_call(
        body, name=name, grid=grid, in_specs=in_specs, out_specs=[out_spec] + f_out_specs,
        out_shape=[out_shape] + f_out,
        scratch_shapes=acc_scratch + f_scratch,
        input_output_aliases=_fused_aliases(fused, n_in, 1),
        compiler_params=_cparams(sem),
    )(*args)
    return res[0] if fused is None else tuple(res)


def _mm_tiles(m, n, k):
    return _tile(m, 1024), _tile(n, 1024), _tile(k, 4096)


def _proj_fwd(h2, ws_in, fused=None):
    t, d = h2.shape
    _, _, cs = ws_in.shape
    w = cs // 2
    tm, tn, tk = _mm_tiles(t, w, d)
    nps, npseg = cs // tn, w // tn
    return _matmul(
        h2, ws_in, grid=(t // tm, 8 * npseg, d // tk), dims=NN, name="proj_fwd", fused=fused,
        a_spec=pl.BlockSpec((tm, tk), lambda m, n, k: (m, k)),
        b_spec=pl.BlockSpec((None, tk, tn), lambda m, n, k: (n // nps, k, n % nps)),
        out_spec=pl.BlockSpec((None, tm, tn), lambda m, n, k: (n // npseg, m, n % npseg)),
        out_shape=jax.ShapeDtypeStruct((8, t, w), F32), acc_shape=(tm, tn))


def _proj_bwd_x(dproj8, ws_in, fused=None):
    _, t, w = dproj8.shape
    _, d, cs = ws_in.shape
    tm, tn, _ = _mm_tiles(t, d, w)
    return _matmul(
        dproj8, ws_in, grid=(t // tm, d // tn, N_CHIPS), dims=NT, name="proj_bwd_x", fused=fused,
        a_spec=pl.BlockSpec((2, tm, w), lambda m, n, k: (k, m, 0)),
        b_spec=pl.BlockSpec((None, tn, cs), lambda m, n, k: (k, n, 0)),
        out_spec=pl.BlockSpec((tm, tn), lambda m, n, k: (m, n)),
        out_shape=jax.ShapeDtypeStruct((t, d), F32), acc_shape=(tm, tn))


def _proj_bwd_w(h2, dproj8):
    t, d = h2.shape
    _, _, w = dproj8.shape
    cs = 2 * w
    tm, tn, tk = _mm_tiles(d, w, t)
    nps, npseg = cs // tn, w // tn
    return _matmul(
        h2, dproj8, grid=(d // tm, 8 * npseg, t // tk), dims=TN, name="proj_bwd_w",
        a_spec=pl.BlockSpec((tk, tm), lambda m, n, k: (k, m)),
        b_spec=pl.BlockSpec((None, tk, tn), lambda m, n, k: (n // npseg, k, n % npseg)),
        out_spec=pl.BlockSpec((None, tm, tn), lambda m, n, k: (n // nps, m, n % nps)),
        out_shape=jax.ShapeDtypeStruct((N_CHIPS, d, cs), BF16), acc_shape=(tm, tn))


def _out_fwd(y2, w_out):
    _, t, w = y2.shape
    _, d = w_out.shape
    tm, tn, tk = _mm_tiles(t, d, w)
    kpg = w // tk
    return _matmul(
        y2, w_out, grid=(t // tm, d // tn, 2 * kpg), dims=NN, name="out_fwd",
        a_spec=pl.BlockSpec((None, tm, tk), lambda m, n, k: (k // kpg, m, k % kpg)),
        b_spec=pl.BlockSpec((tk, tn), lambda m, n, k: (k, n)),
        out_spec=pl.BlockSpec((tm, tn), lambda m, n, k: (m, n)),
        out_shape=jax.ShapeDtypeStruct((t, d), F32), acc_shape=(tm, tn))


def _out_bwd_y(dout, w_out):
    t, d = dout.shape
    w = w_out.shape[0] // 2
    tm, tn, tk = _mm_tiles(t, w, d)
    npg = w // tn
    return _matmul(
        dout, w_out, grid=(t // tm, 2 * npg, d // tk), dims=NT, name="out_bwd_y",
        a_spec=pl.BlockSpec((tm, tk), lambda m, n, k: (m, k)),
        b_spec=pl.BlockSpec((tn, tk), lambda m, n, k: (n, k)),
        out_spec=pl.BlockSpec((None, tm, tn), lambda m, n, k: (n // npg, m, n % npg)),
        out_shape=jax.ShapeDtypeStruct((2, t, w), F32), acc_shape=(tm, tn))


def _out_bwd_w(y2, dout):
    _, t, w = y2.shape
    _, d = dout.shape
    tm, tn, tk = _mm_tiles(w, d, t)
    mpg = w // tm
    return _matmul(
        y2, dout, grid=(2 * mpg, d // tn, t // tk), dims=TN, name="out_bwd_w",
        a_spec=pl.BlockSpec((None, tk, tm), lambda m, n, k: (m // mpg, k, m % mpg)),
        b_spec=pl.BlockSpec((tk, tn), lambda m, n, k: (k, n)),
        out_spec=pl.BlockSpec((tm, tn), lambda m, n, k: (m, n)),
        out_shape=jax.ShapeDtypeStruct((2 * w, d), BF16), acc_shape=(tm, tn))


def _mod_fwd(c_all, w_ada, b_ada):
    bt, d = c_all.shape
    _, n = w_ada.shape
    tn, tk = _tile(n, 512), _tile(d, 1024)
    return _matmul(
        c_all, w_ada, grid=(1, n // tn, d // tk), dims=NN, name="mod_fwd", silu_a=True,
        a_spec=pl.BlockSpec((bt, tk), lambda i, j, l: (0, l)),
        b_spec=pl.BlockSpec((tk, tn), lambda i, j, l: (l, j)),
        bias=b_ada, bias_spec=pl.BlockSpec((1, tn), lambda i, j, l: (0, j)),
        out_spec=pl.BlockSpec((bt, tn), lambda i, j, l: (0, j)),
        out_shape=jax.ShapeDtypeStruct((bt, n), F32), acc_shape=(bt, tn))


def _norm_mod_fwd(x, g_norm, scale, shift):
    b, s, d = x.shape
    ts = _tile(s, 256)

    def body(x_ref, g_ref, sc_ref, sh_ref, h_ref):
        xv = x_ref[...]
        r = lax.rsqrt(jnp.mean(xv * xv, axis=-1, keepdims=True) + EPS)
        y = (xv * r) * g_ref[...]
        h_ref[...] = (y * (1.0 + sc_ref[...]) + sh_ref[...]).astype(BF16)

    row = pl.BlockSpec((None, ts, d), lambda i, j: (i, j, 0))
    per_b = pl.BlockSpec((None, 1, d), lambda i, j: (i, 0, 0))
    return pl.pallas_call(
        body, name="norm_mod_fwd", grid=(b, s // ts),
        in_specs=[row, pl.BlockSpec((1, d), lambda i, j: (0, 0)), per_b, per_b],
        out_specs=row, out_shape=jax.ShapeDtypeStruct((b, s, d), BF16),
        compiler_params=_cparams(("parallel", "parallel")),
    )(x, g_norm, scale, shift)


def _norm_mod_bwd(x, dh, dx1, g_norm, scale):
    b, s, d = x.shape
    ts = _tile(s, 256)

    def body(x_ref, dh_ref, dx1_ref, g_ref, sc_ref, gx_ref, dsh_ref, dsc_ref, dg_ref):
        i, j = pl.program_id(0), pl.program_id(1)

        @pl.when(j == 0)
        def _():
            dsh_ref[...] = jnp.zeros_like(dsh_ref)
            dsc_ref[...] = jnp.zeros_like(dsc_ref)

        @pl.when((i == 0) & (j == 0))
        def _():
            dg_ref[...] = jnp.zeros_like(dg_ref)

        xv, dhv, g = x_ref[...], dh_ref[...], g_ref[...]
        r = lax.rsqrt(jnp.mean(xv * xv, axis=-1, keepdims=True) + EPS)
        xh = xv * r
        dsh_ref[...] += jnp.sum(dhv, axis=0, keepdims=True)
        dsc_ref[...] += jnp.sum(dhv * (xh * g), axis=0, keepdims=True)
        dn = dhv * (1.0 + sc_ref[...])
        dg_ref[...] += jnp.sum(dn * xh, axis=0, keepdims=True)
        u = dn * g
        dx = r * u - xv * (r * r * r) * jnp.mean(u * xv, axis=-1, keepdims=True)
        gx_ref[...] = dx1_ref[...] + dx

    row = pl.BlockSpec((None, ts, d), lambda i, j: (i, j, 0))
    per_b = pl.BlockSpec((None, 1, d), lambda i, j: (i, 0, 0))
    vec = pl.BlockSpec((1, d), lambda i, j: (0, 0))
    return pl.pallas_call(
        body, name="norm_mod_bwd", grid=(b, s // ts),
        in_specs=[row, row, row, vec, per_b],
        out_specs=[row, per_b, per_b, vec],
        out_shape=[jax.ShapeDtypeStruct((b, s, d), F32), jax.ShapeDtypeStruct((b, 1, d), F32),
                   jax.ShapeDtypeStruct((b, 1, d), F32), jax.ShapeDtypeStruct((1, d), F32)],
        compiler_params=_cparams(("arbitrary", "arbitrary")),
    )(x, dh, dx1, g_norm, scale)


def _loss_head(x, out, gate, g_final, target):
    b, s, d = x.shape
    ts = _tile(s, 256)

    def body(x_ref, o_ref, gt_ref, g_ref, t_ref, dx1_ref, dout_ref, dgt_ref, dg_ref, loss_ref):
        i, j = pl.program_id(0), pl.program_id(1)

        @pl.when(j == 0)
        def _():
            dgt_ref[...] = jnp.zeros_like(dgt_ref)

        @pl.when((i == 0) & (j == 0))
        def _():
            dg_ref[...] = jnp.zeros_like(dg_ref)
            loss_ref[...] = jnp.zeros_like(loss_ref)

        ov, gt, g = o_ref[...], gt_ref[...], g_ref[...]
        x1 = x_ref[...] + gt * ov
        r = lax.rsqrt(jnp.mean(x1 * x1, axis=-1, keepdims=True) + EPS)
        xh = x1 * r
        err = xh * g - t_ref[...]
        loss_ref[...] += 0.5 * jnp.sum(jnp.mean(err * err, axis=-1, keepdims=True))
        dfin = err * (1.0 / d)
        dg_ref[...] += jnp.sum(dfin * xh, axis=0, keepdims=True)
        u = dfin * g
        dx1 = r * u - x1 * (r * r * r) * jnp.mean(u * x1, axis=-1, keepdims=True)
        dx1_ref[...] = dx1
        dgt_ref[...] += jnp.sum(dx1 * ov, axis=0, keepdims=True)
        dout_ref[...] = (gt * dx1).astype(BF16)

    row = pl.BlockSpec((None, ts, d), lambda i, j: (i, j, 0))
    per_b = pl.BlockSpec((None, 1, d), lambda i, j: (i, 0, 0))
    vec = pl.BlockSpec((1, d), lambda i, j: (0, 0))
    return pl.pallas_call(
        body, name="loss_head", grid=(b, s // ts),
        in_specs=[row, row, per_b, vec, row],
        out_specs=[row, row, per_b, vec, pl.BlockSpec((1, 128), lambda i, j: (0, 0))],
        out_shape=[jax.ShapeDtypeStruct((b, s, d), F32), jax.ShapeDtypeStruct((b, s, d), BF16),
                   jax.ShapeDtypeStruct((b, 1, d), F32), jax.ShapeDtypeStruct((1, d), F32),
                   jax.ShapeDtypeStruct((1, 128), F32)],
        compiler_params=_cparams(("arbitrary", "arbitrary")),
    )(x, out, gate, g_final, target)


def _head_out(o, zg, g):
    rinv = lax.rsqrt(jnp.mean(o * o, axis=-1, keepdims=True) + EPS)
    return ((o * rinv) * g) * (zg * _sigmoid(zg))


def _head_out_bwd(o, zg, g, dy):
    rinv = lax.rsqrt(jnp.mean(o * o, axis=-1, keepdims=True) + EPS)
    rn = o * rinv
    sg = _sigmoid(zg)
    sil = zg * sg
    dzg = dy * (rn * g) * (sg * (1.0 + zg * (1.0 - sg)))
    dg = jnp.sum(dy * rn * sil, axis=0, keepdims=True)
    drn = dy * g * sil
    do = rinv * drn - o * (rinv * rinv * rinv) * jnp.mean(drn * o, axis=-1, keepdims=True)
    return do, dzg, dg


def _head_spec(s):
    return pl.BlockSpec((None, s, HEAD_DIM), lambda b, h: (b, 0, h))


def _seg_spec(s, seg):
    return pl.BlockSpec((None, None, s, HEAD_DIM), lambda b, h: (seg, b, 0, h))


def _seg4_spec(s, group):
    return pl.BlockSpec((4, None, s, HEAD_DIM), lambda b, h: (group, b, 0, h))


SB_Q_BLOCK = 1024
SB_K_BLOCK = 256


SB_DEAD_LOG2 = -160.0
LOG2_E = 1.4426950408889634
SB_LOGIT_SCALE = LOG2_E / math.sqrt(HEAD_DIM)


def _sb_terms(raw, valid):
    t = jnp.where(valid, raw * SB_LOGIT_SCALE, NEG_BIG)
    e = jnp.exp2(-jnp.abs(t))
    l1m = -(jnp.maximum(t, 0.0) + jnp.log2(1.0 + e))
    return t, l1m, e


def _split_dot(a, u):
    hi = a.astype(BF16)
    lo = (a - hi.astype(F32)).astype(BF16)
    return _dot(hi, u, NN) + _dot(lo, u, NN)


def _sb_fwd(proj8, g_sb):
    _, b, s, w = proj8.shape
    n_heads = w // HEAD_DIM
    tq, tk = _tile(s, SB_Q_BLOCK), _tile(s, SB_K_BLOCK)
    nq, kpq = s // tq, tq // tk
    scale = 1.0 / math.sqrt(HEAD_DIM)

    def body(q_ref, k_ref, v_ref, zg_ref, g_ref, o_ref, tot_ref, y_ref, trips_ref):
        u_excl = (lax.broadcasted_iota(jnp.int32, (tk, tk), 0)
                  > lax.broadcasted_iota(jnp.int32, (tk, tk), 1)).astype(BF16)
        ahead = lax.broadcasted_iota(jnp.int32, (tk, tk), 0) - lax.broadcasted_iota(jnp.int32, (tk, tk), 1)
        g = g_ref[...]

        def qblock(i, _):
            qi = [i * kpq + n for n in range(kpq)]
            rows = [pl.ds(pl.multiple_of(x * tk, tk), tk) for x in qi]
            qs = [q_ref[r, :].astype(BF16) for r in rows]

            def alive(state):
                jj, _, csum = state
                top = functools.reduce(jnp.maximum, [jnp.max(x) for x in csum])
                return (jj <= qi[-1]) & ((jj == 0) | (top > SB_DEAD_LOG2))

            def trip(state):
                jj, acc, csum = state
                cols = [pl.ds(pl.multiple_of(jnp.maximum(x - jj, 0) * tk, tk), tk) for x in qi]
                raw = [_dot(q, k_ref[c, :].astype(BF16), NT) for q, c in zip(qs, cols)]
                terms = [_sb_terms(y, (ahead > -jj * tk) & (jj <= x)) for y, x in zip(raw, qi)]
                sums = [_split_dot(l1m, u_excl) for _, l1m, _ in terms]
                a = [jnp.exp2((t + l1m) + (part + cs)) for (t, l1m, _), part, cs in zip(terms, sums, csum)]
                acc = [x + _dot(y.astype(BF16), v_ref[c, :].astype(BF16), NN) for x, y, c in zip(acc, a, cols)]
                csum = [cs + jnp.sum(l1m, axis=1, keepdims=True) for cs, (_, l1m, _) in zip(csum, terms)]
                return jj + 1, acc, csum

            trips, acc, tot = lax.while_loop(
                alive, trip, (jnp.int32(0), [jnp.zeros((tk, HEAD_DIM), F32)] * kpq, [jnp.zeros((tk, 1), F32)] * kpq))
            for r, x, y in zip(rows, acc, tot):
                o_ref[r, :] = x
                tot_ref[r, :] = jnp.broadcast_to(y, (tk, HEAD_DIM))
                y_ref[r, :] = _head_out(x, zg_ref[r, :], g).astype(BF16)
            trips_ref[0, i] = trips.astype(F32)
            return 0

        lax.fori_loop(0, nq, qblock, 0)

    return pl.pallas_call(
        body, name="sb_fwd", grid=(b, n_heads),
        in_specs=[_seg_spec(s, 0), _seg_spec(s, 1), _seg_spec(s, 2), _seg_spec(s, 3),
                  pl.BlockSpec((1, HEAD_DIM), lambda i, h: (0, h))],
        out_specs=[_head_spec(s), _head_spec(s), _seg_spec(s, 0),
                   pl.BlockSpec((None, None, 1, nq), lambda i, h: (i, h, 0, 0), memory_space=pltpu.SMEM)],
        out_shape=[jax.ShapeDtypeStruct((b, s, w), F32), jax.ShapeDtypeStruct((b, s, w), F32),
                   jax.ShapeDtypeStruct((2, b, s, w), BF16), jax.ShapeDtypeStruct((b, n_heads, 1, nq), F32)],
        compiler_params=_cparams(("parallel", "parallel")),
    )(proj8, proj8, proj8, proj8, g_sb)


def _sb_bwd(proj8, o_sb, tot_sb, trips, dy2, g_sb, fused=None):
    _, b, s, w = proj8.shape
    n_heads = w // HEAD_DIM
    tq, tk = _tile(s, SB_Q_BLOCK), _tile(s, SB_K_BLOCK)
    nq, kpq = s // tq, tq // tk
    scale = 1.0 / math.sqrt(HEAD_DIM)

    f_in, f_in_specs, f_out, f_out_specs, f_scratch = _fused_specs(fused)
    grid = (b, n_heads)

    def body(*refs):
        q_ref, k_ref, v_ref, zg_ref, o_ref, tot_ref, dy_ref, g_ref, trips_ref = refs[:9]
        dp_ref, dg_ref = refs[9 + len(f_in):11 + len(f_in)]
        do_s, dk_s, dv_s = refs[11 + len(f_in) + len(f_out):14 + len(f_in) + len(f_out)]
        f_refs = (*refs[9:9 + len(f_in)], *refs[11 + len(f_in):11 + len(f_in) + len(f_out)],
                  *refs[14 + len(f_in) + len(f_out):])
        _fused_begin(fused, grid, f_refs)
        dq_ref, dk_ref, dv_ref, dzg_ref = (dp_ref.at[n] for n in range(4))
        ri = lax.broadcasted_iota(jnp.int32, (tk, tk), 0)
        ci = lax.broadcasted_iota(jnp.int32, (tk, tk), 1)
        u_le = (ri <= ci).astype(BF16)
        u_lt = (ri < ci).astype(BF16)
        ahead = ri - ci
        g = g_ref[...]

        def prologue(i, dg):
            rows = pl.ds(pl.multiple_of(i * tq, tq), tq)
            do, dzg, dgi = _head_out_bwd(o_ref[rows, :], zg_ref[rows, :], g, dy_ref[rows, :])
            dzg_ref[rows, :] = dzg.astype(BF16)
            do_s[rows, :] = do.astype(BF16)
            return dg + dgi

        dg_ref[...] = lax.fori_loop(0, nq, prologue, jnp.zeros((1, HEAD_DIM), F32))
        dk_s[...] = jnp.zeros_like(dk_s)
        dv_s[...] = jnp.zeros_like(dv_s)

        def qblock(i, _):
            qi = [i * kpq + n for n in range(kpq)]
            rows = [pl.ds(pl.multiple_of(x * tk, tk), tk) for x in qi]
            qs = [q_ref[r, :].astype(BF16) for r in rows]
            dos = [do_s[r, :] for r in rows]
            tots = [tot_ref[r, :][:, :1] for r in rows]
            walked = jnp.clip(trips_ref[0, i].astype(jnp.int32), 1, qi[-1] + 1)

            def trip(u, carry):
                dq, pre_l, pre_g = carry
                jj = walked - 1 - u
                cols = [pl.ds(pl.multiple_of(jnp.maximum(x - jj, 0) * tk, tk), tk) for x in qi]
                ks = [k_ref[c, :].astype(BF16) for c in cols]
                raw = [_dot(q, k, NT) for q, k in zip(qs, ks)]
                da = [_dot(d, v_ref[c, :].astype(BF16), NT) for d, c in zip(dos, cols)]
                terms = [_sb_terms(y, (ahead > -jj * tk) & (jj <= x)) for y, x in zip(raw, qi)]
                sums_l = [_split_dot(l1m, u_le) for _, l1m, _ in terms]
                a = [jnp.exp2((t + l1m) + (tt - (part + pl_)))
                     for (t, l1m, _), part, tt, pl_ in zip(terms, sums_l, tots, pre_l)]
                gg = [x * d for x, d in zip(a, da)]
                sums_g = [_split_dot(x, u_lt) for x in gg]
                dzs = []
                for (t, _, e), x, part, pg in zip(terms, gg, sums_g, pre_g):
                    inv = 1.0 / (1.0 + e)
                    sig = jnp.where(t >= 0.0, inv, e * inv)
                    dzs.append(((x - sig * (x + (part + pg))) * scale).astype(BF16))
                dq = [x + _dot(y, k, NN) for x, y, k in zip(dq, dzs, ks)]
                for x, y, c, q, d in zip(dzs, a, cols, qs, dos):
                    dk_s[c, :] += _dot(x, q, TN)
                    dv_s[c, :] += _dot(y.astype(BF16), d, TN)
                pre_l = [x + jnp.sum(l1m, axis=1, keepdims=True) for x, (_, l1m, _) in zip(pre_l, terms)]
                pre_g = [x + jnp.sum(y, axis=1, keepdims=True) for x, y in zip(pre_g, gg)]
                return dq, pre_l, pre_g

            zero = [jnp.zeros((tk, 1), F32)] * kpq
            dq, _, _ = lax.fori_loop(0, walked, trip, ([jnp.zeros((tk, HEAD_DIM), F32)] * kpq, zero, zero))
            for r, x in zip(rows, dq):
                dq_ref[r, :] = x.astype(BF16)
            return 0

        lax.fori_loop(0, nq, qblock, 0)
        dk_ref[...] = dk_s[...].astype(BF16)
        dv_ref[...] = dv_s[...].astype(BF16)
        _fused_end(fused, grid, f_refs)

    return pl.pallas_call(
        body, name="sb_bwd", grid=grid,
        in_specs=[_seg_spec(s, 0), _seg_spec(s, 1), _seg_spec(s, 2), _seg_spec(s, 3),
                  _head_spec(s), _head_spec(s), _seg_spec(s, 0),
                  pl.BlockSpec((1, HEAD_DIM), lambda i, h: (0, h)),
                  pl.BlockSpec((None, None, 1, nq), lambda i, h: (i, h, 0, 0), memory_space=pltpu.SMEM)] + f_in_specs,
        out_specs=[_seg4_spec(s, 0), pl.BlockSpec((None, 1, HEAD_DIM), lambda i, h: (i, 0, h))] + f_out_specs,
        out_shape=[jax.ShapeDtypeStruct((8, b, s, w), BF16), jax.ShapeDtypeStruct((b, 1, w), F32)] + f_out,
        scratch_shapes=[pltpu.VMEM((s, HEAD_DIM), BF16), pltpu.VMEM((s, HEAD_DIM), F32),
                        pltpu.VMEM((s, HEAD_DIM), F32)] + f_scratch,
        compiler_params=_cparams(("arbitrary", "arbitrary")),
    )(proj8, proj8, proj8, proj8, o_sb, tot_sb, dy2, g_sb, trips, *f_in)


DIL_BLOCK = 128
DIL_GROUP = 4


def _dil_chunks(s, r):
    length = s // r
    out = []
    for rho in range(r):
        for cc in range(length // DIL_BLOCK):
            if r == 1:
                nat = pl.ds(cc * DIL_BLOCK, DIL_BLOCK)
            else:
                nat = pl.ds(rho + r * DIL_BLOCK * cc, DIL_BLOCK, stride=r)
            off = rho * length + cc * DIL_BLOCK
            out.append((nat, pl.ds(off, DIL_BLOCK), pl.ds(off + DIL_BLOCK, DIL_BLOCK)))
    return out


def _dil_masks(slope, r):
    n = DIL_BLOCK
    ri = lax.broadcasted_iota(jnp.int32, (n, 2 * n), 0)
    ci = lax.broadcasted_iota(jnp.int32, (n, 2 * n), 1)
    steps = ri - ci + n
    inside = (steps >= 0) & (steps <= n)
    bias = slope * (steps.astype(F32) * r)
    return jnp.where(inside, -bias, NEG_BIG), jnp.where(inside & (ci >= n), -bias, NEG_BIG)


def _dil_scores(q, k_pc, masks, first):
    return _dot(q, k_pc, NT) * (1.0 / math.sqrt(HEAD_DIM)) + jnp.where(first, masks[1], masks[0])


def _dil_check(s):
    assert (s // DIL_BLOCK) % DIL_GROUP == 0, s
    for window, r in DIL_PAIRS:
        assert window // r == DIL_BLOCK and s % (r * DIL_BLOCK) == 0, (s, window, r)


def _dil_fwd(proj8, g_dil, slopes, y2):
    _, b, s, w = proj8.shape
    n_heads = w // HEAD_DIM
    _dil_check(s)
    n = DIL_BLOCK
    nt = s // n

    def body(q_ref, k_ref, v_ref, zg_ref, g_ref, sl_ref, y_in, o_ref, lse_ref, y_ref,
             qp, kp, vp, pnum, pm, pl_, acc_s, m_s, l_s):
        del y_in
        slope = sl_ref[...][:, :1]
        kp[pl.ds(0, n), :] = jnp.zeros((n, HEAD_DIM), BF16)
        vp[pl.ds(0, n), :] = jnp.zeros((n, HEAD_DIM), BF16)

        for (window, r) in DIL_PAIRS:
            nb = (s // r) // n
            masks = _dil_masks(slope, float(r))
            for nat, per, padded in _dil_chunks(s, r):
                qp[per, :] = q_ref[nat, :].astype(BF16)
                kp[padded, :] = k_ref[nat, :].astype(BF16)
                vp[padded, :] = v_ref[nat, :].astype(BF16)
            num_t, m_t, l_t = (acc_s, m_s, l_s) if r == 1 else (pnum, pm, pl_)

            def tiles(tt, _):
                ts = [tt * DIL_GROUP + i for i in range(DIL_GROUP)]
                rows = [pl.ds(pl.multiple_of(t * n, n), n) for t in ts]
                both = [pl.ds(pl.multiple_of(t * n, n), 2 * n) for t in ts]
                sc = [_dil_scores(qp[rw, :], kp[bt, :], masks, lax.rem(t, nb) == 0)
                      for t, rw, bt in zip(ts, rows, both)]
                m = [jnp.max(x, axis=1, keepdims=True) for x in sc]
                p = [jnp.exp(x - mx) for x, mx in zip(sc, m)]
                num = [_dot(x.astype(BF16), vp[bt, :], NN) for x, bt in zip(p, both)]
                for rw, x, mx, nm in zip(rows, p, m, num):
                    num_t[rw, :] = nm
                    m_t[rw, :] = jnp.broadcast_to(mx, (n, HEAD_DIM))
                    l_t[rw, :] = jnp.broadcast_to(jnp.sum(x, axis=1, keepdims=True), (n, HEAD_DIM))
                return 0

            lax.fori_loop(0, nt // DIL_GROUP, tiles, 0)
            if r != 1:
                for nat, per, _ in _dil_chunks(s, r):
                    m_old, m_new_p = m_s[nat, :], pm[per, :]
                    m_new = jnp.maximum(m_old, m_new_p)
                    a_old, a_p = jnp.exp(m_old - m_new), jnp.exp(m_new_p - m_new)
                    m_s[nat, :] = m_new
                    l_s[nat, :] = l_s[nat, :] * a_old + pl_[per, :] * a_p
                    acc_s[nat, :] = acc_s[nat, :] * a_old + pnum[per, :] * a_p

        g = g_ref[...]

        def finish(t, _):
            rows = pl.ds(pl.multiple_of(t * n, n), n)
            l = l_s[rows, :]
            o = acc_s[rows, :] / l
            o_ref[rows, :] = o
            lse_ref[rows, :] = m_s[rows, :] + jnp.log(l)
            y_ref[rows, :] = _head_out(o, zg_ref[rows, :], g).astype(BF16)
            return 0

        lax.fori_loop(0, nt, finish, 0)

    f32_s = pltpu.VMEM((s, HEAD_DIM), F32)
    bf_s = pltpu.VMEM((s, HEAD_DIM), BF16)
    bf_pad = pltpu.VMEM((s + n, HEAD_DIM), BF16)
    return pl.pallas_call(
        body, name="dil_fwd", grid=(b, n_heads),
        in_specs=[_seg_spec(s, 4), _seg_spec(s, 5), _seg_spec(s, 6), _seg_spec(s, 7),
                  pl.BlockSpec((1, HEAD_DIM), lambda i, h: (0, h)),
                  pl.BlockSpec((None, 1, HEAD_DIM), lambda i, h: (h, 0, 0)), ANY],
        out_specs=[_head_spec(s), _head_spec(s), _seg_spec(s, 1)],
        out_shape=[jax.ShapeDtypeStruct((b, s, w), F32), jax.ShapeDtypeStruct((b, s, w), F32),
                   jax.ShapeDtypeStruct((2, b, s, w), BF16)],
        scratch_shapes=[bf_s, bf_pad, bf_pad, f32_s, f32_s, f32_s, f32_s, f32_s, f32_s],
        input_output_aliases={6: 2},
        compiler_params=_cparams(("parallel", "parallel")),
    )(proj8, proj8, proj8, proj8, g_dil, slopes, y2)


def _dil_bwd(proj8, o_dl, lse_dl, dy2, g_dil, slopes, dproj8):
    _, b, s, w = proj8.shape
    n_heads = w // HEAD_DIM
    _dil_check(s)
    n = DIL_BLOCK
    nt = s // n
    scale = 1.0 / math.sqrt(HEAD_DIM)

    def body(q_ref, k_ref, v_ref, zg_ref, o_ref, lse_ref, dy_ref, g_ref, sl_ref, dp_in, dp_ref, dg_ref,
             do_n, dt_n, dq_n, dk_n, dv_n, qp, kp, vp, dop, dtp, lsep, pdq, pdk, pdv):
        del dp_in
        dq_ref, dk_ref, dv_ref, dzg_ref = (dp_ref.at[i] for i in range(4))
        slope = sl_ref[...][:, :1]
        g = g_ref[...]

        def prologue(t, dg):
            rows = pl.ds(pl.multiple_of(t * n, n), n)
            o = o_ref[rows, :]
            do, dzg, dgi = _head_out_bwd(o, zg_ref[rows, :], g, dy_ref[rows, :])
            dzg_ref[rows, :] = dzg.astype(BF16)
            do_n[rows, :] = do
            dt_n[rows, :] = jnp.broadcast_to(jnp.sum(do * o, axis=-1, keepdims=True), (n, HEAD_DIM))
            return dg + dgi

        dg_ref[...] = lax.fori_loop(0, nt, prologue, jnp.zeros((1, HEAD_DIM), F32))
        dq_n[...] = jnp.zeros_like(dq_n)
        dk_n[...] = jnp.zeros_like(dk_n)
        dv_n[...] = jnp.zeros_like(dv_n)
        kp[pl.ds(0, n), :] = jnp.zeros((n, HEAD_DIM), BF16)
        vp[pl.ds(0, n), :] = jnp.zeros((n, HEAD_DIM), BF16)

        for (window, r) in DIL_PAIRS:
            nb = (s // r) // n
            masks = _dil_masks(slope, float(r))
            for nat, per, padded in _dil_chunks(s, r):
                qp[per, :] = q_ref[nat, :].astype(BF16)
                kp[padded, :] = k_ref[nat, :].astype(BF16)
                vp[padded, :] = v_ref[nat, :].astype(BF16)
                dop[per, :] = do_n[nat, :].astype(BF16)
                dtp[per, :] = dt_n[nat, :]
                lsep[per, :] = lse_ref[nat, :]
            pdk[...] = jnp.zeros_like(pdk)
            pdv[...] = jnp.zeros_like(pdv)

            def tiles(tt, _):
                ts = [tt * DIL_GROUP + i for i in range(DIL_GROUP)]
                rows = [pl.ds(pl.multiple_of(t * n, n), n) for t in ts]
                both = [pl.ds(pl.multiple_of(t * n, n), 2 * n) for t in ts]
                q = [qp[rw, :] for rw in rows]
                do = [dop[rw, :] for rw in rows]
                sc = [_dil_scores(qq, kp[bt, :], masks, lax.rem(t, nb) == 0) for t, qq, bt in zip(ts, q, both)]
                dp = [_dot(dd, vp[bt, :], NT) for dd, bt in zip(do, both)]
                p = [jnp.exp(x - lsep[rw, :][:, :1]) for x, rw in zip(sc, rows)]
                ds = [((x * (y - dtp[rw, :][:, :1])) * scale).astype(BF16) for x, y, rw in zip(p, dp, rows)]
                dq = [_dot(x, kp[bt, :], NN) for x, bt in zip(ds, both)]
                dk = [_dot(x, qq, TN) for x, qq in zip(ds, q)]
                dv = [_dot(x.astype(BF16), dd, TN) for x, dd in zip(p, do)]
                for rw, bt, x, y, z in zip(rows, both, dq, dk, dv):
                    pdq[rw, :] = x
                    pdk[bt, :] += y
                    pdv[bt, :] += z
                return 0

            lax.fori_loop(0, nt // DIL_GROUP, tiles, 0)
            for nat, per, padded in _dil_chunks(s, r):
                dq_n[nat, :] += pdq[per, :]
                dk_n[nat, :] += pdk[padded, :]
                dv_n[nat, :] += pdv[padded, :]

        dq_ref[...] = dq_n[...].astype(BF16)
        dk_ref[...] = dk_n[...].astype(BF16)
        dv_ref[...] = dv_n[...].astype(BF16)

    f32_s = pltpu.VMEM((s, HEAD_DIM), F32)
    f32_pad = pltpu.VMEM((s + n, HEAD_DIM), F32)
    bf_s = pltpu.VMEM((s, HEAD_DIM), BF16)
    bf_pad = pltpu.VMEM((s + n, HEAD_DIM), BF16)
    return pl.pallas_call(
        body, name="dil_bwd", grid=(b, n_heads),
        in_specs=[_seg_spec(s, 4), _seg_spec(s, 5), _seg_spec(s, 6), _seg_spec(s, 7),
                  _head_spec(s), _head_spec(s), _seg_spec(s, 1),
                  pl.BlockSpec((1, HEAD_DIM), lambda i, h: (0, h)),
                  pl.BlockSpec((None, 1, HEAD_DIM), lambda i, h: (h, 0, 0)), ANY],
        out_specs=[_seg4_spec(s, 1), pl.BlockSpec((None, 1, HEAD_DIM), lambda i, h: (i, 0, h))],
        out_shape=[jax.ShapeDtypeStruct((8, b, s, w), BF16), jax.ShapeDtypeStruct((b, 1, w), F32)],
        scratch_shapes=[f32_s] * 5 + [bf_s, bf_pad, bf_pad, bf_s] + [f32_s, f32_s, f32_s, f32_pad, f32_pad],
        input_output_aliases={9: 0},
        compiler_params=_cparams(("parallel", "parallel")),
    )(proj8, proj8, proj8, proj8, o_dl, lse_dl, dy2, g_dil, slopes, dproj8)


def _small_update(gathered, n_b, params, m, v):
    n_dev, _, width = gathered.shape

    def body(g_ref, p_ref, m_ref, v_ref, grad_ref, d_ref, nm_ref, nv_ref, loss_ref):
        for row in range(2):
            acc = None
            for dev in range(n_dev):
                for i in range(n_b):
                    term = g_ref[dev, pl.ds(row * n_b + i, 1), :]
                    acc = term if acc is None else acc + term
            grad_ref[pl.ds(row, 1), :] = acc
        loss = g_ref[0, pl.ds(2 * n_b, 1), pl.ds(0, 128)]
        for dev in range(1, n_dev):
            loss = loss + g_ref[dev, pl.ds(2 * n_b, 1), pl.ds(0, 128)]
        loss_ref[...] = loss
        d, nm, nv = _adamw_math(p_ref[...], grad_ref[...], m_ref[...], v_ref[...])
        d_ref[...] = d
        nm_ref[...] = nm
        nv_ref[...] = nv

    sds = jax.ShapeDtypeStruct((2, width), F32)
    return pl.pallas_call(
        body, name="small_update",
        in_specs=[VMEM_SPEC] * 4, out_specs=[VMEM_SPEC] * 5,
        out_shape=[sds, sds, sds, sds, jax.ShapeDtypeStruct((1, 128), F32)],
        compiler_params=_cparams(),
    )(gathered, params, m, v)


def _wada_update(c_t, dmod, w, m, v):
    d, bt = c_t.shape
    _, n = dmod.shape
    tr, tc = _tile(d, 512), _tile(n, 1024)

    def body(c_ref, dm_ref, w_ref, m_ref, v_ref, g_ref, d_ref, nm_ref, nv_ref):
        cv = c_ref[...]
        cs = (cv * _sigmoid(cv)).astype(BF16)
        grad = _dot(cs, dm_ref[...].astype(BF16), NN)
        g_ref[...] = grad
        dl, nm, nv = _adamw_math(w_ref[...], grad, m_ref[...], v_ref[...])
        d_ref[...] = dl
        nm_ref[...] = nm
        nv_ref[...] = nv

    spec = pl.BlockSpec((tr, tc), lambda i, j: (i, j))
    sds = jax.ShapeDtypeStruct((d, n), F32)
    return pl.pallas_call(
        body, name="wada_update", grid=(d // tr, n // tc),
        in_specs=[pl.BlockSpec((tr, bt), lambda i, j: (i, 0)), pl.BlockSpec((bt, tc), lambda i, j: (0, j)),
                  spec, spec, spec],
        out_specs=[spec] * 4, out_shape=[sds] * 4,
        compiler_params=_cparams(("parallel", "parallel")),
    )(c_t, dmod, w, m, v)


def _reduce_begin(gs, tag):
    ra = _sibling_half_swap(gs, "swap_" + tag)
    pa, own = _pair_sum(gs, ra, "pair_sum_" + tag)
    return _ScatterChips(pa), own


def _reduce_finish(rb, own, w, m, v, tag):
    half = _chip_sum(rb, own, "chip_sum_" + tag)
    return _adamw(w, _sibling_join(half, "join_" + tag), m, v, "adamw_" + tag)


def kernel(x, c, w_ada, b_ada, g_norm, w_in, g_sb, g_dil, w_out, g_final, loss_target, m_w_ada, m_b_ada, m_g_norm, m_w_in, m_g_sb, m_g_dil, m_w_out, m_g_final, v_w_ada, v_b_ada, v_g_norm, v_w_in, v_g_sb, v_g_dil, v_w_out, v_g_final):
    nb, s, d = x.shape
    t = nb * s
    na = w_ada.shape[2]
    cs = w_in.shape[2]
    w = cs // 2
    n_heads = w // HEAD_DIM
    r_out = w_out.shape[1]
    assert 2 * nb + 1 <= 8 and 2 * d + 2 * w <= 3 * d and N_CHIPS * na == 3 * d and N_CHIPS * r_out == 2 * w
    xi, yi, ci = _place()
    chip = 2 * xi + yi
    dev = 2 * chip + ci

    c_all = _allgather8(jnp.pad(c, ((0, 8 - nb), (0, 0))), "gather_c")
    c16 = c_all.reshape(N_DEV, 8, d)[:, :nb].reshape(N_DEV * nb, d)
    b_ada_shard = lax.dynamic_slice(b_ada, (0, chip * na), (1, na))
    mod_part = _mod_fwd(c16, w_ada[0], b_ada_shard)
    mod_all = _allgather8(mod_part, "gather_mod")
    mod_full = mod_all.reshape(N_CHIPS, 2, N_DEV * nb, na)[:, 0].transpose(1, 0, 2).reshape(N_DEV * nb, 3 * d)
    mod = lax.dynamic_slice(mod_full, (dev * nb, 0), (nb, 3 * d))
    shift, scale, gate = (mod[:, i * d:(i + 1) * d].reshape(nb, 1, d) for i in range(3))

    h = _norm_mod_fwd(x, g_norm, scale, shift)
    h2 = h.reshape(t, d)
    ws_in = _allgather_chips(_cast_bf16_slab(w_in[0], "cast_w_in"), "gather_w_in")
    proj8, ws_out = _proj_fwd(h2, ws_in, fused=_GatherChips(_cast_bf16_slab(w_out[0], "cast_w_out")))
    proj8 = proj8.reshape(8, nb, s, w)
    w_out_full = ws_out.reshape(2 * w, d)

    slopes = jnp.exp2(-ALIBI_MAX_BIAS * jnp.arange(1, n_heads + 1, dtype=F32) / n_heads)
    slopes = jnp.broadcast_to(slopes[:, None, None], (n_heads, 1, HEAD_DIM))
    o_sb, tot_sb, y2, sb_trips = _sb_fwd(proj8, g_sb)
    o_dl, lse_dl, y2 = _dil_fwd(proj8, g_dil, slopes, y2)
    y2f = y2.reshape(2, t, w)
    out = _out_fwd(y2f, w_out_full)

    dx1, dout, dgate, dg_final, loss_part = _loss_head(
        x, out.reshape(nb, s, d), gate, g_final.reshape(1, d), loss_target)
    dout2 = dout.reshape(t, d)
    gs_out = _out_bwd_w(y2f, dout2).reshape(N_CHIPS, r_out, d)
    scatter_out, own_out = _reduce_begin(gs_out, "w_out")
    dy2 = _out_bwd_y(dout2, w_out_full).reshape(2, nb, s, w)
    dproj8, dg_sb, rb_out = _sb_bwd(proj8, o_sb, tot_sb, sb_trips, dy2, g_sb, fused=scatter_out)
    dproj8, dg_dl = _dil_bwd(proj8, o_dl, lse_dl, dy2, g_dil, slopes, dproj8)
    dproj8 = dproj8.reshape(8, t, w)
    gs_in = _proj_bwd_w(h2, dproj8)
    scatter_in, own_in = _reduce_begin(gs_in, "w_in")
    dh, rb_in = _proj_bwd_x(dproj8, ws_in, fused=scatter_in)
    grad_x, dshift, dscale, dg_norm = _norm_mod_bwd(x, dh.reshape(nb, s, d), dx1, g_norm, scale)

    width = 3 * d
    dmod = jnp.concatenate([dshift, dscale, dgate], axis=-1).reshape(nb, width)
    gains = jnp.concatenate([dg_sb.reshape(nb, w), dg_dl.reshape(nb, w)], axis=-1)
    gains = jnp.pad(gains, ((0, 0), (2 * d, width - 2 * d - 2 * w)))
    first = jnp.pad(jnp.concatenate([dg_norm, dg_final], axis=-1), ((0, nb - 1), (0, width - 2 * d)))
    loss_row = jnp.pad(loss_part, ((0, 0), (0, width - 128)))
    pack = jnp.concatenate([dmod, gains + first, loss_row, jnp.zeros((8 - 2 * nb - 1, width), F32)], axis=0)
    gathered = _allgather8(pack, "gather_small").reshape(N_DEV, 8, width)

    def stack(bias, gn, gf, gsb, gdl):
        row1 = jnp.concatenate([gn.reshape(1, d), gf.reshape(1, d), gsb.reshape(1, w), gdl.reshape(1, w)], axis=-1)
        return jnp.concatenate([bias.reshape(1, width), jnp.pad(row1, ((0, 0), (0, width - 2 * d - 2 * w)))], axis=0)

    small = _small_update(
        gathered, nb, stack(b_ada, g_norm, g_final, g_sb, g_dil),
        stack(m_b_ada, m_g_norm, m_g_final, m_g_sb, m_g_dil), stack(v_b_ada, v_g_norm, v_g_final, v_g_sb, v_g_dil))
    loss = small[4][0, 0]

    def unstack(a):
        return (a[0:1, :], a[1:2, 0:d], a[1, d:2 * d], a[1:2, 2 * d:2 * d + w], a[1:2, 2 * d + w:2 * d + 2 * w])

    (g_b, g_gn, g_gf, g_gsb, g_gdl), (d_b, d_gn, d_gf, d_gsb, d_gdl), (nm_b, nm_gn, nm_gf, nm_gsb, nm_gdl), \
        (nv_b, nv_gn, nv_gf, nv_gsb, nv_gdl) = (unstack(a) for a in small[:4])

    dmod_all = gathered[:, :nb].reshape(N_DEV * nb, width)
    dmod_cols = lax.dynamic_slice(dmod_all, (0, chip * na), (N_DEV * nb, na))
    g_wa, d_wa, nm_wa, nv_wa = _wada_update(c16.T, dmod_cols, w_ada[0], m_w_ada[0], v_w_ada[0])

    g_wi, d_wi, nm_wi, nv_wi = _reduce_finish(rb_in, own_in, w_in[0], m_w_in[0], v_w_in[0], "w_in")
    g_wo, d_wo, nm_wo, nv_wo = _reduce_finish(rb_out, own_out, w_out[0], m_w_out[0], v_w_out[0], "w_out")

    lead = lambda a: a[None]
    return (loss, grad_x,
            lead(g_wa), g_b, g_gn, lead(g_wi), g_gsb, g_gdl, lead(g_wo), g_gf,
            lead(d_wa), d_b, d_gn, lead(d_wi), d_gsb, d_gdl, lead(d_wo), d_gf,
            lead(nm_wa), nm_b, nm_gn, lead(nm_wi), nm_gsb, nm_gdl, lead(nm_wo), nm_gf,
            lead(nv_wa), nv_b, nv_gn, lead(nv_wi), nv_gsb, nv_gdl, lead(nv_wo), nv_gf)
```

```python
import functools
import math

import jax
import jax.numpy as jnp
from jax import lax
from jax.experimental import pallas as pl
from jax.experimental.pallas import tpu as pltpu

F32 = jnp.float32
BF16 = jnp.bfloat16
MESH = pl.DeviceIdType.MESH

HEAD_DIM = 128
EPS = 1e-6
DIL_PAIRS = ((128, 1), (512, 4), (2048, 16))
ALIBI_MAX_BIAS = 8.0
ADAM_LR = 0.001
ADAM_B1 = 0.9
ADAM_B2 = 0.999
ADAM_EPS = 1e-08
ADAM_WD = 0.01
ADAM_STEP = 10
N_CHIPS = 4
N_DEV = 8
VMEM_LIMIT_BYTES = 56 * 1024 * 1024
NEG_BIG = -1e30

NN = (((1,), (0,)), ((), ()))
NT = (((1,), (1,)), ((), ()))
TN = (((0,), (0,)), ((), ()))

ANY = pl.BlockSpec(memory_space=pl.ANY)
VMEM_SPEC = pl.BlockSpec(memory_space=pltpu.VMEM)


def _cparams(sem=None):
    return pltpu.CompilerParams(dimension_semantics=sem, vmem_limit_bytes=VMEM_LIMIT_BYTES)


def _tile(dim, pref):
    t = min(dim, pref)
    assert dim % t == 0, (dim, pref)
    return t


def _dot(a, b, dims):
    return lax.dot_general(a, b, dims, preferred_element_type=F32)


def _sigmoid(x):
    return 1.0 / (1.0 + jnp.exp(-x))


def _place():
    return lax.axis_index("x"), lax.axis_index("y"), lax.axis_index("c")


def _allgather8(x_shard, name):
    m_per, n = x_shard.shape

    def body(x_ref, out_ref, send_sems, recv_sems, local_sem):
        x, y, c = _place()
        me, sibling = (x, y, c), (x, y, 1 - c)
        chips = [(1 - x, y), (x, 1 - y), (1 - x, 1 - y)]

        def rows(px, py, pc):
            return out_ref.at[pl.ds((4 * px + 2 * py + pc) * m_per, m_per), :]

        def copy(k, block, to, src=None):
            return pltpu.make_async_remote_copy(
                src_ref=rows(*block) if src is None else src, dst_ref=rows(*block),
                send_sem=send_sems.at[k], recv_sem=recv_sems.at[k], device_id=to, device_id_type=MESH)

        mine = pltpu.make_async_copy(x_ref, rows(*me), local_sem)
        mine.start()
        first = [copy(0, me, sibling, src=x_ref)]
        first += [copy(1 + j, me, (*chip, c), src=x_ref) for j, chip in enumerate(chips)]
        for cp in first:
            cp.start()
        passed = [copy(4 + j, (*chip, c), sibling) for j, chip in enumerate(chips)]
        for j, chip in enumerate(chips):
            copy(1 + j, (*chip, c), me).wait_recv()
            passed[j].start()
        copy(0, sibling, me).wait_recv()
        for j, chip in enumerate(chips):
            copy(4 + j, (*chip, 1 - c), me).wait_recv()
        for cp in first + passed:
            cp.wait_send()
        mine.wait()

    return pl.pallas_call(
        body, name=name,
        out_shape=jax.ShapeDtypeStruct((N_DEV * m_per, n), x_shard.dtype),
        in_specs=[VMEM_SPEC], out_specs=VMEM_SPEC,
        scratch_shapes=[pltpu.SemaphoreType.DMA((7,)), pltpu.SemaphoreType.DMA((7,)), pltpu.SemaphoreType.DMA],
    )(x_shard)


class _GatherChips:
    def __init__(self, ws):
        self.inputs = [ws]
        self.out_shapes = [jax.ShapeDtypeStruct(ws.shape, ws.dtype)]
        self.aliases = {0: 0}
        self.scratch = [pltpu.SemaphoreType.DMA((12,)), pltpu.SemaphoreType.DMA((12,))]
        self.quarter = ws.shape[1] // 4

    def _copy(self, refs, k, chip, pc, part, to):
        _, out_ref, send_sems, recv_sems = refs
        rows = out_ref.at[2 * chip[0] + chip[1], pl.ds((2 * pc + part) * self.quarter, self.quarter), :]
        return pltpu.make_async_remote_copy(
            src_ref=rows, dst_ref=rows, send_sem=send_sems.at[k], recv_sem=recv_sems.at[k],
            device_id=to, device_id_type=MESH)

    def _sends(self, refs, phase):
        x, y, c = _place()
        sibling, x_nbr, y_nbr, diag = (x, y, 1 - c), (1 - x, y), (x, 1 - y), (1 - x, 1 - y)
        plan = {
            "start": [(0, (x, y), 0, (*x_nbr, c)), (1, (x, y), 1, (*y_nbr, c)),
                      (2, (x, y), 1, (*x_nbr, c)), (3, (x, y), 0, (*y_nbr, c))],
            "middle": [(4, x_nbr, 0, (*y_nbr, c)), (6, x_nbr, 0, sibling), (5, y_nbr, 1, (*x_nbr, c)),
                       (7, y_nbr, 1, sibling), (8, x_nbr, 1, sibling), (9, y_nbr, 0, sibling)],
            "wait": [(10, diag, 0, sibling), (11, diag, 1, sibling)],
        }[phase]
        return [self._copy(refs, k, chip, c, part, to) for k, chip, part, to in plan]

    def _landings(self, refs, phase):
        x, y, c = _place()
        me, x_nbr, y_nbr, diag = (x, y, c), (1 - x, y), (x, 1 - y), (1 - x, 1 - y)
        plan = {
            "middle": [(0, x_nbr, c, 0), (1, y_nbr, c, 1), (2, x_nbr, c, 1), (3, y_nbr, c, 0)],
            "wait": [(4, diag, c, 0), (5, diag, c, 1)],
            "sibling": [(6, x_nbr, 1 - c, 0), (7, y_nbr, 1 - c, 1), (8, x_nbr, 1 - c, 1), (9, y_nbr, 1 - c, 0),
                        (10, diag, 1 - c, 0), (11, diag, 1 - c, 1)],
        }[phase]
        return [self._copy(refs, k, chip, pc, part, me) for k, chip, pc, part in plan]

    def start(self, *refs):
        for cp in self._sends(refs, "start"):
            cp.start()

    def middle(self, *refs):
        landed = self._landings(refs, "middle")
        passed = self._sends(refs, "middle")
        landed[0].wait_recv()
        passed[0].start()
        passed[1].start()
        landed[1].wait_recv()
        passed[2].start()
        passed[3].start()
        landed[2].wait_recv()
        passed[4].start()
        landed[3].wait_recv()
        passed[5].start()

    def wait(self, *refs):
        landed = self._landings(refs, "wait")
        passed = self._sends(refs, "wait")
        for arrival, cp in zip(landed, passed):
            arrival.wait_recv()
            cp.start()
        for arrival in self._landings(refs, "sibling"):
            arrival.wait_recv()
        for phase in ("start", "middle", "wait"):
            for cp in self._sends(refs, phase):
                cp.wait_send()


def _sibling_half_swap(gs, name):
    n, r, cdim = gs.shape
    half = r // 2

    def body(g_ref, out_ref, send_sem, recv_sem):
        x, y, c = _place()
        cp = pltpu.make_async_remote_copy(
            src_ref=g_ref.at[:, pl.ds((1 - c) * half, half), :], dst_ref=out_ref,
            send_sem=send_sem, recv_sem=recv_sem, device_id=(x, y, 1 - c), device_id_type=MESH)
        cp.start()
        cp.wait()

    return pl.pallas_call(
        body, name=name,
        out_shape=jax.ShapeDtypeStruct((n, half, cdim), gs.dtype),
        in_specs=[ANY], out_specs=ANY,
        scratch_shapes=[pltpu.SemaphoreType.DMA, pltpu.SemaphoreType.DMA],
    )(gs)


class _ScatterChips:
    def __init__(self, pa):
        self.inputs = [pa]
        self.out_shapes = [jax.ShapeDtypeStruct(pa.shape, pa.dtype)]
        self.scratch = [pltpu.SemaphoreType.DMA((3,)), pltpu.SemaphoreType.DMA((3,)), pltpu.SemaphoreType.DMA]

    @staticmethod
    def _mine(p_ref, out_ref, send_sems, recv_sems, local_sem):
        x, y, _ = _place()
        return pltpu.make_async_copy(p_ref.at[2 * x + y], out_ref.at[2 * x + y], local_sem)

    @staticmethod
    def _remote(p_ref, out_ref, send_sems, recv_sems, local_sem, incoming):
        x, y, c = _place()
        me = 2 * x + y
        remote = []
        for j, (px, py) in enumerate([(1 - x, y), (x, 1 - y), (1 - x, 1 - y)]):
            remote.append(pltpu.make_async_remote_copy(
                src_ref=p_ref.at[me if incoming else 2 * px + py], dst_ref=out_ref.at[2 * px + py if incoming else me],
                send_sem=send_sems.at[j], recv_sem=recv_sems.at[j], device_id=(px, py, c), device_id_type=MESH))
        return remote

    def start(self, *refs):
        self._mine(*refs).start()
        for cp in self._remote(*refs, incoming=False):
            cp.start()

    def wait(self, *refs):
        for cp in self._remote(*refs, incoming=True):
            cp.wait_recv()
        for cp in self._remote(*refs, incoming=False):
            cp.wait_send()
        self._mine(*refs).wait()


def _fused_specs(fused):
    if fused is None:
        return [], [], [], [], []
    return (list(fused.inputs), [ANY] * len(fused.inputs), list(fused.out_shapes), [ANY] * len(fused.out_shapes),
            list(fused.scratch))


def _fused_aliases(fused, first_input, first_output):
    aliases = getattr(fused, "aliases", {}) if fused is not None else {}
    return {first_input + i: first_output + o for i, o in aliases.items()}


def _fused_begin(fused, grid, refs):
    if fused is not None:
        first = functools.reduce(lambda p, q: p & q, [pl.program_id(i) == 0 for i in range(len(grid))])
        pl.when(first)(lambda: fused.start(*refs))
        if hasattr(fused, "middle"):
            step = functools.reduce(lambda acc, ig: acc * ig[1] + pl.program_id(ig[0]), enumerate(grid), 0)
            pl.when(step == math.prod(grid) // 2)(lambda: fused.middle(*refs))


def _fused_end(fused, grid, refs):
    if fused is not None:
        last = functools.reduce(lambda p, q: p & q, [pl.program_id(i) == g - 1 for i, g in enumerate(grid)])
        pl.when(last)(lambda: fused.wait(*refs))


def _sibling_join(full, name):
    h2, cdim = full.shape
    h = h2 // 2

    def body(in_ref, out_ref, send_sem, recv_sem):
        del in_ref
        x, y, c = _place()
        mine = out_ref.at[pl.ds(c * h, h), :]
        cp = pltpu.make_async_remote_copy(
            src_ref=mine, dst_ref=mine, send_sem=send_sem, recv_sem=recv_sem,
            device_id=(x, y, 1 - c), device_id_type=MESH)
        cp.start()
        theirs = out_ref.at[pl.ds((1 - c) * h, h), :]
        pltpu.make_async_remote_copy(
            src_ref=theirs, dst_ref=theirs, send_sem=send_sem, recv_sem=recv_sem,
            device_id=(x, y, 1 - c), device_id_type=MESH).wait_recv()
        cp.wait_send()

    return pl.pallas_call(
        body, name=name,
        out_shape=jax.ShapeDtypeStruct(full.shape, full.dtype),
        in_specs=[ANY], out_specs=ANY, input_output_aliases={0: 0},
        scratch_shapes=[pltpu.SemaphoreType.DMA, pltpu.SemaphoreType.DMA],
    )(full)


def _cast_bf16_slab(w, name, with_own=False):
    r, cdim = w.shape
    tr, tc = _tile(r, 512), _tile(cdim, 2048)

    def body(pc_ref, w_ref, o_ref, *own_ref):
        o_ref[...] = w_ref[...].astype(BF16)
        for ref in own_ref:
            ref[...] = w_ref[...].astype(BF16)

    plain = pl.BlockSpec((tr, tc), lambda i, j, pc: (i, j))
    return pl.pallas_call(
        body, name=name,
        grid_spec=pltpu.PrefetchScalarGridSpec(
            num_scalar_prefetch=1, grid=(r // tr, cdim // tc),
            in_specs=[plain],
            out_specs=[pl.BlockSpec((None, tr, tc), lambda i, j, pc: (pc[1], i, j))] + [plain] * with_own),
        out_shape=[jax.ShapeDtypeStruct((N_CHIPS, r, cdim), BF16)] + [jax.ShapeDtypeStruct((r, cdim), BF16)] * with_own,
        compiler_params=_cparams(("parallel", "parallel")),
    )(_place_scalars(), w)


def _place_scalars():
    x, y, c = _place()
    return jnp.stack([c, 2 * x + y]).astype(jnp.int32)


def _pair_sum(gs, ra, name):
    n, r, cdim = gs.shape
    half = r // 2
    tr, tc = _tile(half, 512), _tile(cdim, 2048)
    nt = half // tr

    def body(pc_ref, g_ref, r_ref, o_ref, own_ref):
        val = g_ref[...].astype(F32) + r_ref[...].astype(F32)
        o_ref[...] = val.astype(BF16)

        @pl.when(pl.program_id(2) == pc_ref[1])
        def _():
            own_ref[...] = val

    return pl.pallas_call(
        body, name=name,
        grid_spec=pltpu.PrefetchScalarGridSpec(
            num_scalar_prefetch=1, grid=(nt, cdim // tc, n),
            in_specs=[pl.BlockSpec((None, tr, tc), lambda i, j, s, pc: (s, pc[0] * nt + i, j)),
                      pl.BlockSpec((None, tr, tc), lambda i, j, s, pc: (s, i, j))],
            out_specs=[pl.BlockSpec((None, tr, tc), lambda i, j, s, pc: (s, i, j)),
                       pl.BlockSpec((tr, tc), lambda i, j, s, pc: (i, j))]),
        out_shape=[jax.ShapeDtypeStruct((n, half, cdim), BF16), jax.ShapeDtypeStruct((half, cdim), F32)],
        compiler_params=_cparams(("parallel", "parallel", "arbitrary")),
    )(_place_scalars(), gs, ra)


def _chip_sum(rb, own, name):
    n, h, cdim = rb.shape
    tr, tc = _tile(h, 256), _tile(cdim, 2048)
    nt = h // tr

    def body(pc_ref, r_ref, own_ref, o_ref):
        chip = pc_ref[1]
        acc = None
        for p in range(n):
            term = jnp.where(chip == p, own_ref[...], r_ref[p].astype(F32))
            acc = term if acc is None else acc + term
        o_ref[...] = acc

    return pl.pallas_call(
        body, name=name,
        grid_spec=pltpu.PrefetchScalarGridSpec(
            num_scalar_prefetch=1, grid=(nt, cdim // tc),
            in_specs=[pl.BlockSpec((n, tr, tc), lambda i, j, pc: (0, i, j)),
                      pl.BlockSpec((tr, tc), lambda i, j, pc: (i, j))],
            out_specs=pl.BlockSpec((tr, tc), lambda i, j, pc: (pc[0] * nt + i, j))),
        out_shape=jax.ShapeDtypeStruct((2 * h, cdim), F32),
        compiler_params=_cparams(("parallel", "parallel")),
    )(_place_scalars(), rb, own)


def _adamw_math(w, g, m, v):
    m = ADAM_B1 * m + (1.0 - ADAM_B1) * g
    v = ADAM_B2 * v + (1.0 - ADAM_B2) * (g * g)
    m_hat = m / (1.0 - ADAM_B1 ** ADAM_STEP)
    v_hat = v / (1.0 - ADAM_B2 ** ADAM_STEP)
    delta = -ADAM_LR * (m_hat / (jnp.sqrt(v_hat) + ADAM_EPS) + ADAM_WD * w)
    return delta, m, v


def _adamw(w, g, m, v, name):
    r, cdim = w.shape
    tr, tc = _tile(r, 256), _tile(cdim, 2048)

    def body(w_ref, g_ref, m_ref, v_ref, go_ref, d_ref, nm_ref, nv_ref):
        gv = g_ref[...]
        d, nm, nv = _adamw_math(w_ref[...], gv, m_ref[...], v_ref[...])
        go_ref[...] = gv
        d_ref[...] = d
        nm_ref[...] = nm
        nv_ref[...] = nv

    spec = pl.BlockSpec((tr, tc), lambda i, j: (i, j))
    sds = jax.ShapeDtypeStruct((r, cdim), F32)
    return pl.pallas_call(
        body, name=name, grid=(r // tr, cdim // tc),
        in_specs=[spec] * 4, out_specs=[spec] * 4, out_shape=[sds] * 4,
        compiler_params=_cparams(("parallel", "parallel")),
    )(w, g, m, v)


def _matmul(a, b, *, grid, a_spec, b_spec, out_spec, out_shape, acc_shape, dims, name, bias=None, bias_spec=None,
            silu_a=False, fused=None):
    nk = grid[2]
    f_in, f_in_specs, f_out, f_out_specs, f_scratch = _fused_specs(fused)
    n_in = 2 + (bias is not None)

    acc_scratch = [pltpu.VMEM(acc_shape, F32)] if nk > 1 else []

    def body(*refs):
        a_ref, b_ref = refs[:2]
        bias_ref = refs[2] if bias is not None else None
        o_ref = refs[n_in + len(f_in)]
        n_fixed = n_in + len(f_in) + 1 + len(f_out)
        f_refs = (*refs[n_in:n_in + len(f_in)], *refs[n_in + len(f_in) + 1:n_fixed],
                  *refs[n_fixed + len(acc_scratch):])
        _fused_begin(fused, grid, f_refs)

        def product():
            if len(a_ref.shape) == 3:
                tks = a_ref.shape[2]
                parts = [_dot(a_ref[i], b_ref[:, i * tks:(i + 1) * tks], dims) for i in range(a_ref.shape[0])]
                return functools.reduce(lambda p, q: p + q, parts)
            av = a_ref[...]
            if silu_a:
                av = av * _sigmoid(av)
            return _dot(av.astype(BF16), b_ref[...].astype(BF16), dims)

        def finish(res):
            if bias is not None:
                res = res + bias_ref[...]
            o_ref[...] = res.astype(o_ref.dtype)

        if nk == 1:
            finish(product())
        else:
            acc_ref = refs[n_fixed]
            k = pl.program_id(2)

            @pl.when(k == 0)
            def _():
                acc_ref[...] = product()

            if nk > 2:
                @pl.when((k > 0) & (k < nk - 1))
                def _():
                    acc_ref[...] += product()

            @pl.when(k == nk - 1)
            def _():
                finish(acc_ref[...] + product())

        _fused_end(fused, grid, f_refs)

    in_specs = [a_spec, b_spec] + ([] if bias is None else [bias_spec]) + f_in_specs
    args = (a, b) + (() if bias is None else (bias,)) + tuple(f_in)
    sem = ("parallel", "parallel", "arbitrary") if fused is None else ("arbitrary",) * 3
    res = pl.pallas_call(
        body, name=name, grid=grid, in_specs=in_specs, out_specs=[out_spec] + f_out_specs,
        out_shape=[out_shape] + f_out,
        scratch_shapes=acc_scratch + f_scratch,
        input_output_aliases=_fused_aliases(fused, n_in, 1),
        compiler_params=_cparams(sem),
    )(*args)
    return res[0] if fused is None else tuple(res)


def _mm_tiles(m, n, k):
    return _tile(m, 1024), _tile(n, 1024), _tile(k, 4096)


def _proj_part(h2, wmat, n_seg, w_block, seg_of, w, name, carry=None, fused=None):
    t, d = h2.shape
    tm, tn, _ = _mm_tiles(t, w, d)
    npseg = w // tn
    grid = (t // tm, n_seg * npseg)
    f_in, f_in_specs, f_out, f_out_specs, f_scratch = _fused_specs(fused)
    n_carry = carry is not None

    def body(place_ref, a_ref, b_ref, *refs):
        del place_ref
        o_ref = refs[n_carry + len(f_in)]
        f_refs = (*refs[n_carry:n_carry + len(f_in)], *refs[n_carry + len(f_in) + 1:])
        _fused_begin(fused, grid, f_refs)
        o_ref[...] = _dot(a_ref[...], b_ref[...], NN)
        _fused_end(fused, grid, f_refs)

    w_spec = pl.BlockSpec((d, tn) if wmat.ndim == 2 else (None, d, tn),
                          lambda m, n, place: w_block(n // npseg, n % npseg, place))
    res = pl.pallas_call(
        body, name=name,
        grid_spec=pltpu.PrefetchScalarGridSpec(
            num_scalar_prefetch=1, grid=grid,
            in_specs=[pl.BlockSpec((tm, d), lambda m, n, place: (m, 0)), w_spec] + [ANY] * n_carry + f_in_specs,
            out_specs=[pl.BlockSpec((None, tm, tn), lambda m, n, place: (seg_of(n // npseg, place), m, n % npseg))]
            + f_out_specs,
            scratch_shapes=f_scratch),
        out_shape=[jax.ShapeDtypeStruct((8, t, w), F32)] + f_out,
        input_output_aliases={**({3: 0} if n_carry else {}), **_fused_aliases(fused, 3 + n_carry, 1)},
        compiler_params=_cparams(("arbitrary", "arbitrary")),
    )(_place_scalars(), h2, wmat, *([carry] if n_carry else []), *f_in)
    return tuple(res)


def _proj_fwd_split(h2, w_own, gather_in, gather_out):
    w = w_own.shape[1] // 2
    npseg = w // _mm_tiles(h2.shape[0], w, h2.shape[1])[1]
    proj8, ws_in = _proj_part(
        h2, w_own, 2, lambda j, i, place: (0, j * npseg + i), lambda j, place: 2 * place[1] + j, w,
        "proj_fwd_own", fused=gather_in)

    def shard(j, place):
        return (place[1] + 1 + j // 2) % N_CHIPS

    proj8, ws_out = _proj_part(
        h2, ws_in, 6, lambda j, i, place: (shard(j, place), 0, (j % 2) * npseg + i),
        lambda j, place: 2 * shard(j, place) + j % 2, w, "proj_fwd_rest", carry=proj8, fused=gather_out)
    return proj8, ws_in, ws_out


def _proj_bwd_x(dproj8, ws_in, fused=None):
    _, t, w = dproj8.shape
    _, d, cs = ws_in.shape
    tm, tn, _ = _mm_tiles(t, d, w)
    return _matmul(
        dproj8, ws_in, grid=(t // tm, d // tn, N_CHIPS), dims=NT, name="proj_bwd_x", fused=fused,
        a_spec=pl.BlockSpec((2, tm, w), lambda m, n, k: (k, m, 0)),
        b_spec=pl.BlockSpec((None, tn, cs), lambda m, n, k: (k, n, 0)),
        out_spec=pl.BlockSpec((tm, tn), lambda m, n, k: (m, n)),
        out_shape=jax.ShapeDtypeStruct((t, d), F32), acc_shape=(tm, tn))


def _proj_bwd_w(h2, dproj8):
    t, d = h2.shape
    _, _, w = dproj8.shape
    cs = 2 * w
    tm, tn, tk = _mm_tiles(d, w, t)
    nps, npseg = cs // tn, w // tn
    return _matmul(
        h2, dproj8, grid=(d // tm, 8 * npseg, t // tk), dims=TN, name="proj_bwd_w",
        a_spec=pl.BlockSpec((tk, tm), lambda m, n, k: (k, m)),
        b_spec=pl.BlockSpec((None, tk, tn), lambda m, n, k: (n // npseg, k, n % npseg)),
        out_spec=pl.BlockSpec((None, tm, tn), lambda m, n, k: (n // nps, m, n % nps)),
        out_shape=jax.ShapeDtypeStruct((N_CHIPS, d, cs), BF16), acc_shape=(tm, tn))


def _out_fwd(y2, w_out):
    _, t, w = y2.shape
    _, d = w_out.shape
    tm, tn, tk = _mm_tiles(t, d, w)
    kpg = w // tk
    return _matmul(
        y2, w_out, grid=(t // tm, d // tn, 2 * kpg), dims=NN, name="out_fwd",
        a_spec=pl.BlockSpec((None, tm, tk), lambda m, n, k: (k // kpg, m, k % kpg)),
        b_spec=pl.BlockSpec((tk, tn), lambda m, n, k: (k, n)),
        out_spec=pl.BlockSpec((tm, tn), lambda m, n, k: (m, n)),
        out_shape=jax.ShapeDtypeStruct((t, d), F32), acc_shape=(tm, tn))


def _out_bwd_y(dout, w_out):
    t, d = dout.shape
    w = w_out.shape[0] // 2
    tm, tn, tk = _mm_tiles(t, w, d)
    npg = w // tn
    return _matmul(
        dout, w_out, grid=(t // tm, 2 * npg, d // tk), dims=NT, name="out_bwd_y",
        a_spec=pl.BlockSpec((tm, tk), lambda m, n, k: (m, k)),
        b_spec=pl.BlockSpec((tn, tk), lambda m, n, k: (n, k)),
        out_spec=pl.BlockSpec((None, tm, tn), lambda m, n, k: (n // npg, m, n % npg)),
        out_shape=jax.ShapeDtypeStruct((2, t, w), F32), acc_shape=(tm, tn))


def _out_bwd_w(y2, dout):
    _, t, w = y2.shape
    _, d = dout.shape
    tm, tn, tk = _mm_tiles(w, d, t)
    mpg = w // tm
    return _matmul(
        y2, dout, grid=(2 * mpg, d // tn, t // tk), dims=TN, name="out_bwd_w",
        a_spec=pl.BlockSpec((None, tk, tm), lambda m, n, k: (m // mpg, k, m % mpg)),
        b_spec=pl.BlockSpec((tk, tn), lambda m, n, k: (k, n)),
        out_spec=pl.BlockSpec((tm, tn), lambda m, n, k: (m, n)),
        out_shape=jax.ShapeDtypeStruct((2 * w, d), BF16), acc_shape=(tm, tn))


def _mod_fwd(c_all, w_ada, b_ada):
    bt, d = c_all.shape
    _, n = w_ada.shape
    tn, tk = _tile(n, 512), _tile(d, 1024)
    return _matmul(
        c_all, w_ada, grid=(1, n // tn, d // tk), dims=NN, name="mod_fwd", silu_a=True,
        a_spec=pl.BlockSpec((bt, tk), lambda i, j, l: (0, l)),
        b_spec=pl.BlockSpec((tk, tn), lambda i, j, l: (l, j)),
        bias=b_ada, bias_spec=pl.BlockSpec((1, tn), lambda i, j, l: (0, j)),
        out_spec=pl.BlockSpec((bt, tn), lambda i, j, l: (0, j)),
        out_shape=jax.ShapeDtypeStruct((bt, n), F32), acc_shape=(bt, tn))


def _norm_mod_fwd(x, g_norm, scale, shift):
    b, s, d = x.shape
    ts = _tile(s, 256)

    def body(x_ref, g_ref, sc_ref, sh_ref, h_ref):
        xv = x_ref[...]
        r = lax.rsqrt(jnp.mean(xv * xv, axis=-1, keepdims=True) + EPS)
        y = (xv * r) * g_ref[...]
        h_ref[...] = (y * (1.0 + sc_ref[...]) + sh_ref[...]).astype(BF16)

    row = pl.BlockSpec((None, ts, d), lambda i, j: (i, j, 0))
    per_b = pl.BlockSpec((None, 1, d), lambda i, j: (i, 0, 0))
    return pl.pallas_call(
        body, name="norm_mod_fwd", grid=(b, s // ts),
        in_specs=[row, pl.BlockSpec((1, d), lambda i, j: (0, 0)), per_b, per_b],
        out_specs=row, out_shape=jax.ShapeDtypeStruct((b, s, d), BF16),
        compiler_params=_cparams(("parallel", "parallel")),
    )(x, g_norm, scale, shift)


def _norm_mod_bwd(x, dh, dx1, g_norm, scale):
    b, s, d = x.shape
    ts = _tile(s, 256)

    def body(x_ref, dh_ref, dx1_ref, g_ref, sc_ref, gx_ref, dsh_ref, dsc_ref, dg_ref):
        i, j = pl.program_id(0), pl.program_id(1)

        @pl.when(j == 0)
        def _():
            dsh_ref[...] = jnp.zeros_like(dsh_ref)
            dsc_ref[...] = jnp.zeros_like(dsc_ref)

        @pl.when((i == 0) & (j == 0))
        def _():
            dg_ref[...] = jnp.zeros_like(dg_ref)

        xv, dhv, g = x_ref[...], dh_ref[...], g_ref[...]
        r = lax.rsqrt(jnp.mean(xv * xv, axis=-1, keepdims=True) + EPS)
        xh = xv * r
        dsh_ref[...] += jnp.sum(dhv, axis=0, keepdims=True)
        dsc_ref[...] += jnp.sum(dhv * (xh * g), axis=0, keepdims=True)
        dn = dhv * (1.0 + sc_ref[...])
        dg_ref[...] += jnp.sum(dn * xh, axis=0, keepdims=True)
        u = dn * g
        dx = r * u - xv * (r * r * r) * jnp.mean(u * xv, axis=-1, keepdims=True)
        gx_ref[...] = dx1_ref[...] + dx

    row = pl.BlockSpec((None, ts, d), lambda i, j: (i, j, 0))
    per_b = pl.BlockSpec((None, 1, d), lambda i, j: (i, 0, 0))
    vec = pl.BlockSpec((1, d), lambda i, j: (0, 0))
    return pl.pallas_call(
        body, name="norm_mod_bwd", grid=(b, s // ts),
        in_specs=[row, row, row, vec, per_b],
        out_specs=[row, per_b, per_b, vec],
        out_shape=[jax.ShapeDtypeStruct((b, s, d), F32), jax.ShapeDtypeStruct((b, 1, d), F32),
                   jax.ShapeDtypeStruct((b, 1, d), F32), jax.ShapeDtypeStruct((1, d), F32)],
        compiler_params=_cparams(("arbitrary", "arbitrary")),
    )(x, dh, dx1, g_norm, scale)


def _loss_head(x, out, gate, g_final, target):
    b, s, d = x.shape
    ts = _tile(s, 256)

    def body(x_ref, o_ref, gt_ref, g_ref, t_ref, dx1_ref, dout_ref, dgt_ref, dg_ref, loss_ref):
        i, j = pl.program_id(0), pl.program_id(1)

        @pl.when(j == 0)
        def _():
            dgt_ref[...] = jnp.zeros_like(dgt_ref)

        @pl.when((i == 0) & (j == 0))
        def _():
            dg_ref[...] = jnp.zeros_like(dg_ref)
            loss_ref[...] = jnp.zeros_like(loss_ref)

        ov, gt, g = o_ref[...], gt_ref[...], g_ref[...]
        x1 = x_ref[...] + gt * ov
        r = lax.rsqrt(jnp.mean(x1 * x1, axis=-1, keepdims=True) + EPS)
        xh = x1 * r
        err = xh * g - t_ref[...]
        loss_ref[...] += 0.5 * jnp.sum(jnp.mean(err * err, axis=-1, keepdims=True))
        dfin = err * (1.0 / d)
        dg_ref[...] += jnp.sum(dfin * xh, axis=0, keepdims=True)
        u = dfin * g
        dx1 = r * u - x1 * (r * r * r) * jnp.mean(u * x1, axis=-1, keepdims=True)
        dx1_ref[...] = dx1
        dgt_ref[...] += jnp.sum(dx1 * ov, axis=0, keepdims=True)
        dout_ref[...] = (gt * dx1).astype(BF16)

    row = pl.BlockSpec((None, ts, d), lambda i, j: (i, j, 0))
    per_b = pl.BlockSpec((None, 1, d), lambda i, j: (i, 0, 0))
    vec = pl.BlockSpec((1, d), lambda i, j: (0, 0))
    return pl.pallas_call(
        body, name="loss_head", grid=(b, s // ts),
        in_specs=[row, row, per_b, vec, row],
        out_specs=[row, row, per_b, vec, pl.BlockSpec((1, 128), lambda i, j: (0, 0))],
        out_shape=[jax.ShapeDtypeStruct((b, s, d), F32), jax.ShapeDtypeStruct((b, s, d), BF16),
                   jax.ShapeDtypeStruct((b, 1, d), F32), jax.ShapeDtypeStruct((1, d), F32),
                   jax.ShapeDtypeStruct((1, 128), F32)],
        compiler_params=_cparams(("arbitrary", "arbitrary")),
    )(x, out, gate, g_final, target)


def _head_out(o, zg, g):
    rinv = lax.rsqrt(jnp.mean(o * o, axis=-1, keepdims=True) + EPS)
    return ((o * rinv) * g) * (zg * _sigmoid(zg))


def _head_out_bwd(o, zg, g, dy):
    rinv = lax.rsqrt(jnp.mean(o * o, axis=-1, keepdims=True) + EPS)
    rn = o * rinv
    sg = _sigmoid(zg)
    sil = zg * sg
    dzg = dy * (rn * g) * (sg * (1.0 + zg * (1.0 - sg)))
    dg = jnp.sum(dy * rn * sil, axis=0, keepdims=True)
    drn = dy * g * sil
    do = rinv * drn - o * (rinv * rinv * rinv) * jnp.mean(drn * o, axis=-1, keepdims=True)
    return do, dzg, dg


def _head_spec(s):
    return pl.BlockSpec((None, s, HEAD_DIM), lambda b, h: (b, 0, h))


def _seg_spec(s, seg):
    return pl.BlockSpec((None, None, s, HEAD_DIM), lambda b, h: (seg, b, 0, h))


def _seg4_spec(s, group):
    return pl.BlockSpec((4, None, s, HEAD_DIM), lambda b, h: (group, b, 0, h))


SB_Q_BLOCK = 512
SB_K_BLOCK = 256


SB_DEAD_LOG2 = -160.0
LOG2_E = 1.4426950408889634
SB_LOGIT_SCALE = LOG2_E / math.sqrt(HEAD_DIM)


def _sb_terms(raw, valid):
    t = jnp.where(valid, raw * SB_LOGIT_SCALE, NEG_BIG)
    e = jnp.exp2(-jnp.abs(t))
    l1m = -(jnp.maximum(t, 0.0) + jnp.log2(1.0 + e))
    return t, l1m, e


def _split_dot(a, u):
    hi = a.astype(BF16)
    lo = (a - hi.astype(F32)).astype(BF16)
    return _dot(hi, u, NN) + _dot(lo, u, NN)


def _sb_fwd(proj8, g_sb):
    _, b, s, w = proj8.shape
    n_heads = w // HEAD_DIM
    tq, tk = _tile(s, SB_Q_BLOCK), _tile(s, SB_K_BLOCK)
    nq, kpq = s // tq, tq // tk
    scale = 1.0 / math.sqrt(HEAD_DIM)

    def body(q_ref, k_ref, v_ref, zg_ref, g_ref, o_ref, tot_ref, y_ref, trips_ref):
        u_excl = (lax.broadcasted_iota(jnp.int32, (tk, tk), 0)
                  > lax.broadcasted_iota(jnp.int32, (tk, tk), 1)).astype(BF16)
        ahead = lax.broadcasted_iota(jnp.int32, (tq, tk), 0) - lax.broadcasted_iota(jnp.int32, (tq, tk), 1)
        g = g_ref[...]

        def qblock(i, _):
            rows = pl.ds(pl.multiple_of(i * tq, tq), tq)
            q = q_ref[rows, :].astype(BF16)
            nk = (i + 1) * kpq

            def alive(state):
                jj, _, csum = state
                return (jj <= i) & ((jj == 0) | (jnp.max(csum) > SB_DEAD_LOG2))

            def kblocks(state):
                jj, acc, csum = state
                js = [nk - 1 - (jj * kpq + n) for n in range(kpq)]
                cols = [pl.ds(pl.multiple_of(j * tk, tk), tk) for j in js]
                raw = [_dot(q, k_ref[c, :].astype(BF16), NT) for c in cols]
                terms = [_sb_terms(x, ahead > j * tk - i * tq) for x, j in zip(raw, js)]
                sums = [_split_dot(l1m, u_excl) for _, l1m, _ in terms]
                for (t, l1m, _), part, c in zip(terms, sums, cols):
                    a = jnp.exp2((t + l1m) + (part + csum))
                    acc = acc + _dot(a.astype(BF16), v_ref[c, :].astype(BF16), NN)
                    csum = csum + jnp.sum(l1m, axis=1, keepdims=True)
                return jj + 1, acc, csum

            trips, acc, tot = lax.while_loop(
                alive, kblocks, (jnp.int32(0), jnp.zeros((tq, HEAD_DIM), F32), jnp.zeros((tq, 1), F32)))
            o_ref[rows, :] = acc
            tot_ref[rows, :] = jnp.broadcast_to(tot, (tq, HEAD_DIM))
            y_ref[rows, :] = _head_out(acc, zg_ref[rows, :], g).astype(BF16)
            trips_ref[0, i] = trips.astype(F32)
            return 0

        lax.fori_loop(0, nq, qblock, 0)

    return pl.pallas_call(
        body, name="sb_fwd", grid=(b, n_heads),
        in_specs=[_seg_spec(s, 0), _seg_spec(s, 1), _seg_spec(s, 2), _seg_spec(s, 3),
                  pl.BlockSpec((1, HEAD_DIM), lambda i, h: (0, h))],
        out_specs=[_head_spec(s), _head_spec(s), _seg_spec(s, 0),
                   pl.BlockSpec((None, None, 1, nq), lambda i, h: (i, h, 0, 0), memory_space=pltpu.SMEM)],
        out_shape=[jax.ShapeDtypeStruct((b, s, w), F32), jax.ShapeDtypeStruct((b, s, w), F32),
                   jax.ShapeDtypeStruct((2, b, s, w), BF16), jax.ShapeDtypeStruct((b, n_heads, 1, nq), F32)],
        compiler_params=_cparams(("parallel", "parallel")),
    )(proj8, proj8, proj8, proj8, g_sb)


def _sb_bwd(proj8, o_sb, tot_sb, trips, dy2, g_sb, fused=None):
    _, b, s, w = proj8.shape
    n_heads = w // HEAD_DIM
    tq, tk = _tile(s, SB_Q_BLOCK), _tile(s, SB_K_BLOCK)
    nq, kpq = s // tq, tq // tk
    scale = 1.0 / math.sqrt(HEAD_DIM)

    f_in, f_in_specs, f_out, f_out_specs, f_scratch = _fused_specs(fused)
    grid = (b, n_heads)

    def body(*refs):
        q_ref, k_ref, v_ref, zg_ref, o_ref, tot_ref, dy_ref, g_ref, trips_ref = refs[:9]
        dp_ref, dg_ref = refs[9 + len(f_in):11 + len(f_in)]
        do_s, dk_s, dv_s = refs[11 + len(f_in) + len(f_out):14 + len(f_in) + len(f_out)]
        f_refs = (*refs[9:9 + len(f_in)], *refs[11 + len(f_in):11 + len(f_in) + len(f_out)],
                  *refs[14 + len(f_in) + len(f_out):])
        _fused_begin(fused, grid, f_refs)
        dq_ref, dk_ref, dv_ref, dzg_ref = (dp_ref.at[n] for n in range(4))
        ri = lax.broadcasted_iota(jnp.int32, (tk, tk), 0)
        ci = lax.broadcasted_iota(jnp.int32, (tk, tk), 1)
        u_le = (ri <= ci).astype(BF16)
        u_lt = (ri < ci).astype(BF16)
        ahead = lax.broadcasted_iota(jnp.int32, (tq, tk), 0) - lax.broadcasted_iota(jnp.int32, (tq, tk), 1)
        g = g_ref[...]

        def prologue(i, dg):
            rows = pl.ds(pl.multiple_of(i * tq, tq), tq)
            do, dzg, dgi = _head_out_bwd(o_ref[rows, :], zg_ref[rows, :], g, dy_ref[rows, :])
            dzg_ref[rows, :] = dzg.astype(BF16)
            do_s[rows, :] = do.astype(BF16)
            return dg + dgi

        dg_ref[...] = lax.fori_loop(0, nq, prologue, jnp.zeros((1, HEAD_DIM), F32))
        dk_s[...] = jnp.zeros_like(dk_s)
        dv_s[...] = jnp.zeros_like(dv_s)

        def qblock(i, _):
            rows = pl.ds(pl.multiple_of(i * tq, tq), tq)
            q = q_ref[rows, :].astype(BF16)
            do = do_s[rows, :]
            tot = tot_ref[rows, :][:, :1]

            def kblocks(jj, carry):
                dq, pre_l, pre_g = carry
                js = [jj * kpq + n for n in range(kpq)]
                cols = [pl.ds(pl.multiple_of(j * tk, tk), tk) for j in js]
                ks = [k_ref[c, :].astype(BF16) for c in cols]
                raw = [_dot(q, k, NT) for k in ks]
                da = [_dot(do, v_ref[c, :].astype(BF16), NT) for c in cols]
                terms = [_sb_terms(x, ahead > j * tk - i * tq) for x, j in zip(raw, js)]
                sums_l = [_split_dot(l1m, u_le) for _, l1m, _ in terms]
                a, gg = [], []
                for (t, l1m, _), part, d in zip(terms, sums_l, da):
                    a.append(jnp.exp2((t + l1m) + (tot - (part + pre_l))))
                    gg.append(a[-1] * d)
                    pre_l = pre_l + jnp.sum(l1m, axis=1, keepdims=True)
                sums_g = [_split_dot(x, u_lt) for x in gg]
                dzs = []
                for (t, _, e), x, part in zip(terms, gg, sums_g):
                    big_g = part + pre_g
                    pre_g = pre_g + jnp.sum(x, axis=1, keepdims=True)
                    inv = 1.0 / (1.0 + e)
                    sig = jnp.where(t >= 0.0, inv, e * inv)
                    dzs.append(((x - sig * (x + big_g)) * scale).astype(BF16))
                for x, k in zip(dzs, ks):
                    dq = dq + _dot(x, k, NN)
                for x, y, c in zip(dzs, a, cols):
                    dk_s[c, :] += _dot(x, q, TN)
                    dv_s[c, :] += _dot(y.astype(BF16), do, TN)
                return dq, pre_l, pre_g

            zero = jnp.zeros((tq, 1), F32)
            walked = jnp.clip(trips_ref[0, i].astype(jnp.int32), 1, i + 1)
            dq, _, _ = lax.fori_loop(i + 1 - walked, i + 1, kblocks, (jnp.zeros((tq, HEAD_DIM), F32), zero, zero))
            dq_ref[rows, :] = dq.astype(BF16)
            return 0

        lax.fori_loop(0, nq, qblock, 0)
        dk_ref[...] = dk_s[...].astype(BF16)
        dv_ref[...] = dv_s[...].astype(BF16)
        _fused_end(fused, grid, f_refs)

    return pl.pallas_call(
        body, name="sb_bwd", grid=grid,
        in_specs=[_seg_spec(s, 0), _seg_spec(s, 1), _seg_spec(s, 2), _seg_spec(s, 3),
                  _head_spec(s), _head_spec(s), _seg_spec(s, 0),
                  pl.BlockSpec((1, HEAD_DIM), lambda i, h: (0, h)),
                  pl.BlockSpec((None, None, 1, nq), lambda i, h: (i, h, 0, 0), memory_space=pltpu.SMEM)] + f_in_specs,
        out_specs=[_seg4_spec(s, 0), pl.BlockSpec((None, 1, HEAD_DIM), lambda i, h: (i, 0, h))] + f_out_specs,
        out_shape=[jax.ShapeDtypeStruct((8, b, s, w), BF16), jax.ShapeDtypeStruct((b, 1, w), F32)] + f_out,
        scratch_shapes=[pltpu.VMEM((s, HEAD_DIM), BF16), pltpu.VMEM((s, HEAD_DIM), F32),
                        pltpu.VMEM((s, HEAD_DIM), F32)] + f_scratch,
        compiler_params=_cparams(("arbitrary", "arbitrary")),
    )(proj8, proj8, proj8, proj8, o_sb, tot_sb, dy2, g_sb, trips, *f_in)


DIL_BLOCK = 128
DIL_GROUP = 4


def _dil_chunks(s, r):
    length = s // r
    out = []
    for rho in range(r):
        for cc in range(length // DIL_BLOCK):
            if r == 1:
                nat = pl.ds(cc * DIL_BLOCK, DIL_BLOCK)
            else:
                nat = pl.ds(rho + r * DIL_BLOCK * cc, DIL_BLOCK, stride=r)
            off = rho * length + cc * DIL_BLOCK
            out.append((nat, pl.ds(off, DIL_BLOCK), pl.ds(off + DIL_BLOCK, DIL_BLOCK)))
    return out


def _dil_masks(slope, r):
    n = DIL_BLOCK
    ri = lax.broadcasted_iota(jnp.int32, (n, 2 * n), 0)
    ci = lax.broadcasted_iota(jnp.int32, (n, 2 * n), 1)
    steps = ri - ci + n
    inside = (steps >= 0) & (steps <= n)
    bias = slope * (steps.astype(F32) * r)
    return jnp.where(inside, -bias, NEG_BIG), jnp.where(inside & (ci >= n), -bias, NEG_BIG)


def _dil_scores(q, k_pc, masks, first):
    return _dot(q, k_pc, NT) * (1.0 / math.sqrt(HEAD_DIM)) + jnp.where(first, masks[1], masks[0])


def _dil_check(s):
    assert (s // DIL_BLOCK) % DIL_GROUP == 0, s
    for window, r in DIL_PAIRS:
        assert window // r == DIL_BLOCK and s % (r * DIL_BLOCK) == 0, (s, window, r)


def _dil_fwd(proj8, g_dil, slopes, y2):
    _, b, s, w = proj8.shape
    n_heads = w // HEAD_DIM
    _dil_check(s)
    n = DIL_BLOCK
    nt = s // n

    def body(q_ref, k_ref, v_ref, zg_ref, g_ref, sl_ref, y_in, o_ref, lse_ref, y_ref,
             qp, kp, vp, pnum, pm, pl_, acc_s, m_s, l_s):
        del y_in
        slope = sl_ref[...][:, :1]
        kp[pl.ds(0, n), :] = jnp.zeros((n, HEAD_DIM), BF16)
        vp[pl.ds(0, n), :] = jnp.zeros((n, HEAD_DIM), BF16)

        for (window, r) in DIL_PAIRS:
            nb = (s // r) // n
            masks = _dil_masks(slope, float(r))
            for nat, per, padded in _dil_chunks(s, r):
                qp[per, :] = q_ref[nat, :].astype(BF16)
                kp[padded, :] = k_ref[nat, :].astype(BF16)
                vp[padded, :] = v_ref[nat, :].astype(BF16)
            num_t, m_t, l_t = (acc_s, m_s, l_s) if r == 1 else (pnum, pm, pl_)

            def tiles(tt, _):
                ts = [tt * DIL_GROUP + i for i in range(DIL_GROUP)]
                rows = [pl.ds(pl.multiple_of(t * n, n), n) for t in ts]
                both = [pl.ds(pl.multiple_of(t * n, n), 2 * n) for t in ts]
                sc = [_dil_scores(qp[rw, :], kp[bt, :], masks, lax.rem(t, nb) == 0)
                      for t, rw, bt in zip(ts, rows, both)]
                m = [jnp.max(x, axis=1, keepdims=True) for x in sc]
                p = [jnp.exp(x - mx) for x, mx in zip(sc, m)]
                num = [_dot(x.astype(BF16), vp[bt, :], NN) for x, bt in zip(p, both)]
                for rw, x, mx, nm in zip(rows, p, m, num):
                    num_t[rw, :] = nm
                    m_t[rw, :] = jnp.broadcast_to(mx, (n, HEAD_DIM))
                    l_t[rw, :] = jnp.broadcast_to(jnp.sum(x, axis=1, keepdims=True), (n, HEAD_DIM))
                return 0

            lax.fori_loop(0, nt // DIL_GROUP, tiles, 0)
            if r != 1:
                for nat, per, _ in _dil_chunks(s, r):
                    m_old, m_new_p = m_s[nat, :], pm[per, :]
                    m_new = jnp.maximum(m_old, m_new_p)
                    a_old, a_p = jnp.exp(m_old - m_new), jnp.exp(m_new_p - m_new)
                    m_s[nat, :] = m_new
                    l_s[nat, :] = l_s[nat, :] * a_old + pl_[per, :] * a_p
                    acc_s[nat, :] = acc_s[nat, :] * a_old + pnum[per, :] * a_p

        g = g_ref[...]

        def finish(t, _):
            rows = pl.ds(pl.multiple_of(t * n, n), n)
            l = l_s[rows, :]
            o = acc_s[rows, :] / l
            o_ref[rows, :] = o
            lse_ref[rows, :] = m_s[rows, :] + jnp.log(l)
            y_ref[rows, :] = _head_out(o, zg_ref[rows, :], g).astype(BF16)
            return 0

        lax.fori_loop(0, nt, finish, 0)

    f32_s = pltpu.VMEM((s, HEAD_DIM), F32)
    bf_s = pltpu.VMEM((s, HEAD_DIM), BF16)
    bf_pad = pltpu.VMEM((s + n, HEAD_DIM), BF16)
    return pl.pallas_call(
        body, name="dil_fwd", grid=(b, n_heads),
        in_specs=[_seg_spec(s, 4), _seg_spec(s, 5), _seg_spec(s, 6), _seg_spec(s, 7),
                  pl.BlockSpec((1, HEAD_DIM), lambda i, h: (0, h)),
                  pl.BlockSpec((None, 1, HEAD_DIM), lambda i, h: (h, 0, 0)), ANY],
        out_specs=[_head_spec(s), _head_spec(s), _seg_spec(s, 1)],
        out_shape=[jax.ShapeDtypeStruct((b, s, w), F32), jax.ShapeDtypeStruct((b, s, w), F32),
                   jax.ShapeDtypeStruct((2, b, s, w), BF16)],
        scratch_shapes=[bf_s, bf_pad, bf_pad, f32_s, f32_s, f32_s, f32_s, f32_s, f32_s],
        input_output_aliases={6: 2},
        compiler_params=_cparams(("parallel", "parallel")),
    )(proj8, proj8, proj8, proj8, g_dil, slopes, y2)


def _dil_bwd(proj8, o_dl, lse_dl, dy2, g_dil, slopes, dproj8):
    _, b, s, w = proj8.shape
    n_heads = w // HEAD_DIM
    _dil_check(s)
    n = DIL_BLOCK
    nt = s // n
    scale = 1.0 / math.sqrt(HEAD_DIM)

    def body(q_ref, k_ref, v_ref, zg_ref, o_ref, lse_ref, dy_ref, g_ref, sl_ref, dp_in, dp_ref, dg_ref,
             do_n, dt_n, dq_n, dk_n, dv_n, qp, kp, vp, dop, dtp, lsep, pdq, pdk, pdv):
        del dp_in
        dq_ref, dk_ref, dv_ref, dzg_ref = (dp_ref.at[i] for i in range(4))
        slope = sl_ref[...][:, :1]
        g = g_ref[...]

        def prologue(t, dg):
            rows = pl.ds(pl.multiple_of(t * n, n), n)
            o = o_ref[rows, :]
            do, dzg, dgi = _head_out_bwd(o, zg_ref[rows, :], g, dy_ref[rows, :])
            dzg_ref[rows, :] = dzg.astype(BF16)
            do_n[rows, :] = do
            dt_n[rows, :] = jnp.broadcast_to(jnp.sum(do * o, axis=-1, keepdims=True), (n, HEAD_DIM))
            return dg + dgi

        dg_ref[...] = lax.fori_loop(0, nt, prologue, jnp.zeros((1, HEAD_DIM), F32))
        dq_n[...] = jnp.zeros_like(dq_n)
        dk_n[...] = jnp.zeros_like(dk_n)
        dv_n[...] = jnp.zeros_like(dv_n)
        kp[pl.ds(0, n), :] = jnp.zeros((n, HEAD_DIM), BF16)
        vp[pl.ds(0, n), :] = jnp.zeros((n, HEAD_DIM), BF16)

        for (window, r) in DIL_PAIRS:
            nb = (s // r) // n
            masks = _dil_masks(slope, float(r))
            for nat, per, padded in _dil_chunks(s, r):
                qp[per, :] = q_ref[nat, :].astype(BF16)
                kp[padded, :] = k_ref[nat, :].astype(BF16)
                vp[padded, :] = v_ref[nat, :].astype(BF16)
                dop[per, :] = do_n[nat, :].astype(BF16)
                dtp[per, :] = dt_n[nat, :]
                lsep[per, :] = lse_ref[nat, :]
            pdk[...] = jnp.zeros_like(pdk)
            pdv[...] = jnp.zeros_like(pdv)

            def tiles(tt, _):
                ts = [tt * DIL_GROUP + i for i in range(DIL_GROUP)]
                rows = [pl.ds(pl.multiple_of(t * n, n), n) for t in ts]
                both = [pl.ds(pl.multiple_of(t * n, n), 2 * n) for t in ts]
                q = [qp[rw, :] for rw in rows]
                do = [dop[rw, :] for rw in rows]
                sc = [_dil_scores(qq, kp[bt, :], masks, lax.rem(t, nb) == 0) for t, qq, bt in zip(ts, q, both)]
                dp = [_dot(dd, vp[bt, :], NT) for dd, bt in zip(do, both)]
                p = [jnp.exp(x - lsep[rw, :][:, :1]) for x, rw in zip(sc, rows)]
                ds = [((x * (y - dtp[rw, :][:, :1])) * scale).astype(BF16) for x, y, rw in zip(p, dp, rows)]
                dq = [_dot(x, kp[bt, :], NN) for x, bt in zip(ds, both)]
                dk = [_dot(x, qq, TN) for x, qq in zip(ds, q)]
                dv = [_dot(x.astype(BF16), dd, TN) for x, dd in zip(p, do)]
                for rw, bt, x, y, z in zip(rows, both, dq, dk, dv):
                    pdq[rw, :] = x
                    pdk[bt, :] += y
                    pdv[bt, :] += z
                return 0

            lax.fori_loop(0, nt // DIL_GROUP, tiles, 0)
            for nat, per, padded in _dil_chunks(s, r):
                dq_n[nat, :] += pdq[per, :]
                dk_n[nat, :] += pdk[padded, :]
                dv_n[nat, :] += pdv[padded, :]

        dq_ref[...] = dq_n[...].astype(BF16)
        dk_ref[...] = dk_n[...].astype(BF16)
        dv_ref[...] = dv_n[...].astype(BF16)

    f32_s = pltpu.VMEM((s, HEAD_DIM), F32)
    f32_pad = pltpu.VMEM((s + n, HEAD_DIM), F32)
    bf_s = pltpu.VMEM((s, HEAD_DIM), BF16)
    bf_pad = pltpu.VMEM((s + n, HEAD_DIM), BF16)
    return pl.pallas_call(
        body, name="dil_bwd", grid=(b, n_heads),
        in_specs=[_seg_spec(s, 4), _seg_spec(s, 5), _seg_spec(s, 6), _seg_spec(s, 7),
                  _head_spec(s), _head_spec(s), _seg_spec(s, 1),
                  pl.BlockSpec((1, HEAD_DIM), lambda i, h: (0, h)),
                  pl.BlockSpec((None, 1, HEAD_DIM), lambda i, h: (h, 0, 0)), ANY],
        out_specs=[_seg4_spec(s, 1), pl.BlockSpec((None, 1, HEAD_DIM), lambda i, h: (i, 0, h))],
        out_shape=[jax.ShapeDtypeStruct((8, b, s, w), BF16), jax.ShapeDtypeStruct((b, 1, w), F32)],
        scratch_shapes=[f32_s] * 5 + [bf_s, bf_pad, bf_pad, bf_s] + [f32_s, f32_s, f32_s, f32_pad, f32_pad],
        input_output_aliases={9: 0},
        compiler_params=_cparams(("parallel", "parallel")),
    )(proj8, proj8, proj8, proj8, o_dl, lse_dl, dy2, g_dil, slopes, dproj8)


def _small_update(gathered, n_b, params, m, v):
    n_dev, _, width = gathered.shape

    def body(g_ref, p_ref, m_ref, v_ref, grad_ref, d_ref, nm_ref, nv_ref, loss_ref):
        for row in range(2):
            acc = None
            for dev in range(n_dev):
                for i in range(n_b):
                    term = g_ref[dev, pl.ds(row * n_b + i, 1), :]
                    acc = term if acc is None else acc + term
            grad_ref[pl.ds(row, 1), :] = acc
        loss = g_ref[0, pl.ds(2 * n_b, 1), pl.ds(0, 128)]
        for dev in range(1, n_dev):
            loss = loss + g_ref[dev, pl.ds(2 * n_b, 1), pl.ds(0, 128)]
        loss_ref[...] = loss
        d, nm, nv = _adamw_math(p_ref[...], grad_ref[...], m_ref[...], v_ref[...])
        d_ref[...] = d
        nm_ref[...] = nm
        nv_ref[...] = nv

    sds = jax.ShapeDtypeStruct((2, width), F32)
    return pl.pallas_call(
        body, name="small_update",
        in_specs=[VMEM_SPEC] * 4, out_specs=[VMEM_SPEC] * 5,
        out_shape=[sds, sds, sds, sds, jax.ShapeDtypeStruct((1, 128), F32)],
        compiler_params=_cparams(),
    )(gathered, params, m, v)


def _wada_update(c_t, dmod, w, m, v):
    d, bt = c_t.shape
    _, n = dmod.shape
    tr, tc = _tile(d, 512), _tile(n, 1024)

    def body(c_ref, dm_ref, w_ref, m_ref, v_ref, g_ref, d_ref, nm_ref, nv_ref):
        cv = c_ref[...]
        cs = (cv * _sigmoid(cv)).astype(BF16)
        grad = _dot(cs, dm_ref[...].astype(BF16), NN)
        g_ref[...] = grad
        dl, nm, nv = _adamw_math(w_ref[...], grad, m_ref[...], v_ref[...])
        d_ref[...] = dl
        nm_ref[...] = nm
        nv_ref[...] = nv

    spec = pl.BlockSpec((tr, tc), lambda i, j: (i, j))
    sds = jax.ShapeDtypeStruct((d, n), F32)
    return pl.pallas_call(
        body, name="wada_update", grid=(d // tr, n // tc),
        in_specs=[pl.BlockSpec((tr, bt), lambda i, j: (i, 0)), pl.BlockSpec((bt, tc), lambda i, j: (0, j)),
                  spec, spec, spec],
        out_specs=[spec] * 4, out_shape=[sds] * 4,
        compiler_params=_cparams(("parallel", "parallel")),
    )(c_t, dmod, w, m, v)


def _reduce_begin(gs, tag):
    ra = _sibling_half_swap(gs, "swap_" + tag)
    pa, own = _pair_sum(gs, ra, "pair_sum_" + tag)
    return _ScatterChips(pa), own


def _reduce_finish(rb, own, w, m, v, tag):
    half = _chip_sum(rb, own, "chip_sum_" + tag)
    return _adamw(w, _sibling_join(half, "join_" + tag), m, v, "adamw_" + tag)


def kernel(x, c, w_ada, b_ada, g_norm, w_in, g_sb, g_dil, w_out, g_final, loss_target, m_w_ada, m_b_ada, m_g_norm, m_w_in, m_g_sb, m_g_dil, m_w_out, m_g_final, v_w_ada, v_b_ada, v_g_norm, v_w_in, v_g_sb, v_g_dil, v_w_out, v_g_final):
    nb, s, d = x.shape
    t = nb * s
    na = w_ada.shape[2]
    cs = w_in.shape[2]
    w = cs // 2
    n_heads = w // HEAD_DIM
    r_out = w_out.shape[1]
    assert 2 * nb + 1 <= 8 and 2 * d + 2 * w <= 3 * d and N_CHIPS * na == 3 * d and N_CHIPS * r_out == 2 * w
    xi, yi, ci = _place()
    chip = 2 * xi + yi
    dev = 2 * chip + ci

    c_all = _allgather8(jnp.pad(c, ((0, 8 - nb), (0, 0))), "gather_c")
    c16 = c_all.reshape(N_DEV, 8, d)[:, :nb].reshape(N_DEV * nb, d)
    b_ada_shard = lax.dynamic_slice(b_ada, (0, chip * na), (1, na))
    mod_part = _mod_fwd(c16, w_ada[0], b_ada_shard)
    mod_all = _allgather8(mod_part, "gather_mod")
    mod_full = mod_all.reshape(N_CHIPS, 2, N_DEV * nb, na)[:, 0].transpose(1, 0, 2).reshape(N_DEV * nb, 3 * d)
    mod = lax.dynamic_slice(mod_full, (dev * nb, 0), (nb, 3 * d))
    shift, scale, gate = (mod[:, i * d:(i + 1) * d].reshape(nb, 1, d) for i in range(3))

    h = _norm_mod_fwd(x, g_norm, scale, shift)
    h2 = h.reshape(t, d)
    ws_in_slab, w_in_own = _cast_bf16_slab(w_in[0], "cast_w_in", with_own=True)
    (ws_out_slab,) = _cast_bf16_slab(w_out[0], "cast_w_out")
    proj8, ws_in, ws_out = _proj_fwd_split(h2, w_in_own, _GatherChips(ws_in_slab), _GatherChips(ws_out_slab))
    proj8 = proj8.reshape(8, nb, s, w)
    w_out_full = ws_out.reshape(2 * w, d)

    slopes = jnp.exp2(-ALIBI_MAX_BIAS * jnp.arange(1, n_heads + 1, dtype=F32) / n_heads)
    slopes = jnp.broadcast_to(slopes[:, None, None], (n_heads, 1, HEAD_DIM))
    o_sb, tot_sb, y2, sb_trips = _sb_fwd(proj8, g_sb)
    o_dl, lse_dl, y2 = _dil_fwd(proj8, g_dil, slopes, y2)
    y2f = y2.reshape(2, t, w)
    out = _out_fwd(y2f, w_out_full)

    dx1, dout, dgate, dg_final, loss_part = _loss_head(
        x, out.reshape(nb, s, d), gate, g_final.reshape(1, d), loss_target)
    dout2 = dout.reshape(t, d)
    gs_out = _out_bwd_w(y2f, dout2).reshape(N_CHIPS, r_out, d)
    scatter_out, own_out = _reduce_begin(gs_out, "w_out")
    dy2 = _out_bwd_y(dout2, w_out_full).reshape(2, nb, s, w)
    dproj8, dg_sb, rb_out = _sb_bwd(proj8, o_sb, tot_sb, sb_trips, dy2, g_sb, fused=scatter_out)
    dproj8, dg_dl = _dil_bwd(proj8, o_dl, lse_dl, dy2, g_dil, slopes, dproj8)
    dproj8 = dproj8.reshape(8, t, w)
    gs_in = _proj_bwd_w(h2, dproj8)
    scatter_in, own_in = _reduce_begin(gs_in, "w_in")
    dh, rb_in = _proj_bwd_x(dproj8, ws_in, fused=scatter_in)
    grad_x, dshift, dscale, dg_norm = _norm_mod_bwd(x, dh.reshape(nb, s, d), dx1, g_norm, scale)

    width = 3 * d
    dmod = jnp.concatenate([dshift, dscale, dgate], axis=-1).reshape(nb, width)
    gains = jnp.concatenate([dg_sb.reshape(nb, w), dg_dl.reshape(nb, w)], axis=-1)
    gains = jnp.pad(gains, ((0, 0), (2 * d, width - 2 * d - 2 * w)))
    first = jnp.pad(jnp.concatenate([dg_norm, dg_final], axis=-1), ((0, nb - 1), (0, width - 2 * d)))
    loss_row = jnp.pad(loss_part, ((0, 0), (0, width - 128)))
    pack = jnp.concatenate([dmod, gains + first, loss_row, jnp.zeros((8 - 2 * nb - 1, width), F32)], axis=0)
    gathered = _allgather8(pack, "gather_small").reshape(N_DEV, 8, width)

    def stack(bias, gn, gf, gsb, gdl):
        row1 = jnp.concatenate([gn.reshape(1, d), gf.reshape(1, d), gsb.reshape(1, w), gdl.reshape(1, w)], axis=-1)
        return jnp.concatenate([bias.reshape(1, width), jnp.pad(row1, ((0, 0), (0, width - 2 * d - 2 * w)))], axis=0)

    small = _small_update(
        gathered, nb, stack(b_ada, g_norm, g_final, g_sb, g_dil),
        stack(m_b_ada, m_g_norm, m_g_final, m_g_sb, m_g_dil), stack(v_b_ada, v_g_norm, v_g_final, v_g_sb, v_g_dil))
    loss = small[4][0, 0]

    def unstack(a):
        return (a[0:1, :], a[1:2, 0:d], a[1, d:2 * d], a[1:2, 2 * d:2 * d + w], a[1:2, 2 * d + w:2 * d + 2 * w])

    (g_b, g_gn, g_gf, g_gsb, g_gdl), (d_b, d_gn, d_gf, d_gsb, d_gdl), (nm_b, nm_gn, nm_gf, nm_gsb, nm_gdl), \
        (nv_b, nv_gn, nv_gf, nv_gsb, nv_gdl) = (unstack(a) for a in small[:4])

    dmod_all = gathered[:, :nb].reshape(N_DEV * nb, width)
    dmod_cols = lax.dynamic_slice(dmod_all, (0, chip * na), (N_DEV * nb, na))
    g_wa, d_wa, nm_wa, nv_wa = _wada_update(c16.T, dmod_cols, w_ada[0], m_w_ada[0], v_w_ada[0])

    g_wi, d_wi, nm_wi, nv_wi = _reduce_finish(rb_in, own_in, w_in[0], m_w_in[0], v_w_in[0], "w_in")
    g_wo, d_wo, nm_wo, nv_wo = _reduce_finish(rb_out, own_out, w_out[0], m_w_out[0], v_w_out[0], "w_out")

    lead = lambda a: a[None]
    return (loss, grad_x,
            lead(g_wa), g_b, g_gn, lead(g_wi), g_gsb, g_gdl, lead(g_wo), g_gf,
            lead(d_wa), d_b, d_gn, lead(d_wi), d_gsb, d_gdl, lead(d_wo), d_gf,
            lead(nm_wa), nm_b, nm_gn, lead(nm_wi), nm_gsb, nm_gdl, lead(nm_wo), nm_gf,
            lead(nv_wa), nv_b, nv_gn, lead(nv_wi), nv_gsb, nv_gdl, lead(nv_wo), nv_gf)
```

```python
import functools
import math

import jax
import jax.numpy as jnp
from jax import lax
from jax.experimental import pallas as pl
from jax.experimental.pallas import tpu as pltpu

F32 = jnp.float32
BF16 = jnp.bfloat16
MESH = pl.DeviceIdType.MESH

HEAD_DIM = 128
EPS = 1e-6
DIL_PAIRS = ((128, 1), (512, 4), (2048, 16))
ALIBI_MAX_BIAS = 8.0
ADAM_LR = 0.001
ADAM_B1 = 0.9
ADAM_B2 = 0.999
ADAM_EPS = 1e-08
ADAM_WD = 0.01
ADAM_STEP = 10
N_CHIPS = 4
N_DEV = 8
SCATTER_W_IN_EIGHTHS_IN_MATMUL = 6
VMEM_LIMIT_BYTES = 56 * 1024 * 1024
NEG_BIG = -1e30

NN = (((1,), (0,)), ((), ()))
NT = (((1,), (1,)), ((), ()))
TN = (((0,), (0,)), ((), ()))

ANY = pl.BlockSpec(memory_space=pl.ANY)
VMEM_SPEC = pl.BlockSpec(memory_space=pltpu.VMEM)


def _cparams(sem=None):
    return pltpu.CompilerParams(dimension_semantics=sem, vmem_limit_bytes=VMEM_LIMIT_BYTES)


def _tile(dim, pref):
    t = min(dim, pref)
    assert dim % t == 0, (dim, pref)
    return t


def _dot(a, b, dims):
    return lax.dot_general(a, b, dims, preferred_element_type=F32)


def _sigmoid(x):
    return 1.0 / (1.0 + jnp.exp(-x))


def _place():
    return lax.axis_index("x"), lax.axis_index("y"), lax.axis_index("c")


def _allgather8(x_shard, name):
    m_per, n = x_shard.shape

    def body(x_ref, out_ref, send_sems, recv_sems, local_sem):
        x, y, c = _place()
        me, sibling = (x, y, c), (x, y, 1 - c)
        chips = [(1 - x, y), (x, 1 - y), (1 - x, 1 - y)]

        def rows(px, py, pc):
            return out_ref.at[pl.ds((4 * px + 2 * py + pc) * m_per, m_per), :]

        def copy(k, block, to, src=None):
            return pltpu.make_async_remote_copy(
                src_ref=rows(*block) if src is None else src, dst_ref=rows(*block),
                send_sem=send_sems.at[k], recv_sem=recv_sems.at[k], device_id=to, device_id_type=MESH)

        mine = pltpu.make_async_copy(x_ref, rows(*me), local_sem)
        mine.start()
        first = [copy(0, me, sibling, src=x_ref)]
        first += [copy(1 + j, me, (*chip, c), src=x_ref) for j, chip in enumerate(chips)]
        for cp in first:
            cp.start()
        passed = [copy(4 + j, (*chip, c), sibling) for j, chip in enumerate(chips)]
        for j, chip in enumerate(chips):
            copy(1 + j, (*chip, c), me).wait_recv()
            passed[j].start()
        copy(0, sibling, me).wait_recv()
        for j, chip in enumerate(chips):
            copy(4 + j, (*chip, 1 - c), me).wait_recv()
        for cp in first + passed:
            cp.wait_send()
        mine.wait()

    return pl.pallas_call(
        body, name=name,
        out_shape=jax.ShapeDtypeStruct((N_DEV * m_per, n), x_shard.dtype),
        in_specs=[VMEM_SPEC], out_specs=VMEM_SPEC,
        scratch_shapes=[pltpu.SemaphoreType.DMA((7,)), pltpu.SemaphoreType.DMA((7,)), pltpu.SemaphoreType.DMA],
    )(x_shard)


class _GatherChips:
    def __init__(self, ws):
        self.inputs = [ws]
        self.out_shapes = [jax.ShapeDtypeStruct(ws.shape, ws.dtype)]
        self.aliases = {0: 0}
        self.scratch = [pltpu.SemaphoreType.DMA((12,)), pltpu.SemaphoreType.DMA((12,))]
        self.quarter = ws.shape[1] // 4

    def _copy(self, refs, k, chip, pc, part, to):
        _, out_ref, send_sems, recv_sems = refs
        rows = out_ref.at[2 * chip[0] + chip[1], pl.ds((2 * pc + part) * self.quarter, self.quarter), :]
        return pltpu.make_async_remote_copy(
            src_ref=rows, dst_ref=rows, send_sem=send_sems.at[k], recv_sem=recv_sems.at[k],
            device_id=to, device_id_type=MESH)

    def _sends(self, refs, phase):
        x, y, c = _place()
        sibling, x_nbr, y_nbr, diag = (x, y, 1 - c), (1 - x, y), (x, 1 - y), (1 - x, 1 - y)
        plan = {
            "start": [(0, (x, y), 0, (*x_nbr, c)), (1, (x, y), 1, (*y_nbr, c)),
                      (2, (x, y), 1, (*x_nbr, c)), (3, (x, y), 0, (*y_nbr, c))],
            "middle": [(4, x_nbr, 0, (*y_nbr, c)), (6, x_nbr, 0, sibling), (5, y_nbr, 1, (*x_nbr, c)),
                       (7, y_nbr, 1, sibling), (8, x_nbr, 1, sibling), (9, y_nbr, 0, sibling)],
            "wait": [(10, diag, 0, sibling), (11, diag, 1, sibling)],
        }[phase]
        return [self._copy(refs, k, chip, c, part, to) for k, chip, part, to in plan]

    def _landings(self, refs, phase):
        x, y, c = _place()
        me, x_nbr, y_nbr, diag = (x, y, c), (1 - x, y), (x, 1 - y), (1 - x, 1 - y)
        plan = {
            "middle": [(0, x_nbr, c, 0), (1, y_nbr, c, 1), (2, x_nbr, c, 1), (3, y_nbr, c, 0)],
            "wait": [(4, diag, c, 0), (5, diag, c, 1)],
            "sibling": [(6, x_nbr, 1 - c, 0), (7, y_nbr, 1 - c, 1), (8, x_nbr, 1 - c, 1), (9, y_nbr, 1 - c, 0),
                        (10, diag, 1 - c, 0), (11, diag, 1 - c, 1)],
        }[phase]
        return [self._copy(refs, k, chip, pc, part, me) for k, chip, pc, part in plan]

    def start(self, *refs):
        for cp in self._sends(refs, "start"):
            cp.start()

    def middle(self, *refs):
        landed = self._landings(refs, "middle")
        passed = self._sends(refs, "middle")
        landed[0].wait_recv()
        passed[0].start()
        passed[1].start()
        landed[1].wait_recv()
        passed[2].start()
        passed[3].start()
        landed[2].wait_recv()
        passed[4].start()
        landed[3].wait_recv()
        passed[5].start()

    def wait(self, *refs):
        landed = self._landings(refs, "wait")
        passed = self._sends(refs, "wait")
        for arrival, cp in zip(landed, passed):
            arrival.wait_recv()
            cp.start()
        for arrival in self._landings(refs, "sibling"):
            arrival.wait_recv()
        for phase in ("start", "middle", "wait"):
            for cp in self._sends(refs, phase):
                cp.wait_send()


def _sibling_half_swap(gs, name):
    n, r, cdim = gs.shape
    half = r // 2

    def body(g_ref, out_ref, send_sem, recv_sem):
        x, y, c = _place()
        cp = pltpu.make_async_remote_copy(
            src_ref=g_ref.at[:, pl.ds((1 - c) * half, half), :], dst_ref=out_ref,
            send_sem=send_sem, recv_sem=recv_sem, device_id=(x, y, 1 - c), device_id_type=MESH)
        cp.start()
        cp.wait()

    return pl.pallas_call(
        body, name=name,
        out_shape=jax.ShapeDtypeStruct((n, half, cdim), gs.dtype),
        in_specs=[ANY], out_specs=ANY,
        scratch_shapes=[pltpu.SemaphoreType.DMA, pltpu.SemaphoreType.DMA],
    )(gs)


class _ScatterChips:
    def __init__(self, pa, rows=None, rb=None):
        self.inputs = [pa] + ([] if rb is None else [rb])
        self.out_shapes = [jax.ShapeDtypeStruct(pa.shape, pa.dtype)]
        self.aliases = {} if rb is None else {1: 0}
        self.scratch = [pltpu.SemaphoreType.DMA((3,)), pltpu.SemaphoreType.DMA((3,)), pltpu.SemaphoreType.DMA]
        self.rows = pl.ds(*((0, pa.shape[1]) if rows is None else rows))

    def _mine(self, refs):
        p_ref, (out_ref, _, _, local_sem) = refs[0], refs[-4:]
        x, y, _ = _place()
        return pltpu.make_async_copy(p_ref.at[2 * x + y, self.rows], out_ref.at[2 * x + y, self.rows], local_sem)

    def _remote(self, refs, incoming):
        p_ref, (out_ref, send_sems, recv_sems, _) = refs[0], refs[-4:]
        x, y, c = _place()
        me = 2 * x + y
        remote = []
        for j, (px, py) in enumerate([(1 - x, y), (x, 1 - y), (1 - x, 1 - y)]):
            remote.append(pltpu.make_async_remote_copy(
                src_ref=p_ref.at[me if incoming else 2 * px + py, self.rows],
                dst_ref=out_ref.at[2 * px + py if incoming else me, self.rows],
                send_sem=send_sems.at[j], recv_sem=recv_sems.at[j], device_id=(px, py, c), device_id_type=MESH))
        return remote

    def start(self, *refs):
        self._mine(refs).start()
        for cp in self._remote(refs, incoming=False):
            cp.start()

    def wait(self, *refs):
        for cp in self._remote(refs, incoming=True):
            cp.wait_recv()
        for cp in self._remote(refs, incoming=False):
            cp.wait_send()
        self._mine(refs).wait()


def _fused_specs(fused):
    if fused is None:
        return [], [], [], [], []
    return (list(fused.inputs), [ANY] * len(fused.inputs), list(fused.out_shapes), [ANY] * len(fused.out_shapes),
            list(fused.scratch))


def _host_call(body, fused, *, name, grid, in_specs, out_specs, out_shape, args):
    f_in, f_in_specs, f_out, f_out_specs, f_scratch = _fused_specs(fused)
    n_in, n_out = len(in_specs), len(out_specs)

    def hosted(*refs):
        f_refs = (*refs[n_in:n_in + len(f_in)], *refs[n_in + len(f_in) + n_out:])
        _fused_begin(fused, grid, f_refs)
        body(*refs[:n_in], *refs[n_in + len(f_in):n_in + len(f_in) + n_out])
        _fused_end(fused, grid, f_refs)

    return pl.pallas_call(
        hosted, name=name, grid=grid, in_specs=list(in_specs) + f_in_specs, out_specs=list(out_specs) + f_out_specs,
        out_shape=list(out_shape) + f_out, scratch_shapes=f_scratch,
        input_output_aliases=_fused_aliases(fused, n_in, n_out),
        compiler_params=_cparams(("arbitrary",) * len(grid)),
    )(*args, *f_in)


def _fused_aliases(fused, first_input, first_output):
    aliases = getattr(fused, "aliases", {}) if fused is not None else {}
    return {first_input + i: first_output + o for i, o in aliases.items()}


def _fused_begin(fused, grid, refs):
    if fused is not None:
        first = functools.reduce(lambda p, q: p & q, [pl.program_id(i) == 0 for i in range(len(grid))])
        pl.when(first)(lambda: fused.start(*refs))
        if hasattr(fused, "middle"):
            step = functools.reduce(lambda acc, ig: acc * ig[1] + pl.program_id(ig[0]), enumerate(grid), 0)
            pl.when(step == math.prod(grid) // 2)(lambda: fused.middle(*refs))


def _fused_end(fused, grid, refs):
    if fused is not None:
        last = functools.reduce(lambda p, q: p & q, [pl.program_id(i) == g - 1 for i, g in enumerate(grid)])
        pl.when(last)(lambda: fused.wait(*refs))


def _sibling_join(full, name):
    h2, cdim = full.shape
    h = h2 // 2

    def body(in_ref, out_ref, send_sem, recv_sem):
        del in_ref
        x, y, c = _place()
        mine = out_ref.at[pl.ds(c * h, h), :]
        cp = pltpu.make_async_remote_copy(
            src_ref=mine, dst_ref=mine, send_sem=send_sem, recv_sem=recv_sem,
            device_id=(x, y, 1 - c), device_id_type=MESH)
        cp.start()
        theirs = out_ref.at[pl.ds((1 - c) * h, h), :]
        pltpu.make_async_remote_copy(
            src_ref=theirs, dst_ref=theirs, send_sem=send_sem, recv_sem=recv_sem,
            device_id=(x, y, 1 - c), device_id_type=MESH).wait_recv()
        cp.wait_send()

    return pl.pallas_call(
        body, name=name,
        out_shape=jax.ShapeDtypeStruct(full.shape, full.dtype),
        in_specs=[ANY], out_specs=ANY, input_output_aliases={0: 0},
        scratch_shapes=[pltpu.SemaphoreType.DMA, pltpu.SemaphoreType.DMA],
    )(full)


def _cast_bf16_slab(w, name, with_own=False):
    r, cdim = w.shape
    tr, tc = _tile(r, 512), _tile(cdim, 2048)

    def body(pc_ref, w_ref, o_ref, *own_ref):
        o_ref[...] = w_ref[...].astype(BF16)
        for ref in own_ref:
            ref[...] = w_ref[...].astype(BF16)

    plain = pl.BlockSpec((tr, tc), lambda i, j, pc: (i, j))
    return pl.pallas_call(
        body, name=name,
        grid_spec=pltpu.PrefetchScalarGridSpec(
            num_scalar_prefetch=1, grid=(r // tr, cdim // tc),
            in_specs=[plain],
            out_specs=[pl.BlockSpec((None, tr, tc), lambda i, j, pc: (pc[1], i, j))] + [plain] * with_own),
        out_shape=[jax.ShapeDtypeStruct((N_CHIPS, r, cdim), BF16)] + [jax.ShapeDtypeStruct((r, cdim), BF16)] * with_own,
        compiler_params=_cparams(("parallel", "parallel")),
    )(_place_scalars(), w)


def _place_scalars():
    x, y, c = _place()
    return jnp.stack([c, 2 * x + y]).astype(jnp.int32)


def _pair_sum(gs, ra, name):
    n, r, cdim = gs.shape
    half = r // 2
    tr, tc = _tile(half, 512), _tile(cdim, 2048)
    nt = half // tr

    def body(pc_ref, g_ref, r_ref, o_ref, own_ref):
        val = g_ref[...].astype(F32) + r_ref[...].astype(F32)
        o_ref[...] = val.astype(BF16)

        @pl.when(pl.program_id(2) == pc_ref[1])
        def _():
            own_ref[...] = val

    return pl.pallas_call(
        body, name=name,
        grid_spec=pltpu.PrefetchScalarGridSpec(
            num_scalar_prefetch=1, grid=(nt, cdim // tc, n),
            in_specs=[pl.BlockSpec((None, tr, tc), lambda i, j, s, pc: (s, pc[0] * nt + i, j)),
                      pl.BlockSpec((None, tr, tc), lambda i, j, s, pc: (s, i, j))],
            out_specs=[pl.BlockSpec((None, tr, tc), lambda i, j, s, pc: (s, i, j)),
                       pl.BlockSpec((tr, tc), lambda i, j, s, pc: (i, j))]),
        out_shape=[jax.ShapeDtypeStruct((n, half, cdim), BF16), jax.ShapeDtypeStruct((half, cdim), F32)],
        compiler_params=_cparams(("parallel", "parallel", "arbitrary")),
    )(_place_scalars(), gs, ra)


def _chip_sum(rb, own, name):
    n, h, cdim = rb.shape
    tr, tc = _tile(h, 256), _tile(cdim, 2048)
    nt = h // tr

    def body(pc_ref, r_ref, own_ref, o_ref):
        chip = pc_ref[1]
        acc = None
        for p in range(n):
            term = jnp.where(chip == p, own_ref[...], r_ref[p].astype(F32))
            acc = term if acc is None else acc + term
        o_ref[...] = acc

    return pl.pallas_call(
        body, name=name,
        grid_spec=pltpu.PrefetchScalarGridSpec(
            num_scalar_prefetch=1, grid=(nt, cdim // tc),
            in_specs=[pl.BlockSpec((n, tr, tc), lambda i, j, pc: (0, i, j)),
                      pl.BlockSpec((tr, tc), lambda i, j, pc: (i, j))],
            out_specs=pl.BlockSpec((tr, tc), lambda i, j, pc: (pc[0] * nt + i, j))),
        out_shape=jax.ShapeDtypeStruct((2 * h, cdim), F32),
        compiler_params=_cparams(("parallel", "parallel")),
    )(_place_scalars(), rb, own)


def _adamw_math(w, g, m, v):
    m = ADAM_B1 * m + (1.0 - ADAM_B1) * g
    v = ADAM_B2 * v + (1.0 - ADAM_B2) * (g * g)
    m_hat = m / (1.0 - ADAM_B1 ** ADAM_STEP)
    v_hat = v / (1.0 - ADAM_B2 ** ADAM_STEP)
    delta = -ADAM_LR * (m_hat / (jnp.sqrt(v_hat) + ADAM_EPS) + ADAM_WD * w)
    return delta, m, v


def _adamw(w, g, m, v, name):
    r, cdim = w.shape
    tr, tc = _tile(r, 256), _tile(cdim, 2048)

    def body(w_ref, g_ref, m_ref, v_ref, go_ref, d_ref, nm_ref, nv_ref):
        gv = g_ref[...]
        d, nm, nv = _adamw_math(w_ref[...], gv, m_ref[...], v_ref[...])
        go_ref[...] = gv
        d_ref[...] = d
        nm_ref[...] = nm
        nv_ref[...] = nv

    spec = pl.BlockSpec((tr, tc), lambda i, j: (i, j))
    sds = jax.ShapeDtypeStruct((r, cdim), F32)
    return pl.pallas_call(
        body, name=name, grid=(r // tr, cdim // tc),
        in_specs=[spec] * 4, out_specs=[spec] * 4, out_shape=[sds] * 4,
        compiler_params=_cparams(("parallel", "parallel")),
    )(w, g, m, v)


def _matmul(a, b, *, grid, a_spec, b_spec, out_spec, out_shape, acc_shape, dims, name, bias=None, bias_spec=None,
            silu_a=False, fused=None):
    nk = grid[2]
    f_in, f_in_specs, f_out, f_out_specs, f_scratch = _fused_specs(fused)
    n_in = 2 + (bias is not None)

    acc_scratch = [pltpu.VMEM(acc_shape, F32)] if nk > 1 else []

    def body(*refs):
        a_ref, b_ref = refs[:2]
        bias_ref = refs[2] if bias is not None else None
        o_ref = refs[n_in + len(f_in)]
        n_fixed = n_in + len(f_in) + 1 + len(f_out)
        f_refs = (*refs[n_in:n_in + len(f_in)], *refs[n_in + len(f_in) + 1:n_fixed],
                  *refs[n_fixed + len(acc_scratch):])
        _fused_begin(fused, grid, f_refs)

        def product():
            if len(a_ref.shape) == 3:
                tks = a_ref.shape[2]
                parts = [_dot(a_ref[i], b_ref[:, i * tks:(i + 1) * tks], dims) for i in range(a_ref.shape[0])]
                return functools.reduce(lambda p, q: p + q, parts)
            av = a_ref[...]
            if silu_a:
                av = av * _sigmoid(av)
            return _dot(av.astype(BF16), b_ref[...].astype(BF16), dims)

        def finish(res):
            if bias is not None:
                res = res + bias_ref[...]
            o_ref[...] = res.astype(o_ref.dtype)

        if nk == 1:
            finish(product())
        else:
            acc_ref = refs[n_fixed]
            k = pl.program_id(2)

            @pl.when(k == 0)
            def _():
                acc_ref[...] = product()

            if nk > 2:
                @pl.when((k > 0) & (k < nk - 1))
                def _():
                    acc_ref[...] += product()

            @pl.when(k == nk - 1)
            def _():
                finish(acc_ref[...] + product())

        _fused_end(fused, grid, f_refs)

    in_specs = [a_spec, b_spec] + ([] if bias is None else [bias_spec]) + f_in_specs
    args = (a, b) + (() if bias is None else (bias,)) + tuple(f_in)
    sem = ("parallel", "parallel", "arbitrary") if fused is None else ("arbitrary",) * 3
    res = pl.pallas_call(
        body, name=name, grid=grid, in_specs=in_specs, out_specs=[out_spec] + f_out_specs,
        out_shape=[out_shape] + f_out,
        scratch_shapes=acc_scratch + f_scratch,
        input_output_aliases=_fused_aliases(fused, n_in, 1),
        compiler_params=_cparams(sem),
    )(*args)
    return res[0] if fused is None else tuple(res)


def _mm_tiles(m, n, k):
    return _tile(m, 1024), _tile(n, 1024), _tile(k, 4096)


def _proj_part(h2, wmat, n_seg, w_block, seg_of, w, name, carry=None, fused=None):
    t, d = h2.shape
    tm, tn, _ = _mm_tiles(t, w, d)
    npseg = w // tn
    grid = (t // tm, n_seg * npseg)
    f_in, f_in_specs, f_out, f_out_specs, f_scratch = _fused_specs(fused)
    n_carry = carry is not None

    def body(place_ref, a_ref, b_ref, *refs):
        del place_ref
        o_ref = refs[n_carry + len(f_in)]
        f_refs = (*refs[n_carry:n_carry + len(f_in)], *refs[n_carry + len(f_in) + 1:])
        _fused_begin(fused, grid, f_refs)
        o_ref[...] = _dot(a_ref[...], b_ref[...], NN)
        _fused_end(fused, grid, f_refs)

    w_spec = pl.BlockSpec((d, tn) if wmat.ndim == 2 else (None, d, tn),
                          lambda m, n, place: w_block(n // npseg, n % npseg, place))
    res = pl.pallas_call(
        body, name=name,
        grid_spec=pltpu.PrefetchScalarGridSpec(
            num_scalar_prefetch=1, grid=grid,
            in_specs=[pl.BlockSpec((tm, d), lambda m, n, place: (m, 0)), w_spec] + [ANY] * n_carry + f_in_specs,
            out_specs=[pl.BlockSpec((None, tm, tn), lambda m, n, place: (seg_of(n // npseg, place), m, n % npseg))]
            + f_out_specs,
            scratch_shapes=f_scratch),
        out_shape=[jax.ShapeDtypeStruct((8, t, w), F32)] + f_out,
        input_output_aliases={**({3: 0} if n_carry else {}), **_fused_aliases(fused, 3 + n_carry, 1)},
        compiler_params=_cparams(("arbitrary", "arbitrary")),
    )(_place_scalars(), h2, wmat, *([carry] if n_carry else []), *f_in)
    return tuple(res)


def _proj_fwd_split(h2, w_own, gather_in, gather_out):
    w = w_own.shape[1] // 2
    npseg = w // _mm_tiles(h2.shape[0], w, h2.shape[1])[1]
    proj8, ws_in = _proj_part(
        h2, w_own, 2, lambda j, i, place: (0, j * npseg + i), lambda j, place: 2 * place[1] + j, w,
        "proj_fwd_own", fused=gather_in)

    def shard(j, place):
        return (place[1] + 1 + j // 2) % N_CHIPS

    proj8, ws_out = _proj_part(
        h2, ws_in, 6, lambda j, i, place: (shard(j, place), 0, (j % 2) * npseg + i),
        lambda j, place: 2 * shard(j, place) + j % 2, w, "proj_fwd_rest", carry=proj8, fused=gather_out)
    return proj8, ws_in, ws_out


def _proj_bwd_x(dproj8, ws_in, fused=None):
    _, t, w = dproj8.shape
    _, d, cs = ws_in.shape
    tm, tn, _ = _mm_tiles(t, d, w)
    return _matmul(
        dproj8, ws_in, grid=(t // tm, d // tn, N_CHIPS), dims=NT, name="proj_bwd_x", fused=fused,
        a_spec=pl.BlockSpec((2, tm, w), lambda m, n, k: (k, m, 0)),
        b_spec=pl.BlockSpec((None, tn, cs), lambda m, n, k: (k, n, 0)),
        out_spec=pl.BlockSpec((tm, tn), lambda m, n, k: (m, n)),
        out_shape=jax.ShapeDtypeStruct((t, d), F32), acc_shape=(tm, tn))


def _proj_bwd_w(h2, dproj8):
    t, d = h2.shape
    _, _, w = dproj8.shape
    cs = 2 * w
    tm, tn, tk = _mm_tiles(d, w, t)
    nps, npseg = cs // tn, w // tn
    return _matmul(
        h2, dproj8, grid=(d // tm, 8 * npseg, t // tk), dims=TN, name="proj_bwd_w",
        a_spec=pl.BlockSpec((tk, tm), lambda m, n, k: (k, m)),
        b_spec=pl.BlockSpec((None, tk, tn), lambda m, n, k: (n // npseg, k, n % npseg)),
        out_spec=pl.BlockSpec((None, tm, tn), lambda m, n, k: (n // nps, m, n % nps)),
        out_shape=jax.ShapeDtypeStruct((N_CHIPS, d, cs), BF16), acc_shape=(tm, tn))


def _out_fwd(y2, w_out):
    _, t, w = y2.shape
    _, d = w_out.shape
    tm, tn, tk = _mm_tiles(t, d, w)
    kpg = w // tk
    return _matmul(
        y2, w_out, grid=(t // tm, d // tn, 2 * kpg), dims=NN, name="out_fwd",
        a_spec=pl.BlockSpec((None, tm, tk), lambda m, n, k: (k // kpg, m, k % kpg)),
        b_spec=pl.BlockSpec((tk, tn), lambda m, n, k: (k, n)),
        out_spec=pl.BlockSpec((tm, tn), lambda m, n, k: (m, n)),
        out_shape=jax.ShapeDtypeStruct((t, d), F32), acc_shape=(tm, tn))


def _out_bwd_y(dout, w_out):
    t, d = dout.shape
    w = w_out.shape[0] // 2
    tm, tn, tk = _mm_tiles(t, w, d)
    npg = w // tn
    return _matmul(
        dout, w_out, grid=(t // tm, 2 * npg, d // tk), dims=NT, name="out_bwd_y",
        a_spec=pl.BlockSpec((tm, tk), lambda m, n, k: (m, k)),
        b_spec=pl.BlockSpec((tn, tk), lambda m, n, k: (n, k)),
        out_spec=pl.BlockSpec((None, tm, tn), lambda m, n, k: (n // npg, m, n % npg)),
        out_shape=jax.ShapeDtypeStruct((2, t, w), F32), acc_shape=(tm, tn))


def _out_bwd_w(y2, dout):
    _, t, w = y2.shape
    _, d = dout.shape
    tm, tn, tk = _mm_tiles(w, d, t)
    mpg = w // tm
    return _matmul(
        y2, dout, grid=(2 * mpg, d // tn, t // tk), dims=TN, name="out_bwd_w",
        a_spec=pl.BlockSpec((None, tk, tm), lambda m, n, k: (m // mpg, k, m % mpg)),
        b_spec=pl.BlockSpec((tk, tn), lambda m, n, k: (k, n)),
        out_spec=pl.BlockSpec((tm, tn), lambda m, n, k: (m, n)),
        out_shape=jax.ShapeDtypeStruct((2 * w, d), BF16), acc_shape=(tm, tn))


def _mod_fwd(c_all, w_ada, b_ada):
    bt, d = c_all.shape
    _, n = w_ada.shape
    tn, tk = _tile(n, 512), _tile(d, 1024)
    return _matmul(
        c_all, w_ada, grid=(1, n // tn, d // tk), dims=NN, name="mod_fwd", silu_a=True,
        a_spec=pl.BlockSpec((bt, tk), lambda i, j, l: (0, l)),
        b_spec=pl.BlockSpec((tk, tn), lambda i, j, l: (l, j)),
        bias=b_ada, bias_spec=pl.BlockSpec((1, tn), lambda i, j, l: (0, j)),
        out_spec=pl.BlockSpec((bt, tn), lambda i, j, l: (0, j)),
        out_shape=jax.ShapeDtypeStruct((bt, n), F32), acc_shape=(bt, tn))


def _norm_mod_fwd(x, g_norm, scale, shift):
    b, s, d = x.shape
    ts = _tile(s, 256)

    def body(x_ref, g_ref, sc_ref, sh_ref, h_ref):
        xv = x_ref[...]
        r = lax.rsqrt(jnp.mean(xv * xv, axis=-1, keepdims=True) + EPS)
        y = (xv * r) * g_ref[...]
        h_ref[...] = (y * (1.0 + sc_ref[...]) + sh_ref[...]).astype(BF16)

    row = pl.BlockSpec((None, ts, d), lambda i, j: (i, j, 0))
    per_b = pl.BlockSpec((None, 1, d), lambda i, j: (i, 0, 0))
    return pl.pallas_call(
        body, name="norm_mod_fwd", grid=(b, s // ts),
        in_specs=[row, pl.BlockSpec((1, d), lambda i, j: (0, 0)), per_b, per_b],
        out_specs=row, out_shape=jax.ShapeDtypeStruct((b, s, d), BF16),
        compiler_params=_cparams(("parallel", "parallel")),
    )(x, g_norm, scale, shift)


def _norm_mod_bwd(x, dh, dx1, g_norm, scale, fused=None):
    b, s, d = x.shape
    ts = _tile(s, 256)

    def body(x_ref, dh_ref, dx1_ref, g_ref, sc_ref, gx_ref, dsh_ref, dsc_ref, dg_ref):
        i, j = pl.program_id(0), pl.program_id(1)

        @pl.when(j == 0)
        def _():
            dsh_ref[...] = jnp.zeros_like(dsh_ref)
            dsc_ref[...] = jnp.zeros_like(dsc_ref)

        @pl.when((i == 0) & (j == 0))
        def _():
            dg_ref[...] = jnp.zeros_like(dg_ref)

        xv, dhv, g = x_ref[...], dh_ref[...], g_ref[...]
        r = lax.rsqrt(jnp.mean(xv * xv, axis=-1, keepdims=True) + EPS)
        xh = xv * r
        dsh_ref[...] += jnp.sum(dhv, axis=0, keepdims=True)
        dsc_ref[...] += jnp.sum(dhv * (xh * g), axis=0, keepdims=True)
        dn = dhv * (1.0 + sc_ref[...])
        dg_ref[...] += jnp.sum(dn * xh, axis=0, keepdims=True)
        u = dn * g
        dx = r * u - xv * (r * r * r) * jnp.mean(u * xv, axis=-1, keepdims=True)
        gx_ref[...] = dx1_ref[...] + dx

    row = pl.BlockSpec((None, ts, d), lambda i, j: (i, j, 0))
    per_b = pl.BlockSpec((None, 1, d), lambda i, j: (i, 0, 0))
    vec = pl.BlockSpec((1, d), lambda i, j: (0, 0))
    return _host_call(
        body, fused, name="norm_mod_bwd", grid=(b, s // ts),
        in_specs=[row, row, row, vec, per_b],
        out_specs=[row, per_b, per_b, vec],
        out_shape=[jax.ShapeDtypeStruct((b, s, d), F32), jax.ShapeDtypeStruct((b, 1, d), F32),
                   jax.ShapeDtypeStruct((b, 1, d), F32), jax.ShapeDtypeStruct((1, d), F32)],
        args=(x, dh, dx1, g_norm, scale))


def _loss_head(x, out, gate, g_final, target):
    b, s, d = x.shape
    ts = _tile(s, 256)

    def body(x_ref, o_ref, gt_ref, g_ref, t_ref, dx1_ref, dout_ref, dgt_ref, dg_ref, loss_ref):
        i, j = pl.program_id(0), pl.program_id(1)

        @pl.when(j == 0)
        def _():
            dgt_ref[...] = jnp.zeros_like(dgt_ref)

        @pl.when((i == 0) & (j == 0))
        def _():
            dg_ref[...] = jnp.zeros_like(dg_ref)
            loss_ref[...] = jnp.zeros_like(loss_ref)

        ov, gt, g = o_ref[...], gt_ref[...], g_ref[...]
        x1 = x_ref[...] + gt * ov
        r = lax.rsqrt(jnp.mean(x1 * x1, axis=-1, keepdims=True) + EPS)
        xh = x1 * r
        err = xh * g - t_ref[...]
        loss_ref[...] += 0.5 * jnp.sum(jnp.mean(err * err, axis=-1, keepdims=True))
        dfin = err * (1.0 / d)
        dg_ref[...] += jnp.sum(dfin * xh, axis=0, keepdims=True)
        u = dfin * g
        dx1 = r * u - x1 * (r * r * r) * jnp.mean(u * x1, axis=-1, keepdims=True)
        dx1_ref[...] = dx1
        dgt_ref[...] += jnp.sum(dx1 * ov, axis=0, keepdims=True)
        dout_ref[...] = (gt * dx1).astype(BF16)

    row = pl.BlockSpec((None, ts, d), lambda i, j: (i, j, 0))
    per_b = pl.BlockSpec((None, 1, d), lambda i, j: (i, 0, 0))
    vec = pl.BlockSpec((1, d), lambda i, j: (0, 0))
    return pl.pallas_call(
        body, name="loss_head", grid=(b, s // ts),
        in_specs=[row, row, per_b, vec, row],
        out_specs=[row, row, per_b, vec, pl.BlockSpec((1, 128), lambda i, j: (0, 0))],
        out_shape=[jax.ShapeDtypeStruct((b, s, d), F32), jax.ShapeDtypeStruct((b, s, d), BF16),
                   jax.ShapeDtypeStruct((b, 1, d), F32), jax.ShapeDtypeStruct((1, d), F32),
                   jax.ShapeDtypeStruct((1, 128), F32)],
        compiler_params=_cparams(("arbitrary", "arbitrary")),
    )(x, out, gate, g_final, target)


def _head_out(o, zg, g):
    rinv = lax.rsqrt(jnp.mean(o * o, axis=-1, keepdims=True) + EPS)
    return ((o * rinv) * g) * (zg * _sigmoid(zg))


def _head_out_bwd(o, zg, g, dy):
    rinv = lax.rsqrt(jnp.mean(o * o, axis=-1, keepdims=True) + EPS)
    rn = o * rinv
    sg = _sigmoid(zg)
    sil = zg * sg
    dzg = dy * (rn * g) * (sg * (1.0 + zg * (1.0 - sg)))
    dg = jnp.sum(dy * rn * sil, axis=0, keepdims=True)
    drn = dy * g * sil
    do = rinv * drn - o * (rinv * rinv * rinv) * jnp.mean(drn * o, axis=-1, keepdims=True)
    return do, dzg, dg


def _head_spec(s):
    return pl.BlockSpec((None, s, HEAD_DIM), lambda b, h: (b, 0, h))


def _seg_spec(s, seg):
    return pl.BlockSpec((None, None, s, HEAD_DIM), lambda b, h: (seg, b, 0, h))


def _seg4_spec(s, group):
    return pl.BlockSpec((4, None, s, HEAD_DIM), lambda b, h: (group, b, 0, h))


SB_Q_BLOCK = 512
SB_K_BLOCK = 256


SB_DEAD_LOG2 = -160.0
LOG2_E = 1.4426950408889634
SB_LOGIT_SCALE = LOG2_E / math.sqrt(HEAD_DIM)


def _sb_terms(raw, valid):
    t = jnp.where(valid, raw * SB_LOGIT_SCALE, NEG_BIG)
    e = jnp.exp2(-jnp.abs(t))
    l1m = -(jnp.maximum(t, 0.0) + jnp.log2(1.0 + e))
    return t, l1m, e


def _split_dot(a, u):
    hi = a.astype(BF16)
    lo = (a - hi.astype(F32)).astype(BF16)
    return _dot(hi, u, NN) + _dot(lo, u, NN)


def _sb_fwd(proj8, g_sb):
    _, b, s, w = proj8.shape
    n_heads = w // HEAD_DIM
    tq, tk = _tile(s, SB_Q_BLOCK), _tile(s, SB_K_BLOCK)
    nq, kpq = s // tq, tq // tk
    scale = 1.0 / math.sqrt(HEAD_DIM)

    def body(q_ref, k_ref, v_ref, zg_ref, g_ref, o_ref, tot_ref, y_ref, trips_ref):
        u_excl = (lax.broadcasted_iota(jnp.int32, (tk, tk), 0)
                  > lax.broadcasted_iota(jnp.int32, (tk, tk), 1)).astype(BF16)
        ahead = lax.broadcasted_iota(jnp.int32, (tq, tk), 0) - lax.broadcasted_iota(jnp.int32, (tq, tk), 1)
        g = g_ref[...]

        def qblock(i, _):
            rows = pl.ds(pl.multiple_of(i * tq, tq), tq)
            q = q_ref[rows, :].astype(BF16)
            nk = (i + 1) * kpq

            def alive(state):
                jj, _, csum = state
                return (jj <= i) & ((jj == 0) | (jnp.max(csum) > SB_DEAD_LOG2))

            def kblocks(state):
                jj, acc, csum = state
                js = [nk - 1 - (jj * kpq + n) for n in range(kpq)]
                cols = [pl.ds(pl.multiple_of(j * tk, tk), tk) for j in js]
                raw = [_dot(q, k_ref[c, :].astype(BF16), NT) for c in cols]
                terms = [_sb_terms(x, ahead > j * tk - i * tq) for x, j in zip(raw, js)]
                sums = [_split_dot(l1m, u_excl) for _, l1m, _ in terms]
                for (t, l1m, _), part, c in zip(terms, sums, cols):
                    a = jnp.exp2((t + l1m) + (part + csum))
                    acc = acc + _dot(a.astype(BF16), v_ref[c, :].astype(BF16), NN)
                    csum = csum + jnp.sum(l1m, axis=1, keepdims=True)
                return jj + 1, acc, csum

            trips, acc, tot = lax.while_loop(
                alive, kblocks, (jnp.int32(0), jnp.zeros((tq, HEAD_DIM), F32), jnp.zeros((tq, 1), F32)))
            o_ref[rows, :] = acc
            tot_ref[rows, :] = jnp.broadcast_to(tot, (tq, HEAD_DIM))
            y_ref[rows, :] = _head_out(acc, zg_ref[rows, :], g).astype(BF16)
            trips_ref[0, i] = trips.astype(F32)
            return 0

        lax.fori_loop(0, nq, qblock, 0)

    return pl.pallas_call(
        body, name="sb_fwd", grid=(b, n_heads),
        in_specs=[_seg_spec(s, 0), _seg_spec(s, 1), _seg_spec(s, 2), _seg_spec(s, 3),
                  pl.BlockSpec((1, HEAD_DIM), lambda i, h: (0, h))],
        out_specs=[_head_spec(s), _head_spec(s), _seg_spec(s, 0),
                   pl.BlockSpec((None, None, 1, nq), lambda i, h: (i, h, 0, 0), memory_space=pltpu.SMEM)],
        out_shape=[jax.ShapeDtypeStruct((b, s, w), F32), jax.ShapeDtypeStruct((b, s, w), F32),
                   jax.ShapeDtypeStruct((2, b, s, w), BF16), jax.ShapeDtypeStruct((b, n_heads, 1, nq), F32)],
        compiler_params=_cparams(("parallel", "parallel")),
    )(proj8, proj8, proj8, proj8, g_sb)


def _sb_bwd(proj8, o_sb, tot_sb, trips, dy2, g_sb, fused=None):
    _, b, s, w = proj8.shape
    n_heads = w // HEAD_DIM
    tq, tk = _tile(s, SB_Q_BLOCK), _tile(s, SB_K_BLOCK)
    nq, kpq = s // tq, tq // tk
    scale = 1.0 / math.sqrt(HEAD_DIM)

    f_in, f_in_specs, f_out, f_out_specs, f_scratch = _fused_specs(fused)
    grid = (b, n_heads)

    def body(*refs):
        q_ref, k_ref, v_ref, zg_ref, o_ref, tot_ref, dy_ref, g_ref, trips_ref = refs[:9]
        dp_ref, dg_ref = refs[9 + len(f_in):11 + len(f_in)]
        do_s, dk_s, dv_s = refs[11 + len(f_in) + len(f_out):14 + len(f_in) + len(f_out)]
        f_refs = (*refs[9:9 + len(f_in)], *refs[11 + len(f_in):11 + len(f_in) + len(f_out)],
                  *refs[14 + len(f_in) + len(f_out):])
        _fused_begin(fused, grid, f_refs)
        dq_ref, dk_ref, dv_ref, dzg_ref = (dp_ref.at[n] for n in range(4))
        ri = lax.broadcasted_iota(jnp.int32, (tk, tk), 0)
        ci = lax.broadcasted_iota(jnp.int32, (tk, tk), 1)
        u_le = (ri <= ci).astype(BF16)
        u_lt = (ri < ci).astype(BF16)
        ahead = lax.broadcasted_iota(jnp.int32, (tq, tk), 0) - lax.broadcasted_iota(jnp.int32, (tq, tk), 1)
        g = g_ref[...]

        def prologue(i, dg):
            rows = pl.ds(pl.multiple_of(i * tq, tq), tq)
            do, dzg, dgi = _head_out_bwd(o_ref[rows, :], zg_ref[rows, :], g, dy_ref[rows, :])
            dzg_ref[rows, :] = dzg.astype(BF16)
            do_s[rows, :] = do.astype(BF16)
            return dg + dgi

        dg_ref[...] = lax.fori_loop(0, nq, prologue, jnp.zeros((1, HEAD_DIM), F32))
        dk_s[...] = jnp.zeros_like(dk_s)
        dv_s[...] = jnp.zeros_like(dv_s)

        def qblock(i, _):
            rows = pl.ds(pl.multiple_of(i * tq, tq), tq)
            q = q_ref[rows, :].astype(BF16)
            do = do_s[rows, :]
            tot = tot_ref[rows, :][:, :1]

            def kblocks(jj, carry):
                dq, pre_l, pre_g = carry
                js = [jj * kpq + n for n in range(kpq)]
                cols = [pl.ds(pl.multiple_of(j * tk, tk), tk) for j in js]
                ks = [k_ref[c, :].astype(BF16) for c in cols]
                raw = [_dot(q, k, NT) for k in ks]
                da = [_dot(do, v_ref[c, :].astype(BF16), NT) for c in cols]
                terms = [_sb_terms(x, ahead > j * tk - i * tq) for x, j in zip(raw, js)]
                sums_l = [_split_dot(l1m, u_le) for _, l1m, _ in terms]
                a, gg = [], []
                for (t, l1m, _), part, d in zip(terms, sums_l, da):
                    a.append(jnp.exp2((t + l1m) + (tot - (part + pre_l))))
                    gg.append(a[-1] * d)
                    pre_l = pre_l + jnp.sum(l1m, axis=1, keepdims=True)
                sums_g = [_split_dot(x, u_lt) for x in gg]
                dzs = []
                for (t, _, e), x, part in zip(terms, gg, sums_g):
                    big_g = part + pre_g
                    pre_g = pre_g + jnp.sum(x, axis=1, keepdims=True)
                    inv = 1.0 / (1.0 + e)
                    sig = jnp.where(t >= 0.0, inv, e * inv)
                    dzs.append(((x - sig * (x + big_g)) * scale).astype(BF16))
                for x, k in zip(dzs, ks):
                    dq = dq + _dot(x, k, NN)
                for x, y, c in zip(dzs, a, cols):
                    dk_s[c, :] += _dot(x, q, TN)
                    dv_s[c, :] += _dot(y.astype(BF16), do, TN)
                return dq, pre_l, pre_g

            zero = jnp.zeros((tq, 1), F32)
            walked = jnp.clip(trips_ref[0, i].astype(jnp.int32), 1, i + 1)
            dq, _, _ = lax.fori_loop(i + 1 - walked, i + 1, kblocks, (jnp.zeros((tq, HEAD_DIM), F32), zero, zero))
            dq_ref[rows, :] = dq.astype(BF16)
            return 0

        lax.fori_loop(0, nq, qblock, 0)
        dk_ref[...] = dk_s[...].astype(BF16)
        dv_ref[...] = dv_s[...].astype(BF16)
        _fused_end(fused, grid, f_refs)

    return pl.pallas_call(
        body, name="sb_bwd", grid=grid,
        in_specs=[_seg_spec(s, 0), _seg_spec(s, 1), _seg_spec(s, 2), _seg_spec(s, 3),
                  _head_spec(s), _head_spec(s), _seg_spec(s, 0),
                  pl.BlockSpec((1, HEAD_DIM), lambda i, h: (0, h)),
                  pl.BlockSpec((None, None, 1, nq), lambda i, h: (i, h, 0, 0), memory_space=pltpu.SMEM)] + f_in_specs,
        out_specs=[_seg4_spec(s, 0), pl.BlockSpec((None, 1, HEAD_DIM), lambda i, h: (i, 0, h))] + f_out_specs,
        out_shape=[jax.ShapeDtypeStruct((8, b, s, w), BF16), jax.ShapeDtypeStruct((b, 1, w), F32)] + f_out,
        scratch_shapes=[pltpu.VMEM((s, HEAD_DIM), BF16), pltpu.VMEM((s, HEAD_DIM), F32),
                        pltpu.VMEM((s, HEAD_DIM), F32)] + f_scratch,
        compiler_params=_cparams(("arbitrary", "arbitrary")),
    )(proj8, proj8, proj8, proj8, o_sb, tot_sb, dy2, g_sb, trips, *f_in)


DIL_BLOCK = 128
DIL_GROUP = 4


def _dil_chunks(s, r):
    length = s // r
    out = []
    for rho in range(r):
        for cc in range(length // DIL_BLOCK):
            if r == 1:
                nat = pl.ds(cc * DIL_BLOCK, DIL_BLOCK)
            else:
                nat = pl.ds(rho + r * DIL_BLOCK * cc, DIL_BLOCK, stride=r)
            off = rho * length + cc * DIL_BLOCK
            out.append((nat, pl.ds(off, DIL_BLOCK), pl.ds(off + DIL_BLOCK, DIL_BLOCK)))
    return out


def _dil_masks(slope, r):
    n = DIL_BLOCK
    ri = lax.broadcasted_iota(jnp.int32, (n, 2 * n), 0)
    ci = lax.broadcasted_iota(jnp.int32, (n, 2 * n), 1)
    steps = ri - ci + n
    inside = (steps >= 0) & (steps <= n)
    bias = slope * (steps.astype(F32) * r)
    return jnp.where(inside, -bias, NEG_BIG), jnp.where(inside & (ci >= n), -bias, NEG_BIG)


def _dil_scores(q, k_pc, masks, first):
    return _dot(q, k_pc, NT) * (1.0 / math.sqrt(HEAD_DIM)) + jnp.where(first, masks[1], masks[0])


def _dil_check(s):
    assert (s // DIL_BLOCK) % DIL_GROUP == 0, s
    for window, r in DIL_PAIRS:
        assert window // r == DIL_BLOCK and s % (r * DIL_BLOCK) == 0, (s, window, r)


def _dil_fwd(proj8, g_dil, slopes, y2):
    _, b, s, w = proj8.shape
    n_heads = w // HEAD_DIM
    _dil_check(s)
    n = DIL_BLOCK
    nt = s // n

    def body(q_ref, k_ref, v_ref, zg_ref, g_ref, sl_ref, y_in, o_ref, lse_ref, y_ref,
             qp, kp, vp, pnum, pm, pl_, acc_s, m_s, l_s):
        del y_in
        slope = sl_ref[...][:, :1]
        kp[pl.ds(0, n), :] = jnp.zeros((n, HEAD_DIM), BF16)
        vp[pl.ds(0, n), :] = jnp.zeros((n, HEAD_DIM), BF16)

        for (window, r) in DIL_PAIRS:
            nb = (s // r) // n
            masks = _dil_masks(slope, float(r))
            for nat, per, padded in _dil_chunks(s, r):
                qp[per, :] = q_ref[nat, :].astype(BF16)
                kp[padded, :] = k_ref[nat, :].astype(BF16)
                vp[padded, :] = v_ref[nat, :].astype(BF16)
            num_t, m_t, l_t = (acc_s, m_s, l_s) if r == 1 else (pnum, pm, pl_)

            def tiles(tt, _):
                ts = [tt * DIL_GROUP + i for i in range(DIL_GROUP)]
                rows = [pl.ds(pl.multiple_of(t * n, n), n) for t in ts]
                both = [pl.ds(pl.multiple_of(t * n, n), 2 * n) for t in ts]
                sc = [_dil_scores(qp[rw, :], kp[bt, :], masks, lax.rem(t, nb) == 0)
                      for t, rw, bt in zip(ts, rows, both)]
                m = [jnp.max(x, axis=1, keepdims=True) for x in sc]
                p = [jnp.exp(x - mx) for x, mx in zip(sc, m)]
                num = [_dot(x.astype(BF16), vp[bt, :], NN) for x, bt in zip(p, both)]
                for rw, x, mx, nm in zip(rows, p, m, num):
                    num_t[rw, :] = nm
                    m_t[rw, :] = jnp.broadcast_to(mx, (n, HEAD_DIM))
                    l_t[rw, :] = jnp.broadcast_to(jnp.sum(x, axis=1, keepdims=True), (n, HEAD_DIM))
                return 0

            lax.fori_loop(0, nt // DIL_GROUP, tiles, 0)
            if r != 1:
                for nat, per, _ in _dil_chunks(s, r):
                    m_old, m_new_p = m_s[nat, :], pm[per, :]
                    m_new = jnp.maximum(m_old, m_new_p)
                    a_old, a_p = jnp.exp(m_old - m_new), jnp.exp(m_new_p - m_new)
                    m_s[nat, :] = m_new
                    l_s[nat, :] = l_s[nat, :] * a_old + pl_[per, :] * a_p
                    acc_s[nat, :] = acc_s[nat, :] * a_old + pnum[per, :] * a_p

        g = g_ref[...]

        def finish(t, _):
            rows = pl.ds(pl.multiple_of(t * n, n), n)
            l = l_s[rows, :]
            o = acc_s[rows, :] / l
            o_ref[rows, :] = o
            lse_ref[rows, :] = m_s[rows, :] + jnp.log(l)
            y_ref[rows, :] = _head_out(o, zg_ref[rows, :], g).astype(BF16)
            return 0

        lax.fori_loop(0, nt, finish, 0)

    f32_s = pltpu.VMEM((s, HEAD_DIM), F32)
    bf_s = pltpu.VMEM((s, HEAD_DIM), BF16)
    bf_pad = pltpu.VMEM((s + n, HEAD_DIM), BF16)
    return pl.pallas_call(
        body, name="dil_fwd", grid=(b, n_heads),
        in_specs=[_seg_spec(s, 4), _seg_spec(s, 5), _seg_spec(s, 6), _seg_spec(s, 7),
                  pl.BlockSpec((1, HEAD_DIM), lambda i, h: (0, h)),
                  pl.BlockSpec((None, 1, HEAD_DIM), lambda i, h: (h, 0, 0)), ANY],
        out_specs=[_head_spec(s), _head_spec(s), _seg_spec(s, 1)],
        out_shape=[jax.ShapeDtypeStruct((b, s, w), F32), jax.ShapeDtypeStruct((b, s, w), F32),
                   jax.ShapeDtypeStruct((2, b, s, w), BF16)],
        scratch_shapes=[bf_s, bf_pad, bf_pad, f32_s, f32_s, f32_s, f32_s, f32_s, f32_s],
        input_output_aliases={6: 2},
        compiler_params=_cparams(("parallel", "parallel")),
    )(proj8, proj8, proj8, proj8, g_dil, slopes, y2)


def _dil_bwd(proj8, o_dl, lse_dl, dy2, g_dil, slopes, dproj8):
    _, b, s, w = proj8.shape
    n_heads = w // HEAD_DIM
    _dil_check(s)
    n = DIL_BLOCK
    nt = s // n
    scale = 1.0 / math.sqrt(HEAD_DIM)

    def body(q_ref, k_ref, v_ref, zg_ref, o_ref, lse_ref, dy_ref, g_ref, sl_ref, dp_in, dp_ref, dg_ref,
             do_n, dt_n, dq_n, dk_n, dv_n, qp, kp, vp, dop, dtp, lsep, pdq, pdk, pdv):
        del dp_in
        dq_ref, dk_ref, dv_ref, dzg_ref = (dp_ref.at[i] for i in range(4))
        slope = sl_ref[...][:, :1]
        g = g_ref[...]

        def prologue(t, dg):
            rows = pl.ds(pl.multiple_of(t * n, n), n)
            o = o_ref[rows, :]
            do, dzg, dgi = _head_out_bwd(o, zg_ref[rows, :], g, dy_ref[rows, :])
            dzg_ref[rows, :] = dzg.astype(BF16)
            do_n[rows, :] = do
            dt_n[rows, :] = jnp.broadcast_to(jnp.sum(do * o, axis=-1, keepdims=True), (n, HEAD_DIM))
            return dg + dgi

        dg_ref[...] = lax.fori_loop(0, nt, prologue, jnp.zeros((1, HEAD_DIM), F32))
        dq_n[...] = jnp.zeros_like(dq_n)
        dk_n[...] = jnp.zeros_like(dk_n)
        dv_n[...] = jnp.zeros_like(dv_n)
        kp[pl.ds(0, n), :] = jnp.zeros((n, HEAD_DIM), BF16)
        vp[pl.ds(0, n), :] = jnp.zeros((n, HEAD_DIM), BF16)

        for (window, r) in DIL_PAIRS:
            nb = (s // r) // n
            masks = _dil_masks(slope, float(r))
            for nat, per, padded in _dil_chunks(s, r):
                qp[per, :] = q_ref[nat, :].astype(BF16)
                kp[padded, :] = k_ref[nat, :].astype(BF16)
                vp[padded, :] = v_ref[nat, :].astype(BF16)
                dop[per, :] = do_n[nat, :].astype(BF16)
                dtp[per, :] = dt_n[nat, :]
                lsep[per, :] = lse_ref[nat, :]
            pdk[...] = jnp.zeros_like(pdk)
            pdv[...] = jnp.zeros_like(pdv)

            def tiles(tt, _):
                ts = [tt * DIL_GROUP + i for i in range(DIL_GROUP)]
                rows = [pl.ds(pl.multiple_of(t * n, n), n) for t in ts]
                both = [pl.ds(pl.multiple_of(t * n, n), 2 * n) for t in ts]
                q = [qp[rw, :] for rw in rows]
                do = [dop[rw, :] for rw in rows]
                sc = [_dil_scores(qq, kp[bt, :], masks, lax.rem(t, nb) == 0) for t, qq, bt in zip(ts, q, both)]
                dp = [_dot(dd, vp[bt, :], NT) for dd, bt in zip(do, both)]
                p = [jnp.exp(x - lsep[rw, :][:, :1]) for x, rw in zip(sc, rows)]
                ds = [((x * (y - dtp[rw, :][:, :1])) * scale).astype(BF16) for x, y, rw in zip(p, dp, rows)]
                dq = [_dot(x, kp[bt, :], NN) for x, bt in zip(ds, both)]
                dk = [_dot(x, qq, TN) for x, qq in zip(ds, q)]
                dv = [_dot(x.astype(BF16), dd, TN) for x, dd in zip(p, do)]
                for rw, bt, x, y, z in zip(rows, both, dq, dk, dv):
                    pdq[rw, :] = x
                    pdk[bt, :] += y
                    pdv[bt, :] += z
                return 0

            lax.fori_loop(0, nt // DIL_GROUP, tiles, 0)
            for nat, per, padded in _dil_chunks(s, r):
                dq_n[nat, :] += pdq[per, :]
                dk_n[nat, :] += pdk[padded, :]
                dv_n[nat, :] += pdv[padded, :]

        dq_ref[...] = dq_n[...].astype(BF16)
        dk_ref[...] = dk_n[...].astype(BF16)
        dv_ref[...] = dv_n[...].astype(BF16)

    f32_s = pltpu.VMEM((s, HEAD_DIM), F32)
    f32_pad = pltpu.VMEM((s + n, HEAD_DIM), F32)
    bf_s = pltpu.VMEM((s, HEAD_DIM), BF16)
    bf_pad = pltpu.VMEM((s + n, HEAD_DIM), BF16)
    return pl.pallas_call(
        body, name="dil_bwd", grid=(b, n_heads),
        in_specs=[_seg_spec(s, 4), _seg_spec(s, 5), _seg_spec(s, 6), _seg_spec(s, 7),
                  _head_spec(s), _head_spec(s), _seg_spec(s, 1),
                  pl.BlockSpec((1, HEAD_DIM), lambda i, h: (0, h)),
                  pl.BlockSpec((None, 1, HEAD_DIM), lambda i, h: (h, 0, 0)), ANY],
        out_specs=[_seg4_spec(s, 1), pl.BlockSpec((None, 1, HEAD_DIM), lambda i, h: (i, 0, h))],
        out_shape=[jax.ShapeDtypeStruct((8, b, s, w), BF16), jax.ShapeDtypeStruct((b, 1, w), F32)],
        scratch_shapes=[f32_s] * 5 + [bf_s, bf_pad, bf_pad, bf_s] + [f32_s, f32_s, f32_s, f32_pad, f32_pad],
        input_output_aliases={9: 0},
        compiler_params=_cparams(("parallel", "parallel")),
    )(proj8, proj8, proj8, proj8, o_dl, lse_dl, dy2, g_dil, slopes, dproj8)


def _small_update(gathered, n_b, params, m, v):
    n_dev, _, width = gathered.shape

    def body(g_ref, p_ref, m_ref, v_ref, grad_ref, d_ref, nm_ref, nv_ref, loss_ref):
        for row in range(2):
            acc = None
            for dev in range(n_dev):
                for i in range(n_b):
                    term = g_ref[dev, pl.ds(row * n_b + i, 1), :]
                    acc = term if acc is None else acc + term
            grad_ref[pl.ds(row, 1), :] = acc
        loss = g_ref[0, pl.ds(2 * n_b, 1), pl.ds(0, 128)]
        for dev in range(1, n_dev):
            loss = loss + g_ref[dev, pl.ds(2 * n_b, 1), pl.ds(0, 128)]
        loss_ref[...] = loss
        d, nm, nv = _adamw_math(p_ref[...], grad_ref[...], m_ref[...], v_ref[...])
        d_ref[...] = d
        nm_ref[...] = nm
        nv_ref[...] = nv

    sds = jax.ShapeDtypeStruct((2, width), F32)
    return pl.pallas_call(
        body, name="small_update",
        in_specs=[VMEM_SPEC] * 4, out_specs=[VMEM_SPEC] * 5,
        out_shape=[sds, sds, sds, sds, jax.ShapeDtypeStruct((1, 128), F32)],
        compiler_params=_cparams(),
    )(gathered, params, m, v)


def _wada_update(c_t, dmod, w, m, v, fused=None):
    d, bt = c_t.shape
    _, n = dmod.shape
    tr, tc = _tile(d, 512), _tile(n, 1024)

    def body(c_ref, dm_ref, w_ref, m_ref, v_ref, g_ref, d_ref, nm_ref, nv_ref):
        cv = c_ref[...]
        cs = (cv * _sigmoid(cv)).astype(BF16)
        grad = _dot(cs, dm_ref[...].astype(BF16), NN)
        g_ref[...] = grad
        dl, nm, nv = _adamw_math(w_ref[...], grad, m_ref[...], v_ref[...])
        d_ref[...] = dl
        nm_ref[...] = nm
        nv_ref[...] = nv

    spec = pl.BlockSpec((tr, tc), lambda i, j: (i, j))
    sds = jax.ShapeDtypeStruct((d, n), F32)
    return _host_call(
        body, fused, name="wada_update", grid=(d // tr, n // tc),
        in_specs=[pl.BlockSpec((tr, bt), lambda i, j: (i, 0)), pl.BlockSpec((bt, tc), lambda i, j: (0, j)),
                  spec, spec, spec],
        out_specs=[spec] * 4, out_shape=[sds] * 4, args=(c_t, dmod, w, m, v))


def _reduce_begin(gs, tag):
    ra = _sibling_half_swap(gs, "swap_" + tag)
    return _pair_sum(gs, ra, "pair_sum_" + tag)


def _reduce_finish(rb, own, w, m, v, tag):
    half = _chip_sum(rb, own, "chip_sum_" + tag)
    return _adamw(w, _sibling_join(half, "join_" + tag), m, v, "adamw_" + tag)


def kernel(x, c, w_ada, b_ada, g_norm, w_in, g_sb, g_dil, w_out, g_final, loss_target, m_w_ada, m_b_ada, m_g_norm, m_w_in, m_g_sb, m_g_dil, m_w_out, m_g_final, v_w_ada, v_b_ada, v_g_norm, v_w_in, v_g_sb, v_g_dil, v_w_out, v_g_final):
    nb, s, d = x.shape
    t = nb * s
    na = w_ada.shape[2]
    cs = w_in.shape[2]
    w = cs // 2
    n_heads = w // HEAD_DIM
    r_out = w_out.shape[1]
    assert 2 * nb + 1 <= 8 and 2 * d + 2 * w <= 3 * d and N_CHIPS * na == 3 * d and N_CHIPS * r_out == 2 * w
    xi, yi, ci = _place()
    chip = 2 * xi + yi
    dev = 2 * chip + ci

    c_all = _allgather8(jnp.pad(c, ((0, 8 - nb), (0, 0))), "gather_c")
    c16 = c_all.reshape(N_DEV, 8, d)[:, :nb].reshape(N_DEV * nb, d)
    b_ada_shard = lax.dynamic_slice(b_ada, (0, chip * na), (1, na))
    mod_part = _mod_fwd(c16, w_ada[0], b_ada_shard)
    mod_all = _allgather8(mod_part, "gather_mod")
    mod_full = mod_all.reshape(N_CHIPS, 2, N_DEV * nb, na)[:, 0].transpose(1, 0, 2).reshape(N_DEV * nb, 3 * d)
    mod = lax.dynamic_slice(mod_full, (dev * nb, 0), (nb, 3 * d))
    shift, scale, gate = (mod[:, i * d:(i + 1) * d].reshape(nb, 1, d) for i in range(3))

    h = _norm_mod_fwd(x, g_norm, scale, shift)
    h2 = h.reshape(t, d)
    ws_in_slab, w_in_own = _cast_bf16_slab(w_in[0], "cast_w_in", with_own=True)
    (ws_out_slab,) = _cast_bf16_slab(w_out[0], "cast_w_out")
    proj8, ws_in, ws_out = _proj_fwd_split(h2, w_in_own, _GatherChips(ws_in_slab), _GatherChips(ws_out_slab))
    proj8 = proj8.reshape(8, nb, s, w)
    w_out_full = ws_out.reshape(2 * w, d)

    slopes = jnp.exp2(-ALIBI_MAX_BIAS * jnp.arange(1, n_heads + 1, dtype=F32) / n_heads)
    slopes = jnp.broadcast_to(slopes[:, None, None], (n_heads, 1, HEAD_DIM))
    o_sb, tot_sb, y2, sb_trips = _sb_fwd(proj8, g_sb)
    o_dl, lse_dl, y2 = _dil_fwd(proj8, g_dil, slopes, y2)
    y2f = y2.reshape(2, t, w)
    out = _out_fwd(y2f, w_out_full)

    dx1, dout, dgate, dg_final, loss_part = _loss_head(
        x, out.reshape(nb, s, d), gate, g_final.reshape(1, d), loss_target)
    dout2 = dout.reshape(t, d)
    gs_out = _out_bwd_w(y2f, dout2).reshape(N_CHIPS, r_out, d)
    pa_out, own_out = _reduce_begin(gs_out, "w_out")
    dy2 = _out_bwd_y(dout2, w_out_full).reshape(2, nb, s, w)
    dproj8, dg_sb, rb_out = _sb_bwd(proj8, o_sb, tot_sb, sb_trips, dy2, g_sb, fused=_ScatterChips(pa_out))
    dproj8, dg_dl = _dil_bwd(proj8, o_dl, lse_dl, dy2, g_dil, slopes, dproj8)
    dproj8 = dproj8.reshape(8, t, w)
    gs_in = _proj_bwd_w(h2, dproj8)
    pa_in, own_in = _reduce_begin(gs_in, "w_in")
    rows_x = SCATTER_W_IN_EIGHTHS_IN_MATMUL * (pa_in.shape[1] // 8)
    tail = (pa_in.shape[1] - rows_x) // 2
    dh, rb_in = _proj_bwd_x(dproj8, ws_in, fused=_ScatterChips(pa_in, rows=(0, rows_x)))
    grad_x, dshift, dscale, dg_norm, rb_in = _norm_mod_bwd(
        x, dh.reshape(nb, s, d), dx1, g_norm, scale, fused=_ScatterChips(pa_in, rows=(rows_x, tail), rb=rb_in))

    width = 3 * d
    dmod = jnp.concatenate([dshift, dscale, dgate], axis=-1).reshape(nb, width)
    gains = jnp.concatenate([dg_sb.reshape(nb, w), dg_dl.reshape(nb, w)], axis=-1)
    gains = jnp.pad(gains, ((0, 0), (2 * d, width - 2 * d - 2 * w)))
    first = jnp.pad(jnp.concatenate([dg_norm, dg_final], axis=-1), ((0, nb - 1), (0, width - 2 * d)))
    loss_row = jnp.pad(loss_part, ((0, 0), (0, width - 128)))
    pack = jnp.concatenate([dmod, gains + first, loss_row, jnp.zeros((8 - 2 * nb - 1, width), F32)], axis=0)
    gathered = _allgather8(pack, "gather_small").reshape(N_DEV, 8, width)

    def stack(bias, gn, gf, gsb, gdl):
        row1 = jnp.concatenate([gn.reshape(1, d), gf.reshape(1, d), gsb.reshape(1, w), gdl.reshape(1, w)], axis=-1)
        return jnp.concatenate([bias.reshape(1, width), jnp.pad(row1, ((0, 0), (0, width - 2 * d - 2 * w)))], axis=0)

    small = _small_update(
        gathered, nb, stack(b_ada, g_norm, g_final, g_sb, g_dil),
        stack(m_b_ada, m_g_norm, m_g_final, m_g_sb, m_g_dil), stack(v_b_ada, v_g_norm, v_g_final, v_g_sb, v_g_dil))
    loss = small[4][0, 0]

    def unstack(a):
        return (a[0:1, :], a[1:2, 0:d], a[1, d:2 * d], a[1:2, 2 * d:2 * d + w], a[1:2, 2 * d + w:2 * d + 2 * w])

    (g_b, g_gn, g_gf, g_gsb, g_gdl), (d_b, d_gn, d_gf, d_gsb, d_gdl), (nm_b, nm_gn, nm_gf, nm_gsb, nm_gdl), \
        (nv_b, nv_gn, nv_gf, nv_gsb, nv_gdl) = (unstack(a) for a in small[:4])

    dmod_all = gathered[:, :nb].reshape(N_DEV * nb, width)
    dmod_cols = lax.dynamic_slice(dmod_all, (0, chip * na), (N_DEV * nb, na))
    g_wa, d_wa, nm_wa, nv_wa, rb_in = _wada_update(
        c16.T, dmod_cols, w_ada[0], m_w_ada[0], v_w_ada[0],
        fused=_ScatterChips(pa_in, rows=(rows_x + tail, tail), rb=rb_in))

    g_wi, d_wi, nm_wi, nv_wi = _reduce_finish(rb_in, own_in, w_in[0], m_w_in[0], v_w_in[0], "w_in")
    g_wo, d_wo, nm_wo, nv_wo = _reduce_finish(rb_out, own_out, w_out[0], m_w_out[0], v_w_out[0], "w_out")

    lead = lambda a: a[None]
    return (loss, grad_x,
            lead(g_wa), g_b, g_gn, lead(g_wi), g_gsb, g_gdl, lead(g_wo), g_gf,
            lead(d_wa), d_b, d_gn, lead(d_wi), d_gsb, d_gdl, lead(d_wo), d_gf,
            lead(nm_wa), nm_b, nm_gn, lead(nm_wi), nm_gsb, nm_gdl, lead(nm_wo), nm_gf,
            lead(nv_wa), nv_b, nv_gn, lead(nv_wi), nv_gsb, nv_gdl, lead(nv_wo), nv_gf)
```

```python
import functools
import math

import jax
import jax.numpy as jnp
from jax import lax
from jax.experimental import pallas as pl
from jax.experimental.pallas import tpu as pltpu

F32 = jnp.float32
BF16 = jnp.bfloat16
MESH = pl.DeviceIdType.MESH

HEAD_DIM = 128
EPS = 1e-6
DIL_PAIRS = ((128, 1), (512, 4), (2048, 16))
ALIBI_MAX_BIAS = 8.0
ADAM_LR = 0.001
ADAM_B1 = 0.9
ADAM_B2 = 0.999
ADAM_EPS = 1e-08
ADAM_WD = 0.01
ADAM_STEP = 10
N_CHIPS = 4
N_DEV = 8
VMEM_LIMIT_BYTES = 56 * 1024 * 1024
NEG_BIG = -1e30

NN = (((1,), (0,)), ((), ()))
NT = (((1,), (1,)), ((), ()))
TN = (((0,), (0,)), ((), ()))

ANY = pl.BlockSpec(memory_space=pl.ANY)
VMEM_SPEC = pl.BlockSpec(memory_space=pltpu.VMEM)


def _cparams(sem=None):
    return pltpu.CompilerParams(dimension_semantics=sem, vmem_limit_bytes=VMEM_LIMIT_BYTES)


def _tile(dim, pref):
    t = min(dim, pref)
    assert dim % t == 0, (dim, pref)
    return t


def _dot(a, b, dims):
    return lax.dot_general(a, b, dims, preferred_element_type=F32)


def _sigmoid(x):
    return 1.0 / (1.0 + jnp.exp(-x))


def _place():
    return lax.axis_index("x"), lax.axis_index("y"), lax.axis_index("c")


def _allgather8(x_shard, name):
    m_per, n = x_shard.shape

    def body(x_ref, out_ref, send_sems, recv_sems, local_sem):
        x, y, c = _place()
        me, sibling = (x, y, c), (x, y, 1 - c)
        chips = [(1 - x, y), (x, 1 - y), (1 - x, 1 - y)]

        def rows(px, py, pc):
            return out_ref.at[pl.ds((4 * px + 2 * py + pc) * m_per, m_per), :]

        def copy(k, block, to, src=None):
            return pltpu.make_async_remote_copy(
                src_ref=rows(*block) if src is None else src, dst_ref=rows(*block),
                send_sem=send_sems.at[k], recv_sem=recv_sems.at[k], device_id=to, device_id_type=MESH)

        mine = pltpu.make_async_copy(x_ref, rows(*me), local_sem)
        mine.start()
        first = [copy(0, me, sibling, src=x_ref)]
        first += [copy(1 + j, me, (*chip, c), src=x_ref) for j, chip in enumerate(chips)]
        for cp in first:
            cp.start()
        passed = [copy(4 + j, (*chip, c), sibling) for j, chip in enumerate(chips)]
        for j, chip in enumerate(chips):
            copy(1 + j, (*chip, c), me).wait_recv()
            passed[j].start()
        copy(0, sibling, me).wait_recv()
        for j, chip in enumerate(chips):
            copy(4 + j, (*chip, 1 - c), me).wait_recv()
        for cp in first + passed:
            cp.wait_send()
        mine.wait()

    return pl.pallas_call(
        body, name=name,
        out_shape=jax.ShapeDtypeStruct((N_DEV * m_per, n), x_shard.dtype),
        in_specs=[VMEM_SPEC], out_specs=VMEM_SPEC,
        scratch_shapes=[pltpu.SemaphoreType.DMA((7,)), pltpu.SemaphoreType.DMA((7,)), pltpu.SemaphoreType.DMA],
    )(x_shard)


class _GatherChips:
    def __init__(self, ws):
        self.inputs = [ws]
        self.out_shapes = [jax.ShapeDtypeStruct(ws.shape, ws.dtype)]
        self.aliases = {0: 0}
        self.scratch = [pltpu.SemaphoreType.DMA((12,)), pltpu.SemaphoreType.DMA((12,))]
        self.quarter = ws.shape[1] // 4

    def _copy(self, refs, k, chip, pc, part, to):
        _, out_ref, send_sems, recv_sems = refs
        rows = out_ref.at[2 * chip[0] + chip[1], pl.ds((2 * pc + part) * self.quarter, self.quarter), :]
        return pltpu.make_async_remote_copy(
            src_ref=rows, dst_ref=rows, send_sem=send_sems.at[k], recv_sem=recv_sems.at[k],
            device_id=to, device_id_type=MESH)

    def _sends(self, refs, phase):
        x, y, c = _place()
        sibling, x_nbr, y_nbr, diag = (x, y, 1 - c), (1 - x, y), (x, 1 - y), (1 - x, 1 - y)
        plan = {
            "start": [(0, (x, y), 0, (*x_nbr, c)), (1, (x, y), 1, (*y_nbr, c)),
                      (2, (x, y), 1, (*x_nbr, c)), (3, (x, y), 0, (*y_nbr, c))],
            "middle": [(4, x_nbr, 0, (*y_nbr, c)), (6, x_nbr, 0, sibling), (5, y_nbr, 1, (*x_nbr, c)),
                       (7, y_nbr, 1, sibling), (8, x_nbr, 1, sibling), (9, y_nbr, 0, sibling)],
            "wait": [(10, diag, 0, sibling), (11, diag, 1, sibling)],
        }[phase]
        return [self._copy(refs, k, chip, c, part, to) for k, chip, part, to in plan]

    def _landings(self, refs, phase):
        x, y, c = _place()
        me, x_nbr, y_nbr, diag = (x, y, c), (1 - x, y), (x, 1 - y), (1 - x, 1 - y)
        plan = {
            "middle": [(0, x_nbr, c, 0), (1, y_nbr, c, 1), (2, x_nbr, c, 1), (3, y_nbr, c, 0)],
            "wait": [(4, diag, c, 0), (5, diag, c, 1)],
            "sibling": [(6, x_nbr, 1 - c, 0), (7, y_nbr, 1 - c, 1), (8, x_nbr, 1 - c, 1), (9, y_nbr, 1 - c, 0),
                        (10, diag, 1 - c, 0), (11, diag, 1 - c, 1)],
        }[phase]
        return [self._copy(refs, k, chip, pc, part, me) for k, chip, pc, part in plan]

    def start(self, *refs):
        for cp in self._sends(refs, "start"):
            cp.start()

    def middle(self, *refs):
        landed = self._landings(refs, "middle")
        passed = self._sends(refs, "middle")
        landed[0].wait_recv()
        passed[0].start()
        passed[1].start()
        landed[1].wait_recv()
        passed[2].start()
        passed[3].start()
        landed[2].wait_recv()
        passed[4].start()
        landed[3].wait_recv()
        passed[5].start()

    def wait(self, *refs):
        landed = self._landings(refs, "wait")
        passed = self._sends(refs, "wait")
        for arrival, cp in zip(landed, passed):
            arrival.wait_recv()
            cp.start()
        for arrival in self._landings(refs, "sibling"):
            arrival.wait_recv()
        for phase in ("start", "middle", "wait"):
            for cp in self._sends(refs, phase):
                cp.wait_send()


def _sibling_half_swap(gs, name):
    n, r, cdim = gs.shape
    half = r // 2

    def body(g_ref, out_ref, send_sem, recv_sem):
        x, y, c = _place()
        cp = pltpu.make_async_remote_copy(
            src_ref=g_ref.at[:, pl.ds((1 - c) * half, half), :], dst_ref=out_ref,
            send_sem=send_sem, recv_sem=recv_sem, device_id=(x, y, 1 - c), device_id_type=MESH)
        cp.start()
        cp.wait()

    return pl.pallas_call(
        body, name=name,
        out_shape=jax.ShapeDtypeStruct((n, half, cdim), gs.dtype),
        in_specs=[ANY], out_specs=ANY,
        scratch_shapes=[pltpu.SemaphoreType.DMA, pltpu.SemaphoreType.DMA],
    )(gs)


class _ScatterChips:
    def __init__(self, pa):
        self.inputs = [pa]
        self.out_shapes = [jax.ShapeDtypeStruct(pa.shape, pa.dtype)]
        self.scratch = [pltpu.SemaphoreType.DMA((3,)), pltpu.SemaphoreType.DMA((3,)), pltpu.SemaphoreType.DMA]

    @staticmethod
    def _mine(p_ref, out_ref, send_sems, recv_sems, local_sem):
        x, y, _ = _place()
        return pltpu.make_async_copy(p_ref.at[2 * x + y], out_ref.at[2 * x + y], local_sem)

    @staticmethod
    def _remote(p_ref, out_ref, send_sems, recv_sems, local_sem, incoming):
        x, y, c = _place()
        me = 2 * x + y
        remote = []
        for j, (px, py) in enumerate([(1 - x, y), (x, 1 - y), (1 - x, 1 - y)]):
            remote.append(pltpu.make_async_remote_copy(
                src_ref=p_ref.at[me if incoming else 2 * px + py], dst_ref=out_ref.at[2 * px + py if incoming else me],
                send_sem=send_sems.at[j], recv_sem=recv_sems.at[j], device_id=(px, py, c), device_id_type=MESH))
        return remote

    def start(self, *refs):
        self._mine(*refs).start()
        for cp in self._remote(*refs, incoming=False):
            cp.start()

    def wait(self, *refs):
        for cp in self._remote(*refs, incoming=True):
            cp.wait_recv()
        for cp in self._remote(*refs, incoming=False):
            cp.wait_send()
        self._mine(*refs).wait()


def _fused_specs(fused):
    if fused is None:
        return [], [], [], [], []
    return (list(fused.inputs), [ANY] * len(fused.inputs), list(fused.out_shapes), [ANY] * len(fused.out_shapes),
            list(fused.scratch))


def _fused_aliases(fused, first_input, first_output):
    aliases = getattr(fused, "aliases", {}) if fused is not None else {}
    return {first_input + i: first_output + o for i, o in aliases.items()}


def _fused_begin(fused, grid, refs):
    if fused is not None:
        first = functools.reduce(lambda p, q: p & q, [pl.program_id(i) == 0 for i in range(len(grid))])
        pl.when(first)(lambda: fused.start(*refs))
        if hasattr(fused, "middle"):
            step = functools.reduce(lambda acc, ig: acc * ig[1] + pl.program_id(ig[0]), enumerate(grid), 0)
            pl.when(step == math.prod(grid) // 2)(lambda: fused.middle(*refs))


def _fused_end(fused, grid, refs):
    if fused is not None:
        last = functools.reduce(lambda p, q: p & q, [pl.program_id(i) == g - 1 for i, g in enumerate(grid)])
        pl.when(last)(lambda: fused.wait(*refs))


def _sibling_join(full, name):
    h2, cdim = full.shape
    h = h2 // 2

    def body(in_ref, out_ref, send_sem, recv_sem):
        del in_ref
        x, y, c = _place()
        mine = out_ref.at[pl.ds(c * h, h), :]
        cp = pltpu.make_async_remote_copy(
            src_ref=mine, dst_ref=mine, send_sem=send_sem, recv_sem=recv_sem,
            device_id=(x, y, 1 - c), device_id_type=MESH)
        cp.start()
        theirs = out_ref.at[pl.ds((1 - c) * h, h), :]
        pltpu.make_async_remote_copy(
            src_ref=theirs, dst_ref=theirs, send_sem=send_sem, recv_sem=recv_sem,
            device_id=(x, y, 1 - c), device_id_type=MESH).wait_recv()
        cp.wait_send()

    return pl.pallas_call(
        body, name=name,
        out_shape=jax.ShapeDtypeStruct(full.shape, full.dtype),
        in_specs=[ANY], out_specs=ANY, input_output_aliases={0: 0},
        scratch_shapes=[pltpu.SemaphoreType.DMA, pltpu.SemaphoreType.DMA],
    )(full)


def _cast_bf16_slab(w, name, with_own=False):
    r, cdim = w.shape
    tr, tc = _tile(r, 512), _tile(cdim, 2048)

    def body(pc_ref, w_ref, o_ref, *own_ref):
        o_ref[...] = w_ref[...].astype(BF16)
        for ref in own_ref:
            ref[...] = w_ref[...].astype(BF16)

    plain = pl.BlockSpec((tr, tc), lambda i, j, pc: (i, j))
    return pl.pallas_call(
        body, name=name,
        grid_spec=pltpu.PrefetchScalarGridSpec(
            num_scalar_prefetch=1, grid=(r // tr, cdim // tc),
            in_specs=[plain],
            out_specs=[pl.BlockSpec((None, tr, tc), lambda i, j, pc: (pc[1], i, j))] + [plain] * with_own),
        out_shape=[jax.ShapeDtypeStruct((N_CHIPS, r, cdim), BF16)] + [jax.ShapeDtypeStruct((r, cdim), BF16)] * with_own,
        compiler_params=_cparams(("parallel", "parallel")),
    )(_place_scalars(), w)


def _place_scalars():
    x, y, c = _place()
    return jnp.stack([c, 2 * x + y]).astype(jnp.int32)


def _pair_sum(gs, ra, name):
    n, r, cdim = gs.shape
    half = r // 2
    tr, tc = _tile(half, 512), _tile(cdim, 2048)
    nt = half // tr

    def body(pc_ref, g_ref, r_ref, o_ref, own_ref):
        val = g_ref[...].astype(F32) + r_ref[...].astype(F32)
        o_ref[...] = val.astype(BF16)

        @pl.when(pl.program_id(2) == pc_ref[1])
        def _():
            own_ref[...] = val

    return pl.pallas_call(
        body, name=name,
        grid_spec=pltpu.PrefetchScalarGridSpec(
            num_scalar_prefetch=1, grid=(nt, cdim // tc, n),
            in_specs=[pl.BlockSpec((None, tr, tc), lambda i, j, s, pc: (s, pc[0] * nt + i, j)),
                      pl.BlockSpec((None, tr, tc), lambda i, j, s, pc: (s, i, j))],
            out_specs=[pl.BlockSpec((None, tr, tc), lambda i, j, s, pc: (s, i, j)),
                       pl.BlockSpec((tr, tc), lambda i, j, s, pc: (i, j))]),
        out_shape=[jax.ShapeDtypeStruct((n, half, cdim), BF16), jax.ShapeDtypeStruct((half, cdim), F32)],
        compiler_params=_cparams(("parallel", "parallel", "arbitrary")),
    )(_place_scalars(), gs, ra)


def _chip_sum(rb, own, name):
    n, h, cdim = rb.shape
    tr, tc = _tile(h, 256), _tile(cdim, 2048)
    nt = h // tr

    def body(pc_ref, r_ref, own_ref, o_ref):
        chip = pc_ref[1]
        acc = None
        for p in range(n):
            term = jnp.where(chip == p, own_ref[...], r_ref[p].astype(F32))
            acc = term if acc is None else acc + term
        o_ref[...] = acc

    return pl.pallas_call(
        body, name=name,
        grid_spec=pltpu.PrefetchScalarGridSpec(
            num_scalar_prefetch=1, grid=(nt, cdim // tc),
            in_specs=[pl.BlockSpec((n, tr, tc), lambda i, j, pc: (0, i, j)),
                      pl.BlockSpec((tr, tc), lambda i, j, pc: (i, j))],
            out_specs=pl.BlockSpec((tr, tc), lambda i, j, pc: (pc[0] * nt + i, j))),
        out_shape=jax.ShapeDtypeStruct((2 * h, cdim), F32),
        compiler_params=_cparams(("parallel", "parallel")),
    )(_place_scalars(), rb, own)


def _adamw_math(w, g, m, v):
    m = ADAM_B1 * m + (1.0 - ADAM_B1) * g
    v = ADAM_B2 * v + (1.0 - ADAM_B2) * (g * g)
    m_hat = m / (1.0 - ADAM_B1 ** ADAM_STEP)
    v_hat = v / (1.0 - ADAM_B2 ** ADAM_STEP)
    delta = -ADAM_LR * (m_hat / (jnp.sqrt(v_hat) + ADAM_EPS) + ADAM_WD * w)
    return delta, m, v


def _adamw(w, g, m, v, name):
    r, cdim = w.shape
    tr, tc = _tile(r, 256), _tile(cdim, 2048)

    def body(w_ref, g_ref, m_ref, v_ref, go_ref, d_ref, nm_ref, nv_ref):
        gv = g_ref[...]
        d, nm, nv = _adamw_math(w_ref[...], gv, m_ref[...], v_ref[...])
        go_ref[...] = gv
        d_ref[...] = d
        nm_ref[...] = nm
        nv_ref[...] = nv

    spec = pl.BlockSpec((tr, tc), lambda i, j: (i, j))
    sds = jax.ShapeDtypeStruct((r, cdim), F32)
    return pl.pallas_call(
        body, name=name, grid=(r // tr, cdim // tc),
        in_specs=[spec] * 4, out_specs=[spec] * 4, out_shape=[sds] * 4,
        compiler_params=_cparams(("parallel", "parallel")),
    )(w, g, m, v)


def _matmul(a, b, *, grid, a_spec, b_spec, out_spec, out_shape, acc_shape, dims, name, bias=None, bias_spec=None,
            silu_a=False, fused=None):
    nk = grid[2]
    f_in, f_in_specs, f_out, f_out_specs, f_scratch = _fused_specs(fused)
    n_in = 2 + (bias is not None)

    acc_scratch = [pltpu.VMEM(acc_shape, F32)] if nk > 1 else []

    def body(*refs):
        a_ref, b_ref = refs[:2]
        bias_ref = refs[2] if bias is not None else None
        o_ref = refs[n_in + len(f_in)]
        n_fixed = n_in + len(f_in) + 1 + len(f_out)
        f_refs = (*refs[n_in:n_in + len(f_in)], *refs[n_in + len(f_in) + 1:n_fixed],
                  *refs[n_fixed + len(acc_scratch):])
        _fused_begin(fused, grid, f_refs)

        def product():
            if len(a_ref.shape) == 3:
                tks = a_ref.shape[2]
                parts = [_dot(a_ref[i], b_ref[:, i * tks:(i + 1) * tks], dims) for i in range(a_ref.shape[0])]
                return functools.reduce(lambda p, q: p + q, parts)
            av = a_ref[...]
            if silu_a:
                av = av * _sigmoid(av)
            return _dot(av.astype(BF16), b_ref[...].astype(BF16), dims)

        def finish(res):
            if bias is not None:
                res = res + bias_ref[...]
            o_ref[...] = res.astype(o_ref.dtype)

        if nk == 1:
            finish(product())
        else:
            acc_ref = refs[n_fixed]
            k = pl.program_id(2)

            @pl.when(k == 0)
            def _():
                acc_ref[...] = product()

            if nk > 2:
                @pl.when((k > 0) & (k < nk - 1))
                def _():
                    acc_ref[...] += product()

            @pl.when(k == nk - 1)
            def _():
                finish(acc_ref[...] + product())

        _fused_end(fused, grid, f_refs)

    in_specs = [a_spec, b_spec] + ([] if bias is None else [bias_spec]) + f_in_specs
    args = (a, b) + (() if bias is None else (bias,)) + tuple(f_in)
    sem = ("parallel", "parallel", "arbitrary") if fused is None else ("arbitrary",) * 3
    res = pl.pallas_call(
        body, name=name, grid=grid, in_specs=in_specs, out_specs=[out_spec] + f_out_specs,
        out_shape=[out_shape] + f_out,
        scratch_shapes=acc_scratch + f_scratch,
        input_output_aliases=_fused_aliases(fused, n_in, 1),
        compiler_params=_cparams(sem),
    )(*args)
    return res[0] if fused is None else tuple(res)


def _mm_tiles(m, n, k):
    return _tile(m, 1024), _tile(n, 1024), _tile(k, 4096)


def _proj_part(h2, wmat, n_seg, w_block, seg_of, w, name, carry=None, fused=None):
    t, d = h2.shape
    tm, tn, _ = _mm_tiles(t, w, d)
    npseg = w // tn
    grid = (t // tm, n_seg * npseg)
    f_in, f_in_specs, f_out, f_out_specs, f_scratch = _fused_specs(fused)
    n_carry = carry is not None

    def body(place_ref, a_ref, b_ref, *refs):
        del place_ref
        o_ref = refs[n_carry + len(f_in)]
        f_refs = (*refs[n_carry:n_carry + len(f_in)], *refs[n_carry + len(f_in) + 1:])
        _fused_begin(fused, grid, f_refs)
        o_ref[...] = _dot(a_ref[...], b_ref[...], NN)
        _fused_end(fused, grid, f_refs)

    w_spec = pl.BlockSpec((d, tn) if wmat.ndim == 2 else (None, d, tn),
                          lambda m, n, place: w_block(n // npseg, n % npseg, place))
    res = pl.pallas_call(
        body, name=name,
        grid_spec=pltpu.PrefetchScalarGridSpec(
            num_scalar_prefetch=1, grid=grid,
            in_specs=[pl.BlockSpec((tm, d), lambda m, n, place: (m, 0)), w_spec] + [ANY] * n_carry + f_in_specs,
            out_specs=[pl.BlockSpec((None, tm, tn), lambda m, n, place: (seg_of(n // npseg, place), m, n % npseg))]
            + f_out_specs,
            scratch_shapes=f_scratch),
        out_shape=[jax.ShapeDtypeStruct((8, t, w), F32)] + f_out,
        input_output_aliases={**({3: 0} if n_carry else {}), **_fused_aliases(fused, 3 + n_carry, 1)},
        compiler_params=_cparams(("arbitrary", "arbitrary")),
    )(_place_scalars(), h2, wmat, *([carry] if n_carry else []), *f_in)
    return tuple(res)


def _proj_fwd_split(h2, w_own, gather_in, gather_out):
    w = w_own.shape[1] // 2
    npseg = w // _mm_tiles(h2.shape[0], w, h2.shape[1])[1]
    proj8, ws_in = _proj_part(
        h2, w_own, 2, lambda j, i, place: (0, j * npseg + i), lambda j, place: 2 * place[1] + j, w,
        "proj_fwd_own", fused=gather_in)

    def shard(j, place):
        return (place[1] + 1 + j // 2) % N_CHIPS

    proj8, ws_out = _proj_part(
        h2, ws_in, 6, lambda j, i, place: (shard(j, place), 0, (j % 2) * npseg + i),
        lambda j, place: 2 * shard(j, place) + j % 2, w, "proj_fwd_rest", carry=proj8, fused=gather_out)
    return proj8, ws_in, ws_out


def _proj_bwd_x(dproj8, ws_in, fused=None):
    _, t, w = dproj8.shape
    _, d, cs = ws_in.shape
    tm, tn, _ = _mm_tiles(t, d, w)
    return _matmul(
        dproj8, ws_in, grid=(t // tm, d // tn, N_CHIPS), dims=NT, name="proj_bwd_x", fused=fused,
        a_spec=pl.BlockSpec((2, tm, w), lambda m, n, k: (k, m, 0)),
        b_spec=pl.BlockSpec((None, tn, cs), lambda m, n, k: (k, n, 0)),
        out_spec=pl.BlockSpec((tm, tn), lambda m, n, k: (m, n)),
        out_shape=jax.ShapeDtypeStruct((t, d), F32), acc_shape=(tm, tn))


def _proj_bwd_w(h2, dproj8):
    t, d = h2.shape
    _, _, w = dproj8.shape
    cs = 2 * w
    tm, tn, tk = _mm_tiles(d, w, t)
    nps, npseg = cs // tn, w // tn
    return _matmul(
        h2, dproj8, grid=(d // tm, 8 * npseg, t // tk), dims=TN, name="proj_bwd_w",
        a_spec=pl.BlockSpec((tk, tm), lambda m, n, k: (k, m)),
        b_spec=pl.BlockSpec((None, tk, tn), lambda m, n, k: (n // npseg, k, n % npseg)),
        out_spec=pl.BlockSpec((None, tm, tn), lambda m, n, k: (n // nps, m, n % nps)),
        out_shape=jax.ShapeDtypeStruct((N_CHIPS, d, cs), BF16), acc_shape=(tm, tn))


def _out_fwd(y2, w_out):
    _, t, w = y2.shape
    _, d = w_out.shape
    tm, tn, tk = _mm_tiles(t, d, w)
    kpg = w // tk
    return _matmul(
        y2, w_out, grid=(t // tm, d // tn, 2 * kpg), dims=NN, name="out_fwd",
        a_spec=pl.BlockSpec((None, tm, tk), lambda m, n, k: (k // kpg, m, k % kpg)),
        b_spec=pl.BlockSpec((tk, tn), lambda m, n, k: (k, n)),
        out_spec=pl.BlockSpec((tm, tn), lambda m, n, k: (m, n)),
        out_shape=jax.ShapeDtypeStruct((t, d), F32), acc_shape=(tm, tn))


def _out_bwd_y(dout, w_out):
    t, d = dout.shape
    w = w_out.shape[0] // 2
    tm, tn, tk = _mm_tiles(t, w, d)
    npg = w // tn
    return _matmul(
        dout, w_out, grid=(t // tm, 2 * npg, d // tk), dims=NT, name="out_bwd_y",
        a_spec=pl.BlockSpec((tm, tk), lambda m, n, k: (m, k)),
        b_spec=pl.BlockSpec((tn, tk), lambda m, n, k: (n, k)),
        out_spec=pl.BlockSpec((None, tm, tn), lambda m, n, k: (n // npg, m, n % npg)),
        out_shape=jax.ShapeDtypeStruct((2, t, w), F32), acc_shape=(tm, tn))


def _out_bwd_w(y2, dout):
    _, t, w = y2.shape
    _, d = dout.shape
    tm, tn, tk = _mm_tiles(w, d, t)
    mpg = w // tm
    return _matmul(
        y2, dout, grid=(2 * mpg, d // tn, t // tk), dims=TN, name="out_bwd_w",
        a_spec=pl.BlockSpec((None, tk, tm), lambda m, n, k: (m // mpg, k, m % mpg)),
        b_spec=pl.BlockSpec((tk, tn), lambda m, n, k: (k, n)),
        out_spec=pl.BlockSpec((tm, tn), lambda m, n, k: (m, n)),
        out_shape=jax.ShapeDtypeStruct((2 * w, d), BF16), acc_shape=(tm, tn))


def _mod_fwd(c_all, w_ada, b_ada):
    bt, d = c_all.shape
    _, n = w_ada.shape
    tn, tk = _tile(n, 512), _tile(d, 1024)
    return _matmul(
        c_all, w_ada, grid=(1, n // tn, d // tk), dims=NN, name="mod_fwd", silu_a=True,
        a_spec=pl.BlockSpec((bt, tk), lambda i, j, l: (0, l)),
        b_spec=pl.BlockSpec((tk, tn), lambda i, j, l: (l, j)),
        bias=b_ada, bias_spec=pl.BlockSpec((1, tn), lambda i, j, l: (0, j)),
        out_spec=pl.BlockSpec((bt, tn), lambda i, j, l: (0, j)),
        out_shape=jax.ShapeDtypeStruct((bt, n), F32), acc_shape=(bt, tn))


def _norm_mod_fwd(x, g_norm, scale, shift):
    b, s, d = x.shape
    ts = _tile(s, 256)

    def body(x_ref, g_ref, sc_ref, sh_ref, h_ref):
        xv = x_ref[...]
        r = lax.rsqrt(jnp.mean(xv * xv, axis=-1, keepdims=True) + EPS)
        y = (xv * r) * g_ref[...]
        h_ref[...] = (y * (1.0 + sc_ref[...]) + sh_ref[...]).astype(BF16)

    row = pl.BlockSpec((None, ts, d), lambda i, j: (i, j, 0))
    per_b = pl.BlockSpec((None, 1, d), lambda i, j: (i, 0, 0))
    return pl.pallas_call(
        body, name="norm_mod_fwd", grid=(b, s // ts),
        in_specs=[row, pl.BlockSpec((1, d), lambda i, j: (0, 0)), per_b, per_b],
        out_specs=row, out_shape=jax.ShapeDtypeStruct((b, s, d), BF16),
        compiler_params=_cparams(("parallel", "parallel")),
    )(x, g_norm, scale, shift)


def _norm_mod_bwd(x, dh, dx1, g_norm, scale):
    b, s, d = x.shape
    ts = _tile(s, 256)

    def body(x_ref, dh_ref, dx1_ref, g_ref, sc_ref, gx_ref, dsh_ref, dsc_ref, dg_ref):
        i, j = pl.program_id(0), pl.program_id(1)

        @pl.when(j == 0)
        def _():
            dsh_ref[...] = jnp.zeros_like(dsh_ref)
            dsc_ref[...] = jnp.zeros_like(dsc_ref)

        @pl.when((i == 0) & (j == 0))
        def _():
            dg_ref[...] = jnp.zeros_like(dg_ref)

        xv, dhv, g = x_ref[...], dh_ref[...], g_ref[...]
        r = lax.rsqrt(jnp.mean(xv * xv, axis=-1, keepdims=True) + EPS)
        xh = xv * r
        dsh_ref[...] += jnp.sum(dhv, axis=0, keepdims=True)
        dsc_ref[...] += jnp.sum(dhv * (xh * g), axis=0, keepdims=True)
        dn = dhv * (1.0 + sc_ref[...])
        dg_ref[...] += jnp.sum(dn * xh, axis=0, keepdims=True)
        u = dn * g
        dx = r * u - xv * (r * r * r) * jnp.mean(u * xv, axis=-1, keepdims=True)
        gx_ref[...] = dx1_ref[...] + dx

    row = pl.BlockSpec((None, ts, d), lambda i, j: (i, j, 0))
    per_b = pl.BlockSpec((None, 1, d), lambda i, j: (i, 0, 0))
    vec = pl.BlockSpec((1, d), lambda i, j: (0, 0))
    return pl.pallas_call(
        body, name="norm_mod_bwd", grid=(b, s // ts),
        in_specs=[row, row, row, vec, per_b],
        out_specs=[row, per_b, per_b, vec],
        out_shape=[jax.ShapeDtypeStruct((b, s, d), F32), jax.ShapeDtypeStruct((b, 1, d), F32),
                   jax.ShapeDtypeStruct((b, 1, d), F32), jax.ShapeDtypeStruct((1, d), F32)],
        compiler_params=_cparams(("arbitrary", "arbitrary")),
    )(x, dh, dx1, g_norm, scale)


def _loss_head(x, out, gate, g_final, target):
    b, s, d = x.shape
    ts = _tile(s, 256)

    def body(x_ref, o_ref, gt_ref, g_ref, t_ref, dx1_ref, dout_ref, dgt_ref, dg_ref, loss_ref):
        i, j = pl.program_id(0), pl.program_id(1)

        @pl.when(j == 0)
        def _():
            dgt_ref[...] = jnp.zeros_like(dgt_ref)

        @pl.when((i == 0) & (j == 0))
        def _():
            dg_ref[...] = jnp.zeros_like(dg_ref)
            loss_ref[...] = jnp.zeros_like(loss_ref)

        ov, gt, g = o_ref[...], gt_ref[...], g_ref[...]
        x1 = x_ref[...] + gt * ov
        r = lax.rsqrt(jnp.mean(x1 * x1, axis=-1, keepdims=True) + EPS)
        xh = x1 * r
        err = xh * g - t_ref[...]
        loss_ref[...] += 0.5 * jnp.sum(jnp.mean(err * err, axis=-1, keepdims=True))
        dfin = err * (1.0 / d)
        dg_ref[...] += jnp.sum(dfin * xh, axis=0, keepdims=True)
        u = dfin * g
        dx1 = r * u - x1 * (r * r * r) * jnp.mean(u * x1, axis=-1, keepdims=True)
        dx1_ref[...] = dx1
        dgt_ref[...] += jnp.sum(dx1 * ov, axis=0, keepdims=True)
        dout_ref[...] = (gt * dx1).astype(BF16)

    row = pl.BlockSpec((None, ts, d), lambda i, j: (i, j, 0))
    per_b = pl.BlockSpec((None, 1, d), lambda i, j: (i, 0, 0))
    vec = pl.BlockSpec((1, d), lambda i, j: (0, 0))
    return pl.pallas_call(
        body, name="loss_head", grid=(b, s // ts),
        in_specs=[row, row, per_b, vec, row],
        out_specs=[row, row, per_b, vec, pl.BlockSpec((1, 128), lambda i, j: (0, 0))],
        out_shape=[jax.ShapeDtypeStruct((b, s, d), F32), jax.ShapeDtypeStruct((b, s, d), BF16),
                   jax.ShapeDtypeStruct((b, 1, d), F32), jax.ShapeDtypeStruct((1, d), F32),
                   jax.ShapeDtypeStruct((1, 128), F32)],
        compiler_params=_cparams(("arbitrary", "arbitrary")),
    )(x, out, gate, g_final, target)


def _head_out(o, zg, g):
    rinv = lax.rsqrt(jnp.mean(o * o, axis=-1, keepdims=True) + EPS)
    return ((o * rinv) * g) * (zg * _sigmoid(zg))


def _head_out_bwd(o, zg, g, dy):
    rinv = lax.rsqrt(jnp.mean(o * o, axis=-1, keepdims=True) + EPS)
    rn = o * rinv
    sg = _sigmoid(zg)
    sil = zg * sg
    dzg = dy * (rn * g) * (sg * (1.0 + zg * (1.0 - sg)))
    dg = jnp.sum(dy * rn * sil, axis=0, keepdims=True)
    drn = dy * g * sil
    do = rinv * drn - o * (rinv * rinv * rinv) * jnp.mean(drn * o, axis=-1, keepdims=True)
    return do, dzg, dg


def _head_spec(s):
    return pl.BlockSpec((None, s, HEAD_DIM), lambda b, h: (b, 0, h))


def _seg_spec(s, seg):
    return pl.BlockSpec((None, None, s, HEAD_DIM), lambda b, h: (seg, b, 0, h))


def _seg4_spec(s, group):
    return pl.BlockSpec((4, None, s, HEAD_DIM), lambda b, h: (group, b, 0, h))


SB_Q_BLOCK = 512
SB_K_BLOCK = 256


SB_DEAD_LOG2 = -160.0
LOG2_E = 1.4426950408889634
SB_LOGIT_SCALE = LOG2_E / math.sqrt(HEAD_DIM)


def _sb_terms(raw, valid):
    t = jnp.where(valid, raw * SB_LOGIT_SCALE, NEG_BIG)
    e = jnp.exp2(-jnp.abs(t))
    l1m = -(jnp.maximum(t, 0.0) + jnp.log2(1.0 + e))
    return t, l1m, e


def _split_dot(a, u):
    hi = a.astype(BF16)
    lo = (a - hi.astype(F32)).astype(BF16)
    return _dot(hi, u, NN) + _dot(lo, u, NN)


def _sb_fwd(proj8, g_sb):
    _, b, s, w = proj8.shape
    n_heads = w // HEAD_DIM
    tq, tk = _tile(s, SB_Q_BLOCK), _tile(s, SB_K_BLOCK)
    nq, kpq = s // tq, tq // tk
    scale = 1.0 / math.sqrt(HEAD_DIM)

    def body(q_ref, k_ref, v_ref, zg_ref, g_ref, o_ref, tot_ref, y_ref, trips_ref):
        u_excl = (lax.broadcasted_iota(jnp.int32, (tk, tk), 0)
                  > lax.broadcasted_iota(jnp.int32, (tk, tk), 1)).astype(BF16)
        ahead = lax.broadcasted_iota(jnp.int32, (tq, tk), 0) - lax.broadcasted_iota(jnp.int32, (tq, tk), 1)
        g = g_ref[...]

        def qblock(i, _):
            rows = pl.ds(pl.multiple_of(i * tq, tq), tq)
            q = q_ref[rows, :].astype(BF16)
            nk = (i + 1) * kpq

            def alive(state):
                jj, _, csum = state
                return (jj <= i) & ((jj == 0) | (jnp.max(csum) > SB_DEAD_LOG2))

            def kblocks(state):
                jj, acc, csum = state
                js = [nk - 1 - (jj * kpq + n) for n in range(kpq)]
                cols = [pl.ds(pl.multiple_of(j * tk, tk), tk) for j in js]
                raw = [_dot(q, k_ref[c, :].astype(BF16), NT) for c in cols]
                terms = [_sb_terms(x, ahead > j * tk - i * tq) for x, j in zip(raw, js)]
                sums = [_split_dot(l1m, u_excl) for _, l1m, _ in terms]
                for (t, l1m, _), part, c in zip(terms, sums, cols):
                    a = jnp.exp2((t + l1m) + (part + csum))
                    acc = acc + _dot(a.astype(BF16), v_ref[c, :].astype(BF16), NN)
                    csum = csum + jnp.sum(l1m, axis=1, keepdims=True)
                return jj + 1, acc, csum

            trips, acc, tot = lax.while_loop(
                alive, kblocks, (jnp.int32(0), jnp.zeros((tq, HEAD_DIM), F32), jnp.zeros((tq, 1), F32)))
            o_ref[rows, :] = acc
            tot_ref[rows, :] = jnp.broadcast_to(tot, (tq, HEAD_DIM))
            y_ref[rows, :] = _head_out(acc, zg_ref[rows, :], g).astype(BF16)
            trips_ref[0, i] = trips.astype(F32)
            return 0

        lax.fori_loop(0, nq, qblock, 0)

    return pl.pallas_call(
        body, name="sb_fwd", grid=(b, n_heads),
        in_specs=[_seg_spec(s, 0), _seg_spec(s, 1), _seg_spec(s, 2), _seg_spec(s, 3),
                  pl.BlockSpec((1, HEAD_DIM), lambda i, h: (0, h))],
        out_specs=[_head_spec(s), _head_spec(s), _seg_spec(s, 0),
                   pl.BlockSpec((None, None, 1, nq), lambda i, h: (i, h, 0, 0), memory_space=pltpu.SMEM)],
        out_shape=[jax.ShapeDtypeStruct((b, s, w), F32), jax.ShapeDtypeStruct((b, s, w), F32),
                   jax.ShapeDtypeStruct((2, b, s, w), BF16), jax.ShapeDtypeStruct((b, n_heads, 1, nq), F32)],
        compiler_params=_cparams(("parallel", "parallel")),
    )(proj8, proj8, proj8, proj8, g_sb)


def _sb_bwd(proj8, o_sb, tot_sb, trips, dy2, g_sb, fused=None):
    _, b, s, w = proj8.shape
    n_heads = w // HEAD_DIM
    tq, tk = _tile(s, SB_Q_BLOCK), _tile(s, SB_K_BLOCK)
    nq, kpq = s // tq, tq // tk
    scale = 1.0 / math.sqrt(HEAD_DIM)

    f_in, f_in_specs, f_out, f_out_specs, f_scratch = _fused_specs(fused)
    grid = (b, n_heads)

    def body(*refs):
        q_ref, k_ref, v_ref, zg_ref, o_ref, tot_ref, dy_ref, g_ref, trips_ref = refs[:9]
        dp_ref, dg_ref = refs[9 + len(f_in):11 + len(f_in)]
        do_s, dk_s, dv_s = refs[11 + len(f_in) + len(f_out):14 + len(f_in) + len(f_out)]
        f_refs = (*refs[9:9 + len(f_in)], *refs[11 + len(f_in):11 + len(f_in) + len(f_out)],
                  *refs[14 + len(f_in) + len(f_out):])
        _fused_begin(fused, grid, f_refs)
        dq_ref, dk_ref, dv_ref, dzg_ref = (dp_ref.at[n] for n in range(4))
        ri = lax.broadcasted_iota(jnp.int32, (tk, tk), 0)
        ci = lax.broadcasted_iota(jnp.int32, (tk, tk), 1)
        u_le = (ri <= ci).astype(BF16)
        u_lt = (ri < ci).astype(BF16)
        ahead = lax.broadcasted_iota(jnp.int32, (tq, tk), 0) - lax.broadcasted_iota(jnp.int32, (tq, tk), 1)
        g = g_ref[...]

        def prologue(i, dg):
            rows = pl.ds(pl.multiple_of(i * tq, tq), tq)
            do, dzg, dgi = _head_out_bwd(o_ref[rows, :], zg_ref[rows, :], g, dy_ref[rows, :])
            dzg_ref[rows, :] = dzg.astype(BF16)
            do_s[rows, :] = do.astype(BF16)
            return dg + dgi

        dg_ref[...] = lax.fori_loop(0, nq, prologue, jnp.zeros((1, HEAD_DIM), F32))
        dk_s[...] = jnp.zeros_like(dk_s)
        dv_s[...] = jnp.zeros_like(dv_s)

        def qblock(i, _):
            rows = pl.ds(pl.multiple_of(i * tq, tq), tq)
            q = q_ref[rows, :].astype(BF16)
            do = do_s[rows, :]
            tot = tot_ref[rows, :][:, :1]

            def kblocks(jj, carry):
                dq, pre_l, pre_g = carry
                js = [jj * kpq + n for n in range(kpq)]
                cols = [pl.ds(pl.multiple_of(j * tk, tk), tk) for j in js]
                ks = [k_ref[c, :].astype(BF16) for c in cols]
                raw = [_dot(q, k, NT) for k in ks]
                da = [_dot(do, v_ref[c, :].astype(BF16), NT) for c in cols]
                terms = [_sb_terms(x, ahead > j * tk - i * tq) for x, j in zip(raw, js)]
                sums_l = [_split_dot(l1m, u_le) for _, l1m, _ in terms]
                a, gg = [], []
                for (t, l1m, _), part, d in zip(terms, sums_l, da):
                    a.append(jnp.exp2((t + l1m) + (tot - (part + pre_l))))
                    gg.append(a[-1] * d)
                    pre_l = pre_l + jnp.sum(l1m, axis=1, keepdims=True)
                sums_g = [_split_dot(x, u_lt) for x in gg]
                dzs = []
                for (t, _, e), x, part in zip(terms, gg, sums_g):
                    big_g = part + pre_g
                    pre_g = pre_g + jnp.sum(x, axis=1, keepdims=True)
                    inv = 1.0 / (1.0 + e)
                    sig = jnp.where(t >= 0.0, inv, e * inv)
                    dzs.append(((x - sig * (x + big_g)) * scale).astype(BF16))
                for x, k in zip(dzs, ks):
                    dq = dq + _dot(x, k, NN)
                for x, y, c in zip(dzs, a, cols):
                    dk_s[c, :] += _dot(x, q, TN)
                    dv_s[c, :] += _dot(y.astype(BF16), do, TN)
                return dq, pre_l, pre_g

            zero = jnp.zeros((tq, 1), F32)
            walked = jnp.clip(trips_ref[0, i].astype(jnp.int32), 1, i + 1)
            dq, _, _ = lax.fori_loop(i + 1 - walked, i + 1, kblocks, (jnp.zeros((tq, HEAD_DIM), F32), zero, zero))
            dq_ref[rows, :] = dq.astype(BF16)
            return 0

        lax.fori_loop(0, nq, qblock, 0)
        dk_ref[...] = dk_s[...].astype(BF16)
        dv_ref[...] = dv_s[...].astype(BF16)
        _fused_end(fused, grid, f_refs)

    return pl.pallas_call(
        body, name="sb_bwd", grid=grid,
        in_specs=[_seg_spec(s, 0), _seg_spec(s, 1), _seg_spec(s, 2), _seg_spec(s, 3),
                  _head_spec(s), _head_spec(s), _seg_spec(s, 0),
                  pl.BlockSpec((1, HEAD_DIM), lambda i, h: (0, h)),
                  pl.BlockSpec((None, None, 1, nq), lambda i, h: (i, h, 0, 0), memory_space=pltpu.SMEM)] + f_in_specs,
        out_specs=[_seg4_spec(s, 0), pl.BlockSpec((None, 1, HEAD_DIM), lambda i, h: (i, 0, h))] + f_out_specs,
        out_shape=[jax.ShapeDtypeStruct((8, b, s, w), BF16), jax.ShapeDtypeStruct((b, 1, w), F32)] + f_out,
        scratch_shapes=[pltpu.VMEM((s, HEAD_DIM), BF16), pltpu.VMEM((s, HEAD_DIM), F32),
                        pltpu.VMEM((s, HEAD_DIM), F32)] + f_scratch,
        compiler_params=_cparams(("arbitrary", "arbitrary")),
    )(proj8, proj8, proj8, proj8, o_sb, tot_sb, dy2, g_sb, trips, *f_in)


DIL_BLOCK = 128
DIL_GROUP = 8


def _dil_chunks(s, r):
    length = s // r
    out = []
    for rho in range(r):
        for cc in range(length // DIL_BLOCK):
            if r == 1:
                nat = pl.ds(cc * DIL_BLOCK, DIL_BLOCK)
            else:
                nat = pl.ds(rho + r * DIL_BLOCK * cc, DIL_BLOCK, stride=r)
            off = rho * length + cc * DIL_BLOCK
            out.append((nat, pl.ds(off, DIL_BLOCK), pl.ds(off + DIL_BLOCK, DIL_BLOCK)))
    return out


def _dil_masks(slope, r):
    n = DIL_BLOCK
    ri = lax.broadcasted_iota(jnp.int32, (n, 2 * n), 0)
    ci = lax.broadcasted_iota(jnp.int32, (n, 2 * n), 1)
    steps = ri - ci + n
    inside = (steps >= 0) & (steps <= n)
    bias = slope * (steps.astype(F32) * r)
    return jnp.where(inside, -bias, NEG_BIG), jnp.where(inside & (ci >= n), -bias, NEG_BIG)


def _dil_scores(q, k_pc, masks, first):
    return _dot(q, k_pc, NT) * (1.0 / math.sqrt(HEAD_DIM)) + jnp.where(first, masks[1], masks[0])


def _dil_check(s):
    assert (s // DIL_BLOCK) % DIL_GROUP == 0, s
    for window, r in DIL_PAIRS:
        assert window // r == DIL_BLOCK and s % (r * DIL_BLOCK) == 0, (s, window, r)


def _dil_fwd(proj8, g_dil, slopes, y2):
    _, b, s, w = proj8.shape
    n_heads = w // HEAD_DIM
    _dil_check(s)
    n = DIL_BLOCK
    nt = s // n

    def body(q_ref, k_ref, v_ref, zg_ref, g_ref, sl_ref, y_in, o_ref, lse_ref, y_ref,
             qp, kp, vp, pnum, pm, pl_, acc_s, m_s, l_s):
        del y_in
        slope = sl_ref[...][:, :1]
        kp[pl.ds(0, n), :] = jnp.zeros((n, HEAD_DIM), BF16)
        vp[pl.ds(0, n), :] = jnp.zeros((n, HEAD_DIM), BF16)

        for (window, r) in DIL_PAIRS:
            nb = (s // r) // n
            masks = _dil_masks(slope, float(r))
            for nat, per, padded in _dil_chunks(s, r):
                qp[per, :] = q_ref[nat, :].astype(BF16)
                kp[padded, :] = k_ref[nat, :].astype(BF16)
                vp[padded, :] = v_ref[nat, :].astype(BF16)
            num_t, m_t, l_t = (acc_s, m_s, l_s) if r == 1 else (pnum, pm, pl_)

            def tiles(tt, _):
                ts = [tt * DIL_GROUP + i for i in range(DIL_GROUP)]
                rows = [pl.ds(pl.multiple_of(t * n, n), n) for t in ts]
                both = [pl.ds(pl.multiple_of(t * n, n), 2 * n) for t in ts]
                sc = [_dil_scores(qp[rw, :], kp[bt, :], masks, lax.rem(t, nb) == 0)
                      for t, rw, bt in zip(ts, rows, both)]
                m = [jnp.max(x, axis=1, keepdims=True) for x in sc]
                p = [jnp.exp(x - mx) for x, mx in zip(sc, m)]
                num = [_dot(x.astype(BF16), vp[bt, :], NN) for x, bt in zip(p, both)]
                for rw, x, mx, nm in zip(rows, p, m, num):
                    num_t[rw, :] = nm
                    m_t[rw, :] = jnp.broadcast_to(mx, (n, HEAD_DIM))
                    l_t[rw, :] = jnp.broadcast_to(jnp.sum(x, axis=1, keepdims=True), (n, HEAD_DIM))
                return 0

            lax.fori_loop(0, nt // DIL_GROUP, tiles, 0)
            if r != 1:
                for nat, per, _ in _dil_chunks(s, r):
                    m_old, m_new_p = m_s[nat, :], pm[per, :]
                    m_new = jnp.maximum(m_old, m_new_p)
                    a_old, a_p = jnp.exp(m_old - m_new), jnp.exp(m_new_p - m_new)
                    m_s[nat, :] = m_new
                    l_s[nat, :] = l_s[nat, :] * a_old + pl_[per, :] * a_p
                    acc_s[nat, :] = acc_s[nat, :] * a_old + pnum[per, :] * a_p

        g = g_ref[...]

        def finish(t, _):
            rows = pl.ds(pl.multiple_of(t * n, n), n)
            l = l_s[rows, :]
            o = acc_s[rows, :] / l
            o_ref[rows, :] = o
            lse_ref[rows, :] = m_s[rows, :] + jnp.log(l)
            y_ref[rows, :] = _head_out(o, zg_ref[rows, :], g).astype(BF16)
            return 0

        lax.fori_loop(0, nt, finish, 0, unroll=4)

    f32_s = pltpu.VMEM((s, HEAD_DIM), F32)
    bf_s = pltpu.VMEM((s, HEAD_DIM), BF16)
    bf_pad = pltpu.VMEM((s + n, HEAD_DIM), BF16)
    return pl.pallas_call(
        body, name="dil_fwd", grid=(b, n_heads),
        in_specs=[_seg_spec(s, 4), _seg_spec(s, 5), _seg_spec(s, 6), _seg_spec(s, 7),
                  pl.BlockSpec((1, HEAD_DIM), lambda i, h: (0, h)),
                  pl.BlockSpec((None, 1, HEAD_DIM), lambda i, h: (h, 0, 0)), ANY],
        out_specs=[_head_spec(s), _head_spec(s), _seg_spec(s, 1)],
        out_shape=[jax.ShapeDtypeStruct((b, s, w), F32), jax.ShapeDtypeStruct((b, s, w), F32),
                   jax.ShapeDtypeStruct((2, b, s, w), BF16)],
        scratch_shapes=[bf_s, bf_pad, bf_pad, f32_s, f32_s, f32_s, f32_s, f32_s, f32_s],
        input_output_aliases={6: 2},
        compiler_params=_cparams(("parallel", "parallel")),
    )(proj8, proj8, proj8, proj8, g_dil, slopes, y2)


def _dil_bwd(proj8, o_dl, lse_dl, dy2, g_dil, slopes, dproj8):
    _, b, s, w = proj8.shape
    n_heads = w // HEAD_DIM
    _dil_check(s)
    n = DIL_BLOCK
    nt = s // n
    scale = 1.0 / math.sqrt(HEAD_DIM)

    def body(q_ref, k_ref, v_ref, zg_ref, o_ref, lse_ref, dy_ref, g_ref, sl_ref, dp_in, dp_ref, dg_ref,
             do_n, dt_n, dq_n, dk_n, dv_n, qp, kp, vp, dop, dtp, lsep, pdq, pdk, pdv):
        del dp_in
        dq_ref, dk_ref, dv_ref, dzg_ref = (dp_ref.at[i] for i in range(4))
        slope = sl_ref[...][:, :1]
        g = g_ref[...]

        def prologue(t, dg):
            rows = pl.ds(pl.multiple_of(t * n, n), n)
            o = o_ref[rows, :]
            do, dzg, dgi = _head_out_bwd(o, zg_ref[rows, :], g, dy_ref[rows, :])
            dzg_ref[rows, :] = dzg.astype(BF16)
            do_n[rows, :] = do
            dt_n[rows, :] = jnp.broadcast_to(jnp.sum(do * o, axis=-1, keepdims=True), (n, HEAD_DIM))
            return dg + dgi

        dg_ref[...] = lax.fori_loop(0, nt, prologue, jnp.zeros((1, HEAD_DIM), F32), unroll=4)
        dq_n[...] = jnp.zeros_like(dq_n)
        dk_n[...] = jnp.zeros_like(dk_n)
        dv_n[...] = jnp.zeros_like(dv_n)
        kp[pl.ds(0, n), :] = jnp.zeros((n, HEAD_DIM), BF16)
        vp[pl.ds(0, n), :] = jnp.zeros((n, HEAD_DIM), BF16)

        for (window, r) in DIL_PAIRS:
            nb = (s // r) // n
            masks = _dil_masks(slope, float(r))
            for nat, per, padded in _dil_chunks(s, r):
                qp[per, :] = q_ref[nat, :].astype(BF16)
                kp[padded, :] = k_ref[nat, :].astype(BF16)
                vp[padded, :] = v_ref[nat, :].astype(BF16)
                dop[per, :] = do_n[nat, :].astype(BF16)
                dtp[per, :] = dt_n[nat, :]
                lsep[per, :] = lse_ref[nat, :]
            pdk[...] = jnp.zeros_like(pdk)
            pdv[...] = jnp.zeros_like(pdv)

            def tiles(tt, _):
                ts = [tt * DIL_GROUP + i for i in range(DIL_GROUP)]
                rows = [pl.ds(pl.multiple_of(t * n, n), n) for t in ts]
                both = [pl.ds(pl.multiple_of(t * n, n), 2 * n) for t in ts]
                q = [qp[rw, :] for rw in rows]
                do = [dop[rw, :] for rw in rows]
                sc = [_dil_scores(qq, kp[bt, :], masks, lax.rem(t, nb) == 0) for t, qq, bt in zip(ts, q, both)]
                dp = [_dot(dd, vp[bt, :], NT) for dd, bt in zip(do, both)]
                p = [jnp.exp(x - lsep[rw, :][:, :1]) for x, rw in zip(sc, rows)]
                ds = [((x * (y - dtp[rw, :][:, :1])) * scale).astype(BF16) for x, y, rw in zip(p, dp, rows)]
                dq = [_dot(x, kp[bt, :], NN) for x, bt in zip(ds, both)]
                dk = [_dot(x, qq, TN) for x, qq in zip(ds, q)]
                dv = [_dot(x.astype(BF16), dd, TN) for x, dd in zip(p, do)]
                for rw, bt, x, y, z in zip(rows, both, dq, dk, dv):
                    pdq[rw, :] = x
                    pdk[bt, :] += y
                    pdv[bt, :] += z
                return 0

            lax.fori_loop(0, nt // DIL_GROUP, tiles, 0)
            for nat, per, padded in _dil_chunks(s, r):
                dq_n[nat, :] += pdq[per, :]
                dk_n[nat, :] += pdk[padded, :]
                dv_n[nat, :] += pdv[padded, :]

        dq_ref[...] = dq_n[...].astype(BF16)
        dk_ref[...] = dk_n[...].astype(BF16)
        dv_ref[...] = dv_n[...].astype(BF16)

    f32_s = pltpu.VMEM((s, HEAD_DIM), F32)
    f32_pad = pltpu.VMEM((s + n, HEAD_DIM), F32)
    bf_s = pltpu.VMEM((s, HEAD_DIM), BF16)
    bf_pad = pltpu.VMEM((s + n, HEAD_DIM), BF16)
    return pl.pallas_call(
        body, name="dil_bwd", grid=(b, n_heads),
        in_specs=[_seg_spec(s, 4), _seg_spec(s, 5), _seg_spec(s, 6), _seg_spec(s, 7),
                  _head_spec(s), _head_spec(s), _seg_spec(s, 1),
                  pl.BlockSpec((1, HEAD_DIM), lambda i, h: (0, h)),
                  pl.BlockSpec((None, 1, HEAD_DIM), lambda i, h: (h, 0, 0)), ANY],
        out_specs=[_seg4_spec(s, 1), pl.BlockSpec((None, 1, HEAD_DIM), lambda i, h: (i, 0, h))],
        out_shape=[jax.ShapeDtypeStruct((8, b, s, w), BF16), jax.ShapeDtypeStruct((b, 1, w), F32)],
        scratch_shapes=[f32_s] * 5 + [bf_s, bf_pad, bf_pad, bf_s] + [f32_s, f32_s, f32_s, f32_pad, f32_pad],
        input_output_aliases={9: 0},
        compiler_params=_cparams(("parallel", "parallel")),
    )(proj8, proj8, proj8, proj8, o_dl, lse_dl, dy2, g_dil, slopes, dproj8)


def _small_update(gathered, n_b, params, m, v):
    n_dev, _, width = gathered.shape

    def body(g_ref, p_ref, m_ref, v_ref, grad_ref, d_ref, nm_ref, nv_ref, loss_ref):
        for row in range(2):
            acc = None
            for dev in range(n_dev):
                for i in range(n_b):
                    term = g_ref[dev, pl.ds(row * n_b + i, 1), :]
                    acc = term if acc is None else acc + term
            grad_ref[pl.ds(row, 1), :] = acc
        loss = g_ref[0, pl.ds(2 * n_b, 1), pl.ds(0, 128)]
        for dev in range(1, n_dev):
            loss = loss + g_ref[dev, pl.ds(2 * n_b, 1), pl.ds(0, 128)]
        loss_ref[...] = loss
        d, nm, nv = _adamw_math(p_ref[...], grad_ref[...], m_ref[...], v_ref[...])
        d_ref[...] = d
        nm_ref[...] = nm
        nv_ref[...] = nv

    sds = jax.ShapeDtypeStruct((2, width), F32)
    return pl.pallas_call(
        body, name="small_update",
        in_specs=[VMEM_SPEC] * 4, out_specs=[VMEM_SPEC] * 5,
        out_shape=[sds, sds, sds, sds, jax.ShapeDtypeStruct((1, 128), F32)],
        compiler_params=_cparams(),
    )(gathered, params, m, v)


def _wada_update(c_t, dmod, w, m, v):
    d, bt = c_t.shape
    _, n = dmod.shape
    tr, tc = _tile(d, 512), _tile(n, 1024)

    def body(c_ref, dm_ref, w_ref, m_ref, v_ref, g_ref, d_ref, nm_ref, nv_ref):
        cv = c_ref[...]
        cs = (cv * _sigmoid(cv)).astype(BF16)
        grad = _dot(cs, dm_ref[...].astype(BF16), NN)
        g_ref[...] = grad
        dl, nm, nv = _adamw_math(w_ref[...], grad, m_ref[...], v_ref[...])
        d_ref[...] = dl
        nm_ref[...] = nm
        nv_ref[...] = nv

    spec = pl.BlockSpec((tr, tc), lambda i, j: (i, j))
    sds = jax.ShapeDtypeStruct((d, n), F32)
    return pl.pallas_call(
        body, name="wada_update", grid=(d // tr, n // tc),
        in_specs=[pl.BlockSpec((tr, bt), lambda i, j: (i, 0)), pl.BlockSpec((bt, tc), lambda i, j: (0, j)),
                  spec, spec, spec],
        out_specs=[spec] * 4, out_shape=[sds] * 4,
        compiler_params=_cparams(("parallel", "parallel")),
    )(c_t, dmod, w, m, v)


def _reduce_begin(gs, tag):
    ra = _sibling_half_swap(gs, "swap_" + tag)
    pa, own = _pair_sum(gs, ra, "pair_sum_" + tag)
    return _ScatterChips(pa), own


def _reduce_finish(rb, own, w, m, v, tag):
    half = _chip_sum(rb, own, "chip_sum_" + tag)
    return _adamw(w, _sibling_join(half, "join_" + tag), m, v, "adamw_" + tag)


def kernel(x, c, w_ada, b_ada, g_norm, w_in, g_sb, g_dil, w_out, g_final, loss_target, m_w_ada, m_b_ada, m_g_norm, m_w_in, m_g_sb, m_g_dil, m_w_out, m_g_final, v_w_ada, v_b_ada, v_g_norm, v_w_in, v_g_sb, v_g_dil, v_w_out, v_g_final):
    nb, s, d = x.shape
    t = nb * s
    na = w_ada.shape[2]
    cs = w_in.shape[2]
    w = cs // 2
    n_heads = w // HEAD_DIM
    r_out = w_out.shape[1]
    assert 2 * nb + 1 <= 8 and 2 * d + 2 * w <= 3 * d and N_CHIPS * na == 3 * d and N_CHIPS * r_out == 2 * w
    xi, yi, ci = _place()
    chip = 2 * xi + yi
    dev = 2 * chip + ci

    c_all = _allgather8(jnp.pad(c, ((0, 8 - nb), (0, 0))), "gather_c")
    c16 = c_all.reshape(N_DEV, 8, d)[:, :nb].reshape(N_DEV * nb, d)
    b_ada_shard = lax.dynamic_slice(b_ada, (0, chip * na), (1, na))
    mod_part = _mod_fwd(c16, w_ada[0], b_ada_shard)
    mod_all = _allgather8(mod_part, "gather_mod")
    mod_full = mod_all.reshape(N_CHIPS, 2, N_DEV * nb, na)[:, 0].transpose(1, 0, 2).reshape(N_DEV * nb, 3 * d)
    mod = lax.dynamic_slice(mod_full, (dev * nb, 0), (nb, 3 * d))
    shift, scale, gate = (mod[:, i * d:(i + 1) * d].reshape(nb, 1, d) for i in range(3))

    h = _norm_mod_fwd(x, g_norm, scale, shift)
    h2 = h.reshape(t, d)
    ws_in_slab, w_in_own = _cast_bf16_slab(w_in[0], "cast_w_in", with_own=True)
    (ws_out_slab,) = _cast_bf16_slab(w_out[0], "cast_w_out")
    proj8, ws_in, ws_out = _proj_fwd_split(h2, w_in_own, _GatherChips(ws_in_slab), _GatherChips(ws_out_slab))
    proj8 = proj8.reshape(8, nb, s, w)
    w_out_full = ws_out.reshape(2 * w, d)

    slopes = jnp.exp2(-ALIBI_MAX_BIAS * jnp.arange(1, n_heads + 1, dtype=F32) / n_heads)
    slopes = jnp.broadcast_to(slopes[:, None, None], (n_heads, 1, HEAD_DIM))
    o_sb, tot_sb, y2, sb_trips = _sb_fwd(proj8, g_sb)
    o_dl, lse_dl, y2 = _dil_fwd(proj8, g_dil, slopes, y2)
    y2f = y2.reshape(2, t, w)
    out = _out_fwd(y2f, w_out_full)

    dx1, dout, dgate, dg_final, loss_part = _loss_head(
        x, out.reshape(nb, s, d), gate, g_final.reshape(1, d), loss_target)
    dout2 = dout.reshape(t, d)
    gs_out = _out_bwd_w(y2f, dout2).reshape(N_CHIPS, r_out, d)
    scatter_out, own_out = _reduce_begin(gs_out, "w_out")
    dy2 = _out_bwd_y(dout2, w_out_full).reshape(2, nb, s, w)
    dproj8, dg_sb, rb_out = _sb_bwd(proj8, o_sb, tot_sb, sb_trips, dy2, g_sb, fused=scatter_out)
    dproj8, dg_dl = _dil_bwd(proj8, o_dl, lse_dl, dy2, g_dil, slopes, dproj8)
    dproj8 = dproj8.reshape(8, t, w)
    gs_in = _proj_bwd_w(h2, dproj8)
    scatter_in, own_in = _reduce_begin(gs_in, "w_in")
    dh, rb_in = _proj_bwd_x(dproj8, ws_in, fused=scatter_in)
    grad_x, dshift, dscale, dg_norm = _norm_mod_bwd(x, dh.reshape(nb, s, d), dx1, g_norm, scale)

    width = 3 * d
    dmod = jnp.concatenate([dshift, dscale, dgate], axis=-1).reshape(nb, width)
    gains = jnp.concatenate([dg_sb.reshape(nb, w), dg_dl.reshape(nb, w)], axis=-1)
    gains = jnp.pad(gains, ((0, 0), (2 * d, width - 2 * d - 2 * w)))
    first = jnp.pad(jnp.concatenate([dg_norm, dg_final], axis=-1), ((0, nb - 1), (0, width - 2 * d)))
    loss_row = jnp.pad(loss_part, ((0, 0), (0, width - 128)))
    pack = jnp.concatenate([dmod, gains + first, loss_row, jnp.zeros((8 - 2 * nb - 1, width), F32)], axis=0)
    gathered = _allgather8(pack, "gather_small").reshape(N_DEV, 8, width)

    def stack(bias, gn, gf, gsb, gdl):
        row1 = jnp.concatenate([gn.reshape(1, d), gf.reshape(1, d), gsb.reshape(1, w), gdl.reshape(1, w)], axis=-1)
        return jnp.concatenate([bias.reshape(1, width), jnp.pad(row1, ((0, 0), (0, width - 2 * d - 2 * w)))], axis=0)

    small = _small_update(
        gathered, nb, stack(b_ada, g_norm, g_final, g_sb, g_dil),
        stack(m_b_ada, m_g_norm, m_g_final, m_g_sb, m_g_dil), stack(v_b_ada, v_g_norm, v_g_final, v_g_sb, v_g_dil))
    loss = small[4][0, 0]

    def unstack(a):
        return (a[0:1, :], a[1:2, 0:d], a[1, d:2 * d], a[1:2, 2 * d:2 * d + w], a[1:2, 2 * d + w:2 * d + 2 * w])

    (g_b, g_gn, g_gf, g_gsb, g_gdl), (d_b, d_gn, d_gf, d_gsb, d_gdl), (nm_b, nm_gn, nm_gf, nm_gsb, nm_gdl), \
        (nv_b, nv_gn, nv_gf, nv_gsb, nv_gdl) = (unstack(a) for a in small[:4])

    dmod_all = gathered[:, :nb].reshape(N_DEV * nb, width)
    dmod_cols = lax.dynamic_slice(dmod_all, (0, chip * na), (N_DEV * nb, na))
    g_wa, d_wa, nm_wa, nv_wa = _wada_update(c16.T, dmod_cols, w_ada[0], m_w_ada[0], v_w_ada[0])

    g_wi, d_wi, nm_wi, nv_wi = _reduce_finish(rb_in, own_in, w_in[0], m_w_in[0], v_w_in[0], "w_in")
    g_wo, d_wo, nm_wo, nv_wo = _reduce_finish(rb_out, own_out, w_out[0], m_w_out[0], v_w_out[0], "w_out")

    lead = lambda a: a[None]
    return (loss, grad_x,
            lead(g_wa), g_b, g_gn, lead(g_wi), g_gsb, g_gdl, lead(g_wo), g_gf,
            lead(d_wa), d_b, d_gn, lead(d_wi), d_gsb, d_gdl, lead(d_wo), d_gf,
            lead(nm_wa), nm_b, nm_gn, lead(nm_wi), nm_gsb, nm_gdl, lead(nm_wo), nm_gf,
            lead(nv_wa), nv_b, nv_gn, lead(nv_wi), nv_gsb, nv_gdl, lead(nv_wo), nv_gf)
```

```python
import functools
import math

import jax
import jax.numpy as jnp
from jax import lax
from jax.experimental import pallas as pl
from jax.experimental.pallas import tpu as pltpu

F32 = jnp.float32
BF16 = jnp.bfloat16
MESH = pl.DeviceIdType.MESH

HEAD_DIM = 128
EPS = 1e-6
DIL_PAIRS = ((128, 1), (512, 4), (2048, 16))
ALIBI_MAX_BIAS = 8.0
ADAM_LR = 0.001
ADAM_B1 = 0.9
ADAM_B2 = 0.999
ADAM_EPS = 1e-08
ADAM_WD = 0.01
ADAM_STEP = 10
N_CHIPS = 4
N_DEV = 8
VMEM_LIMIT_BYTES = 56 * 1024 * 1024
NEG_BIG = -1e30

NN = (((1,), (0,)), ((), ()))
NT = (((1,), (1,)), ((), ()))
TN = (((0,), (0,)), ((), ()))

ANY = pl.BlockSpec(memory_space=pl.ANY)
VMEM_SPEC = pl.BlockSpec(memory_space=pltpu.VMEM)


def _cparams(sem=None):
    return pltpu.CompilerParams(dimension_semantics=sem, vmem_limit_bytes=VMEM_LIMIT_BYTES)


def _tile(dim, pref):
    t = min(dim, pref)
    assert dim % t == 0, (dim, pref)
    return t


def _dot(a, b, dims):
    return lax.dot_general(a, b, dims, preferred_element_type=F32)


def _sigmoid(x):
    return 1.0 / (1.0 + jnp.exp(-x))


def _place():
    return lax.axis_index("x"), lax.axis_index("y"), lax.axis_index("c")


def _allgather8(x_shard, name):
    m_per, n = x_shard.shape

    def body(x_ref, out_ref, send_sems, recv_sems, local_sem):
        x, y, c = _place()
        me, sibling = (x, y, c), (x, y, 1 - c)
        chips = [(1 - x, y), (x, 1 - y), (1 - x, 1 - y)]

        def rows(px, py, pc):
            return out_ref.at[pl.ds((4 * px + 2 * py + pc) * m_per, m_per), :]

        def copy(k, block, to, src=None):
            return pltpu.make_async_remote_copy(
                src_ref=rows(*block) if src is None else src, dst_ref=rows(*block),
                send_sem=send_sems.at[k], recv_sem=recv_sems.at[k], device_id=to, device_id_type=MESH)

        mine = pltpu.make_async_copy(x_ref, rows(*me), local_sem)
        mine.start()
        first = [copy(0, me, sibling, src=x_ref)]
        first += [copy(1 + j, me, (*chip, c), src=x_ref) for j, chip in enumerate(chips)]
        for cp in first:
            cp.start()
        passed = [copy(4 + j, (*chip, c), sibling) for j, chip in enumerate(chips)]
        for j, chip in enumerate(chips):
            copy(1 + j, (*chip, c), me).wait_recv()
            passed[j].start()
        copy(0, sibling, me).wait_recv()
        for j, chip in enumerate(chips):
            copy(4 + j, (*chip, 1 - c), me).wait_recv()
        for cp in first + passed:
            cp.wait_send()
        mine.wait()

    return pl.pallas_call(
        body, name=name,
        out_shape=jax.ShapeDtypeStruct((N_DEV * m_per, n), x_shard.dtype),
        in_specs=[VMEM_SPEC], out_specs=VMEM_SPEC,
        scratch_shapes=[pltpu.SemaphoreType.DMA((7,)), pltpu.SemaphoreType.DMA((7,)), pltpu.SemaphoreType.DMA],
    )(x_shard)


class _GatherChips:
    def __init__(self, ws):
        self.inputs = [ws]
        self.out_shapes = [jax.ShapeDtypeStruct(ws.shape, ws.dtype)]
        self.aliases = {0: 0}
        self.scratch = [pltpu.SemaphoreType.DMA((12,)), pltpu.SemaphoreType.DMA((12,))]
        self.quarter = ws.shape[1] // 4

    def _copy(self, refs, k, chip, pc, part, to):
        _, out_ref, send_sems, recv_sems = refs
        rows = out_ref.at[2 * chip[0] + chip[1], pl.ds((2 * pc + part) * self.quarter, self.quarter), :]
        return pltpu.make_async_remote_copy(
            src_ref=rows, dst_ref=rows, send_sem=send_sems.at[k], recv_sem=recv_sems.at[k],
            device_id=to, device_id_type=MESH)

    def _sends(self, refs, phase):
        x, y, c = _place()
        sibling, x_nbr, y_nbr, diag = (x, y, 1 - c), (1 - x, y), (x, 1 - y), (1 - x, 1 - y)
        plan = {
            "start": [(0, (x, y), 0, (*x_nbr, c)), (1, (x, y), 1, (*y_nbr, c)),
                      (2, (x, y), 1, (*x_nbr, c)), (3, (x, y), 0, (*y_nbr, c))],
            "middle": [(4, x_nbr, 0, (*y_nbr, c)), (6, x_nbr, 0, sibling), (5, y_nbr, 1, (*x_nbr, c)),
                       (7, y_nbr, 1, sibling), (8, x_nbr, 1, sibling), (9, y_nbr, 0, sibling)],
            "wait": [(10, diag, 0, sibling), (11, diag, 1, sibling)],
        }[phase]
        return [self._copy(refs, k, chip, c, part, to) for k, chip, part, to in plan]

    def _landings(self, refs, phase):
        x, y, c = _place()
        me, x_nbr, y_nbr, diag = (x, y, c), (1 - x, y), (x, 1 - y), (1 - x, 1 - y)
        plan = {
            "middle": [(0, x_nbr, c, 0), (1, y_nbr, c, 1), (2, x_nbr, c, 1), (3, y_nbr, c, 0)],
            "wait": [(4, diag, c, 0), (5, diag, c, 1)],
            "sibling": [(6, x_nbr, 1 - c, 0), (7, y_nbr, 1 - c, 1), (8, x_nbr, 1 - c, 1), (9, y_nbr, 1 - c, 0),
                        (10, diag, 1 - c, 0), (11, diag, 1 - c, 1)],
        }[phase]
        return [self._copy(refs, k, chip, pc, part, me) for k, chip, pc, part in plan]

    def start(self, *refs):
        for cp in self._sends(refs, "start"):
            cp.start()

    def middle(self, *refs):
        landed = self._landings(refs, "middle")
        passed = self._sends(refs, "middle")
        landed[0].wait_recv()
        passed[0].start()
        passed[1].start()
        landed[1].wait_recv()
        passed[2].start()
        passed[3].start()
        landed[2].wait_recv()
        passed[4].start()
        landed[3].wait_recv()
        passed[5].start()

    def wait(self, *refs):
        landed = self._landings(refs, "wait")
        passed = self._sends(refs, "wait")
        for arrival, cp in zip(landed, passed):
            arrival.wait_recv()
            cp.start()
        for arrival in self._landings(refs, "sibling"):
            arrival.wait_recv()
        for phase in ("start", "middle", "wait"):
            for cp in self._sends(refs, phase):
                cp.wait_send()


def _sibling_half_swap(gs, name):
    n, r, cdim = gs.shape
    half = r // 2

    def body(g_ref, out_ref, send_sem, recv_sem):
        x, y, c = _place()
        cp = pltpu.make_async_remote_copy(
            src_ref=g_ref.at[:, pl.ds((1 - c) * half, half), :], dst_ref=out_ref,
            send_sem=send_sem, recv_sem=recv_sem, device_id=(x, y, 1 - c), device_id_type=MESH)
        cp.start()
        cp.wait()

    return pl.pallas_call(
        body, name=name,
        out_shape=jax.ShapeDtypeStruct((n, half, cdim), gs.dtype),
        in_specs=[ANY], out_specs=ANY,
        scratch_shapes=[pltpu.SemaphoreType.DMA, pltpu.SemaphoreType.DMA],
    )(gs)


class _ScatterChips:
    def __init__(self, pa):
        self.inputs = [pa]
        self.out_shapes = [jax.ShapeDtypeStruct(pa.shape, pa.dtype)]
        self.scratch = [pltpu.SemaphoreType.DMA((3,)), pltpu.SemaphoreType.DMA((3,)), pltpu.SemaphoreType.DMA]

    @staticmethod
    def _mine(p_ref, out_ref, send_sems, recv_sems, local_sem):
        x, y, _ = _place()
        return pltpu.make_async_copy(p_ref.at[2 * x + y], out_ref.at[2 * x + y], local_sem)

    @staticmethod
    def _remote(p_ref, out_ref, send_sems, recv_sems, local_sem, incoming):
        x, y, c = _place()
        me = 2 * x + y
        remote = []
        for j, (px, py) in enumerate([(1 - x, y), (x, 1 - y), (1 - x, 1 - y)]):
            remote.append(pltpu.make_async_remote_copy(
                src_ref=p_ref.at[me if incoming else 2 * px + py], dst_ref=out_ref.at[2 * px + py if incoming else me],
                send_sem=send_sems.at[j], recv_sem=recv_sems.at[j], device_id=(px, py, c), device_id_type=MESH))
        return remote

    def start(self, *refs):
        self._mine(*refs).start()
        for cp in self._remote(*refs, incoming=False):
            cp.start()

    def wait(self, *refs):
        for cp in self._remote(*refs, incoming=True):
            cp.wait_recv()
        for cp in self._remote(*refs, incoming=False):
            cp.wait_send()
        self._mine(*refs).wait()


def _fused_specs(fused):
    if fused is None:
        return [], [], [], [], []
    return (list(fused.inputs), [ANY] * len(fused.inputs), list(fused.out_shapes), [ANY] * len(fused.out_shapes),
            list(fused.scratch))


def _fused_aliases(fused, first_input, first_output):
    aliases = getattr(fused, "aliases", {}) if fused is not None else {}
    return {first_input + i: first_output + o for i, o in aliases.items()}


def _fused_begin(fused, grid, refs):
    if fused is not None:
        first = functools.reduce(lambda p, q: p & q, [pl.program_id(i) == 0 for i in range(len(grid))])
        pl.when(first)(lambda: fused.start(*refs))
        if hasattr(fused, "middle"):
            step = functools.reduce(lambda acc, ig: acc * ig[1] + pl.program_id(ig[0]), enumerate(grid), 0)
            pl.when(step == math.prod(grid) // 2)(lambda: fused.middle(*refs))


def _fused_end(fused, grid, refs):
    if fused is not None:
        last = functools.reduce(lambda p, q: p & q, [pl.program_id(i) == g - 1 for i, g in enumerate(grid)])
        pl.when(last)(lambda: fused.wait(*refs))


def _sibling_join(full, name):
    h2, cdim = full.shape
    h = h2 // 2

    def body(in_ref, out_ref, send_sem, recv_sem):
        del in_ref
        x, y, c = _place()
        mine = out_ref.at[pl.ds(c * h, h), :]
        cp = pltpu.make_async_remote_copy(
            src_ref=mine, dst_ref=mine, send_sem=send_sem, recv_sem=recv_sem,
            device_id=(x, y, 1 - c), device_id_type=MESH)
        cp.start()
        theirs = out_ref.at[pl.ds((1 - c) * h, h), :]
        pltpu.make_async_remote_copy(
            src_ref=theirs, dst_ref=theirs, send_sem=send_sem, recv_sem=recv_sem,
            device_id=(x, y, 1 - c), device_id_type=MESH).wait_recv()
        cp.wait_send()

    return pl.pallas_call(
        body, name=name,
        out_shape=jax.ShapeDtypeStruct(full.shape, full.dtype),
        in_specs=[ANY], out_specs=ANY, input_output_aliases={0: 0},
        scratch_shapes=[pltpu.SemaphoreType.DMA, pltpu.SemaphoreType.DMA],
    )(full)


def _cast_bf16_slab(w, name, with_own=False):
    r, cdim = w.shape
    tr, tc = _tile(r, 512), _tile(cdim, 2048)

    def body(pc_ref, w_ref, o_ref, *own_ref):
        o_ref[...] = w_ref[...].astype(BF16)
        for ref in own_ref:
            ref[...] = w_ref[...].astype(BF16)

    plain = pl.BlockSpec((tr, tc), lambda i, j, pc: (i, j))
    return pl.pallas_call(
        body, name=name,
        grid_spec=pltpu.PrefetchScalarGridSpec(
            num_scalar_prefetch=1, grid=(r // tr, cdim // tc),
            in_specs=[plain],
            out_specs=[pl.BlockSpec((None, tr, tc), lambda i, j, pc: (pc[1], i, j))] + [plain] * with_own),
        out_shape=[jax.ShapeDtypeStruct((N_CHIPS, r, cdim), BF16)] + [jax.ShapeDtypeStruct((r, cdim), BF16)] * with_own,
        compiler_params=_cparams(("parallel", "parallel")),
    )(_place_scalars(), w)


def _place_scalars():
    x, y, c = _place()
    return jnp.stack([c, 2 * x + y]).astype(jnp.int32)


def _pair_sum(gs, ra, name):
    n, r, cdim = gs.shape
    half = r // 2
    tr, tc = _tile(half, 512), _tile(cdim, 2048)
    nt = half // tr

    def body(pc_ref, g_ref, r_ref, o_ref, own_ref):
        val = g_ref[...].astype(F32) + r_ref[...].astype(F32)
        o_ref[...] = val.astype(BF16)

        @pl.when(pl.program_id(2) == pc_ref[1])
        def _():
            own_ref[...] = val

    return pl.pallas_call(
        body, name=name,
        grid_spec=pltpu.PrefetchScalarGridSpec(
            num_scalar_prefetch=1, grid=(nt, cdim // tc, n),
            in_specs=[pl.BlockSpec((None, tr, tc), lambda i, j, s, pc: (s, pc[0] * nt + i, j)),
                      pl.BlockSpec((None, tr, tc), lambda i, j, s, pc: (s, i, j))],
            out_specs=[pl.BlockSpec((None, tr, tc), lambda i, j, s, pc: (s, i, j)),
                       pl.BlockSpec((tr, tc), lambda i, j, s, pc: (i, j))]),
        out_shape=[jax.ShapeDtypeStruct((n, half, cdim), BF16), jax.ShapeDtypeStruct((half, cdim), F32)],
        compiler_params=_cparams(("parallel", "parallel", "arbitrary")),
    )(_place_scalars(), gs, ra)


def _chip_sum(rb, own, name):
    n, h, cdim = rb.shape
    tr, tc = _tile(h, 256), _tile(cdim, 2048)
    nt = h // tr

    def body(pc_ref, r_ref, own_ref, o_ref):
        chip = pc_ref[1]
        acc = None
        for p in range(n):
            term = jnp.where(chip == p, own_ref[...], r_ref[p].astype(F32))
            acc = term if acc is None else acc + term
        o_ref[...] = acc

    return pl.pallas_call(
        body, name=name,
        grid_spec=pltpu.PrefetchScalarGridSpec(
            num_scalar_prefetch=1, grid=(nt, cdim // tc),
            in_specs=[pl.BlockSpec((n, tr, tc), lambda i, j, pc: (0, i, j)),
                      pl.BlockSpec((tr, tc), lambda i, j, pc: (i, j))],
            out_specs=pl.BlockSpec((tr, tc), lambda i, j, pc: (pc[0] * nt + i, j))),
        out_shape=jax.ShapeDtypeStruct((2 * h, cdim), F32),
        compiler_params=_cparams(("parallel", "parallel")),
    )(_place_scalars(), rb, own)


def _adamw_math(w, g, m, v):
    m = ADAM_B1 * m + (1.0 - ADAM_B1) * g
    v = ADAM_B2 * v + (1.0 - ADAM_B2) * (g * g)
    m_hat = m / (1.0 - ADAM_B1 ** ADAM_STEP)
    v_hat = v / (1.0 - ADAM_B2 ** ADAM_STEP)
    delta = -ADAM_LR * (m_hat / (jnp.sqrt(v_hat) + ADAM_EPS) + ADAM_WD * w)
    return delta, m, v


def _adamw(w, g, m, v, name):
    r, cdim = w.shape
    tr, tc = _tile(r, 256), _tile(cdim, 2048)

    def body(w_ref, g_ref, m_ref, v_ref, go_ref, d_ref, nm_ref, nv_ref):
        gv = g_ref[...]
        d, nm, nv = _adamw_math(w_ref[...], gv, m_ref[...], v_ref[...])
        go_ref[...] = gv
        d_ref[...] = d
        nm_ref[...] = nm
        nv_ref[...] = nv

    spec = pl.BlockSpec((tr, tc), lambda i, j: (i, j))
    sds = jax.ShapeDtypeStruct((r, cdim), F32)
    return pl.pallas_call(
        body, name=name, grid=(r // tr, cdim // tc),
        in_specs=[spec] * 4, out_specs=[spec] * 4, out_shape=[sds] * 4,
        compiler_params=_cparams(("parallel", "parallel")),
    )(w, g, m, v)


def _matmul(a, b, *, grid, a_spec, b_spec, out_spec, out_shape, acc_shape, dims, name, bias=None, bias_spec=None,
            silu_a=False, fused=None):
    nk = grid[2]
    f_in, f_in_specs, f_out, f_out_specs, f_scratch = _fused_specs(fused)
    n_in = 2 + (bias is not None)

    acc_scratch = [pltpu.VMEM(acc_shape, F32)] if nk > 1 else []

    def body(*refs):
        a_ref, b_ref = refs[:2]
        bias_ref = refs[2] if bias is not None else None
        o_ref = refs[n_in + len(f_in)]
        n_fixed = n_in + len(f_in) + 1 + len(f_out)
        f_refs = (*refs[n_in:n_in + len(f_in)], *refs[n_in + len(f_in) + 1:n_fixed],
                  *refs[n_fixed + len(acc_scratch):])
        _fused_begin(fused, grid, f_refs)

        def product():
            if len(a_ref.shape) == 3:
                tks = a_ref.shape[2]
                parts = [_dot(a_ref[i], b_ref[:, i * tks:(i + 1) * tks], dims) for i in range(a_ref.shape[0])]
                return functools.reduce(lambda p, q: p + q, parts)
            av = a_ref[...]
            if silu_a:
                av = av * _sigmoid(av)
            return _dot(av.astype(BF16), b_ref[...].astype(BF16), dims)

        def finish(res):
            if bias is not None:
                res = res + bias_ref[...]
            o_ref[...] = res.astype(o_ref.dtype)

        if nk == 1:
            finish(product())
        else:
            acc_ref = refs[n_fixed]
            k = pl.program_id(2)

            @pl.when(k == 0)
            def _():
                acc_ref[...] = product()

            if nk > 2:
                @pl.when((k > 0) & (k < nk - 1))
                def _():
                    acc_ref[...] += product()

            @pl.when(k == nk - 1)
            def _():
                finish(acc_ref[...] + product())

        _fused_end(fused, grid, f_refs)

    in_specs = [a_spec, b_spec] + ([] if bias is None else [bias_spec]) + f_in_specs
    args = (a, b) + (() if bias is None else (bias,)) + tuple(f_in)
    sem = ("parallel", "parallel", "arbitrary") if fused is None else ("arbitrary",) * 3
    res = pl.pallas_call(
        body, name=name, grid=grid, in_specs=in_specs, out_specs=[out_spec] + f_out_specs,
        out_shape=[out_shape] + f_out,
        scratch_shapes=acc_scratch + f_scratch,
        input_output_aliases=_fused_aliases(fused, n_in, 1),
        compiler_params=_cparams(sem),
    )(*args)
    return res[0] if fused is None else tuple(res)


def _mm_tiles(m, n, k):
    return _tile(m, 1024), _tile(n, 1024), _tile(k, 4096)


def _proj_part(h2, wmat, n_seg, w_block, seg_of, w, name, carry=None, fused=None):
    t, d = h2.shape
    tm, tn, _ = _mm_tiles(t, w, d)
    npseg = w // tn
    grid = (t // tm, n_seg * npseg)
    f_in, f_in_specs, f_out, f_out_specs, f_scratch = _fused_specs(fused)
    n_carry = carry is not None

    def body(place_ref, a_ref, b_ref, *refs):
        del place_ref
        o_ref = refs[n_carry + len(f_in)]
        f_refs = (*refs[n_carry:n_carry + len(f_in)], *refs[n_carry + len(f_in) + 1:])
        _fused_begin(fused, grid, f_refs)
        o_ref[...] = _dot(a_ref[...], b_ref[...], NN)
        _fused_end(fused, grid, f_refs)

    w_spec = pl.BlockSpec((d, tn) if wmat.ndim == 2 else (None, d, tn),
                          lambda m, n, place: w_block(n // npseg, n % npseg, place))
    res = pl.pallas_call(
        body, name=name,
        grid_spec=pltpu.PrefetchScalarGridSpec(
            num_scalar_prefetch=1, grid=grid,
            in_specs=[pl.BlockSpec((tm, d), lambda m, n, place: (m, 0)), w_spec] + [ANY] * n_carry + f_in_specs,
            out_specs=[pl.BlockSpec((None, tm, tn), lambda m, n, place: (seg_of(n // npseg, place), m, n % npseg))]
            + f_out_specs,
            scratch_shapes=f_scratch),
        out_shape=[jax.ShapeDtypeStruct((8, t, w), F32)] + f_out,
        input_output_aliases={**({3: 0} if n_carry else {}), **_fused_aliases(fused, 3 + n_carry, 1)},
        compiler_params=_cparams(("arbitrary", "arbitrary")),
    )(_place_scalars(), h2, wmat, *([carry] if n_carry else []), *f_in)
    return tuple(res)


def _proj_fwd_split(h2, w_own, gather_in, gather_out):
    w = w_own.shape[1] // 2
    npseg = w // _mm_tiles(h2.shape[0], w, h2.shape[1])[1]
    proj8, ws_in = _proj_part(
        h2, w_own, 2, lambda j, i, place: (0, j * npseg + i), lambda j, place: 2 * place[1] + j, w,
        "proj_fwd_own", fused=gather_in)

    def shard(j, place):
        return (place[1] + 1 + j // 2) % N_CHIPS

    proj8, ws_out = _proj_part(
        h2, ws_in, 6, lambda j, i, place: (shard(j, place), 0, (j % 2) * npseg + i),
        lambda j, place: 2 * shard(j, place) + j % 2, w, "proj_fwd_rest", carry=proj8, fused=gather_out)
    return proj8, ws_in, ws_out


def _proj_bwd_x(dproj8, ws_in, fused=None):
    _, t, w = dproj8.shape
    _, d, cs = ws_in.shape
    tm, tn, _ = _mm_tiles(t, d, w)
    return _matmul(
        dproj8, ws_in, grid=(t // tm, d // tn, N_CHIPS), dims=NT, name="proj_bwd_x", fused=fused,
        a_spec=pl.BlockSpec((2, tm, w), lambda m, n, k: (k, m, 0)),
        b_spec=pl.BlockSpec((None, tn, cs), lambda m, n, k: (k, n, 0)),
        out_spec=pl.BlockSpec((tm, tn), lambda m, n, k: (m, n)),
        out_shape=jax.ShapeDtypeStruct((t, d), F32), acc_shape=(tm, tn))


def _proj_bwd_w(h2, dproj8):
    t, d = h2.shape
    _, _, w = dproj8.shape
    cs = 2 * w
    tm, tn, tk = _mm_tiles(d, w, t)
    nps, npseg = cs // tn, w // tn
    return _matmul(
        h2, dproj8, grid=(d // tm, 8 * npseg, t // tk), dims=TN, name="proj_bwd_w",
        a_spec=pl.BlockSpec((tk, tm), lambda m, n, k: (k, m)),
        b_spec=pl.BlockSpec((None, tk, tn), lambda m, n, k: (n // npseg, k, n % npseg)),
        out_spec=pl.BlockSpec((None, tm, tn), lambda m, n, k: (n // nps, m, n % nps)),
        out_shape=jax.ShapeDtypeStruct((N_CHIPS, d, cs), BF16), acc_shape=(tm, tn))


def _out_fwd(y2, w_out):
    _, t, w = y2.shape
    _, d = w_out.shape
    tm, tn, tk = _mm_tiles(t, d, w)
    kpg = w // tk
    return _matmul(
        y2, w_out, grid=(t // tm, d // tn, 2 * kpg), dims=NN, name="out_fwd",
        a_spec=pl.BlockSpec((None, tm, tk), lambda m, n, k: (k // kpg, m, k % kpg)),
        b_spec=pl.BlockSpec((tk, tn), lambda m, n, k: (k, n)),
        out_spec=pl.BlockSpec((tm, tn), lambda m, n, k: (m, n)),
        out_shape=jax.ShapeDtypeStruct((t, d), F32), acc_shape=(tm, tn))


def _out_bwd_y(dout, w_out):
    t, d = dout.shape
    w = w_out.shape[0] // 2
    tm, tn, tk = _mm_tiles(t, w, d)
    npg = w // tn
    return _matmul(
        dout, w_out, grid=(t // tm, 2 * npg, d // tk), dims=NT, name="out_bwd_y",
        a_spec=pl.BlockSpec((tm, tk), lambda m, n, k: (m, k)),
        b_spec=pl.BlockSpec((tn, tk), lambda m, n, k: (n, k)),
        out_spec=pl.BlockSpec((None, tm, tn), lambda m, n, k: (n // npg, m, n % npg)),
        out_shape=jax.ShapeDtypeStruct((2, t, w), F32), acc_shape=(tm, tn))


def _out_bwd_w(y2, dout):
    _, t, w = y2.shape
    _, d = dout.shape
    tm, tn, tk = _mm_tiles(w, d, t)
    mpg = w // tm
    return _matmul(
        y2, dout, grid=(2 * mpg, d // tn, t // tk), dims=TN, name="out_bwd_w",
        a_spec=pl.BlockSpec((None, tk, tm), lambda m, n, k: (m // mpg, k, m % mpg)),
        b_spec=pl.BlockSpec((tk, tn), lambda m, n, k: (k, n)),
        out_spec=pl.BlockSpec((tm, tn), lambda m, n, k: (m, n)),
        out_shape=jax.ShapeDtypeStruct((2 * w, d), BF16), acc_shape=(tm, tn))


def _mod_fwd(c_all, w_ada, b_ada):
    bt, d = c_all.shape
    _, n = w_ada.shape
    tn, tk = _tile(n, 512), _tile(d, 1024)
    return _matmul(
        c_all, w_ada, grid=(1, n // tn, d // tk), dims=NN, name="mod_fwd", silu_a=True,
        a_spec=pl.BlockSpec((bt, tk), lambda i, j, l: (0, l)),
        b_spec=pl.BlockSpec((tk, tn), lambda i, j, l: (l, j)),
        bias=b_ada, bias_spec=pl.BlockSpec((1, tn), lambda i, j, l: (0, j)),
        out_spec=pl.BlockSpec((bt, tn), lambda i, j, l: (0, j)),
        out_shape=jax.ShapeDtypeStruct((bt, n), F32), acc_shape=(bt, tn))


def _norm_mod_fwd(x, g_norm, scale, shift):
    b, s, d = x.shape
    ts = _tile(s, 256)

    def body(x_ref, g_ref, sc_ref, sh_ref, h_ref):
        xv = x_ref[...]
        r = lax.rsqrt(jnp.mean(xv * xv, axis=-1, keepdims=True) + EPS)
        y = (xv * r) * g_ref[...]
        h_ref[...] = (y * (1.0 + sc_ref[...]) + sh_ref[...]).astype(BF16)

    row = pl.BlockSpec((None, ts, d), lambda i, j: (i, j, 0))
    per_b = pl.BlockSpec((None, 1, d), lambda i, j: (i, 0, 0))
    return pl.pallas_call(
        body, name="norm_mod_fwd", grid=(b, s // ts),
        in_specs=[row, pl.BlockSpec((1, d), lambda i, j: (0, 0)), per_b, per_b],
        out_specs=row, out_shape=jax.ShapeDtypeStruct((b, s, d), BF16),
        compiler_params=_cparams(("parallel", "parallel")),
    )(x, g_norm, scale, shift)


def _norm_mod_bwd(x, dh, dx1, g_norm, scale):
    b, s, d = x.shape
    ts = _tile(s, 256)

    def body(x_ref, dh_ref, dx1_ref, g_ref, sc_ref, gx_ref, dsh_ref, dsc_ref, dg_ref):
        i, j = pl.program_id(0), pl.program_id(1)

        @pl.when(j == 0)
        def _():
            dsh_ref[...] = jnp.zeros_like(dsh_ref)
            dsc_ref[...] = jnp.zeros_like(dsc_ref)

        @pl.when((i == 0) & (j == 0))
        def _():
            dg_ref[...] = jnp.zeros_like(dg_ref)

        xv, dhv, g = x_ref[...], dh_ref[...], g_ref[...]
        r = lax.rsqrt(jnp.mean(xv * xv, axis=-1, keepdims=True) + EPS)
        xh = xv * r
        dsh_ref[...] += jnp.sum(dhv, axis=0, keepdims=True)
        dsc_ref[...] += jnp.sum(dhv * (xh * g), axis=0, keepdims=True)
        dn = dhv * (1.0 + sc_ref[...])
        dg_ref[...] += jnp.sum(dn * xh, axis=0, keepdims=True)
        u = dn * g
        dx = r * u - xv * (r * r * r) * jnp.mean(u * xv, axis=-1, keepdims=True)
        gx_ref[...] = dx1_ref[...] + dx

    row = pl.BlockSpec((None, ts, d), lambda i, j: (i, j, 0))
    per_b = pl.BlockSpec((None, 1, d), lambda i, j: (i, 0, 0))
    vec = pl.BlockSpec((1, d), lambda i, j: (0, 0))
    return pl.pallas_call(
        body, name="norm_mod_bwd", grid=(b, s // ts),
        in_specs=[row, row, row, vec, per_b],
        out_specs=[row, per_b, per_b, vec],
        out_shape=[jax.ShapeDtypeStruct((b, s, d), F32), jax.ShapeDtypeStruct((b, 1, d), F32),
                   jax.ShapeDtypeStruct((b, 1, d), F32), jax.ShapeDtypeStruct((1, d), F32)],
        compiler_params=_cparams(("arbitrary", "arbitrary")),
    )(x, dh, dx1, g_norm, scale)


def _loss_head(x, out, gate, g_final, target):
    b, s, d = x.shape
    ts = _tile(s, 256)

    def body(x_ref, o_ref, gt_ref, g_ref, t_ref, dx1_ref, dout_ref, dgt_ref, dg_ref, loss_ref):
        i, j = pl.program_id(0), pl.program_id(1)

        @pl.when(j == 0)
        def _():
            dgt_ref[...] = jnp.zeros_like(dgt_ref)

        @pl.when((i == 0) & (j == 0))
        def _():
            dg_ref[...] = jnp.zeros_like(dg_ref)
            loss_ref[...] = jnp.zeros_like(loss_ref)

        ov, gt, g = o_ref[...], gt_ref[...], g_ref[...]
        x1 = x_ref[...] + gt * ov
        r = lax.rsqrt(jnp.mean(x1 * x1, axis=-1, keepdims=True) + EPS)
        xh = x1 * r
        err = xh * g - t_ref[...]
        loss_ref[...] += 0.5 * jnp.sum(jnp.mean(err * err, axis=-1, keepdims=True))
        dfin = err * (1.0 / d)
        dg_ref[...] += jnp.sum(dfin * xh, axis=0, keepdims=True)
        u = dfin * g
        dx1 = r * u - x1 * (r * r * r) * jnp.mean(u * x1, axis=-1, keepdims=True)
        dx1_ref[...] = dx1
        dgt_ref[...] += jnp.sum(dx1 * ov, axis=0, keepdims=True)
        dout_ref[...] = (gt * dx1).astype(BF16)

    row = pl.BlockSpec((None, ts, d), lambda i, j: (i, j, 0))
    per_b = pl.BlockSpec((None, 1, d), lambda i, j: (i, 0, 0))
    vec = pl.BlockSpec((1, d), lambda i, j: (0, 0))
    return pl.pallas_call(
        body, name="loss_head", grid=(b, s // ts),
        in_specs=[row, row, per_b, vec, row],
        out_specs=[row, row, per_b, vec, pl.BlockSpec((1, 128), lambda i, j: (0, 0))],
        out_shape=[jax.ShapeDtypeStruct((b, s, d), F32), jax.ShapeDtypeStruct((b, s, d), BF16),
                   jax.ShapeDtypeStruct((b, 1, d), F32), jax.ShapeDtypeStruct((1, d), F32),
                   jax.ShapeDtypeStruct((1, 128), F32)],
        compiler_params=_cparams(("arbitrary", "arbitrary")),
    )(x, out, gate, g_final, target)


def _head_out(o, zg, g):
    rinv = lax.rsqrt(jnp.mean(o * o, axis=-1, keepdims=True) + EPS)
    return ((o * rinv) * g) * (zg * _sigmoid(zg))


def _head_out_bwd(o, zg, g, dy):
    rinv = lax.rsqrt(jnp.mean(o * o, axis=-1, keepdims=True) + EPS)
    rn = o * rinv
    sg = _sigmoid(zg)
    sil = zg * sg
    dzg = dy * (rn * g) * (sg * (1.0 + zg * (1.0 - sg)))
    dg = jnp.sum(dy * rn * sil, axis=0, keepdims=True)
    drn = dy * g * sil
    do = rinv * drn - o * (rinv * rinv * rinv) * jnp.mean(drn * o, axis=-1, keepdims=True)
    return do, dzg, dg


def _head_spec(s):
    return pl.BlockSpec((None, s, HEAD_DIM), lambda b, h: (b, 0, h))


def _seg_spec(s, seg):
    return pl.BlockSpec((None, None, s, HEAD_DIM), lambda b, h: (seg, b, 0, h))


def _seg4_spec(s, group):
    return pl.BlockSpec((4, None, s, HEAD_DIM), lambda b, h: (group, b, 0, h))


SB_Q_BLOCK = 512
SB_K_BLOCK = 256


SB_DEAD_LOG2 = -160.0
LOG2_E = 1.4426950408889634
SB_LOGIT_SCALE = LOG2_E / math.sqrt(HEAD_DIM)


def _sb_terms(raw, valid):
    t = jnp.where(valid, raw * SB_LOGIT_SCALE, NEG_BIG)
    e = jnp.exp2(-jnp.abs(t))
    l1m = -(jnp.maximum(t, 0.0) + jnp.log2(1.0 + e))
    return t, l1m, e


def _split_dot(a, u):
    hi = a.astype(BF16)
    lo = (a - hi.astype(F32)).astype(BF16)
    return _dot(hi, u, NN) + _dot(lo, u, NN)


def _sb_fwd(proj8, g_sb):
    _, b, s, w = proj8.shape
    n_heads = w // HEAD_DIM
    tq, tk = _tile(s, SB_Q_BLOCK), _tile(s, SB_K_BLOCK)
    nq, kpq = s // tq, tq // tk
    scale = 1.0 / math.sqrt(HEAD_DIM)

    def body(q_ref, k_ref, v_ref, zg_ref, g_ref, o_ref, tot_ref, y_ref, trips_ref):
        u_excl = (lax.broadcasted_iota(jnp.int32, (tk, tk), 0)
                  > lax.broadcasted_iota(jnp.int32, (tk, tk), 1)).astype(BF16)
        ahead = lax.broadcasted_iota(jnp.int32, (tq, tk), 0) - lax.broadcasted_iota(jnp.int32, (tq, tk), 1)
        g = g_ref[...]

        def qblock(i, _):
            rows = pl.ds(pl.multiple_of(i * tq, tq), tq)
            q = q_ref[rows, :].astype(BF16)
            nk = (i + 1) * kpq

            def alive(state):
                jj, _, csum = state
                return (jj <= i) & ((jj == 0) | (jnp.max(csum) > SB_DEAD_LOG2))

            def kblocks(state):
                jj, acc, csum = state
                js = [nk - 1 - (jj * kpq + n) for n in range(kpq)]
                cols = [pl.ds(pl.multiple_of(j * tk, tk), tk) for j in js]
                raw = [_dot(q, k_ref[c, :].astype(BF16), NT) for c in cols]
                terms = [_sb_terms(x, ahead > j * tk - i * tq) for x, j in zip(raw, js)]
                sums = [_split_dot(l1m, u_excl) for _, l1m, _ in terms]
                for (t, l1m, _), part, c in zip(terms, sums, cols):
                    a = jnp.exp2((t + l1m) + (part + csum))
                    acc = acc + _dot(a.astype(BF16), v_ref[c, :].astype(BF16), NN)
                    csum = csum + jnp.sum(l1m, axis=1, keepdims=True)
                return jj + 1, acc, csum

            trips, acc, tot = lax.while_loop(
                alive, kblocks, (jnp.int32(0), jnp.zeros((tq, HEAD_DIM), F32), jnp.zeros((tq, 1), F32)))
            o_ref[rows, :] = acc
            tot_ref[rows, :] = jnp.broadcast_to(tot, (tq, HEAD_DIM))
            y_ref[rows, :] = _head_out(acc, zg_ref[rows, :], g).astype(BF16)
            trips_ref[0, i] = trips.astype(F32)
            return 0

        lax.fori_loop(0, nq, qblock, 0)

    return pl.pallas_call(
        body, name="sb_fwd", grid=(b, n_heads),
        in_specs=[_seg_spec(s, 0), _seg_spec(s, 1), _seg_spec(s, 2), _seg_spec(s, 3),
                  pl.BlockSpec((1, HEAD_DIM), lambda i, h: (0, h))],
        out_specs=[_head_spec(s), _head_spec(s), _seg_spec(s, 0),
                   pl.BlockSpec((None, None, 1, nq), lambda i, h: (i, h, 0, 0), memory_space=pltpu.SMEM)],
        out_shape=[jax.ShapeDtypeStruct((b, s, w), F32), jax.ShapeDtypeStruct((b, s, w), F32),
                   jax.ShapeDtypeStruct((2, b, s, w), BF16), jax.ShapeDtypeStruct((b, n_heads, 1, nq), F32)],
        compiler_params=_cparams(("parallel", "parallel")),
    )(proj8, proj8, proj8, proj8, g_sb)


def _sb_bwd(proj8, o_sb, tot_sb, trips, dy2, g_sb, fused=None):
    _, b, s, w = proj8.shape
    n_heads = w // HEAD_DIM
    tq, tk = _tile(s, SB_Q_BLOCK), _tile(s, SB_K_BLOCK)
    nq, kpq = s // tq, tq // tk
    scale = 1.0 / math.sqrt(HEAD_DIM)

    f_in, f_in_specs, f_out, f_out_specs, f_scratch = _fused_specs(fused)
    grid = (b, n_heads)

    def body(*refs):
        q_ref, k_ref, v_ref, zg_ref, o_ref, tot_ref, dy_ref, g_ref, trips_ref = refs[:9]
        dp_ref, dg_ref = refs[9 + len(f_in):11 + len(f_in)]
        do_s, dk_s, dv_s = refs[11 + len(f_in) + len(f_out):14 + len(f_in) + len(f_out)]
        f_refs = (*refs[9:9 + len(f_in)], *refs[11 + len(f_in):11 + len(f_in) + len(f_out)],
                  *refs[14 + len(f_in) + len(f_out):])
        _fused_begin(fused, grid, f_refs)
        dq_ref, dk_ref, dv_ref, dzg_ref = (dp_ref.at[n] for n in range(4))
        ri = lax.broadcasted_iota(jnp.int32, (tk, tk), 0)
        ci = lax.broadcasted_iota(jnp.int32, (tk, tk), 1)
        u_le = (ri <= ci).astype(BF16)
        u_lt = (ri < ci).astype(BF16)
        ahead = lax.broadcasted_iota(jnp.int32, (tq, tk), 0) - lax.broadcasted_iota(jnp.int32, (tq, tk), 1)
        g = g_ref[...]

        def prologue(i, dg):
            rows = pl.ds(pl.multiple_of(i * tq, tq), tq)
            do, dzg, dgi = _head_out_bwd(o_ref[rows, :], zg_ref[rows, :], g, dy_ref[rows, :])
            dzg_ref[rows, :] = dzg.astype(BF16)
            do_s[rows, :] = do.astype(BF16)
            return dg + dgi

        dg_ref[...] = lax.fori_loop(0, nq, prologue, jnp.zeros((1, HEAD_DIM), F32))
        dk_s[...] = jnp.zeros_like(dk_s)
        dv_s[...] = jnp.zeros_like(dv_s)

        def qblock(i, _):
            rows = pl.ds(pl.multiple_of(i * tq, tq), tq)
            q = q_ref[rows, :].astype(BF16)
            do = do_s[rows, :]
            tot = tot_ref[rows, :][:, :1]

            def kblocks(jj, carry):
                dq, pre_l, pre_g = carry
                js = [jj * kpq + n for n in range(kpq)]
                cols = [pl.ds(pl.multiple_of(j * tk, tk), tk) for j in js]
                ks = [k_ref[c, :].astype(BF16) for c in cols]
                raw = [_dot(q, k, NT) for k in ks]
                da = [_dot(do, v_ref[c, :].astype(BF16), NT) for c in cols]
                terms = [_sb_terms(x, ahead > j * tk - i * tq) for x, j in zip(raw, js)]
                sums_l = [_split_dot(l1m, u_le) for _, l1m, _ in terms]
                a, gg = [], []
                for (t, l1m, _), part, d in zip(terms, sums_l, da):
                    a.append(jnp.exp2((t + l1m) + (tot - (part + pre_l))))
                    gg.append(a[-1] * d)
                    pre_l = pre_l + jnp.sum(l1m, axis=1, keepdims=True)
                sums_g = [_split_dot(x, u_lt) for x in gg]
                dzs = []
                for (t, _, e), x, part in zip(terms, gg, sums_g):
                    big_g = part + pre_g
                    pre_g = pre_g + jnp.sum(x, axis=1, keepdims=True)
                    inv = 1.0 / (1.0 + e)
                    sig = jnp.where(t >= 0.0, inv, e * inv)
                    dzs.append(((x - sig * (x + big_g)) * scale).astype(BF16))
                for x, k in zip(dzs, ks):
                    dq = dq + _dot(x, k, NN)
                for x, y, c in zip(dzs, a, cols):
                    dk_s[c, :] += _dot(x, q, TN)
                    dv_s[c, :] += _dot(y.astype(BF16), do, TN)
                return dq, pre_l, pre_g

            zero = jnp.zeros((tq, 1), F32)
            walked = jnp.clip(trips_ref[0, i].astype(jnp.int32), 1, i + 1)
            dq, _, _ = lax.fori_loop(i + 1 - walked, i + 1, kblocks, (jnp.zeros((tq, HEAD_DIM), F32), zero, zero))
            dq_ref[rows, :] = dq.astype(BF16)
            return 0

        lax.fori_loop(0, nq, qblock, 0)
        dk_ref[...] = dk_s[...].astype(BF16)
        dv_ref[...] = dv_s[...].astype(BF16)
        _fused_end(fused, grid, f_refs)

    return pl.pallas_call(
        body, name="sb_bwd", grid=grid,
        in_specs=[_seg_spec(s, 0), _seg_spec(s, 1), _seg_spec(s, 2), _seg_spec(s, 3),
                  _head_spec(s), _head_spec(s), _seg_spec(s, 0),
                  pl.BlockSpec((1, HEAD_DIM), lambda i, h: (0, h)),
                  pl.BlockSpec((None, None, 1, nq), lambda i, h: (i, h, 0, 0), memory_space=pltpu.SMEM)] + f_in_specs,
        out_specs=[_seg4_spec(s, 0), pl.BlockSpec((None, 1, HEAD_DIM), lambda i, h: (i, 0, h))] + f_out_specs,
        out_shape=[jax.ShapeDtypeStruct((8, b, s, w), BF16), jax.ShapeDtypeStruct((b, 1, w), F32)] + f_out,
        scratch_shapes=[pltpu.VMEM((s, HEAD_DIM), BF16), pltpu.VMEM((s, HEAD_DIM), F32),
                        pltpu.VMEM((s, HEAD_DIM), F32)] + f_scratch,
        compiler_params=_cparams(("arbitrary", "arbitrary")),
    )(proj8, proj8, proj8, proj8, o_sb, tot_sb, dy2, g_sb, trips, *f_in)


DIL_BLOCK = 128
DIL_GROUP = 16


def _dil_chunks(s, r):
    length = s // r
    out = []
    for rho in range(r):
        for cc in range(length // DIL_BLOCK):
            if r == 1:
                nat = pl.ds(cc * DIL_BLOCK, DIL_BLOCK)
            else:
                nat = pl.ds(rho + r * DIL_BLOCK * cc, DIL_BLOCK, stride=r)
            off = rho * length + cc * DIL_BLOCK
            out.append((nat, pl.ds(off, DIL_BLOCK), pl.ds(off + DIL_BLOCK, DIL_BLOCK)))
    return out


def _dil_masks(slope, r):
    n = DIL_BLOCK
    ri = lax.broadcasted_iota(jnp.int32, (n, 2 * n), 0)
    ci = lax.broadcasted_iota(jnp.int32, (n, 2 * n), 1)
    steps = ri - ci + n
    inside = (steps >= 0) & (steps <= n)
    bias = slope * (steps.astype(F32) * r)
    return jnp.where(inside, -bias, NEG_BIG), jnp.where(inside & (ci >= n), -bias, NEG_BIG)


def _dil_scores(q, k_pc, masks, first):
    return _dot(q, k_pc, NT) * (1.0 / math.sqrt(HEAD_DIM)) + jnp.where(first, masks[1], masks[0])


def _dil_check(s):
    assert (s // DIL_BLOCK) % DIL_GROUP == 0, s
    for window, r in DIL_PAIRS:
        assert window // r == DIL_BLOCK and s % (r * DIL_BLOCK) == 0, (s, window, r)


def _dil_fwd(proj8, g_dil, slopes, y2):
    _, b, s, w = proj8.shape
    n_heads = w // HEAD_DIM
    _dil_check(s)
    n = DIL_BLOCK
    nt = s // n

    def body(q_ref, k_ref, v_ref, zg_ref, g_ref, sl_ref, y_in, o_ref, lse_ref, y_ref,
             qp, kp, vp, pnum, pm, pl_, acc_s, m_s, l_s):
        del y_in
        slope = sl_ref[...][:, :1]
        kp[pl.ds(0, n), :] = jnp.zeros((n, HEAD_DIM), BF16)
        vp[pl.ds(0, n), :] = jnp.zeros((n, HEAD_DIM), BF16)

        for (window, r) in DIL_PAIRS:
            nb = (s // r) // n
            masks = _dil_masks(slope, float(r))
            for nat, per, padded in _dil_chunks(s, r):
                qp[per, :] = q_ref[nat, :].astype(BF16)
                kp[padded, :] = k_ref[nat, :].astype(BF16)
                vp[padded, :] = v_ref[nat, :].astype(BF16)
            num_t, m_t, l_t = (acc_s, m_s, l_s) if r == 1 else (pnum, pm, pl_)

            def tiles(tt, _):
                ts = [tt * DIL_GROUP + i for i in range(DIL_GROUP)]
                rows = [pl.ds(pl.multiple_of(t * n, n), n) for t in ts]
                both = [pl.ds(pl.multiple_of(t * n, n), 2 * n) for t in ts]
                sc = [_dil_scores(qp[rw, :], kp[bt, :], masks, lax.rem(t, nb) == 0)
                      for t, rw, bt in zip(ts, rows, both)]
                m = [jnp.max(x, axis=1, keepdims=True) for x in sc]
                p = [jnp.exp(x - mx) for x, mx in zip(sc, m)]
                num = [_dot(x.astype(BF16), vp[bt, :], NN) for x, bt in zip(p, both)]
                for rw, x, mx, nm in zip(rows, p, m, num):
                    num_t[rw, :] = nm
                    m_t[rw, :] = jnp.broadcast_to(mx, (n, HEAD_DIM))
                    l_t[rw, :] = jnp.broadcast_to(jnp.sum(x, axis=1, keepdims=True), (n, HEAD_DIM))
                return 0

            lax.fori_loop(0, nt // DIL_GROUP, tiles, 0)
            if r != 1:
                for nat, per, _ in _dil_chunks(s, r):
                    m_old, m_new_p = m_s[nat, :], pm[per, :]
                    m_new = jnp.maximum(m_old, m_new_p)
                    a_old, a_p = jnp.exp(m_old - m_new), jnp.exp(m_new_p - m_new)
                    m_s[nat, :] = m_new
                    l_s[nat, :] = l_s[nat, :] * a_old + pl_[per, :] * a_p
                    acc_s[nat, :] = acc_s[nat, :] * a_old + pnum[per, :] * a_p

        g = g_ref[...]

        def finish(t, _):
            rows = pl.ds(t * n, n)
            l = l_s[rows, :]
            o = acc_s[rows, :] / l
            o_ref[rows, :] = o
            lse_ref[rows, :] = m_s[rows, :] + jnp.log(l)
            y_ref[rows, :] = _head_out(o, zg_ref[rows, :], g).astype(BF16)
            return 0

        for t in range(nt):
            finish(t, 0)

    f32_s = pltpu.VMEM((s, HEAD_DIM), F32)
    bf_s = pltpu.VMEM((s, HEAD_DIM), BF16)
    bf_pad = pltpu.VMEM((s + n, HEAD_DIM), BF16)
    return pl.pallas_call(
        body, name="dil_fwd", grid=(b, n_heads),
        in_specs=[_seg_spec(s, 4), _seg_spec(s, 5), _seg_spec(s, 6), _seg_spec(s, 7),
                  pl.BlockSpec((1, HEAD_DIM), lambda i, h: (0, h)),
                  pl.BlockSpec((None, 1, HEAD_DIM), lambda i, h: (h, 0, 0)), ANY],
        out_specs=[_head_spec(s), _head_spec(s), _seg_spec(s, 1)],
        out_shape=[jax.ShapeDtypeStruct((b, s, w), F32), jax.ShapeDtypeStruct((b, s, w), F32),
                   jax.ShapeDtypeStruct((2, b, s, w), BF16)],
        scratch_shapes=[bf_s, bf_pad, bf_pad, f32_s, f32_s, f32_s, f32_s, f32_s, f32_s],
        input_output_aliases={6: 2},
        compiler_params=_cparams(("parallel", "parallel")),
    )(proj8, proj8, proj8, proj8, g_dil, slopes, y2)


def _dil_bwd(proj8, o_dl, lse_dl, dy2, g_dil, slopes, dproj8):
    _, b, s, w = proj8.shape
    n_heads = w // HEAD_DIM
    _dil_check(s)
    n = DIL_BLOCK
    nt = s // n
    scale = 1.0 / math.sqrt(HEAD_DIM)

    def body(q_ref, k_ref, v_ref, zg_ref, o_ref, lse_ref, dy_ref, g_ref, sl_ref, dp_in, dp_ref, dg_ref,
             do_n, dt_n, dq_n, dk_n, dv_n, qp, kp, vp, dop, dtp, lsep, pdq, pdk, pdv):
        del dp_in
        dq_ref, dk_ref, dv_ref, dzg_ref = (dp_ref.at[i] for i in range(4))
        slope = sl_ref[...][:, :1]
        g = g_ref[...]

        def prologue(t, dg):
            rows = pl.ds(t * n, n)
            o = o_ref[rows, :]
            do, dzg, dgi = _head_out_bwd(o, zg_ref[rows, :], g, dy_ref[rows, :])
            dzg_ref[rows, :] = dzg.astype(BF16)
            do_n[rows, :] = do
            dt_n[rows, :] = jnp.broadcast_to(jnp.sum(do * o, axis=-1, keepdims=True), (n, HEAD_DIM))
            return dg + dgi

        dg_ref[...] = functools.reduce(lambda dg, t: prologue(t, dg), range(nt), jnp.zeros((1, HEAD_DIM), F32))
        dq_n[...] = jnp.zeros_like(dq_n)
        dk_n[...] = jnp.zeros_like(dk_n)
        dv_n[...] = jnp.zeros_like(dv_n)
        kp[pl.ds(0, n), :] = jnp.zeros((n, HEAD_DIM), BF16)
        vp[pl.ds(0, n), :] = jnp.zeros((n, HEAD_DIM), BF16)

        for (window, r) in DIL_PAIRS:
            nb = (s // r) // n
            masks = _dil_masks(slope, float(r))
            for nat, per, padded in _dil_chunks(s, r):
                qp[per, :] = q_ref[nat, :].astype(BF16)
                kp[padded, :] = k_ref[nat, :].astype(BF16)
                vp[padded, :] = v_ref[nat, :].astype(BF16)
                dop[per, :] = do_n[nat, :].astype(BF16)
                dtp[per, :] = dt_n[nat, :]
                lsep[per, :] = lse_ref[nat, :]
            pdk[...] = jnp.zeros_like(pdk)
            pdv[...] = jnp.zeros_like(pdv)

            def tiles(tt, _):
                ts = [tt * DIL_GROUP + i for i in range(DIL_GROUP)]
                rows = [pl.ds(pl.multiple_of(t * n, n), n) for t in ts]
                both = [pl.ds(pl.multiple_of(t * n, n), 2 * n) for t in ts]
                q = [qp[rw, :] for rw in rows]
                do = [dop[rw, :] for rw in rows]
                sc = [_dil_scores(qq, kp[bt, :], masks, lax.rem(t, nb) == 0) for t, qq, bt in zip(ts, q, both)]
                dp = [_dot(dd, vp[bt, :], NT) for dd, bt in zip(do, both)]
                p = [jnp.exp(x - lsep[rw, :][:, :1]) for x, rw in zip(sc, rows)]
                ds = [((x * (y - dtp[rw, :][:, :1])) * scale).astype(BF16) for x, y, rw in zip(p, dp, rows)]
                dq = [_dot(x, kp[bt, :], NN) for x, bt in zip(ds, both)]
                dk = [_dot(x, qq, TN) for x, qq in zip(ds, q)]
                dv = [_dot(x.astype(BF16), dd, TN) for x, dd in zip(p, do)]
                for rw, bt, x, y, z in zip(rows, both, dq, dk, dv):
                    pdq[rw, :] = x
                    pdk[bt, :] += y
                    pdv[bt, :] += z
                return 0

            lax.fori_loop(0, nt // DIL_GROUP, tiles, 0)
            for nat, per, padded in _dil_chunks(s, r):
                dq_n[nat, :] += pdq[per, :]
                dk_n[nat, :] += pdk[padded, :]
                dv_n[nat, :] += pdv[padded, :]

        dq_ref[...] = dq_n[...].astype(BF16)
        dk_ref[...] = dk_n[...].astype(BF16)
        dv_ref[...] = dv_n[...].astype(BF16)

    f32_s = pltpu.VMEM((s, HEAD_DIM), F32)
    f32_pad = pltpu.VMEM((s + n, HEAD_DIM), F32)
    bf_s = pltpu.VMEM((s, HEAD_DIM), BF16)
    bf_pad = pltpu.VMEM((s + n, HEAD_DIM), BF16)
    return pl.pallas_call(
        body, name="dil_bwd", grid=(b, n_heads),
        in_specs=[_seg_spec(s, 4), _seg_spec(s, 5), _seg_spec(s, 6), _seg_spec(s, 7),
                  _head_spec(s), _head_spec(s), _seg_spec(s, 1),
                  pl.BlockSpec((1, HEAD_DIM), lambda i, h: (0, h)),
                  pl.BlockSpec((None, 1, HEAD_DIM), lambda i, h: (h, 0, 0)), ANY],
        out_specs=[_seg4_spec(s, 1), pl.BlockSpec((None, 1, HEAD_DIM), lambda i, h: (i, 0, h))],
        out_shape=[jax.ShapeDtypeStruct((8, b, s, w), BF16), jax.ShapeDtypeStruct((b, 1, w), F32)],
        scratch_shapes=[f32_s] * 5 + [bf_s, bf_pad, bf_pad, bf_s] + [f32_s, f32_s, f32_s, f32_pad, f32_pad],
        input_output_aliases={9: 0},
        compiler_params=_cparams(("parallel", "parallel")),
    )(proj8, proj8, proj8, proj8, o_dl, lse_dl, dy2, g_dil, slopes, dproj8)


def _small_update(gathered, n_b, params, m, v):
    n_dev, _, width = gathered.shape

    def body(g_ref, p_ref, m_ref, v_ref, grad_ref, d_ref, nm_ref, nv_ref, loss_ref):
        for row in range(2):
            acc = None
            for dev in range(n_dev):
                for i in range(n_b):
                    term = g_ref[dev, pl.ds(row * n_b + i, 1), :]
                    acc = term if acc is None else acc + term
            grad_ref[pl.ds(row, 1), :] = acc
        loss = g_ref[0, pl.ds(2 * n_b, 1), pl.ds(0, 128)]
        for dev in range(1, n_dev):
            loss = loss + g_ref[dev, pl.ds(2 * n_b, 1), pl.ds(0, 128)]
        loss_ref[...] = loss
        d, nm, nv = _adamw_math(p_ref[...], grad_ref[...], m_ref[...], v_ref[...])
        d_ref[...] = d
        nm_ref[...] = nm
        nv_ref[...] = nv

    sds = jax.ShapeDtypeStruct((2, width), F32)
    return pl.pallas_call(
        body, name="small_update",
        in_specs=[VMEM_SPEC] * 4, out_specs=[VMEM_SPEC] * 5,
        out_shape=[sds, sds, sds, sds, jax.ShapeDtypeStruct((1, 128), F32)],
        compiler_params=_cparams(),
    )(gathered, params, m, v)


def _wada_update(c_t, dmod, w, m, v):
    d, bt = c_t.shape
    _, n = dmod.shape
    tr, tc = _tile(d, 512), _tile(n, 1024)

    def body(c_ref, dm_ref, w_ref, m_ref, v_ref, g_ref, d_ref, nm_ref, nv_ref):
        cv = c_ref[...]
        cs = (cv * _sigmoid(cv)).astype(BF16)
        grad = _dot(cs, dm_ref[...].astype(BF16), NN)
        g_ref[...] = grad
        dl, nm, nv = _adamw_math(w_ref[...], grad, m_ref[...], v_ref[...])
        d_ref[...] = dl
        nm_ref[...] = nm
        nv_ref[...] = nv

    spec = pl.BlockSpec((tr, tc), lambda i, j: (i, j))
    sds = jax.ShapeDtypeStruct((d, n), F32)
    return pl.pallas_call(
        body, name="wada_update", grid=(d // tr, n // tc),
        in_specs=[pl.BlockSpec((tr, bt), lambda i, j: (i, 0)), pl.BlockSpec((bt, tc), lambda i, j: (0, j)),
                  spec, spec, spec],
        out_specs=[spec] * 4, out_shape=[sds] * 4,
        compiler_params=_cparams(("parallel", "parallel")),
    )(c_t, dmod, w, m, v)


def _reduce_begin(gs, tag):
    ra = _sibling_half_swap(gs, "swap_" + tag)
    pa, own = _pair_sum(gs, ra, "pair_sum_" + tag)
    return _ScatterChips(pa), own


def _reduce_finish(rb, own, w, m, v, tag):
    half = _chip_sum(rb, own, "chip_sum_" + tag)
    return _adamw(w, _sibling_join(half, "join_" + tag), m, v, "adamw_" + tag)


def kernel(x, c, w_ada, b_ada, g_norm, w_in, g_sb, g_dil, w_out, g_final, loss_target, m_w_ada, m_b_ada, m_g_norm, m_w_in, m_g_sb, m_g_dil, m_w_out, m_g_final, v_w_ada, v_b_ada, v_g_norm, v_w_in, v_g_sb, v_g_dil, v_w_out, v_g_final):
    nb, s, d = x.shape
    t = nb * s
    na = w_ada.shape[2]
    cs = w_in.shape[2]
    w = cs // 2
    n_heads = w // HEAD_DIM
    r_out = w_out.shape[1]
    assert 2 * nb + 1 <= 8 and 2 * d + 2 * w <= 3 * d and N_CHIPS * na == 3 * d and N_CHIPS * r_out == 2 * w
    xi, yi, ci = _place()
    chip = 2 * xi + yi
    dev = 2 * chip + ci

    c_all = _allgather8(jnp.pad(c, ((0, 8 - nb), (0, 0))), "gather_c")
    c16 = c_all.reshape(N_DEV, 8, d)[:, :nb].reshape(N_DEV * nb, d)
    b_ada_shard = lax.dynamic_slice(b_ada, (0, chip * na), (1, na))
    mod_part = _mod_fwd(c16, w_ada[0], b_ada_shard)
    mod_all = _allgather8(mod_part, "gather_mod")
    mod_full = mod_all.reshape(N_CHIPS, 2, N_DEV * nb, na)[:, 0].transpose(1, 0, 2).reshape(N_DEV * nb, 3 * d)
    mod = lax.dynamic_slice(mod_full, (dev * nb, 0), (nb, 3 * d))
    shift, scale, gate = (mod[:, i * d:(i + 1) * d].reshape(nb, 1, d) for i in range(3))

    h = _norm_mod_fwd(x, g_norm, scale, shift)
    h2 = h.reshape(t, d)
    ws_in_slab, w_in_own = _cast_bf16_slab(w_in[0], "cast_w_in", with_own=True)
    (ws_out_slab,) = _cast_bf16_slab(w_out[0], "cast_w_out")
    proj8, ws_in, ws_out = _proj_fwd_split(h2, w_in_own, _GatherChips(ws_in_slab), _GatherChips(ws_out_slab))
    proj8 = proj8.reshape(8, nb, s, w)
    w_out_full = ws_out.reshape(2 * w, d)

    slopes = jnp.exp2(-ALIBI_MAX_BIAS * jnp.arange(1, n_heads + 1, dtype=F32) / n_heads)
    slopes = jnp.broadcast_to(slopes[:, None, None], (n_heads, 1, HEAD_DIM))
    o_sb, tot_sb, y2, sb_trips = _sb_fwd(proj8, g_sb)
    o_dl, lse_dl, y2 = _dil_fwd(proj8, g_dil, slopes, y2)
    y2f = y2.reshape(2, t, w)
    out = _out_fwd(y2f, w_out_full)

    dx1, dout, dgate, dg_final, loss_part = _loss_head(
        x, out.reshape(nb, s, d), gate, g_final.reshape(1, d), loss_target)
    dout2 = dout.reshape(t, d)
    gs_out = _out_bwd_w(y2f, dout2).reshape(N_CHIPS, r_out, d)
    scatter_out, own_out = _reduce_begin(gs_out, "w_out")
    dy2 = _out_bwd_y(dout2, w_out_full).reshape(2, nb, s, w)
    dproj8, dg_sb, rb_out = _sb_bwd(proj8, o_sb, tot_sb, sb_trips, dy2, g_sb, fused=scatter_out)
    dproj8, dg_dl = _dil_bwd(proj8, o_dl, lse_dl, dy2, g_dil, slopes, dproj8)
    dproj8 = dproj8.reshape(8, t, w)
    gs_in = _proj_bwd_w(h2, dproj8)
    scatter_in, own_in = _reduce_begin(gs_in, "w_in")
    dh, rb_in = _proj_bwd_x(dproj8, ws_in, fused=scatter_in)
    grad_x, dshift, dscale, dg_norm = _norm_mod_bwd(x, dh.reshape(nb, s, d), dx1, g_norm, scale)

    width = 3 * d
    dmod = jnp.concatenate([dshift, dscale, dgate], axis=-1).reshape(nb, width)
    gains = jnp.concatenate([dg_sb.reshape(nb, w), dg_dl.reshape(nb, w)], axis=-1)
    gains = jnp.pad(gains, ((0, 0), (2 * d, width - 2 * d - 2 * w)))
    first = jnp.pad(jnp.concatenate([dg_norm, dg_final], axis=-1), ((0, nb - 1), (0, width - 2 * d)))
    loss_row = jnp.pad(loss_part, ((0, 0), (0, width - 128)))
    pack = jnp.concatenate([dmod, gains + first, loss_row, jnp.zeros((8 - 2 * nb - 1, width), F32)], axis=0)
    gathered = _allgather8(pack, "gather_small").reshape(N_DEV, 8, width)

    def stack(bias, gn, gf, gsb, gdl):
        row1 = jnp.concatenate([gn.reshape(1, d), gf.reshape(1, d), gsb.reshape(1, w), gdl.reshape(1, w)], axis=-1)
        return jnp.concatenate([bias.reshape(1, width), jnp.pad(row1, ((0, 0), (0, width - 2 * d - 2 * w)))], axis=0)

    small = _small_update(
        gathered, nb, stack(b_ada, g_norm, g_final, g_sb, g_dil),
        stack(m_b_ada, m_g_norm, m_g_final, m_g_sb, m_g_dil), stack(v_b_ada, v_g_norm, v_g_final, v_g_sb, v_g_dil))
    loss = small[4][0, 0]

    def unstack(a):
        return (a[0:1, :], a[1:2, 0:d], a[1, d:2 * d], a[1:2, 2 * d:2 * d + w], a[1:2, 2 * d + w:2 * d + 2 * w])

    (g_b, g_gn, g_gf, g_gsb, g_gdl), (d_b, d_gn, d_gf, d_gsb, d_gdl), (nm_b, nm_gn, nm_gf, nm_gsb, nm_gdl), \
        (nv_b, nv_gn, nv_gf, nv_gsb, nv_gdl) = (unstack(a) for a in small[:4])

    dmod_all = gathered[:, :nb].reshape(N_DEV * nb, width)
    dmod_cols = lax.dynamic_slice(dmod_all, (0, chip * na), (N_DEV * nb, na))
    g_wa, d_wa, nm_wa, nv_wa = _wada_update(c16.T, dmod_cols, w_ada[0], m_w_ada[0], v_w_ada[0])

    g_wi, d_wi, nm_wi, nv_wi = _reduce_finish(rb_in, own_in, w_in[0], m_w_in[0], v_w_in[0], "w_in")
    g_wo, d_wo, nm_wo, nv_wo = _reduce_finish(rb_out, own_out, w_out[0], m_w_out[0], v_w_out[0], "w_out")

    lead = lambda a: a[None]
    return (loss, grad_x,
            lead(g_wa), g_b, g_gn, lead(g_wi), g_gsb, g_gdl, lead(g_wo), g_gf,
            lead(d_wa), d_b, d_gn, lead(d_wi), d_gsb, d_gdl, lead(d_wo), d_gf,
            lead(nm_wa), nm_b, nm_gn, lead(nm_wi), nm_gsb, nm_gdl, lead(nm_wo), nm_gf,
            lead(nv_wa), nv_b, nv_gn, lead(nv_wi), nv_gsb, nv_gdl, lead(nv_wo), nv_gf)
```

```python
import functools
import math

import jax
import jax.numpy as jnp
from jax import lax
from jax.experimental import pallas as pl
from jax.experimental.pallas import tpu as pltpu

F32 = jnp.float32
BF16 = jnp.bfloat16
MESH = pl.DeviceIdType.MESH

HEAD_DIM = 128
EPS = 1e-6
DIL_PAIRS = ((128, 1), (512, 4), (2048, 16))
ALIBI_MAX_BIAS = 8.0
ADAM_LR = 0.001
ADAM_B1 = 0.9
ADAM_B2 = 0.999
ADAM_EPS = 1e-08
ADAM_WD = 0.01
ADAM_STEP = 10
N_CHIPS = 4
N_DEV = 8
VMEM_LIMIT_BYTES = 56 * 1024 * 1024
NEG_BIG = -1e30

NN = (((1,), (0,)), ((), ()))
NT = (((1,), (1,)), ((), ()))
TN = (((0,), (0,)), ((), ()))

ANY = pl.BlockSpec(memory_space=pl.ANY)
VMEM_SPEC = pl.BlockSpec(memory_space=pltpu.VMEM)


def _cparams(sem=None):
    return pltpu.CompilerParams(dimension_semantics=sem, vmem_limit_bytes=VMEM_LIMIT_BYTES)


def _tile(dim, pref):
    t = min(dim, pref)
    assert dim % t == 0, (dim, pref)
    return t


def _dot(a, b, dims):
    return lax.dot_general(a, b, dims, preferred_element_type=F32)


def _sigmoid(x):
    return 1.0 / (1.0 + jnp.exp(-x))


def _place():
    return lax.axis_index("x"), lax.axis_index("y"), lax.axis_index("c")


def _allgather8(x_shard, name):
    m_per, n = x_shard.shape

    def body(x_ref, out_ref, send_sems, recv_sems, local_sem):
        x, y, c = _place()
        me, sibling = (x, y, c), (x, y, 1 - c)
        chips = [(1 - x, y), (x, 1 - y), (1 - x, 1 - y)]

        def rows(px, py, pc):
            return out_ref.at[pl.ds((4 * px + 2 * py + pc) * m_per, m_per), :]

        def copy(k, block, to, src=None):
            return pltpu.make_async_remote_copy(
                src_ref=rows(*block) if src is None else src, dst_ref=rows(*block),
                send_sem=send_sems.at[k], recv_sem=recv_sems.at[k], device_id=to, device_id_type=MESH)

        mine = pltpu.make_async_copy(x_ref, rows(*me), local_sem)
        mine.start()
        first = [copy(0, me, sibling, src=x_ref)]
        first += [copy(1 + j, me, (*chip, c), src=x_ref) for j, chip in enumerate(chips)]
        for cp in first:
            cp.start()
        passed = [copy(4 + j, (*chip, c), sibling) for j, chip in enumerate(chips)]
        for j, chip in enumerate(chips):
            copy(1 + j, (*chip, c), me).wait_recv()
            passed[j].start()
        copy(0, sibling, me).wait_recv()
        for j, chip in enumerate(chips):
            copy(4 + j, (*chip, 1 - c), me).wait_recv()
        for cp in first + passed:
            cp.wait_send()
        mine.wait()

    return pl.pallas_call(
        body, name=name,
        out_shape=jax.ShapeDtypeStruct((N_DEV * m_per, n), x_shard.dtype),
        in_specs=[VMEM_SPEC], out_specs=VMEM_SPEC,
        scratch_shapes=[pltpu.SemaphoreType.DMA((7,)), pltpu.SemaphoreType.DMA((7,)), pltpu.SemaphoreType.DMA],
    )(x_shard)


class _GatherChips:
    def __init__(self, ws):
        self.inputs = [ws]
        self.out_shapes = [jax.ShapeDtypeStruct(ws.shape, ws.dtype)]
        self.aliases = {0: 0}
        self.scratch = [pltpu.SemaphoreType.DMA((12,)), pltpu.SemaphoreType.DMA((12,))]
        self.quarter = ws.shape[1] // 4

    def _copy(self, refs, k, chip, pc, part, to):
        _, out_ref, send_sems, recv_sems = refs
        rows = out_ref.at[2 * chip[0] + chip[1], pl.ds((2 * pc + part) * self.quarter, self.quarter), :]
        return pltpu.make_async_remote_copy(
            src_ref=rows, dst_ref=rows, send_sem=send_sems.at[k], recv_sem=recv_sems.at[k],
            device_id=to, device_id_type=MESH)

    def _sends(self, refs, phase):
        x, y, c = _place()
        sibling, x_nbr, y_nbr, diag = (x, y, 1 - c), (1 - x, y), (x, 1 - y), (1 - x, 1 - y)
        plan = {
            "start": [(0, (x, y), 0, (*x_nbr, c)), (1, (x, y), 1, (*y_nbr, c)),
                      (2, (x, y), 1, (*x_nbr, c)), (3, (x, y), 0, (*y_nbr, c))],
            "middle": [(4, x_nbr, 0, (*y_nbr, c)), (6, x_nbr, 0, sibling), (5, y_nbr, 1, (*x_nbr, c)),
                       (7, y_nbr, 1, sibling), (8, x_nbr, 1, sibling), (9, y_nbr, 0, sibling)],
            "wait": [(10, diag, 0, sibling), (11, diag, 1, sibling)],
        }[phase]
        return [self._copy(refs, k, chip, c, part, to) for k, chip, part, to in plan]

    def _landings(self, refs, phase):
        x, y, c = _place()
        me, x_nbr, y_nbr, diag = (x, y, c), (1 - x, y), (x, 1 - y), (1 - x, 1 - y)
        plan = {
            "middle": [(0, x_nbr, c, 0), (1, y_nbr, c, 1), (2, x_nbr, c, 1), (3, y_nbr, c, 0)],
            "wait": [(4, diag, c, 0), (5, diag, c, 1)],
            "sibling": [(6, x_nbr, 1 - c, 0), (7, y_nbr, 1 - c, 1), (8, x_nbr, 1 - c, 1), (9, y_nbr, 1 - c, 0),
                        (10, diag, 1 - c, 0), (11, diag, 1 - c, 1)],
        }[phase]
        return [self._copy(refs, k, chip, pc, part, me) for k, chip, pc, part in plan]

    def start(self, *refs):
        for cp in self._sends(refs, "start"):
            cp.start()

    def middle(self, *refs):
        landed = self._landings(refs, "middle")
        passed = self._sends(refs, "middle")
        landed[0].wait_recv()
        passed[0].start()
        passed[1].start()
        landed[1].wait_recv()
        passed[2].start()
        passed[3].start()
        landed[2].wait_recv()
        passed[4].start()
        landed[3].wait_recv()
        passed[5].start()

    def wait(self, *refs):
        landed = self._landings(refs, "wait")
        passed = self._sends(refs, "wait")
        for arrival, cp in zip(landed, passed):
            arrival.wait_recv()
            cp.start()
        for arrival in self._landings(refs, "sibling"):
            arrival.wait_recv()
        for phase in ("start", "middle", "wait"):
            for cp in self._sends(refs, phase):
                cp.wait_send()


class _GatherNear(_GatherChips):
    middle = None

    def wait(self, *refs):
        x, y, c = _place()
        sibling, x_nbr, y_nbr = (x, y, 1 - c), (1 - x, y), (x, 1 - y)
        plan = [(6, x_nbr, 0), (7, y_nbr, 1), (8, x_nbr, 1), (9, y_nbr, 0)]
        passed = [self._copy(refs, k, chip, c, part, sibling) for k, chip, part in plan]
        for arrival, cp in zip(self._landings(refs, "middle"), passed):
            arrival.wait_recv()
            cp.start()
        for k, chip, part in plan:
            self._copy(refs, k, chip, 1 - c, part, (x, y, c)).wait_recv()
        for cp in self._sends(refs, "start"):
            cp.wait_send()
        for k, chip, part in plan:
            self._copy(refs, k, chip, c, part, sibling).wait_send()


class _RelayDiag:
    def __init__(self, ws):
        self.inputs = [ws]
        self.out_shapes = [jax.ShapeDtypeStruct(ws.shape[1:], ws.dtype)]
        self.scratch = [pltpu.SemaphoreType.DMA((4,)), pltpu.SemaphoreType.DMA((4,))]
        self.quarter = ws.shape[1] // 4

    def _rows(self, pc, part):
        return pl.ds((2 * pc + part) * self.quarter, self.quarter)

    def _relays(self, refs):
        ws_ref, wd_ref, send_sems, recv_sems = refs
        x, y, c = _place()
        out = []
        for k, (src_chip, to) in enumerate([((1 - x, y), (x, 1 - y)), ((x, 1 - y), (1 - x, y))]):
            out.append(pltpu.make_async_remote_copy(
                src_ref=ws_ref.at[2 * src_chip[0] + src_chip[1], self._rows(c, k), :],
                dst_ref=wd_ref.at[self._rows(c, k), :],
                send_sem=send_sems.at[k], recv_sem=recv_sems.at[k], device_id=(*to, c), device_id_type=MESH))
        return out

    def _local(self, refs, k, pc, to):
        _, wd_ref, send_sems, recv_sems = refs
        rows = wd_ref.at[self._rows(pc, k % 2), :]
        return pltpu.make_async_remote_copy(
            src_ref=rows, dst_ref=rows, send_sem=send_sems.at[k], recv_sem=recv_sems.at[k],
            device_id=to, device_id_type=MESH)

    def start(self, *refs):
        for cp in self._relays(refs):
            cp.start()

    def wait(self, *refs):
        x, y, c = _place()
        me, sibling = (x, y, c), (x, y, 1 - c)
        for k in (0, 1):
            self._local(refs, k, c, me).wait_recv()
            self._local(refs, 2 + k, c, sibling).start()
        for k in (0, 1):
            self._local(refs, 2 + k, 1 - c, me).wait_recv()
        for cp in self._relays(refs):
            cp.wait_send()
        for k in (0, 1):
            self._local(refs, 2 + k, c, sibling).wait_send()


def _sibling_half_swap(gs, name):
    n, r, cdim = gs.shape
    half = r // 2

    def body(g_ref, out_ref, send_sem, recv_sem):
        x, y, c = _place()
        cp = pltpu.make_async_remote_copy(
            src_ref=g_ref.at[:, pl.ds((1 - c) * half, half), :], dst_ref=out_ref,
            send_sem=send_sem, recv_sem=recv_sem, device_id=(x, y, 1 - c), device_id_type=MESH)
        cp.start()
        cp.wait()

    return pl.pallas_call(
        body, name=name,
        out_shape=jax.ShapeDtypeStruct((n, half, cdim), gs.dtype),
        in_specs=[ANY], out_specs=ANY,
        scratch_shapes=[pltpu.SemaphoreType.DMA, pltpu.SemaphoreType.DMA],
    )(gs)


class _ScatterChips:
    def __init__(self, pa):
        self.inputs = [pa]
        self.out_shapes = [jax.ShapeDtypeStruct(pa.shape, pa.dtype)]
        self.scratch = [pltpu.SemaphoreType.DMA((3,)), pltpu.SemaphoreType.DMA((3,)), pltpu.SemaphoreType.DMA]

    @staticmethod
    def _mine(p_ref, out_ref, send_sems, recv_sems, local_sem):
        x, y, _ = _place()
        return pltpu.make_async_copy(p_ref.at[2 * x + y], out_ref.at[2 * x + y], local_sem)

    @staticmethod
    def _remote(p_ref, out_ref, send_sems, recv_sems, local_sem, incoming):
        x, y, c = _place()
        me = 2 * x + y
        remote = []
        for j, (px, py) in enumerate([(1 - x, y), (x, 1 - y), (1 - x, 1 - y)]):
            remote.append(pltpu.make_async_remote_copy(
                src_ref=p_ref.at[me if incoming else 2 * px + py], dst_ref=out_ref.at[2 * px + py if incoming else me],
                send_sem=send_sems.at[j], recv_sem=recv_sems.at[j], device_id=(px, py, c), device_id_type=MESH))
        return remote

    def start(self, *refs):
        self._mine(*refs).start()
        for cp in self._remote(*refs, incoming=False):
            cp.start()

    def wait(self, *refs):
        for cp in self._remote(*refs, incoming=True):
            cp.wait_recv()
        for cp in self._remote(*refs, incoming=False):
            cp.wait_send()
        self._mine(*refs).wait()


def _fused_specs(fused):
    if fused is None:
        return [], [], [], [], []
    return (list(fused.inputs), [ANY] * len(fused.inputs), list(fused.out_shapes), [ANY] * len(fused.out_shapes),
            list(fused.scratch))


def _fused_aliases(fused, first_input, first_output):
    aliases = getattr(fused, "aliases", {}) if fused is not None else {}
    return {first_input + i: first_output + o for i, o in aliases.items()}


def _fused_begin(fused, grid, refs):
    if fused is not None:
        first = functools.reduce(lambda p, q: p & q, [pl.program_id(i) == 0 for i in range(len(grid))])
        pl.when(first)(lambda: fused.start(*refs))
        if callable(getattr(fused, "middle", None)):
            step = functools.reduce(lambda acc, ig: acc * ig[1] + pl.program_id(ig[0]), enumerate(grid), 0)
            pl.when(step == math.prod(grid) // 2)(lambda: fused.middle(*refs))


def _fused_end(fused, grid, refs):
    if fused is not None:
        last = functools.reduce(lambda p, q: p & q, [pl.program_id(i) == g - 1 for i, g in enumerate(grid)])
        pl.when(last)(lambda: fused.wait(*refs))


def _sibling_join(full, name):
    h2, cdim = full.shape
    h = h2 // 2

    def body(in_ref, out_ref, send_sem, recv_sem):
        del in_ref
        x, y, c = _place()
        mine = out_ref.at[pl.ds(c * h, h), :]
        cp = pltpu.make_async_remote_copy(
            src_ref=mine, dst_ref=mine, send_sem=send_sem, recv_sem=recv_sem,
            device_id=(x, y, 1 - c), device_id_type=MESH)
        cp.start()
        theirs = out_ref.at[pl.ds((1 - c) * h, h), :]
        pltpu.make_async_remote_copy(
            src_ref=theirs, dst_ref=theirs, send_sem=send_sem, recv_sem=recv_sem,
            device_id=(x, y, 1 - c), device_id_type=MESH).wait_recv()
        cp.wait_send()

    return pl.pallas_call(
        body, name=name,
        out_shape=jax.ShapeDtypeStruct(full.shape, full.dtype),
        in_specs=[ANY], out_specs=ANY, input_output_aliases={0: 0},
        scratch_shapes=[pltpu.SemaphoreType.DMA, pltpu.SemaphoreType.DMA],
    )(full)


def _cast_bf16_slab(w, name, with_own=False):
    r, cdim = w.shape
    tr, tc = _tile(r, 512), _tile(cdim, 2048)

    def body(pc_ref, w_ref, o_ref, *own_ref):
        o_ref[...] = w_ref[...].astype(BF16)
        for ref in own_ref:
            ref[...] = w_ref[...].astype(BF16)

    plain = pl.BlockSpec((tr, tc), lambda i, j, pc: (i, j))
    return pl.pallas_call(
        body, name=name,
        grid_spec=pltpu.PrefetchScalarGridSpec(
            num_scalar_prefetch=1, grid=(r // tr, cdim // tc),
            in_specs=[plain],
            out_specs=[pl.BlockSpec((None, tr, tc), lambda i, j, pc: (pc[1], i, j))] + [plain] * with_own),
        out_shape=[jax.ShapeDtypeStruct((N_CHIPS, r, cdim), BF16)] + [jax.ShapeDtypeStruct((r, cdim), BF16)] * with_own,
        compiler_params=_cparams(("parallel", "parallel")),
    )(_place_scalars(), w)


def _place_scalars():
    x, y, c = _place()
    return jnp.stack([c, 2 * x + y]).astype(jnp.int32)


def _pair_sum(gs, ra, name):
    n, r, cdim = gs.shape
    half = r // 2
    tr, tc = _tile(half, 512), _tile(cdim, 2048)
    nt = half // tr

    def body(pc_ref, g_ref, r_ref, o_ref, own_ref):
        val = g_ref[...].astype(F32) + r_ref[...].astype(F32)
        o_ref[...] = val.astype(BF16)

        @pl.when(pl.program_id(2) == pc_ref[1])
        def _():
            own_ref[...] = val

    return pl.pallas_call(
        body, name=name,
        grid_spec=pltpu.PrefetchScalarGridSpec(
            num_scalar_prefetch=1, grid=(nt, cdim // tc, n),
            in_specs=[pl.BlockSpec((None, tr, tc), lambda i, j, s, pc: (s, pc[0] * nt + i, j)),
                      pl.BlockSpec((None, tr, tc), lambda i, j, s, pc: (s, i, j))],
            out_specs=[pl.BlockSpec((None, tr, tc), lambda i, j, s, pc: (s, i, j)),
                       pl.BlockSpec((tr, tc), lambda i, j, s, pc: (i, j))]),
        out_shape=[jax.ShapeDtypeStruct((n, half, cdim), BF16), jax.ShapeDtypeStruct((half, cdim), F32)],
        compiler_params=_cparams(("parallel", "parallel", "arbitrary")),
    )(_place_scalars(), gs, ra)


def _chip_sum(rb, own, name):
    n, h, cdim = rb.shape
    tr, tc = _tile(h, 256), _tile(cdim, 2048)
    nt = h // tr

    def body(pc_ref, r_ref, own_ref, o_ref):
        chip = pc_ref[1]
        acc = None
        for p in range(n):
            term = jnp.where(chip == p, own_ref[...], r_ref[p].astype(F32))
            acc = term if acc is None else acc + term
        o_ref[...] = acc

    return pl.pallas_call(
        body, name=name,
        grid_spec=pltpu.PrefetchScalarGridSpec(
            num_scalar_prefetch=1, grid=(nt, cdim // tc),
            in_specs=[pl.BlockSpec((n, tr, tc), lambda i, j, pc: (0, i, j)),
                      pl.BlockSpec((tr, tc), lambda i, j, pc: (i, j))],
            out_specs=pl.BlockSpec((tr, tc), lambda i, j, pc: (pc[0] * nt + i, j))),
        out_shape=jax.ShapeDtypeStruct((2 * h, cdim), F32),
        compiler_params=_cparams(("parallel", "parallel")),
    )(_place_scalars(), rb, own)


def _adamw_math(w, g, m, v):
    m = ADAM_B1 * m + (1.0 - ADAM_B1) * g
    v = ADAM_B2 * v + (1.0 - ADAM_B2) * (g * g)
    m_hat = m / (1.0 - ADAM_B1 ** ADAM_STEP)
    v_hat = v / (1.0 - ADAM_B2 ** ADAM_STEP)
    delta = -ADAM_LR * (m_hat / (jnp.sqrt(v_hat) + ADAM_EPS) + ADAM_WD * w)
    return delta, m, v


def _adamw(w, g, m, v, name):
    r, cdim = w.shape
    tr, tc = _tile(r, 256), _tile(cdim, 2048)

    def body(w_ref, g_ref, m_ref, v_ref, go_ref, d_ref, nm_ref, nv_ref):
        gv = g_ref[...]
        d, nm, nv = _adamw_math(w_ref[...], gv, m_ref[...], v_ref[...])
        go_ref[...] = gv
        d_ref[...] = d
        nm_ref[...] = nm
        nv_ref[...] = nv

    spec = pl.BlockSpec((tr, tc), lambda i, j: (i, j))
    sds = jax.ShapeDtypeStruct((r, cdim), F32)
    return pl.pallas_call(
        body, name=name, grid=(r // tr, cdim // tc),
        in_specs=[spec] * 4, out_specs=[spec] * 4, out_shape=[sds] * 4,
        compiler_params=_cparams(("parallel", "parallel")),
    )(w, g, m, v)


def _matmul(a, b, *, grid, a_spec, b_spec, out_spec, out_shape, acc_shape, dims, name, bias=None, bias_spec=None,
            silu_a=False, fused=None):
    nk = grid[2]
    f_in, f_in_specs, f_out, f_out_specs, f_scratch = _fused_specs(fused)
    n_in = 2 + (bias is not None)

    acc_scratch = [pltpu.VMEM(acc_shape, F32)] if nk > 1 else []

    def body(*refs):
        a_ref, b_ref = refs[:2]
        bias_ref = refs[2] if bias is not None else None
        o_ref = refs[n_in + len(f_in)]
        n_fixed = n_in + len(f_in) + 1 + len(f_out)
        f_refs = (*refs[n_in:n_in + len(f_in)], *refs[n_in + len(f_in) + 1:n_fixed],
                  *refs[n_fixed + len(acc_scratch):])
        _fused_begin(fused, grid, f_refs)

        def product():
            if len(a_ref.shape) == 3:
                tks = a_ref.shape[2]
                parts = [_dot(a_ref[i], b_ref[:, i * tks:(i + 1) * tks], dims) for i in range(a_ref.shape[0])]
                return functools.reduce(lambda p, q: p + q, parts)
            av = a_ref[...]
            if silu_a:
                av = av * _sigmoid(av)
            return _dot(av.astype(BF16), b_ref[...].astype(BF16), dims)

        def finish(res):
            if bias is not None:
                res = res + bias_ref[...]
            o_ref[...] = res.astype(o_ref.dtype)

        if nk == 1:
            finish(product())
        else:
            acc_ref = refs[n_fixed]
            k = pl.program_id(2)

            @pl.when(k == 0)
            def _():
                acc_ref[...] = product()

            if nk > 2:
                @pl.when((k > 0) & (k < nk - 1))
                def _():
                    acc_ref[...] += product()

            @pl.when(k == nk - 1)
            def _():
                finish(acc_ref[...] + product())

        _fused_end(fused, grid, f_refs)

    in_specs = [a_spec, b_spec] + ([] if bias is None else [bias_spec]) + f_in_specs
    args = (a, b) + (() if bias is None else (bias,)) + tuple(f_in)
    sem = ("parallel", "parallel", "arbitrary") if fused is None else ("arbitrary",) * 3
    res = pl.pallas_call(
        body, name=name, grid=grid, in_specs=in_specs, out_specs=[out_spec] + f_out_specs,
        out_shape=[out_shape] + f_out,
        scratch_shapes=acc_scratch + f_scratch,
        input_output_aliases=_fused_aliases(fused, n_in, 1),
        compiler_params=_cparams(sem),
    )(*args)
    return res[0] if fused is None else tuple(res)


def _mm_tiles(m, n, k):
    return _tile(m, 1024), _tile(n, 1024), _tile(k, 4096)


def _proj_part(h2, wmat, n_seg, w_block, seg_of, w, name, carry=None, fused=None):
    t, d = h2.shape
    tm, tn, _ = _mm_tiles(t, w, d)
    npseg = w // tn
    grid = (t // tm, n_seg * npseg)
    f_in, f_in_specs, f_out, f_out_specs, f_scratch = _fused_specs(fused)
    n_carry = carry is not None

    def body(place_ref, a_ref, b_ref, *refs):
        del place_ref
        o_ref = refs[n_carry + len(f_in)]
        f_refs = (*refs[n_carry:n_carry + len(f_in)], *refs[n_carry + len(f_in) + 1:])
        _fused_begin(fused, grid, f_refs)
        o_ref[...] = _dot(a_ref[...], b_ref[...], NN)
        _fused_end(fused, grid, f_refs)

    w_spec = pl.BlockSpec((d, tn) if wmat.ndim == 2 else (None, d, tn),
                          lambda m, n, place: w_block(n // npseg, n % npseg, place))
    res = pl.pallas_call(
        body, name=name,
        grid_spec=pltpu.PrefetchScalarGridSpec(
            num_scalar_prefetch=1, grid=grid,
            in_specs=[pl.BlockSpec((tm, d), lambda m, n, place: (m, 0)), w_spec] + [ANY] * n_carry + f_in_specs,
            out_specs=[pl.BlockSpec((None, tm, tn), lambda m, n, place: (seg_of(n // npseg, place), m, n % npseg))]
            + f_out_specs,
            scratch_shapes=f_scratch),
        out_shape=[jax.ShapeDtypeStruct((8, t, w), F32)] + f_out,
        input_output_aliases={**({3: 0} if n_carry else {}), **_fused_aliases(fused, 3 + n_carry, 1)},
        compiler_params=_cparams(("arbitrary", "arbitrary")),
    )(_place_scalars(), h2, wmat, *([carry] if n_carry else []), *f_in)
    return tuple(res)


def _proj_fwd_split(h2, w_own, gather_in, gather_out):
    w = w_own.shape[1] // 2
    npseg = w // _mm_tiles(h2.shape[0], w, h2.shape[1])[1]
    proj8, ws_in = _proj_part(
        h2, w_own, 2, lambda j, i, place: (0, j * npseg + i), lambda j, place: 2 * place[1] + j, w,
        "proj_fwd_own", fused=gather_in)

    def near(j, place):
        return place[1] ^ (2 - j // 2)

    proj8, w_diag = _proj_part(
        h2, ws_in, 4, lambda j, i, place: (near(j, place), 0, (j % 2) * npseg + i),
        lambda j, place: 2 * near(j, place) + j % 2, w, "proj_fwd_near", carry=proj8, fused=_RelayDiag(ws_in))
    proj8, ws_out = _proj_part(
        h2, w_diag, 2, lambda j, i, place: (0, j * npseg + i), lambda j, place: 2 * (place[1] ^ 3) + j, w,
        "proj_fwd_diag", carry=proj8, fused=gather_out)
    return proj8, ws_in, w_diag, ws_out


def _proj_bwd_x(dproj8, ws_in, fused=None):
    _, t, w = dproj8.shape
    _, d, cs = ws_in.shape
    tm, tn, _ = _mm_tiles(t, d, w)
    return _matmul(
        dproj8, ws_in, grid=(t // tm, d // tn, N_CHIPS), dims=NT, name="proj_bwd_x", fused=fused,
        a_spec=pl.BlockSpec((2, tm, w), lambda m, n, k: (k, m, 0)),
        b_spec=pl.BlockSpec((None, tn, cs), lambda m, n, k: (k, n, 0)),
        out_spec=pl.BlockSpec((tm, tn), lambda m, n, k: (m, n)),
        out_shape=jax.ShapeDtypeStruct((t, d), F32), acc_shape=(tm, tn))


def _proj_bwd_w(h2, dproj8):
    t, d = h2.shape
    _, _, w = dproj8.shape
    cs = 2 * w
    tm, tn, tk = _mm_tiles(d, w, t)
    nps, npseg = cs // tn, w // tn
    return _matmul(
        h2, dproj8, grid=(d // tm, 8 * npseg, t // tk), dims=TN, name="proj_bwd_w",
        a_spec=pl.BlockSpec((tk, tm), lambda m, n, k: (k, m)),
        b_spec=pl.BlockSpec((None, tk, tn), lambda m, n, k: (n // npseg, k, n % npseg)),
        out_spec=pl.BlockSpec((None, tm, tn), lambda m, n, k: (n // nps, m, n % nps)),
        out_shape=jax.ShapeDtypeStruct((N_CHIPS, d, cs), BF16), acc_shape=(tm, tn))


def _out_fwd(y2, w_out):
    _, t, w = y2.shape
    _, d = w_out.shape
    tm, tn, tk = _mm_tiles(t, d, w)
    kpg = w // tk
    return _matmul(
        y2, w_out, grid=(t // tm, d // tn, 2 * kpg), dims=NN, name="out_fwd",
        a_spec=pl.BlockSpec((None, tm, tk), lambda m, n, k: (k // kpg, m, k % kpg)),
        b_spec=pl.BlockSpec((tk, tn), lambda m, n, k: (k, n)),
        out_spec=pl.BlockSpec((tm, tn), lambda m, n, k: (m, n)),
        out_shape=jax.ShapeDtypeStruct((t, d), F32), acc_shape=(tm, tn))


def _out_bwd_y(dout, w_out):
    t, d = dout.shape
    w = w_out.shape[0] // 2
    tm, tn, tk = _mm_tiles(t, w, d)
    npg = w // tn
    return _matmul(
        dout, w_out, grid=(t // tm, 2 * npg, d // tk), dims=NT, name="out_bwd_y",
        a_spec=pl.BlockSpec((tm, tk), lambda m, n, k: (m, k)),
        b_spec=pl.BlockSpec((tn, tk), lambda m, n, k: (n, k)),
        out_spec=pl.BlockSpec((None, tm, tn), lambda m, n, k: (n // npg, m, n % npg)),
        out_shape=jax.ShapeDtypeStruct((2, t, w), F32), acc_shape=(tm, tn))


def _out_bwd_w(y2, dout):
    _, t, w = y2.shape
    _, d = dout.shape
    tm, tn, tk = _mm_tiles(w, d, t)
    mpg = w // tm
    return _matmul(
        y2, dout, grid=(2 * mpg, d // tn, t // tk), dims=TN, name="out_bwd_w",
        a_spec=pl.BlockSpec((None, tk, tm), lambda m, n, k: (m // mpg, k, m % mpg)),
        b_spec=pl.BlockSpec((tk, tn), lambda m, n, k: (k, n)),
        out_spec=pl.BlockSpec((tm, tn), lambda m, n, k: (m, n)),
        out_shape=jax.ShapeDtypeStruct((2 * w, d), BF16), acc_shape=(tm, tn))


def _mod_fwd(c_all, w_ada, b_ada):
    bt, d = c_all.shape
    _, n = w_ada.shape
    tn, tk = _tile(n, 512), _tile(d, 1024)
    return _matmul(
        c_all, w_ada, grid=(1, n // tn, d // tk), dims=NN, name="mod_fwd", silu_a=True,
        a_spec=pl.BlockSpec((bt, tk), lambda i, j, l: (0, l)),
        b_spec=pl.BlockSpec((tk, tn), lambda i, j, l: (l, j)),
        bias=b_ada, bias_spec=pl.BlockSpec((1, tn), lambda i, j, l: (0, j)),
        out_spec=pl.BlockSpec((bt, tn), lambda i, j, l: (0, j)),
        out_shape=jax.ShapeDtypeStruct((bt, n), F32), acc_shape=(bt, tn))


def _norm_mod_fwd(x, g_norm, scale, shift):
    b, s, d = x.shape
    ts = _tile(s, 256)

    def body(x_ref, g_ref, sc_ref, sh_ref, h_ref):
        xv = x_ref[...]
        r = lax.rsqrt(jnp.mean(xv * xv, axis=-1, keepdims=True) + EPS)
        y = (xv * r) * g_ref[...]
        h_ref[...] = (y * (1.0 + sc_ref[...]) + sh_ref[...]).astype(BF16)

    row = pl.BlockSpec((None, ts, d), lambda i, j: (i, j, 0))
    per_b = pl.BlockSpec((None, 1, d), lambda i, j: (i, 0, 0))
    return pl.pallas_call(
        body, name="norm_mod_fwd", grid=(b, s // ts),
        in_specs=[row, pl.BlockSpec((1, d), lambda i, j: (0, 0)), per_b, per_b],
        out_specs=row, out_shape=jax.ShapeDtypeStruct((b, s, d), BF16),
        compiler_params=_cparams(("parallel", "parallel")),
    )(x, g_norm, scale, shift)


def _norm_mod_bwd(x, dh, dx1, g_norm, scale):
    b, s, d = x.shape
    ts = _tile(s, 256)

    def body(x_ref, dh_ref, dx1_ref, g_ref, sc_ref, gx_ref, dsh_ref, dsc_ref, dg_ref):
        i, j = pl.program_id(0), pl.program_id(1)

        @pl.when(j == 0)
        def _():
            dsh_ref[...] = jnp.zeros_like(dsh_ref)
            dsc_ref[...] = jnp.zeros_like(dsc_ref)

        @pl.when((i == 0) & (j == 0))
        def _():
            dg_ref[...] = jnp.zeros_like(dg_ref)

        xv, dhv, g = x_ref[...], dh_ref[...], g_ref[...]
        r = lax.rsqrt(jnp.mean(xv * xv, axis=-1, keepdims=True) + EPS)
        xh = xv * r
        dsh_ref[...] += jnp.sum(dhv, axis=0, keepdims=True)
        dsc_ref[...] += jnp.sum(dhv * (xh * g), axis=0, keepdims=True)
        dn = dhv * (1.0 + sc_ref[...])
        dg_ref[...] += jnp.sum(dn * xh, axis=0, keepdims=True)
        u = dn * g
        dx = r * u - xv * (r * r * r) * jnp.mean(u * xv, axis=-1, keepdims=True)
        gx_ref[...] = dx1_ref[...] + dx

    row = pl.BlockSpec((None, ts, d), lambda i, j: (i, j, 0))
    per_b = pl.BlockSpec((None, 1, d), lambda i, j: (i, 0, 0))
    vec = pl.BlockSpec((1, d), lambda i, j: (0, 0))
    return pl.pallas_call(
        body, name="norm_mod_bwd", grid=(b, s // ts),
        in_specs=[row, row, row, vec, per_b],
        out_specs=[row, per_b, per_b, vec],
        out_shape=[jax.ShapeDtypeStruct((b, s, d), F32), jax.ShapeDtypeStruct((b, 1, d), F32),
                   jax.ShapeDtypeStruct((b, 1, d), F32), jax.ShapeDtypeStruct((1, d), F32)],
        compiler_params=_cparams(("arbitrary", "arbitrary")),
    )(x, dh, dx1, g_norm, scale)


def _loss_head(x, out, gate, g_final, target):
    b, s, d = x.shape
    ts = _tile(s, 256)

    def body(x_ref, o_ref, gt_ref, g_ref, t_ref, dx1_ref, dout_ref, dgt_ref, dg_ref, loss_ref):
        i, j = pl.program_id(0), pl.program_id(1)

        @pl.when(j == 0)
        def _():
            dgt_ref[...] = jnp.zeros_like(dgt_ref)

        @pl.when((i == 0) & (j == 0))
        def _():
            dg_ref[...] = jnp.zeros_like(dg_ref)
            loss_ref[...] = jnp.zeros_like(loss_ref)

        ov, gt, g = o_ref[...], gt_ref[...], g_ref[...]
        x1 = x_ref[...] + gt * ov
        r = lax.rsqrt(jnp.mean(x1 * x1, axis=-1, keepdims=True) + EPS)
        xh = x1 * r
        err = xh * g - t_ref[...]
        loss_ref[...] += 0.5 * jnp.sum(jnp.mean(err * err, axis=-1, keepdims=True))
        dfin = err * (1.0 / d)
        dg_ref[...] += jnp.sum(dfin * xh, axis=0, keepdims=True)
        u = dfin * g
        dx1 = r * u - x1 * (r * r * r) * jnp.mean(u * x1, axis=-1, keepdims=True)
        dx1_ref[...] = dx1
        dgt_ref[...] += jnp.sum(dx1 * ov, axis=0, keepdims=True)
        dout_ref[...] = (gt * dx1).astype(BF16)

    row = pl.BlockSpec((None, ts, d), lambda i, j: (i, j, 0))
    per_b = pl.BlockSpec((None, 1, d), lambda i, j: (i, 0, 0))
    vec = pl.BlockSpec((1, d), lambda i, j: (0, 0))
    return pl.pallas_call(
        body, name="loss_head", grid=(b, s // ts),
        in_specs=[row, row, per_b, vec, row],
        out_specs=[row, row, per_b, vec, pl.BlockSpec((1, 128), lambda i, j: (0, 0))],
        out_shape=[jax.ShapeDtypeStruct((b, s, d), F32), jax.ShapeDtypeStruct((b, s, d), BF16),
                   jax.ShapeDtypeStruct((b, 1, d), F32), jax.ShapeDtypeStruct((1, d), F32),
                   jax.ShapeDtypeStruct((1, 128), F32)],
        compiler_params=_cparams(("arbitrary", "arbitrary")),
    )(x, out, gate, g_final, target)


def _head_out(o, zg, g):
    rinv = lax.rsqrt(jnp.mean(o * o, axis=-1, keepdims=True) + EPS)
    return ((o * rinv) * g) * (zg * _sigmoid(zg))


def _head_out_bwd(o, zg, g, dy):
    rinv = lax.rsqrt(jnp.mean(o * o, axis=-1, keepdims=True) + EPS)
    rn = o * rinv
    sg = _sigmoid(zg)
    sil = zg * sg
    dzg = dy * (rn * g) * (sg * (1.0 + zg * (1.0 - sg)))
    dg = jnp.sum(dy * rn * sil, axis=0, keepdims=True)
    drn = dy * g * sil
    do = rinv * drn - o * (rinv * rinv * rinv) * jnp.mean(drn * o, axis=-1, keepdims=True)
    return do, dzg, dg


def _head_spec(s):
    return pl.BlockSpec((None, s, HEAD_DIM), lambda b, h: (b, 0, h))


def _seg_spec(s, seg):
    return pl.BlockSpec((None, None, s, HEAD_DIM), lambda b, h: (seg, b, 0, h))


def _seg4_spec(s, group):
    return pl.BlockSpec((4, None, s, HEAD_DIM), lambda b, h: (group, b, 0, h))


SB_Q_BLOCK = 512
SB_K_BLOCK = 256


SB_DEAD_LOG2 = -160.0
LOG2_E = 1.4426950408889634
SB_LOGIT_SCALE = LOG2_E / math.sqrt(HEAD_DIM)


def _sb_terms(raw, valid):
    t = jnp.where(valid, raw * SB_LOGIT_SCALE, NEG_BIG)
    e = jnp.exp2(-jnp.abs(t))
    l1m = -(jnp.maximum(t, 0.0) + jnp.log2(1.0 + e))
    return t, l1m, e


def _split_dot(a, u):
    hi = a.astype(BF16)
    lo = (a - hi.astype(F32)).astype(BF16)
    return _dot(hi, u, NN) + _dot(lo, u, NN)


def _sb_fwd(proj8, g_sb):
    _, b, s, w = proj8.shape
    n_heads = w // HEAD_DIM
    tq, tk = _tile(s, SB_Q_BLOCK), _tile(s, SB_K_BLOCK)
    nq, kpq = s // tq, tq // tk
    scale = 1.0 / math.sqrt(HEAD_DIM)

    def body(q_ref, k_ref, v_ref, zg_ref, g_ref, o_ref, tot_ref, y_ref, trips_ref):
        u_excl = (lax.broadcasted_iota(jnp.int32, (tk, tk), 0)
                  > lax.broadcasted_iota(jnp.int32, (tk, tk), 1)).astype(BF16)
        ahead = lax.broadcasted_iota(jnp.int32, (tq, tk), 0) - lax.broadcasted_iota(jnp.int32, (tq, tk), 1)
        g = g_ref[...]

        def qblock(i, _):
            rows = pl.ds(pl.multiple_of(i * tq, tq), tq)
            q = q_ref[rows, :].astype(BF16)
            nk = (i + 1) * kpq

            def alive(state):
                jj, _, csum = state
                return (jj <= i) & ((jj == 0) | (jnp.max(csum) > SB_DEAD_LOG2))

            def kblocks(state):
                jj, acc, csum = state
                js = [nk - 1 - (jj * kpq + n) for n in range(kpq)]
                cols = [pl.ds(pl.multiple_of(j * tk, tk), tk) for j in js]
                raw = [_dot(q, k_ref[c, :].astype(BF16), NT) for c in cols]
                terms = [_sb_terms(x, ahead > j * tk - i * tq) for x, j in zip(raw, js)]
                sums = [_split_dot(l1m, u_excl) for _, l1m, _ in terms]
                for (t, l1m, _), part, c in zip(terms, sums, cols):
                    a = jnp.exp2((t + l1m) + (part + csum))
                    acc = acc + _dot(a.astype(BF16), v_ref[c, :].astype(BF16), NN)
                    csum = csum + jnp.sum(l1m, axis=1, keepdims=True)
                return jj + 1, acc, csum

            trips, acc, tot = lax.while_loop(
                alive, kblocks, (jnp.int32(0), jnp.zeros((tq, HEAD_DIM), F32), jnp.zeros((tq, 1), F32)))
            o_ref[rows, :] = acc
            tot_ref[rows, :] = jnp.broadcast_to(tot, (tq, HEAD_DIM))
            y_ref[rows, :] = _head_out(acc, zg_ref[rows, :], g).astype(BF16)
            trips_ref[0, i] = trips.astype(F32)
            return 0

        lax.fori_loop(0, nq, qblock, 0)

    return pl.pallas_call(
        body, name="sb_fwd", grid=(b, n_heads),
        in_specs=[_seg_spec(s, 0), _seg_spec(s, 1), _seg_spec(s, 2), _seg_spec(s, 3),
                  pl.BlockSpec((1, HEAD_DIM), lambda i, h: (0, h))],
        out_specs=[_head_spec(s), _head_spec(s), _seg_spec(s, 0),
                   pl.BlockSpec((None, None, 1, nq), lambda i, h: (i, h, 0, 0), memory_space=pltpu.SMEM)],
        out_shape=[jax.ShapeDtypeStruct((b, s, w), F32), jax.ShapeDtypeStruct((b, s, w), F32),
                   jax.ShapeDtypeStruct((2, b, s, w), BF16), jax.ShapeDtypeStruct((b, n_heads, 1, nq), F32)],
        compiler_params=_cparams(("parallel", "parallel")),
    )(proj8, proj8, proj8, proj8, g_sb)


def _sb_bwd(proj8, o_sb, tot_sb, trips, dy2, g_sb, fused=None):
    _, b, s, w = proj8.shape
    n_heads = w // HEAD_DIM
    tq, tk = _tile(s, SB_Q_BLOCK), _tile(s, SB_K_BLOCK)
    nq, kpq = s // tq, tq // tk
    scale = 1.0 / math.sqrt(HEAD_DIM)

    f_in, f_in_specs, f_out, f_out_specs, f_scratch = _fused_specs(fused)
    grid = (b, n_heads)

    def body(*refs):
        q_ref, k_ref, v_ref, zg_ref, o_ref, tot_ref, dy_ref, g_ref, trips_ref = refs[:9]
        dp_ref, dg_ref = refs[9 + len(f_in):11 + len(f_in)]
        do_s, dk_s, dv_s = refs[11 + len(f_in) + len(f_out):14 + len(f_in) + len(f_out)]
        f_refs = (*refs[9:9 + len(f_in)], *refs[11 + len(f_in):11 + len(f_in) + len(f_out)],
                  *refs[14 + len(f_in) + len(f_out):])
        _fused_begin(fused, grid, f_refs)
        dq_ref, dk_ref, dv_ref, dzg_ref = (dp_ref.at[n] for n in range(4))
        ri = lax.broadcasted_iota(jnp.int32, (tk, tk), 0)
        ci = lax.broadcasted_iota(jnp.int32, (tk, tk), 1)
        u_le = (ri <= ci).astype(BF16)
        u_lt = (ri < ci).astype(BF16)
        ahead = lax.broadcasted_iota(jnp.int32, (tq, tk), 0) - lax.broadcasted_iota(jnp.int32, (tq, tk), 1)
        g = g_ref[...]

        def prologue(i, dg):
            rows = pl.ds(pl.multiple_of(i * tq, tq), tq)
            do, dzg, dgi = _head_out_bwd(o_ref[rows, :], zg_ref[rows, :], g, dy_ref[rows, :])
            dzg_ref[rows, :] = dzg.astype(BF16)
            do_s[rows, :] = do.astype(BF16)
            return dg + dgi

        dg_ref[...] = lax.fori_loop(0, nq, prologue, jnp.zeros((1, HEAD_DIM), F32))
        dk_s[...] = jnp.zeros_like(dk_s)
        dv_s[...] = jnp.zeros_like(dv_s)

        def qblock(i, _):
            rows = pl.ds(pl.multiple_of(i * tq, tq), tq)
            q = q_ref[rows, :].astype(BF16)
            do = do_s[rows, :]
            tot = tot_ref[rows, :][:, :1]

            def kblocks(jj, carry):
                dq, pre_l, pre_g = carry
                js = [jj * kpq + n for n in range(kpq)]
                cols = [pl.ds(pl.multiple_of(j * tk, tk), tk) for j in js]
                ks = [k_ref[c, :].astype(BF16) for c in cols]
                raw = [_dot(q, k, NT) for k in ks]
                da = [_dot(do, v_ref[c, :].astype(BF16), NT) for c in cols]
                terms = [_sb_terms(x, ahead > j * tk - i * tq) for x, j in zip(raw, js)]
                sums_l = [_split_dot(l1m, u_le) for _, l1m, _ in terms]
                a, gg = [], []
                for (t, l1m, _), part, d in zip(terms, sums_l, da):
                    a.append(jnp.exp2((t + l1m) + (tot - (part + pre_l))))
                    gg.append(a[-1] * d)
                    pre_l = pre_l + jnp.sum(l1m, axis=1, keepdims=True)
                sums_g = [_split_dot(x, u_lt) for x in gg]
                dzs = []
                for (t, _, e), x, part in zip(terms, gg, sums_g):
                    big_g = part + pre_g
                    pre_g = pre_g + jnp.sum(x, axis=1, keepdims=True)
                    inv = 1.0 / (1.0 + e)
                    sig = jnp.where(t >= 0.0, inv, e * inv)
                    dzs.append(((x - sig * (x + big_g)) * scale).astype(BF16))
                for x, k in zip(dzs, ks):
                    dq = dq + _dot(x, k, NN)
                for x, y, c in zip(dzs, a, cols):
                    dk_s[c, :] += _dot(x, q, TN)
                    dv_s[c, :] += _dot(y.astype(BF16), do, TN)
                return dq, pre_l, pre_g

            zero = jnp.zeros((tq, 1), F32)
            walked = jnp.clip(trips_ref[0, i].astype(jnp.int32), 1, i + 1)
            dq, _, _ = lax.fori_loop(i + 1 - walked, i + 1, kblocks, (jnp.zeros((tq, HEAD_DIM), F32), zero, zero))
            dq_ref[rows, :] = dq.astype(BF16)
            return 0

        lax.fori_loop(0, nq, qblock, 0)
        dk_ref[...] = dk_s[...].astype(BF16)
        dv_ref[...] = dv_s[...].astype(BF16)
        _fused_end(fused, grid, f_refs)

    return pl.pallas_call(
        body, name="sb_bwd", grid=grid,
        in_specs=[_seg_spec(s, 0), _seg_spec(s, 1), _seg_spec(s, 2), _seg_spec(s, 3),
                  _head_spec(s), _head_spec(s), _seg_spec(s, 0),
                  pl.BlockSpec((1, HEAD_DIM), lambda i, h: (0, h)),
                  pl.BlockSpec((None, None, 1, nq), lambda i, h: (i, h, 0, 0), memory_space=pltpu.SMEM)] + f_in_specs,
        out_specs=[_seg4_spec(s, 0), pl.BlockSpec((None, 1, HEAD_DIM), lambda i, h: (i, 0, h))] + f_out_specs,
        out_shape=[jax.ShapeDtypeStruct((8, b, s, w), BF16), jax.ShapeDtypeStruct((b, 1, w), F32)] + f_out,
        scratch_shapes=[pltpu.VMEM((s, HEAD_DIM), BF16), pltpu.VMEM((s, HEAD_DIM), F32),
                        pltpu.VMEM((s, HEAD_DIM), F32)] + f_scratch,
        compiler_params=_cparams(("arbitrary", "arbitrary")),
    )(proj8, proj8, proj8, proj8, o_sb, tot_sb, dy2, g_sb, trips, *f_in)


DIL_BLOCK = 128
DIL_GROUP = 16


def _dil_chunks(s, r):
    length = s // r
    out = []
    for rho in range(r):
        for cc in range(length // DIL_BLOCK):
            if r == 1:
                nat = pl.ds(cc * DIL_BLOCK, DIL_BLOCK)
            else:
                nat = pl.ds(rho + r * DIL_BLOCK * cc, DIL_BLOCK, stride=r)
            off = rho * length + cc * DIL_BLOCK
            out.append((nat, pl.ds(off, DIL_BLOCK), pl.ds(off + DIL_BLOCK, DIL_BLOCK)))
    return out


def _dil_masks(slope, r):
    n = DIL_BLOCK
    ri = lax.broadcasted_iota(jnp.int32, (n, 2 * n), 0)
    ci = lax.broadcasted_iota(jnp.int32, (n, 2 * n), 1)
    steps = ri - ci + n
    inside = (steps >= 0) & (steps <= n)
    bias = slope * (steps.astype(F32) * r)
    return jnp.where(inside, -bias, NEG_BIG), jnp.where(inside & (ci >= n), -bias, NEG_BIG)


def _dil_scores(q, k_pc, masks, first):
    return _dot(q, k_pc, NT) * (1.0 / math.sqrt(HEAD_DIM)) + jnp.where(first, masks[1], masks[0])


def _dil_check(s):
    assert (s // DIL_BLOCK) % DIL_GROUP == 0, s
    for window, r in DIL_PAIRS:
        assert window // r == DIL_BLOCK and s % (r * DIL_BLOCK) == 0, (s, window, r)


def _dil_fwd(proj8, g_dil, slopes, y2):
    _, b, s, w = proj8.shape
    n_heads = w // HEAD_DIM
    _dil_check(s)
    n = DIL_BLOCK
    nt = s // n

    def body(q_ref, k_ref, v_ref, zg_ref, g_ref, sl_ref, y_in, o_ref, lse_ref, y_ref,
             qp, kp, vp, pnum, pm, pl_, acc_s, m_s, l_s):
        del y_in
        slope = sl_ref[...][:, :1]
        kp[pl.ds(0, n), :] = jnp.zeros((n, HEAD_DIM), BF16)
        vp[pl.ds(0, n), :] = jnp.zeros((n, HEAD_DIM), BF16)

        for (window, r) in DIL_PAIRS:
            nb = (s // r) // n
            masks = _dil_masks(slope, float(r))
            for nat, per, padded in _dil_chunks(s, r):
                qp[per, :] = q_ref[nat, :].astype(BF16)
                kp[padded, :] = k_ref[nat, :].astype(BF16)
                vp[padded, :] = v_ref[nat, :].astype(BF16)
            num_t, m_t, l_t = (acc_s, m_s, l_s) if r == 1 else (pnum, pm, pl_)

            def tiles(tt, _):
                ts = [tt * DIL_GROUP + i for i in range(DIL_GROUP)]
                rows = [pl.ds(pl.multiple_of(t * n, n), n) for t in ts]
                both = [pl.ds(pl.multiple_of(t * n, n), 2 * n) for t in ts]
                sc = [_dil_scores(qp[rw, :], kp[bt, :], masks, lax.rem(t, nb) == 0)
                      for t, rw, bt in zip(ts, rows, both)]
                m = [jnp.max(x, axis=1, keepdims=True) for x in sc]
                p = [jnp.exp(x - mx) for x, mx in zip(sc, m)]
                num = [_dot(x.astype(BF16), vp[bt, :], NN) for x, bt in zip(p, both)]
                for rw, x, mx, nm in zip(rows, p, m, num):
                    num_t[rw, :] = nm
                    m_t[rw, :] = jnp.broadcast_to(mx, (n, HEAD_DIM))
                    l_t[rw, :] = jnp.broadcast_to(jnp.sum(x, axis=1, keepdims=True), (n, HEAD_DIM))
                return 0

            lax.fori_loop(0, nt // DIL_GROUP, tiles, 0)
            if r != 1:
                for nat, per, _ in _dil_chunks(s, r):
                    m_old, m_new_p = m_s[nat, :], pm[per, :]
                    m_new = jnp.maximum(m_old, m_new_p)
                    a_old, a_p = jnp.exp(m_old - m_new), jnp.exp(m_new_p - m_new)
                    m_s[nat, :] = m_new
                    l_s[nat, :] = l_s[nat, :] * a_old + pl_[per, :] * a_p
                    acc_s[nat, :] = acc_s[nat, :] * a_old + pnum[per, :] * a_p

        g = g_ref[...]

        def finish(t, _):
            rows = pl.ds(t * n, n)
            l = l_s[rows, :]
            o = acc_s[rows, :] / l
            o_ref[rows, :] = o
            lse_ref[rows, :] = m_s[rows, :] + jnp.log(l)
            y_ref[rows, :] = _head_out(o, zg_ref[rows, :], g).astype(BF16)
            return 0

        for t in range(nt):
            finish(t, 0)

    f32_s = pltpu.VMEM((s, HEAD_DIM), F32)
    bf_s = pltpu.VMEM((s, HEAD_DIM), BF16)
    bf_pad = pltpu.VMEM((s + n, HEAD_DIM), BF16)
    return pl.pallas_call(
        body, name="dil_fwd", grid=(b, n_heads),
        in_specs=[_seg_spec(s, 4), _seg_spec(s, 5), _seg_spec(s, 6), _seg_spec(s, 7),
                  pl.BlockSpec((1, HEAD_DIM), lambda i, h: (0, h)),
                  pl.BlockSpec((None, 1, HEAD_DIM), lambda i, h: (h, 0, 0)), ANY],
        out_specs=[_head_spec(s), _head_spec(s), _seg_spec(s, 1)],
        out_shape=[jax.ShapeDtypeStruct((b, s, w), F32), jax.ShapeDtypeStruct((b, s, w), F32),
                   jax.ShapeDtypeStruct((2, b, s, w), BF16)],
        scratch_shapes=[bf_s, bf_pad, bf_pad, f32_s, f32_s, f32_s, f32_s, f32_s, f32_s],
        input_output_aliases={6: 2},
        compiler_params=_cparams(("parallel", "parallel")),
    )(proj8, proj8, proj8, proj8, g_dil, slopes, y2)


def _dil_bwd(proj8, o_dl, lse_dl, dy2, g_dil, slopes, dproj8):
    _, b, s, w = proj8.shape
    n_heads = w // HEAD_DIM
    _dil_check(s)
    n = DIL_BLOCK
    nt = s // n
    scale = 1.0 / math.sqrt(HEAD_DIM)

    def body(q_ref, k_ref, v_ref, zg_ref, o_ref, lse_ref, dy_ref, g_ref, sl_ref, dp_in, dp_ref, dg_ref,
             do_n, dt_n, dq_n, dk_n, dv_n, qp, kp, vp, dop, dtp, lsep, pdq, pdk, pdv):
        del dp_in
        dq_ref, dk_ref, dv_ref, dzg_ref = (dp_ref.at[i] for i in range(4))
        slope = sl_ref[...][:, :1]
        g = g_ref[...]

        def prologue(t, dg):
            rows = pl.ds(t * n, n)
            o = o_ref[rows, :]
            do, dzg, dgi = _head_out_bwd(o, zg_ref[rows, :], g, dy_ref[rows, :])
            dzg_ref[rows, :] = dzg.astype(BF16)
            do_n[rows, :] = do
            dt_n[rows, :] = jnp.broadcast_to(jnp.sum(do * o, axis=-1, keepdims=True), (n, HEAD_DIM))
            return dg + dgi

        dg_ref[...] = functools.reduce(lambda dg, t: prologue(t, dg), range(nt), jnp.zeros((1, HEAD_DIM), F32))
        dq_n[...] = jnp.zeros_like(dq_n)
        dk_n[...] = jnp.zeros_like(dk_n)
        dv_n[...] = jnp.zeros_like(dv_n)
        kp[pl.ds(0, n), :] = jnp.zeros((n, HEAD_DIM), BF16)
        vp[pl.ds(0, n), :] = jnp.zeros((n, HEAD_DIM), BF16)

        for (window, r) in DIL_PAIRS:
            nb = (s // r) // n
            masks = _dil_masks(slope, float(r))
            for nat, per, padded in _dil_chunks(s, r):
                qp[per, :] = q_ref[nat, :].astype(BF16)
                kp[padded, :] = k_ref[nat, :].astype(BF16)
                vp[padded, :] = v_ref[nat, :].astype(BF16)
                dop[per, :] = do_n[nat, :].astype(BF16)
                dtp[per, :] = dt_n[nat, :]
                lsep[per, :] = lse_ref[nat, :]
            pdk[...] = jnp.zeros_like(pdk)
            pdv[...] = jnp.zeros_like(pdv)

            def tiles(tt, _):
                ts = [tt * DIL_GROUP + i for i in range(DIL_GROUP)]
                rows = [pl.ds(pl.multiple_of(t * n, n), n) for t in ts]
                both = [pl.ds(pl.multiple_of(t * n, n), 2 * n) for t in ts]
                q = [qp[rw, :] for rw in rows]
                do = [dop[rw, :] for rw in rows]
                sc = [_dil_scores(qq, kp[bt, :], masks, lax.rem(t, nb) == 0) for t, qq, bt in zip(ts, q, both)]
                dp = [_dot(dd, vp[bt, :], NT) for dd, bt in zip(do, both)]
                p = [jnp.exp(x - lsep[rw, :][:, :1]) for x, rw in zip(sc, rows)]
                ds = [((x * (y - dtp[rw, :][:, :1])) * scale).astype(BF16) for x, y, rw in zip(p, dp, rows)]
                dq = [_dot(x, kp[bt, :], NN) for x, bt in zip(ds, both)]
                dk = [_dot(x, qq, TN) for x, qq in zip(ds, q)]
                dv = [_dot(x.astype(BF16), dd, TN) for x, dd in zip(p, do)]
                for rw, bt, x, y, z in zip(rows, both, dq, dk, dv):
                    pdq[rw, :] = x
                    pdk[bt, :] += y
                    pdv[bt, :] += z
                return 0

            lax.fori_loop(0, nt // DIL_GROUP, tiles, 0)
            for nat, per, padded in _dil_chunks(s, r):
                dq_n[nat, :] += pdq[per, :]
                dk_n[nat, :] += pdk[padded, :]
                dv_n[nat, :] += pdv[padded, :]

        dq_ref[...] = dq_n[...].astype(BF16)
        dk_ref[...] = dk_n[...].astype(BF16)
        dv_ref[...] = dv_n[...].astype(BF16)

    f32_s = pltpu.VMEM((s, HEAD_DIM), F32)
    f32_pad = pltpu.VMEM((s + n, HEAD_DIM), F32)
    bf_s = pltpu.VMEM((s, HEAD_DIM), BF16)
    bf_pad = pltpu.VMEM((s + n, HEAD_DIM), BF16)
    return pl.pallas_call(
        body, name="dil_bwd", grid=(b, n_heads),
        in_specs=[_seg_spec(s, 4), _seg_spec(s, 5), _seg_spec(s, 6), _seg_spec(s, 7),
                  _head_spec(s), _head_spec(s), _seg_spec(s, 1),
                  pl.BlockSpec((1, HEAD_DIM), lambda i, h: (0, h)),
                  pl.BlockSpec((None, 1, HEAD_DIM), lambda i, h: (h, 0, 0)), ANY],
        out_specs=[_seg4_spec(s, 1), pl.BlockSpec((None, 1, HEAD_DIM), lambda i, h: (i, 0, h))],
        out_shape=[jax.ShapeDtypeStruct((8, b, s, w), BF16), jax.ShapeDtypeStruct((b, 1, w), F32)],
        scratch_shapes=[f32_s] * 5 + [bf_s, bf_pad, bf_pad, bf_s] + [f32_s, f32_s, f32_s, f32_pad, f32_pad],
        input_output_aliases={9: 0},
        compiler_params=_cparams(("parallel", "parallel")),
    )(proj8, proj8, proj8, proj8, o_dl, lse_dl, dy2, g_dil, slopes, dproj8)


def _small_update(gathered, n_b, params, m, v):
    n_dev, _, width = gathered.shape

    def body(g_ref, p_ref, m_ref, v_ref, grad_ref, d_ref, nm_ref, nv_ref, loss_ref):
        for row in range(2):
            acc = None
            for dev in range(n_dev):
                for i in range(n_b):
                    term = g_ref[dev, pl.ds(row * n_b + i, 1), :]
                    acc = term if acc is None else acc + term
            grad_ref[pl.ds(row, 1), :] = acc
        loss = g_ref[0, pl.ds(2 * n_b, 1), pl.ds(0, 128)]
        for dev in range(1, n_dev):
            loss = loss + g_ref[dev, pl.ds(2 * n_b, 1), pl.ds(0, 128)]
        loss_ref[...] = loss
        d, nm, nv = _adamw_math(p_ref[...], grad_ref[...], m_ref[...], v_ref[...])
        d_ref[...] = d
        nm_ref[...] = nm
        nv_ref[...] = nv

    sds = jax.ShapeDtypeStruct((2, width), F32)
    return pl.pallas_call(
        body, name="small_update",
        in_specs=[VMEM_SPEC] * 4, out_specs=[VMEM_SPEC] * 5,
        out_shape=[sds, sds, sds, sds, jax.ShapeDtypeStruct((1, 128), F32)],
        compiler_params=_cparams(),
    )(gathered, params, m, v)


def _wada_update(c_t, dmod, w, m, v):
    d, bt = c_t.shape
    _, n = dmod.shape
    tr, tc = _tile(d, 512), _tile(n, 1024)

    def body(c_ref, dm_ref, w_ref, m_ref, v_ref, g_ref, d_ref, nm_ref, nv_ref):
        cv = c_ref[...]
        cs = (cv * _sigmoid(cv)).astype(BF16)
        grad = _dot(cs, dm_ref[...].astype(BF16), NN)
        g_ref[...] = grad
        dl, nm, nv = _adamw_math(w_ref[...], grad, m_ref[...], v_ref[...])
        d_ref[...] = dl
        nm_ref[...] = nm
        nv_ref[...] = nv

    spec = pl.BlockSpec((tr, tc), lambda i, j: (i, j))
    sds = jax.ShapeDtypeStruct((d, n), F32)
    return pl.pallas_call(
        body, name="wada_update", grid=(d // tr, n // tc),
        in_specs=[pl.BlockSpec((tr, bt), lambda i, j: (i, 0)), pl.BlockSpec((bt, tc), lambda i, j: (0, j)),
                  spec, spec, spec],
        out_specs=[spec] * 4, out_shape=[sds] * 4,
        compiler_params=_cparams(("parallel", "parallel")),
    )(c_t, dmod, w, m, v)


def _reduce_begin(gs, tag):
    ra = _sibling_half_swap(gs, "swap_" + tag)
    pa, own = _pair_sum(gs, ra, "pair_sum_" + tag)
    return _ScatterChips(pa), own


def _reduce_finish(rb, own, w, m, v, tag):
    half = _chip_sum(rb, own, "chip_sum_" + tag)
    return _adamw(w, _sibling_join(half, "join_" + tag), m, v, "adamw_" + tag)


def kernel(x, c, w_ada, b_ada, g_norm, w_in, g_sb, g_dil, w_out, g_final, loss_target, m_w_ada, m_b_ada, m_g_norm, m_w_in, m_g_sb, m_g_dil, m_w_out, m_g_final, v_w_ada, v_b_ada, v_g_norm, v_w_in, v_g_sb, v_g_dil, v_w_out, v_g_final):
    nb, s, d = x.shape
    t = nb * s
    na = w_ada.shape[2]
    cs = w_in.shape[2]
    w = cs // 2
    n_heads = w // HEAD_DIM
    r_out = w_out.shape[1]
    assert 2 * nb + 1 <= 8 and 2 * d + 2 * w <= 3 * d and N_CHIPS * na == 3 * d and N_CHIPS * r_out == 2 * w
    xi, yi, ci = _place()
    chip = 2 * xi + yi
    dev = 2 * chip + ci

    c_all = _allgather8(jnp.pad(c, ((0, 8 - nb), (0, 0))), "gather_c")
    c16 = c_all.reshape(N_DEV, 8, d)[:, :nb].reshape(N_DEV * nb, d)
    b_ada_shard = lax.dynamic_slice(b_ada, (0, chip * na), (1, na))
    mod_part = _mod_fwd(c16, w_ada[0], b_ada_shard)
    mod_all = _allgather8(mod_part, "gather_mod")
    mod_full = mod_all.reshape(N_CHIPS, 2, N_DEV * nb, na)[:, 0].transpose(1, 0, 2).reshape(N_DEV * nb, 3 * d)
    mod = lax.dynamic_slice(mod_full, (dev * nb, 0), (nb, 3 * d))
    shift, scale, gate = (mod[:, i * d:(i + 1) * d].reshape(nb, 1, d) for i in range(3))

    h = _norm_mod_fwd(x, g_norm, scale, shift)
    h2 = h.reshape(t, d)
    ws_in_slab, w_in_own = _cast_bf16_slab(w_in[0], "cast_w_in", with_own=True)
    (ws_out_slab,) = _cast_bf16_slab(w_out[0], "cast_w_out")
    proj8, ws_in, w_in_diag, ws_out = _proj_fwd_split(
        h2, w_in_own, _GatherNear(ws_in_slab), _GatherChips(ws_out_slab))
    ws_in = lax.dynamic_update_slice(ws_in, w_in_diag[None], (chip ^ 3, 0, 0))
    proj8 = proj8.reshape(8, nb, s, w)
    w_out_full = ws_out.reshape(2 * w, d)

    slopes = jnp.exp2(-ALIBI_MAX_BIAS * jnp.arange(1, n_heads + 1, dtype=F32) / n_heads)
    slopes = jnp.broadcast_to(slopes[:, None, None], (n_heads, 1, HEAD_DIM))
    o_sb, tot_sb, y2, sb_trips = _sb_fwd(proj8, g_sb)
    o_dl, lse_dl, y2 = _dil_fwd(proj8, g_dil, slopes, y2)
    y2f = y2.reshape(2, t, w)
    out = _out_fwd(y2f, w_out_full)

    dx1, dout, dgate, dg_final, loss_part = _loss_head(
        x, out.reshape(nb, s, d), gate, g_final.reshape(1, d), loss_target)
    dout2 = dout.reshape(t, d)
    gs_out = _out_bwd_w(y2f, dout2).reshape(N_CHIPS, r_out, d)
    scatter_out, own_out = _reduce_begin(gs_out, "w_out")
    dy2 = _out_bwd_y(dout2, w_out_full).reshape(2, nb, s, w)
    dproj8, dg_sb, rb_out = _sb_bwd(proj8, o_sb, tot_sb, sb_trips, dy2, g_sb, fused=scatter_out)
    dproj8, dg_dl = _dil_bwd(proj8, o_dl, lse_dl, dy2, g_dil, slopes, dproj8)
    dproj8 = dproj8.reshape(8, t, w)
    gs_in = _proj_bwd_w(h2, dproj8)
    scatter_in, own_in = _reduce_begin(gs_in, "w_in")
    dh, rb_in = _proj_bwd_x(dproj8, ws_in, fused=scatter_in)
    grad_x, dshift, dscale, dg_norm = _norm_mod_bwd(x, dh.reshape(nb, s, d), dx1, g_norm, scale)

    width = 3 * d
    dmod = jnp.concatenate([dshift, dscale, dgate], axis=-1).reshape(nb, width)
    gains = jnp.concatenate([dg_sb.reshape(nb, w), dg_dl.reshape(nb, w)], axis=-1)
    gains = jnp.pad(gains, ((0, 0), (2 * d, width - 2 * d - 2 * w)))
    first = jnp.pad(jnp.concatenate([dg_norm, dg_final], axis=-1), ((0, nb - 1), (0, width - 2 * d)))
    loss_row = jnp.pad(loss_part, ((0, 0), (0, width - 128)))
    pack = jnp.concatenate([dmod, gains + first, loss_row, jnp.zeros((8 - 2 * nb - 1, width), F32)], axis=0)
    gathered = _allgather8(pack, "gather_small").reshape(N_DEV, 8, width)

    def stack(bias, gn, gf, gsb, gdl):
        row1 = jnp.concatenate([gn.reshape(1, d), gf.reshape(1, d), gsb.reshape(1, w), gdl.reshape(1, w)], axis=-1)
        return jnp.concatenate([bias.reshape(1, width), jnp.pad(row1, ((0, 0), (0, width - 2 * d - 2 * w)))], axis=0)

    small = _small_update(
        gathered, nb, stack(b_ada, g_norm, g_final, g_sb, g_dil),
        stack(m_b_ada, m_g_norm, m_g_final, m_g_sb, m_g_dil), stack(v_b_ada, v_g_norm, v_g_final, v_g_sb, v_g_dil))
    loss = small[4][0, 0]

    def unstack(a):
        return (a[0:1, :], a[1:2, 0:d], a[1, d:2 * d], a[1:2, 2 * d:2 * d + w], a[1:2, 2 * d + w:2 * d + 2 * w])

    (g_b, g_gn, g_gf, g_gsb, g_gdl), (d_b, d_gn, d_gf, d_gsb, d_gdl), (nm_b, nm_gn, nm_gf, nm_gsb, nm_gdl), \
        (nv_b, nv_gn, nv_gf, nv_gsb, nv_gdl) = (unstack(a) for a in small[:4])

    dmod_all = gathered[:, :nb].reshape(N_DEV * nb, width)
    dmod_cols = lax.dynamic_slice(dmod_all, (0, chip * na), (N_DEV * nb, na))
    g_wa, d_wa, nm_wa, nv_wa = _wada_update(c16.T, dmod_cols, w_ada[0], m_w_ada[0], v_w_ada[0])

    g_wi, d_wi, nm_wi, nv_wi = _reduce_finish(rb_in, own_in, w_in[0], m_w_in[0], v_w_in[0], "w_in")
    g_wo, d_wo, nm_wo, nv_wo = _reduce_finish(rb_out, own_out, w_out[0], m_w_out[0], v_w_out[0], "w_out")

    lead = lambda a: a[None]
    return (loss, grad_x,
            lead(g_wa), g_b, g_gn, lead(g_wi), g_gsb, g_gdl, lead(g_wo), g_gf,
            lead(d_wa), d_b, d_gn, lead(d_wi), d_gsb, d_gdl, lead(d_wo), d_gf,
            lead(nm_wa), nm_b, nm_gn, lead(nm_wi), nm_gsb, nm_gdl, lead(nm_wo), nm_gf,
            lead(nv_wa), nv_b, nv_gn, lead(nv_wi), nv_gsb, nv_gdl, lead(nv_wo), nv_gf)
```

```python
import functools
import math

import jax
import jax.numpy as jnp
from jax import lax
from jax.experimental import pallas as pl
from jax.experimental.pallas import tpu as pltpu

F32 = jnp.float32
BF16 = jnp.bfloat16
MESH = pl.DeviceIdType.MESH

HEAD_DIM = 128
EPS = 1e-6
DIL_PAIRS = ((128, 1), (512, 4), (2048, 16))
ALIBI_MAX_BIAS = 8.0
ADAM_LR = 0.001
ADAM_B1 = 0.9
ADAM_B2 = 0.999
ADAM_EPS = 1e-08
ADAM_WD = 0.01
ADAM_STEP = 10
N_CHIPS = 4
N_DEV = 8
VMEM_LIMIT_BYTES = 56 * 1024 * 1024
NEG_BIG = -1e30

NN = (((1,), (0,)), ((), ()))
NT = (((1,), (1,)), ((), ()))
TN = (((0,), (0,)), ((), ()))

ANY = pl.BlockSpec(memory_space=pl.ANY)
VMEM_SPEC = pl.BlockSpec(memory_space=pltpu.VMEM)


def _cparams(sem=None):
    return pltpu.CompilerParams(dimension_semantics=sem, vmem_limit_bytes=VMEM_LIMIT_BYTES)


def _tile(dim, pref):
    t = min(dim, pref)
    assert dim % t == 0, (dim, pref)
    return t


def _dot(a, b, dims):
    return lax.dot_general(a, b, dims, preferred_element_type=F32)


def _sigmoid(x):
    return 1.0 / (1.0 + jnp.exp(-x))


def _place():
    return lax.axis_index("x"), lax.axis_index("y"), lax.axis_index("c")


def _allgather8(x_shard, name):
    m_per, n = x_shard.shape

    def body(x_ref, out_ref, send_sems, recv_sems, local_sem):
        x, y, c = _place()
        me, sibling = (x, y, c), (x, y, 1 - c)
        chips = [(1 - x, y), (x, 1 - y), (1 - x, 1 - y)]

        def rows(px, py, pc):
            return out_ref.at[pl.ds((4 * px + 2 * py + pc) * m_per, m_per), :]

        def copy(k, block, to, src=None):
            return pltpu.make_async_remote_copy(
                src_ref=rows(*block) if src is None else src, dst_ref=rows(*block),
                send_sem=send_sems.at[k], recv_sem=recv_sems.at[k], device_id=to, device_id_type=MESH)

        mine = pltpu.make_async_copy(x_ref, rows(*me), local_sem)
        mine.start()
        first = [copy(0, me, sibling, src=x_ref)]
        first += [copy(1 + j, me, (*chip, c), src=x_ref) for j, chip in enumerate(chips)]
        for cp in first:
            cp.start()
        passed = [copy(4 + j, (*chip, c), sibling) for j, chip in enumerate(chips)]
        for j, chip in enumerate(chips):
            copy(1 + j, (*chip, c), me).wait_recv()
            passed[j].start()
        copy(0, sibling, me).wait_recv()
        for j, chip in enumerate(chips):
            copy(4 + j, (*chip, 1 - c), me).wait_recv()
        for cp in first + passed:
            cp.wait_send()
        mine.wait()

    return pl.pallas_call(
        body, name=name,
        out_shape=jax.ShapeDtypeStruct((N_DEV * m_per, n), x_shard.dtype),
        in_specs=[VMEM_SPEC], out_specs=VMEM_SPEC,
        scratch_shapes=[pltpu.SemaphoreType.DMA((7,)), pltpu.SemaphoreType.DMA((7,)), pltpu.SemaphoreType.DMA],
    )(x_shard)


class _GatherChips:
    def __init__(self, ws):
        self.inputs = [ws]
        self.out_shapes = [jax.ShapeDtypeStruct(ws.shape, ws.dtype)]
        self.aliases = {0: 0}
        self.scratch = [pltpu.SemaphoreType.DMA((12,)), pltpu.SemaphoreType.DMA((12,))]
        self.quarter = ws.shape[1] // 4

    def _copy(self, refs, k, chip, pc, part, to):
        _, out_ref, send_sems, recv_sems = refs
        rows = out_ref.at[2 * chip[0] + chip[1], pl.ds((2 * pc + part) * self.quarter, self.quarter), :]
        return pltpu.make_async_remote_copy(
            src_ref=rows, dst_ref=rows, send_sem=send_sems.at[k], recv_sem=recv_sems.at[k],
            device_id=to, device_id_type=MESH)

    def _sends(self, refs, phase):
        x, y, c = _place()
        sibling, x_nbr, y_nbr, diag = (x, y, 1 - c), (1 - x, y), (x, 1 - y), (1 - x, 1 - y)
        plan = {
            "start": [(0, (x, y), 0, (*x_nbr, c)), (1, (x, y), 1, (*y_nbr, c)),
                      (2, (x, y), 1, (*x_nbr, c)), (3, (x, y), 0, (*y_nbr, c))],
            "middle": [(4, x_nbr, 0, (*y_nbr, c)), (6, x_nbr, 0, sibling), (5, y_nbr, 1, (*x_nbr, c)),
                       (7, y_nbr, 1, sibling), (8, x_nbr, 1, sibling), (9, y_nbr, 0, sibling)],
            "wait": [(10, diag, 0, sibling), (11, diag, 1, sibling)],
        }[phase]
        return [self._copy(refs, k, chip, c, part, to) for k, chip, part, to in plan]

    def _landings(self, refs, phase):
        x, y, c = _place()
        me, x_nbr, y_nbr, diag = (x, y, c), (1 - x, y), (x, 1 - y), (1 - x, 1 - y)
        plan = {
            "middle": [(0, x_nbr, c, 0), (1, y_nbr, c, 1), (2, x_nbr, c, 1), (3, y_nbr, c, 0)],
            "wait": [(4, diag, c, 0), (5, diag, c, 1)],
            "sibling": [(6, x_nbr, 1 - c, 0), (7, y_nbr, 1 - c, 1), (8, x_nbr, 1 - c, 1), (9, y_nbr, 1 - c, 0),
                        (10, diag, 1 - c, 0), (11, diag, 1 - c, 1)],
        }[phase]
        return [self._copy(refs, k, chip, pc, part, me) for k, chip, pc, part in plan]

    def start(self, *refs):
        for cp in self._sends(refs, "start"):
            cp.start()

    def middle(self, *refs):
        landed = self._landings(refs, "middle")
        passed = self._sends(refs, "middle")
        landed[0].wait_recv()
        passed[0].start()
        passed[1].start()
        landed[1].wait_recv()
        passed[2].start()
        passed[3].start()
        landed[2].wait_recv()
        passed[4].start()
        landed[3].wait_recv()
        passed[5].start()

    def wait(self, *refs):
        landed = self._landings(refs, "wait")
        passed = self._sends(refs, "wait")
        for arrival, cp in zip(landed, passed):
            arrival.wait_recv()
            cp.start()
        for arrival in self._landings(refs, "sibling"):
            arrival.wait_recv()
        for phase in ("start", "middle", "wait"):
            for cp in self._sends(refs, phase):
                cp.wait_send()


class _GatherNear(_GatherChips):
    middle = None

    def wait(self, *refs):
        x, y, c = _place()
        sibling, x_nbr, y_nbr = (x, y, 1 - c), (1 - x, y), (x, 1 - y)
        plan = [(6, x_nbr, 0), (7, y_nbr, 1), (8, x_nbr, 1), (9, y_nbr, 0)]
        passed = [self._copy(refs, k, chip, c, part, sibling) for k, chip, part in plan]
        for arrival, cp in zip(self._landings(refs, "middle"), passed):
            arrival.wait_recv()
            cp.start()
        for k, chip, part in plan:
            self._copy(refs, k, chip, 1 - c, part, (x, y, c)).wait_recv()
        for cp in self._sends(refs, "start"):
            cp.wait_send()
        for k, chip, part in plan:
            self._copy(refs, k, chip, c, part, sibling).wait_send()


class _RelayDiag:
    def __init__(self, ws):
        self.inputs = [ws]
        self.out_shapes = [jax.ShapeDtypeStruct(ws.shape[1:], ws.dtype)]
        self.scratch = [pltpu.SemaphoreType.DMA((4,)), pltpu.SemaphoreType.DMA((4,))]
        self.quarter = ws.shape[1] // 4

    def _rows(self, pc, part):
        return pl.ds((2 * pc + part) * self.quarter, self.quarter)

    def _relays(self, refs):
        ws_ref, wd_ref, send_sems, recv_sems = refs
        x, y, c = _place()
        out = []
        for k, (src_chip, to) in enumerate([((1 - x, y), (x, 1 - y)), ((x, 1 - y), (1 - x, y))]):
            out.append(pltpu.make_async_remote_copy(
                src_ref=ws_ref.at[2 * src_chip[0] + src_chip[1], self._rows(c, k), :],
                dst_ref=wd_ref.at[self._rows(c, k), :],
                send_sem=send_sems.at[k], recv_sem=recv_sems.at[k], device_id=(*to, c), device_id_type=MESH))
        return out

    def _local(self, refs, k, pc, to):
        _, wd_ref, send_sems, recv_sems = refs
        rows = wd_ref.at[self._rows(pc, k % 2), :]
        return pltpu.make_async_remote_copy(
            src_ref=rows, dst_ref=rows, send_sem=send_sems.at[k], recv_sem=recv_sems.at[k],
            device_id=to, device_id_type=MESH)

    def start(self, *refs):
        for cp in self._relays(refs):
            cp.start()

    def wait(self, *refs):
        x, y, c = _place()
        me, sibling = (x, y, c), (x, y, 1 - c)
        for k in (0, 1):
            self._local(refs, k, c, me).wait_recv()
            self._local(refs, 2 + k, c, sibling).start()
        for k in (0, 1):
            self._local(refs, 2 + k, 1 - c, me).wait_recv()
        for cp in self._relays(refs):
            cp.wait_send()
        for k in (0, 1):
            self._local(refs, 2 + k, c, sibling).wait_send()


def _sibling_half_swap(gs, name):
    n, r, cdim = gs.shape
    half = r // 2

    def body(g_ref, out_ref, send_sem, recv_sem):
        x, y, c = _place()
        cp = pltpu.make_async_remote_copy(
            src_ref=g_ref.at[:, pl.ds((1 - c) * half, half), :], dst_ref=out_ref,
            send_sem=send_sem, recv_sem=recv_sem, device_id=(x, y, 1 - c), device_id_type=MESH)
        cp.start()
        cp.wait()

    return pl.pallas_call(
        body, name=name,
        out_shape=jax.ShapeDtypeStruct((n, half, cdim), gs.dtype),
        in_specs=[ANY], out_specs=ANY,
        scratch_shapes=[pltpu.SemaphoreType.DMA, pltpu.SemaphoreType.DMA],
    )(gs)


class _ScatterChips:
    def __init__(self, pa):
        self.inputs = [pa]
        self.out_shapes = [jax.ShapeDtypeStruct(pa.shape, pa.dtype)]
        self.scratch = [pltpu.SemaphoreType.DMA((3,)), pltpu.SemaphoreType.DMA((3,)), pltpu.SemaphoreType.DMA]

    @staticmethod
    def _mine(p_ref, out_ref, send_sems, recv_sems, local_sem):
        x, y, _ = _place()
        return pltpu.make_async_copy(p_ref.at[2 * x + y], out_ref.at[2 * x + y], local_sem)

    @staticmethod
    def _remote(p_ref, out_ref, send_sems, recv_sems, local_sem, incoming):
        x, y, c = _place()
        me = 2 * x + y
        remote = []
        for j, (px, py) in enumerate([(1 - x, y), (x, 1 - y), (1 - x, 1 - y)]):
            remote.append(pltpu.make_async_remote_copy(
                src_ref=p_ref.at[me if incoming else 2 * px + py], dst_ref=out_ref.at[2 * px + py if incoming else me],
                send_sem=send_sems.at[j], recv_sem=recv_sems.at[j], device_id=(px, py, c), device_id_type=MESH))
        return remote

    def start(self, *refs):
        self._mine(*refs).start()
        for cp in self._remote(*refs, incoming=False):
            cp.start()

    def wait(self, *refs):
        for cp in self._remote(*refs, incoming=True):
            cp.wait_recv()
        for cp in self._remote(*refs, incoming=False):
            cp.wait_send()
        self._mine(*refs).wait()


def _fused_specs(fused):
    if fused is None:
        return [], [], [], [], []
    return (list(fused.inputs), [ANY] * len(fused.inputs), list(fused.out_shapes), [ANY] * len(fused.out_shapes),
            list(fused.scratch))


def _fused_aliases(fused, first_input, first_output):
    aliases = getattr(fused, "aliases", {}) if fused is not None else {}
    return {first_input + i: first_output + o for i, o in aliases.items()}


def _fused_begin(fused, grid, refs):
    if fused is not None:
        first = functools.reduce(lambda p, q: p & q, [pl.program_id(i) == 0 for i in range(len(grid))])
        pl.when(first)(lambda: fused.start(*refs))
        if callable(getattr(fused, "middle", None)):
            step = functools.reduce(lambda acc, ig: acc * ig[1] + pl.program_id(ig[0]), enumerate(grid), 0)
            pl.when(step == math.prod(grid) // 2)(lambda: fused.middle(*refs))


def _fused_end(fused, grid, refs):
    if fused is not None:
        last = functools.reduce(lambda p, q: p & q, [pl.program_id(i) == g - 1 for i, g in enumerate(grid)])
        pl.when(last)(lambda: fused.wait(*refs))


def _sibling_join(full, name):
    h2, cdim = full.shape
    h = h2 // 2

    def body(in_ref, out_ref, send_sem, recv_sem):
        del in_ref
        x, y, c = _place()
        mine = out_ref.at[pl.ds(c * h, h), :]
        cp = pltpu.make_async_remote_copy(
            src_ref=mine, dst_ref=mine, send_sem=send_sem, recv_sem=recv_sem,
            device_id=(x, y, 1 - c), device_id_type=MESH)
        cp.start()
        theirs = out_ref.at[pl.ds((1 - c) * h, h), :]
        pltpu.make_async_remote_copy(
            src_ref=theirs, dst_ref=theirs, send_sem=send_sem, recv_sem=recv_sem,
            device_id=(x, y, 1 - c), device_id_type=MESH).wait_recv()
        cp.wait_send()

    return pl.pallas_call(
        body, name=name,
        out_shape=jax.ShapeDtypeStruct(full.shape, full.dtype),
        in_specs=[ANY], out_specs=ANY, input_output_aliases={0: 0},
        scratch_shapes=[pltpu.SemaphoreType.DMA, pltpu.SemaphoreType.DMA],
    )(full)


def _cast_bf16_slab(w, name, with_own=False):
    r, cdim = w.shape
    tr, tc = _tile(r, 512), _tile(cdim, 2048)

    def body(pc_ref, w_ref, o_ref, *own_ref):
        o_ref[...] = w_ref[...].astype(BF16)
        for ref in own_ref:
            ref[...] = w_ref[...].astype(BF16)

    plain = pl.BlockSpec((tr, tc), lambda i, j, pc: (i, j))
    return pl.pallas_call(
        body, name=name,
        grid_spec=pltpu.PrefetchScalarGridSpec(
            num_scalar_prefetch=1, grid=(r // tr, cdim // tc),
            in_specs=[plain],
            out_specs=[pl.BlockSpec((None, tr, tc), lambda i, j, pc: (pc[1], i, j))] + [plain] * with_own),
        out_shape=[jax.ShapeDtypeStruct((N_CHIPS, r, cdim), BF16)] + [jax.ShapeDtypeStruct((r, cdim), BF16)] * with_own,
        compiler_params=_cparams(("parallel", "parallel")),
    )(_place_scalars(), w)


def _place_scalars():
    x, y, c = _place()
    return jnp.stack([c, 2 * x + y]).astype(jnp.int32)


def _pair_sum(gs, ra, name):
    n, r, cdim = gs.shape
    half = r // 2
    tr, tc = _tile(half, 512), _tile(cdim, 2048)
    nt = half // tr

    def body(pc_ref, g_ref, r_ref, o_ref, own_ref):
        val = g_ref[...].astype(F32) + r_ref[...].astype(F32)
        o_ref[...] = val.astype(BF16)

        @pl.when(pl.program_id(2) == pc_ref[1])
        def _():
            own_ref[...] = val

    return pl.pallas_call(
        body, name=name,
        grid_spec=pltpu.PrefetchScalarGridSpec(
            num_scalar_prefetch=1, grid=(nt, cdim // tc, n),
            in_specs=[pl.BlockSpec((None, tr, tc), lambda i, j, s, pc: (s, pc[0] * nt + i, j)),
                      pl.BlockSpec((None, tr, tc), lambda i, j, s, pc: (s, i, j))],
            out_specs=[pl.BlockSpec((None, tr, tc), lambda i, j, s, pc: (s, i, j)),
                       pl.BlockSpec((tr, tc), lambda i, j, s, pc: (i, j))]),
        out_shape=[jax.ShapeDtypeStruct((n, half, cdim), BF16), jax.ShapeDtypeStruct((half, cdim), F32)],
        compiler_params=_cparams(("parallel", "parallel", "arbitrary")),
    )(_place_scalars(), gs, ra)


def _chip_sum(rb, own, name):
    n, h, cdim = rb.shape
    tr, tc = _tile(h, 256), _tile(cdim, 2048)
    nt = h // tr

    def body(pc_ref, r_ref, own_ref, o_ref):
        chip = pc_ref[1]
        acc = None
        for p in range(n):
            term = jnp.where(chip == p, own_ref[...], r_ref[p].astype(F32))
            acc = term if acc is None else acc + term
        o_ref[...] = acc

    return pl.pallas_call(
        body, name=name,
        grid_spec=pltpu.PrefetchScalarGridSpec(
            num_scalar_prefetch=1, grid=(nt, cdim // tc),
            in_specs=[pl.BlockSpec((n, tr, tc), lambda i, j, pc: (0, i, j)),
                      pl.BlockSpec((tr, tc), lambda i, j, pc: (i, j))],
            out_specs=pl.BlockSpec((tr, tc), lambda i, j, pc: (pc[0] * nt + i, j))),
        out_shape=jax.ShapeDtypeStruct((2 * h, cdim), F32),
        compiler_params=_cparams(("parallel", "parallel")),
    )(_place_scalars(), rb, own)


def _adamw_math(w, g, m, v):
    m = ADAM_B1 * m + (1.0 - ADAM_B1) * g
    v = ADAM_B2 * v + (1.0 - ADAM_B2) * (g * g)
    m_hat = m / (1.0 - ADAM_B1 ** ADAM_STEP)
    v_hat = v / (1.0 - ADAM_B2 ** ADAM_STEP)
    delta = -ADAM_LR * (m_hat / (jnp.sqrt(v_hat) + ADAM_EPS) + ADAM_WD * w)
    return delta, m, v


def _adamw(w, g, m, v, name):
    r, cdim = w.shape
    tr, tc = _tile(r, 256), _tile(cdim, 2048)

    def body(w_ref, g_ref, m_ref, v_ref, go_ref, d_ref, nm_ref, nv_ref):
        gv = g_ref[...]
        d, nm, nv = _adamw_math(w_ref[...], gv, m_ref[...], v_ref[...])
        go_ref[...] = gv
        d_ref[...] = d
        nm_ref[...] = nm
        nv_ref[...] = nv

    spec = pl.BlockSpec((tr, tc), lambda i, j: (i, j))
    sds = jax.ShapeDtypeStruct((r, cdim), F32)
    return pl.pallas_call(
        body, name=name, grid=(r // tr, cdim // tc),
        in_specs=[spec] * 4, out_specs=[spec] * 4, out_shape=[sds] * 4,
        compiler_params=_cparams(("parallel", "parallel")),
    )(w, g, m, v)


def _matmul(a, b, *, grid, a_spec, b_spec, out_spec, out_shape, acc_shape, dims, name, bias=None, bias_spec=None,
            silu_a=False, fused=None):
    nk = grid[2]
    f_in, f_in_specs, f_out, f_out_specs, f_scratch = _fused_specs(fused)
    n_in = 2 + (bias is not None)

    acc_scratch = [pltpu.VMEM(acc_shape, F32)] if nk > 1 else []

    def body(*refs):
        a_ref, b_ref = refs[:2]
        bias_ref = refs[2] if bias is not None else None
        o_ref = refs[n_in + len(f_in)]
        n_fixed = n_in + len(f_in) + 1 + len(f_out)
        f_refs = (*refs[n_in:n_in + len(f_in)], *refs[n_in + len(f_in) + 1:n_fixed],
                  *refs[n_fixed + len(acc_scratch):])
        _fused_begin(fused, grid, f_refs)

        def product():
            if len(a_ref.shape) == 3:
                tks = a_ref.shape[2]
                parts = [_dot(a_ref[i], b_ref[:, i * tks:(i + 1) * tks], dims) for i in range(a_ref.shape[0])]
                return functools.reduce(lambda p, q: p + q, parts)
            av = a_ref[...]
            if silu_a:
                av = av * _sigmoid(av)
            return _dot(av.astype(BF16), b_ref[...].astype(BF16), dims)

        def finish(res):
            if bias is not None:
                res = res + bias_ref[...]
            o_ref[...] = res.astype(o_ref.dtype)

        if nk == 1:
            finish(product())
        else:
            acc_ref = refs[n_fixed]
            k = pl.program_id(2)

            @pl.when(k == 0)
            def _():
                acc_ref[...] = product()

            if nk > 2:
                @pl.when((k > 0) & (k < nk - 1))
                def _():
                    acc_ref[...] += product()

            @pl.when(k == nk - 1)
            def _():
                finish(acc_ref[...] + product())

        _fused_end(fused, grid, f_refs)

    in_specs = [a_spec, b_spec] + ([] if bias is None else [bias_spec]) + f_in_specs
    args = (a, b) + (() if bias is None else (bias,)) + tuple(f_in)
    sem = ("parallel", "parallel", "arbitrary") if fused is None else ("arbitrary",) * 3
    res = pl.pallas_call(
        body, name=name, grid=grid, in_specs=in_specs, out_specs=[out_spec] + f_out_specs,
        out_shape=[out_shape] + f_out,
        scratch_shapes=acc_scratch + f_scratch,
        input_output_aliases=_fused_aliases(fused, n_in, 1),
        compiler_params=_cparams(sem),
    )(*args)
    return res[0] if fused is None else tuple(res)


def _mm_tiles(m, n, k):
    return _tile(m, 1024), _tile(n, 1024), _tile(k, 4096)


def _proj_part(h2, wmat, n_seg, w_block, seg_of, w, name, carry=None, fused=None):
    t, d = h2.shape
    tm, tn, _ = _mm_tiles(t, w, d)
    npseg = w // tn
    grid = (t // tm, n_seg * npseg)
    f_in, f_in_specs, f_out, f_out_specs, f_scratch = _fused_specs(fused)
    n_carry = carry is not None

    def body(place_ref, a_ref, b_ref, *refs):
        del place_ref
        o_ref = refs[n_carry + len(f_in)]
        f_refs = (*refs[n_carry:n_carry + len(f_in)], *refs[n_carry + len(f_in) + 1:])
        _fused_begin(fused, grid, f_refs)
        o_ref[...] = _dot(a_ref[...], b_ref[...], NN)
        _fused_end(fused, grid, f_refs)

    w_spec = pl.BlockSpec((d, tn) if wmat.ndim == 2 else (None, d, tn),
                          lambda m, n, place: w_block(n // npseg, n % npseg, place))
    res = pl.pallas_call(
        body, name=name,
        grid_spec=pltpu.PrefetchScalarGridSpec(
            num_scalar_prefetch=1, grid=grid,
            in_specs=[pl.BlockSpec((tm, d), lambda m, n, place: (m, 0)), w_spec] + [ANY] * n_carry + f_in_specs,
            out_specs=[pl.BlockSpec((None, tm, tn), lambda m, n, place: (seg_of(n // npseg, place), m, n % npseg))]
            + f_out_specs,
            scratch_shapes=f_scratch),
        out_shape=[jax.ShapeDtypeStruct((8, t, w), F32)] + f_out,
        input_output_aliases={**({3: 0} if n_carry else {}), **_fused_aliases(fused, 3 + n_carry, 1)},
        compiler_params=_cparams(("arbitrary", "arbitrary")),
    )(_place_scalars(), h2, wmat, *([carry] if n_carry else []), *f_in)
    return tuple(res)


def _proj_fwd_split(h2, w_own, gather_in):
    w = w_own.shape[1] // 2
    npseg = w // _mm_tiles(h2.shape[0], w, h2.shape[1])[1]
    proj8, ws_in = _proj_part(
        h2, w_own, 2, lambda j, i, place: (0, j * npseg + i), lambda j, place: 2 * place[1] + j, w,
        "proj_fwd_own", fused=gather_in)

    def near(j, place):
        return place[1] ^ (2 - j // 2)

    proj8, w_diag = _proj_part(
        h2, ws_in, 4, lambda j, i, place: (near(j, place), 0, (j % 2) * npseg + i),
        lambda j, place: 2 * near(j, place) + j % 2, w, "proj_fwd_near", carry=proj8, fused=_RelayDiag(ws_in))
    (proj8,) = _proj_part(
        h2, w_diag, 2, lambda j, i, place: (0, j * npseg + i), lambda j, place: 2 * (place[1] ^ 3) + j, w,
        "proj_fwd_diag", carry=proj8)
    return proj8, ws_in, w_diag


def _proj_bwd_x(dproj8, ws_in, fused=None):
    _, t, w = dproj8.shape
    _, d, cs = ws_in.shape
    tm, tn, _ = _mm_tiles(t, d, w)
    return _matmul(
        dproj8, ws_in, grid=(t // tm, d // tn, N_CHIPS), dims=NT, name="proj_bwd_x", fused=fused,
        a_spec=pl.BlockSpec((2, tm, w), lambda m, n, k: (k, m, 0)),
        b_spec=pl.BlockSpec((None, tn, cs), lambda m, n, k: (k, n, 0)),
        out_spec=pl.BlockSpec((tm, tn), lambda m, n, k: (m, n)),
        out_shape=jax.ShapeDtypeStruct((t, d), F32), acc_shape=(tm, tn))


def _proj_bwd_w(h2, dproj8):
    t, d = h2.shape
    _, _, w = dproj8.shape
    cs = 2 * w
    tm, tn, tk = _mm_tiles(d, w, t)
    nps, npseg = cs // tn, w // tn
    return _matmul(
        h2, dproj8, grid=(d // tm, 8 * npseg, t // tk), dims=TN, name="proj_bwd_w",
        a_spec=pl.BlockSpec((tk, tm), lambda m, n, k: (k, m)),
        b_spec=pl.BlockSpec((None, tk, tn), lambda m, n, k: (n // npseg, k, n % npseg)),
        out_spec=pl.BlockSpec((None, tm, tn), lambda m, n, k: (n // nps, m, n % nps)),
        out_shape=jax.ShapeDtypeStruct((N_CHIPS, d, cs), BF16), acc_shape=(tm, tn))


def _out_fwd(y2, w_out):
    _, t, w = y2.shape
    _, d = w_out.shape
    tm, tn, tk = _mm_tiles(t, d, w)
    kpg = w // tk
    return _matmul(
        y2, w_out, grid=(t // tm, d // tn, 2 * kpg), dims=NN, name="out_fwd",
        a_spec=pl.BlockSpec((None, tm, tk), lambda m, n, k: (k // kpg, m, k % kpg)),
        b_spec=pl.BlockSpec((tk, tn), lambda m, n, k: (k, n)),
        out_spec=pl.BlockSpec((tm, tn), lambda m, n, k: (m, n)),
        out_shape=jax.ShapeDtypeStruct((t, d), F32), acc_shape=(tm, tn))


def _out_bwd_y(dout, w_out):
    t, d = dout.shape
    w = w_out.shape[0] // 2
    tm, tn, tk = _mm_tiles(t, w, d)
    npg = w // tn
    return _matmul(
        dout, w_out, grid=(t // tm, 2 * npg, d // tk), dims=NT, name="out_bwd_y",
        a_spec=pl.BlockSpec((tm, tk), lambda m, n, k: (m, k)),
        b_spec=pl.BlockSpec((tn, tk), lambda m, n, k: (n, k)),
        out_spec=pl.BlockSpec((None, tm, tn), lambda m, n, k: (n // npg, m, n % npg)),
        out_shape=jax.ShapeDtypeStruct((2, t, w), F32), acc_shape=(tm, tn))


def _out_bwd_w(y2, dout):
    _, t, w = y2.shape
    _, d = dout.shape
    tm, tn, tk = _mm_tiles(w, d, t)
    mpg = w // tm
    return _matmul(
        y2, dout, grid=(2 * mpg, d // tn, t // tk), dims=TN, name="out_bwd_w",
        a_spec=pl.BlockSpec((None, tk, tm), lambda m, n, k: (m // mpg, k, m % mpg)),
        b_spec=pl.BlockSpec((tk, tn), lambda m, n, k: (k, n)),
        out_spec=pl.BlockSpec((tm, tn), lambda m, n, k: (m, n)),
        out_shape=jax.ShapeDtypeStruct((2 * w, d), BF16), acc_shape=(tm, tn))


def _mod_fwd(c_all, w_ada, b_ada):
    bt, d = c_all.shape
    _, n = w_ada.shape
    tn, tk = _tile(n, 512), _tile(d, 1024)
    return _matmul(
        c_all, w_ada, grid=(1, n // tn, d // tk), dims=NN, name="mod_fwd", silu_a=True,
        a_spec=pl.BlockSpec((bt, tk), lambda i, j, l: (0, l)),
        b_spec=pl.BlockSpec((tk, tn), lambda i, j, l: (l, j)),
        bias=b_ada, bias_spec=pl.BlockSpec((1, tn), lambda i, j, l: (0, j)),
        out_spec=pl.BlockSpec((bt, tn), lambda i, j, l: (0, j)),
        out_shape=jax.ShapeDtypeStruct((bt, n), F32), acc_shape=(bt, tn))


def _norm_mod_fwd(x, g_norm, scale, shift):
    b, s, d = x.shape
    ts = _tile(s, 256)

    def body(x_ref, g_ref, sc_ref, sh_ref, h_ref):
        xv = x_ref[...]
        r = lax.rsqrt(jnp.mean(xv * xv, axis=-1, keepdims=True) + EPS)
        y = (xv * r) * g_ref[...]
        h_ref[...] = (y * (1.0 + sc_ref[...]) + sh_ref[...]).astype(BF16)

    row = pl.BlockSpec((None, ts, d), lambda i, j: (i, j, 0))
    per_b = pl.BlockSpec((None, 1, d), lambda i, j: (i, 0, 0))
    return pl.pallas_call(
        body, name="norm_mod_fwd", grid=(b, s // ts),
        in_specs=[row, pl.BlockSpec((1, d), lambda i, j: (0, 0)), per_b, per_b],
        out_specs=row, out_shape=jax.ShapeDtypeStruct((b, s, d), BF16),
        compiler_params=_cparams(("parallel", "parallel")),
    )(x, g_norm, scale, shift)


def _norm_mod_bwd(x, dh, dx1, g_norm, scale):
    b, s, d = x.shape
    ts = _tile(s, 256)

    def body(x_ref, dh_ref, dx1_ref, g_ref, sc_ref, gx_ref, dsh_ref, dsc_ref, dg_ref):
        i, j = pl.program_id(0), pl.program_id(1)

        @pl.when(j == 0)
        def _():
            dsh_ref[...] = jnp.zeros_like(dsh_ref)
            dsc_ref[...] = jnp.zeros_like(dsc_ref)

        @pl.when((i == 0) & (j == 0))
        def _():
            dg_ref[...] = jnp.zeros_like(dg_ref)

        xv, dhv, g = x_ref[...], dh_ref[...], g_ref[...]
        r = lax.rsqrt(jnp.mean(xv * xv, axis=-1, keepdims=True) + EPS)
        xh = xv * r
        dsh_ref[...] += jnp.sum(dhv, axis=0, keepdims=True)
        dsc_ref[...] += jnp.sum(dhv * (xh * g), axis=0, keepdims=True)
        dn = dhv * (1.0 + sc_ref[...])
        dg_ref[...] += jnp.sum(dn * xh, axis=0, keepdims=True)
        u = dn * g
        dx = r * u - xv * (r * r * r) * jnp.mean(u * xv, axis=-1, keepdims=True)
        gx_ref[...] = dx1_ref[...] + dx

    row = pl.BlockSpec((None, ts, d), lambda i, j: (i, j, 0))
    per_b = pl.BlockSpec((None, 1, d), lambda i, j: (i, 0, 0))
    vec = pl.BlockSpec((1, d), lambda i, j: (0, 0))
    return pl.pallas_call(
        body, name="norm_mod_bwd", grid=(b, s // ts),
        in_specs=[row, row, row, vec, per_b],
        out_specs=[row, per_b, per_b, vec],
        out_shape=[jax.ShapeDtypeStruct((b, s, d), F32), jax.ShapeDtypeStruct((b, 1, d), F32),
                   jax.ShapeDtypeStruct((b, 1, d), F32), jax.ShapeDtypeStruct((1, d), F32)],
        compiler_params=_cparams(("arbitrary", "arbitrary")),
    )(x, dh, dx1, g_norm, scale)


def _loss_head(x, out, gate, g_final, target):
    b, s, d = x.shape
    ts = _tile(s, 256)

    def body(x_ref, o_ref, gt_ref, g_ref, t_ref, dx1_ref, dout_ref, dgt_ref, dg_ref, loss_ref):
        i, j = pl.program_id(0), pl.program_id(1)

        @pl.when(j == 0)
        def _():
            dgt_ref[...] = jnp.zeros_like(dgt_ref)

        @pl.when((i == 0) & (j == 0))
        def _():
            dg_ref[...] = jnp.zeros_like(dg_ref)
            loss_ref[...] = jnp.zeros_like(loss_ref)

        ov, gt, g = o_ref[...], gt_ref[...], g_ref[...]
        x1 = x_ref[...] + gt * ov
        r = lax.rsqrt(jnp.mean(x1 * x1, axis=-1, keepdims=True) + EPS)
        xh = x1 * r
        err = xh * g - t_ref[...]
        loss_ref[...] += 0.5 * jnp.sum(jnp.mean(err * err, axis=-1, keepdims=True))
        dfin = err * (1.0 / d)
        dg_ref[...] += jnp.sum(dfin * xh, axis=0, keepdims=True)
        u = dfin * g
        dx1 = r * u - x1 * (r * r * r) * jnp.mean(u * x1, axis=-1, keepdims=True)
        dx1_ref[...] = dx1
        dgt_ref[...] += jnp.sum(dx1 * ov, axis=0, keepdims=True)
        dout_ref[...] = (gt * dx1).astype(BF16)

    row = pl.BlockSpec((None, ts, d), lambda i, j: (i, j, 0))
    per_b = pl.BlockSpec((None, 1, d), lambda i, j: (i, 0, 0))
    vec = pl.BlockSpec((1, d), lambda i, j: (0, 0))
    return pl.pallas_call(
        body, name="loss_head", grid=(b, s // ts),
        in_specs=[row, row, per_b, vec, row],
        out_specs=[row, row, per_b, vec, pl.BlockSpec((1, 128), lambda i, j: (0, 0))],
        out_shape=[jax.ShapeDtypeStruct((b, s, d), F32), jax.ShapeDtypeStruct((b, s, d), BF16),
                   jax.ShapeDtypeStruct((b, 1, d), F32), jax.ShapeDtypeStruct((1, d), F32),
                   jax.ShapeDtypeStruct((1, 128), F32)],
        compiler_params=_cparams(("arbitrary", "arbitrary")),
    )(x, out, gate, g_final, target)


def _head_out(o, zg, g):
    rinv = lax.rsqrt(jnp.mean(o * o, axis=-1, keepdims=True) + EPS)
    return ((o * rinv) * g) * (zg * _sigmoid(zg))


def _head_out_bwd(o, zg, g, dy):
    rinv = lax.rsqrt(jnp.mean(o * o, axis=-1, keepdims=True) + EPS)
    rn = o * rinv
    sg = _sigmoid(zg)
    sil = zg * sg
    dzg = dy * (rn * g) * (sg * (1.0 + zg * (1.0 - sg)))
    dg = jnp.sum(dy * rn * sil, axis=0, keepdims=True)
    drn = dy * g * sil
    do = rinv * drn - o * (rinv * rinv * rinv) * jnp.mean(drn * o, axis=-1, keepdims=True)
    return do, dzg, dg


def _head_spec(s):
    return pl.BlockSpec((None, s, HEAD_DIM), lambda b, h: (b, 0, h))


def _seg_spec(s, seg):
    return pl.BlockSpec((None, None, s, HEAD_DIM), lambda b, h: (seg, b, 0, h))


def _seg4_spec(s, group):
    return pl.BlockSpec((4, None, s, HEAD_DIM), lambda b, h: (group, b, 0, h))


SB_Q_BLOCK = 512
SB_K_BLOCK = 256


SB_DEAD_LOG2 = -160.0
LOG2_E = 1.4426950408889634
SB_LOGIT_SCALE = LOG2_E / math.sqrt(HEAD_DIM)


def _sb_terms(raw, valid):
    t = jnp.where(valid, raw * SB_LOGIT_SCALE, NEG_BIG)
    e = jnp.exp2(-jnp.abs(t))
    l1m = -(jnp.maximum(t, 0.0) + jnp.log2(1.0 + e))
    return t, l1m, e


def _split_dot(a, u):
    hi = a.astype(BF16)
    lo = (a - hi.astype(F32)).astype(BF16)
    return _dot(hi, u, NN) + _dot(lo, u, NN)


def _sb_fwd(proj8, g_sb, fused=None):
    _, b, s, w = proj8.shape
    n_heads = w // HEAD_DIM
    tq, tk = _tile(s, SB_Q_BLOCK), _tile(s, SB_K_BLOCK)
    nq, kpq = s // tq, tq // tk
    scale = 1.0 / math.sqrt(HEAD_DIM)

    f_in, f_in_specs, f_out, f_out_specs, f_scratch = _fused_specs(fused)
    grid = (b, n_heads)

    def body(*refs):
        q_ref, k_ref, v_ref, zg_ref, g_ref = refs[:5]
        o_ref, tot_ref, y_ref, trips_ref = refs[5 + len(f_in):9 + len(f_in)]
        f_refs = (*refs[5:5 + len(f_in)], *refs[9 + len(f_in):])
        _fused_begin(fused, grid, f_refs)
        u_excl = (lax.broadcasted_iota(jnp.int32, (tk, tk), 0)
                  > lax.broadcasted_iota(jnp.int32, (tk, tk), 1)).astype(BF16)
        ahead = lax.broadcasted_iota(jnp.int32, (tq, tk), 0) - lax.broadcasted_iota(jnp.int32, (tq, tk), 1)
        g = g_ref[...]

        def qblock(i, _):
            rows = pl.ds(pl.multiple_of(i * tq, tq), tq)
            q = q_ref[rows, :].astype(BF16)
            nk = (i + 1) * kpq

            def alive(state):
                jj, _, csum = state
                return (jj <= i) & ((jj == 0) | (jnp.max(csum) > SB_DEAD_LOG2))

            def kblocks(state):
                jj, acc, csum = state
                js = [nk - 1 - (jj * kpq + n) for n in range(kpq)]
                cols = [pl.ds(pl.multiple_of(j * tk, tk), tk) for j in js]
                raw = [_dot(q, k_ref[c, :].astype(BF16), NT) for c in cols]
                terms = [_sb_terms(x, ahead > j * tk - i * tq) for x, j in zip(raw, js)]
                sums = [_split_dot(l1m, u_excl) for _, l1m, _ in terms]
                for (t, l1m, _), part, c in zip(terms, sums, cols):
                    a = jnp.exp2((t + l1m) + (part + csum))
                    acc = acc + _dot(a.astype(BF16), v_ref[c, :].astype(BF16), NN)
                    csum = csum + jnp.sum(l1m, axis=1, keepdims=True)
                return jj + 1, acc, csum

            trips, acc, tot = lax.while_loop(
                alive, kblocks, (jnp.int32(0), jnp.zeros((tq, HEAD_DIM), F32), jnp.zeros((tq, 1), F32)))
            o_ref[rows, :] = acc
            tot_ref[rows, :] = jnp.broadcast_to(tot, (tq, HEAD_DIM))
            y_ref[rows, :] = _head_out(acc, zg_ref[rows, :], g).astype(BF16)
            trips_ref[0, i] = trips.astype(F32)
            return 0

        lax.fori_loop(0, nq, qblock, 0)
        _fused_end(fused, grid, f_refs)

    return pl.pallas_call(
        body, name="sb_fwd", grid=grid,
        in_specs=[_seg_spec(s, 0), _seg_spec(s, 1), _seg_spec(s, 2), _seg_spec(s, 3),
                  pl.BlockSpec((1, HEAD_DIM), lambda i, h: (0, h))] + f_in_specs,
        out_specs=[_head_spec(s), _head_spec(s), _seg_spec(s, 0),
                   pl.BlockSpec((None, None, 1, nq), lambda i, h: (i, h, 0, 0), memory_space=pltpu.SMEM)]
        + f_out_specs,
        out_shape=[jax.ShapeDtypeStruct((b, s, w), F32), jax.ShapeDtypeStruct((b, s, w), F32),
                   jax.ShapeDtypeStruct((2, b, s, w), BF16), jax.ShapeDtypeStruct((b, n_heads, 1, nq), F32)] + f_out,
        scratch_shapes=f_scratch, input_output_aliases=_fused_aliases(fused, 5, 4),
        compiler_params=_cparams(("arbitrary", "arbitrary")),
    )(proj8, proj8, proj8, proj8, g_sb, *f_in)


def _sb_bwd(proj8, o_sb, tot_sb, trips, dy2, g_sb, fused=None):
    _, b, s, w = proj8.shape
    n_heads = w // HEAD_DIM
    tq, tk = _tile(s, SB_Q_BLOCK), _tile(s, SB_K_BLOCK)
    nq, kpq = s // tq, tq // tk
    scale = 1.0 / math.sqrt(HEAD_DIM)

    f_in, f_in_specs, f_out, f_out_specs, f_scratch = _fused_specs(fused)
    grid = (b, n_heads)

    def body(*refs):
        q_ref, k_ref, v_ref, zg_ref, o_ref, tot_ref, dy_ref, g_ref, trips_ref = refs[:9]
        dp_ref, dg_ref = refs[9 + len(f_in):11 + len(f_in)]
        do_s, dk_s, dv_s = refs[11 + len(f_in) + len(f_out):14 + len(f_in) + len(f_out)]
        f_refs = (*refs[9:9 + len(f_in)], *refs[11 + len(f_in):11 + len(f_in) + len(f_out)],
                  *refs[14 + len(f_in) + len(f_out):])
        _fused_begin(fused, grid, f_refs)
        dq_ref, dk_ref, dv_ref, dzg_ref = (dp_ref.at[n] for n in range(4))
        ri = lax.broadcasted_iota(jnp.int32, (tk, tk), 0)
        ci = lax.broadcasted_iota(jnp.int32, (tk, tk), 1)
        u_le = (ri <= ci).astype(BF16)
        u_lt = (ri < ci).astype(BF16)
        ahead = lax.broadcasted_iota(jnp.int32, (tq, tk), 0) - lax.broadcasted_iota(jnp.int32, (tq, tk), 1)
        g = g_ref[...]

        def prologue(i, dg):
            rows = pl.ds(pl.multiple_of(i * tq, tq), tq)
            do, dzg, dgi = _head_out_bwd(o_ref[rows, :], zg_ref[rows, :], g, dy_ref[rows, :])
            dzg_ref[rows, :] = dzg.astype(BF16)
            do_s[rows, :] = do.astype(BF16)
            return dg + dgi

        dg_ref[...] = lax.fori_loop(0, nq, prologue, jnp.zeros((1, HEAD_DIM), F32))
        dk_s[...] = jnp.zeros_like(dk_s)
        dv_s[...] = jnp.zeros_like(dv_s)

        def qblock(i, _):
            rows = pl.ds(pl.multiple_of(i * tq, tq), tq)
            q = q_ref[rows, :].astype(BF16)
            do = do_s[rows, :]
            tot = tot_ref[rows, :][:, :1]

            def kblocks(jj, carry):
                dq, pre_l, pre_g = carry
                js = [jj * kpq + n for n in range(kpq)]
                cols = [pl.ds(pl.multiple_of(j * tk, tk), tk) for j in js]
                ks = [k_ref[c, :].astype(BF16) for c in cols]
                raw = [_dot(q, k, NT) for k in ks]
                da = [_dot(do, v_ref[c, :].astype(BF16), NT) for c in cols]
                terms = [_sb_terms(x, ahead > j * tk - i * tq) for x, j in zip(raw, js)]
                sums_l = [_split_dot(l1m, u_le) for _, l1m, _ in terms]
                a, gg = [], []
                for (t, l1m, _), part, d in zip(terms, sums_l, da):
                    a.append(jnp.exp2((t + l1m) + (tot - (part + pre_l))))
                    gg.append(a[-1] * d)
                    pre_l = pre_l + jnp.sum(l1m, axis=1, keepdims=True)
                sums_g = [_split_dot(x, u_lt) for x in gg]
                dzs = []
                for (t, _, e), x, part in zip(terms, gg, sums_g):
                    big_g = part + pre_g
                    pre_g = pre_g + jnp.sum(x, axis=1, keepdims=True)
                    inv = 1.0 / (1.0 + e)
                    sig = jnp.where(t >= 0.0, inv, e * inv)
                    dzs.append(((x - sig * (x + big_g)) * scale).astype(BF16))
                for x, k in zip(dzs, ks):
                    dq = dq + _dot(x, k, NN)
                for x, y, c in zip(dzs, a, cols):
                    dk_s[c, :] += _dot(x, q, TN)
                    dv_s[c, :] += _dot(y.astype(BF16), do, TN)
                return dq, pre_l, pre_g

            zero = jnp.zeros((tq, 1), F32)
            walked = jnp.clip(trips_ref[0, i].astype(jnp.int32), 1, i + 1)
            dq, _, _ = lax.fori_loop(i + 1 - walked, i + 1, kblocks, (jnp.zeros((tq, HEAD_DIM), F32), zero, zero))
            dq_ref[rows, :] = dq.astype(BF16)
            return 0

        lax.fori_loop(0, nq, qblock, 0)
        dk_ref[...] = dk_s[...].astype(BF16)
        dv_ref[...] = dv_s[...].astype(BF16)
        _fused_end(fused, grid, f_refs)

    return pl.pallas_call(
        body, name="sb_bwd", grid=grid,
        in_specs=[_seg_spec(s, 0), _seg_spec(s, 1), _seg_spec(s, 2), _seg_spec(s, 3),
                  _head_spec(s), _head_spec(s), _seg_spec(s, 0),
                  pl.BlockSpec((1, HEAD_DIM), lambda i, h: (0, h)),
                  pl.BlockSpec((None, None, 1, nq), lambda i, h: (i, h, 0, 0), memory_space=pltpu.SMEM)] + f_in_specs,
        out_specs=[_seg4_spec(s, 0), pl.BlockSpec((None, 1, HEAD_DIM), lambda i, h: (i, 0, h))] + f_out_specs,
        out_shape=[jax.ShapeDtypeStruct((8, b, s, w), BF16), jax.ShapeDtypeStruct((b, 1, w), F32)] + f_out,
        scratch_shapes=[pltpu.VMEM((s, HEAD_DIM), BF16), pltpu.VMEM((s, HEAD_DIM), F32),
                        pltpu.VMEM((s, HEAD_DIM), F32)] + f_scratch,
        compiler_params=_cparams(("arbitrary", "arbitrary")),
    )(proj8, proj8, proj8, proj8, o_sb, tot_sb, dy2, g_sb, trips, *f_in)


DIL_BLOCK = 128
DIL_GROUP = 16


def _dil_chunks(s, r):
    length = s // r
    out = []
    for rho in range(r):
        for cc in range(length // DIL_BLOCK):
            if r == 1:
                nat = pl.ds(cc * DIL_BLOCK, DIL_BLOCK)
            else:
                nat = pl.ds(rho + r * DIL_BLOCK * cc, DIL_BLOCK, stride=r)
            off = rho * length + cc * DIL_BLOCK
            out.append((nat, pl.ds(off, DIL_BLOCK), pl.ds(off + DIL_BLOCK, DIL_BLOCK)))
    return out


def _dil_masks(slope, r):
    n = DIL_BLOCK
    ri = lax.broadcasted_iota(jnp.int32, (n, 2 * n), 0)
    ci = lax.broadcasted_iota(jnp.int32, (n, 2 * n), 1)
    steps = ri - ci + n
    inside = (steps >= 0) & (steps <= n)
    bias = slope * (steps.astype(F32) * r)
    return jnp.where(inside, -bias, NEG_BIG), jnp.where(inside & (ci >= n), -bias, NEG_BIG)


def _dil_scores(q, k_pc, masks, first):
    return _dot(q, k_pc, NT) * (1.0 / math.sqrt(HEAD_DIM)) + jnp.where(first, masks[1], masks[0])


def _dil_check(s):
    assert (s // DIL_BLOCK) % DIL_GROUP == 0, s
    for window, r in DIL_PAIRS:
        assert window // r == DIL_BLOCK and s % (r * DIL_BLOCK) == 0, (s, window, r)


def _dil_fwd(proj8, g_dil, slopes, y2):
    _, b, s, w = proj8.shape
    n_heads = w // HEAD_DIM
    _dil_check(s)
    n = DIL_BLOCK
    nt = s // n

    def body(q_ref, k_ref, v_ref, zg_ref, g_ref, sl_ref, y_in, o_ref, lse_ref, y_ref,
             qp, kp, vp, pnum, pm, pl_, acc_s, m_s, l_s):
        del y_in
        slope = sl_ref[...][:, :1]
        kp[pl.ds(0, n), :] = jnp.zeros((n, HEAD_DIM), BF16)
        vp[pl.ds(0, n), :] = jnp.zeros((n, HEAD_DIM), BF16)

        for (window, r) in DIL_PAIRS:
            nb = (s // r) // n
            masks = _dil_masks(slope, float(r))
            for nat, per, padded in _dil_chunks(s, r):
                qp[per, :] = q_ref[nat, :].astype(BF16)
                kp[padded, :] = k_ref[nat, :].astype(BF16)
                vp[padded, :] = v_ref[nat, :].astype(BF16)
            num_t, m_t, l_t = (acc_s, m_s, l_s) if r == 1 else (pnum, pm, pl_)

            def tiles(tt, _):
                ts = [tt * DIL_GROUP + i for i in range(DIL_GROUP)]
                rows = [pl.ds(pl.multiple_of(t * n, n), n) for t in ts]
                both = [pl.ds(pl.multiple_of(t * n, n), 2 * n) for t in ts]
                sc = [_dil_scores(qp[rw, :], kp[bt, :], masks, lax.rem(t, nb) == 0)
                      for t, rw, bt in zip(ts, rows, both)]
                m = [jnp.max(x, axis=1, keepdims=True) for x in sc]
                p = [jnp.exp(x - mx) for x, mx in zip(sc, m)]
                num = [_dot(x.astype(BF16), vp[bt, :], NN) for x, bt in zip(p, both)]
                for rw, x, mx, nm in zip(rows, p, m, num):
                    num_t[rw, :] = nm
                    m_t[rw, :] = jnp.broadcast_to(mx, (n, HEAD_DIM))
                    l_t[rw, :] = jnp.broadcast_to(jnp.sum(x, axis=1, keepdims=True), (n, HEAD_DIM))
                return 0

            lax.fori_loop(0, nt // DIL_GROUP, tiles, 0)
            if r != 1:
                for nat, per, _ in _dil_chunks(s, r):
                    m_old, m_new_p = m_s[nat, :], pm[per, :]
                    m_new = jnp.maximum(m_old, m_new_p)
                    a_old, a_p = jnp.exp(m_old - m_new), jnp.exp(m_new_p - m_new)
                    m_s[nat, :] = m_new
                    l_s[nat, :] = l_s[nat, :] * a_old + pl_[per, :] * a_p
                    acc_s[nat, :] = acc_s[nat, :] * a_old + pnum[per, :] * a_p

        g = g_ref[...]

        def finish(t, _):
            rows = pl.ds(t * n, n)
            l = l_s[rows, :]
            o = acc_s[rows, :] / l
            o_ref[rows, :] = o
            lse_ref[rows, :] = m_s[rows, :] + jnp.log(l)
            y_ref[rows, :] = _head_out(o, zg_ref[rows, :], g).astype(BF16)
            return 0

        for t in range(nt):
            finish(t, 0)

    f32_s = pltpu.VMEM((s, HEAD_DIM), F32)
    bf_s = pltpu.VMEM((s, HEAD_DIM), BF16)
    bf_pad = pltpu.VMEM((s + n, HEAD_DIM), BF16)
    return pl.pallas_call(
        body, name="dil_fwd", grid=(b, n_heads),
        in_specs=[_seg_spec(s, 4), _seg_spec(s, 5), _seg_spec(s, 6), _seg_spec(s, 7),
                  pl.BlockSpec((1, HEAD_DIM), lambda i, h: (0, h)),
                  pl.BlockSpec((None, 1, HEAD_DIM), lambda i, h: (h, 0, 0)), ANY],
        out_specs=[_head_spec(s), _head_spec(s), _seg_spec(s, 1)],
        out_shape=[jax.ShapeDtypeStruct((b, s, w), F32), jax.ShapeDtypeStruct((b, s, w), F32),
                   jax.ShapeDtypeStruct((2, b, s, w), BF16)],
        scratch_shapes=[bf_s, bf_pad, bf_pad, f32_s, f32_s, f32_s, f32_s, f32_s, f32_s],
        input_output_aliases={6: 2},
        compiler_params=_cparams(("parallel", "parallel")),
    )(proj8, proj8, proj8, proj8, g_dil, slopes, y2)


def _dil_bwd(proj8, o_dl, lse_dl, dy2, g_dil, slopes, dproj8):
    _, b, s, w = proj8.shape
    n_heads = w // HEAD_DIM
    _dil_check(s)
    n = DIL_BLOCK
    nt = s // n
    scale = 1.0 / math.sqrt(HEAD_DIM)

    def body(q_ref, k_ref, v_ref, zg_ref, o_ref, lse_ref, dy_ref, g_ref, sl_ref, dp_in, dp_ref, dg_ref,
             do_n, dt_n, dq_n, dk_n, dv_n, qp, kp, vp, dop, dtp, lsep, pdq, pdk, pdv):
        del dp_in
        dq_ref, dk_ref, dv_ref, dzg_ref = (dp_ref.at[i] for i in range(4))
        slope = sl_ref[...][:, :1]
        g = g_ref[...]

        def prologue(t, dg):
            rows = pl.ds(t * n, n)
            o = o_ref[rows, :]
            do, dzg, dgi = _head_out_bwd(o, zg_ref[rows, :], g, dy_ref[rows, :])
            dzg_ref[rows, :] = dzg.astype(BF16)
            do_n[rows, :] = do
            dt_n[rows, :] = jnp.broadcast_to(jnp.sum(do * o, axis=-1, keepdims=True), (n, HEAD_DIM))
            return dg + dgi

        dg_ref[...] = functools.reduce(lambda dg, t: prologue(t, dg), range(nt), jnp.zeros((1, HEAD_DIM), F32))
        dq_n[...] = jnp.zeros_like(dq_n)
        dk_n[...] = jnp.zeros_like(dk_n)
        dv_n[...] = jnp.zeros_like(dv_n)
        kp[pl.ds(0, n), :] = jnp.zeros((n, HEAD_DIM), BF16)
        vp[pl.ds(0, n), :] = jnp.zeros((n, HEAD_DIM), BF16)

        for (window, r) in DIL_PAIRS:
            nb = (s // r) // n
            masks = _dil_masks(slope, float(r))
            for nat, per, padded in _dil_chunks(s, r):
                qp[per, :] = q_ref[nat, :].astype(BF16)
                kp[padded, :] = k_ref[nat, :].astype(BF16)
                vp[padded, :] = v_ref[nat, :].astype(BF16)
                dop[per, :] = do_n[nat, :].astype(BF16)
                dtp[per, :] = dt_n[nat, :]
                lsep[per, :] = lse_ref[nat, :]
            pdk[...] = jnp.zeros_like(pdk)
            pdv[...] = jnp.zeros_like(pdv)

            def tiles(tt, _):
                ts = [tt * DIL_GROUP + i for i in range(DIL_GROUP)]
                rows = [pl.ds(pl.multiple_of(t * n, n), n) for t in ts]
                both = [pl.ds(pl.multiple_of(t * n, n), 2 * n) for t in ts]
                q = [qp[rw, :] for rw in rows]
                do = [dop[rw, :] for rw in rows]
                sc = [_dil_scores(qq, kp[bt, :], masks, lax.rem(t, nb) == 0) for t, qq, bt in zip(ts, q, both)]
                dp = [_dot(dd, vp[bt, :], NT) for dd, bt in zip(do, both)]
                p = [jnp.exp(x - lsep[rw, :][:, :1]) for x, rw in zip(sc, rows)]
                ds = [((x * (y - dtp[rw, :][:, :1])) * scale).astype(BF16) for x, y, rw in zip(p, dp, rows)]
                dq = [_dot(x, kp[bt, :], NN) for x, bt in zip(ds, both)]
                dk = [_dot(x, qq, TN) for x, qq in zip(ds, q)]
                dv = [_dot(x.astype(BF16), dd, TN) for x, dd in zip(p, do)]
                for rw, bt, x, y, z in zip(rows, both, dq, dk, dv):
                    pdq[rw, :] = x
                    pdk[bt, :] += y
                    pdv[bt, :] += z
                return 0

            lax.fori_loop(0, nt // DIL_GROUP, tiles, 0)
            for nat, per, padded in _dil_chunks(s, r):
                dq_n[nat, :] += pdq[per, :]
                dk_n[nat, :] += pdk[padded, :]
                dv_n[nat, :] += pdv[padded, :]

        dq_ref[...] = dq_n[...].astype(BF16)
        dk_ref[...] = dk_n[...].astype(BF16)
        dv_ref[...] = dv_n[...].astype(BF16)

    f32_s = pltpu.VMEM((s, HEAD_DIM), F32)
    f32_pad = pltpu.VMEM((s + n, HEAD_DIM), F32)
    bf_s = pltpu.VMEM((s, HEAD_DIM), BF16)
    bf_pad = pltpu.VMEM((s + n, HEAD_DIM), BF16)
    return pl.pallas_call(
        body, name="dil_bwd", grid=(b, n_heads),
        in_specs=[_seg_spec(s, 4), _seg_spec(s, 5), _seg_spec(s, 6), _seg_spec(s, 7),
                  _head_spec(s), _head_spec(s), _seg_spec(s, 1),
                  pl.BlockSpec((1, HEAD_DIM), lambda i, h: (0, h)),
                  pl.BlockSpec((None, 1, HEAD_DIM), lambda i, h: (h, 0, 0)), ANY],
        out_specs=[_seg4_spec(s, 1), pl.BlockSpec((None, 1, HEAD_DIM), lambda i, h: (i, 0, h))],
        out_shape=[jax.ShapeDtypeStruct((8, b, s, w), BF16), jax.ShapeDtypeStruct((b, 1, w), F32)],
        scratch_shapes=[f32_s] * 5 + [bf_s, bf_pad, bf_pad, bf_s] + [f32_s, f32_s, f32_s, f32_pad, f32_pad],
        input_output_aliases={9: 0},
        compiler_params=_cparams(("parallel", "parallel")),
    )(proj8, proj8, proj8, proj8, o_dl, lse_dl, dy2, g_dil, slopes, dproj8)


def _small_update(gathered, n_b, params, m, v):
    n_dev, _, width = gathered.shape

    def body(g_ref, p_ref, m_ref, v_ref, grad_ref, d_ref, nm_ref, nv_ref, loss_ref):
        for row in range(2):
            acc = None
            for dev in range(n_dev):
                for i in range(n_b):
                    term = g_ref[dev, pl.ds(row * n_b + i, 1), :]
                    acc = term if acc is None else acc + term
            grad_ref[pl.ds(row, 1), :] = acc
        loss = g_ref[0, pl.ds(2 * n_b, 1), pl.ds(0, 128)]
        for dev in range(1, n_dev):
            loss = loss + g_ref[dev, pl.ds(2 * n_b, 1), pl.ds(0, 128)]
        loss_ref[...] = loss
        d, nm, nv = _adamw_math(p_ref[...], grad_ref[...], m_ref[...], v_ref[...])
        d_ref[...] = d
        nm_ref[...] = nm
        nv_ref[...] = nv

    sds = jax.ShapeDtypeStruct((2, width), F32)
    return pl.pallas_call(
        body, name="small_update",
        in_specs=[VMEM_SPEC] * 4, out_specs=[VMEM_SPEC] * 5,
        out_shape=[sds, sds, sds, sds, jax.ShapeDtypeStruct((1, 128), F32)],
        compiler_params=_cparams(),
    )(gathered, params, m, v)


def _wada_update(c_t, dmod, w, m, v):
    d, bt = c_t.shape
    _, n = dmod.shape
    tr, tc = _tile(d, 512), _tile(n, 1024)

    def body(c_ref, dm_ref, w_ref, m_ref, v_ref, g_ref, d_ref, nm_ref, nv_ref):
        cv = c_ref[...]
        cs = (cv * _sigmoid(cv)).astype(BF16)
        grad = _dot(cs, dm_ref[...].astype(BF16), NN)
        g_ref[...] = grad
        dl, nm, nv = _adamw_math(w_ref[...], grad, m_ref[...], v_ref[...])
        d_ref[...] = dl
        nm_ref[...] = nm
        nv_ref[...] = nv

    spec = pl.BlockSpec((tr, tc), lambda i, j: (i, j))
    sds = jax.ShapeDtypeStruct((d, n), F32)
    return pl.pallas_call(
        body, name="wada_update", grid=(d // tr, n // tc),
        in_specs=[pl.BlockSpec((tr, bt), lambda i, j: (i, 0)), pl.BlockSpec((bt, tc), lambda i, j: (0, j)),
                  spec, spec, spec],
        out_specs=[spec] * 4, out_shape=[sds] * 4,
        compiler_params=_cparams(("parallel", "parallel")),
    )(c_t, dmod, w, m, v)


def _reduce_begin(gs, tag):
    ra = _sibling_half_swap(gs, "swap_" + tag)
    pa, own = _pair_sum(gs, ra, "pair_sum_" + tag)
    return _ScatterChips(pa), own


def _reduce_finish(rb, own, w, m, v, tag):
    half = _chip_sum(rb, own, "chip_sum_" + tag)
    return _adamw(w, _sibling_join(half, "join_" + tag), m, v, "adamw_" + tag)


def kernel(x, c, w_ada, b_ada, g_norm, w_in, g_sb, g_dil, w_out, g_final, loss_target, m_w_ada, m_b_ada, m_g_norm, m_w_in, m_g_sb, m_g_dil, m_w_out, m_g_final, v_w_ada, v_b_ada, v_g_norm, v_w_in, v_g_sb, v_g_dil, v_w_out, v_g_final):
    nb, s, d = x.shape
    t = nb * s
    na = w_ada.shape[2]
    cs = w_in.shape[2]
    w = cs // 2
    n_heads = w // HEAD_DIM
    r_out = w_out.shape[1]
    assert 2 * nb + 1 <= 8 and 2 * d + 2 * w <= 3 * d and N_CHIPS * na == 3 * d and N_CHIPS * r_out == 2 * w
    xi, yi, ci = _place()
    chip = 2 * xi + yi
    dev = 2 * chip + ci

    c_all = _allgather8(jnp.pad(c, ((0, 8 - nb), (0, 0))), "gather_c")
    c16 = c_all.reshape(N_DEV, 8, d)[:, :nb].reshape(N_DEV * nb, d)
    b_ada_shard = lax.dynamic_slice(b_ada, (0, chip * na), (1, na))
    mod_part = _mod_fwd(c16, w_ada[0], b_ada_shard)
    mod_all = _allgather8(mod_part, "gather_mod")
    mod_full = mod_all.reshape(N_CHIPS, 2, N_DEV * nb, na)[:, 0].transpose(1, 0, 2).reshape(N_DEV * nb, 3 * d)
    mod = lax.dynamic_slice(mod_full, (dev * nb, 0), (nb, 3 * d))
    shift, scale, gate = (mod[:, i * d:(i + 1) * d].reshape(nb, 1, d) for i in range(3))

    h = _norm_mod_fwd(x, g_norm, scale, shift)
    h2 = h.reshape(t, d)
    ws_in_slab, w_in_own = _cast_bf16_slab(w_in[0], "cast_w_in", with_own=True)
    (ws_out_slab,) = _cast_bf16_slab(w_out[0], "cast_w_out")
    proj8, ws_in, w_in_diag = _proj_fwd_split(h2, w_in_own, _GatherNear(ws_in_slab))
    ws_in = lax.dynamic_update_slice(ws_in, w_in_diag[None], (chip ^ 3, 0, 0))
    proj8 = proj8.reshape(8, nb, s, w)

    slopes = jnp.exp2(-ALIBI_MAX_BIAS * jnp.arange(1, n_heads + 1, dtype=F32) / n_heads)
    slopes = jnp.broadcast_to(slopes[:, None, None], (n_heads, 1, HEAD_DIM))
    o_sb, tot_sb, y2, sb_trips, ws_out = _sb_fwd(proj8, g_sb, fused=_GatherChips(ws_out_slab))
    w_out_full = ws_out.reshape(2 * w, d)
    o_dl, lse_dl, y2 = _dil_fwd(proj8, g_dil, slopes, y2)
    y2f = y2.reshape(2, t, w)
    out = _out_fwd(y2f, w_out_full)

    dx1, dout, dgate, dg_final, loss_part = _loss_head(
        x, out.reshape(nb, s, d), gate, g_final.reshape(1, d), loss_target)
    dout2 = dout.reshape(t, d)
    gs_out = _out_bwd_w(y2f, dout2).reshape(N_CHIPS, r_out, d)
    scatter_out, own_out = _reduce_begin(gs_out, "w_out")
    dy2 = _out_bwd_y(dout2, w_out_full).reshape(2, nb, s, w)
    dproj8, dg_sb, rb_out = _sb_bwd(proj8, o_sb, tot_sb, sb_trips, dy2, g_sb, fused=scatter_out)
    dproj8, dg_dl = _dil_bwd(proj8, o_dl, lse_dl, dy2, g_dil, slopes, dproj8)
    dproj8 = dproj8.reshape(8, t, w)
    gs_in = _proj_bwd_w(h2, dproj8)
    scatter_in, own_in = _reduce_begin(gs_in, "w_in")
    dh, rb_in = _proj_bwd_x(dproj8, ws_in, fused=scatter_in)
    grad_x, dshift, dscale, dg_norm = _norm_mod_bwd(x, dh.reshape(nb, s, d), dx1, g_norm, scale)

    width = 3 * d
    dmod = jnp.concatenate([dshift, dscale, dgate], axis=-1).reshape(nb, width)
    gains = jnp.concatenate([dg_sb.reshape(nb, w), dg_dl.reshape(nb, w)], axis=-1)
    gains = jnp.pad(gains, ((0, 0), (2 * d, width - 2 * d - 2 * w)))
    first = jnp.pad(jnp.concatenate([dg_norm, dg_final], axis=-1), ((0, nb - 1), (0, width - 2 * d)))
    loss_row = jnp.pad(loss_part, ((0, 0), (0, width - 128)))
    pack = jnp.concatenate([dmod, gains + first, loss_row, jnp.zeros((8 - 2 * nb - 1, width), F32)], axis=0)
    gathered = _allgather8(pack, "gather_small").reshape(N_DEV, 8, width)

    def stack(bias, gn, gf, gsb, gdl):
        row1 = jnp.concatenate([gn.reshape(1, d), gf.reshape(1, d), gsb.reshape(1, w), gdl.reshape(1, w)], axis=-1)
        return jnp.concatenate([bias.reshape(1, width), jnp.pad(row1, ((0, 0), (0, width - 2 * d - 2 * w)))], axis=0)

    small = _small_update(
        gathered, nb, stack(b_ada, g_norm, g_final, g_sb, g_dil),
        stack(m_b_ada, m_g_norm, m_g_final, m_g_sb, m_g_dil), stack(v_b_ada, v_g_norm, v_g_final, v_g_sb, v_g_dil))
    loss = small[4][0, 0]

    def unstack(a):
        return (a[0:1, :], a[1:2, 0:d], a[1, d:2 * d], a[1:2, 2 * d:2 * d + w], a[1:2, 2 * d + w:2 * d + 2 * w])

    (g_b, g_gn, g_gf, g_gsb, g_gdl), (d_b, d_gn, d_gf, d_gsb, d_gdl), (nm_b, nm_gn, nm_gf, nm_gsb, nm_gdl), \
        (nv_b, nv_gn, nv_gf, nv_gsb, nv_gdl) = (unstack(a) for a in small[:4])

    dmod_all = gathered[:, :nb].reshape(N_DEV * nb, width)
    dmod_cols = lax.dynamic_slice(dmod_all, (0, chip * na), (N_DEV * nb, na))
    g_wa, d_wa, nm_wa, nv_wa = _wada_update(c16.T, dmod_cols, w_ada[0], m_w_ada[0], v_w_ada[0])

    g_wi, d_wi, nm_wi, nv_wi = _reduce_finish(rb_in, own_in, w_in[0], m_w_in[0], v_w_in[0], "w_in")
    g_wo, d_wo, nm_wo, nv_wo = _reduce_finish(rb_out, own_out, w_out[0], m_w_out[0], v_w_out[0], "w_out")

    lead = lambda a: a[None]
    return (loss, grad_x,
            lead(g_wa), g_b, g_gn, lead(g_wi), g_gsb, g_gdl, lead(g_wo), g_gf,
            lead(d_wa), d_b, d_gn, lead(d_wi), d_gsb, d_gdl, lead(d_wo), d_gf,
            lead(nm_wa), nm_b, nm_gn, lead(nm_wi), nm_gsb, nm_gdl, lead(nm_wo), nm_gf,
            lead(nv_wa), nv_b, nv_gn, lead(nv_wi), nv_gsb, nv_gdl, lead(nv_wo), nv_gf)
```
